```python
import jax, jax.numpy as jnp
from jax import lax
import numpy as np

D_MODEL = 1024
BATCH = 4
SEQ = 4096
DEPTH = 1

CHUNK = 64
HEAD_DIM = 64
RW_WIDTH = 512
FX_WIDTH = 512
RW_HEADS = RW_WIDTH // HEAD_DIM
FX_HEADS = FX_WIDTH // HEAD_DIM
MIX_WIDTH = RW_WIDTH + FX_WIDTH
RW_DECAY_LORA = 32
RW_AAA_LORA = 32
RW_GATE_LORA = 96
RW_COLS = 3 * RW_WIDTH + RW_DECAY_LORA + RW_AAA_LORA + RW_GATE_LORA
FX_COLS = 4 * FX_WIDTH + FX_HEADS
IN_COLS = RW_COLS + FX_COLS
RW_SPLITS = [RW_WIDTH, 2 * RW_WIDTH, 3 * RW_WIDTH, 3 * RW_WIDTH + RW_DECAY_LORA,
             3 * RW_WIDTH + RW_DECAY_LORA + RW_AAA_LORA]
FX_SPLITS = [FX_WIDTH, 2 * FX_WIDTH, 3 * FX_WIDTH, 4 * FX_WIDTH]
RW_GN_EPS = 64e-5
QK_EPS = 1e-6
LN_EPS = 1e-5
Q_BLOCK = 128
N_EXPERTS = 32
TOP_K = 4
EXPERT_FF = D_MODEL
SWIGLU_ALPHA = 1.702
SWIGLU_LIMIT = 7.0
MOE_BLOCK = 128
DEEPNORM_ALPHA = (2 * DEPTH) ** 0.25
DEEPNORM_BETA = (8 * DEPTH) ** -0.25

kernel_name = 'hybrid_rwkv7_fox_moe_encoder'


def _layer_norm(x, w, b):
    xf = x.astype(jnp.float32)
    mu = jnp.mean(xf, axis=-1, keepdims=True)
    var = jnp.mean(jnp.square(xf - mu), axis=-1, keepdims=True)
    return ((xf - mu) * lax.rsqrt(var + LN_EPS) * w + b).astype(x.dtype)


def _rms_norm(x, w):
    xf = x.astype(jnp.float32)
    return (xf * lax.rsqrt(jnp.mean(jnp.square(xf), axis=-1, keepdims=True) + QK_EPS) * w).astype(x.dtype)


def _rwkv7_mixer(p, mu, w0, w2, a0, a2, g2, k_k, k_a, r_k, gn_w, gn_b):
    B, S, _ = p.shape
    H, N = RW_HEADS, HEAD_DIM
    prev = jnp.pad(p, ((0, 0), (1, 0), (0, 0)))[:, :-1]
    p = p + mu * (prev - p)
    r, k, v, wd, ad, gd = jnp.split(p, RW_SPLITS, axis=-1)
    w = -jax.nn.softplus(-(w0 + jnp.tanh(wd) @ w2)) - 0.5
    decay = jnp.exp(-jnp.exp(w.astype(jnp.float32)))
    a = jax.nn.sigmoid(a0 + ad @ a2)
    g = jax.nn.sigmoid(gd) @ g2
    kk = (k * k_k).reshape(B, S, H, N).astype(jnp.float32)
    kk = kk / jnp.maximum(jnp.linalg.norm(kk, axis=-1, keepdims=True), 1e-12)
    k = k * (1 + (a - 1) * k_a)
    r_h, k_h, v_h, a_h = (t.reshape(B, S, H, N) for t in (r, k, v, a))
    w_h = decay.reshape(B, S, H, N)

    def to_chunks(t):
        return t.astype(jnp.float32).transpose(1, 0, 2, 3).reshape(S // CHUNK, CHUNK, B, H, N)

    xs = tuple(to_chunks(t) for t in (r_h, w_h, k_h, v_h, -kk, kk * a_h))

    def step(state, inp):
        rt, wt, kt, vt, at, bt = inp
        sa = jnp.einsum('bhvk,bhk->bhv', state, at)
        state = state * wt[:, :, None, :] + sa[..., None] * bt[:, :, None, :] + vt[..., None] * kt[:, :, None, :]
        return state, jnp.einsum('bhvk,bhk->bhv', state, rt)

    def chunk_step(state, chunk):
        return lax.scan(step, state, chunk)

    s0 = jnp.zeros((B, H, N, N), jnp.float32)
    _, ys = lax.scan(chunk_step, s0, xs)
    y = ys.reshape(S, B, H, N).transpose(1, 0, 2, 3)
    ym = jnp.mean(y, axis=-1, keepdims=True)
    yv = jnp.mean(jnp.square(y - ym), axis=-1, keepdims=True)
    yn = ((y - ym) * lax.rsqrt(yv + RW_GN_EPS)).reshape(B, S, RW_WIDTH) * gn_w + gn_b
    bonus = (jnp.sum(r_h * k_h * r_k, axis=-1, keepdims=True) * v_h).reshape(B, S, RW_WIDTH)
    return (yn.astype(p.dtype) + bonus) * g


def _fox_mixer(p, b_f, q_norm, k_norm):
    B, S, _ = p.shape
    H, N = FX_HEADS, HEAD_DIM
    q, k, v, og, fz = jnp.split(p, FX_SPLITS, axis=-1)
    q = _rms_norm(q.reshape(B, S, H, N), q_norm).transpose(0, 2, 1, 3)
    k = _rms_norm(k.reshape(B, S, H, N), k_norm).transpose(0, 2, 1, 3)
    v = v.reshape(B, S, H, N).transpose(0, 2, 1, 3)
    log_f = jax.nn.log_sigmoid(fz.astype(jnp.float32) + b_f.astype(jnp.float32))
    c = jnp.cumsum(log_f, axis=1).transpose(0, 2, 1)
    scale = HEAD_DIM ** -0.5
    outs = []
    for i in range(S // Q_BLOCK):
        q0 = i * Q_BLOCK
        L = q0 + Q_BLOCK
        s = jnp.einsum('bhqd,bhkd->bhqk', q[:, :, q0:L], k[:, :, :L]).astype(jnp.float32) * scale
        s = s + c[:, :, q0:L, None] - c[:, :, None, :L]
        causal = jnp.arange(L)[None, :] <= (q0 + jnp.arange(Q_BLOCK))[:, None]
        s = jnp.where(causal, s, -jnp.inf)
        pr = jax.nn.softmax(s, axis=-1).astype(v.dtype)
        outs.append(jnp.einsum('bhqk,bhkd->bhqd', pr, v[:, :, :L]))
    o = jnp.concatenate(outs, axis=2).transpose(0, 2, 1, 3).reshape(B, S, FX_WIDTH)
    return o * jax.nn.sigmoid(og)


def _moe(h, router_w, router_b, w1, b1, w2, b2):
    B, S, D = h.shape
    T = B * S
    xt = h.reshape(T, D)
    logits = (xt @ router_w + router_b).astype(jnp.float32)
    top_val, top_idx = lax.top_k(logits, TOP_K)
    gates = jax.nn.softmax(top_val, axis=-1)
    e_flat = top_idx.reshape(-1).astype(jnp.int32)
    g_flat = gates.reshape(-1)
    tok_flat = jnp.repeat(jnp.arange(T, dtype=jnp.int32), TOP_K)
    n_assign = T * TOP_K
    n_rows = -(-n_assign // MOE_BLOCK) * MOE_BLOCK + N_EXPERTS * MOE_BLOCK
    n_blocks = n_rows // MOE_BLOCK
    order = jnp.argsort(e_flat, stable=True)
    e_sorted = e_flat[order]
    counts = jnp.bincount(e_flat, length=N_EXPERTS)
    starts = jnp.cumsum(counts) - counts
    padded = (counts + MOE_BLOCK - 1) // MOE_BLOCK * MOE_BLOCK
    pends = jnp.cumsum(padded)
    pstarts = pends - padded
    dest = pstarts[e_sorted] + (jnp.arange(n_assign, dtype=jnp.int32) - starts[e_sorted])
    row_tok = jnp.full((n_rows,), T, jnp.int32).at[dest].set(tok_flat[order])
    row_gate = jnp.zeros((n_rows,), jnp.float32).at[dest].set(g_flat[order])
    block_exp = jnp.minimum(jnp.searchsorted(pends, jnp.arange(n_blocks) * MOE_BLOCK, side='right'),
                            N_EXPERTS - 1).astype(jnp.int32)
    xs = jnp.concatenate([xt, jnp.zeros((1, D), xt.dtype)], axis=0)[row_tok].reshape(n_blocks, MOE_BLOCK, D)

    def expert_block(args):
        xb, e = args
        hid = xb @ w1[e] + b1[e]
        x_glu = jnp.minimum(hid[..., ::2], SWIGLU_LIMIT)
        x_lin = jnp.clip(hid[..., 1::2], -SWIGLU_LIMIT, SWIGLU_LIMIT)
        act = x_glu * jax.nn.sigmoid(SWIGLU_ALPHA * x_glu) * (x_lin + 1)
        return act @ w2[e] + b2[e]

    out = lax.map(expert_block, (xs, block_exp)).reshape(n_rows, D)
    out = out * row_gate[:, None].astype(out.dtype)
    y = jax.ops.segment_sum(out, row_tok, num_segments=T + 1)[:T]
    return y.reshape(B, S, D)


def setup_inputs(seed: int = 0) -> dict:
    key = jax.random.key(seed)
    ks = jax.random.split(key, 32)
    L = DEPTH
    nrm = lambda k, shape, s: jax.random.normal(k, shape, jnp.float32) * s
    uni = lambda k, shape, lo, hi: jax.random.uniform(k, shape, jnp.float32, lo, hi)
    return {
        'x': nrm(ks[0], (BATCH, SEQ, D_MODEL), 1.0),
        'ln_in_w': 1.0 + nrm(ks[1], (D_MODEL,), 0.02),
        'ln_in_b': nrm(ks[2], (D_MODEL,), 0.02),
        'w_in': nrm(ks[3], (L, D_MODEL, IN_COLS), D_MODEL ** -0.5),
        'rw_mu': uni(ks[4], (L, RW_COLS), 0.0, 1.0),
        'rw_w0': uni(ks[5], (L, RW_WIDTH), -6.0, 1.0),
        'rw_w2': nrm(ks[6], (L, RW_DECAY_LORA, RW_WIDTH), 0.1),
        'rw_a0': nrm(ks[7], (L, RW_WIDTH), 0.5),
        'rw_a2': nrm(ks[8], (L, RW_AAA_LORA, RW_WIDTH), 0.1),
        'rw_g2': nrm(ks[9], (L, RW_GATE_LORA, RW_WIDTH), RW_GATE_LORA ** -0.5),
        'rw_k_k': 0.85 + nrm(ks[10], (L, RW_WIDTH), 0.05),
        'rw_k_a': 1.0 + nrm(ks[11], (L, RW_WIDTH), 0.05),
        'rw_r_k': nrm(ks[12], (L, RW_HEADS, HEAD_DIM), 0.1),
        'rw_gn_w': 1.0 + nrm(ks[13], (L, RW_WIDTH), 0.02),
        'rw_gn_b': nrm(ks[14], (L, RW_WIDTH), 0.02),
        'fx_b_f': uni(ks[15], (L, FX_HEADS), 1.0, 5.0),
        'fx_q_norm': 1.0 + nrm(ks[16], (L, HEAD_DIM), 0.02),
        'fx_k_norm': 1.0 + nrm(ks[17], (L, HEAD_DIM), 0.02),
        'w_o': nrm(ks[18], (L, MIX_WIDTH, D_MODEL), MIX_WIDTH ** -0.5 * DEEPNORM_BETA),
        'ln1_w': 1.0 + nrm(ks[19], (L, D_MODEL), 0.02),
        'ln1_b': nrm(ks[20], (L, D_MODEL), 0.02),
        'router_w': nrm(ks[21], (L, D_MODEL, N_EXPERTS), D_MODEL ** -0.5),
        'router_b': nrm(ks[22], (L, N_EXPERTS), 0.01),
        'exp_w1': nrm(ks[23], (L, N_EXPERTS, D_MODEL, 2 * EXPERT_FF), D_MODEL ** -0.5),
        'exp_b1': nrm(ks[24], (L, N_EXPERTS, 2 * EXPERT_FF), 0.01),
        'exp_w2': nrm(ks[25], (L, N_EXPERTS, EXPERT_FF, D_MODEL), EXPERT_FF ** -0.5 * DEEPNORM_BETA),
        'exp_b2': nrm(ks[26], (L, N_EXPERTS, D_MODEL), 0.01),
        'ln2_w': 1.0 + nrm(ks[27], (L, D_MODEL), 0.02),
        'ln2_b': nrm(ks[28], (L, D_MODEL), 0.02),
    }


def reference(x, ln_in_w, ln_in_b, w_in, rw_mu, rw_w0, rw_w2, rw_a0, rw_a2, rw_g2, rw_k_k, rw_k_a,
              rw_r_k, rw_gn_w, rw_gn_b, fx_b_f, fx_q_norm, fx_k_norm, w_o, ln1_w, ln1_b,
              router_w, router_b, exp_w1, exp_b1, exp_w2, exp_b2, ln2_w, ln2_b):
    h = _layer_norm(x, ln_in_w, ln_in_b)
    for l in range(DEPTH):
        proj = h @ w_in[l]
        p_rw, p_fx = proj[..., :RW_COLS], proj[..., RW_COLS:]
        y_rw = _rwkv7_mixer(p_rw, rw_mu[l], rw_w0[l], rw_w2[l], rw_a0[l], rw_a2[l], rw_g2[l],
                            rw_k_k[l], rw_k_a[l], rw_r_k[l], rw_gn_w[l], rw_gn_b[l])
        y_fx = _fox_mixer(p_fx, fx_b_f[l], fx_q_norm[l], fx_k_norm[l])
        mix = jnp.concatenate([y_rw, y_fx], axis=-1) @ w_o[l]
        h = _layer_norm(DEEPNORM_ALPHA * h + mix, ln1_w[l], ln1_b[l])
        ffn = _moe(h, router_w[l], router_b[l], exp_w1[l], exp_b1[l], exp_w2[l], exp_b2[l])
        h = _layer_norm(DEEPNORM_ALPHA * h + ffn, ln2_w[l], ln2_b[l])
    return h
```

```python
import functools

import jax
import jax.numpy as jnp
from jax import lax
from jax.experimental import pallas as pl
from jax.experimental.pallas import tpu as pltpu

F32 = jnp.float32
BF16 = jnp.bfloat16
HIGHEST = lax.Precision.HIGHEST

HEAD_DIM = 64
PAIR = 2 * HEAD_DIM
WKV_CHUNK = 64
RW_GN_EPS = 64e-5
QK_EPS = 1e-6
LN_EPS = 1e-5
TOP_K = 4
SWIGLU_ALPHA = 1.702
SWIGLU_LIMIT = 7.0
NEG_BIG = -1e30

ROW_TILE = 256
ATTN_TQ = 512
ATTN_TK = 512
MOE_TILE = 256
VMEM_LIMIT = 48 * 1024 * 1024


def _cparams(sem):
    return pltpu.CompilerParams(dimension_semantics=sem, vmem_limit_bytes=VMEM_LIMIT)


def _dot(a, b):
    return jnp.dot(a.astype(BF16), b.astype(BF16), preferred_element_type=F32)


def _dot_t(a, b):
    return lax.dot_general(a.astype(BF16), b.astype(BF16), (((1,), (1,)), ((), ())),
                           preferred_element_type=F32)


def _segsum(x, seg):
    hi = x.astype(BF16)
    lo = (x - hi.astype(F32)).astype(BF16)
    return (jnp.dot(hi, seg, preferred_element_type=F32) + jnp.dot(lo, seg, preferred_element_type=F32))


def _sigmoid(x):
    return 1.0 / (1.0 + jnp.exp(-x))


def _softplus(x):
    return jnp.maximum(x, 0.0) + jnp.log(1.0 + jnp.exp(-jnp.abs(x)))


def _layer_norm(x, w, b):
    mu = jnp.mean(x, axis=-1, keepdims=True)
    xc = x - mu
    var = jnp.mean(xc * xc, axis=-1, keepdims=True)
    return xc * lax.rsqrt(var + LN_EPS) * w + b


def _ln_proj_kernel(x_ref, lnw_ref, lnb_ref, w_ref, wfz_ref, h_ref, prw_ref, pfx_ref, fzt_ref, *, n_rw):
    h = _layer_norm(x_ref[...], lnw_ref[...], lnb_ref[...])
    h_ref[...] = h
    p = jnp.dot(h.astype(BF16), w_ref[...], preferred_element_type=F32)
    prw_ref[...] = p[:, :n_rw]
    pfx_ref[...] = p[:, n_rw:]
    fzt_ref[...] = lax.dot_general(wfz_ref[...], h, (((1,), (1,)), ((), ())),
                                   precision=HIGHEST, preferred_element_type=F32)


def _ln_proj(x2, lnw, lnb, w_main, wfz_t, n_rw):
    T, D = x2.shape
    n_all = w_main.shape[1]
    n_fx = n_all - n_rw
    nh = wfz_t.shape[0]
    tm = ROW_TILE
    return pl.pallas_call(
        functools.partial(_ln_proj_kernel, n_rw=n_rw),
        grid=(T // tm,),
        in_specs=[
            pl.BlockSpec((tm, D), lambda i: (i, 0)),
            pl.BlockSpec((1, D), lambda i: (0, 0)),
            pl.BlockSpec((1, D), lambda i: (0, 0)),
            pl.BlockSpec((D, n_all), lambda i: (0, 0)),
            pl.BlockSpec((nh, D), lambda i: (0, 0)),
        ],
        out_specs=[
            pl.BlockSpec((tm, D), lambda i: (i, 0)),
            pl.BlockSpec((tm, n_rw), lambda i: (i, 0)),
            pl.BlockSpec((tm, n_fx), lambda i: (i, 0)),
            pl.BlockSpec((nh, tm), lambda i: (0, i)),
        ],
        out_shape=[
            jax.ShapeDtypeStruct((T, D), F32),
            jax.ShapeDtypeStruct((T, n_rw), F32),
            jax.ShapeDtypeStruct((T, n_fx), F32),
            jax.ShapeDtypeStruct((nh, T), F32),
        ],
        compiler_params=_cparams(("parallel",)),
        name="ln_proj",
    )(x2, lnw, lnb, w_main, wfz_t)


def _rw_prep_kernel(p_ref, mu_ref, w0_ref, w2_ref, a0_ref, a2_ref, g2_ref, kkw_ref, ka_ref, rk_ref, seg_ref,
                    r_ref, lw_ref, k_ref, v_ref, kk_ref, alr_ref, g_ref, bonus_ref, carry_ref, *, width):
    @pl.when(pl.program_id(1) == 0)
    def _():
        carry_ref[...] = jnp.zeros_like(carry_ref)

    p = p_ref[...]
    tm = p.shape[0]
    prev = pltpu.roll(p, 1, axis=0)
    first_row = lax.broadcasted_iota(jnp.int32, p.shape, 0) == 0
    prev = jnp.where(first_row, carry_ref[...], prev)
    carry_ref[...] = p[tm - 1:tm, :]
    ps = p + mu_ref[...] * (prev - p)

    r = ps[:, 0:width]
    k = ps[:, width:2 * width]
    v = ps[:, 2 * width:3 * width]
    lora = ps[:, 3 * width:]
    seg = seg_ref[...]

    wl = w0_ref[...] + _dot(jnp.tanh(lora), w2_ref[...])
    w_raw = -_softplus(-wl) - 0.5
    lw_ref[...] = -jnp.exp(w_raw)
    alr = _sigmoid(a0_ref[...] + _dot(lora, a2_ref[...]))
    g_ref[...] = _dot(_sigmoid(lora), g2_ref[...])
    kkp = k * kkw_ref[...]
    nrm = jnp.sqrt(_segsum(kkp * kkp, seg))
    kk_ref[...] = kkp / jnp.maximum(nrm, 1e-12)
    k2 = k * (1.0 + (alr - 1.0) * ka_ref[...])
    bonus_ref[...] = _segsum(r * k2 * rk_ref[...], seg) * v
    r_ref[...] = r
    k_ref[...] = k2
    v_ref[...] = v
    alr_ref[...] = alr


def _rw_prep(p_rw, B, S, width, mu, w0, w2p, a0, a2p, g2p, kkw, ka, rk, seg):
    T, n_rw = p_rw.shape
    tm = ROW_TILE
    ns = S // tm
    n_lora = n_rw - 3 * width
    row = lambda b, s: (b * ns + s, 0)
    fixed = lambda b, s: (0, 0)
    vec = pl.BlockSpec((1, width), fixed)
    out = pl.BlockSpec((tm, width), row)
    return pl.pallas_call(
        functools.partial(_rw_prep_kernel, width=width),
        grid=(B, ns),
        in_specs=[
            pl.BlockSpec((tm, n_rw), row),
            pl.BlockSpec((1, n_rw), fixed),
            vec, pl.BlockSpec((n_lora, width), fixed),
            vec, pl.BlockSpec((n_lora, width), fixed),
            pl.BlockSpec((n_lora, width), fixed),
            vec, vec, vec,
            pl.BlockSpec((width, width), fixed),
        ],
        out_specs=[out] * 8,
        out_shape=[jax.ShapeDtypeStruct((T, width), F32)] * 8,
        scratch_shapes=[pltpu.VMEM((1, n_rw), F32)],
        compiler_params=_cparams(("parallel", "arbitrary")),
        name="rw_prep",
    )(p_rw, mu, w0, w2p, a0, a2p, g2p, kkw, ka, rk, seg)


def _stack_heads(x):
    lane = lax.broadcasted_iota(jnp.int32, x.shape, 1)
    return jnp.concatenate([jnp.where(lane < HEAD_DIM, x, 0.0), jnp.where(lane >= HEAD_DIM, x, 0.0)], axis=0)


def _wkv_kernel(r_ref, lw_ref, k_ref, v_ref, kk_ref, alr_ref, g_ref, bonus_ref, gnw_ref, gnb_ref, tri_ref,
                segm_ref, y_ref, state_ref):
    C = WKV_CHUNK

    @pl.when(pl.program_id(1) == 0)
    def _():
        state_ref[...] = jnp.zeros_like(state_ref)

    lw = lw_ref[...]
    cum = jnp.dot(tri_ref[...], lw, precision=HIGHEST, preferred_element_type=F32)
    total = cum[C - 1:C, :]
    p_inc = jnp.exp(cum)
    p_exc = jnp.exp(cum - lw)
    p_inv = jnp.exp(-cum)
    p_rem = jnp.exp(total - cum)
    p_tot = jnp.exp(total)
    kk = kk_ref[...]
    k2 = k_ref[...]
    b = kk * alr_ref[...]
    a_t = -kk * p_exc
    r_t = r_ref[...] * p_inc
    b_t = b * p_inv
    k_t = k2 * p_inv
    b_h = b * p_rem
    k_h = k2 * p_rem
    v = v_ref[...]

    ri = lax.broadcasted_iota(jnp.int32, (2 * C, 2 * C), 0)
    ci = lax.broadcasted_iota(jnp.int32, (2 * C, 2 * C), 1)
    same = (ri // C) == (ci // C)
    strict = same & ((ci % C) < (ri % C))
    incl = same & ((ci % C) <= (ri % C))
    eye = (ri == ci).astype(F32)

    ys = []
    for p in range(lw.shape[1] // PAIR):
        sl = slice(p * PAIR, (p + 1) * PAIR)
        a2, r2, b2, kt2 = (_stack_heads(t[:, sl]) for t in (a_t, r_t, b_t, k_t))
        bh2, kh2, v2 = (_stack_heads(t[:, sl]) for t in (b_h, k_h, v))
        m = _dot_t(jnp.concatenate([a2, r2], axis=0), jnp.concatenate([b2, kt2], axis=0))
        n_ab = jnp.where(strict, m[:2 * C, :2 * C], 0.0)
        m_ak = jnp.where(strict, m[:2 * C, 2 * C:], 0.0)
        m_rb = jnp.where(incl, m[2 * C:, :2 * C], 0.0)
        m_rk = jnp.where(incl, m[2 * C:, 2 * C:], 0.0)
        inv = eye + n_ab
        pw = n_ab
        for _ in range(C.bit_length() - 2):
            pw = _dot(pw, pw)
            inv = inv + _dot(inv, pw)
        wu = _dot(inv, jnp.concatenate([a2, _dot(m_ak, v2)], axis=1))
        qy = _dot(m_rb, wu) + jnp.concatenate([r2, _dot(m_rk, v2)], axis=1)
        s0 = state_ref[p]
        y2 = _dot(qy[:, :PAIR], s0) + qy[:, PAIR:]
        ys.append(y2[:C] + y2[C:])
        g_t = _dot(bh2.T, wu[:, :PAIR]) + eye * p_tot[:, sl]
        h_t = _dot(jnp.concatenate([bh2.T, kh2.T], axis=1), jnp.concatenate([wu[:, PAIR:], v2], axis=0))
        state_ref[p] = _dot(g_t, s0) + h_t

    y = jnp.concatenate(ys, axis=1)
    segm = segm_ref[...]
    mean = _segsum(y, segm)
    yc = y - mean
    var = _segsum(yc * yc, segm)
    yn = yc * lax.rsqrt(var + RW_GN_EPS) * gnw_ref[...] + gnb_ref[...]
    y_ref[...] = ((yn + bonus_ref[...]) * g_ref[...]).astype(y_ref.dtype)


def _wkv(r, lw, k2, v, kk, alr, g, bonus, gnw, gnb, tri, segm, B, S):
    T, width = r.shape
    C = WKV_CHUNK
    nc = S // C
    row = lambda b, c: (b * nc + c, 0)
    fixed = lambda b, c: (0, 0)
    blk = pl.BlockSpec((C, width), row)
    vec = pl.BlockSpec((1, width), fixed)
    return pl.pallas_call(
        _wkv_kernel,
        grid=(B, nc),
        in_specs=[blk] * 8 + [vec, vec, pl.BlockSpec((C, C), fixed), pl.BlockSpec((width, width), fixed)],
        out_specs=blk,
        out_shape=jax.ShapeDtypeStruct((T, width), BF16),
        scratch_shapes=[pltpu.VMEM((width // PAIR, PAIR, PAIR), F32)],
        compiler_params=_cparams(("parallel", "arbitrary")),
        name="wkv",
    )(r, lw, k2, v, kk, alr, g, bonus, gnw, gnb, tri, segm)


def _fx_prep_kernel(q_ref, k_ref, v_ref, fzt_ref, bf_ref, qw_ref, kw_ref, segm_ref, triu_ref, tril_ref,
                    qn_ref, kn_ref, vb_ref, crow_ref, ccol_ref, carry_row_ref, carry_col_ref):
    @pl.when(pl.program_id(1) == 0)
    def _():
        carry_row_ref[...] = jnp.zeros_like(carry_row_ref)
        carry_col_ref[...] = jnp.zeros_like(carry_col_ref)

    segm = segm_ref[...]
    q = q_ref[...]
    k = k_ref[...]
    qn = q * lax.rsqrt(_segsum(q * q, segm) + QK_EPS) * qw_ref[...]
    kn = k * lax.rsqrt(_segsum(k * k, segm) + QK_EPS) * kw_ref[...]
    qn_ref[...] = (qn * (HEAD_DIM ** -0.5)).astype(qn_ref.dtype)
    kn_ref[...] = kn.astype(kn_ref.dtype)
    vb_ref[...] = v_ref[...].astype(vb_ref.dtype)

    lf = -_softplus(-(fzt_ref[...] + bf_ref[...]))
    tm = lf.shape[1]
    c_row = jnp.dot(lf, triu_ref[...], precision=HIGHEST, preferred_element_type=F32) + carry_row_ref[...]
    c_col = lax.dot_general(tril_ref[...], lf, (((1,), (1,)), ((), ())), precision=HIGHEST,
                            preferred_element_type=F32) + carry_col_ref[...]
    crow_ref[0] = c_row
    ccol_ref[0] = c_col
    carry_row_ref[...] = c_row[:, tm - 1:tm]
    carry_col_ref[...] = c_col[tm - 1:tm, :]


def _fx_prep(p_fx, fzt, bf, qw, kw, segm, triu, tril, B, S, width):
    T = p_fx.shape[0]
    nh = fzt.shape[0]
    tm = ROW_TILE
    ns = S // tm
    fixed = lambda b, s: (0, 0)
    col = lambda j: pl.BlockSpec((tm, width), lambda b, s: (b * ns + s, j))
    out = pl.BlockSpec((tm, width), lambda b, s: (b * ns + s, 0))
    vec = pl.BlockSpec((1, width), fixed)
    return pl.pallas_call(
        _fx_prep_kernel,
        grid=(B, ns),
        in_specs=[
            col(0), col(1), col(2),
            pl.BlockSpec((nh, tm), lambda b, s: (0, b * ns + s)),
            pl.BlockSpec((nh, 1), fixed),
            vec, vec,
            pl.BlockSpec((width, width), fixed),
            pl.BlockSpec((tm, tm), fixed),
            pl.BlockSpec((tm, tm), fixed),
        ],
        out_specs=[out, out, out,
                   pl.BlockSpec((1, nh, tm), lambda b, s: (b, 0, s)),
                   pl.BlockSpec((1, tm, nh), lambda b, s: (b, s, 0))],
        out_shape=[jax.ShapeDtypeStruct((T, width), BF16)] * 3 + [
            jax.ShapeDtypeStruct((B, nh, S), F32), jax.ShapeDtypeStruct((B, S, nh), F32)],
        scratch_shapes=[pltpu.VMEM((nh, 1), F32), pltpu.VMEM((1, nh), F32)],
        compiler_params=_cparams(("parallel", "arbitrary")),
        name="fx_prep",
    )(p_fx, p_fx, p_fx, fzt, bf, qw, kw, segm, triu, tril)


def _fox_attn_kernel(q_ref, k_ref, v_ref, crow_ref, ccol_ref, og_ref, o_ref, m_ref, l_ref, acc_ref):
    i = pl.program_id(2)
    j = pl.program_id(3)
    tq = q_ref.shape[0]
    tk = k_ref.shape[0]

    @pl.when(j == 0)
    def _():
        m_ref[...] = jnp.full_like(m_ref, NEG_BIG)
        l_ref[...] = jnp.zeros_like(l_ref)
        acc_ref[...] = jnp.zeros_like(acc_ref)

    def step(masked):
        q = q_ref[...]
        k = k_ref[...]
        v = v_ref[...]
        lane = lax.broadcasted_iota(jnp.int32, q.shape, 1)
        if masked:
            row = lax.broadcasted_iota(jnp.int32, (tq, tk), 0)
            col = lax.broadcasted_iota(jnp.int32, (tq, tk), 1)
            causal = col <= row
        for h in range(2):
            in_head = (lane >= HEAD_DIM) if h else (lane < HEAD_DIM)
            qh = jnp.where(in_head, q, jnp.zeros_like(q))
            s = lax.dot_general(qh, k, (((1,), (1,)), ((), ())), preferred_element_type=F32)
            s = s + (ccol_ref[0, 0, :, h:h + 1] - crow_ref[0, 0, h:h + 1, :])
            if masked:
                s = jnp.where(causal, s, NEG_BIG)
            m_old = m_ref[h]
            m_new = jnp.maximum(m_old, jnp.max(s, axis=1, keepdims=True))
            alpha = jnp.exp(m_old - m_new)
            pr = jnp.exp(s - m_new)
            l_ref[h] = alpha * l_ref[h] + jnp.sum(pr, axis=1, keepdims=True)
            acc_ref[h] = alpha * acc_ref[h] + jnp.dot(pr.astype(BF16), v, preferred_element_type=F32)
            m_ref[h] = m_new

    @pl.when(j < i)
    def _():
        step(False)

    @pl.when(j == i)
    def _():
        step(True)
        lane = lax.broadcasted_iota(jnp.int32, (tq, PAIR), 1)
        o = jnp.where(lane < HEAD_DIM, acc_ref[0] / l_ref[0], acc_ref[1] / l_ref[1])
        o_ref[...] = (o * _sigmoid(og_ref[...])).astype(o_ref.dtype)


def _fox_attn(qn, kn, vb, c_row, c_col, p_fx, B, S, width):
    T = qn.shape[0]
    tq, tk = ATTN_TQ, ATTN_TK
    assert tq == tk
    nq = S // tq
    npair = width // PAIR
    og_col0 = 3 * width // PAIR
    kv = pl.BlockSpec((tk, PAIR), lambda b, p, i, j: (b * nq + jnp.minimum(i, j), p))
    qo = pl.BlockSpec((tq, PAIR), lambda b, p, i, j: (b * nq + i, p))
    return pl.pallas_call(
        _fox_attn_kernel,
        grid=(B, npair, nq, nq),
        in_specs=[
            qo, kv, kv,
            pl.BlockSpec((1, 1, 2, tk), lambda b, p, i, j: (b, p, 0, jnp.minimum(i, j))),
            pl.BlockSpec((1, 1, tq, 2), lambda b, p, i, j: (b, p, i, 0)),
            pl.BlockSpec((tq, PAIR), lambda b, p, i, j: (b * nq + i, og_col0 + p)),
        ],
        out_specs=qo,
        out_shape=jax.ShapeDtypeStruct((T, width), BF16),
        scratch_shapes=[pltpu.VMEM((2, tq, 1), F32), pltpu.VMEM((2, tq, 1), F32), pltpu.VMEM((2, tq, PAIR), F32)],
        compiler_params=_cparams(("parallel", "parallel", "parallel", "arbitrary")),
        name="fox_attn",
    )(qn, kn, vb, c_row, c_col, p_fx)


def _out_ln_kernel(yrw_ref, yfx_ref, h_ref, wo1_ref, wo2_ref, lnw_ref, lnb_ref, rw_ref, rb_ref,
                   h1_ref, h1b_ref, logit_ref, *, alpha):
    mix = (jnp.dot(yrw_ref[...], wo1_ref[...], preferred_element_type=F32)
           + jnp.dot(yfx_ref[...], wo2_ref[...], preferred_element_type=F32))
    h1 = _layer_norm(alpha * h_ref[...] + mix, lnw_ref[...], lnb_ref[...])
    h1_ref[...] = h1
    h1b_ref[...] = h1.astype(h1b_ref.dtype)
    logit_ref[...] = jnp.dot(h1, rw_ref[...], precision=HIGHEST, preferred_element_type=F32) + rb_ref[...]


def _out_ln(y_rw, y_fx, h0, wo1, wo2, lnw, lnb, rw, rb, alpha):
    T, D = h0.shape
    width = y_rw.shape[1]
    ne = rw.shape[1]
    tm = ROW_TILE
    row = lambda i: (i, 0)
    fixed = lambda i: (0, 0)
    return pl.pallas_call(
        functools.partial(_out_ln_kernel, alpha=alpha),
        grid=(T // tm,),
        in_specs=[
            pl.BlockSpec((tm, width), row), pl.BlockSpec((tm, width), row), pl.BlockSpec((tm, D), row),
            pl.BlockSpec((width, D), fixed), pl.BlockSpec((width, D), fixed),
            pl.BlockSpec((1, D), fixed), pl.BlockSpec((1, D), fixed),
            pl.BlockSpec((D, ne), fixed), pl.BlockSpec((1, ne), fixed),
        ],
        out_specs=[pl.BlockSpec((tm, D), row), pl.BlockSpec((tm, D), row), pl.BlockSpec((tm, ne), row)],
        out_shape=[jax.ShapeDtypeStruct((T, D), F32), jax.ShapeDtypeStruct((T, D), BF16),
                   jax.ShapeDtypeStruct((T, ne), F32)],
        compiler_params=_cparams(("parallel",)),
        name="out_ln",
    )(y_rw, y_fx, h0, wo1, wo2, lnw, lnb, rw, rb)


def _moe_kernel(bexp_ref, nused_ref, x_ref, w1g_ref, w1l_ref, b1g_ref, b1l_ref, w2_ref, b2_ref, o_ref):
    i = pl.program_id(0)

    @pl.when(i < nused_ref[0])
    def _():
        x = x_ref[...]
        x_glu = jnp.minimum(jnp.dot(x, w1g_ref[0], preferred_element_type=F32) + b1g_ref[0], SWIGLU_LIMIT)
        x_lin = jnp.clip(jnp.dot(x, w1l_ref[0], preferred_element_type=F32) + b1l_ref[0],
                         -SWIGLU_LIMIT, SWIGLU_LIMIT)
        act = x_glu * _sigmoid(SWIGLU_ALPHA * x_glu) * (x_lin + 1.0)
        o_ref[...] = jnp.dot(act.astype(BF16), w2_ref[0], preferred_element_type=F32) + b2_ref[0]

    @pl.when(i >= nused_ref[0])
    def _():
        o_ref[...] = jnp.zeros_like(o_ref)


def _moe_ffn(block_exp, n_used, xs, w1g, w1l, b1g, b1l, w2, b2):
    n_rows, D = xs.shape
    F = w1g.shape[2]
    tm = MOE_TILE
    n_blocks = n_rows // tm
    live = lambda i, be, nu: jnp.minimum(i, nu[0] - 1)
    wspec = lambda shape: pl.BlockSpec((1,) + shape, lambda i, be, nu: (be[live(i, be, nu)], 0, 0))
    grid_spec = pltpu.PrefetchScalarGridSpec(
        num_scalar_prefetch=2,
        grid=(n_blocks,),
        in_specs=[
            pl.BlockSpec((tm, D), lambda i, be, nu: (live(i, be, nu), 0)),
            wspec((D, F)), wspec((D, F)), wspec((1, F)), wspec((1, F)), wspec((F, D)), wspec((1, D)),
        ],
        out_specs=pl.BlockSpec((tm, D), lambda i, be, nu: (i, 0)),
    )
    return pl.pallas_call(
        _moe_kernel,
        grid_spec=grid_spec,
        out_shape=jax.ShapeDtypeStruct((n_rows, D), F32),
        compiler_params=_cparams(("arbitrary",)),
        name="moe_ffn",
    )(block_exp, n_used, xs, w1g, w1l, b1g, b1l, w2, b2)


def _combine_ln_kernel(h_ref, eo_ref, gate_ref, lnw_ref, lnb_ref, o_ref, *, alpha):
    gates = gate_ref[...]
    ffn = eo_ref[0] * gates[:, 0:1]
    for j in range(1, TOP_K):
        ffn = ffn + eo_ref[j] * gates[:, j:j + 1]
    o_ref[...] = _layer_norm(alpha * h_ref[...] + ffn, lnw_ref[...], lnb_ref[...])


def _combine_ln(h1, eo, gates, lnw, lnb, alpha):
    T, D = h1.shape
    tm = ROW_TILE
    return pl.pallas_call(
        functools.partial(_combine_ln_kernel, alpha=alpha),
        grid=(T // tm,),
        in_specs=[
            pl.BlockSpec((tm, D), lambda i: (i, 0)),
            pl.BlockSpec((TOP_K, tm, D), lambda i: (0, i, 0)),
            pl.BlockSpec((tm, TOP_K), lambda i: (i, 0)),
            pl.BlockSpec((1, D), lambda i: (0, 0)),
            pl.BlockSpec((1, D), lambda i: (0, 0)),
        ],
        out_specs=pl.BlockSpec((tm, D), lambda i: (i, 0)),
        out_shape=jax.ShapeDtypeStruct((T, D), F32),
        compiler_params=_cparams(("parallel",)),
        name="combine_ln",
    )(h1, eo, gates, lnw, lnb)


def _pad_to(x, n, axis):
    pad = [(0, 0)] * x.ndim
    pad[axis] = (0, n - x.shape[axis])
    return jnp.pad(x, pad)


def _block_diag_ones(width, value=1.0):
    idx = jnp.arange(width) // HEAD_DIM
    return jnp.where(idx[:, None] == idx[None, :], value, 0.0).astype(BF16)


def _route(logits, n_experts, tile):
    T = logits.shape[0]
    top_val, top_idx = lax.top_k(logits[:, :n_experts], TOP_K)
    gates = jax.nn.softmax(top_val, axis=-1)
    e_flat = top_idx.reshape(-1).astype(jnp.int32)
    onehot = (e_flat[:, None] == jnp.arange(n_experts, dtype=jnp.int32)[None, :]).astype(jnp.int32)
    csum = jnp.cumsum(onehot, axis=0)
    rank = jnp.take_along_axis(csum, e_flat[:, None], axis=1)[:, 0] - 1
    counts = csum[-1]
    padded = (counts + tile - 1) // tile * tile
    pends = jnp.cumsum(padded)
    pstarts = pends - padded
    dest = (pstarts[e_flat] + rank).astype(jnp.int32)
    n_rows = T * TOP_K + n_experts * tile
    n_blocks = n_rows // tile
    tok = jnp.arange(T * TOP_K, dtype=jnp.int32) // TOP_K
    row_tok = jnp.zeros((n_rows,), jnp.int32).at[dest].set(tok)
    block_exp = jnp.minimum(jnp.searchsorted(pends, jnp.arange(n_blocks, dtype=jnp.int32) * tile, side='right'),
                            n_experts - 1).astype(jnp.int32)
    n_used = (pends[-1] // tile).astype(jnp.int32).reshape(1)
    return gates, dest.reshape(T, TOP_K), row_tok, block_exp, n_used


def kernel(x, ln_in_w, ln_in_b, w_in, rw_mu, rw_w0, rw_w2, rw_a0, rw_a2, rw_g2, rw_k_k, rw_k_a, rw_r_k,
           rw_gn_w, rw_gn_b, fx_b_f, fx_q_norm, fx_k_norm, w_o, ln1_w, ln1_b, router_w, router_b,
           exp_w1, exp_b1, exp_w2, exp_b2, ln2_w, ln2_b):
    B, S, D = x.shape
    T = B * S
    depth = w_in.shape[0]
    alpha = (2 * depth) ** 0.25
    rw_w = rw_w0.shape[1]
    fx_heads = fx_b_f.shape[1]
    fx_w = fx_heads * HEAD_DIM
    d_lora, a_lora, g_lora = rw_w2.shape[1], rw_a2.shape[1], rw_g2.shape[1]
    n_lora = d_lora + a_lora + g_lora
    lora_pad = -(-n_lora // 128) * 128
    rw_cols = 3 * rw_w + n_lora
    n_rw = 3 * rw_w + lora_pad
    n_experts = router_w.shape[2]
    ne_pad = -(-n_experts // 128) * 128
    ffw = exp_w2.shape[2]
    row = lambda a: a.reshape(1, -1)

    seg_rw = _block_diag_ones(rw_w)
    segm_rw = _block_diag_ones(rw_w, 1.0 / HEAD_DIM)
    segm_fx = _block_diag_ones(fx_w, 1.0 / HEAD_DIM)
    tidx = jnp.arange(ROW_TILE)
    triu = (tidx[:, None] <= tidx[None, :]).astype(F32)
    tril = triu.T
    cidx = jnp.arange(WKV_CHUNK)
    tri_c = (cidx[:, None] >= cidx[None, :]).astype(F32)

    assert depth == 1, "single-layer block"
    l = 0
    x2 = x.reshape(T, D)
    w_l = w_in[l]
    w_main = jnp.concatenate(
        [_pad_to(w_l[:, :rw_cols], n_rw, 1), w_l[:, rw_cols:rw_cols + 4 * fx_w]], axis=1).astype(BF16)
    wfz_t = w_l[:, rw_cols + 4 * fx_w:].T
    h0, p_rw, p_fx, fzt = _ln_proj(x2, row(ln_in_w), row(ln_in_b), w_main, wfz_t, n_rw)

    mu = _pad_to(row(rw_mu[l]), n_rw, 1)
    w2p = _pad_to(rw_w2[l], lora_pad, 0).astype(BF16)
    a2p = _pad_to(jnp.pad(rw_a2[l], ((d_lora, 0), (0, 0))), lora_pad, 0).astype(BF16)
    g2p = _pad_to(jnp.pad(rw_g2[l], ((d_lora + a_lora, 0), (0, 0))), lora_pad, 0).astype(BF16)
    r, lw, k2, v, kk, alr, g, bonus = _rw_prep(
        p_rw, B, S, rw_w, mu, row(rw_w0[l]), w2p, row(rw_a0[l]), a2p, g2p,
        row(rw_k_k[l]), row(rw_k_a[l]), row(rw_r_k[l]), seg_rw)
    y_rw = _wkv(r, lw, k2, v, kk, alr, g, bonus, row(rw_gn_w[l]), row(rw_gn_b[l]), tri_c, segm_rw, B, S)

    qw = row(jnp.tile(fx_q_norm[l], fx_heads))
    kw = row(jnp.tile(fx_k_norm[l], fx_heads))
    qn, kn, vb, c_row, c_col = _fx_prep(p_fx, fzt, fx_b_f[l].reshape(-1, 1), qw, kw, segm_fx, triu, tril,
                                        B, S, fx_w)
    c_row = c_row.reshape(B, fx_heads // 2, 2, S)
    c_col = c_col.reshape(B, S, fx_heads // 2, 2).transpose(0, 2, 1, 3)
    y_fx = _fox_attn(qn, kn, vb, c_row, c_col, p_fx, B, S, fx_w)

    wo = w_o[l].astype(BF16)
    rw_pad = _pad_to(router_w[l], ne_pad, 1)
    rb_pad = _pad_to(row(router_b[l]), ne_pad, 1)
    h1, h1b, logits = _out_ln(y_rw, y_fx, h0, wo[:rw_w], wo[rw_w:], row(ln1_w[l]), row(ln1_b[l]),
                              rw_pad, rb_pad, alpha)

    gates, pos, row_tok, block_exp, n_used = _route(logits, n_experts, MOE_TILE)
    xs = h1b[row_tok]
    w1 = exp_w1[l]
    w1g = w1[:, :, 0::2].astype(BF16)
    w1l = w1[:, :, 1::2].astype(BF16)
    b1 = exp_b1[l]
    b1g = b1[:, None, 0::2]
    b1l = b1[:, None, 1::2]
    eo_rows = _moe_ffn(block_exp, n_used, xs, w1g, w1l, b1g, b1l, exp_w2[l].astype(BF16), exp_b2[l][:, None, :])
    eo = eo_rows[pos.T]
    h = _combine_ln(h1, eo, gates, row(ln2_w[l]), row(ln2_b[l]), alpha)
    return h.reshape(B, S, D)
```

```python
import functools

import jax
import jax.numpy as jnp
from jax import lax
from jax.experimental import pallas as pl
from jax.experimental.pallas import tpu as pltpu

F32 = jnp.float32
BF16 = jnp.bfloat16
HIGHEST = lax.Precision.HIGHEST

HEAD_DIM = 64
PAIR = 2 * HEAD_DIM
WKV_CHUNK = 64
RW_GN_EPS = 64e-5
QK_EPS = 1e-6
LN_EPS = 1e-5
TOP_K = 4
SWIGLU_ALPHA = 1.702
SWIGLU_LIMIT = 7.0
NEG_BIG = -1e30

ROW_TILE = 256
ATTN_TQ = 512
ATTN_TK = 512
MOE_TILE = 256
VMEM_LIMIT = 48 * 1024 * 1024


def _cparams(sem):
    return pltpu.CompilerParams(dimension_semantics=sem, vmem_limit_bytes=VMEM_LIMIT)


def _dot(a, b):
    return jnp.dot(a.astype(BF16), b.astype(BF16), preferred_element_type=F32)


def _dot_t(a, b):
    return lax.dot_general(a.astype(BF16), b.astype(BF16), (((1,), (1,)), ((), ())),
                           preferred_element_type=F32)


def _segsum(x, seg):
    hi = x.astype(BF16)
    lo = (x - hi.astype(F32)).astype(BF16)
    return (jnp.dot(hi, seg, preferred_element_type=F32) + jnp.dot(lo, seg, preferred_element_type=F32))


def _sigmoid(x):
    return 1.0 / (1.0 + jnp.exp(-x))


def _softplus(x):
    return jnp.maximum(x, 0.0) + jnp.log(1.0 + jnp.exp(-jnp.abs(x)))


def _layer_norm(x, w, b):
    mu = jnp.mean(x, axis=-1, keepdims=True)
    xc = x - mu
    var = jnp.mean(xc * xc, axis=-1, keepdims=True)
    return xc * lax.rsqrt(var + LN_EPS) * w + b


def _ln_proj_kernel(x_ref, lnw_ref, lnb_ref, w_ref, wfz_ref, h_ref, prw_ref, pfx_ref, fzt_ref, *, n_rw):
    h = _layer_norm(x_ref[...], lnw_ref[...], lnb_ref[...])
    h_ref[...] = h
    p = jnp.dot(h.astype(BF16), w_ref[...], preferred_element_type=F32)
    prw_ref[...] = p[:, :n_rw]
    pfx_ref[...] = p[:, n_rw:]
    fzt_ref[...] = lax.dot_general(wfz_ref[...], h, (((1,), (1,)), ((), ())),
                                   precision=HIGHEST, preferred_element_type=F32)


def _ln_proj(x2, lnw, lnb, w_main, wfz_t, n_rw):
    T, D = x2.shape
    n_all = w_main.shape[1]
    n_fx = n_all - n_rw
    nh = wfz_t.shape[0]
    tm = ROW_TILE
    return pl.pallas_call(
        functools.partial(_ln_proj_kernel, n_rw=n_rw),
        grid=(T // tm,),
        in_specs=[
            pl.BlockSpec((tm, D), lambda i: (i, 0)),
            pl.BlockSpec((1, D), lambda i: (0, 0)),
            pl.BlockSpec((1, D), lambda i: (0, 0)),
            pl.BlockSpec((D, n_all), lambda i: (0, 0)),
            pl.BlockSpec((nh, D), lambda i: (0, 0)),
        ],
        out_specs=[
            pl.BlockSpec((tm, D), lambda i: (i, 0)),
            pl.BlockSpec((tm, n_rw), lambda i: (i, 0)),
            pl.BlockSpec((tm, n_fx), lambda i: (i, 0)),
            pl.BlockSpec((nh, tm), lambda i: (0, i)),
        ],
        out_shape=[
            jax.ShapeDtypeStruct((T, D), F32),
            jax.ShapeDtypeStruct((T, n_rw), F32),
            jax.ShapeDtypeStruct((T, n_fx), F32),
            jax.ShapeDtypeStruct((nh, T), F32),
        ],
        compiler_params=_cparams(("parallel",)),
        name="ln_proj",
    )(x2, lnw, lnb, w_main, wfz_t)


def _rw_prep_kernel(p_ref, mu_ref, w0_ref, w2_ref, a0_ref, a2_ref, g2_ref, kkw_ref, ka_ref, rk_ref, seg_ref,
                    r_ref, lw_ref, k_ref, v_ref, kk_ref, alr_ref, g_ref, bonus_ref, carry_ref, *, width):
    @pl.when(pl.program_id(1) == 0)
    def _():
        carry_ref[...] = jnp.zeros_like(carry_ref)

    p = p_ref[...]
    tm = p.shape[0]
    prev = pltpu.roll(p, 1, axis=0)
    first_row = lax.broadcasted_iota(jnp.int32, p.shape, 0) == 0
    prev = jnp.where(first_row, carry_ref[...], prev)
    carry_ref[...] = p[tm - 1:tm, :]
    ps = p + mu_ref[...] * (prev - p)

    r = ps[:, 0:width]
    k = ps[:, width:2 * width]
    v = ps[:, 2 * width:3 * width]
    lora = ps[:, 3 * width:]
    seg = seg_ref[...]

    wl = w0_ref[...] + _dot(jnp.tanh(lora), w2_ref[...])
    w_raw = -_softplus(-wl) - 0.5
    lw_ref[...] = -jnp.exp(w_raw)
    alr = _sigmoid(a0_ref[...] + _dot(lora, a2_ref[...]))
    g_ref[...] = _dot(_sigmoid(lora), g2_ref[...])
    kkp = k * kkw_ref[...]
    nrm = jnp.sqrt(_segsum(kkp * kkp, seg))
    kk_ref[...] = kkp / jnp.maximum(nrm, 1e-12)
    k2 = k * (1.0 + (alr - 1.0) * ka_ref[...])
    bonus_ref[...] = _segsum(r * k2 * rk_ref[...], seg) * v
    r_ref[...] = r
    k_ref[...] = k2
    v_ref[...] = v
    alr_ref[...] = alr


def _rw_prep(p_rw, B, S, width, mu, w0, w2p, a0, a2p, g2p, kkw, ka, rk, seg):
    T, n_rw = p_rw.shape
    tm = ROW_TILE
    ns = S // tm
    n_lora = n_rw - 3 * width
    row = lambda b, s: (b * ns + s, 0)
    fixed = lambda b, s: (0, 0)
    vec = pl.BlockSpec((1, width), fixed)
    out = pl.BlockSpec((tm, width), row)
    return pl.pallas_call(
        functools.partial(_rw_prep_kernel, width=width),
        grid=(B, ns),
        in_specs=[
            pl.BlockSpec((tm, n_rw), row),
            pl.BlockSpec((1, n_rw), fixed),
            vec, pl.BlockSpec((n_lora, width), fixed),
            vec, pl.BlockSpec((n_lora, width), fixed),
            pl.BlockSpec((n_lora, width), fixed),
            vec, vec, vec,
            pl.BlockSpec((width, width), fixed),
        ],
        out_specs=[out] * 8,
        out_shape=[jax.ShapeDtypeStruct((T, width), F32)] * 8,
        scratch_shapes=[pltpu.VMEM((1, n_rw), F32)],
        compiler_params=_cparams(("parallel", "arbitrary")),
        name="rw_prep",
    )(p_rw, mu, w0, w2p, a0, a2p, g2p, kkw, ka, rk, seg)


def _stack_heads(x):
    lane = lax.broadcasted_iota(jnp.int32, x.shape, 1)
    return jnp.concatenate([jnp.where(lane < HEAD_DIM, x, 0.0), jnp.where(lane >= HEAD_DIM, x, 0.0)], axis=0)


def _wkv_kernel(r_ref, lw_ref, k_ref, v_ref, kk_ref, alr_ref, g_ref, bonus_ref, gnw_ref, gnb_ref, tri_ref,
                segm_ref, y_ref, state_ref):
    C = WKV_CHUNK

    @pl.when(pl.program_id(1) == 0)
    def _():
        state_ref[...] = jnp.zeros_like(state_ref)

    lw = lw_ref[...]
    cum = jnp.dot(tri_ref[...], lw, precision=HIGHEST, preferred_element_type=F32)
    total = cum[C - 1:C, :]
    p_inc = jnp.exp(cum)
    p_exc = jnp.exp(cum - lw)
    p_inv = jnp.exp(-cum)
    p_rem = jnp.exp(total - cum)
    p_tot = jnp.exp(total)
    kk = kk_ref[...]
    k2 = k_ref[...]
    b = kk * alr_ref[...]
    a_t = -kk * p_exc
    r_t = r_ref[...] * p_inc
    b_t = b * p_inv
    k_t = k2 * p_inv
    b_h = b * p_rem
    k_h = k2 * p_rem
    v = v_ref[...]

    ri = lax.broadcasted_iota(jnp.int32, (2 * C, 2 * C), 0)
    ci = lax.broadcasted_iota(jnp.int32, (2 * C, 2 * C), 1)
    same = (ri // C) == (ci // C)
    strict = same & ((ci % C) < (ri % C))
    incl = same & ((ci % C) <= (ri % C))
    eye = (ri == ci).astype(F32)

    ys = []
    for p in range(lw.shape[1] // PAIR):
        sl = slice(p * PAIR, (p + 1) * PAIR)
        a2, r2, b2, kt2 = (_stack_heads(t[:, sl]) for t in (a_t, r_t, b_t, k_t))
        bh2, kh2, v2 = (_stack_heads(t[:, sl]) for t in (b_h, k_h, v))
        m = _dot_t(jnp.concatenate([a2, r2], axis=0), jnp.concatenate([b2, kt2], axis=0))
        n_ab = jnp.where(strict, m[:2 * C, :2 * C], 0.0)
        m_ak = jnp.where(strict, m[:2 * C, 2 * C:], 0.0)
        m_rb = jnp.where(incl, m[2 * C:, :2 * C], 0.0)
        m_rk = jnp.where(incl, m[2 * C:, 2 * C:], 0.0)
        inv = eye + n_ab
        pw = n_ab
        for _ in range(C.bit_length() - 2):
            pw = _dot(pw, pw)
            inv = inv + _dot(inv, pw)
        wu = _dot(inv, jnp.concatenate([a2, _dot(m_ak, v2)], axis=1))
        qy = _dot(m_rb, wu) + jnp.concatenate([r2, _dot(m_rk, v2)], axis=1)
        s0 = state_ref[p]
        y2 = _dot(qy[:, :PAIR], s0) + qy[:, PAIR:]
        ys.append(y2[:C] + y2[C:])
        g_t = _dot(bh2.T, wu[:, :PAIR]) + eye * p_tot[:, sl]
        h_t = _dot(jnp.concatenate([bh2.T, kh2.T], axis=1), jnp.concatenate([wu[:, PAIR:], v2], axis=0))
        state_ref[p] = _dot(g_t, s0) + h_t

    y = jnp.concatenate(ys, axis=1)
    segm = segm_ref[...]
    mean = _segsum(y, segm)
    yc = y - mean
    var = _segsum(yc * yc, segm)
    yn = yc * lax.rsqrt(var + RW_GN_EPS) * gnw_ref[...] + gnb_ref[...]
    y_ref[...] = ((yn + bonus_ref[...]) * g_ref[...]).astype(y_ref.dtype)


def _wkv(r, lw, k2, v, kk, alr, g, bonus, gnw, gnb, tri, segm, B, S):
    T, width = r.shape
    C = WKV_CHUNK
    nc = S // C
    row = lambda b, c: (b * nc + c, 0)
    fixed = lambda b, c: (0, 0)
    blk = pl.BlockSpec((C, width), row)
    vec = pl.BlockSpec((1, width), fixed)
    return pl.pallas_call(
        _wkv_kernel,
        grid=(B, nc),
        in_specs=[blk] * 8 + [vec, vec, pl.BlockSpec((C, C), fixed), pl.BlockSpec((width, width), fixed)],
        out_specs=blk,
        out_shape=jax.ShapeDtypeStruct((T, width), BF16),
        scratch_shapes=[pltpu.VMEM((width // PAIR, PAIR, PAIR), F32)],
        compiler_params=_cparams(("parallel", "arbitrary")),
        name="wkv",
    )(r, lw, k2, v, kk, alr, g, bonus, gnw, gnb, tri, segm)


def _fx_prep_kernel(q_ref, k_ref, v_ref, fzt_ref, bf_ref, qw_ref, kw_ref, segm_ref, triu_ref, tril_ref,
                    qn_ref, kn_ref, vb_ref, crow_ref, ccol_ref, carry_row_ref, carry_col_ref):
    @pl.when(pl.program_id(1) == 0)
    def _():
        carry_row_ref[...] = jnp.zeros_like(carry_row_ref)
        carry_col_ref[...] = jnp.zeros_like(carry_col_ref)

    segm = segm_ref[...]
    q = q_ref[...]
    k = k_ref[...]
    qn = q * lax.rsqrt(_segsum(q * q, segm) + QK_EPS) * qw_ref[...]
    kn = k * lax.rsqrt(_segsum(k * k, segm) + QK_EPS) * kw_ref[...]
    qn_ref[...] = (qn * (HEAD_DIM ** -0.5)).astype(qn_ref.dtype)
    kn_ref[...] = kn.astype(kn_ref.dtype)
    vb_ref[...] = v_ref[...].astype(vb_ref.dtype)

    lf = -_softplus(-(fzt_ref[...] + bf_ref[...]))
    tm = lf.shape[1]
    c_row = jnp.dot(lf, triu_ref[...], precision=HIGHEST, preferred_element_type=F32) + carry_row_ref[...]
    c_col = lax.dot_general(tril_ref[...], lf, (((1,), (1,)), ((), ())), precision=HIGHEST,
                            preferred_element_type=F32) + carry_col_ref[...]
    crow_ref[0] = c_row
    ccol_ref[0] = c_col
    carry_row_ref[...] = c_row[:, tm - 1:tm]
    carry_col_ref[...] = c_col[tm - 1:tm, :]


def _fx_prep(p_fx, fzt, bf, qw, kw, segm, triu, tril, B, S, width):
    T = p_fx.shape[0]
    nh = fzt.shape[0]
    tm = ROW_TILE
    ns = S // tm
    fixed = lambda b, s: (0, 0)
    col = lambda j: pl.BlockSpec((tm, width), lambda b, s: (b * ns + s, j))
    out = pl.BlockSpec((tm, width), lambda b, s: (b * ns + s, 0))
    vec = pl.BlockSpec((1, width), fixed)
    return pl.pallas_call(
        _fx_prep_kernel,
        grid=(B, ns),
        in_specs=[
            col(0), col(1), col(2),
            pl.BlockSpec((nh, tm), lambda b, s: (0, b * ns + s)),
            pl.BlockSpec((nh, 1), fixed),
            vec, vec,
            pl.BlockSpec((width, width), fixed),
            pl.BlockSpec((tm, tm), fixed),
            pl.BlockSpec((tm, tm), fixed),
        ],
        out_specs=[out, out, out,
                   pl.BlockSpec((1, nh, tm), lambda b, s: (b, 0, s)),
                   pl.BlockSpec((1, tm, nh), lambda b, s: (b, s, 0))],
        out_shape=[jax.ShapeDtypeStruct((T, width), BF16)] * 3 + [
            jax.ShapeDtypeStruct((B, nh, S), F32), jax.ShapeDtypeStruct((B, S, nh), F32)],
        scratch_shapes=[pltpu.VMEM((nh, 1), F32), pltpu.VMEM((1, nh), F32)],
        compiler_params=_cparams(("parallel", "arbitrary")),
        name="fx_prep",
    )(p_fx, p_fx, p_fx, fzt, bf, qw, kw, segm, triu, tril)


def _fox_attn_kernel(q_ref, k_ref, v_ref, crow_ref, ccol_ref, og_ref, o_ref, m_ref, l_ref, acc_ref):
    i = pl.program_id(2)
    j = pl.program_id(3)
    tq = q_ref.shape[0]
    tk = k_ref.shape[0]

    @pl.when(j == 0)
    def _():
        m_ref[...] = jnp.full_like(m_ref, NEG_BIG)
        l_ref[...] = jnp.zeros_like(l_ref)
        acc_ref[...] = jnp.zeros_like(acc_ref)

    def step(masked):
        q = q_ref[...]
        k = k_ref[...]
        v = v_ref[...]
        lane = lax.broadcasted_iota(jnp.int32, q.shape, 1)
        if masked:
            row = lax.broadcasted_iota(jnp.int32, (tq, tk), 0)
            col = lax.broadcasted_iota(jnp.int32, (tq, tk), 1)
            causal = col <= row
        for h in range(2):
            in_head = (lane >= HEAD_DIM) if h else (lane < HEAD_DIM)
            qh = jnp.where(in_head, q, jnp.zeros_like(q))
            s = lax.dot_general(qh, k, (((1,), (1,)), ((), ())), preferred_element_type=F32)
            s = s + (ccol_ref[0, 0, :, h:h + 1] - crow_ref[0, 0, h:h + 1, :])
            if masked:
                s = jnp.where(causal, s, NEG_BIG)
            m_old = m_ref[h]
            m_new = jnp.maximum(m_old, jnp.max(s, axis=1, keepdims=True))
            alpha = jnp.exp(m_old - m_new)
            pr = jnp.exp(s - m_new)
            l_ref[h] = alpha * l_ref[h] + jnp.sum(pr, axis=1, keepdims=True)
            acc_ref[h] = alpha * acc_ref[h] + jnp.dot(pr.astype(BF16), v, preferred_element_type=F32)
            m_ref[h] = m_new

    @pl.when(j < i)
    def _():
        step(False)

    @pl.when(j == i)
    def _():
        step(True)
        lane = lax.broadcasted_iota(jnp.int32, (tq, PAIR), 1)
        o = jnp.where(lane < HEAD_DIM, acc_ref[0] / l_ref[0], acc_ref[1] / l_ref[1])
        o_ref[...] = (o * _sigmoid(og_ref[...])).astype(o_ref.dtype)


def _fox_attn(qn, kn, vb, c_row, c_col, p_fx, B, S, width):
    T = qn.shape[0]
    tq, tk = ATTN_TQ, ATTN_TK
    assert tq == tk
    nq = S // tq
    npair = width // PAIR
    og_col0 = 3 * width // PAIR
    kv = pl.BlockSpec((tk, PAIR), lambda b, p, i, j: (b * nq + jnp.minimum(i, j), p))
    qo = pl.BlockSpec((tq, PAIR), lambda b, p, i, j: (b * nq + i, p))
    return pl.pallas_call(
        _fox_attn_kernel,
        grid=(B, npair, nq, nq),
        in_specs=[
            qo, kv, kv,
            pl.BlockSpec((1, 1, 2, tk), lambda b, p, i, j: (b, p, 0, jnp.minimum(i, j))),
            pl.BlockSpec((1, 1, tq, 2), lambda b, p, i, j: (b, p, i, 0)),
            pl.BlockSpec((tq, PAIR), lambda b, p, i, j: (b * nq + i, og_col0 + p)),
        ],
        out_specs=qo,
        out_shape=jax.ShapeDtypeStruct((T, width), BF16),
        scratch_shapes=[pltpu.VMEM((2, tq, 1), F32), pltpu.VMEM((2, tq, 1), F32), pltpu.VMEM((2, tq, PAIR), F32)],
        compiler_params=_cparams(("parallel", "parallel", "parallel", "arbitrary")),
        name="fox_attn",
    )(qn, kn, vb, c_row, c_col, p_fx)


def _out_ln_kernel(yrw_ref, yfx_ref, h_ref, wo1_ref, wo2_ref, lnw_ref, lnb_ref, rw_ref, rb_ref,
                   h1_ref, logit_ref, *, alpha):
    mix = (jnp.dot(yrw_ref[...], wo1_ref[...], preferred_element_type=F32)
           + jnp.dot(yfx_ref[...], wo2_ref[...], preferred_element_type=F32))
    h1 = _layer_norm(alpha * h_ref[...] + mix, lnw_ref[...], lnb_ref[...])
    h1_ref[...] = h1
    logit_ref[...] = jnp.dot(h1, rw_ref[...], precision=HIGHEST, preferred_element_type=F32) + rb_ref[...]


def _out_ln(y_rw, y_fx, h0, wo1, wo2, lnw, lnb, rw, rb, alpha):
    T, D = h0.shape
    width = y_rw.shape[1]
    ne = rw.shape[1]
    tm = ROW_TILE
    row = lambda i: (i, 0)
    fixed = lambda i: (0, 0)
    return pl.pallas_call(
        functools.partial(_out_ln_kernel, alpha=alpha),
        grid=(T // tm,),
        in_specs=[
            pl.BlockSpec((tm, width), row), pl.BlockSpec((tm, width), row), pl.BlockSpec((tm, D), row),
            pl.BlockSpec((width, D), fixed), pl.BlockSpec((width, D), fixed),
            pl.BlockSpec((1, D), fixed), pl.BlockSpec((1, D), fixed),
            pl.BlockSpec((D, ne), fixed), pl.BlockSpec((1, ne), fixed),
        ],
        out_specs=[pl.BlockSpec((tm, D), row), pl.BlockSpec((tm, ne), row)],
        out_shape=[jax.ShapeDtypeStruct((T, D), F32), jax.ShapeDtypeStruct((T, ne), F32)],
        compiler_params=_cparams(("parallel",)),
        name="out_ln",
    )(y_rw, y_fx, h0, wo1, wo2, lnw, lnb, rw, rb)


DEINT_COLS = 256


def _w1_split_kernel(w_ref, perm_ref, g_ref, l_ref):
    half = DEINT_COLS // 2
    for c in range(w_ref.shape[2] // DEINT_COLS):
        blk = w_ref[0, :, c * DEINT_COLS:(c + 1) * DEINT_COLS].astype(BF16)
        out = jnp.dot(blk, perm_ref[...], preferred_element_type=F32).astype(BF16)
        g_ref[0, :, c * half:(c + 1) * half] = out[:, :half]
        l_ref[0, :, c * half:(c + 1) * half] = out[:, half:]


def _w1_split(w1):
    E, D, F2 = w1.shape
    tr = 512
    half = DEINT_COLS // 2
    src = jnp.arange(DEINT_COLS)
    dst = jnp.where(src % 2 == 0, src // 2, half + src // 2)
    perm = (dst[:, None] == jnp.arange(DEINT_COLS)[None, :]).astype(BF16)
    out = pl.BlockSpec((1, tr, F2 // 2), lambda e, r: (e, r, 0))
    return pl.pallas_call(
        _w1_split_kernel,
        grid=(E, D // tr),
        in_specs=[pl.BlockSpec((1, tr, F2), lambda e, r: (e, r, 0)),
                  pl.BlockSpec((DEINT_COLS, DEINT_COLS), lambda e, r: (0, 0))],
        out_specs=[out, out],
        out_shape=[jax.ShapeDtypeStruct((E, D, F2 // 2), BF16)] * 2,
        compiler_params=_cparams(("parallel", "parallel")),
        name="w1_split",
    )(w1, perm)


def _moe_kernel(bexp_ref, nused_ref, x_ref, w1g_ref, w1l_ref, b1g_ref, b1l_ref, w2_ref, b2_ref, o_ref):
    i = pl.program_id(0)

    @pl.when(i < nused_ref[0])
    def _():
        x = x_ref[...].astype(BF16)
        x_glu = jnp.minimum(jnp.dot(x, w1g_ref[0], preferred_element_type=F32) + b1g_ref[0], SWIGLU_LIMIT)
        x_lin = jnp.clip(jnp.dot(x, w1l_ref[0], preferred_element_type=F32) + b1l_ref[0],
                         -SWIGLU_LIMIT, SWIGLU_LIMIT)
        act = x_glu * _sigmoid(SWIGLU_ALPHA * x_glu) * (x_lin + 1.0)
        o_ref[...] = jnp.dot(act.astype(BF16), w2_ref[0], preferred_element_type=F32) + b2_ref[0]

    @pl.when(i >= nused_ref[0])
    def _():
        o_ref[...] = jnp.zeros_like(o_ref)


def _moe_ffn(block_exp, n_used, xs, w1g, w1l, b1g, b1l, w2, b2):
    n_rows, D = xs.shape
    F = w1g.shape[2]
    tm = MOE_TILE
    n_blocks = n_rows // tm
    live = lambda i, be, nu: jnp.minimum(i, nu[0] - 1)
    wspec = lambda shape: pl.BlockSpec((1,) + shape, lambda i, be, nu: (be[live(i, be, nu)], 0, 0))
    grid_spec = pltpu.PrefetchScalarGridSpec(
        num_scalar_prefetch=2,
        grid=(n_blocks,),
        in_specs=[
            pl.BlockSpec((tm, D), lambda i, be, nu: (live(i, be, nu), 0)),
            wspec((D, F)), wspec((D, F)), wspec((1, F)), wspec((1, F)), wspec((F, D)), wspec((1, D)),
        ],
        out_specs=pl.BlockSpec((tm, D), lambda i, be, nu: (i, 0)),
    )
    return pl.pallas_call(
        _moe_kernel,
        grid_spec=grid_spec,
        out_shape=jax.ShapeDtypeStruct((n_rows, D), F32),
        compiler_params=_cparams(("arbitrary",)),
        name="moe_ffn",
    )(block_exp, n_used, xs, w1g, w1l, b1g, b1l, w2, b2)


def _combine_ln_kernel(h_ref, eo_ref, gate_ref, lnw_ref, lnb_ref, o_ref, *, alpha):
    gates = gate_ref[...]
    ffn = eo_ref[0] * gates[:, 0:1]
    for j in range(1, TOP_K):
        ffn = ffn + eo_ref[j] * gates[:, j:j + 1]
    o_ref[...] = _layer_norm(alpha * h_ref[...] + ffn, lnw_ref[...], lnb_ref[...])


def _combine_ln(h1, eo, gates, lnw, lnb, alpha):
    T, D = h1.shape
    tm = ROW_TILE
    return pl.pallas_call(
        functools.partial(_combine_ln_kernel, alpha=alpha),
        grid=(T // tm,),
        in_specs=[
            pl.BlockSpec((tm, D), lambda i: (i, 0)),
            pl.BlockSpec((TOP_K, tm, D), lambda i: (0, i, 0)),
            pl.BlockSpec((tm, TOP_K), lambda i: (i, 0)),
            pl.BlockSpec((1, D), lambda i: (0, 0)),
            pl.BlockSpec((1, D), lambda i: (0, 0)),
        ],
        out_specs=pl.BlockSpec((tm, D), lambda i: (i, 0)),
        out_shape=jax.ShapeDtypeStruct((T, D), F32),
        compiler_params=_cparams(("parallel",)),
        name="combine_ln",
    )(h1, eo, gates, lnw, lnb)


def _pad_to(x, n, axis):
    pad = [(0, 0)] * x.ndim
    pad[axis] = (0, n - x.shape[axis])
    return jnp.pad(x, pad)


def _block_diag_ones(width, value=1.0):
    idx = jnp.arange(width) // HEAD_DIM
    return jnp.where(idx[:, None] == idx[None, :], value, 0.0).astype(BF16)


def _route(logits, n_experts, tile):
    T = logits.shape[0]
    top_val, top_idx = lax.top_k(logits[:, :n_experts], TOP_K)
    gates = jax.nn.softmax(top_val, axis=-1)
    e_flat = top_idx.reshape(-1).astype(jnp.int32)
    onehot = (e_flat[:, None] == jnp.arange(n_experts, dtype=jnp.int32)[None, :]).astype(jnp.int32)
    csum = jnp.cumsum(onehot, axis=0)
    rank = jnp.take_along_axis(csum, e_flat[:, None], axis=1)[:, 0] - 1
    counts = csum[-1]
    padded = (counts + tile - 1) // tile * tile
    pends = jnp.cumsum(padded)
    pstarts = pends - padded
    dest = (pstarts[e_flat] + rank).astype(jnp.int32)
    n_assign = T * TOP_K
    n_rows = n_assign + n_experts * tile
    n_blocks = n_rows // tile
    block_start = jnp.arange(n_blocks, dtype=jnp.int32) * tile
    block_exp = jnp.minimum(jnp.sum((pends[None, :] <= block_start[:, None]).astype(jnp.int32), axis=1),
                            n_experts - 1).astype(jnp.int32)
    n_used = (pends[-1] // tile).astype(jnp.int32).reshape(1)
    assert n_experts * n_assign < 2 ** 31
    order = lax.sort(e_flat * n_assign + jnp.arange(n_assign, dtype=jnp.int32)) % n_assign
    starts = jnp.cumsum(counts) - counts
    row_exp = jnp.repeat(block_exp, tile)
    slot = jnp.arange(n_rows, dtype=jnp.int32) - pstarts[row_exp]
    src = jnp.clip(starts[row_exp] + slot, 0, n_assign - 1)
    row_tok = jnp.where(slot < counts[row_exp], order[src] // TOP_K, 0).astype(jnp.int32)
    return gates, dest.reshape(T, TOP_K), row_tok, block_exp, n_used


def kernel(x, ln_in_w, ln_in_b, w_in, rw_mu, rw_w0, rw_w2, rw_a0, rw_a2, rw_g2, rw_k_k, rw_k_a, rw_r_k,
           rw_gn_w, rw_gn_b, fx_b_f, fx_q_norm, fx_k_norm, w_o, ln1_w, ln1_b, router_w, router_b,
           exp_w1, exp_b1, exp_w2, exp_b2, ln2_w, ln2_b):
    B, S, D = x.shape
    T = B * S
    depth = w_in.shape[0]
    alpha = (2 * depth) ** 0.25
    rw_w = rw_w0.shape[1]
    fx_heads = fx_b_f.shape[1]
    fx_w = fx_heads * HEAD_DIM
    d_lora, a_lora, g_lora = rw_w2.shape[1], rw_a2.shape[1], rw_g2.shape[1]
    n_lora = d_lora + a_lora + g_lora
    lora_pad = -(-n_lora // 128) * 128
    rw_cols = 3 * rw_w + n_lora
    n_rw = 3 * rw_w + lora_pad
    n_experts = router_w.shape[2]
    ne_pad = -(-n_experts // 128) * 128
    ffw = exp_w2.shape[2]
    row = lambda a: a.reshape(1, -1)

    seg_rw = _block_diag_ones(rw_w)
    segm_rw = _block_diag_ones(rw_w, 1.0 / HEAD_DIM)
    segm_fx = _block_diag_ones(fx_w, 1.0 / HEAD_DIM)
    tidx = jnp.arange(ROW_TILE)
    triu = (tidx[:, None] <= tidx[None, :]).astype(F32)
    tril = triu.T
    cidx = jnp.arange(WKV_CHUNK)
    tri_c = (cidx[:, None] >= cidx[None, :]).astype(F32)

    assert depth == 1, "single-layer block"
    l = 0
    x2 = x.reshape(T, D)
    w_l = w_in[l]
    w_main = jnp.concatenate(
        [_pad_to(w_l[:, :rw_cols], n_rw, 1), w_l[:, rw_cols:rw_cols + 4 * fx_w]], axis=1).astype(BF16)
    wfz_t = w_l[:, rw_cols + 4 * fx_w:].T
    h0, p_rw, p_fx, fzt = _ln_proj(x2, row(ln_in_w), row(ln_in_b), w_main, wfz_t, n_rw)

    mu = _pad_to(row(rw_mu[l]), n_rw, 1)
    w2p = _pad_to(rw_w2[l], lora_pad, 0).astype(BF16)
    a2p = _pad_to(jnp.pad(rw_a2[l], ((d_lora, 0), (0, 0))), lora_pad, 0).astype(BF16)
    g2p = _pad_to(jnp.pad(rw_g2[l], ((d_lora + a_lora, 0), (0, 0))), lora_pad, 0).astype(BF16)
    r, lw, k2, v, kk, alr, g, bonus = _rw_prep(
        p_rw, B, S, rw_w, mu, row(rw_w0[l]), w2p, row(rw_a0[l]), a2p, g2p,
        row(rw_k_k[l]), row(rw_k_a[l]), row(rw_r_k[l]), seg_rw)
    y_rw = _wkv(r, lw, k2, v, kk, alr, g, bonus, row(rw_gn_w[l]), row(rw_gn_b[l]), tri_c, segm_rw, B, S)

    qw = row(jnp.tile(fx_q_norm[l], fx_heads))
    kw = row(jnp.tile(fx_k_norm[l], fx_heads))
    qn, kn, vb, c_row, c_col = _fx_prep(p_fx, fzt, fx_b_f[l].reshape(-1, 1), qw, kw, segm_fx, triu, tril,
                                        B, S, fx_w)
    c_row = c_row.reshape(B, fx_heads // 2, 2, S)
    c_col = c_col.reshape(B, S, fx_heads // 2, 2).transpose(0, 2, 1, 3)
    y_fx = _fox_attn(qn, kn, vb, c_row, c_col, p_fx, B, S, fx_w)

    wo = w_o[l].astype(BF16)
    rw_pad = _pad_to(router_w[l], ne_pad, 1)
    rb_pad = _pad_to(row(router_b[l]), ne_pad, 1)
    h1, logits = _out_ln(y_rw, y_fx, h0, wo[:rw_w], wo[rw_w:], row(ln1_w[l]), row(ln1_b[l]),
                         rw_pad, rb_pad, alpha)

    gates, pos, row_tok, block_exp, n_used = _route(logits, n_experts, MOE_TILE)
    xs = h1[row_tok]
    w1g, w1l = _w1_split(exp_w1[l])
    b1 = exp_b1[l]
    b1g = b1[:, None, 0::2]
    b1l = b1[:, None, 1::2]
    eo_rows = _moe_ffn(block_exp, n_used, xs, w1g, w1l, b1g, b1l, exp_w2[l].astype(BF16), exp_b2[l][:, None, :])
    eo = eo_rows[pos.T]
    h = _combine_ln(h1, eo, gates, row(ln2_w[l]), row(ln2_b[l]), alpha)
    return h.reshape(B, S, D)
```

```python
import functools

import jax
import jax.numpy as jnp
from jax import lax
from jax.experimental import pallas as pl
from jax.experimental.pallas import tpu as pltpu

F32 = jnp.float32
BF16 = jnp.bfloat16
HIGHEST = lax.Precision.HIGHEST

HEAD_DIM = 64
PAIR = 2 * HEAD_DIM
WKV_CHUNK = 64
RW_GN_EPS = 64e-5
QK_EPS = 1e-6
LN_EPS = 1e-5
TOP_K = 4
SWIGLU_ALPHA = 1.702
SWIGLU_LIMIT = 7.0
NEG_BIG = -1e30
LOG2E = 1.4426950408889634

ROW_TILE = 256
ATTN_TQ = 512
ATTN_TK = 512
MOE_TILE = 256
WKV_BATCH = 4
VMEM_LIMIT = 48 * 1024 * 1024


def _cparams(sem):
    return pltpu.CompilerParams(dimension_semantics=sem, vmem_limit_bytes=VMEM_LIMIT)


def _dot(a, b):
    return jnp.dot(a.astype(BF16), b.astype(BF16), preferred_element_type=F32)


def _dot_t(a, b):
    return lax.dot_general(a.astype(BF16), b.astype(BF16), (((1,), (1,)), ((), ())),
                           preferred_element_type=F32)


def _segsum(x, seg):
    hi = x.astype(BF16)
    lo = (x - hi.astype(F32)).astype(BF16)
    return (jnp.dot(hi, seg, preferred_element_type=F32) + jnp.dot(lo, seg, preferred_element_type=F32))


def _sigmoid(x):
    return 1.0 / (1.0 + jnp.exp(-x))


def _softplus(x):
    return jnp.maximum(x, 0.0) + jnp.log(1.0 + jnp.exp(-jnp.abs(x)))


def _layer_norm(x, w, b):
    mu = jnp.mean(x, axis=-1, keepdims=True)
    xc = x - mu
    var = jnp.mean(xc * xc, axis=-1, keepdims=True)
    return xc * lax.rsqrt(var + LN_EPS) * w + b


def _ln_proj_kernel(x_ref, lnw_ref, lnb_ref, w_ref, wfz_ref, h_ref, prw_ref, pfx_ref, fzt_ref, *, n_rw):
    h = _layer_norm(x_ref[...], lnw_ref[...], lnb_ref[...])
    h_ref[...] = h
    p = jnp.dot(h.astype(BF16), w_ref[...], preferred_element_type=F32)
    prw_ref[...] = p[:, :n_rw]
    pfx_ref[...] = p[:, n_rw:]
    fzt_ref[...] = lax.dot_general(wfz_ref[...], h, (((1,), (1,)), ((), ())),
                                   precision=HIGHEST, preferred_element_type=F32)


def _ln_proj(x2, lnw, lnb, w_main, wfz_t, n_rw):
    T, D = x2.shape
    n_all = w_main.shape[1]
    n_fx = n_all - n_rw
    nh = wfz_t.shape[0]
    tm = ROW_TILE
    return pl.pallas_call(
        functools.partial(_ln_proj_kernel, n_rw=n_rw),
        grid=(T // tm,),
        in_specs=[
            pl.BlockSpec((tm, D), lambda i: (i, 0)),
            pl.BlockSpec((1, D), lambda i: (0, 0)),
            pl.BlockSpec((1, D), lambda i: (0, 0)),
            pl.BlockSpec((D, n_all), lambda i: (0, 0)),
            pl.BlockSpec((nh, D), lambda i: (0, 0)),
        ],
        out_specs=[
            pl.BlockSpec((tm, D), lambda i: (i, 0)),
            pl.BlockSpec((tm, n_rw), lambda i: (i, 0)),
            pl.BlockSpec((tm, n_fx), lambda i: (i, 0)),
            pl.BlockSpec((nh, tm), lambda i: (0, i)),
        ],
        out_shape=[
            jax.ShapeDtypeStruct((T, D), F32),
            jax.ShapeDtypeStruct((T, n_rw), F32),
            jax.ShapeDtypeStruct((T, n_fx), F32),
            jax.ShapeDtypeStruct((nh, T), F32),
        ],
        compiler_params=_cparams(("parallel",)),
        name="ln_proj",
    )(x2, lnw, lnb, w_main, wfz_t)


def _rw_prep_kernel(p_ref, mu_ref, w0_ref, w2_ref, a0_ref, a2_ref, g2_ref, kkw_ref, ka_ref, rk_ref, seg_ref,
                    r_ref, lw_ref, k_ref, v_ref, kk_ref, alr_ref, g_ref, bonus_ref, carry_ref, *, width):
    @pl.when(pl.program_id(1) == 0)
    def _():
        carry_ref[...] = jnp.zeros_like(carry_ref)

    p = p_ref[...]
    tm = p.shape[0]
    prev = pltpu.roll(p, 1, axis=0)
    first_row = lax.broadcasted_iota(jnp.int32, p.shape, 0) == 0
    prev = jnp.where(first_row, carry_ref[...], prev)
    carry_ref[...] = p[tm - 1:tm, :]
    ps = p + mu_ref[...] * (prev - p)

    r = ps[:, 0:width]
    k = ps[:, width:2 * width]
    v = ps[:, 2 * width:3 * width]
    lora = ps[:, 3 * width:]
    seg = seg_ref[...]

    wl = w0_ref[...] + _dot(jnp.tanh(lora), w2_ref[...])
    w_raw = -_softplus(-wl) - 0.5
    lw_ref[...] = -jnp.exp(w_raw)
    alr = _sigmoid(a0_ref[...] + _dot(lora, a2_ref[...]))
    g_ref[...] = _dot(_sigmoid(lora), g2_ref[...])
    kkp = k * kkw_ref[...]
    nrm = jnp.sqrt(_segsum(kkp * kkp, seg))
    kk_ref[...] = kkp / jnp.maximum(nrm, 1e-12)
    k2 = k * (1.0 + (alr - 1.0) * ka_ref[...])
    bonus_ref[...] = _segsum(r * k2 * rk_ref[...], seg) * v
    r_ref[...] = r
    k_ref[...] = k2
    v_ref[...] = v
    alr_ref[...] = alr


def _rw_prep(p_rw, B, S, width, mu, w0, w2p, a0, a2p, g2p, kkw, ka, rk, seg):
    T, n_rw = p_rw.shape
    tm = ROW_TILE
    ns = S // tm
    n_lora = n_rw - 3 * width
    row = lambda b, s: (b * ns + s, 0)
    fixed = lambda b, s: (0, 0)
    vec = pl.BlockSpec((1, width), fixed)
    out = pl.BlockSpec((tm, width), row)
    return pl.pallas_call(
        functools.partial(_rw_prep_kernel, width=width),
        grid=(B, ns),
        in_specs=[
            pl.BlockSpec((tm, n_rw), row),
            pl.BlockSpec((1, n_rw), fixed),
            vec, pl.BlockSpec((n_lora, width), fixed),
            vec, pl.BlockSpec((n_lora, width), fixed),
            pl.BlockSpec((n_lora, width), fixed),
            vec, vec, vec,
            pl.BlockSpec((width, width), fixed),
        ],
        out_specs=[out] * 8,
        out_shape=[jax.ShapeDtypeStruct((T, width), F32)] * 8,
        scratch_shapes=[pltpu.VMEM((1, n_rw), F32)],
        compiler_params=_cparams(("parallel", "arbitrary")),
        name="rw_prep",
    )(p_rw, mu, w0, w2p, a0, a2p, g2p, kkw, ka, rk, seg)


def _stack_heads(x):
    lane = lax.broadcasted_iota(jnp.int32, x.shape, 1)
    return jnp.concatenate([jnp.where(lane < HEAD_DIM, x, 0.0), jnp.where(lane >= HEAD_DIM, x, 0.0)], axis=0)


def _wkv_kernel(r_ref, lw_ref, k_ref, v_ref, kk_ref, alr_ref, g_ref, bonus_ref, gnw_ref, gnb_ref, tri_ref,
                segm_ref, y_ref, state_ref):
    C = WKV_CHUNK
    nb, _, width = lw_ref.shape
    npair = width // PAIR

    @pl.when(pl.program_id(1) == 0)
    def _():
        state_ref[...] = jnp.zeros_like(state_ref)

    ri = lax.broadcasted_iota(jnp.int32, (2 * C, 2 * C), 0)
    ci = lax.broadcasted_iota(jnp.int32, (2 * C, 2 * C), 1)
    same = (ri // C) == (ci // C)
    strict = same & ((ci % C) < (ri % C))
    incl = same & ((ci % C) <= (ri % C))
    eye = (ri == ci).astype(F32)

    lhs, rhs, a2s, r2s, v2s, bhts, bkts, ptots = [], [], [], [], [], [], [], []
    for bi in range(nb):
        lw = lw_ref[bi]
        cum = jnp.dot(tri_ref[...], lw, precision=HIGHEST, preferred_element_type=F32)
        total = cum[C - 1:C, :]
        p_inv = jnp.exp(-cum)
        p_rem = jnp.exp(total - cum)
        p_tot = jnp.exp(total)
        kk = kk_ref[bi]
        k2 = k_ref[bi]
        b = kk * alr_ref[bi]
        a_t = -kk * jnp.exp(cum - lw)
        r_t = r_ref[bi] * jnp.exp(cum)
        b_t = b * p_inv
        k_t = k2 * p_inv
        b_h = b * p_rem
        k_h = k2 * p_rem
        v = v_ref[bi]
        for p in range(npair):
            sl = slice(p * PAIR, (p + 1) * PAIR)
            a2, r2, b2, kt2 = (_stack_heads(t[:, sl]) for t in (a_t, r_t, b_t, k_t))
            bh2, kh2, v2 = (_stack_heads(t[:, sl]) for t in (b_h, k_h, v))
            lhs.append(jnp.concatenate([a2, r2], axis=0))
            rhs.append(jnp.concatenate([b2, kt2], axis=0))
            a2s.append(a2)
            r2s.append(r2)
            v2s.append(v2)
            bhts.append(bh2.T)
            bkts.append(jnp.concatenate([bh2.T, kh2.T], axis=1))
            ptots.append(p_tot[:, sl])

    chains = range(nb * npair)
    m = [_dot_t(lhs[c], rhs[c]) for c in chains]
    n_ab = [jnp.where(strict, m[c][:2 * C, :2 * C], 0.0) for c in chains]
    m_ak = [jnp.where(strict, m[c][:2 * C, 2 * C:], 0.0) for c in chains]
    m_rb = [jnp.where(incl, m[c][2 * C:, :2 * C], 0.0) for c in chains]
    m_rk = [jnp.where(incl, m[c][2 * C:, 2 * C:], 0.0) for c in chains]
    mv = [_dot(m_ak[c], v2s[c]) for c in chains]
    mrkv = [_dot(m_rk[c], v2s[c]) for c in chains]
    inv = [eye + n_ab[c] for c in chains]
    pw = n_ab
    for _ in range(C.bit_length() - 2):
        pw = [_dot(pw[c], pw[c]) for c in chains]
        inv = [inv[c] + _dot(inv[c], pw[c]) for c in chains]
    wu = [_dot(inv[c], jnp.concatenate([a2s[c], mv[c]], axis=1)) for c in chains]
    qy = [_dot(m_rb[c], wu[c]) + jnp.concatenate([r2s[c], mrkv[c]], axis=1) for c in chains]
    g_t = [_dot(bhts[c], wu[c][:, :PAIR]) + eye * ptots[c] for c in chains]
    h_t = [_dot(bkts[c], jnp.concatenate([wu[c][:, PAIR:], v2s[c]], axis=0)) for c in chains]
    s0 = [state_ref[c] for c in chains]
    y2 = [_dot(qy[c][:, :PAIR], s0[c]) + qy[c][:, PAIR:] for c in chains]
    for c in chains:
        state_ref[c] = _dot(g_t[c], s0[c]) + h_t[c]

    segm = segm_ref[...]
    for bi in range(nb):
        y = jnp.concatenate([y2[bi * npair + p][:C] + y2[bi * npair + p][C:] for p in range(npair)], axis=1)
        mean = _segsum(y, segm)
        yc = y - mean
        var = _segsum(yc * yc, segm)
        yn = yc * lax.rsqrt(var + RW_GN_EPS) * gnw_ref[...] + gnb_ref[...]
        y_ref[bi] = ((yn + bonus_ref[bi]) * g_ref[bi]).astype(y_ref.dtype)


def _wkv(r, lw, k2, v, kk, alr, g, bonus, gnw, gnb, tri, segm, B, S):
    T, width = r.shape
    C = WKV_CHUNK
    nb = WKV_BATCH
    nc = S // C
    fixed = lambda b, c: (0, 0)
    blk = pl.BlockSpec((nb, C, width), lambda b, c: (b, c, 0))
    vec = pl.BlockSpec((1, width), fixed)
    ins = [t.reshape(B, S, width) for t in (r, lw, k2, v, kk, alr, g, bonus)]
    y = pl.pallas_call(
        _wkv_kernel,
        grid=(B // nb, nc),
        in_specs=[blk] * 8 + [vec, vec, pl.BlockSpec((C, C), fixed), pl.BlockSpec((width, width), fixed)],
        out_specs=blk,
        out_shape=jax.ShapeDtypeStruct((B, S, width), BF16),
        scratch_shapes=[pltpu.VMEM((nb * width // PAIR, PAIR, PAIR), F32)],
        compiler_params=_cparams(("parallel", "arbitrary")),
        name="wkv",
    )(*ins, gnw, gnb, tri, segm)
    return y.reshape(T, width)


def _split3(x):
    hi = x.astype(BF16).astype(F32)
    mid = (x - hi).astype(BF16).astype(F32)
    return hi, mid, x - hi - mid


def _fx_prep_kernel(q_ref, k_ref, v_ref, fzt_ref, bf_ref, qw_ref, kw_ref, segm_ref, tril_ref, spread_ref,
                    place_ref, oneq_ref, onek_ref, onev_ref, qa_ref, ka_ref, va_ref, carry_ref):
    @pl.when(pl.program_id(1) == 0)
    def _():
        carry_ref[...] = jnp.zeros_like(carry_ref)

    segm = segm_ref[...]
    spread = spread_ref[...]
    q = q_ref[...]
    k = k_ref[...]
    tm = q.shape[0]
    qn = q * lax.rsqrt(_segsum(q * q, segm) + QK_EPS) * (qw_ref[...] * (HEAD_DIM ** -0.5 * LOG2E))
    kn = k * lax.rsqrt(_segsum(k * k, segm) + QK_EPS) * kw_ref[...]

    lf = -_softplus(-(fzt_ref[...] + bf_ref[...]))
    c = lax.dot_general(tril_ref[...], lf, (((1,), (1,)), ((), ())), precision=HIGHEST,
                        preferred_element_type=F32) + carry_ref[...]
    carry_ref[...] = c[tm - 1:tm, :]
    placed = jnp.dot(c * LOG2E, place_ref[...], precision=HIGHEST, preferred_element_type=F32)
    hi, mid, lo = _split3(placed)
    lane = lax.broadcasted_iota(jnp.int32, placed.shape, 1) % PAIR
    c_q = jnp.where(lane == HEAD_DIM + 3, hi, jnp.where(lane == HEAD_DIM + 4, mid,
                                                        jnp.where(lane == HEAD_DIM + 5, lo, 0.0)))
    c_k = jnp.where(lane == HEAD_DIM, hi, jnp.where(lane == HEAD_DIM + 1, mid,
                                                    jnp.where(lane == HEAD_DIM + 2, lo, 0.0)))
    qa_ref[...] = (_dot(qn, spread) + c_q + oneq_ref[...]).astype(qa_ref.dtype)
    ka_ref[...] = (_dot(kn, spread) - c_k + onek_ref[...]).astype(ka_ref.dtype)
    va_ref[...] = (_dot(v_ref[...], spread) + onev_ref[...]).astype(va_ref.dtype)


def _fx_prep(p_fx, fzt, bf, qw, kw, segm, tril, B, S, width):
    T = p_fx.shape[0]
    nh = fzt.shape[0]
    wide = nh * PAIR
    tm = ROW_TILE
    ns = S // tm
    fixed = lambda b, s: (0, 0)
    col = lambda j: pl.BlockSpec((tm, width), lambda b, s: (b * ns + s, j))
    out = pl.BlockSpec((tm, wide), lambda b, s: (b * ns + s, 0))
    vec = pl.BlockSpec((1, width), fixed)
    wvec = pl.BlockSpec((1, wide), fixed)

    src = jnp.arange(width)
    spread = ((src // HEAD_DIM * PAIR + src % HEAD_DIM)[:, None] == jnp.arange(wide)[None, :]).astype(BF16)
    lane = jnp.arange(wide) % PAIR
    head = jnp.arange(wide) // PAIR
    place = ((head[None, :] == jnp.arange(nh)[:, None]) & (lane >= HEAD_DIM) & (lane < HEAD_DIM + 6)).astype(F32)
    one_q = ((lane >= HEAD_DIM) & (lane < HEAD_DIM + 3)).astype(F32).reshape(1, wide)
    one_k = ((lane >= HEAD_DIM + 3) & (lane < HEAD_DIM + 6)).astype(F32).reshape(1, wide)
    one_v = (lane >= HEAD_DIM).astype(F32).reshape(1, wide)
    return pl.pallas_call(
        _fx_prep_kernel,
        grid=(B, ns),
        in_specs=[
            col(0), col(1), col(2),
            pl.BlockSpec((nh, tm), lambda b, s: (0, b * ns + s)),
            pl.BlockSpec((nh, 1), fixed),
            vec, vec,
            pl.BlockSpec((width, width), fixed),
            pl.BlockSpec((tm, tm), fixed),
            pl.BlockSpec((width, wide), fixed),
            pl.BlockSpec((nh, wide), fixed),
            wvec, wvec, wvec,
        ],
        out_specs=[out, out, out],
        out_shape=[jax.ShapeDtypeStruct((T, wide), BF16)] * 3,
        scratch_shapes=[pltpu.VMEM((1, nh), F32)],
        compiler_params=_cparams(("parallel", "arbitrary")),
        name="fx_prep",
    )(p_fx, p_fx, p_fx, fzt, bf, qw, kw, segm, tril, spread, place, one_q, one_k, one_v)


def _fox_attn_kernel(q_ref, k_ref, v_ref, og_ref, o_ref, m_ref, acc_ref):
    i = pl.program_id(2)
    j = pl.program_id(3)
    tq = q_ref.shape[0]
    tk = k_ref.shape[0]
    heads = range(2)
    grp = lambda ref, h: ref[:, h * PAIR:(h + 1) * PAIR]

    @pl.when(j == 0)
    def _():
        m_ref[...] = jnp.full_like(m_ref, NEG_BIG)
        acc_ref[...] = jnp.zeros_like(acc_ref)

    def step(masked):
        s = [lax.dot_general(grp(q_ref, h), grp(k_ref, h), (((1,), (1,)), ((), ())), preferred_element_type=F32)
             for h in heads]
        if masked:
            row = lax.broadcasted_iota(jnp.int32, (tq, tk), 0)
            col = lax.broadcasted_iota(jnp.int32, (tq, tk), 1)
            s = [jnp.where(col <= row, s[h], NEG_BIG) for h in heads]
        m_old = [m_ref[h] for h in heads]
        m_new = [jnp.maximum(m_old[h], jnp.max(s[h], axis=1, keepdims=True)) for h in heads]
        alpha = [jnp.exp2(m_old[h] - m_new[h]) for h in heads]
        pr = [jnp.exp2(s[h] - pltpu.repeat(m_new[h], tk // PAIR, axis=1)).astype(BF16) for h in heads]
        pv = [jnp.dot(pr[h], grp(v_ref, h), preferred_element_type=F32) for h in heads]
        for h in heads:
            acc_ref[h] = alpha[h] * acc_ref[h] + pv[h]
            m_ref[h] = m_new[h]

    @pl.when(j < i)
    def _():
        step(False)

    @pl.when(j == i)
    def _():
        step(True)
        a0 = acc_ref[0]
        a1 = acc_ref[1]
        lane = lax.broadcasted_iota(jnp.int32, a0.shape, 1)
        o0 = a0 * pltpu.roll(1.0 / a0, HEAD_DIM, axis=1)
        o1 = pltpu.roll(a1, HEAD_DIM, axis=1) * (1.0 / a1)
        o = jnp.where(lane < HEAD_DIM, o0, o1)
        o_ref[...] = (o * _sigmoid(og_ref[...])).astype(o_ref.dtype)


def _fox_attn(qa, ka, va, p_fx, B, S, width):
    T = qa.shape[0]
    tq, tk = ATTN_TQ, ATTN_TK
    assert tq == tk
    nq = S // tq
    npair = width // PAIR
    og_col0 = 3 * width // PAIR
    kv = pl.BlockSpec((tk, 2 * PAIR), lambda b, p, i, j: (b * nq + jnp.minimum(i, j), p))
    return pl.pallas_call(
        _fox_attn_kernel,
        grid=(B, npair, nq, nq),
        in_specs=[
            pl.BlockSpec((tq, 2 * PAIR), lambda b, p, i, j: (b * nq + i, p)), kv, kv,
            pl.BlockSpec((tq, PAIR), lambda b, p, i, j: (b * nq + i, og_col0 + p)),
        ],
        out_specs=pl.BlockSpec((tq, PAIR), lambda b, p, i, j: (b * nq + i, p)),
        out_shape=jax.ShapeDtypeStruct((T, width), BF16),
        scratch_shapes=[pltpu.VMEM((2, tq, PAIR), F32), pltpu.VMEM((2, tq, PAIR), F32)],
        compiler_params=_cparams(("parallel", "parallel", "parallel", "arbitrary")),
        name="fox_attn",
    )(qa, ka, va, p_fx)


def _out_ln_kernel(yrw_ref, yfx_ref, h_ref, wo1_ref, wo2_ref, lnw_ref, lnb_ref, rw_ref, rb_ref,
                   h1_ref, logit_ref, *, alpha):
    mix = (jnp.dot(yrw_ref[...], wo1_ref[...], preferred_element_type=F32)
           + jnp.dot(yfx_ref[...], wo2_ref[...], preferred_element_type=F32))
    h1 = _layer_norm(alpha * h_ref[...] + mix, lnw_ref[...], lnb_ref[...])
    h1_ref[...] = h1
    logit_ref[...] = jnp.dot(h1, rw_ref[...], precision=HIGHEST, preferred_element_type=F32) + rb_ref[...]


def _out_ln(y_rw, y_fx, h0, wo1, wo2, lnw, lnb, rw, rb, alpha):
    T, D = h0.shape
    width = y_rw.shape[1]
    ne = rw.shape[1]
    tm = ROW_TILE
    row = lambda i: (i, 0)
    fixed = lambda i: (0, 0)
    return pl.pallas_call(
        functools.partial(_out_ln_kernel, alpha=alpha),
        grid=(T // tm,),
        in_specs=[
            pl.BlockSpec((tm, width), row), pl.BlockSpec((tm, width), row), pl.BlockSpec((tm, D), row),
            pl.BlockSpec((width, D), fixed), pl.BlockSpec((width, D), fixed),
            pl.BlockSpec((1, D), fixed), pl.BlockSpec((1, D), fixed),
            pl.BlockSpec((D, ne), fixed), pl.BlockSpec((1, ne), fixed),
        ],
        out_specs=[pl.BlockSpec((tm, D), row), pl.BlockSpec((tm, ne), row)],
        out_shape=[jax.ShapeDtypeStruct((T, D), F32), jax.ShapeDtypeStruct((T, ne), F32)],
        compiler_params=_cparams(("parallel",)),
        name="out_ln",
    )(y_rw, y_fx, h0, wo1, wo2, lnw, lnb, rw, rb)


DEINT_COLS = 256


def _w1_split_kernel(w_ref, perm_ref, g_ref, l_ref):
    half = DEINT_COLS // 2
    for c in range(w_ref.shape[2] // DEINT_COLS):
        blk = w_ref[0, :, c * DEINT_COLS:(c + 1) * DEINT_COLS].astype(BF16)
        out = jnp.dot(blk, perm_ref[...], preferred_element_type=F32).astype(BF16)
        g_ref[0, :, c * half:(c + 1) * half] = out[:, :half]
        l_ref[0, :, c * half:(c + 1) * half] = out[:, half:]


def _w1_split(w1):
    E, D, F2 = w1.shape
    tr = 512
    half = DEINT_COLS // 2
    src = jnp.arange(DEINT_COLS)
    dst = jnp.where(src % 2 == 0, src // 2, half + src // 2)
    perm = (dst[:, None] == jnp.arange(DEINT_COLS)[None, :]).astype(BF16)
    out = pl.BlockSpec((1, tr, F2 // 2), lambda e, r: (e, r, 0))
    return pl.pallas_call(
        _w1_split_kernel,
        grid=(E, D // tr),
        in_specs=[pl.BlockSpec((1, tr, F2), lambda e, r: (e, r, 0)),
                  pl.BlockSpec((DEINT_COLS, DEINT_COLS), lambda e, r: (0, 0))],
        out_specs=[out, out],
        out_shape=[jax.ShapeDtypeStruct((E, D, F2 // 2), BF16)] * 2,
        compiler_params=_cparams(("parallel", "parallel")),
        name="w1_split",
    )(w1, perm)


def _moe_kernel(bexp_ref, nused_ref, x_ref, w1g_ref, w1l_ref, b1g_ref, b1l_ref, w2_ref, b2_ref, o_ref):
    i = pl.program_id(0)

    @pl.when(i < nused_ref[0])
    def _():
        x = x_ref[...].astype(BF16)
        x_glu = jnp.minimum(jnp.dot(x, w1g_ref[0], preferred_element_type=F32) + b1g_ref[0], SWIGLU_LIMIT)
        x_lin = jnp.clip(jnp.dot(x, w1l_ref[0], preferred_element_type=F32) + b1l_ref[0],
                         -SWIGLU_LIMIT, SWIGLU_LIMIT)
        act = x_glu * _sigmoid(SWIGLU_ALPHA * x_glu) * (x_lin + 1.0)
        o_ref[...] = jnp.dot(act.astype(BF16), w2_ref[0], preferred_element_type=F32) + b2_ref[0]

    @pl.when(i >= nused_ref[0])
    def _():
        o_ref[...] = jnp.zeros_like(o_ref)


def _moe_ffn(block_exp, n_used, xs, w1g, w1l, b1g, b1l, w2, b2):
    n_rows, D = xs.shape
    F = w1g.shape[2]
    tm = MOE_TILE
    n_blocks = n_rows // tm
    live = lambda i, be, nu: jnp.minimum(i, nu[0] - 1)
    wspec = lambda shape: pl.BlockSpec((1,) + shape, lambda i, be, nu: (be[live(i, be, nu)], 0, 0))
    grid_spec = pltpu.PrefetchScalarGridSpec(
        num_scalar_prefetch=2,
        grid=(n_blocks,),
        in_specs=[
            pl.BlockSpec((tm, D), lambda i, be, nu: (live(i, be, nu), 0)),
            wspec((D, F)), wspec((D, F)), wspec((1, F)), wspec((1, F)), wspec((F, D)), wspec((1, D)),
        ],
        out_specs=pl.BlockSpec((tm, D), lambda i, be, nu: (i, 0)),
    )
    return pl.pallas_call(
        _moe_kernel,
        grid_spec=grid_spec,
        out_shape=jax.ShapeDtypeStruct((n_rows, D), F32),
        compiler_params=_cparams(("arbitrary",)),
        name="moe_ffn",
    )(block_exp, n_used, xs, w1g, w1l, b1g, b1l, w2, b2)


def _combine_ln_kernel(h_ref, eo_ref, gate_ref, lnw_ref, lnb_ref, o_ref, *, alpha):
    gates = gate_ref[...]
    ffn = eo_ref[0] * gates[:, 0:1]
    for j in range(1, TOP_K):
        ffn = ffn + eo_ref[j] * gates[:, j:j + 1]
    o_ref[...] = _layer_norm(alpha * h_ref[...] + ffn, lnw_ref[...], lnb_ref[...])


def _combine_ln(h1, eo, gates, lnw, lnb, alpha):
    T, D = h1.shape
    tm = ROW_TILE
    return pl.pallas_call(
        functools.partial(_combine_ln_kernel, alpha=alpha),
        grid=(T // tm,),
        in_specs=[
            pl.BlockSpec((tm, D), lambda i: (i, 0)),
            pl.BlockSpec((TOP_K, tm, D), lambda i: (0, i, 0)),
            pl.BlockSpec((tm, TOP_K), lambda i: (i, 0)),
            pl.BlockSpec((1, D), lambda i: (0, 0)),
            pl.BlockSpec((1, D), lambda i: (0, 0)),
        ],
        out_specs=pl.BlockSpec((tm, D), lambda i: (i, 0)),
        out_shape=jax.ShapeDtypeStruct((T, D), F32),
        compiler_params=_cparams(("parallel",)),
        name="combine_ln",
    )(h1, eo, gates, lnw, lnb)


def _pad_to(x, n, axis):
    pad = [(0, 0)] * x.ndim
    pad[axis] = (0, n - x.shape[axis])
    return jnp.pad(x, pad)


def _block_diag_ones(width, value=1.0):
    idx = jnp.arange(width) // HEAD_DIM
    return jnp.where(idx[:, None] == idx[None, :], value, 0.0).astype(BF16)


def _route(logits, n_experts, tile):
    T = logits.shape[0]
    top_val, top_idx = lax.top_k(logits[:, :n_experts], TOP_K)
    gates = jax.nn.softmax(top_val, axis=-1)
    e_flat = top_idx.reshape(-1).astype(jnp.int32)
    onehot = (e_flat[:, None] == jnp.arange(n_experts, dtype=jnp.int32)[None, :]).astype(jnp.int32)
    csum = jnp.cumsum(onehot, axis=0)
    rank = jnp.take_along_axis(csum, e_flat[:, None], axis=1)[:, 0] - 1
    counts = csum[-1]
    padded = (counts + tile - 1) // tile * tile
    pends = jnp.cumsum(padded)
    pstarts = pends - padded
    dest = (pstarts[e_flat] + rank).astype(jnp.int32)
    n_assign = T * TOP_K
    n_rows = n_assign + n_experts * tile
    n_blocks = n_rows // tile
    block_start = jnp.arange(n_blocks, dtype=jnp.int32) * tile
    block_exp = jnp.minimum(jnp.sum((pends[None, :] <= block_start[:, None]).astype(jnp.int32), axis=1),
                            n_experts - 1).astype(jnp.int32)
    n_used = (pends[-1] // tile).astype(jnp.int32).reshape(1)
    assert n_experts * n_assign < 2 ** 31
    order = lax.sort(e_flat * n_assign + jnp.arange(n_assign, dtype=jnp.int32)) % n_assign
    starts = jnp.cumsum(counts) - counts
    row_exp = jnp.repeat(block_exp, tile)
    slot = jnp.arange(n_rows, dtype=jnp.int32) - pstarts[row_exp]
    src = jnp.clip(starts[row_exp] + slot, 0, n_assign - 1)
    row_tok = jnp.where(slot < counts[row_exp], order[src] // TOP_K, 0).astype(jnp.int32)
    return gates, dest.reshape(T, TOP_K), row_tok, block_exp, n_used


def kernel(x, ln_in_w, ln_in_b, w_in, rw_mu, rw_w0, rw_w2, rw_a0, rw_a2, rw_g2, rw_k_k, rw_k_a, rw_r_k,
           rw_gn_w, rw_gn_b, fx_b_f, fx_q_norm, fx_k_norm, w_o, ln1_w, ln1_b, router_w, router_b,
           exp_w1, exp_b1, exp_w2, exp_b2, ln2_w, ln2_b):
    B, S, D = x.shape
    T = B * S
    depth = w_in.shape[0]
    alpha = (2 * depth) ** 0.25
    rw_w = rw_w0.shape[1]
    fx_heads = fx_b_f.shape[1]
    fx_w = fx_heads * HEAD_DIM
    d_lora, a_lora, g_lora = rw_w2.shape[1], rw_a2.shape[1], rw_g2.shape[1]
    n_lora = d_lora + a_lora + g_lora
    lora_pad = -(-n_lora // 128) * 128
    rw_cols = 3 * rw_w + n_lora
    n_rw = 3 * rw_w + lora_pad
    n_experts = router_w.shape[2]
    ne_pad = -(-n_experts // 128) * 128
    row = lambda a: a.reshape(1, -1)

    seg_rw = _block_diag_ones(rw_w)
    segm_rw = _block_diag_ones(rw_w, 1.0 / HEAD_DIM)
    segm_fx = _block_diag_ones(fx_w, 1.0 / HEAD_DIM)
    tidx = jnp.arange(ROW_TILE)
    tril = (tidx[:, None] >= tidx[None, :]).astype(F32)
    cidx = jnp.arange(WKV_CHUNK)
    tri_c = (cidx[:, None] >= cidx[None, :]).astype(F32)

    assert depth == 1, "single-layer block"
    l = 0
    x2 = x.reshape(T, D)
    w_l = w_in[l]
    w_main = jnp.concatenate(
        [_pad_to(w_l[:, :rw_cols], n_rw, 1), w_l[:, rw_cols:rw_cols + 4 * fx_w]], axis=1).astype(BF16)
    wfz_t = w_l[:, rw_cols + 4 * fx_w:].T
    h0, p_rw, p_fx, fzt = _ln_proj(x2, row(ln_in_w), row(ln_in_b), w_main, wfz_t, n_rw)

    mu = _pad_to(row(rw_mu[l]), n_rw, 1)
    w2p = _pad_to(rw_w2[l], lora_pad, 0).astype(BF16)
    a2p = _pad_to(jnp.pad(rw_a2[l], ((d_lora, 0), (0, 0))), lora_pad, 0).astype(BF16)
    g2p = _pad_to(jnp.pad(rw_g2[l], ((d_lora + a_lora, 0), (0, 0))), lora_pad, 0).astype(BF16)
    r, lw, k2, v, kk, alr, g, bonus = _rw_prep(
        p_rw, B, S, rw_w, mu, row(rw_w0[l]), w2p, row(rw_a0[l]), a2p, g2p,
        row(rw_k_k[l]), row(rw_k_a[l]), row(rw_r_k[l]), seg_rw)
    y_rw = _wkv(r, lw, k2, v, kk, alr, g, bonus, row(rw_gn_w[l]), row(rw_gn_b[l]), tri_c, segm_rw, B, S)

    qw = row(jnp.tile(fx_q_norm[l], fx_heads))
    kw = row(jnp.tile(fx_k_norm[l], fx_heads))
    qa, ka, va = _fx_prep(p_fx, fzt, fx_b_f[l].reshape(-1, 1), qw, kw, segm_fx, tril, B, S, fx_w)
    y_fx = _fox_attn(qa, ka, va, p_fx, B, S, fx_w)

    wo = w_o[l].astype(BF16)
    rw_pad = _pad_to(router_w[l], ne_pad, 1)
    rb_pad = _pad_to(row(router_b[l]), ne_pad, 1)
    h1, logits = _out_ln(y_rw, y_fx, h0, wo[:rw_w], wo[rw_w:], row(ln1_w[l]), row(ln1_b[l]),
                         rw_pad, rb_pad, alpha)

    gates, pos, row_tok, block_exp, n_used = _route(logits, n_experts, MOE_TILE)
    xs = h1[row_tok]
    w1g, w1l = _w1_split(exp_w1[l])
    b1 = exp_b1[l]
    b1g = b1[:, None, 0::2]
    b1l = b1[:, None, 1::2]
    eo_rows = _moe_ffn(block_exp, n_used, xs, w1g, w1l, b1g, b1l, exp_w2[l].astype(BF16), exp_b2[l][:, None, :])
    eo = eo_rows[pos.T]
    h = _combine_ln(h1, eo, gates, row(ln2_w[l]), row(ln2_b[l]), alpha)
    return h.reshape(B, S, D)
```

```python
import functools

import jax
import jax.numpy as jnp
from jax import lax
from jax.experimental import pallas as pl
from jax.experimental.pallas import tpu as pltpu

F32 = jnp.float32
BF16 = jnp.bfloat16
HIGHEST = lax.Precision.HIGHEST

HEAD_DIM = 64
PAIR = 2 * HEAD_DIM
WKV_CHUNK = 64
RW_GN_EPS = 64e-5
QK_EPS = 1e-6
LN_EPS = 1e-5
TOP_K = 4
SWIGLU_ALPHA = 1.702
SWIGLU_LIMIT = 7.0
NEG_BIG = -1e30
LOG2E = 1.4426950408889634

ROW_TILE = 256
ATTN_TQ = 512
ATTN_TK = 512
MOE_TILE = 256
WKV_BATCH = 4
VMEM_LIMIT = 48 * 1024 * 1024


def _cparams(sem):
    return pltpu.CompilerParams(dimension_semantics=sem, vmem_limit_bytes=VMEM_LIMIT)


def _dot(a, b):
    return jnp.dot(a.astype(BF16), b.astype(BF16), preferred_element_type=F32)


def _dot_t(a, b):
    return lax.dot_general(a.astype(BF16), b.astype(BF16), (((1,), (1,)), ((), ())),
                           preferred_element_type=F32)


def _segsum(x, seg):
    hi = x.astype(BF16)
    lo = (x - hi.astype(F32)).astype(BF16)
    return (jnp.dot(hi, seg, preferred_element_type=F32) + jnp.dot(lo, seg, preferred_element_type=F32))


def _sigmoid(x):
    return 1.0 / (1.0 + jnp.exp(-x))


def _softplus(x):
    return jnp.maximum(x, 0.0) + jnp.log(1.0 + jnp.exp(-jnp.abs(x)))


def _layer_norm(x, w, b):
    mu = jnp.mean(x, axis=-1, keepdims=True)
    xc = x - mu
    var = jnp.mean(xc * xc, axis=-1, keepdims=True)
    return xc * lax.rsqrt(var + LN_EPS) * w + b


def _ln_proj_kernel(x_ref, lnw_ref, lnb_ref, w_ref, wfz_ref, h_ref, prw_ref, pfx_ref, fzt_ref, *, n_rw):
    h = _layer_norm(x_ref[...], lnw_ref[...], lnb_ref[...])
    h_ref[...] = h
    p = jnp.dot(h.astype(BF16), w_ref[...], preferred_element_type=F32)
    prw_ref[...] = p[:, :n_rw]
    pfx_ref[...] = p[:, n_rw:]
    fzt_ref[...] = lax.dot_general(wfz_ref[...], h, (((1,), (1,)), ((), ())),
                                   precision=HIGHEST, preferred_element_type=F32)


def _ln_proj(x2, lnw, lnb, w_main, wfz_t, n_rw):
    T, D = x2.shape
    n_all = w_main.shape[1]
    n_fx = n_all - n_rw
    nh = wfz_t.shape[0]
    tm = ROW_TILE
    return pl.pallas_call(
        functools.partial(_ln_proj_kernel, n_rw=n_rw),
        grid=(T // tm,),
        in_specs=[
            pl.BlockSpec((tm, D), lambda i: (i, 0)),
            pl.BlockSpec((1, D), lambda i: (0, 0)),
            pl.BlockSpec((1, D), lambda i: (0, 0)),
            pl.BlockSpec((D, n_all), lambda i: (0, 0)),
            pl.BlockSpec((nh, D), lambda i: (0, 0)),
        ],
        out_specs=[
            pl.BlockSpec((tm, D), lambda i: (i, 0)),
            pl.BlockSpec((tm, n_rw), lambda i: (i, 0)),
            pl.BlockSpec((tm, n_fx), lambda i: (i, 0)),
            pl.BlockSpec((nh, tm), lambda i: (0, i)),
        ],
        out_shape=[
            jax.ShapeDtypeStruct((T, D), F32),
            jax.ShapeDtypeStruct((T, n_rw), F32),
            jax.ShapeDtypeStruct((T, n_fx), F32),
            jax.ShapeDtypeStruct((nh, T), F32),
        ],
        compiler_params=_cparams(("parallel",)),
        name="ln_proj",
    )(x2, lnw, lnb, w_main, wfz_t)


def _rw_prep_kernel(p_ref, mu_ref, w0_ref, w2_ref, a0_ref, a2_ref, g2_ref, kkw_ref, ka_ref, rk_ref, seg_ref,
                    r_ref, lw_ref, k_ref, v_ref, kk_ref, alr_ref, g_ref, bonus_ref, carry_ref, *, width):
    @pl.when(pl.program_id(1) == 0)
    def _():
        carry_ref[...] = jnp.zeros_like(carry_ref)

    p = p_ref[...]
    tm = p.shape[0]
    prev = pltpu.roll(p, 1, axis=0)
    first_row = lax.broadcasted_iota(jnp.int32, p.shape, 0) == 0
    prev = jnp.where(first_row, carry_ref[...], prev)
    carry_ref[...] = p[tm - 1:tm, :]
    ps = p + mu_ref[...] * (prev - p)

    r = ps[:, 0:width]
    k = ps[:, width:2 * width]
    v = ps[:, 2 * width:3 * width]
    lora = ps[:, 3 * width:]
    seg = seg_ref[...]

    wl = w0_ref[...] + _dot(jnp.tanh(lora), w2_ref[...])
    w_raw = -_softplus(-wl) - 0.5
    lw_ref[...] = -jnp.exp(w_raw)
    alr = _sigmoid(a0_ref[...] + _dot(lora, a2_ref[...]))
    g_ref[...] = _dot(_sigmoid(lora), g2_ref[...])
    kkp = k * kkw_ref[...]
    nrm = jnp.sqrt(_segsum(kkp * kkp, seg))
    kk_ref[...] = kkp / jnp.maximum(nrm, 1e-12)
    k2 = k * (1.0 + (alr - 1.0) * ka_ref[...])
    bonus_ref[...] = _segsum(r * k2 * rk_ref[...], seg) * v
    r_ref[...] = r
    k_ref[...] = k2
    v_ref[...] = v
    alr_ref[...] = alr


def _rw_prep(p_rw, B, S, width, mu, w0, w2p, a0, a2p, g2p, kkw, ka, rk, seg):
    T, n_rw = p_rw.shape
    tm = ROW_TILE
    ns = S // tm
    n_lora = n_rw - 3 * width
    row = lambda b, s: (b * ns + s, 0)
    fixed = lambda b, s: (0, 0)
    vec = pl.BlockSpec((1, width), fixed)
    out = pl.BlockSpec((tm, width), row)
    return pl.pallas_call(
        functools.partial(_rw_prep_kernel, width=width),
        grid=(B, ns),
        in_specs=[
            pl.BlockSpec((tm, n_rw), row),
            pl.BlockSpec((1, n_rw), fixed),
            vec, pl.BlockSpec((n_lora, width), fixed),
            vec, pl.BlockSpec((n_lora, width), fixed),
            pl.BlockSpec((n_lora, width), fixed),
            vec, vec, vec,
            pl.BlockSpec((width, width), fixed),
        ],
        out_specs=[out] * 8,
        out_shape=[jax.ShapeDtypeStruct((T, width), F32)] * 8,
        scratch_shapes=[pltpu.VMEM((1, n_rw), F32)],
        compiler_params=_cparams(("parallel", "arbitrary")),
        name="rw_prep",
    )(p_rw, mu, w0, w2p, a0, a2p, g2p, kkw, ka, rk, seg)


def _stack_heads(x):
    lane = lax.broadcasted_iota(jnp.int32, x.shape, 1)
    return jnp.concatenate([jnp.where(lane < HEAD_DIM, x, 0.0), jnp.where(lane >= HEAD_DIM, x, 0.0)], axis=0)


def _wkv_kernel(r_ref, lw_ref, k_ref, v_ref, kk_ref, alr_ref, g_ref, bonus_ref, gnw_ref, gnb_ref, tri_ref,
                segm_ref, y_ref, state_ref):
    C = WKV_CHUNK
    nb, _, width = lw_ref.shape
    npair = width // PAIR

    @pl.when(pl.program_id(1) == 0)
    def _():
        state_ref[...] = jnp.zeros_like(state_ref)

    ri = lax.broadcasted_iota(jnp.int32, (2 * C, 2 * C), 0)
    ci = lax.broadcasted_iota(jnp.int32, (2 * C, 2 * C), 1)
    same = (ri // C) == (ci // C)
    strict = same & ((ci % C) < (ri % C))
    incl = same & ((ci % C) <= (ri % C))
    eye = (ri == ci).astype(F32)

    lhs, rhs, a2s, r2s, v2s, bhts, bkts, ptots = [], [], [], [], [], [], [], []
    for bi in range(nb):
        lw = lw_ref[bi]
        cum = jnp.dot(tri_ref[...], lw, precision=HIGHEST, preferred_element_type=F32)
        total = cum[C - 1:C, :]
        p_inv = jnp.exp(-cum)
        p_rem = jnp.exp(total - cum)
        p_tot = jnp.exp(total)
        kk = kk_ref[bi]
        k2 = k_ref[bi]
        b = kk * alr_ref[bi]
        a_t = -kk * jnp.exp(cum - lw)
        r_t = r_ref[bi] * jnp.exp(cum)
        b_t = b * p_inv
        k_t = k2 * p_inv
        b_h = b * p_rem
        k_h = k2 * p_rem
        v = v_ref[bi]
        for p in range(npair):
            sl = slice(p * PAIR, (p + 1) * PAIR)
            a2, r2, b2, kt2 = (_stack_heads(t[:, sl]) for t in (a_t, r_t, b_t, k_t))
            bh2, kh2, v2 = (_stack_heads(t[:, sl]) for t in (b_h, k_h, v))
            lhs.append(jnp.concatenate([a2, r2], axis=0))
            rhs.append(jnp.concatenate([b2, kt2], axis=0))
            a2s.append(a2)
            r2s.append(r2)
            v2s.append(v2)
            bhts.append(bh2.T)
            bkts.append(jnp.concatenate([bh2.T, kh2.T], axis=1))
            ptots.append(p_tot[:, sl])

    chains = range(nb * npair)
    m = [_dot_t(lhs[c], rhs[c]) for c in chains]
    n_ab = [jnp.where(strict, m[c][:2 * C, :2 * C], 0.0) for c in chains]
    m_ak = [jnp.where(strict, m[c][:2 * C, 2 * C:], 0.0) for c in chains]
    m_rb = [jnp.where(incl, m[c][2 * C:, :2 * C], 0.0) for c in chains]
    m_rk = [jnp.where(incl, m[c][2 * C:, 2 * C:], 0.0) for c in chains]
    mv = [_dot(m_ak[c], v2s[c]) for c in chains]
    mrkv = [_dot(m_rk[c], v2s[c]) for c in chains]
    inv = [eye + n_ab[c] for c in chains]
    pw = n_ab
    for _ in range(C.bit_length() - 2):
        pw = [_dot(pw[c], pw[c]) for c in chains]
        inv = [inv[c] + _dot(inv[c], pw[c]) for c in chains]
    wu = [_dot(inv[c], jnp.concatenate([a2s[c], mv[c]], axis=1)) for c in chains]
    qy = [_dot(m_rb[c], wu[c]) + jnp.concatenate([r2s[c], mrkv[c]], axis=1) for c in chains]
    g_t = [_dot(bhts[c], wu[c][:, :PAIR]) + eye * ptots[c] for c in chains]
    h_t = [_dot(bkts[c], jnp.concatenate([wu[c][:, PAIR:], v2s[c]], axis=0)) for c in chains]
    s0 = [state_ref[c] for c in chains]
    y2 = [_dot(qy[c][:, :PAIR], s0[c]) + qy[c][:, PAIR:] for c in chains]
    for c in chains:
        state_ref[c] = _dot(g_t[c], s0[c]) + h_t[c]

    segm = segm_ref[...]
    for bi in range(nb):
        y = jnp.concatenate([y2[bi * npair + p][:C] + y2[bi * npair + p][C:] for p in range(npair)], axis=1)
        mean = _segsum(y, segm)
        yc = y - mean
        var = _segsum(yc * yc, segm)
        yn = yc * lax.rsqrt(var + RW_GN_EPS) * gnw_ref[...] + gnb_ref[...]
        y_ref[bi] = ((yn + bonus_ref[bi]) * g_ref[bi]).astype(y_ref.dtype)


def _wkv(r, lw, k2, v, kk, alr, g, bonus, gnw, gnb, tri, segm, B, S):
    T, width = r.shape
    C = WKV_CHUNK
    nb = WKV_BATCH
    nc = S // C
    fixed = lambda b, c: (0, 0)
    blk = pl.BlockSpec((nb, C, width), lambda b, c: (b, c, 0))
    vec = pl.BlockSpec((1, width), fixed)
    ins = [t.reshape(B, S, width) for t in (r, lw, k2, v, kk, alr, g, bonus)]
    y = pl.pallas_call(
        _wkv_kernel,
        grid=(B // nb, nc),
        in_specs=[blk] * 8 + [vec, vec, pl.BlockSpec((C, C), fixed), pl.BlockSpec((width, width), fixed)],
        out_specs=blk,
        out_shape=jax.ShapeDtypeStruct((B, S, width), BF16),
        scratch_shapes=[pltpu.VMEM((nb * width // PAIR, PAIR, PAIR), F32)],
        compiler_params=_cparams(("parallel", "arbitrary")),
        name="wkv",
    )(*ins, gnw, gnb, tri, segm)
    return y.reshape(T, width)


def _split3(x):
    hi = x.astype(BF16).astype(F32)
    mid = (x - hi).astype(BF16).astype(F32)
    return hi, mid, x - hi - mid


def _spread_heads(x):
    lane = lax.broadcasted_iota(jnp.int32, (x.shape[0], PAIR), 1)
    groups = []
    for p in range(x.shape[1] // PAIR):
        blk = x[:, p * PAIR:(p + 1) * PAIR]
        groups.append(jnp.where(lane < HEAD_DIM, blk, 0.0))
        groups.append(jnp.where(lane < HEAD_DIM, pltpu.roll(blk, HEAD_DIM, axis=1), 0.0))
    return jnp.concatenate(groups, axis=1)


def _fx_prep_kernel(q_ref, k_ref, v_ref, fzt_ref, bf_ref, qw_ref, kw_ref, segm_ref, tril_ref,
                    place_ref, oneq_ref, onek_ref, onev_ref, qa_ref, ka_ref, va_ref, carry_ref):
    @pl.when(pl.program_id(1) == 0)
    def _():
        carry_ref[...] = jnp.zeros_like(carry_ref)

    segm = segm_ref[...]
    q = q_ref[...]
    k = k_ref[...]
    tm = q.shape[0]
    qn = q * lax.rsqrt(_segsum(q * q, segm) + QK_EPS) * (qw_ref[...] * (HEAD_DIM ** -0.5 * LOG2E))
    kn = k * lax.rsqrt(_segsum(k * k, segm) + QK_EPS) * kw_ref[...]

    lf = -_softplus(-(fzt_ref[...] + bf_ref[...]))
    c = lax.dot_general(tril_ref[...], lf, (((1,), (1,)), ((), ())), precision=HIGHEST,
                        preferred_element_type=F32) + carry_ref[...]
    carry_ref[...] = c[tm - 1:tm, :]
    placed = jnp.dot(c * LOG2E, place_ref[...], precision=HIGHEST, preferred_element_type=F32)
    hi, mid, lo = _split3(placed)
    lane = lax.broadcasted_iota(jnp.int32, placed.shape, 1) % PAIR
    c_q = jnp.where(lane == HEAD_DIM + 3, hi, jnp.where(lane == HEAD_DIM + 4, mid,
                                                        jnp.where(lane == HEAD_DIM + 5, lo, 0.0)))
    c_k = jnp.where(lane == HEAD_DIM, hi, jnp.where(lane == HEAD_DIM + 1, mid,
                                                    jnp.where(lane == HEAD_DIM + 2, lo, 0.0)))
    qa_ref[...] = (_spread_heads(qn) + c_q + oneq_ref[...]).astype(qa_ref.dtype)
    ka_ref[...] = (_spread_heads(kn) - c_k + onek_ref[...]).astype(ka_ref.dtype)
    va_ref[...] = (_spread_heads(v_ref[...]) + onev_ref[...]).astype(va_ref.dtype)


def _fx_prep(p_fx, fzt, bf, qw, kw, segm, tril, B, S, width):
    T = p_fx.shape[0]
    nh = fzt.shape[0]
    wide = nh * PAIR
    tm = ROW_TILE
    ns = S // tm
    fixed = lambda b, s: (0, 0)
    col = lambda j: pl.BlockSpec((tm, width), lambda b, s: (b * ns + s, j))
    out = pl.BlockSpec((tm, wide), lambda b, s: (b * ns + s, 0))
    vec = pl.BlockSpec((1, width), fixed)
    wvec = pl.BlockSpec((1, wide), fixed)

    lane = jnp.arange(wide) % PAIR
    head = jnp.arange(wide) // PAIR
    place = ((head[None, :] == jnp.arange(nh)[:, None]) & (lane >= HEAD_DIM) & (lane < HEAD_DIM + 6)).astype(F32)
    one_q = ((lane >= HEAD_DIM) & (lane < HEAD_DIM + 3)).astype(F32).reshape(1, wide)
    one_k = ((lane >= HEAD_DIM + 3) & (lane < HEAD_DIM + 6)).astype(F32).reshape(1, wide)
    one_v = (lane >= HEAD_DIM).astype(F32).reshape(1, wide)
    return pl.pallas_call(
        _fx_prep_kernel,
        grid=(B, ns),
        in_specs=[
            col(0), col(1), col(2),
            pl.BlockSpec((nh, tm), lambda b, s: (0, b * ns + s)),
            pl.BlockSpec((nh, 1), fixed),
            vec, vec,
            pl.BlockSpec((width, width), fixed),
            pl.BlockSpec((tm, tm), fixed),
            pl.BlockSpec((nh, wide), fixed),
            wvec, wvec, wvec,
        ],
        out_specs=[out, out, out],
        out_shape=[jax.ShapeDtypeStruct((T, wide), BF16)] * 3,
        scratch_shapes=[pltpu.VMEM((1, nh), F32)],
        compiler_params=_cparams(("parallel", "arbitrary")),
        name="fx_prep",
    )(p_fx, p_fx, p_fx, fzt, bf, qw, kw, segm, tril, place, one_q, one_k, one_v)


def _fox_attn_kernel(qi_ref, kj_ref, q_ref, k_ref, v_ref, og_ref, o_ref, m_ref, acc_ref):
    i = qi_ref[pl.program_id(2)]
    j = kj_ref[pl.program_id(2)]
    tq = q_ref.shape[0]
    tk = k_ref.shape[0]
    heads = range(2)
    grp = lambda ref, h: ref[:, h * PAIR:(h + 1) * PAIR]

    @pl.when(j == 0)
    def _():
        m_ref[...] = jnp.full_like(m_ref, NEG_BIG)
        acc_ref[...] = jnp.zeros_like(acc_ref)

    def step(masked):
        s = [lax.dot_general(grp(q_ref, h), grp(k_ref, h), (((1,), (1,)), ((), ())), preferred_element_type=F32)
             for h in heads]
        if masked:
            row = lax.broadcasted_iota(jnp.int32, (tq, tk), 0)
            col = lax.broadcasted_iota(jnp.int32, (tq, tk), 1)
            s = [jnp.where(col <= row, s[h], NEG_BIG) for h in heads]
        m_old = [m_ref[h] for h in heads]
        m_new = [jnp.maximum(m_old[h], jnp.max(s[h], axis=1, keepdims=True)) for h in heads]
        alpha = [jnp.exp2(m_old[h] - m_new[h]) for h in heads]
        pr = [jnp.exp2(s[h] - pltpu.repeat(m_new[h], tk // PAIR, axis=1)).astype(BF16) for h in heads]
        pv = [jnp.dot(pr[h], grp(v_ref, h), preferred_element_type=F32) for h in heads]
        for h in heads:
            acc_ref[h] = alpha[h] * acc_ref[h] + pv[h]
            m_ref[h] = m_new[h]

    @pl.when(j < i)
    def _():
        step(False)

    @pl.when(j == i)
    def _():
        step(True)
        a0 = acc_ref[0]
        a1 = acc_ref[1]
        lane = lax.broadcasted_iota(jnp.int32, a0.shape, 1)
        o0 = a0 * pltpu.roll(1.0 / a0, HEAD_DIM, axis=1)
        o1 = pltpu.roll(a1, HEAD_DIM, axis=1) * (1.0 / a1)
        o = jnp.where(lane < HEAD_DIM, o0, o1)
        o_ref[...] = (o * _sigmoid(og_ref[...])).astype(o_ref.dtype)


def _fox_attn(qa, ka, va, p_fx, B, S, width):
    T = qa.shape[0]
    tq, tk = ATTN_TQ, ATTN_TK
    assert tq == tk
    nq = S // tq
    npair = width // PAIR
    og_col0 = 3 * width // PAIR
    pairs = [(i, j) for i in range(nq) for j in range(i + 1)]
    qi = jnp.array([ij[0] for ij in pairs], jnp.int32)
    kj = jnp.array([ij[1] for ij in pairs], jnp.int32)
    kv = pl.BlockSpec((tk, 2 * PAIR), lambda b, p, t, qi, kj: (b * nq + kj[t], p))
    grid_spec = pltpu.PrefetchScalarGridSpec(
        num_scalar_prefetch=2,
        grid=(B, npair, len(pairs)),
        in_specs=[
            pl.BlockSpec((tq, 2 * PAIR), lambda b, p, t, qi, kj: (b * nq + qi[t], p)), kv, kv,
            pl.BlockSpec((tq, PAIR), lambda b, p, t, qi, kj: (b * nq + qi[t], og_col0 + p)),
        ],
        out_specs=pl.BlockSpec((tq, PAIR), lambda b, p, t, qi, kj: (b * nq + qi[t], p)),
        scratch_shapes=[pltpu.VMEM((2, tq, PAIR), F32), pltpu.VMEM((2, tq, PAIR), F32)],
    )
    return pl.pallas_call(
        _fox_attn_kernel,
        grid_spec=grid_spec,
        out_shape=jax.ShapeDtypeStruct((T, width), BF16),
        compiler_params=_cparams(("parallel", "parallel", "arbitrary")),
        name="fox_attn",
    )(qi, kj, qa, ka, va, p_fx)


def _out_ln_kernel(yrw_ref, yfx_ref, h_ref, wo1_ref, wo2_ref, lnw_ref, lnb_ref, rw_ref, rb_ref,
                   h1_ref, logit_ref, *, alpha):
    mix = (jnp.dot(yrw_ref[...], wo1_ref[...], preferred_element_type=F32)
           + jnp.dot(yfx_ref[...], wo2_ref[...], preferred_element_type=F32))
    h1 = _layer_norm(alpha * h_ref[...] + mix, lnw_ref[...], lnb_ref[...])
    h1_ref[...] = h1
    logit_ref[...] = jnp.dot(h1, rw_ref[...], precision=HIGHEST, preferred_element_type=F32) + rb_ref[...]


def _out_ln(y_rw, y_fx, h0, wo1, wo2, lnw, lnb, rw, rb, alpha):
    T, D = h0.shape
    width = y_rw.shape[1]
    ne = rw.shape[1]
    tm = ROW_TILE
    row = lambda i: (i, 0)
    fixed = lambda i: (0, 0)
    return pl.pallas_call(
        functools.partial(_out_ln_kernel, alpha=alpha),
        grid=(T // tm,),
        in_specs=[
            pl.BlockSpec((tm, width), row), pl.BlockSpec((tm, width), row), pl.BlockSpec((tm, D), row),
            pl.BlockSpec((width, D), fixed), pl.BlockSpec((width, D), fixed),
            pl.BlockSpec((1, D), fixed), pl.BlockSpec((1, D), fixed),
            pl.BlockSpec((D, ne), fixed), pl.BlockSpec((1, ne), fixed),
        ],
        out_specs=[pl.BlockSpec((tm, D), row), pl.BlockSpec((tm, ne), row)],
        out_shape=[jax.ShapeDtypeStruct((T, D), F32), jax.ShapeDtypeStruct((T, ne), F32)],
        compiler_params=_cparams(("parallel",)),
        name="out_ln",
    )(y_rw, y_fx, h0, wo1, wo2, lnw, lnb, rw, rb)


DEINT_COLS = 256


def _w1_split_kernel(w_ref, perm_ref, g_ref, l_ref):
    half = DEINT_COLS // 2
    for c in range(w_ref.shape[2] // DEINT_COLS):
        blk = w_ref[0, :, c * DEINT_COLS:(c + 1) * DEINT_COLS].astype(BF16)
        out = jnp.dot(blk, perm_ref[...], preferred_element_type=F32).astype(BF16)
        g_ref[0, :, c * half:(c + 1) * half] = out[:, :half]
        l_ref[0, :, c * half:(c + 1) * half] = out[:, half:]


def _w1_split(w1):
    E, D, F2 = w1.shape
    tr = 512
    half = DEINT_COLS // 2
    src = jnp.arange(DEINT_COLS)
    dst = jnp.where(src % 2 == 0, src // 2, half + src // 2)
    perm = (dst[:, None] == jnp.arange(DEINT_COLS)[None, :]).astype(BF16)
    out = pl.BlockSpec((1, tr, F2 // 2), lambda e, r: (e, r, 0))
    return pl.pallas_call(
        _w1_split_kernel,
        grid=(E, D // tr),
        in_specs=[pl.BlockSpec((1, tr, F2), lambda e, r: (e, r, 0)),
                  pl.BlockSpec((DEINT_COLS, DEINT_COLS), lambda e, r: (0, 0))],
        out_specs=[out, out],
        out_shape=[jax.ShapeDtypeStruct((E, D, F2 // 2), BF16)] * 2,
        compiler_params=_cparams(("parallel", "parallel")),
        name="w1_split",
    )(w1, perm)


def _moe_kernel(bexp_ref, nused_ref, x_ref, w1g_ref, w1l_ref, b1g_ref, b1l_ref, w2_ref, b2_ref, o_ref):
    i = pl.program_id(0)

    @pl.when(i < nused_ref[0])
    def _():
        x = x_ref[...].astype(BF16)
        x_glu = jnp.minimum(jnp.dot(x, w1g_ref[0], preferred_element_type=F32) + b1g_ref[0], SWIGLU_LIMIT)
        x_lin = jnp.clip(jnp.dot(x, w1l_ref[0], preferred_element_type=F32) + b1l_ref[0],
                         -SWIGLU_LIMIT, SWIGLU_LIMIT)
        act = x_glu * _sigmoid(SWIGLU_ALPHA * x_glu) * (x_lin + 1.0)
        o_ref[...] = jnp.dot(act.astype(BF16), w2_ref[0], preferred_element_type=F32) + b2_ref[0]

    @pl.when(i >= nused_ref[0])
    def _():
        o_ref[...] = jnp.zeros_like(o_ref)


def _moe_ffn(block_exp, n_used, xs, w1g, w1l, b1g, b1l, w2, b2):
    n_rows, D = xs.shape
    F = w1g.shape[2]
    tm = MOE_TILE
    n_blocks = n_rows // tm
    live = lambda i, be, nu: jnp.minimum(i, nu[0] - 1)
    wspec = lambda shape: pl.BlockSpec((1,) + shape, lambda i, be, nu: (be[live(i, be, nu)], 0, 0))
    grid_spec = pltpu.PrefetchScalarGridSpec(
        num_scalar_prefetch=2,
        grid=(n_blocks,),
        in_specs=[
            pl.BlockSpec((tm, D), lambda i, be, nu: (live(i, be, nu), 0)),
            wspec((D, F)), wspec((D, F)), wspec((1, F)), wspec((1, F)), wspec((F, D)), wspec((1, D)),
        ],
        out_specs=pl.BlockSpec((tm, D), lambda i, be, nu: (i, 0)),
    )
    return pl.pallas_call(
        _moe_kernel,
        grid_spec=grid_spec,
        out_shape=jax.ShapeDtypeStruct((n_rows, D), F32),
        compiler_params=_cparams(("arbitrary",)),
        name="moe_ffn",
    )(block_exp, n_used, xs, w1g, w1l, b1g, b1l, w2, b2)


def _combine_ln_kernel(h_ref, eo_ref, gate_ref, lnw_ref, lnb_ref, o_ref, *, alpha):
    gates = gate_ref[...]
    ffn = eo_ref[0] * gates[:, 0:1]
    for j in range(1, TOP_K):
        ffn = ffn + eo_ref[j] * gates[:, j:j + 1]
    o_ref[...] = _layer_norm(alpha * h_ref[...] + ffn, lnw_ref[...], lnb_ref[...])


def _combine_ln(h1, eo, gates, lnw, lnb, alpha):
    T, D = h1.shape
    tm = ROW_TILE
    return pl.pallas_call(
        functools.partial(_combine_ln_kernel, alpha=alpha),
        grid=(T // tm,),
        in_specs=[
            pl.BlockSpec((tm, D), lambda i: (i, 0)),
            pl.BlockSpec((TOP_K, tm, D), lambda i: (0, i, 0)),
            pl.BlockSpec((tm, TOP_K), lambda i: (i, 0)),
            pl.BlockSpec((1, D), lambda i: (0, 0)),
            pl.BlockSpec((1, D), lambda i: (0, 0)),
        ],
        out_specs=pl.BlockSpec((tm, D), lambda i: (i, 0)),
        out_shape=jax.ShapeDtypeStruct((T, D), F32),
        compiler_params=_cparams(("parallel",)),
        name="combine_ln",
    )(h1, eo, gates, lnw, lnb)


def _pad_to(x, n, axis):
    pad = [(0, 0)] * x.ndim
    pad[axis] = (0, n - x.shape[axis])
    return jnp.pad(x, pad)


def _block_diag_ones(width, value=1.0):
    idx = jnp.arange(width) // HEAD_DIM
    return jnp.where(idx[:, None] == idx[None, :], value, 0.0).astype(BF16)


def _route(logits, n_experts, tile):
    T = logits.shape[0]
    top_val, top_idx = lax.top_k(logits[:, :n_experts], TOP_K)
    gates = jax.nn.softmax(top_val, axis=-1)
    e_flat = top_idx.reshape(-1).astype(jnp.int32)
    n_assign = T * TOP_K
    n_rows = n_assign + n_experts * tile
    n_blocks = n_rows // tile
    assert n_experts * n_assign < 2 ** 31
    eids = jnp.arange(n_experts, dtype=jnp.int32)
    aids = jnp.arange(n_assign, dtype=jnp.int32)
    skeys = lax.sort(e_flat * n_assign + aids)
    order = skeys % n_assign
    e_sorted = skeys // n_assign
    counts = jnp.sum((e_flat[:, None] == eids[None, :]).astype(jnp.int32), axis=0)
    starts = jnp.cumsum(counts) - counts
    padded = (counts + tile - 1) // tile * tile
    pends = jnp.cumsum(padded)
    pstarts = pends - padded
    shift = pstarts - starts
    dest_sorted = aids + jnp.sum(jnp.where(e_sorted[:, None] == eids[None, :], shift[None, :], 0), axis=1)
    _, dest = lax.sort((order, dest_sorted), num_keys=1)
    block_start = jnp.arange(n_blocks, dtype=jnp.int32) * tile
    block_exp = jnp.minimum(jnp.sum((pends[None, :] <= block_start[:, None]).astype(jnp.int32), axis=1),
                            n_experts - 1).astype(jnp.int32)
    n_used = (pends[-1] // tile).astype(jnp.int32).reshape(1)
    in_exp = block_start - pstarts[block_exp]
    n_valid = counts[block_exp] - in_exp
    within = jnp.arange(tile, dtype=jnp.int32)[None, :]
    src = jnp.clip((starts[block_exp] + in_exp)[:, None] + within, 0, n_assign - 1)
    row_tok = jnp.where(within < n_valid[:, None], order[src.reshape(-1)].reshape(n_blocks, tile) // TOP_K, 0)
    return gates, dest.reshape(T, TOP_K), row_tok.reshape(-1).astype(jnp.int32), block_exp, n_used


def kernel(x, ln_in_w, ln_in_b, w_in, rw_mu, rw_w0, rw_w2, rw_a0, rw_a2, rw_g2, rw_k_k, rw_k_a, rw_r_k,
           rw_gn_w, rw_gn_b, fx_b_f, fx_q_norm, fx_k_norm, w_o, ln1_w, ln1_b, router_w, router_b,
           exp_w1, exp_b1, exp_w2, exp_b2, ln2_w, ln2_b):
    B, S, D = x.shape
    T = B * S
    depth = w_in.shape[0]
    alpha = (2 * depth) ** 0.25
    rw_w = rw_w0.shape[1]
    fx_heads = fx_b_f.shape[1]
    fx_w = fx_heads * HEAD_DIM
    d_lora, a_lora, g_lora = rw_w2.shape[1], rw_a2.shape[1], rw_g2.shape[1]
    n_lora = d_lora + a_lora + g_lora
    lora_pad = -(-n_lora // 128) * 128
    rw_cols = 3 * rw_w + n_lora
    n_rw = 3 * rw_w + lora_pad
    n_experts = router_w.shape[2]
    ne_pad = -(-n_experts // 128) * 128
    row = lambda a: a.reshape(1, -1)

    seg_rw = _block_diag_ones(rw_w)
    segm_rw = _block_diag_ones(rw_w, 1.0 / HEAD_DIM)
    segm_fx = _block_diag_ones(fx_w, 1.0 / HEAD_DIM)
    tidx = jnp.arange(ROW_TILE)
    tril = (tidx[:, None] >= tidx[None, :]).astype(F32)
    cidx = jnp.arange(WKV_CHUNK)
    tri_c = (cidx[:, None] >= cidx[None, :]).astype(F32)

    assert depth == 1, "single-layer block"
    l = 0
    x2 = x.reshape(T, D)
    w_l = w_in[l]
    w_main = jnp.concatenate(
        [_pad_to(w_l[:, :rw_cols], n_rw, 1), w_l[:, rw_cols:rw_cols + 4 * fx_w]], axis=1).astype(BF16)
    wfz_t = w_l[:, rw_cols + 4 * fx_w:].T
    h0, p_rw, p_fx, fzt = _ln_proj(x2, row(ln_in_w), row(ln_in_b), w_main, wfz_t, n_rw)

    mu = _pad_to(row(rw_mu[l]), n_rw, 1)
    w2p = _pad_to(rw_w2[l], lora_pad, 0).astype(BF16)
    a2p = _pad_to(jnp.pad(rw_a2[l], ((d_lora, 0), (0, 0))), lora_pad, 0).astype(BF16)
    g2p = _pad_to(jnp.pad(rw_g2[l], ((d_lora + a_lora, 0), (0, 0))), lora_pad, 0).astype(BF16)
    r, lw, k2, v, kk, alr, g, bonus = _rw_prep(
        p_rw, B, S, rw_w, mu, row(rw_w0[l]), w2p, row(rw_a0[l]), a2p, g2p,
        row(rw_k_k[l]), row(rw_k_a[l]), row(rw_r_k[l]), seg_rw)
    y_rw = _wkv(r, lw, k2, v, kk, alr, g, bonus, row(rw_gn_w[l]), row(rw_gn_b[l]), tri_c, segm_rw, B, S)

    qw = row(jnp.tile(fx_q_norm[l], fx_heads))
    kw = row(jnp.tile(fx_k_norm[l], fx_heads))
    qa, ka, va = _fx_prep(p_fx, fzt, fx_b_f[l].reshape(-1, 1), qw, kw, segm_fx, tril, B, S, fx_w)
    y_fx = _fox_attn(qa, ka, va, p_fx, B, S, fx_w)

    wo = w_o[l].astype(BF16)
    rw_pad = _pad_to(router_w[l], ne_pad, 1)
    rb_pad = _pad_to(row(router_b[l]), ne_pad, 1)
    h1, logits = _out_ln(y_rw, y_fx, h0, wo[:rw_w], wo[rw_w:], row(ln1_w[l]), row(ln1_b[l]),
                         rw_pad, rb_pad, alpha)

    gates, pos, row_tok, block_exp, n_used = _route(logits, n_experts, MOE_TILE)
    xs = h1[row_tok]
    w1g, w1l = _w1_split(exp_w1[l])
    b1 = exp_b1[l]
    b1g = b1[:, None, 0::2]
    b1l = b1[:, None, 1::2]
    eo_rows = _moe_ffn(block_exp, n_used, xs, w1g, w1l, b1g, b1l, exp_w2[l].astype(BF16), exp_b2[l][:, None, :])
    eo = eo_rows[pos.T]
    h = _combine_ln(h1, eo, gates, row(ln2_w[l]), row(ln2_b[l]), alpha)
    return h.reshape(B, S, D)
```

```python
import functools

import jax
import jax.numpy as jnp
from jax import lax
from jax.experimental import pallas as pl
from jax.experimental.pallas import tpu as pltpu

F32 = jnp.float32
BF16 = jnp.bfloat16
HIGHEST = lax.Precision.HIGHEST

HEAD_DIM = 64
PAIR = 2 * HEAD_DIM
WKV_CHUNK = 64
RW_GN_EPS = 64e-5
QK_EPS = 1e-6
LN_EPS = 1e-5
TOP_K = 4
SWIGLU_ALPHA = 1.702
SWIGLU_LIMIT = 7.0
NEG_BIG = -1e30
LOG2E = 1.4426950408889634

ROW_TILE = 256
ATTN_TQ = 512
ATTN_TK = 512
ATTN_HEADS = 4
MOE_TILE = 256
WKV_BATCH = 4
VMEM_LIMIT = 48 * 1024 * 1024


def _cparams(sem):
    return pltpu.CompilerParams(dimension_semantics=sem, vmem_limit_bytes=VMEM_LIMIT)


def _dot(a, b):
    return jnp.dot(a.astype(BF16), b.astype(BF16), preferred_element_type=F32)


def _dot_t(a, b):
    return lax.dot_general(a.astype(BF16), b.astype(BF16), (((1,), (1,)), ((), ())),
                           preferred_element_type=F32)


def _segsum(x, seg):
    hi = x.astype(BF16)
    lo = (x - hi.astype(F32)).astype(BF16)
    return (jnp.dot(hi, seg, preferred_element_type=F32) + jnp.dot(lo, seg, preferred_element_type=F32))


def _sigmoid(x):
    return 1.0 / (1.0 + jnp.exp(-x))


def _softplus(x):
    return jnp.maximum(x, 0.0) + jnp.log(1.0 + jnp.exp(-jnp.abs(x)))


def _layer_norm(x, w, b):
    mu = jnp.mean(x, axis=-1, keepdims=True)
    xc = x - mu
    var = jnp.mean(xc * xc, axis=-1, keepdims=True)
    return xc * lax.rsqrt(var + LN_EPS) * w + b


def _ln_proj_kernel(x_ref, lnw_ref, lnb_ref, w_ref, wfz_ref, h_ref, prw_ref, pfx_ref, fzt_ref, *, n_rw):
    h = _layer_norm(x_ref[...], lnw_ref[...], lnb_ref[...])
    h_ref[...] = h
    p = jnp.dot(h.astype(BF16), w_ref[...], preferred_element_type=F32)
    prw_ref[...] = p[:, :n_rw]
    pfx_ref[...] = p[:, n_rw:]
    fzt_ref[...] = lax.dot_general(wfz_ref[...], h, (((1,), (1,)), ((), ())),
                                   precision=HIGHEST, preferred_element_type=F32)


def _ln_proj(x2, lnw, lnb, w_main, wfz_t, n_rw):
    T, D = x2.shape
    n_all = w_main.shape[1]
    n_fx = n_all - n_rw
    nh = wfz_t.shape[0]
    tm = ROW_TILE
    return pl.pallas_call(
        functools.partial(_ln_proj_kernel, n_rw=n_rw),
        grid=(T // tm,),
        in_specs=[
            pl.BlockSpec((tm, D), lambda i: (i, 0)),
            pl.BlockSpec((1, D), lambda i: (0, 0)),
            pl.BlockSpec((1, D), lambda i: (0, 0)),
            pl.BlockSpec((D, n_all), lambda i: (0, 0)),
            pl.BlockSpec((nh, D), lambda i: (0, 0)),
        ],
        out_specs=[
            pl.BlockSpec((tm, D), lambda i: (i, 0)),
            pl.BlockSpec((tm, n_rw), lambda i: (i, 0)),
            pl.BlockSpec((tm, n_fx), lambda i: (i, 0)),
            pl.BlockSpec((nh, tm), lambda i: (0, i)),
        ],
        out_shape=[
            jax.ShapeDtypeStruct((T, D), F32),
            jax.ShapeDtypeStruct((T, n_rw), F32),
            jax.ShapeDtypeStruct((T, n_fx), F32),
            jax.ShapeDtypeStruct((nh, T), F32),
        ],
        compiler_params=_cparams(("parallel",)),
        name="ln_proj",
    )(x2, lnw, lnb, w_main, wfz_t)


def _rw_prep_kernel(p_ref, mu_ref, w0_ref, w2_ref, a0_ref, a2_ref, g2_ref, kkw_ref, ka_ref, rk_ref, seg_ref,
                    r_ref, lw_ref, k_ref, v_ref, kk_ref, alr_ref, g_ref, bonus_ref, carry_ref, *, width):
    @pl.when(pl.program_id(1) == 0)
    def _():
        carry_ref[...] = jnp.zeros_like(carry_ref)

    p = p_ref[...]
    tm = p.shape[0]
    prev = pltpu.roll(p, 1, axis=0)
    first_row = lax.broadcasted_iota(jnp.int32, p.shape, 0) == 0
    prev = jnp.where(first_row, carry_ref[...], prev)
    carry_ref[...] = p[tm - 1:tm, :]
    ps = p + mu_ref[...] * (prev - p)

    r = ps[:, 0:width]
    k = ps[:, width:2 * width]
    v = ps[:, 2 * width:3 * width]
    lora = ps[:, 3 * width:]
    seg = seg_ref[...]

    wl = w0_ref[...] + _dot(jnp.tanh(lora), w2_ref[...])
    w_raw = -_softplus(-wl) - 0.5
    lw_ref[...] = -jnp.exp(w_raw)
    alr = _sigmoid(a0_ref[...] + _dot(lora, a2_ref[...]))
    g_ref[...] = _dot(_sigmoid(lora), g2_ref[...])
    kkp = k * kkw_ref[...]
    nrm = jnp.sqrt(_segsum(kkp * kkp, seg))
    kk_ref[...] = kkp / jnp.maximum(nrm, 1e-12)
    k2 = k * (1.0 + (alr - 1.0) * ka_ref[...])
    bonus_ref[...] = _segsum(r * k2 * rk_ref[...], seg) * v
    r_ref[...] = r
    k_ref[...] = k2
    v_ref[...] = v
    alr_ref[...] = alr


def _rw_prep(p_rw, B, S, width, mu, w0, w2p, a0, a2p, g2p, kkw, ka, rk, seg):
    T, n_rw = p_rw.shape
    tm = ROW_TILE
    ns = S // tm
    n_lora = n_rw - 3 * width
    row = lambda b, s: (b * ns + s, 0)
    fixed = lambda b, s: (0, 0)
    vec = pl.BlockSpec((1, width), fixed)
    out = pl.BlockSpec((tm, width), row)
    return pl.pallas_call(
        functools.partial(_rw_prep_kernel, width=width),
        grid=(B, ns),
        in_specs=[
            pl.BlockSpec((tm, n_rw), row),
            pl.BlockSpec((1, n_rw), fixed),
            vec, pl.BlockSpec((n_lora, width), fixed),
            vec, pl.BlockSpec((n_lora, width), fixed),
            pl.BlockSpec((n_lora, width), fixed),
            vec, vec, vec,
            pl.BlockSpec((width, width), fixed),
        ],
        out_specs=[out] * 8,
        out_shape=[jax.ShapeDtypeStruct((T, width), F32)] * 8,
        scratch_shapes=[pltpu.VMEM((1, n_rw), F32)],
        compiler_params=_cparams(("parallel", "arbitrary")),
        name="rw_prep",
    )(p_rw, mu, w0, w2p, a0, a2p, g2p, kkw, ka, rk, seg)


def _stack_heads(x):
    lane = lax.broadcasted_iota(jnp.int32, x.shape, 1)
    return jnp.concatenate([jnp.where(lane < HEAD_DIM, x, 0.0), jnp.where(lane >= HEAD_DIM, x, 0.0)], axis=0)


def _wkv_kernel(r_ref, lw_ref, k_ref, v_ref, kk_ref, alr_ref, g_ref, bonus_ref, gnw_ref, gnb_ref, tri_ref,
                segm_ref, y_ref, state_ref):
    C = WKV_CHUNK
    nb, _, width = lw_ref.shape
    npair = width // PAIR

    @pl.when(pl.program_id(1) == 0)
    def _():
        state_ref[...] = jnp.zeros_like(state_ref)

    ri = lax.broadcasted_iota(jnp.int32, (2 * C, 2 * C), 0)
    ci = lax.broadcasted_iota(jnp.int32, (2 * C, 2 * C), 1)
    same = (ri // C) == (ci // C)
    strict = same & ((ci % C) < (ri % C))
    incl = same & ((ci % C) <= (ri % C))
    eye = (ri == ci).astype(F32)

    lhs, rhs, a2s, r2s, v2s, bhts, bkts, ptots = [], [], [], [], [], [], [], []
    for bi in range(nb):
        lw = lw_ref[bi]
        cum = jnp.dot(tri_ref[...], lw, precision=HIGHEST, preferred_element_type=F32)
        total = cum[C - 1:C, :]
        p_inv = jnp.exp(-cum)
        p_rem = jnp.exp(total - cum)
        p_tot = jnp.exp(total)
        kk = kk_ref[bi]
        k2 = k_ref[bi]
        b = kk * alr_ref[bi]
        a_t = -kk * jnp.exp(cum - lw)
        r_t = r_ref[bi] * jnp.exp(cum)
        b_t = b * p_inv
        k_t = k2 * p_inv
        b_h = b * p_rem
        k_h = k2 * p_rem
        v = v_ref[bi]
        for p in range(npair):
            sl = slice(p * PAIR, (p + 1) * PAIR)
            a2, r2, b2, kt2 = (_stack_heads(t[:, sl]) for t in (a_t, r_t, b_t, k_t))
            bh2, kh2, v2 = (_stack_heads(t[:, sl]) for t in (b_h, k_h, v))
            lhs.append(jnp.concatenate([a2, r2], axis=0))
            rhs.append(jnp.concatenate([b2, kt2], axis=0))
            a2s.append(a2)
            r2s.append(r2)
            v2s.append(v2)
            bhts.append(bh2.T)
            bkts.append(jnp.concatenate([bh2.T, kh2.T], axis=1))
            ptots.append(p_tot[:, sl])

    chains = range(nb * npair)
    m = [_dot_t(lhs[c], rhs[c]) for c in chains]
    n_ab = [jnp.where(strict, m[c][:2 * C, :2 * C], 0.0) for c in chains]
    m_ak = [jnp.where(strict, m[c][:2 * C, 2 * C:], 0.0) for c in chains]
    m_rb = [jnp.where(incl, m[c][2 * C:, :2 * C], 0.0) for c in chains]
    m_rk = [jnp.where(incl, m[c][2 * C:, 2 * C:], 0.0) for c in chains]
    mv = [_dot(m_ak[c], v2s[c]) for c in chains]
    mrkv = [_dot(m_rk[c], v2s[c]) for c in chains]
    inv = [eye + n_ab[c] for c in chains]
    pw = n_ab
    for _ in range(C.bit_length() - 2):
        pw = [_dot(pw[c], pw[c]) for c in chains]
        inv = [inv[c] + _dot(inv[c], pw[c]) for c in chains]
    wu = [_dot(inv[c], jnp.concatenate([a2s[c], mv[c]], axis=1)) for c in chains]
    qy = [_dot(m_rb[c], wu[c]) + jnp.concatenate([r2s[c], mrkv[c]], axis=1) for c in chains]
    g_t = [_dot(bhts[c], wu[c][:, :PAIR]) + eye * ptots[c] for c in chains]
    h_t = [_dot(bkts[c], jnp.concatenate([wu[c][:, PAIR:], v2s[c]], axis=0)) for c in chains]
    s0 = [state_ref[c] for c in chains]
    y2 = [_dot(qy[c][:, :PAIR], s0[c]) + qy[c][:, PAIR:] for c in chains]
    for c in chains:
        state_ref[c] = _dot(g_t[c], s0[c]) + h_t[c]

    segm = segm_ref[...]
    for bi in range(nb):
        y = jnp.concatenate([y2[bi * npair + p][:C] + y2[bi * npair + p][C:] for p in range(npair)], axis=1)
        mean = _segsum(y, segm)
        yc = y - mean
        var = _segsum(yc * yc, segm)
        yn = yc * lax.rsqrt(var + RW_GN_EPS) * gnw_ref[...] + gnb_ref[...]
        y_ref[bi] = ((yn + bonus_ref[bi]) * g_ref[bi]).astype(y_ref.dtype)


def _wkv(r, lw, k2, v, kk, alr, g, bonus, gnw, gnb, tri, segm, B, S):
    T, width = r.shape
    C = WKV_CHUNK
    nb = WKV_BATCH
    nc = S // C
    fixed = lambda b, c: (0, 0)
    blk = pl.BlockSpec((nb, C, width), lambda b, c: (b, c, 0))
    vec = pl.BlockSpec((1, width), fixed)
    ins = [t.reshape(B, S, width) for t in (r, lw, k2, v, kk, alr, g, bonus)]
    y = pl.pallas_call(
        _wkv_kernel,
        grid=(B // nb, nc),
        in_specs=[blk] * 8 + [vec, vec, pl.BlockSpec((C, C), fixed), pl.BlockSpec((width, width), fixed)],
        out_specs=blk,
        out_shape=jax.ShapeDtypeStruct((B, S, width), BF16),
        scratch_shapes=[pltpu.VMEM((nb * width // PAIR, PAIR, PAIR), F32)],
        compiler_params=_cparams(("parallel", "arbitrary")),
        name="wkv",
    )(*ins, gnw, gnb, tri, segm)
    return y.reshape(T, width)


def _split3(x):
    hi = x.astype(BF16).astype(F32)
    mid = (x - hi).astype(BF16).astype(F32)
    return hi, mid, x - hi - mid


def _spread_heads(x):
    lane = lax.broadcasted_iota(jnp.int32, (x.shape[0], PAIR), 1)
    groups = []
    for p in range(x.shape[1] // PAIR):
        blk = x[:, p * PAIR:(p + 1) * PAIR]
        groups.append(jnp.where(lane < HEAD_DIM, blk, 0.0))
        groups.append(jnp.where(lane < HEAD_DIM, pltpu.roll(blk, HEAD_DIM, axis=1), 0.0))
    return jnp.concatenate(groups, axis=1)


def _fx_prep_kernel(q_ref, k_ref, v_ref, fzt_ref, bf_ref, qw_ref, kw_ref, segm_ref, tril_ref,
                    place_ref, oneq_ref, onek_ref, onev_ref, qa_ref, ka_ref, va_ref, carry_ref):
    @pl.when(pl.program_id(1) == 0)
    def _():
        carry_ref[...] = jnp.zeros_like(carry_ref)

    segm = segm_ref[...]
    q = q_ref[...]
    k = k_ref[...]
    tm = q.shape[0]
    qn = q * lax.rsqrt(_segsum(q * q, segm) + QK_EPS) * (qw_ref[...] * (HEAD_DIM ** -0.5 * LOG2E))
    kn = k * lax.rsqrt(_segsum(k * k, segm) + QK_EPS) * kw_ref[...]

    lf = -_softplus(-(fzt_ref[...] + bf_ref[...]))
    c = lax.dot_general(tril_ref[...], lf, (((1,), (1,)), ((), ())), precision=HIGHEST,
                        preferred_element_type=F32) + carry_ref[...]
    carry_ref[...] = c[tm - 1:tm, :]
    placed = jnp.dot(c * LOG2E, place_ref[...], precision=HIGHEST, preferred_element_type=F32)
    hi, mid, lo = _split3(placed)
    lane = lax.broadcasted_iota(jnp.int32, placed.shape, 1) % PAIR
    c_q = jnp.where(lane == HEAD_DIM + 3, hi, jnp.where(lane == HEAD_DIM + 4, mid,
                                                        jnp.where(lane == HEAD_DIM + 5, lo, 0.0)))
    c_k = jnp.where(lane == HEAD_DIM, hi, jnp.where(lane == HEAD_DIM + 1, mid,
                                                    jnp.where(lane == HEAD_DIM + 2, lo, 0.0)))
    qa_ref[...] = (_spread_heads(qn) + c_q + oneq_ref[...]).astype(qa_ref.dtype)
    ka_ref[...] = (_spread_heads(kn) - c_k + onek_ref[...]).astype(ka_ref.dtype)
    va_ref[...] = (_spread_heads(v_ref[...]) + onev_ref[...]).astype(va_ref.dtype)


def _fx_prep(p_fx, fzt, bf, qw, kw, segm, tril, B, S, width):
    T = p_fx.shape[0]
    nh = fzt.shape[0]
    wide = nh * PAIR
    tm = ROW_TILE
    ns = S // tm
    fixed = lambda b, s: (0, 0)
    col = lambda j: pl.BlockSpec((tm, width), lambda b, s: (b * ns + s, j))
    out = pl.BlockSpec((tm, wide), lambda b, s: (b * ns + s, 0))
    vec = pl.BlockSpec((1, width), fixed)
    wvec = pl.BlockSpec((1, wide), fixed)

    lane = jnp.arange(wide) % PAIR
    head = jnp.arange(wide) // PAIR
    place = ((head[None, :] == jnp.arange(nh)[:, None]) & (lane >= HEAD_DIM) & (lane < HEAD_DIM + 6)).astype(F32)
    one_q = ((lane >= HEAD_DIM) & (lane < HEAD_DIM + 3)).astype(F32).reshape(1, wide)
    one_k = ((lane >= HEAD_DIM + 3) & (lane < HEAD_DIM + 6)).astype(F32).reshape(1, wide)
    one_v = (lane >= HEAD_DIM).astype(F32).reshape(1, wide)
    return pl.pallas_call(
        _fx_prep_kernel,
        grid=(B, ns),
        in_specs=[
            col(0), col(1), col(2),
            pl.BlockSpec((nh, tm), lambda b, s: (0, b * ns + s)),
            pl.BlockSpec((nh, 1), fixed),
            vec, vec,
            pl.BlockSpec((width, width), fixed),
            pl.BlockSpec((tm, tm), fixed),
            pl.BlockSpec((nh, wide), fixed),
            wvec, wvec, wvec,
        ],
        out_specs=[out, out, out],
        out_shape=[jax.ShapeDtypeStruct((T, wide), BF16)] * 3,
        scratch_shapes=[pltpu.VMEM((1, nh), F32)],
        compiler_params=_cparams(("parallel", "arbitrary")),
        name="fx_prep",
    )(p_fx, p_fx, p_fx, fzt, bf, qw, kw, segm, tril, place, one_q, one_k, one_v)


def _fox_attn_kernel(qi_ref, kj_ref, q_ref, k_ref, v_ref, og_ref, o_ref, m_ref, acc_ref):
    i = qi_ref[pl.program_id(2)]
    j = kj_ref[pl.program_id(2)]
    tq = q_ref.shape[0]
    tk = k_ref.shape[0]
    heads = range(ATTN_HEADS)
    grp = lambda ref, h: ref[:, h * PAIR:(h + 1) * PAIR]

    @pl.when(j == 0)
    def _():
        m_ref[...] = jnp.full_like(m_ref, NEG_BIG)
        acc_ref[...] = jnp.zeros_like(acc_ref)

    def step(masked):
        s = [lax.dot_general(grp(q_ref, h), grp(k_ref, h), (((1,), (1,)), ((), ())), preferred_element_type=F32)
             for h in heads]
        if masked:
            row = lax.broadcasted_iota(jnp.int32, (tq, tk), 0)
            col = lax.broadcasted_iota(jnp.int32, (tq, tk), 1)
            s = [jnp.where(col <= row, s[h], NEG_BIG) for h in heads]
        m_old = [m_ref[h] for h in heads]
        m_new = [jnp.maximum(m_old[h], jnp.max(s[h], axis=1, keepdims=True)) for h in heads]
        alpha = [jnp.exp2(m_old[h] - m_new[h]) for h in heads]
        pr = [jnp.exp2(s[h] - pltpu.repeat(m_new[h], tk // PAIR, axis=1)).astype(BF16) for h in heads]
        pv = [jnp.dot(pr[h], grp(v_ref, h), preferred_element_type=F32) for h in heads]
        for h in heads:
            acc_ref[h] = alpha[h] * acc_ref[h] + pv[h]
            m_ref[h] = m_new[h]

    @pl.when(j < i)
    def _():
        step(False)

    @pl.when(j == i)
    def _():
        step(True)
        lane = lax.broadcasted_iota(jnp.int32, (tq, PAIR), 1)
        outs = []
        for p in range(ATTN_HEADS // 2):
            a0 = acc_ref[2 * p]
            a1 = acc_ref[2 * p + 1]
            o0 = a0 * pltpu.roll(1.0 / a0, HEAD_DIM, axis=1)
            o1 = pltpu.roll(a1, HEAD_DIM, axis=1) * (1.0 / a1)
            outs.append(jnp.where(lane < HEAD_DIM, o0, o1))
        o = jnp.concatenate(outs, axis=1)
        o_ref[...] = (o * _sigmoid(og_ref[...])).astype(o_ref.dtype)


def _fox_attn(qa, ka, va, p_fx, B, S, width):
    T = qa.shape[0]
    tq, tk = ATTN_TQ, ATTN_TK
    assert tq == tk
    nq = S // tq
    nh = ATTN_HEADS
    ow = nh * HEAD_DIM
    ngroup = width // ow
    og_col0 = 3 * width // ow
    pairs = [(i, j) for i in range(nq) for j in range(i + 1)]
    qi = jnp.array([ij[0] for ij in pairs], jnp.int32)
    kj = jnp.array([ij[1] for ij in pairs], jnp.int32)
    kv = pl.BlockSpec((tk, nh * PAIR), lambda b, p, t, qi, kj: (b * nq + kj[t], p))
    grid_spec = pltpu.PrefetchScalarGridSpec(
        num_scalar_prefetch=2,
        grid=(B, ngroup, len(pairs)),
        in_specs=[
            pl.BlockSpec((tq, nh * PAIR), lambda b, p, t, qi, kj: (b * nq + qi[t], p)), kv, kv,
            pl.BlockSpec((tq, ow), lambda b, p, t, qi, kj: (b * nq + qi[t], og_col0 + p)),
        ],
        out_specs=pl.BlockSpec((tq, ow), lambda b, p, t, qi, kj: (b * nq + qi[t], p)),
        scratch_shapes=[pltpu.VMEM((nh, tq, PAIR), F32), pltpu.VMEM((nh, tq, PAIR), F32)],
    )
    return pl.pallas_call(
        _fox_attn_kernel,
        grid_spec=grid_spec,
        out_shape=jax.ShapeDtypeStruct((T, width), BF16),
        compiler_params=_cparams(("parallel", "parallel", "arbitrary")),
        name="fox_attn",
    )(qi, kj, qa, ka, va, p_fx)


def _out_ln_kernel(yrw_ref, yfx_ref, h_ref, wo1_ref, wo2_ref, lnw_ref, lnb_ref, rw_ref, rb_ref,
                   h1_ref, logit_ref, *, alpha):
    mix = (jnp.dot(yrw_ref[...], wo1_ref[...], preferred_element_type=F32)
           + jnp.dot(yfx_ref[...], wo2_ref[...], preferred_element_type=F32))
    h1 = _layer_norm(alpha * h_ref[...] + mix, lnw_ref[...], lnb_ref[...])
    h1_ref[...] = h1
    ne = rb_ref.shape[1]
    h_hi = h1.astype(BF16)
    h_lo = (h1 - h_hi.astype(F32)).astype(BF16)
    rw = rw_ref[...]
    first = jnp.dot(h_hi, rw, preferred_element_type=F32)
    second = jnp.dot(h_lo, rw[:, :ne], preferred_element_type=F32)
    logit_ref[...] = first[:, :ne] + first[:, ne:] + second + rb_ref[...]


def _out_ln(y_rw, y_fx, h0, wo1, wo2, lnw, lnb, rw, rb, alpha):
    T, D = h0.shape
    width = y_rw.shape[1]
    ne = rb.shape[1]
    tm = ROW_TILE
    row = lambda i: (i, 0)
    fixed = lambda i: (0, 0)
    rw_hi = rw.astype(BF16)
    rw_lo = (rw - rw_hi.astype(F32)).astype(BF16)
    rw = jnp.concatenate([rw_hi, rw_lo], axis=1)
    return pl.pallas_call(
        functools.partial(_out_ln_kernel, alpha=alpha),
        grid=(T // tm,),
        in_specs=[
            pl.BlockSpec((tm, width), row), pl.BlockSpec((tm, width), row), pl.BlockSpec((tm, D), row),
            pl.BlockSpec((width, D), fixed), pl.BlockSpec((width, D), fixed),
            pl.BlockSpec((1, D), fixed), pl.BlockSpec((1, D), fixed),
            pl.BlockSpec((D, 2 * ne), fixed), pl.BlockSpec((1, ne), fixed),
        ],
        out_specs=[pl.BlockSpec((tm, D), row), pl.BlockSpec((tm, ne), row)],
        out_shape=[jax.ShapeDtypeStruct((T, D), F32), jax.ShapeDtypeStruct((T, ne), F32)],
        compiler_params=_cparams(("parallel",)),
        name="out_ln",
    )(y_rw, y_fx, h0, wo1, wo2, lnw, lnb, rw, rb)


DEINT_COLS = 256


def _w1_split_kernel(w_ref, perm_ref, g_ref, l_ref):
    half = DEINT_COLS // 2
    for c in range(w_ref.shape[2] // DEINT_COLS):
        blk = w_ref[0, :, c * DEINT_COLS:(c + 1) * DEINT_COLS].astype(BF16)
        out = jnp.dot(blk, perm_ref[...], preferred_element_type=F32).astype(BF16)
        g_ref[0, :, c * half:(c + 1) * half] = out[:, :half]
        l_ref[0, :, c * half:(c + 1) * half] = out[:, half:]


def _w1_split(w1):
    E, D, F2 = w1.shape
    tr = 512
    half = DEINT_COLS // 2
    src = jnp.arange(DEINT_COLS)
    dst = jnp.where(src % 2 == 0, src // 2, half + src // 2)
    perm = (dst[:, None] == jnp.arange(DEINT_COLS)[None, :]).astype(BF16)
    out = pl.BlockSpec((1, tr, F2 // 2), lambda e, r: (e, r, 0))
    return pl.pallas_call(
        _w1_split_kernel,
        grid=(E, D // tr),
        in_specs=[pl.BlockSpec((1, tr, F2), lambda e, r: (e, r, 0)),
                  pl.BlockSpec((DEINT_COLS, DEINT_COLS), lambda e, r: (0, 0))],
        out_specs=[out, out],
        out_shape=[jax.ShapeDtypeStruct((E, D, F2 // 2), BF16)] * 2,
        compiler_params=_cparams(("parallel", "parallel")),
        name="w1_split",
    )(w1, perm)


def _moe_kernel(bexp_ref, nused_ref, x_ref, w1g_ref, w1l_ref, b1g_ref, b1l_ref, w2_ref, b2_ref, o_ref):
    i = pl.program_id(0)

    @pl.when(i < nused_ref[0])
    def _():
        x = x_ref[...].astype(BF16)
        x_glu = jnp.minimum(jnp.dot(x, w1g_ref[0], preferred_element_type=F32) + b1g_ref[0], SWIGLU_LIMIT)
        x_lin = jnp.clip(jnp.dot(x, w1l_ref[0], preferred_element_type=F32) + b1l_ref[0],
                         -SWIGLU_LIMIT, SWIGLU_LIMIT)
        act = x_glu * _sigmoid(SWIGLU_ALPHA * x_glu) * (x_lin + 1.0)
        o_ref[...] = jnp.dot(act.astype(BF16), w2_ref[0], preferred_element_type=F32) + b2_ref[0]

    @pl.when(i >= nused_ref[0])
    def _():
        o_ref[...] = jnp.zeros_like(o_ref)


def _moe_ffn(block_exp, n_used, xs, w1g, w1l, b1g, b1l, w2, b2):
    n_rows, D = xs.shape
    F = w1g.shape[2]
    tm = MOE_TILE
    n_blocks = n_rows // tm
    live = lambda i, be, nu: jnp.minimum(i, nu[0] - 1)
    wspec = lambda shape: pl.BlockSpec((1,) + shape, lambda i, be, nu: (be[live(i, be, nu)], 0, 0))
    grid_spec = pltpu.PrefetchScalarGridSpec(
        num_scalar_prefetch=2,
        grid=(n_blocks,),
        in_specs=[
            pl.BlockSpec((tm, D), lambda i, be, nu: (live(i, be, nu), 0)),
            wspec((D, F)), wspec((D, F)), wspec((1, F)), wspec((1, F)), wspec((F, D)), wspec((1, D)),
        ],
        out_specs=pl.BlockSpec((tm, D), lambda i, be, nu: (i, 0)),
    )
    return pl.pallas_call(
        _moe_kernel,
        grid_spec=grid_spec,
        out_shape=jax.ShapeDtypeStruct((n_rows, D), F32),
        compiler_params=_cparams(("arbitrary",)),
        name="moe_ffn",
    )(block_exp, n_used, xs, w1g, w1l, b1g, b1l, w2, b2)


def _combine_ln_kernel(h_ref, eo_ref, gate_ref, lnw_ref, lnb_ref, o_ref, *, alpha):
    gates = gate_ref[...]
    ffn = eo_ref[0] * gates[:, 0:1]
    for j in range(1, TOP_K):
        ffn = ffn + eo_ref[j] * gates[:, j:j + 1]
    o_ref[...] = _layer_norm(alpha * h_ref[...] + ffn, lnw_ref[...], lnb_ref[...])


def _combine_ln(h1, eo, gates, lnw, lnb, alpha):
    T, D = h1.shape
    tm = ROW_TILE
    return pl.pallas_call(
        functools.partial(_combine_ln_kernel, alpha=alpha),
        grid=(T // tm,),
        in_specs=[
            pl.BlockSpec((tm, D), lambda i: (i, 0)),
            pl.BlockSpec((TOP_K, tm, D), lambda i: (0, i, 0)),
            pl.BlockSpec((tm, TOP_K), lambda i: (i, 0)),
            pl.BlockSpec((1, D), lambda i: (0, 0)),
            pl.BlockSpec((1, D), lambda i: (0, 0)),
        ],
        out_specs=pl.BlockSpec((tm, D), lambda i: (i, 0)),
        out_shape=jax.ShapeDtypeStruct((T, D), F32),
        compiler_params=_cparams(("parallel",)),
        name="combine_ln",
    )(h1, eo, gates, lnw, lnb)


def _pad_to(x, n, axis):
    pad = [(0, 0)] * x.ndim
    pad[axis] = (0, n - x.shape[axis])
    return jnp.pad(x, pad)


def _block_diag_ones(width, value=1.0):
    idx = jnp.arange(width) // HEAD_DIM
    return jnp.where(idx[:, None] == idx[None, :], value, 0.0).astype(BF16)


def _route(logits, n_experts, tile):
    T = logits.shape[0]
    top_val, top_idx = lax.top_k(logits[:, :n_experts], TOP_K)
    gates = jax.nn.softmax(top_val, axis=-1)
    e_flat = top_idx.reshape(-1).astype(jnp.int32)
    n_assign = T * TOP_K
    n_rows = n_assign + n_experts * tile
    n_blocks = n_rows // tile
    assert n_experts * n_assign < 2 ** 31
    eids = jnp.arange(n_experts, dtype=jnp.int32)
    aids = jnp.arange(n_assign, dtype=jnp.int32)
    skeys = lax.sort(e_flat * n_assign + aids)
    order = skeys % n_assign
    e_sorted = skeys // n_assign
    counts = jnp.sum((e_flat[:, None] == eids[None, :]).astype(jnp.int32), axis=0)
    starts = jnp.cumsum(counts) - counts
    padded = (counts + tile - 1) // tile * tile
    pends = jnp.cumsum(padded)
    pstarts = pends - padded
    shift = pstarts - starts
    dest_sorted = aids + jnp.sum(jnp.where(e_sorted[:, None] == eids[None, :], shift[None, :], 0), axis=1)
    _, dest = lax.sort((order, dest_sorted), num_keys=1)
    block_start = jnp.arange(n_blocks, dtype=jnp.int32) * tile
    block_exp = jnp.minimum(jnp.sum((pends[None, :] <= block_start[:, None]).astype(jnp.int32), axis=1),
                            n_experts - 1).astype(jnp.int32)
    n_used = (pends[-1] // tile).astype(jnp.int32).reshape(1)
    of_block = lambda per_expert: jnp.sum(
        jnp.where(block_exp[:, None] == eids[None, :], per_expert[None, :], 0), axis=1)
    in_exp = block_start - of_block(pstarts)
    n_valid = of_block(counts) - in_exp
    within = jnp.arange(tile, dtype=jnp.int32)[None, :]
    src = jnp.clip((of_block(starts) + in_exp)[:, None] + within, 0, n_assign - 1)
    row_tok = jnp.where(within < n_valid[:, None], order[src.reshape(-1)].reshape(n_blocks, tile) // TOP_K, 0)
    return gates, dest.reshape(T, TOP_K), row_tok.reshape(-1).astype(jnp.int32), block_exp, n_used


def kernel(x, ln_in_w, ln_in_b, w_in, rw_mu, rw_w0, rw_w2, rw_a0, rw_a2, rw_g2, rw_k_k, rw_k_a, rw_r_k,
           rw_gn_w, rw_gn_b, fx_b_f, fx_q_norm, fx_k_norm, w_o, ln1_w, ln1_b, router_w, router_b,
           exp_w1, exp_b1, exp_w2, exp_b2, ln2_w, ln2_b):
    B, S, D = x.shape
    T = B * S
    depth = w_in.shape[0]
    alpha = (2 * depth) ** 0.25
    rw_w = rw_w0.shape[1]
    fx_heads = fx_b_f.shape[1]
    fx_w = fx_heads * HEAD_DIM
    d_lora, a_lora, g_lora = rw_w2.shape[1], rw_a2.shape[1], rw_g2.shape[1]
    n_lora = d_lora + a_lora + g_lora
    lora_pad = -(-n_lora // 128) * 128
    rw_cols = 3 * rw_w + n_lora
    n_rw = 3 * rw_w + lora_pad
    n_experts = router_w.shape[2]
    ne_pad = -(-n_experts // 128) * 128
    row = lambda a: a.reshape(1, -1)

    seg_rw = _block_diag_ones(rw_w)
    segm_rw = _block_diag_ones(rw_w, 1.0 / HEAD_DIM)
    segm_fx = _block_diag_ones(fx_w, 1.0 / HEAD_DIM)
    tidx = jnp.arange(ROW_TILE)
    tril = (tidx[:, None] >= tidx[None, :]).astype(F32)
    cidx = jnp.arange(WKV_CHUNK)
    tri_c = (cidx[:, None] >= cidx[None, :]).astype(F32)

    assert depth == 1, "single-layer block"
    l = 0
    x2 = x.reshape(T, D)
    w_l = w_in[l]
    w_main = jnp.concatenate(
        [_pad_to(w_l[:, :rw_cols], n_rw, 1), w_l[:, rw_cols:rw_cols + 4 * fx_w]], axis=1).astype(BF16)
    wfz_t = w_l[:, rw_cols + 4 * fx_w:].T
    h0, p_rw, p_fx, fzt = _ln_proj(x2, row(ln_in_w), row(ln_in_b), w_main, wfz_t, n_rw)

    mu = _pad_to(row(rw_mu[l]), n_rw, 1)
    w2p = _pad_to(rw_w2[l], lora_pad, 0).astype(BF16)
    a2p = _pad_to(jnp.pad(rw_a2[l], ((d_lora, 0), (0, 0))), lora_pad, 0).astype(BF16)
    g2p = _pad_to(jnp.pad(rw_g2[l], ((d_lora + a_lora, 0), (0, 0))), lora_pad, 0).astype(BF16)
    r, lw, k2, v, kk, alr, g, bonus = _rw_prep(
        p_rw, B, S, rw_w, mu, row(rw_w0[l]), w2p, row(rw_a0[l]), a2p, g2p,
        row(rw_k_k[l]), row(rw_k_a[l]), row(rw_r_k[l]), seg_rw)
    y_rw = _wkv(r, lw, k2, v, kk, alr, g, bonus, row(rw_gn_w[l]), row(rw_gn_b[l]), tri_c, segm_rw, B, S)

    qw = row(jnp.tile(fx_q_norm[l], fx_heads))
    kw = row(jnp.tile(fx_k_norm[l], fx_heads))
    qa, ka, va = _fx_prep(p_fx, fzt, fx_b_f[l].reshape(-1, 1), qw, kw, segm_fx, tril, B, S, fx_w)
    y_fx = _fox_attn(qa, ka, va, p_fx, B, S, fx_w)

    wo = w_o[l].astype(BF16)
    rw_pad = _pad_to(router_w[l], ne_pad, 1)
    rb_pad = _pad_to(row(router_b[l]), ne_pad, 1)
    h1, logits = _out_ln(y_rw, y_fx, h0, wo[:rw_w], wo[rw_w:], row(ln1_w[l]), row(ln1_b[l]),
                         rw_pad, rb_pad, alpha)

    gates, pos, row_tok, block_exp, n_used = _route(logits, n_experts, MOE_TILE)
    xs = h1[row_tok]
    w1g, w1l = _w1_split(exp_w1[l])
    b1 = exp_b1[l]
    b1g = b1[:, None, 0::2]
    b1l = b1[:, None, 1::2]
    eo_rows = _moe_ffn(block_exp, n_used, xs, w1g, w1l, b1g, b1l, exp_w2[l].astype(BF16), exp_b2[l][:, None, :])
    eo = eo_rows[pos.T]
    h = _combine_ln(h1, eo, gates, row(ln2_w[l]), row(ln2_b[l]), alpha)
    return h.reshape(B, S, D)
```

```python
import functools

import jax
import jax.numpy as jnp
from jax import lax
from jax.experimental import pallas as pl
from jax.experimental.pallas import tpu as pltpu

F32 = jnp.float32
BF16 = jnp.bfloat16
HIGHEST = lax.Precision.HIGHEST

HEAD_DIM = 64
PAIR = 2 * HEAD_DIM
WKV_CHUNK = 64
RW_GN_EPS = 64e-5
QK_EPS = 1e-6
LN_EPS = 1e-5
TOP_K = 4
SWIGLU_ALPHA = 1.702
SWIGLU_LIMIT = 7.0
NEG_BIG = -1e30
LOG2E = 1.4426950408889634

ROW_TILE = 256
ATTN_TQ = 512
ATTN_TK = 512
ATTN_HEADS = 4
MOE_TILE = 256
WKV_BATCH = 4
VMEM_LIMIT = 48 * 1024 * 1024


def _cparams(sem):
    return pltpu.CompilerParams(dimension_semantics=sem, vmem_limit_bytes=VMEM_LIMIT)


def _dot(a, b):
    return jnp.dot(a.astype(BF16), b.astype(BF16), preferred_element_type=F32)


def _dot_t(a, b):
    return lax.dot_general(a.astype(BF16), b.astype(BF16), (((1,), (1,)), ((), ())),
                           preferred_element_type=F32)


def _segsum(x, seg):
    hi = x.astype(BF16)
    lo = (x - hi.astype(F32)).astype(BF16)
    return (jnp.dot(hi, seg, preferred_element_type=F32) + jnp.dot(lo, seg, preferred_element_type=F32))


def _sigmoid(x):
    return 1.0 / (1.0 + jnp.exp(-x))


def _softplus(x):
    return jnp.maximum(x, 0.0) + jnp.log(1.0 + jnp.exp(-jnp.abs(x)))


def _layer_norm(x, w, b):
    mu = jnp.mean(x, axis=-1, keepdims=True)
    xc = x - mu
    var = jnp.mean(xc * xc, axis=-1, keepdims=True)
    return xc * lax.rsqrt(var + LN_EPS) * w + b


def _ln_proj_kernel(x_ref, lnw_ref, lnb_ref, w_ref, wfz_ref, h_ref, prw_ref, pfx_ref, fzt_ref, *, n_rw):
    h = _layer_norm(x_ref[...], lnw_ref[...], lnb_ref[...])
    h_ref[...] = h
    p = jnp.dot(h.astype(BF16), w_ref[...], preferred_element_type=F32)
    prw_ref[...] = p[:, :n_rw]
    pfx_ref[...] = p[:, n_rw:]
    fzt_ref[...] = lax.dot_general(wfz_ref[...], h, (((1,), (1,)), ((), ())),
                                   precision=HIGHEST, preferred_element_type=F32)


def _ln_proj(x2, lnw, lnb, w_main, wfz_t, n_rw):
    T, D = x2.shape
    n_all = w_main.shape[1]
    n_fx = n_all - n_rw
    nh = wfz_t.shape[0]
    tm = ROW_TILE
    return pl.pallas_call(
        functools.partial(_ln_proj_kernel, n_rw=n_rw),
        grid=(T // tm,),
        in_specs=[
            pl.BlockSpec((tm, D), lambda i: (i, 0)),
            pl.BlockSpec((1, D), lambda i: (0, 0)),
            pl.BlockSpec((1, D), lambda i: (0, 0)),
            pl.BlockSpec((D, n_all), lambda i: (0, 0)),
            pl.BlockSpec((nh, D), lambda i: (0, 0)),
        ],
        out_specs=[
            pl.BlockSpec((tm, D), lambda i: (i, 0)),
            pl.BlockSpec((tm, n_rw), lambda i: (i, 0)),
            pl.BlockSpec((tm, n_fx), lambda i: (i, 0)),
            pl.BlockSpec((nh, tm), lambda i: (0, i)),
        ],
        out_shape=[
            jax.ShapeDtypeStruct((T, D), F32),
            jax.ShapeDtypeStruct((T, n_rw), F32),
            jax.ShapeDtypeStruct((T, n_fx), F32),
            jax.ShapeDtypeStruct((nh, T), F32),
        ],
        compiler_params=_cparams(("parallel",)),
        name="ln_proj",
    )(x2, lnw, lnb, w_main, wfz_t)


def _rw_prep_kernel(p_ref, mu_ref, w0_ref, w2_ref, a0_ref, a2_ref, g2_ref, kkw_ref, ka_ref, rk_ref, seg_ref,
                    r_ref, lw_ref, k_ref, v_ref, kk_ref, alr_ref, g_ref, bonus_ref, carry_ref, *, width):
    @pl.when(pl.program_id(1) == 0)
    def _():
        carry_ref[...] = jnp.zeros_like(carry_ref)

    p = p_ref[...]
    tm = p.shape[0]
    prev = pltpu.roll(p, 1, axis=0)
    first_row = lax.broadcasted_iota(jnp.int32, p.shape, 0) == 0
    prev = jnp.where(first_row, carry_ref[...], prev)
    carry_ref[...] = p[tm - 1:tm, :]
    ps = p + mu_ref[...] * (prev - p)

    r = ps[:, 0:width]
    k = ps[:, width:2 * width]
    v = ps[:, 2 * width:3 * width]
    lora = ps[:, 3 * width:]
    seg = seg_ref[...]

    wl = w0_ref[...] + _dot(jnp.tanh(lora), w2_ref[...])
    w_raw = -_softplus(-wl) - 0.5
    lw_ref[...] = -jnp.exp(w_raw)
    alr = _sigmoid(a0_ref[...] + _dot(lora, a2_ref[...]))
    g_ref[...] = _dot(_sigmoid(lora), g2_ref[...])
    kkp = k * kkw_ref[...]
    nrm = jnp.sqrt(_segsum(kkp * kkp, seg))
    kk_ref[...] = kkp / jnp.maximum(nrm, 1e-12)
    k2 = k * (1.0 + (alr - 1.0) * ka_ref[...])
    bonus_ref[...] = _segsum(r * k2 * rk_ref[...], seg) * v
    r_ref[...] = r
    k_ref[...] = k2
    v_ref[...] = v
    alr_ref[...] = alr


def _rw_prep(p_rw, B, S, width, mu, w0, w2p, a0, a2p, g2p, kkw, ka, rk, seg):
    T, n_rw = p_rw.shape
    tm = ROW_TILE
    ns = S // tm
    n_lora = n_rw - 3 * width
    row = lambda b, s: (b * ns + s, 0)
    fixed = lambda b, s: (0, 0)
    vec = pl.BlockSpec((1, width), fixed)
    out = pl.BlockSpec((tm, width), row)
    return pl.pallas_call(
        functools.partial(_rw_prep_kernel, width=width),
        grid=(B, ns),
        in_specs=[
            pl.BlockSpec((tm, n_rw), row),
            pl.BlockSpec((1, n_rw), fixed),
            vec, pl.BlockSpec((n_lora, width), fixed),
            vec, pl.BlockSpec((n_lora, width), fixed),
            pl.BlockSpec((n_lora, width), fixed),
            vec, vec, vec,
            pl.BlockSpec((width, width), fixed),
        ],
        out_specs=[out] * 8,
        out_shape=[jax.ShapeDtypeStruct((T, width), F32)] * 8,
        scratch_shapes=[pltpu.VMEM((1, n_rw), F32)],
        compiler_params=_cparams(("parallel", "arbitrary")),
        name="rw_prep",
    )(p_rw, mu, w0, w2p, a0, a2p, g2p, kkw, ka, rk, seg)


def _stack_heads(x):
    lane = lax.broadcasted_iota(jnp.int32, x.shape, 1)
    return jnp.concatenate([jnp.where(lane < HEAD_DIM, x, 0.0), jnp.where(lane >= HEAD_DIM, x, 0.0)], axis=0)


def _wkv_kernel(r_ref, lw_ref, k_ref, v_ref, kk_ref, alr_ref, g_ref, bonus_ref, gnw_ref, gnb_ref, tri_ref,
                segm_ref, y_ref, state_ref):
    C = WKV_CHUNK
    nb, _, width = lw_ref.shape
    npair = width // PAIR

    @pl.when(pl.program_id(1) == 0)
    def _():
        state_ref[...] = jnp.zeros_like(state_ref)

    ri = lax.broadcasted_iota(jnp.int32, (2 * C, 2 * C), 0)
    ci = lax.broadcasted_iota(jnp.int32, (2 * C, 2 * C), 1)
    same = (ri // C) == (ci // C)
    strict = same & ((ci % C) < (ri % C))
    incl = same & ((ci % C) <= (ri % C))
    eye = (ri == ci).astype(F32)

    lhs, rhs, a2s, r2s, v2s, bhts, bkts, ptots = [], [], [], [], [], [], [], []
    for bi in range(nb):
        lw = lw_ref[bi]
        cum = jnp.dot(tri_ref[...], lw, precision=HIGHEST, preferred_element_type=F32)
        total = cum[C - 1:C, :]
        p_inv = jnp.exp(-cum)
        p_rem = jnp.exp(total - cum)
        p_tot = jnp.exp(total)
        kk = kk_ref[bi]
        k2 = k_ref[bi]
        b = kk * alr_ref[bi]
        a_t = -kk * jnp.exp(cum - lw)
        r_t = r_ref[bi] * jnp.exp(cum)
        b_t = b * p_inv
        k_t = k2 * p_inv
        b_h = b * p_rem
        k_h = k2 * p_rem
        v = v_ref[bi]
        for p in range(npair):
            sl = slice(p * PAIR, (p + 1) * PAIR)
            a2, r2, b2, kt2 = (_stack_heads(t[:, sl]) for t in (a_t, r_t, b_t, k_t))
            bh2, kh2, v2 = (_stack_heads(t[:, sl]) for t in (b_h, k_h, v))
            lhs.append(jnp.concatenate([a2, r2], axis=0))
            rhs.append(jnp.concatenate([b2, kt2], axis=0))
            a2s.append(a2)
            r2s.append(r2)
            v2s.append(v2)
            bhts.append(bh2.T)
            bkts.append(jnp.concatenate([bh2.T, kh2.T], axis=1))
            ptots.append(p_tot[:, sl])

    chains = range(nb * npair)
    m = [_dot_t(lhs[c], rhs[c]) for c in chains]
    n_ab = [jnp.where(strict, m[c][:2 * C, :2 * C], 0.0) for c in chains]
    m_ak = [jnp.where(strict, m[c][:2 * C, 2 * C:], 0.0) for c in chains]
    m_rb = [jnp.where(incl, m[c][2 * C:, :2 * C], 0.0) for c in chains]
    m_rk = [jnp.where(incl, m[c][2 * C:, 2 * C:], 0.0) for c in chains]
    mv = [_dot(m_ak[c], v2s[c]) for c in chains]
    mrkv = [_dot(m_rk[c], v2s[c]) for c in chains]
    inv = [eye + n_ab[c] for c in chains]
    pw = n_ab
    for _ in range(C.bit_length() - 2):
        pw = [_dot(pw[c], pw[c]) for c in chains]
        inv = [inv[c] + _dot(inv[c], pw[c]) for c in chains]
    wu = [_dot(inv[c], jnp.concatenate([a2s[c], mv[c]], axis=1)) for c in chains]
    qy = [_dot(m_rb[c], wu[c]) + jnp.concatenate([r2s[c], mrkv[c]], axis=1) for c in chains]
    g_t = [_dot(bhts[c], wu[c][:, :PAIR]) + eye * ptots[c] for c in chains]
    h_t = [_dot(bkts[c], jnp.concatenate([wu[c][:, PAIR:], v2s[c]], axis=0)) for c in chains]
    s0 = [state_ref[c] for c in chains]
    y2 = [_dot(qy[c][:, :PAIR], s0[c]) + qy[c][:, PAIR:] for c in chains]
    for c in chains:
        state_ref[c] = _dot(g_t[c], s0[c]) + h_t[c]

    segm = segm_ref[...]
    for bi in range(nb):
        y = jnp.concatenate([y2[bi * npair + p][:C] + y2[bi * npair + p][C:] for p in range(npair)], axis=1)
        mean = _segsum(y, segm)
        yc = y - mean
        var = _segsum(yc * yc, segm)
        yn = yc * lax.rsqrt(var + RW_GN_EPS) * gnw_ref[...] + gnb_ref[...]
        y_ref[bi] = ((yn + bonus_ref[bi]) * g_ref[bi]).astype(y_ref.dtype)


def _wkv(r, lw, k2, v, kk, alr, g, bonus, gnw, gnb, tri, segm, B, S):
    T, width = r.shape
    C = WKV_CHUNK
    nb = WKV_BATCH
    nc = S // C
    fixed = lambda b, c: (0, 0)
    blk = pl.BlockSpec((nb, C, width), lambda b, c: (b, c, 0))
    vec = pl.BlockSpec((1, width), fixed)
    ins = [t.reshape(B, S, width) for t in (r, lw, k2, v, kk, alr, g, bonus)]
    y = pl.pallas_call(
        _wkv_kernel,
        grid=(B // nb, nc),
        in_specs=[blk] * 8 + [vec, vec, pl.BlockSpec((C, C), fixed), pl.BlockSpec((width, width), fixed)],
        out_specs=blk,
        out_shape=jax.ShapeDtypeStruct((B, S, width), BF16),
        scratch_shapes=[pltpu.VMEM((nb * width // PAIR, PAIR, PAIR), F32)],
        compiler_params=_cparams(("parallel", "arbitrary")),
        name="wkv",
    )(*ins, gnw, gnb, tri, segm)
    return y.reshape(T, width)


def _split3(x):
    hi = x.astype(BF16).astype(F32)
    mid = (x - hi).astype(BF16).astype(F32)
    return hi, mid, x - hi - mid


def _spread_heads(x):
    lane = lax.broadcasted_iota(jnp.int32, (x.shape[0], PAIR), 1)
    groups = []
    for p in range(x.shape[1] // PAIR):
        blk = x[:, p * PAIR:(p + 1) * PAIR]
        groups.append(jnp.where(lane < HEAD_DIM, blk, 0.0))
        groups.append(jnp.where(lane < HEAD_DIM, pltpu.roll(blk, HEAD_DIM, axis=1), 0.0))
    return jnp.concatenate(groups, axis=1)


def _fx_prep_kernel(q_ref, k_ref, v_ref, fzt_ref, bf_ref, qw_ref, kw_ref, segm_ref, tril_ref,
                    place_ref, oneq_ref, onek_ref, onev_ref, qa_ref, ka_ref, va_ref, carry_ref):
    @pl.when(pl.program_id(1) == 0)
    def _():
        carry_ref[...] = jnp.zeros_like(carry_ref)

    segm = segm_ref[...]
    q = q_ref[...]
    k = k_ref[...]
    tm = q.shape[0]
    qn = q * lax.rsqrt(_segsum(q * q, segm) + QK_EPS) * (qw_ref[...] * (HEAD_DIM ** -0.5 * LOG2E))
    kn = k * lax.rsqrt(_segsum(k * k, segm) + QK_EPS) * kw_ref[...]

    lf = -_softplus(-(fzt_ref[...] + bf_ref[...]))
    c = lax.dot_general(tril_ref[...], lf, (((1,), (1,)), ((), ())), precision=HIGHEST,
                        preferred_element_type=F32) + carry_ref[...]
    carry_ref[...] = c[tm - 1:tm, :]
    placed = jnp.dot(c * LOG2E, place_ref[...], precision=HIGHEST, preferred_element_type=F32)
    hi, mid, lo = _split3(placed)
    lane = lax.broadcasted_iota(jnp.int32, placed.shape, 1) % PAIR
    c_q = jnp.where(lane == HEAD_DIM + 3, hi, jnp.where(lane == HEAD_DIM + 4, mid,
                                                        jnp.where(lane == HEAD_DIM + 5, lo, 0.0)))
    c_k = jnp.where(lane == HEAD_DIM, hi, jnp.where(lane == HEAD_DIM + 1, mid,
                                                    jnp.where(lane == HEAD_DIM + 2, lo, 0.0)))
    qa_ref[...] = (_spread_heads(qn) + c_q + oneq_ref[...]).astype(qa_ref.dtype)
    ka_ref[...] = (_spread_heads(kn) - c_k + onek_ref[...]).astype(ka_ref.dtype)
    va_ref[...] = (_spread_heads(v_ref[...]) + onev_ref[...]).astype(va_ref.dtype)


def _fx_prep(p_fx, fzt, bf, qw, kw, segm, tril, B, S, width):
    T = p_fx.shape[0]
    nh = fzt.shape[0]
    wide = nh * PAIR
    tm = ROW_TILE
    ns = S // tm
    fixed = lambda b, s: (0, 0)
    col = lambda j: pl.BlockSpec((tm, width), lambda b, s: (b * ns + s, j))
    out = pl.BlockSpec((tm, wide), lambda b, s: (b * ns + s, 0))
    vec = pl.BlockSpec((1, width), fixed)
    wvec = pl.BlockSpec((1, wide), fixed)

    lane = jnp.arange(wide) % PAIR
    head = jnp.arange(wide) // PAIR
    place = ((head[None, :] == jnp.arange(nh)[:, None]) & (lane >= HEAD_DIM) & (lane < HEAD_DIM + 6)).astype(F32)
    one_q = ((lane >= HEAD_DIM) & (lane < HEAD_DIM + 3)).astype(F32).reshape(1, wide)
    one_k = ((lane >= HEAD_DIM + 3) & (lane < HEAD_DIM + 6)).astype(F32).reshape(1, wide)
    one_v = (lane >= HEAD_DIM).astype(F32).reshape(1, wide)
    return pl.pallas_call(
        _fx_prep_kernel,
        grid=(B, ns),
        in_specs=[
            col(0), col(1), col(2),
            pl.BlockSpec((nh, tm), lambda b, s: (0, b * ns + s)),
            pl.BlockSpec((nh, 1), fixed),
            vec, vec,
            pl.BlockSpec((width, width), fixed),
            pl.BlockSpec((tm, tm), fixed),
            pl.BlockSpec((nh, wide), fixed),
            wvec, wvec, wvec,
        ],
        out_specs=[out, out, out],
        out_shape=[jax.ShapeDtypeStruct((T, wide), BF16)] * 3,
        scratch_shapes=[pltpu.VMEM((1, nh), F32)],
        compiler_params=_cparams(("parallel", "arbitrary")),
        name="fx_prep",
    )(p_fx, p_fx, p_fx, fzt, bf, qw, kw, segm, tril, place, one_q, one_k, one_v)


def _fox_attn_kernel(qi_ref, kj_ref, q_ref, k_ref, v_ref, og_ref, o_ref, m_ref, acc_ref):
    i = qi_ref[pl.program_id(2)]
    j = kj_ref[pl.program_id(2)]
    tq = q_ref.shape[0]
    tk = k_ref.shape[0]
    heads = range(ATTN_HEADS)
    grp = lambda ref, h: ref[:, h * PAIR:(h + 1) * PAIR]

    @pl.when(j == 0)
    def _():
        m_ref[...] = jnp.full_like(m_ref, NEG_BIG)
        acc_ref[...] = jnp.zeros_like(acc_ref)

    def step(masked):
        s = [lax.dot_general(grp(q_ref, h), grp(k_ref, h), (((1,), (1,)), ((), ())), preferred_element_type=F32)
             for h in heads]
        if masked:
            row = lax.broadcasted_iota(jnp.int32, (tq, tk), 0)
            col = lax.broadcasted_iota(jnp.int32, (tq, tk), 1)
            s = [jnp.where(col <= row, s[h], NEG_BIG) for h in heads]
        m_old = [m_ref[h] for h in heads]
        m_new = [jnp.maximum(m_old[h], jnp.max(s[h], axis=1, keepdims=True)) for h in heads]
        alpha = [jnp.exp2(m_old[h] - m_new[h]) for h in heads]
        pr = [jnp.exp2(s[h] - pltpu.repeat(m_new[h], tk // PAIR, axis=1)).astype(BF16) for h in heads]
        pv = [jnp.dot(pr[h], grp(v_ref, h), preferred_element_type=F32) for h in heads]
        for h in heads:
            acc_ref[h] = alpha[h] * acc_ref[h] + pv[h]
            m_ref[h] = m_new[h]

    @pl.when(j < i)
    def _():
        step(False)

    @pl.when(j == i)
    def _():
        step(True)
        lane = lax.broadcasted_iota(jnp.int32, (tq, PAIR), 1)
        outs = []
        for p in range(ATTN_HEADS // 2):
            a0 = acc_ref[2 * p]
            a1 = acc_ref[2 * p + 1]
            o0 = a0 * pltpu.roll(1.0 / a0, HEAD_DIM, axis=1)
            o1 = pltpu.roll(a1, HEAD_DIM, axis=1) * (1.0 / a1)
            outs.append(jnp.where(lane < HEAD_DIM, o0, o1))
        o = jnp.concatenate(outs, axis=1)
        o_ref[...] = (o * _sigmoid(og_ref[...])).astype(o_ref.dtype)


def _fox_attn(qa, ka, va, p_fx, B, S, width):
    T = qa.shape[0]
    tq, tk = ATTN_TQ, ATTN_TK
    assert tq == tk
    nq = S // tq
    nh = ATTN_HEADS
    ow = nh * HEAD_DIM
    ngroup = width // ow
    og_col0 = 3 * width // ow
    pairs = [(i, j) for i in range(nq) for j in range(i + 1)]
    qi = jnp.array([ij[0] for ij in pairs], jnp.int32)
    kj = jnp.array([ij[1] for ij in pairs], jnp.int32)
    kv = pl.BlockSpec((tk, nh * PAIR), lambda b, p, t, qi, kj: (b * nq + kj[t], p))
    grid_spec = pltpu.PrefetchScalarGridSpec(
        num_scalar_prefetch=2,
        grid=(B, ngroup, len(pairs)),
        in_specs=[
            pl.BlockSpec((tq, nh * PAIR), lambda b, p, t, qi, kj: (b * nq + qi[t], p)), kv, kv,
            pl.BlockSpec((tq, ow), lambda b, p, t, qi, kj: (b * nq + qi[t], og_col0 + p)),
        ],
        out_specs=pl.BlockSpec((tq, ow), lambda b, p, t, qi, kj: (b * nq + qi[t], p)),
        scratch_shapes=[pltpu.VMEM((nh, tq, PAIR), F32), pltpu.VMEM((nh, tq, PAIR), F32)],
    )
    return pl.pallas_call(
        _fox_attn_kernel,
        grid_spec=grid_spec,
        out_shape=jax.ShapeDtypeStruct((T, width), BF16),
        compiler_params=_cparams(("parallel", "parallel", "arbitrary")),
        name="fox_attn",
    )(qi, kj, qa, ka, va, p_fx)


def _out_ln_kernel(yrw_ref, yfx_ref, h_ref, wo1_ref, wo2_ref, lnw_ref, lnb_ref, rw_ref, rb_ref,
                   h1_ref, logit_ref, *, alpha):
    mix = (jnp.dot(yrw_ref[...], wo1_ref[...], preferred_element_type=F32)
           + jnp.dot(yfx_ref[...], wo2_ref[...], preferred_element_type=F32))
    h1 = _layer_norm(alpha * h_ref[...] + mix, lnw_ref[...], lnb_ref[...])
    h1_ref[...] = h1
    ne = rb_ref.shape[1]
    h_hi = h1.astype(BF16)
    h_lo = (h1 - h_hi.astype(F32)).astype(BF16)
    rw = rw_ref[...]
    first = jnp.dot(h_hi, rw, preferred_element_type=F32)
    second = jnp.dot(h_lo, rw[:, :ne], preferred_element_type=F32)
    logit_ref[...] = first[:, :ne] + first[:, ne:] + second + rb_ref[...]


def _out_ln(y_rw, y_fx, h0, wo1, wo2, lnw, lnb, rw, rb, alpha):
    T, D = h0.shape
    width = y_rw.shape[1]
    ne = rb.shape[1]
    tm = ROW_TILE
    row = lambda i: (i, 0)
    fixed = lambda i: (0, 0)
    rw_hi = rw.astype(BF16)
    rw_lo = (rw - rw_hi.astype(F32)).astype(BF16)
    rw = jnp.concatenate([rw_hi, rw_lo], axis=1)
    return pl.pallas_call(
        functools.partial(_out_ln_kernel, alpha=alpha),
        grid=(T // tm,),
        in_specs=[
            pl.BlockSpec((tm, width), row), pl.BlockSpec((tm, width), row), pl.BlockSpec((tm, D), row),
            pl.BlockSpec((width, D), fixed), pl.BlockSpec((width, D), fixed),
            pl.BlockSpec((1, D), fixed), pl.BlockSpec((1, D), fixed),
            pl.BlockSpec((D, 2 * ne), fixed), pl.BlockSpec((1, ne), fixed),
        ],
        out_specs=[pl.BlockSpec((tm, D), row), pl.BlockSpec((tm, ne), row)],
        out_shape=[jax.ShapeDtypeStruct((T, D), F32), jax.ShapeDtypeStruct((T, ne), F32)],
        compiler_params=_cparams(("parallel",)),
        name="out_ln",
    )(y_rw, y_fx, h0, wo1, wo2, lnw, lnb, rw, rb)


DEINT_COLS = 256


def _w1_split_kernel(w_ref, perm_ref, g_ref, l_ref):
    half = DEINT_COLS // 2
    for c in range(w_ref.shape[2] // DEINT_COLS):
        blk = w_ref[0, :, c * DEINT_COLS:(c + 1) * DEINT_COLS].astype(BF16)
        out = jnp.dot(blk, perm_ref[...], preferred_element_type=F32).astype(BF16)
        g_ref[0, :, c * half:(c + 1) * half] = out[:, :half]
        l_ref[0, :, c * half:(c + 1) * half] = out[:, half:]


def _w1_split(w1):
    E, D, F2 = w1.shape
    tr = 512
    half = DEINT_COLS // 2
    src = jnp.arange(DEINT_COLS)
    dst = jnp.where(src % 2 == 0, src // 2, half + src // 2)
    perm = (dst[:, None] == jnp.arange(DEINT_COLS)[None, :]).astype(BF16)
    out = pl.BlockSpec((1, tr, F2 // 2), lambda e, r: (e, r, 0))
    return pl.pallas_call(
        _w1_split_kernel,
        grid=(E, D // tr),
        in_specs=[pl.BlockSpec((1, tr, F2), lambda e, r: (e, r, 0)),
                  pl.BlockSpec((DEINT_COLS, DEINT_COLS), lambda e, r: (0, 0))],
        out_specs=[out, out],
        out_shape=[jax.ShapeDtypeStruct((E, D, F2 // 2), BF16)] * 2,
        compiler_params=_cparams(("parallel", "parallel")),
        name="w1_split",
    )(w1, perm)


def _pack_bf16_pairs(x):
    n = x.shape[1] // 2
    bits = pltpu.bitcast(x.astype(BF16).astype(F32), jnp.uint32)
    return lax.shift_right_logical(bits[:, :n], jnp.uint32(16)) | bits[:, n:]


def _unpack_bf16_pairs(u):
    lo = pltpu.bitcast(lax.shift_left(u, jnp.uint32(16)), F32)
    hi = pltpu.bitcast(u & jnp.uint32(0xFFFF0000), F32)
    return jnp.concatenate([lo, hi], axis=1)


def _moe_kernel(bexp_ref, nused_ref, x_ref, w1g_ref, w1l_ref, b1g_ref, b1l_ref, w2_ref, b2_ref, o_ref):
    i = pl.program_id(0)

    @pl.when(i < nused_ref[0])
    def _():
        x = x_ref[...].astype(BF16)
        x_glu = jnp.minimum(jnp.dot(x, w1g_ref[0], preferred_element_type=F32) + b1g_ref[0], SWIGLU_LIMIT)
        x_lin = jnp.clip(jnp.dot(x, w1l_ref[0], preferred_element_type=F32) + b1l_ref[0],
                         -SWIGLU_LIMIT, SWIGLU_LIMIT)
        act = x_glu * _sigmoid(SWIGLU_ALPHA * x_glu) * (x_lin + 1.0)
        o = jnp.dot(act.astype(BF16), w2_ref[0], preferred_element_type=F32) + b2_ref[0]
        o_ref[...] = _pack_bf16_pairs(o)

    @pl.when(i >= nused_ref[0])
    def _():
        o_ref[...] = jnp.zeros_like(o_ref)


def _moe_ffn(block_exp, n_used, xs, w1g, w1l, b1g, b1l, w2, b2):
    n_rows, D = xs.shape
    F = w1g.shape[2]
    tm = MOE_TILE
    n_blocks = n_rows // tm
    live = lambda i, be, nu: jnp.minimum(i, nu[0] - 1)
    wspec = lambda shape: pl.BlockSpec((1,) + shape, lambda i, be, nu: (be[live(i, be, nu)], 0, 0))
    grid_spec = pltpu.PrefetchScalarGridSpec(
        num_scalar_prefetch=2,
        grid=(n_blocks,),
        in_specs=[
            pl.BlockSpec((tm, D), lambda i, be, nu: (live(i, be, nu), 0)),
            wspec((D, F)), wspec((D, F)), wspec((1, F)), wspec((1, F)), wspec((F, D)), wspec((1, D)),
        ],
        out_specs=pl.BlockSpec((tm, D // 2), lambda i, be, nu: (i, 0)),
    )
    return pl.pallas_call(
        _moe_kernel,
        grid_spec=grid_spec,
        out_shape=jax.ShapeDtypeStruct((n_rows, D // 2), jnp.uint32),
        compiler_params=_cparams(("arbitrary",)),
        name="moe_ffn",
    )(block_exp, n_used, xs, w1g, w1l, b1g, b1l, w2, b2)


def _combine_ln_kernel(h_ref, eo_ref, gate_ref, lnw_ref, lnb_ref, o_ref, *, alpha):
    gates = gate_ref[...]
    ffn = _unpack_bf16_pairs(eo_ref[0]) * gates[:, 0:1]
    for j in range(1, TOP_K):
        ffn = ffn + _unpack_bf16_pairs(eo_ref[j]) * gates[:, j:j + 1]
    o_ref[...] = _layer_norm(alpha * h_ref[...] + ffn, lnw_ref[...], lnb_ref[...])


def _combine_ln(h1, eo, gates, lnw, lnb, alpha):
    T, D = h1.shape
    tm = ROW_TILE
    return pl.pallas_call(
        functools.partial(_combine_ln_kernel, alpha=alpha),
        grid=(T // tm,),
        in_specs=[
            pl.BlockSpec((tm, D), lambda i: (i, 0)),
            pl.BlockSpec((TOP_K, tm, D // 2), lambda i: (0, i, 0)),
            pl.BlockSpec((tm, TOP_K), lambda i: (i, 0)),
            pl.BlockSpec((1, D), lambda i: (0, 0)),
            pl.BlockSpec((1, D), lambda i: (0, 0)),
        ],
        out_specs=pl.BlockSpec((tm, D), lambda i: (i, 0)),
        out_shape=jax.ShapeDtypeStruct((T, D), F32),
        compiler_params=_cparams(("parallel",)),
        name="combine_ln",
    )(h1, eo, gates, lnw, lnb)


def _pad_to(x, n, axis):
    pad = [(0, 0)] * x.ndim
    pad[axis] = (0, n - x.shape[axis])
    return jnp.pad(x, pad)


def _block_diag_ones(width, value=1.0):
    idx = jnp.arange(width) // HEAD_DIM
    return jnp.where(idx[:, None] == idx[None, :], value, 0.0).astype(BF16)


def _route(logits, n_experts, tile):
    T = logits.shape[0]
    top_val, top_idx = lax.top_k(logits[:, :n_experts], TOP_K)
    gates = jax.nn.softmax(top_val, axis=-1)
    e_flat = top_idx.reshape(-1).astype(jnp.int32)
    n_assign = T * TOP_K
    n_rows = n_assign + n_experts * tile
    n_blocks = n_rows // tile
    assert n_experts * n_assign < 2 ** 31
    eids = jnp.arange(n_experts, dtype=jnp.int32)
    aids = jnp.arange(n_assign, dtype=jnp.int32)
    skeys = lax.sort(e_flat * n_assign + aids)
    order = skeys % n_assign
    e_sorted = skeys // n_assign
    counts = jnp.sum((e_flat[:, None] == eids[None, :]).astype(jnp.int32), axis=0)
    starts = jnp.cumsum(counts) - counts
    padded = (counts + tile - 1) // tile * tile
    pends = jnp.cumsum(padded)
    pstarts = pends - padded
    shift = pstarts - starts
    dest_sorted = aids + jnp.sum(jnp.where(e_sorted[:, None] == eids[None, :], shift[None, :], 0), axis=1)
    _, dest = lax.sort((order, dest_sorted), num_keys=1)
    block_start = jnp.arange(n_blocks, dtype=jnp.int32) * tile
    block_exp = jnp.minimum(jnp.sum((pends[None, :] <= block_start[:, None]).astype(jnp.int32), axis=1),
                            n_experts - 1).astype(jnp.int32)
    n_used = (pends[-1] // tile).astype(jnp.int32).reshape(1)
    of_block = lambda per_expert: jnp.sum(
        jnp.where(block_exp[:, None] == eids[None, :], per_expert[None, :], 0), axis=1)
    in_exp = block_start - of_block(pstarts)
    n_valid = of_block(counts) - in_exp
    within = jnp.arange(tile, dtype=jnp.int32)[None, :]
    src = jnp.clip((of_block(starts) + in_exp)[:, None] + within, 0, n_assign - 1)
    row_tok = jnp.where(within < n_valid[:, None], order[src.reshape(-1)].reshape(n_blocks, tile) // TOP_K, 0)
    return gates, dest.reshape(T, TOP_K), row_tok.reshape(-1).astype(jnp.int32), block_exp, n_used


def kernel(x, ln_in_w, ln_in_b, w_in, rw_mu, rw_w0, rw_w2, rw_a0, rw_a2, rw_g2, rw_k_k, rw_k_a, rw_r_k,
           rw_gn_w, rw_gn_b, fx_b_f, fx_q_norm, fx_k_norm, w_o, ln1_w, ln1_b, router_w, router_b,
           exp_w1, exp_b1, exp_w2, exp_b2, ln2_w, ln2_b):
    B, S, D = x.shape
    T = B * S
    depth = w_in.shape[0]
    alpha = (2 * depth) ** 0.25
    rw_w = rw_w0.shape[1]
    fx_heads = fx_b_f.shape[1]
    fx_w = fx_heads * HEAD_DIM
    d_lora, a_lora, g_lora = rw_w2.shape[1], rw_a2.shape[1], rw_g2.shape[1]
    n_lora = d_lora + a_lora + g_lora
    lora_pad = -(-n_lora // 128) * 128
    rw_cols = 3 * rw_w + n_lora
    n_rw = 3 * rw_w + lora_pad
    n_experts = router_w.shape[2]
    ne_pad = -(-n_experts // 128) * 128
    row = lambda a: a.reshape(1, -1)

    seg_rw = _block_diag_ones(rw_w)
    segm_rw = _block_diag_ones(rw_w, 1.0 / HEAD_DIM)
    segm_fx = _block_diag_ones(fx_w, 1.0 / HEAD_DIM)
    tidx = jnp.arange(ROW_TILE)
    tril = (tidx[:, None] >= tidx[None, :]).astype(F32)
    cidx = jnp.arange(WKV_CHUNK)
    tri_c = (cidx[:, None] >= cidx[None, :]).astype(F32)

    assert depth == 1, "single-layer block"
    l = 0
    x2 = x.reshape(T, D)
    w_l = w_in[l]
    w_main = jnp.concatenate(
        [_pad_to(w_l[:, :rw_cols], n_rw, 1), w_l[:, rw_cols:rw_cols + 4 * fx_w]], axis=1).astype(BF16)
    wfz_t = w_l[:, rw_cols + 4 * fx_w:].T
    h0, p_rw, p_fx, fzt = _ln_proj(x2, row(ln_in_w), row(ln_in_b), w_main, wfz_t, n_rw)

    mu = _pad_to(row(rw_mu[l]), n_rw, 1)
    w2p = _pad_to(rw_w2[l], lora_pad, 0).astype(BF16)
    a2p = _pad_to(jnp.pad(rw_a2[l], ((d_lora, 0), (0, 0))), lora_pad, 0).astype(BF16)
    g2p = _pad_to(jnp.pad(rw_g2[l], ((d_lora + a_lora, 0), (0, 0))), lora_pad, 0).astype(BF16)
    r, lw, k2, v, kk, alr, g, bonus = _rw_prep(
        p_rw, B, S, rw_w, mu, row(rw_w0[l]), w2p, row(rw_a0[l]), a2p, g2p,
        row(rw_k_k[l]), row(rw_k_a[l]), row(rw_r_k[l]), seg_rw)
    y_rw = _wkv(r, lw, k2, v, kk, alr, g, bonus, row(rw_gn_w[l]), row(rw_gn_b[l]), tri_c, segm_rw, B, S)

    qw = row(jnp.tile(fx_q_norm[l], fx_heads))
    kw = row(jnp.tile(fx_k_norm[l], fx_heads))
    qa, ka, va = _fx_prep(p_fx, fzt, fx_b_f[l].reshape(-1, 1), qw, kw, segm_fx, tril, B, S, fx_w)
    y_fx = _fox_attn(qa, ka, va, p_fx, B, S, fx_w)

    wo = w_o[l].astype(BF16)
    rw_pad = _pad_to(router_w[l], ne_pad, 1)
    rb_pad = _pad_to(row(router_b[l]), ne_pad, 1)
    h1, logits = _out_ln(y_rw, y_fx, h0, wo[:rw_w], wo[rw_w:], row(ln1_w[l]), row(ln1_b[l]),
                         rw_pad, rb_pad, alpha)

    gates, pos, row_tok, block_exp, n_used = _route(logits, n_experts, MOE_TILE)
    xs = h1[row_tok]
    w1g, w1l = _w1_split(exp_w1[l])
    b1 = exp_b1[l]
    b1g = b1[:, None, 0::2]
    b1l = b1[:, None, 1::2]
    eo_rows = _moe_ffn(block_exp, n_used, xs, w1g, w1l, b1g, b1l, exp_w2[l].astype(BF16), exp_b2[l][:, None, :])
    eo = eo_rows[pos.T]
    h = _combine_ln(h1, eo, gates, row(ln2_w[l]), row(ln2_b[l]), alpha)
    return h.reshape(B, S, D)
```

```python
import functools

import jax
import jax.numpy as jnp
from jax import lax
from jax.experimental import pallas as pl
from jax.experimental.pallas import tpu as pltpu

F32 = jnp.float32
BF16 = jnp.bfloat16
HIGHEST = lax.Precision.HIGHEST

HEAD_DIM = 64
PAIR = 2 * HEAD_DIM
WKV_CHUNK = 64
RW_GN_EPS = 64e-5
QK_EPS = 1e-6
LN_EPS = 1e-5
TOP_K = 4
SWIGLU_ALPHA = 1.702
SWIGLU_LIMIT = 7.0
NEG_BIG = -1e30
LOG2E = 1.4426950408889634

ROW_TILE = 256
ATTN_TQ = 512
ATTN_TK = 512
ATTN_HEADS = 4
MOE_TILE = 256
WKV_BATCH = 4
VMEM_LIMIT = 48 * 1024 * 1024


def _cparams(sem):
    return pltpu.CompilerParams(dimension_semantics=sem, vmem_limit_bytes=VMEM_LIMIT)


def _dot(a, b):
    return jnp.dot(a.astype(BF16), b.astype(BF16), preferred_element_type=F32)


def _dot_t(a, b):
    return lax.dot_general(a.astype(BF16), b.astype(BF16), (((1,), (1,)), ((), ())),
                           preferred_element_type=F32)


def _segsum(x, seg):
    hi = x.astype(BF16)
    lo = (x - hi.astype(F32)).astype(BF16)
    return (jnp.dot(hi, seg, preferred_element_type=F32) + jnp.dot(lo, seg, preferred_element_type=F32))


def _sigmoid(x):
    return 1.0 / (1.0 + jnp.exp(-x))


def _softplus(x):
    return jnp.maximum(x, 0.0) + jnp.log(1.0 + jnp.exp(-jnp.abs(x)))


def _layer_norm(x, w, b):
    mu = jnp.mean(x, axis=-1, keepdims=True)
    xc = x - mu
    var = jnp.mean(xc * xc, axis=-1, keepdims=True)
    return xc * lax.rsqrt(var + LN_EPS) * w + b


def _ln_proj_kernel(x_ref, lnw_ref, lnb_ref, w_ref, wfz_ref, h_ref, prw_ref, pfx_ref, fzt_ref, *, n_rw):
    h = _layer_norm(x_ref[...], lnw_ref[...], lnb_ref[...])
    h_ref[...] = h
    p = jnp.dot(h.astype(BF16), w_ref[...], preferred_element_type=F32)
    prw_ref[...] = p[:, :n_rw]
    pfx_ref[...] = p[:, n_rw:]
    fzt_ref[...] = lax.dot_general(wfz_ref[...], h, (((1,), (1,)), ((), ())),
                                   precision=HIGHEST, preferred_element_type=F32)


def _ln_proj(x2, lnw, lnb, w_main, wfz_t, n_rw):
    T, D = x2.shape
    n_all = w_main.shape[1]
    n_fx = n_all - n_rw
    nh = wfz_t.shape[0]
    tm = ROW_TILE
    return pl.pallas_call(
        functools.partial(_ln_proj_kernel, n_rw=n_rw),
        grid=(T // tm,),
        in_specs=[
            pl.BlockSpec((tm, D), lambda i: (i, 0)),
            pl.BlockSpec((1, D), lambda i: (0, 0)),
            pl.BlockSpec((1, D), lambda i: (0, 0)),
            pl.BlockSpec((D, n_all), lambda i: (0, 0)),
            pl.BlockSpec((nh, D), lambda i: (0, 0)),
        ],
        out_specs=[
            pl.BlockSpec((tm, D), lambda i: (i, 0)),
            pl.BlockSpec((tm, n_rw), lambda i: (i, 0)),
            pl.BlockSpec((tm, n_fx), lambda i: (i, 0)),
            pl.BlockSpec((nh, tm), lambda i: (0, i)),
        ],
        out_shape=[
            jax.ShapeDtypeStruct((T, D), F32),
            jax.ShapeDtypeStruct((T, n_rw), F32),
            jax.ShapeDtypeStruct((T, n_fx), F32),
            jax.ShapeDtypeStruct((nh, T), F32),
        ],
        compiler_params=_cparams(("parallel",)),
        name="ln_proj",
    )(x2, lnw, lnb, w_main, wfz_t)


def _rw_prep_kernel(p_ref, mu_ref, w0_ref, w2_ref, a0_ref, a2_ref, g2_ref, kkw_ref, ka_ref, rk_ref, seg_ref,
                    r_ref, lw_ref, k_ref, v_ref, kk_ref, alr_ref, g_ref, bonus_ref, carry_ref, *, width):
    @pl.when(pl.program_id(1) == 0)
    def _():
        carry_ref[...] = jnp.zeros_like(carry_ref)

    p = p_ref[...]
    tm = p.shape[0]
    prev = pltpu.roll(p, 1, axis=0)
    first_row = lax.broadcasted_iota(jnp.int32, p.shape, 0) == 0
    prev = jnp.where(first_row, carry_ref[...], prev)
    carry_ref[...] = p[tm - 1:tm, :]
    ps = p + mu_ref[...] * (prev - p)

    r = ps[:, 0:width]
    k = ps[:, width:2 * width]
    v = ps[:, 2 * width:3 * width]
    lora = ps[:, 3 * width:]
    seg = seg_ref[...]

    wl = w0_ref[...] + _dot(jnp.tanh(lora), w2_ref[...])
    w_raw = -_softplus(-wl) - 0.5
    lw_ref[...] = -jnp.exp(w_raw)
    alr = _sigmoid(a0_ref[...] + _dot(lora, a2_ref[...]))
    g_ref[...] = _dot(_sigmoid(lora), g2_ref[...])
    kkp = k * kkw_ref[...]
    nrm = jnp.sqrt(_segsum(kkp * kkp, seg))
    kk_ref[...] = kkp / jnp.maximum(nrm, 1e-12)
    k2 = k * (1.0 + (alr - 1.0) * ka_ref[...])
    bonus_ref[...] = _segsum(r * k2 * rk_ref[...], seg) * v
    r_ref[...] = r
    k_ref[...] = k2
    v_ref[...] = v
    alr_ref[...] = alr


def _rw_prep(p_rw, B, S, width, mu, w0, w2p, a0, a2p, g2p, kkw, ka, rk, seg):
    T, n_rw = p_rw.shape
    tm = ROW_TILE
    ns = S // tm
    n_lora = n_rw - 3 * width
    row = lambda b, s: (b * ns + s, 0)
    fixed = lambda b, s: (0, 0)
    vec = pl.BlockSpec((1, width), fixed)
    out = pl.BlockSpec((tm, width), row)
    return pl.pallas_call(
        functools.partial(_rw_prep_kernel, width=width),
        grid=(B, ns),
        in_specs=[
            pl.BlockSpec((tm, n_rw), row),
            pl.BlockSpec((1, n_rw), fixed),
            vec, pl.BlockSpec((n_lora, width), fixed),
            vec, pl.BlockSpec((n_lora, width), fixed),
            pl.BlockSpec((n_lora, width), fixed),
            vec, vec, vec,
            pl.BlockSpec((width, width), fixed),
        ],
        out_specs=[out] * 8,
        out_shape=[jax.ShapeDtypeStruct((T, width), F32)] * 8,
        scratch_shapes=[pltpu.VMEM((1, n_rw), F32)],
        compiler_params=_cparams(("parallel", "arbitrary")),
        name="rw_prep",
    )(p_rw, mu, w0, w2p, a0, a2p, g2p, kkw, ka, rk, seg)


def _stack_heads(x):
    lane = lax.broadcasted_iota(jnp.int32, x.shape, 1)
    return jnp.concatenate([jnp.where(lane < HEAD_DIM, x, 0.0), jnp.where(lane >= HEAD_DIM, x, 0.0)], axis=0)


def _wkv_kernel(r_ref, lw_ref, k_ref, v_ref, kk_ref, alr_ref, g_ref, bonus_ref, gnw_ref, gnb_ref, tri_ref,
                segm_ref, y_ref, state_ref):
    C = WKV_CHUNK
    nb, _, width = lw_ref.shape
    npair = width // PAIR

    @pl.when(pl.program_id(1) == 0)
    def _():
        state_ref[...] = jnp.zeros_like(state_ref)

    ri = lax.broadcasted_iota(jnp.int32, (2 * C, 2 * C), 0)
    ci = lax.broadcasted_iota(jnp.int32, (2 * C, 2 * C), 1)
    same = (ri // C) == (ci // C)
    strict = same & ((ci % C) < (ri % C))
    incl = same & ((ci % C) <= (ri % C))
    eye = (ri == ci).astype(F32)

    lhs, rhs, a2s, r2s, v2s, bhts, bkts, ptots = [], [], [], [], [], [], [], []
    for bi in range(nb):
        lw = lw_ref[bi]
        cum = jnp.dot(tri_ref[...], lw, precision=HIGHEST, preferred_element_type=F32)
        total = cum[C - 1:C, :]
        p_inv = jnp.exp(-cum)
        p_rem = jnp.exp(total - cum)
        p_tot = jnp.exp(total)
        kk = kk_ref[bi]
        k2 = k_ref[bi]
        b = kk * alr_ref[bi]
        a_t = -kk * jnp.exp(cum - lw)
        r_t = r_ref[bi] * jnp.exp(cum)
        b_t = b * p_inv
        k_t = k2 * p_inv
        b_h = b * p_rem
        k_h = k2 * p_rem
        v = v_ref[bi]
        for p in range(npair):
            sl = slice(p * PAIR, (p + 1) * PAIR)
            a2, r2, b2, kt2 = (_stack_heads(t[:, sl]) for t in (a_t, r_t, b_t, k_t))
            bh2, kh2, v2 = (_stack_heads(t[:, sl]) for t in (b_h, k_h, v))
            lhs.append(jnp.concatenate([a2, r2], axis=0))
            rhs.append(jnp.concatenate([b2, kt2], axis=0))
            a2s.append(a2)
            r2s.append(r2)
            v2s.append(v2)
            bhts.append(bh2.T)
            bkts.append(jnp.concatenate([bh2.T, kh2.T], axis=1))
            ptots.append(p_tot[:, sl])

    chains = range(nb * npair)
    m = [_dot_t(lhs[c], rhs[c]) for c in chains]
    n_ab = [jnp.where(strict, m[c][:2 * C, :2 * C], 0.0) for c in chains]
    m_ak = [jnp.where(strict, m[c][:2 * C, 2 * C:], 0.0) for c in chains]
    m_rb = [jnp.where(incl, m[c][2 * C:, :2 * C], 0.0) for c in chains]
    m_rk = [jnp.where(incl, m[c][2 * C:, 2 * C:], 0.0) for c in chains]
    mv = [_dot(m_ak[c], v2s[c]) for c in chains]
    mrkv = [_dot(m_rk[c], v2s[c]) for c in chains]
    inv = [eye + n_ab[c] for c in chains]
    pw = n_ab
    for _ in range(C.bit_length() - 2):
        pw = [_dot(pw[c], pw[c]) for c in chains]
        inv = [inv[c] + _dot(inv[c], pw[c]) for c in chains]
    wu = [_dot(inv[c], jnp.concatenate([a2s[c], mv[c]], axis=1)) for c in chains]
    qy = [_dot(m_rb[c], wu[c]) + jnp.concatenate([r2s[c], mrkv[c]], axis=1) for c in chains]
    g_t = [_dot(bhts[c], wu[c][:, :PAIR]) + eye * ptots[c] for c in chains]
    h_t = [_dot(bkts[c], jnp.concatenate([wu[c][:, PAIR:], v2s[c]], axis=0)) for c in chains]
    s0 = [state_ref[c] for c in chains]
    y2 = [_dot(qy[c][:, :PAIR], s0[c]) + qy[c][:, PAIR:] for c in chains]
    for c in chains:
        state_ref[c] = _dot(g_t[c], s0[c]) + h_t[c]

    segm = segm_ref[...]
    for bi in range(nb):
        y = jnp.concatenate([y2[bi * npair + p][:C] + y2[bi * npair + p][C:] for p in range(npair)], axis=1)
        mean = _segsum(y, segm)
        yc = y - mean
        var = _segsum(yc * yc, segm)
        yn = yc * lax.rsqrt(var + RW_GN_EPS) * gnw_ref[...] + gnb_ref[...]
        y_ref[bi] = ((yn + bonus_ref[bi]) * g_ref[bi]).astype(y_ref.dtype)


def _wkv(r, lw, k2, v, kk, alr, g, bonus, gnw, gnb, tri, segm, B, S):
    T, width = r.shape
    C = WKV_CHUNK
    nb = WKV_BATCH
    nc = S // C
    fixed = lambda b, c: (0, 0)
    blk = pl.BlockSpec((nb, C, width), lambda b, c: (b, c, 0))
    vec = pl.BlockSpec((1, width), fixed)
    ins = [t.reshape(B, S, width) for t in (r, lw, k2, v, kk, alr, g, bonus)]
    y = pl.pallas_call(
        _wkv_kernel,
        grid=(B // nb, nc),
        in_specs=[blk] * 8 + [vec, vec, pl.BlockSpec((C, C), fixed), pl.BlockSpec((width, width), fixed)],
        out_specs=blk,
        out_shape=jax.ShapeDtypeStruct((B, S, width), BF16),
        scratch_shapes=[pltpu.VMEM((nb * width // PAIR, PAIR, PAIR), F32)],
        compiler_params=_cparams(("parallel", "arbitrary")),
        name="wkv",
    )(*ins, gnw, gnb, tri, segm)
    return y.reshape(T, width)


def _split3(x):
    hi = x.astype(BF16).astype(F32)
    mid = (x - hi).astype(BF16).astype(F32)
    return hi, mid, x - hi - mid


def _spread_heads(x):
    lane = lax.broadcasted_iota(jnp.int32, (x.shape[0], PAIR), 1)
    groups = []
    for p in range(x.shape[1] // PAIR):
        blk = x[:, p * PAIR:(p + 1) * PAIR]
        groups.append(jnp.where(lane < HEAD_DIM, blk, 0.0))
        groups.append(jnp.where(lane < HEAD_DIM, pltpu.roll(blk, HEAD_DIM, axis=1), 0.0))
    return jnp.concatenate(groups, axis=1)


def _fx_prep_kernel(q_ref, k_ref, v_ref, fzt_ref, bf_ref, qw_ref, kw_ref, segm_ref, tril_ref,
                    place_ref, oneq_ref, onek_ref, onev_ref, qa_ref, ka_ref, va_ref, carry_ref):
    @pl.when(pl.program_id(1) == 0)
    def _():
        carry_ref[...] = jnp.zeros_like(carry_ref)

    segm = segm_ref[...]
    q = q_ref[...]
    k = k_ref[...]
    tm = q.shape[0]
    qn = q * lax.rsqrt(_segsum(q * q, segm) + QK_EPS) * (qw_ref[...] * (HEAD_DIM ** -0.5 * LOG2E))
    kn = k * lax.rsqrt(_segsum(k * k, segm) + QK_EPS) * kw_ref[...]

    lf = -_softplus(-(fzt_ref[...] + bf_ref[...]))
    c = lax.dot_general(tril_ref[...], lf, (((1,), (1,)), ((), ())), precision=HIGHEST,
                        preferred_element_type=F32) + carry_ref[...]
    carry_ref[...] = c[tm - 1:tm, :]
    placed = jnp.dot(c * LOG2E, place_ref[...], precision=HIGHEST, preferred_element_type=F32)
    hi, mid, lo = _split3(placed)
    lane = lax.broadcasted_iota(jnp.int32, placed.shape, 1) % PAIR
    c_q = jnp.where(lane == HEAD_DIM + 3, hi, jnp.where(lane == HEAD_DIM + 4, mid,
                                                        jnp.where(lane == HEAD_DIM + 5, lo, 0.0)))
    c_k = jnp.where(lane == HEAD_DIM, hi, jnp.where(lane == HEAD_DIM + 1, mid,
                                                    jnp.where(lane == HEAD_DIM + 2, lo, 0.0)))
    qa_ref[...] = (_spread_heads(qn) + c_q + oneq_ref[...]).astype(qa_ref.dtype)
    ka_ref[...] = (_spread_heads(kn) - c_k + onek_ref[...]).astype(ka_ref.dtype)
    va_ref[...] = (_spread_heads(v_ref[...]) + onev_ref[...]).astype(va_ref.dtype)


def _fx_prep(p_fx, fzt, bf, qw, kw, segm, tril, B, S, width):
    T = p_fx.shape[0]
    nh = fzt.shape[0]
    wide = nh * PAIR
    tm = ROW_TILE
    ns = S // tm
    fixed = lambda b, s: (0, 0)
    col = lambda j: pl.BlockSpec((tm, width), lambda b, s: (b * ns + s, j))
    out = pl.BlockSpec((tm, wide), lambda b, s: (b * ns + s, 0))
    vec = pl.BlockSpec((1, width), fixed)
    wvec = pl.BlockSpec((1, wide), fixed)

    lane = jnp.arange(wide) % PAIR
    head = jnp.arange(wide) // PAIR
    place = ((head[None, :] == jnp.arange(nh)[:, None]) & (lane >= HEAD_DIM) & (lane < HEAD_DIM + 6)).astype(F32)
    one_q = ((lane >= HEAD_DIM) & (lane < HEAD_DIM + 3)).astype(F32).reshape(1, wide)
    one_k = ((lane >= HEAD_DIM + 3) & (lane < HEAD_DIM + 6)).astype(F32).reshape(1, wide)
    one_v = (lane >= HEAD_DIM).astype(F32).reshape(1, wide)
    return pl.pallas_call(
        _fx_prep_kernel,
        grid=(B, ns),
        in_specs=[
            col(0), col(1), col(2),
            pl.BlockSpec((nh, tm), lambda b, s: (0, b * ns + s)),
            pl.BlockSpec((nh, 1), fixed),
            vec, vec,
            pl.BlockSpec((width, width), fixed),
            pl.BlockSpec((tm, tm), fixed),
            pl.BlockSpec((nh, wide), fixed),
            wvec, wvec, wvec,
        ],
        out_specs=[out, out, out],
        out_shape=[jax.ShapeDtypeStruct((T, wide), BF16)] * 3,
        scratch_shapes=[pltpu.VMEM((1, nh), F32)],
        compiler_params=_cparams(("parallel", "arbitrary")),
        name="fx_prep",
    )(p_fx, p_fx, p_fx, fzt, bf, qw, kw, segm, tril, place, one_q, one_k, one_v)


def _fox_attn_kernel(qi_ref, kj_ref, q_ref, k_ref, v_ref, og_ref, o_ref, m_ref, acc_ref):
    i = qi_ref[pl.program_id(2)]
    j = kj_ref[pl.program_id(2)]
    tq = q_ref.shape[0]
    tk = k_ref.shape[0]
    heads = range(ATTN_HEADS)
    grp = lambda ref, h: ref[:, h * PAIR:(h + 1) * PAIR]

    @pl.when(j == 0)
    def _():
        m_ref[...] = jnp.full_like(m_ref, NEG_BIG)
        acc_ref[...] = jnp.zeros_like(acc_ref)

    def step(masked):
        s = [lax.dot_general(grp(q_ref, h), grp(k_ref, h), (((1,), (1,)), ((), ())), preferred_element_type=F32)
             for h in heads]
        if masked:
            row = lax.broadcasted_iota(jnp.int32, (tq, tk), 0)
            col = lax.broadcasted_iota(jnp.int32, (tq, tk), 1)
            s = [jnp.where(col <= row, s[h], NEG_BIG) for h in heads]
        m_old = [m_ref[h] for h in heads]
        m_new = [jnp.maximum(m_old[h], jnp.max(s[h], axis=1, keepdims=True)) for h in heads]
        alpha = [jnp.exp2(m_old[h] - m_new[h]) for h in heads]
        pr = [jnp.exp2(s[h] - jnp.concatenate([m_new[h]] * (tk // PAIR), axis=1)).astype(BF16) for h in heads]
        pv = [jnp.dot(pr[h], grp(v_ref, h), preferred_element_type=F32) for h in heads]
        for h in heads:
            acc_ref[h] = alpha[h] * acc_ref[h] + pv[h]
            m_ref[h] = m_new[h]

    @pl.when(j < i)
    def _():
        step(False)

    @pl.when(j == i)
    def _():
        step(True)
        lane = lax.broadcasted_iota(jnp.int32, (tq, PAIR), 1)
        outs = []
        for p in range(ATTN_HEADS // 2):
            a0 = acc_ref[2 * p]
            a1 = acc_ref[2 * p + 1]
            o0 = a0 * pltpu.roll(1.0 / a0, HEAD_DIM, axis=1)
            o1 = pltpu.roll(a1, HEAD_DIM, axis=1) * (1.0 / a1)
            outs.append(jnp.where(lane < HEAD_DIM, o0, o1))
        o = jnp.concatenate(outs, axis=1)
        o_ref[...] = (o * _sigmoid(og_ref[...])).astype(o_ref.dtype)


def _fox_attn(qa, ka, va, p_fx, B, S, width):
    T = qa.shape[0]
    tq, tk = ATTN_TQ, ATTN_TK
    assert tq == tk
    nq = S // tq
    nh = ATTN_HEADS
    ow = nh * HEAD_DIM
    ngroup = width // ow
    og_col0 = 3 * width // ow
    pairs = [(i, j) for i in range(nq) for j in range(i + 1)]
    qi = jnp.array([ij[0] for ij in pairs], jnp.int32)
    kj = jnp.array([ij[1] for ij in pairs], jnp.int32)
    kv = pl.BlockSpec((tk, nh * PAIR), lambda b, p, t, qi, kj: (b * nq + kj[t], p))
    grid_spec = pltpu.PrefetchScalarGridSpec(
        num_scalar_prefetch=2,
        grid=(B, ngroup, len(pairs)),
        in_specs=[
            pl.BlockSpec((tq, nh * PAIR), lambda b, p, t, qi, kj: (b * nq + qi[t], p)), kv, kv,
            pl.BlockSpec((tq, ow), lambda b, p, t, qi, kj: (b * nq + qi[t], og_col0 + p)),
        ],
        out_specs=pl.BlockSpec((tq, ow), lambda b, p, t, qi, kj: (b * nq + qi[t], p)),
        scratch_shapes=[pltpu.VMEM((nh, tq, PAIR), F32), pltpu.VMEM((nh, tq, PAIR), F32)],
    )
    return pl.pallas_call(
        _fox_attn_kernel,
        grid_spec=grid_spec,
        out_shape=jax.ShapeDtypeStruct((T, width), BF16),
        compiler_params=_cparams(("parallel", "parallel", "arbitrary")),
        name="fox_attn",
    )(qi, kj, qa, ka, va, p_fx)


def _out_ln_kernel(yrw_ref, yfx_ref, h_ref, wo1_ref, wo2_ref, lnw_ref, lnb_ref, rw_ref, rb_ref,
                   h1_ref, logit_ref, *, alpha):
    mix = (jnp.dot(yrw_ref[...], wo1_ref[...], preferred_element_type=F32)
           + jnp.dot(yfx_ref[...], wo2_ref[...], preferred_element_type=F32))
    h1 = _layer_norm(alpha * h_ref[...] + mix, lnw_ref[...], lnb_ref[...])
    h1_ref[...] = h1
    ne = rb_ref.shape[1]
    h_hi = h1.astype(BF16)
    h_lo = (h1 - h_hi.astype(F32)).astype(BF16)
    rw = rw_ref[...]
    first = jnp.dot(h_hi, rw, preferred_element_type=F32)
    second = jnp.dot(h_lo, rw[:, :ne], preferred_element_type=F32)
    logit_ref[...] = first[:, :ne] + first[:, ne:] + second + rb_ref[...]


def _out_ln(y_rw, y_fx, h0, wo1, wo2, lnw, lnb, rw, rb, alpha):
    T, D = h0.shape
    width = y_rw.shape[1]
    ne = rb.shape[1]
    tm = ROW_TILE
    row = lambda i: (i, 0)
    fixed = lambda i: (0, 0)
    rw_hi = rw.astype(BF16)
    rw_lo = (rw - rw_hi.astype(F32)).astype(BF16)
    rw = jnp.concatenate([rw_hi, rw_lo], axis=1)
    return pl.pallas_call(
        functools.partial(_out_ln_kernel, alpha=alpha),
        grid=(T // tm,),
        in_specs=[
            pl.BlockSpec((tm, width), row), pl.BlockSpec((tm, width), row), pl.BlockSpec((tm, D), row),
            pl.BlockSpec((width, D), fixed), pl.BlockSpec((width, D), fixed),
            pl.BlockSpec((1, D), fixed), pl.BlockSpec((1, D), fixed),
            pl.BlockSpec((D, 2 * ne), fixed), pl.BlockSpec((1, ne), fixed),
        ],
        out_specs=[pl.BlockSpec((tm, D), row), pl.BlockSpec((tm, ne), row)],
        out_shape=[jax.ShapeDtypeStruct((T, D), F32), jax.ShapeDtypeStruct((T, ne), F32)],
        compiler_params=_cparams(("parallel",)),
        name="out_ln",
    )(y_rw, y_fx, h0, wo1, wo2, lnw, lnb, rw, rb)


DEINT_COLS = 256
MOE_VMEM_LIMIT = 56 * 1024 * 1024

def _pack_bf16_pairs(x):
    n = x.shape[1] // 2
    bits = pltpu.bitcast(x.astype(BF16).astype(F32), jnp.uint32)
    return lax.shift_right_logical(bits[:, :n], jnp.uint32(16)) | bits[:, n:]


def _unpack_bf16_pairs(u):
    lo = pltpu.bitcast(lax.shift_left(u, jnp.uint32(16)), F32)
    hi = pltpu.bitcast(u & jnp.uint32(0xFFFF0000), F32)
    return jnp.concatenate([lo, hi], axis=1)


def _moe_kernel(bexp_ref, nused_ref, first_ref, nexte_ref, slot_ref,
                x_ref, w1_hbm, w2_hbm, b1g_ref, b1l_ref, b2_ref, perm_ref, o_ref,
                w1_stage, w2_stage, w1g_bf, w1l_bf, w2_bf, sem):
    i = pl.program_id(0)
    live = i < nused_ref[0]

    def weight_copies(e, s):
        return (pltpu.make_async_copy(w1_hbm.at[e], w1_stage.at[s], sem.at[0, s]),
                pltpu.make_async_copy(w2_hbm.at[e], w2_stage.at[s], sem.at[1, s]))

    @pl.when(live & (i == 0))
    def _():
        for cp in weight_copies(bexp_ref[0], 0):
            cp.start()

    @pl.when(live & (first_ref[i] == 1))
    def _():
        s = slot_ref[i]
        for cp in weight_copies(bexp_ref[i], s):
            cp.wait()

        @pl.when(nexte_ref[i] >= 0)
        def _():
            for cp in weight_copies(nexte_ref[i], 1 - s):
                cp.start()

        half = DEINT_COLS // 2
        for c in range(w1_stage.shape[2] // DEINT_COLS):
            blk = w1_stage[s, :, c * DEINT_COLS:(c + 1) * DEINT_COLS].astype(BF16)
            out = jnp.dot(blk, perm_ref[...], preferred_element_type=F32).astype(BF16)
            w1g_bf[:, c * half:(c + 1) * half] = out[:, :half]
            w1l_bf[:, c * half:(c + 1) * half] = out[:, half:]
        w2_bf[...] = w2_stage[s].astype(BF16)

    @pl.when(live)
    def _():
        x = x_ref[...].astype(BF16)
        x_glu = jnp.minimum(jnp.dot(x, w1g_bf[...], preferred_element_type=F32) + b1g_ref[0], SWIGLU_LIMIT)
        x_lin = jnp.clip(jnp.dot(x, w1l_bf[...], preferred_element_type=F32) + b1l_ref[0],
                         -SWIGLU_LIMIT, SWIGLU_LIMIT)
        act = x_glu * _sigmoid(SWIGLU_ALPHA * x_glu) * (x_lin + 1.0)
        o = jnp.dot(act.astype(BF16), w2_bf[...], preferred_element_type=F32) + b2_ref[0]
        o_ref[...] = _pack_bf16_pairs(o)

    @pl.when(jnp.logical_not(live))
    def _():
        o_ref[...] = jnp.zeros_like(o_ref)


def _moe_ffn(block_exp, n_used, xs, w1, b1g, b1l, w2, b2):
    n_rows, D = xs.shape
    E, _, F2 = w1.shape
    F = F2 // 2
    tm = MOE_TILE
    n_blocks = n_rows // tm

    idx = jnp.arange(n_blocks, dtype=jnp.int32)
    first = jnp.concatenate([jnp.ones((1,), jnp.bool_), block_exp[1:] != block_exp[:-1]])
    slot = ((jnp.cumsum(first.astype(jnp.int32)) - 1) % 2).astype(jnp.int32)
    cand = jnp.where(first & (idx < n_used[0]), idx, n_blocks)
    next_first = jnp.min(jnp.where(idx[None, :] > idx[:, None], cand[None, :], n_blocks), axis=1)
    next_e = jnp.where(next_first < n_blocks, block_exp[jnp.minimum(next_first, n_blocks - 1)], -1).astype(jnp.int32)

    half = DEINT_COLS // 2
    src = jnp.arange(DEINT_COLS)
    dst = jnp.where(src % 2 == 0, src // 2, half + src // 2)
    perm = (dst[:, None] == jnp.arange(DEINT_COLS)[None, :]).astype(BF16)

    live = lambda i, be, nu: jnp.minimum(i, nu[0] - 1)
    bspec = lambda n: pl.BlockSpec((1, 1, n), lambda i, be, nu, fi, ne, sl: (be[live(i, be, nu)], 0, 0))
    grid_spec = pltpu.PrefetchScalarGridSpec(
        num_scalar_prefetch=5,
        grid=(n_blocks,),
        in_specs=[
            pl.BlockSpec((tm, D), lambda i, be, nu, fi, ne, sl: (live(i, be, nu), 0)),
            pl.BlockSpec(memory_space=pl.ANY),
            pl.BlockSpec(memory_space=pl.ANY),
            bspec(F), bspec(F), bspec(D),
            pl.BlockSpec((DEINT_COLS, DEINT_COLS), lambda i, be, nu, fi, ne, sl: (0, 0)),
        ],
        out_specs=pl.BlockSpec((tm, D // 2), lambda i, be, nu, fi, ne, sl: (i, 0)),
        scratch_shapes=[
            pltpu.VMEM((2, D, F2), F32), pltpu.VMEM((2, F, D), F32),
            pltpu.VMEM((D, F), BF16), pltpu.VMEM((D, F), BF16), pltpu.VMEM((F, D), BF16),
            pltpu.SemaphoreType.DMA((2, 2)),
        ],
    )
    return pl.pallas_call(
        _moe_kernel,
        grid_spec=grid_spec,
        out_shape=jax.ShapeDtypeStruct((n_rows, D // 2), jnp.uint32),
        compiler_params=pltpu.CompilerParams(dimension_semantics=("arbitrary",), vmem_limit_bytes=MOE_VMEM_LIMIT),
        name="moe_ffn",
    )(block_exp, n_used, first.astype(jnp.int32), next_e, slot, xs, w1, w2, b1g, b1l, b2, perm)


def _combine_ln_kernel(h_ref, eo_ref, gate_ref, lnw_ref, lnb_ref, o_ref, *, alpha):
    gates = gate_ref[...]
    ffn = _unpack_bf16_pairs(eo_ref[0]) * gates[:, 0:1]
    for j in range(1, TOP_K):
        ffn = ffn + _unpack_bf16_pairs(eo_ref[j]) * gates[:, j:j + 1]
    o_ref[...] = _layer_norm(alpha * h_ref[...] + ffn, lnw_ref[...], lnb_ref[...])


def _combine_ln(h1, eo, gates, lnw, lnb, alpha):
    T, D = h1.shape
    tm = ROW_TILE
    return pl.pallas_call(
        functools.partial(_combine_ln_kernel, alpha=alpha),
        grid=(T // tm,),
        in_specs=[
            pl.BlockSpec((tm, D), lambda i: (i, 0)),
            pl.BlockSpec((TOP_K, tm, D // 2), lambda i: (0, i, 0)),
            pl.BlockSpec((tm, TOP_K), lambda i: (i, 0)),
            pl.BlockSpec((1, D), lambda i: (0, 0)),
            pl.BlockSpec((1, D), lambda i: (0, 0)),
        ],
        out_specs=pl.BlockSpec((tm, D), lambda i: (i, 0)),
        out_shape=jax.ShapeDtypeStruct((T, D), F32),
        compiler_params=_cparams(("parallel",)),
        name="combine_ln",
    )(h1, eo, gates, lnw, lnb)


def _pad_to(x, n, axis):
    pad = [(0, 0)] * x.ndim
    pad[axis] = (0, n - x.shape[axis])
    return jnp.pad(x, pad)


def _block_diag_ones(width, value=1.0):
    idx = jnp.arange(width) // HEAD_DIM
    return jnp.where(idx[:, None] == idx[None, :], value, 0.0).astype(BF16)


def _route(logits, n_experts, tile):
    T = logits.shape[0]
    top_val, top_idx = lax.top_k(logits[:, :n_experts], TOP_K)
    gates = jax.nn.softmax(top_val, axis=-1)
    e_flat = top_idx.reshape(-1).astype(jnp.int32)
    n_assign = T * TOP_K
    n_rows = n_assign + n_experts * tile
    n_blocks = n_rows // tile
    assert n_experts * n_assign < 2 ** 31
    eids = jnp.arange(n_experts, dtype=jnp.int32)
    aids = jnp.arange(n_assign, dtype=jnp.int32)
    skeys = lax.sort(e_flat * n_assign + aids)
    order = skeys % n_assign
    e_sorted = skeys // n_assign
    counts = jnp.sum((e_flat[:, None] == eids[None, :]).astype(jnp.int32), axis=0)
    starts = jnp.cumsum(counts) - counts
    padded = (counts + tile - 1) // tile * tile
    pends = jnp.cumsum(padded)
    pstarts = pends - padded
    shift = pstarts - starts
    dest_sorted = aids + jnp.sum(jnp.where(e_sorted[:, None] == eids[None, :], shift[None, :], 0), axis=1)
    _, dest = lax.sort((order, dest_sorted), num_keys=1)
    block_start = jnp.arange(n_blocks, dtype=jnp.int32) * tile
    block_exp = jnp.minimum(jnp.sum((pends[None, :] <= block_start[:, None]).astype(jnp.int32), axis=1),
                            n_experts - 1).astype(jnp.int32)
    n_used = (pends[-1] // tile).astype(jnp.int32).reshape(1)
    of_block = lambda per_expert: jnp.sum(
        jnp.where(block_exp[:, None] == eids[None, :], per_expert[None, :], 0), axis=1)
    in_exp = block_start - of_block(pstarts)
    n_valid = of_block(counts) - in_exp
    within = jnp.arange(tile, dtype=jnp.int32)[None, :]
    src = jnp.clip((of_block(starts) + in_exp)[:, None] + within, 0, n_assign - 1)
    row_tok = jnp.where(within < n_valid[:, None], order[src.reshape(-1)].reshape(n_blocks, tile) // TOP_K, 0)
    return gates, dest.reshape(T, TOP_K), row_tok.reshape(-1).astype(jnp.int32), block_exp, n_used


def kernel(x, ln_in_w, ln_in_b, w_in, rw_mu, rw_w0, rw_w2, rw_a0, rw_a2, rw_g2, rw_k_k, rw_k_a, rw_r_k,
           rw_gn_w, rw_gn_b, fx_b_f, fx_q_norm, fx_k_norm, w_o, ln1_w, ln1_b, router_w, router_b,
           exp_w1, exp_b1, exp_w2, exp_b2, ln2_w, ln2_b):
    B, S, D = x.shape
    T = B * S
    depth = w_in.shape[0]
    alpha = (2 * depth) ** 0.25
    rw_w = rw_w0.shape[1]
    fx_heads = fx_b_f.shape[1]
    fx_w = fx_heads * HEAD_DIM
    d_lora, a_lora, g_lora = rw_w2.shape[1], rw_a2.shape[1], rw_g2.shape[1]
    n_lora = d_lora + a_lora + g_lora
    lora_pad = -(-n_lora // 128) * 128
    rw_cols = 3 * rw_w + n_lora
    n_rw = 3 * rw_w + lora_pad
    n_experts = router_w.shape[2]
    ne_pad = -(-n_experts // 128) * 128
    row = lambda a: a.reshape(1, -1)

    seg_rw = _block_diag_ones(rw_w)
    segm_rw = _block_diag_ones(rw_w, 1.0 / HEAD_DIM)
    segm_fx = _block_diag_ones(fx_w, 1.0 / HEAD_DIM)
    tidx = jnp.arange(ROW_TILE)
    tril = (tidx[:, None] >= tidx[None, :]).astype(F32)
    cidx = jnp.arange(WKV_CHUNK)
    tri_c = (cidx[:, None] >= cidx[None, :]).astype(F32)

    assert depth == 1, "single-layer block"
    l = 0
    x2 = x.reshape(T, D)
    w_l = w_in[l]
    w_main = jnp.concatenate(
        [_pad_to(w_l[:, :rw_cols], n_rw, 1), w_l[:, rw_cols:rw_cols + 4 * fx_w]], axis=1).astype(BF16)
    wfz_t = w_l[:, rw_cols + 4 * fx_w:].T
    h0, p_rw, p_fx, fzt = _ln_proj(x2, row(ln_in_w), row(ln_in_b), w_main, wfz_t, n_rw)

    mu = _pad_to(row(rw_mu[l]), n_rw, 1)
    w2p = _pad_to(rw_w2[l], lora_pad, 0).astype(BF16)
    a2p = _pad_to(jnp.pad(rw_a2[l], ((d_lora, 0), (0, 0))), lora_pad, 0).astype(BF16)
    g2p = _pad_to(jnp.pad(rw_g2[l], ((d_lora + a_lora, 0), (0, 0))), lora_pad, 0).astype(BF16)
    r, lw, k2, v, kk, alr, g, bonus = _rw_prep(
        p_rw, B, S, rw_w, mu, row(rw_w0[l]), w2p, row(rw_a0[l]), a2p, g2p,
        row(rw_k_k[l]), row(rw_k_a[l]), row(rw_r_k[l]), seg_rw)
    y_rw = _wkv(r, lw, k2, v, kk, alr, g, bonus, row(rw_gn_w[l]), row(rw_gn_b[l]), tri_c, segm_rw, B, S)

    qw = row(jnp.tile(fx_q_norm[l], fx_heads))
    kw = row(jnp.tile(fx_k_norm[l], fx_heads))
    qa, ka, va = _fx_prep(p_fx, fzt, fx_b_f[l].reshape(-1, 1), qw, kw, segm_fx, tril, B, S, fx_w)
    y_fx = _fox_attn(qa, ka, va, p_fx, B, S, fx_w)

    wo = w_o[l].astype(BF16)
    rw_pad = _pad_to(router_w[l], ne_pad, 1)
    rb_pad = _pad_to(row(router_b[l]), ne_pad, 1)
    h1, logits = _out_ln(y_rw, y_fx, h0, wo[:rw_w], wo[rw_w:], row(ln1_w[l]), row(ln1_b[l]),
                         rw_pad, rb_pad, alpha)

    gates, pos, row_tok, block_exp, n_used = _route(logits, n_experts, MOE_TILE)
    xs = h1[row_tok]
    b1 = exp_b1[l]
    b1g = b1[:, None, 0::2]
    b1l = b1[:, None, 1::2]
    eo_rows = _moe_ffn(block_exp, n_used, xs, exp_w1[l], b1g, b1l, exp_w2[l], exp_b2[l][:, None, :])
    eo = eo_rows[pos.T]
    h = _combine_ln(h1, eo, gates, row(ln2_w[l]), row(ln2_b[l]), alpha)
    return h.reshape(B, S, D)
```

```python
import functools

import jax
import jax.numpy as jnp
from jax import lax
from jax.experimental import pallas as pl
from jax.experimental.pallas import tpu as pltpu

F32 = jnp.float32
BF16 = jnp.bfloat16
HIGHEST = lax.Precision.HIGHEST

HEAD_DIM = 64
PAIR = 2 * HEAD_DIM
WKV_CHUNK = 64
RW_GN_EPS = 64e-5
QK_EPS = 1e-6
LN_EPS = 1e-5
TOP_K = 4
SWIGLU_ALPHA = 1.702
SWIGLU_LIMIT = 7.0
NEG_BIG = -1e30
LOG2E = 1.4426950408889634

ROW_TILE = 256
ATTN_TQ = 512
ATTN_TK = 512
ATTN_HEADS = 8
MOE_TILE = 512
WKV_BATCH = 4
VMEM_LIMIT = 48 * 1024 * 1024


def _cparams(sem):
    return pltpu.CompilerParams(dimension_semantics=sem, vmem_limit_bytes=VMEM_LIMIT)


def _dot(a, b):
    return jnp.dot(a.astype(BF16), b.astype(BF16), preferred_element_type=F32)


def _dot_t(a, b):
    return lax.dot_general(a.astype(BF16), b.astype(BF16), (((1,), (1,)), ((), ())),
                           preferred_element_type=F32)


def _segsum(x, seg):
    hi = x.astype(BF16)
    lo = (x - hi.astype(F32)).astype(BF16)
    return (jnp.dot(hi, seg, preferred_element_type=F32) + jnp.dot(lo, seg, preferred_element_type=F32))


def _sigmoid(x):
    return 1.0 / (1.0 + jnp.exp(-x))


def _softplus(x):
    return jnp.maximum(x, 0.0) + jnp.log(1.0 + jnp.exp(-jnp.abs(x)))


def _layer_norm(x, w, b):
    mu = jnp.mean(x, axis=-1, keepdims=True)
    xc = x - mu
    var = jnp.mean(xc * xc, axis=-1, keepdims=True)
    return xc * lax.rsqrt(var + LN_EPS) * w + b


def _ln_proj_kernel(x_ref, lnw_ref, lnb_ref, w_ref, wfz_ref, h_ref, prw_ref, pfx_ref, fzt_ref, *, n_rw):
    h = _layer_norm(x_ref[...], lnw_ref[...], lnb_ref[...])
    h_ref[...] = h
    p = jnp.dot(h.astype(BF16), w_ref[...], preferred_element_type=F32)
    prw_ref[...] = p[:, :n_rw]
    pfx_ref[...] = p[:, n_rw:]
    fzt_ref[...] = lax.dot_general(wfz_ref[...], h, (((1,), (1,)), ((), ())),
                                   precision=HIGHEST, preferred_element_type=F32)


def _ln_proj(x2, lnw, lnb, w_main, wfz_t, n_rw):
    T, D = x2.shape
    n_all = w_main.shape[1]
    n_fx = n_all - n_rw
    nh = wfz_t.shape[0]
    tm = ROW_TILE
    return pl.pallas_call(
        functools.partial(_ln_proj_kernel, n_rw=n_rw),
        grid=(T // tm,),
        in_specs=[
            pl.BlockSpec((tm, D), lambda i: (i, 0)),
            pl.BlockSpec((1, D), lambda i: (0, 0)),
            pl.BlockSpec((1, D), lambda i: (0, 0)),
            pl.BlockSpec((D, n_all), lambda i: (0, 0)),
            pl.BlockSpec((nh, D), lambda i: (0, 0)),
        ],
        out_specs=[
            pl.BlockSpec((tm, D), lambda i: (i, 0)),
            pl.BlockSpec((tm, n_rw), lambda i: (i, 0)),
            pl.BlockSpec((tm, n_fx), lambda i: (i, 0)),
            pl.BlockSpec((nh, tm), lambda i: (0, i)),
        ],
        out_shape=[
            jax.ShapeDtypeStruct((T, D), F32),
            jax.ShapeDtypeStruct((T, n_rw), F32),
            jax.ShapeDtypeStruct((T, n_fx), F32),
            jax.ShapeDtypeStruct((nh, T), F32),
        ],
        compiler_params=_cparams(("parallel",)),
        name="ln_proj",
    )(x2, lnw, lnb, w_main, wfz_t)


def _rw_prep_kernel(p_ref, mu_ref, w0_ref, w2_ref, a0_ref, a2_ref, g2_ref, kkw_ref, ka_ref, rk_ref, seg_ref,
                    r_ref, lw_ref, k_ref, v_ref, kk_ref, alr_ref, g_ref, bonus_ref, carry_ref, *, width):
    @pl.when(pl.program_id(1) == 0)
    def _():
        carry_ref[...] = jnp.zeros_like(carry_ref)

    p = p_ref[...]
    tm = p.shape[0]
    prev = pltpu.roll(p, 1, axis=0)
    first_row = lax.broadcasted_iota(jnp.int32, p.shape, 0) == 0
    prev = jnp.where(first_row, carry_ref[...], prev)
    carry_ref[...] = p[tm - 1:tm, :]
    ps = p + mu_ref[...] * (prev - p)

    r = ps[:, 0:width]
    k = ps[:, width:2 * width]
    v = ps[:, 2 * width:3 * width]
    lora = ps[:, 3 * width:]
    seg = seg_ref[...]

    wl = w0_ref[...] + _dot(jnp.tanh(lora), w2_ref[...])
    w_raw = -_softplus(-wl) - 0.5
    lw_ref[...] = -jnp.exp(w_raw)
    alr = _sigmoid(a0_ref[...] + _dot(lora, a2_ref[...]))
    g_ref[...] = _dot(_sigmoid(lora), g2_ref[...])
    kkp = k * kkw_ref[...]
    nrm = jnp.sqrt(_segsum(kkp * kkp, seg))
    kk_ref[...] = kkp / jnp.maximum(nrm, 1e-12)
    k2 = k * (1.0 + (alr - 1.0) * ka_ref[...])
    bonus_ref[...] = _segsum(r * k2 * rk_ref[...], seg) * v
    r_ref[...] = r
    k_ref[...] = k2
    v_ref[...] = v
    alr_ref[...] = alr


def _rw_prep(p_rw, B, S, width, mu, w0, w2p, a0, a2p, g2p, kkw, ka, rk, seg):
    T, n_rw = p_rw.shape
    tm = ROW_TILE
    ns = S // tm
    n_lora = n_rw - 3 * width
    row = lambda b, s: (b * ns + s, 0)
    fixed = lambda b, s: (0, 0)
    vec = pl.BlockSpec((1, width), fixed)
    out = pl.BlockSpec((tm, width), row)
    return pl.pallas_call(
        functools.partial(_rw_prep_kernel, width=width),
        grid=(B, ns),
        in_specs=[
            pl.BlockSpec((tm, n_rw), row),
            pl.BlockSpec((1, n_rw), fixed),
            vec, pl.BlockSpec((n_lora, width), fixed),
            vec, pl.BlockSpec((n_lora, width), fixed),
            pl.BlockSpec((n_lora, width), fixed),
            vec, vec, vec,
            pl.BlockSpec((width, width), fixed),
        ],
        out_specs=[out] * 8,
        out_shape=[jax.ShapeDtypeStruct((T, width), F32)] * 8,
        scratch_shapes=[pltpu.VMEM((1, n_rw), F32)],
        compiler_params=_cparams(("parallel", "arbitrary")),
        name="rw_prep",
    )(p_rw, mu, w0, w2p, a0, a2p, g2p, kkw, ka, rk, seg)


def _stack_heads(x):
    lane = lax.broadcasted_iota(jnp.int32, x.shape, 1)
    return jnp.concatenate([jnp.where(lane < HEAD_DIM, x, 0.0), jnp.where(lane >= HEAD_DIM, x, 0.0)], axis=0)


def _wkv_kernel(r_ref, lw_ref, k_ref, v_ref, kk_ref, alr_ref, g_ref, bonus_ref, gnw_ref, gnb_ref, tri_ref,
                segm_ref, y_ref, state_ref):
    C = WKV_CHUNK
    nb, _, width = lw_ref.shape
    npair = width // PAIR

    @pl.when(pl.program_id(1) == 0)
    def _():
        state_ref[...] = jnp.zeros_like(state_ref)

    ri = lax.broadcasted_iota(jnp.int32, (2 * C, 2 * C), 0)
    ci = lax.broadcasted_iota(jnp.int32, (2 * C, 2 * C), 1)
    same = (ri // C) == (ci // C)
    strict = same & ((ci % C) < (ri % C))
    incl = same & ((ci % C) <= (ri % C))
    eye = (ri == ci).astype(F32)

    lhs, rhs, a2s, r2s, v2s, bhts, bkts, ptots = [], [], [], [], [], [], [], []
    for bi in range(nb):
        lw = lw_ref[bi]
        cum = jnp.dot(tri_ref[...], lw, precision=HIGHEST, preferred_element_type=F32)
        total = cum[C - 1:C, :]
        p_inv = jnp.exp(-cum)
        p_rem = jnp.exp(total - cum)
        p_tot = jnp.exp(total)
        kk = kk_ref[bi]
        k2 = k_ref[bi]
        b = kk * alr_ref[bi]
        a_t = -kk * jnp.exp(cum - lw)
        r_t = r_ref[bi] * jnp.exp(cum)
        b_t = b * p_inv
        k_t = k2 * p_inv
        b_h = b * p_rem
        k_h = k2 * p_rem
        v = v_ref[bi]
        for p in range(npair):
            sl = slice(p * PAIR, (p + 1) * PAIR)
            a2, r2, b2, kt2 = (_stack_heads(t[:, sl]) for t in (a_t, r_t, b_t, k_t))
            bh2, kh2, v2 = (_stack_heads(t[:, sl]) for t in (b_h, k_h, v))
            lhs.append(jnp.concatenate([a2, r2], axis=0))
            rhs.append(jnp.concatenate([b2, kt2], axis=0))
            a2s.append(a2)
            r2s.append(r2)
            v2s.append(v2)
            bhts.append(bh2.T)
            bkts.append(jnp.concatenate([bh2.T, kh2.T], axis=1))
            ptots.append(p_tot[:, sl])

    chains = range(nb * npair)
    m = [_dot_t(lhs[c], rhs[c]) for c in chains]
    n_ab = [jnp.where(strict, m[c][:2 * C, :2 * C], 0.0) for c in chains]
    m_ak = [jnp.where(strict, m[c][:2 * C, 2 * C:], 0.0) for c in chains]
    m_rb = [jnp.where(incl, m[c][2 * C:, :2 * C], 0.0) for c in chains]
    m_rk = [jnp.where(incl, m[c][2 * C:, 2 * C:], 0.0) for c in chains]
    mv = [_dot(m_ak[c], v2s[c]) for c in chains]
    mrkv = [_dot(m_rk[c], v2s[c]) for c in chains]
    inv = [eye + n_ab[c] for c in chains]
    pw = n_ab
    for _ in range(C.bit_length() - 2):
        pw = [_dot(pw[c], pw[c]) for c in chains]
        inv = [inv[c] + _dot(inv[c], pw[c]) for c in chains]
    wu = [_dot(inv[c], jnp.concatenate([a2s[c], mv[c]], axis=1)) for c in chains]
    qy = [_dot(m_rb[c], wu[c]) + jnp.concatenate([r2s[c], mrkv[c]], axis=1) for c in chains]
    g_t = [_dot(bhts[c], wu[c][:, :PAIR]) + eye * ptots[c] for c in chains]
    h_t = [_dot(bkts[c], jnp.concatenate([wu[c][:, PAIR:], v2s[c]], axis=0)) for c in chains]
    s0 = [state_ref[c] for c in chains]
    y2 = [_dot(qy[c][:, :PAIR], s0[c]) + qy[c][:, PAIR:] for c in chains]
    for c in chains:
        state_ref[c] = _dot(g_t[c], s0[c]) + h_t[c]

    segm = segm_ref[...]
    for bi in range(nb):
        y = jnp.concatenate([y2[bi * npair + p][:C] + y2[bi * npair + p][C:] for p in range(npair)], axis=1)
        mean = _segsum(y, segm)
        yc = y - mean
        var = _segsum(yc * yc, segm)
        yn = yc * lax.rsqrt(var + RW_GN_EPS) * gnw_ref[...] + gnb_ref[...]
        y_ref[bi] = ((yn + bonus_ref[bi]) * g_ref[bi]).astype(y_ref.dtype)


def _wkv(r, lw, k2, v, kk, alr, g, bonus, gnw, gnb, tri, segm, B, S):
    T, width = r.shape
    C = WKV_CHUNK
    nb = WKV_BATCH
    nc = S // C
    fixed = lambda b, c: (0, 0)
    blk = pl.BlockSpec((nb, C, width), lambda b, c: (b, c, 0))
    vec = pl.BlockSpec((1, width), fixed)
    ins = [t.reshape(B, S, width) for t in (r, lw, k2, v, kk, alr, g, bonus)]
    y = pl.pallas_call(
        _wkv_kernel,
        grid=(B // nb, nc),
        in_specs=[blk] * 8 + [vec, vec, pl.BlockSpec((C, C), fixed), pl.BlockSpec((width, width), fixed)],
        out_specs=blk,
        out_shape=jax.ShapeDtypeStruct((B, S, width), BF16),
        scratch_shapes=[pltpu.VMEM((nb * width // PAIR, PAIR, PAIR), F32)],
        compiler_params=_cparams(("parallel", "arbitrary")),
        name="wkv",
    )(*ins, gnw, gnb, tri, segm)
    return y.reshape(T, width)


def _split3(x):
    hi = x.astype(BF16).astype(F32)
    mid = (x - hi).astype(BF16).astype(F32)
    return hi, mid, x - hi - mid


def _spread_heads(x):
    lane = lax.broadcasted_iota(jnp.int32, (x.shape[0], PAIR), 1)
    groups = []
    for p in range(x.shape[1] // PAIR):
        blk = x[:, p * PAIR:(p + 1) * PAIR]
        groups.append(jnp.where(lane < HEAD_DIM, blk, 0.0))
        groups.append(jnp.where(lane < HEAD_DIM, pltpu.roll(blk, HEAD_DIM, axis=1), 0.0))
    return jnp.concatenate(groups, axis=1)


def _fx_prep_kernel(q_ref, k_ref, v_ref, fzt_ref, bf_ref, qw_ref, kw_ref, segm_ref, tril_ref,
                    place_ref, oneq_ref, onek_ref, onev_ref, qa_ref, ka_ref, va_ref, carry_ref):
    @pl.when(pl.program_id(1) == 0)
    def _():
        carry_ref[...] = jnp.zeros_like(carry_ref)

    segm = segm_ref[...]
    q = q_ref[...]
    k = k_ref[...]
    tm = q.shape[0]
    qn = q * lax.rsqrt(_segsum(q * q, segm) + QK_EPS) * (qw_ref[...] * (HEAD_DIM ** -0.5 * LOG2E))
    kn = k * lax.rsqrt(_segsum(k * k, segm) + QK_EPS) * kw_ref[...]

    lf = -_softplus(-(fzt_ref[...] + bf_ref[...]))
    c = lax.dot_general(tril_ref[...], lf, (((1,), (1,)), ((), ())), precision=HIGHEST,
                        preferred_element_type=F32) + carry_ref[...]
    carry_ref[...] = c[tm - 1:tm, :]
    placed = jnp.dot(c * LOG2E, place_ref[...], precision=HIGHEST, preferred_element_type=F32)
    hi, mid, lo = _split3(placed)
    lane = lax.broadcasted_iota(jnp.int32, placed.shape, 1) % PAIR
    c_q = jnp.where(lane == HEAD_DIM + 3, hi, jnp.where(lane == HEAD_DIM + 4, mid,
                                                        jnp.where(lane == HEAD_DIM + 5, lo, 0.0)))
    c_k = jnp.where(lane == HEAD_DIM, hi, jnp.where(lane == HEAD_DIM + 1, mid,
                                                    jnp.where(lane == HEAD_DIM + 2, lo, 0.0)))
    qa_ref[...] = (_spread_heads(qn) + c_q + oneq_ref[...]).astype(qa_ref.dtype)
    ka_ref[...] = (_spread_heads(kn) - c_k + onek_ref[...]).astype(ka_ref.dtype)
    va_ref[...] = (_spread_heads(v_ref[...]) + onev_ref[...]).astype(va_ref.dtype)


def _fx_prep(p_fx, fzt, bf, qw, kw, segm, tril, B, S, width):
    T = p_fx.shape[0]
    nh = fzt.shape[0]
    wide = nh * PAIR
    tm = ROW_TILE
    ns = S // tm
    fixed = lambda b, s: (0, 0)
    col = lambda j: pl.BlockSpec((tm, width), lambda b, s: (b * ns + s, j))
    out = pl.BlockSpec((tm, wide), lambda b, s: (b * ns + s, 0))
    vec = pl.BlockSpec((1, width), fixed)
    wvec = pl.BlockSpec((1, wide), fixed)

    lane = jnp.arange(wide) % PAIR
    head = jnp.arange(wide) // PAIR
    place = ((head[None, :] == jnp.arange(nh)[:, None]) & (lane >= HEAD_DIM) & (lane < HEAD_DIM + 6)).astype(F32)
    one_q = ((lane >= HEAD_DIM) & (lane < HEAD_DIM + 3)).astype(F32).reshape(1, wide)
    one_k = ((lane >= HEAD_DIM + 3) & (lane < HEAD_DIM + 6)).astype(F32).reshape(1, wide)
    one_v = (lane >= HEAD_DIM).astype(F32).reshape(1, wide)
    return pl.pallas_call(
        _fx_prep_kernel,
        grid=(B, ns),
        in_specs=[
            col(0), col(1), col(2),
            pl.BlockSpec((nh, tm), lambda b, s: (0, b * ns + s)),
            pl.BlockSpec((nh, 1), fixed),
            vec, vec,
            pl.BlockSpec((width, width), fixed),
            pl.BlockSpec((tm, tm), fixed),
            pl.BlockSpec((nh, wide), fixed),
            wvec, wvec, wvec,
        ],
        out_specs=[out, out, out],
        out_shape=[jax.ShapeDtypeStruct((T, wide), BF16)] * 3,
        scratch_shapes=[pltpu.VMEM((1, nh), F32)],
        compiler_params=_cparams(("parallel", "arbitrary")),
        name="fx_prep",
    )(p_fx, p_fx, p_fx, fzt, bf, qw, kw, segm, tril, place, one_q, one_k, one_v)


def _fox_attn_kernel(qi_ref, kj_ref, q_ref, k_ref, v_ref, og_ref, o_ref, m_ref, acc_ref):
    i = qi_ref[pl.program_id(2)]
    j = kj_ref[pl.program_id(2)]
    tq = q_ref.shape[0]
    tk = k_ref.shape[0]
    heads = range(ATTN_HEADS)
    grp = lambda ref, h: ref[:, h * PAIR:(h + 1) * PAIR]

    @pl.when(j == 0)
    def _():
        m_ref[...] = jnp.full_like(m_ref, NEG_BIG)
        acc_ref[...] = jnp.zeros_like(acc_ref)

    def step(masked):
        s = [lax.dot_general(grp(q_ref, h), grp(k_ref, h), (((1,), (1,)), ((), ())), preferred_element_type=F32)
             for h in heads]
        if masked:
            row = lax.broadcasted_iota(jnp.int32, (tq, tk), 0)
            col = lax.broadcasted_iota(jnp.int32, (tq, tk), 1)
            s = [jnp.where(col <= row, s[h], NEG_BIG) for h in heads]
        m_old = [m_ref[h] for h in heads]
        m_new = [jnp.maximum(m_old[h], jnp.max(s[h], axis=1, keepdims=True)) for h in heads]
        alpha = [jnp.exp2(m_old[h] - m_new[h]) for h in heads]
        pr = [jnp.exp2(s[h] - jnp.concatenate([m_new[h]] * (tk // PAIR), axis=1)).astype(BF16) for h in heads]
        pv = [jnp.dot(pr[h], grp(v_ref, h), preferred_element_type=F32) for h in heads]
        for h in heads:
            acc_ref[h] = alpha[h] * acc_ref[h] + pv[h]
            m_ref[h] = m_new[h]

    @pl.when(j < i)
    def _():
        step(False)

    @pl.when(j == i)
    def _():
        step(True)
        lane = lax.broadcasted_iota(jnp.int32, (tq, PAIR), 1)
        outs = []
        for p in range(ATTN_HEADS // 2):
            a0 = acc_ref[2 * p]
            a1 = acc_ref[2 * p + 1]
            o0 = a0 * pltpu.roll(1.0 / a0, HEAD_DIM, axis=1)
            o1 = pltpu.roll(a1, HEAD_DIM, axis=1) * (1.0 / a1)
            outs.append(jnp.where(lane < HEAD_DIM, o0, o1))
        o = jnp.concatenate(outs, axis=1)
        o_ref[...] = (o * _sigmoid(og_ref[...])).astype(o_ref.dtype)


def _fox_attn(qa, ka, va, p_fx, B, S, width):
    T = qa.shape[0]
    tq, tk = ATTN_TQ, ATTN_TK
    assert tq == tk
    nq = S // tq
    nh = ATTN_HEADS
    ow = nh * HEAD_DIM
    ngroup = width // ow
    og_col0 = 3 * width // ow
    pairs = [(i, j) for i in range(nq) for j in range(i + 1)]
    qi = jnp.array([ij[0] for ij in pairs], jnp.int32)
    kj = jnp.array([ij[1] for ij in pairs], jnp.int32)
    kv = pl.BlockSpec((tk, nh * PAIR), lambda b, p, t, qi, kj: (b * nq + kj[t], p))
    grid_spec = pltpu.PrefetchScalarGridSpec(
        num_scalar_prefetch=2,
        grid=(B, ngroup, len(pairs)),
        in_specs=[
            pl.BlockSpec((tq, nh * PAIR), lambda b, p, t, qi, kj: (b * nq + qi[t], p)), kv, kv,
            pl.BlockSpec((tq, ow), lambda b, p, t, qi, kj: (b * nq + qi[t], og_col0 + p)),
        ],
        out_specs=pl.BlockSpec((tq, ow), lambda b, p, t, qi, kj: (b * nq + qi[t], p)),
        scratch_shapes=[pltpu.VMEM((nh, tq, PAIR), F32), pltpu.VMEM((nh, tq, PAIR), F32)],
    )
    return pl.pallas_call(
        _fox_attn_kernel,
        grid_spec=grid_spec,
        out_shape=jax.ShapeDtypeStruct((T, width), BF16),
        compiler_params=_cparams(("parallel", "parallel", "arbitrary")),
        name="fox_attn",
    )(qi, kj, qa, ka, va, p_fx)


def _out_ln_kernel(yrw_ref, yfx_ref, h_ref, wo1_ref, wo2_ref, lnw_ref, lnb_ref, rw_ref, rb_ref,
                   h1_ref, logit_ref, *, alpha):
    mix = (jnp.dot(yrw_ref[...], wo1_ref[...], preferred_element_type=F32)
           + jnp.dot(yfx_ref[...], wo2_ref[...], preferred_element_type=F32))
    h1 = _layer_norm(alpha * h_ref[...] + mix, lnw_ref[...], lnb_ref[...])
    h1_ref[...] = h1
    ne = rb_ref.shape[1]
    h_hi = h1.astype(BF16)
    h_lo = (h1 - h_hi.astype(F32)).astype(BF16)
    rw = rw_ref[...]
    first = jnp.dot(h_hi, rw, preferred_element_type=F32)
    second = jnp.dot(h_lo, rw[:, :ne], preferred_element_type=F32)
    logit_ref[...] = first[:, :ne] + first[:, ne:] + second + rb_ref[...]


def _out_ln(y_rw, y_fx, h0, wo1, wo2, lnw, lnb, rw, rb, alpha):
    T, D = h0.shape
    width = y_rw.shape[1]
    ne = rb.shape[1]
    tm = ROW_TILE
    row = lambda i: (i, 0)
    fixed = lambda i: (0, 0)
    rw_hi = rw.astype(BF16)
    rw_lo = (rw - rw_hi.astype(F32)).astype(BF16)
    rw = jnp.concatenate([rw_hi, rw_lo], axis=1)
    return pl.pallas_call(
        functools.partial(_out_ln_kernel, alpha=alpha),
        grid=(T // tm,),
        in_specs=[
            pl.BlockSpec((tm, width), row), pl.BlockSpec((tm, width), row), pl.BlockSpec((tm, D), row),
            pl.BlockSpec((width, D), fixed), pl.BlockSpec((width, D), fixed),
            pl.BlockSpec((1, D), fixed), pl.BlockSpec((1, D), fixed),
            pl.BlockSpec((D, 2 * ne), fixed), pl.BlockSpec((1, ne), fixed),
        ],
        out_specs=[pl.BlockSpec((tm, D), row), pl.BlockSpec((tm, ne), row)],
        out_shape=[jax.ShapeDtypeStruct((T, D), F32), jax.ShapeDtypeStruct((T, ne), F32)],
        compiler_params=_cparams(("parallel",)),
        name="out_ln",
    )(y_rw, y_fx, h0, wo1, wo2, lnw, lnb, rw, rb)


DEINT_COLS = 256
MOE_VMEM_LIMIT = 56 * 1024 * 1024

def _pack_bf16_pairs(x):
    n = x.shape[1] // 2
    bits = pltpu.bitcast(x.astype(BF16).astype(F32), jnp.uint32)
    return lax.shift_right_logical(bits[:, :n], jnp.uint32(16)) | bits[:, n:]


def _unpack_bf16_pairs(u):
    lo = pltpu.bitcast(lax.shift_left(u, jnp.uint32(16)), F32)
    hi = pltpu.bitcast(u & jnp.uint32(0xFFFF0000), F32)
    return jnp.concatenate([lo, hi], axis=1)


def _moe_kernel(bexp_ref, nused_ref, first_ref, nexte_ref, slot_ref,
                x_ref, w1_hbm, w2_hbm, b1g_ref, b1l_ref, b2_ref, perm_ref, o_ref,
                w1_stage, w2_stage, w1g_bf, w1l_bf, w2_bf, sem):
    i = pl.program_id(0)
    live = i < nused_ref[0]

    def weight_copies(e, s):
        return (pltpu.make_async_copy(w1_hbm.at[e], w1_stage.at[s], sem.at[0, s]),
                pltpu.make_async_copy(w2_hbm.at[e], w2_stage.at[s], sem.at[1, s]))

    @pl.when(live & (i == 0))
    def _():
        for cp in weight_copies(bexp_ref[0], 0):
            cp.start()

    @pl.when(live & (first_ref[i] == 1))
    def _():
        s = slot_ref[i]
        for cp in weight_copies(bexp_ref[i], s):
            cp.wait()

        @pl.when(nexte_ref[i] >= 0)
        def _():
            for cp in weight_copies(nexte_ref[i], 1 - s):
                cp.start()

        half = DEINT_COLS // 2
        for c in range(w1_stage.shape[2] // DEINT_COLS):
            blk = w1_stage[s, :, c * DEINT_COLS:(c + 1) * DEINT_COLS].astype(BF16)
            out = jnp.dot(blk, perm_ref[...], preferred_element_type=F32).astype(BF16)
            w1g_bf[:, c * half:(c + 1) * half] = out[:, :half]
            w1l_bf[:, c * half:(c + 1) * half] = out[:, half:]
        w2_bf[...] = w2_stage[s].astype(BF16)

    @pl.when(live)
    def _():
        x = x_ref[...].astype(BF16)
        x_glu = jnp.minimum(jnp.dot(x, w1g_bf[...], preferred_element_type=F32) + b1g_ref[0], SWIGLU_LIMIT)
        x_lin = jnp.clip(jnp.dot(x, w1l_bf[...], preferred_element_type=F32) + b1l_ref[0],
                         -SWIGLU_LIMIT, SWIGLU_LIMIT)
        act = x_glu * _sigmoid(SWIGLU_ALPHA * x_glu) * (x_lin + 1.0)
        o = jnp.dot(act.astype(BF16), w2_bf[...], preferred_element_type=F32) + b2_ref[0]
        o_ref[...] = _pack_bf16_pairs(o)

    @pl.when(jnp.logical_not(live))
    def _():
        o_ref[...] = jnp.zeros_like(o_ref)


def _moe_ffn(block_exp, n_used, xs, w1, b1g, b1l, w2, b2):
    n_rows, D = xs.shape
    E, _, F2 = w1.shape
    F = F2 // 2
    tm = MOE_TILE
    n_blocks = n_rows // tm

    idx = jnp.arange(n_blocks, dtype=jnp.int32)
    first = jnp.concatenate([jnp.ones((1,), jnp.bool_), block_exp[1:] != block_exp[:-1]])
    slot = ((jnp.cumsum(first.astype(jnp.int32)) - 1) % 2).astype(jnp.int32)
    cand = jnp.where(first & (idx < n_used[0]), idx, n_blocks)
    next_first = jnp.min(jnp.where(idx[None, :] > idx[:, None], cand[None, :], n_blocks), axis=1)
    next_e = jnp.where(next_first < n_blocks, block_exp[jnp.minimum(next_first, n_blocks - 1)], -1).astype(jnp.int32)

    half = DEINT_COLS // 2
    src = jnp.arange(DEINT_COLS)
    dst = jnp.where(src % 2 == 0, src // 2, half + src // 2)
    perm = (dst[:, None] == jnp.arange(DEINT_COLS)[None, :]).astype(BF16)

    live = lambda i, be, nu: jnp.minimum(i, nu[0] - 1)
    bspec = lambda n: pl.BlockSpec((1, 1, n), lambda i, be, nu, fi, ne, sl: (be[live(i, be, nu)], 0, 0))
    grid_spec = pltpu.PrefetchScalarGridSpec(
        num_scalar_prefetch=5,
        grid=(n_blocks,),
        in_specs=[
            pl.BlockSpec((tm, D), lambda i, be, nu, fi, ne, sl: (live(i, be, nu), 0)),
            pl.BlockSpec(memory_space=pl.ANY),
            pl.BlockSpec(memory_space=pl.ANY),
            bspec(F), bspec(F), bspec(D),
            pl.BlockSpec((DEINT_COLS, DEINT_COLS), lambda i, be, nu, fi, ne, sl: (0, 0)),
        ],
        out_specs=pl.BlockSpec((tm, D // 2), lambda i, be, nu, fi, ne, sl: (i, 0)),
        scratch_shapes=[
            pltpu.VMEM((2, D, F2), F32), pltpu.VMEM((2, F, D), F32),
            pltpu.VMEM((D, F), BF16), pltpu.VMEM((D, F), BF16), pltpu.VMEM((F, D), BF16),
            pltpu.SemaphoreType.DMA((2, 2)),
        ],
    )
    return pl.pallas_call(
        _moe_kernel,
        grid_spec=grid_spec,
        out_shape=jax.ShapeDtypeStruct((n_rows, D // 2), jnp.uint32),
        compiler_params=pltpu.CompilerParams(dimension_semantics=("arbitrary",), vmem_limit_bytes=MOE_VMEM_LIMIT),
        name="moe_ffn",
    )(block_exp, n_used, first.astype(jnp.int32), next_e, slot, xs, w1, w2, b1g, b1l, b2, perm)


def _combine_ln_kernel(h_ref, eo_ref, gate_ref, lnw_ref, lnb_ref, o_ref, *, alpha):
    gates = gate_ref[...]
    ffn = _unpack_bf16_pairs(eo_ref[0]) * gates[:, 0:1]
    for j in range(1, TOP_K):
        ffn = ffn + _unpack_bf16_pairs(eo_ref[j]) * gates[:, j:j + 1]
    o_ref[...] = _layer_norm(alpha * h_ref[...] + ffn, lnw_ref[...], lnb_ref[...])


def _combine_ln(h1, eo, gates, lnw, lnb, alpha):
    T, D = h1.shape
    tm = ROW_TILE
    return pl.pallas_call(
        functools.partial(_combine_ln_kernel, alpha=alpha),
        grid=(T // tm,),
        in_specs=[
            pl.BlockSpec((tm, D), lambda i: (i, 0)),
            pl.BlockSpec((TOP_K, tm, D // 2), lambda i: (0, i, 0)),
            pl.BlockSpec((tm, TOP_K), lambda i: (i, 0)),
            pl.BlockSpec((1, D), lambda i: (0, 0)),
            pl.BlockSpec((1, D), lambda i: (0, 0)),
        ],
        out_specs=pl.BlockSpec((tm, D), lambda i: (i, 0)),
        out_shape=jax.ShapeDtypeStruct((T, D), F32),
        compiler_params=_cparams(("parallel",)),
        name="combine_ln",
    )(h1, eo, gates, lnw, lnb)


def _pad_to(x, n, axis):
    pad = [(0, 0)] * x.ndim
    pad[axis] = (0, n - x.shape[axis])
    return jnp.pad(x, pad)


def _block_diag_ones(width, value=1.0):
    idx = jnp.arange(width) // HEAD_DIM
    return jnp.where(idx[:, None] == idx[None, :], value, 0.0).astype(BF16)


def _route(logits, n_experts, tile):
    T = logits.shape[0]
    top_val, top_idx = lax.top_k(logits[:, :n_experts], TOP_K)
    gates = jax.nn.softmax(top_val, axis=-1)
    e_flat = top_idx.reshape(-1).astype(jnp.int32)
    n_assign = T * TOP_K
    n_rows = n_assign + n_experts * tile
    n_blocks = n_rows // tile
    assert n_experts * n_assign < 2 ** 31
    eids = jnp.arange(n_experts, dtype=jnp.int32)
    aids = jnp.arange(n_assign, dtype=jnp.int32)
    skeys = lax.sort(e_flat * n_assign + aids)
    order = skeys % n_assign
    e_sorted = skeys // n_assign
    counts = jnp.sum((e_flat[:, None] == eids[None, :]).astype(jnp.int32), axis=0)
    starts = jnp.cumsum(counts) - counts
    padded = (counts + tile - 1) // tile * tile
    pends = jnp.cumsum(padded)
    pstarts = pends - padded
    shift = pstarts - starts
    dest_sorted = aids + jnp.sum(jnp.where(e_sorted[:, None] == eids[None, :], shift[None, :], 0), axis=1)
    _, dest = lax.sort((order, dest_sorted), num_keys=1)
    block_start = jnp.arange(n_blocks, dtype=jnp.int32) * tile
    block_exp = jnp.minimum(jnp.sum((pends[None, :] <= block_start[:, None]).astype(jnp.int32), axis=1),
                            n_experts - 1).astype(jnp.int32)
    n_used = (pends[-1] // tile).astype(jnp.int32).reshape(1)
    of_block = lambda per_expert: jnp.sum(
        jnp.where(block_exp[:, None] == eids[None, :], per_expert[None, :], 0), axis=1)
    in_exp = block_start - of_block(pstarts)
    n_valid = of_block(counts) - in_exp
    within = jnp.arange(tile, dtype=jnp.int32)[None, :]
    src = jnp.clip((of_block(starts) + in_exp)[:, None] + within, 0, n_assign - 1)
    row_tok = jnp.where(within < n_valid[:, None], order[src.reshape(-1)].reshape(n_blocks, tile) // TOP_K, 0)
    return gates, dest.reshape(T, TOP_K), row_tok.reshape(-1).astype(jnp.int32), block_exp, n_used


def kernel(x, ln_in_w, ln_in_b, w_in, rw_mu, rw_w0, rw_w2, rw_a0, rw_a2, rw_g2, rw_k_k, rw_k_a, rw_r_k,
           rw_gn_w, rw_gn_b, fx_b_f, fx_q_norm, fx_k_norm, w_o, ln1_w, ln1_b, router_w, router_b,
           exp_w1, exp_b1, exp_w2, exp_b2, ln2_w, ln2_b):
    B, S, D = x.shape
    T = B * S
    depth = w_in.shape[0]
    alpha = (2 * depth) ** 0.25
    rw_w = rw_w0.shape[1]
    fx_heads = fx_b_f.shape[1]
    fx_w = fx_heads * HEAD_DIM
    d_lora, a_lora, g_lora = rw_w2.shape[1], rw_a2.shape[1], rw_g2.shape[1]
    n_lora = d_lora + a_lora + g_lora
    lora_pad = -(-n_lora // 128) * 128
    rw_cols = 3 * rw_w + n_lora
    n_rw = 3 * rw_w + lora_pad
    n_experts = router_w.shape[2]
    ne_pad = -(-n_experts // 128) * 128
    row = lambda a: a.reshape(1, -1)

    seg_rw = _block_diag_ones(rw_w)
    segm_rw = _block_diag_ones(rw_w, 1.0 / HEAD_DIM)
    segm_fx = _block_diag_ones(fx_w, 1.0 / HEAD_DIM)
    tidx = jnp.arange(ROW_TILE)
    tril = (tidx[:, None] >= tidx[None, :]).astype(F32)
    cidx = jnp.arange(WKV_CHUNK)
    tri_c = (cidx[:, None] >= cidx[None, :]).astype(F32)

    assert depth == 1, "single-layer block"
    l = 0
    x2 = x.reshape(T, D)
    w_l = w_in[l]
    w_main = jnp.concatenate(
        [_pad_to(w_l[:, :rw_cols], n_rw, 1), w_l[:, rw_cols:rw_cols + 4 * fx_w]], axis=1).astype(BF16)
    wfz_t = w_l[:, rw_cols + 4 * fx_w:].T
    h0, p_rw, p_fx, fzt = _ln_proj(x2, row(ln_in_w), row(ln_in_b), w_main, wfz_t, n_rw)

    mu = _pad_to(row(rw_mu[l]), n_rw, 1)
    w2p = _pad_to(rw_w2[l], lora_pad, 0).astype(BF16)
    a2p = _pad_to(jnp.pad(rw_a2[l], ((d_lora, 0), (0, 0))), lora_pad, 0).astype(BF16)
    g2p = _pad_to(jnp.pad(rw_g2[l], ((d_lora + a_lora, 0), (0, 0))), lora_pad, 0).astype(BF16)
    r, lw, k2, v, kk, alr, g, bonus = _rw_prep(
        p_rw, B, S, rw_w, mu, row(rw_w0[l]), w2p, row(rw_a0[l]), a2p, g2p,
        row(rw_k_k[l]), row(rw_k_a[l]), row(rw_r_k[l]), seg_rw)
    y_rw = _wkv(r, lw, k2, v, kk, alr, g, bonus, row(rw_gn_w[l]), row(rw_gn_b[l]), tri_c, segm_rw, B, S)

    qw = row(jnp.tile(fx_q_norm[l], fx_heads))
    kw = row(jnp.tile(fx_k_norm[l], fx_heads))
    qa, ka, va = _fx_prep(p_fx, fzt, fx_b_f[l].reshape(-1, 1), qw, kw, segm_fx, tril, B, S, fx_w)
    y_fx = _fox_attn(qa, ka, va, p_fx, B, S, fx_w)

    wo = w_o[l].astype(BF16)
    rw_pad = _pad_to(router_w[l], ne_pad, 1)
    rb_pad = _pad_to(row(router_b[l]), ne_pad, 1)
    h1, logits = _out_ln(y_rw, y_fx, h0, wo[:rw_w], wo[rw_w:], row(ln1_w[l]), row(ln1_b[l]),
                         rw_pad, rb_pad, alpha)

    gates, pos, row_tok, block_exp, n_used = _route(logits, n_experts, MOE_TILE)
    xs = h1[row_tok]
    b1 = exp_b1[l]
    b1g = b1[:, None, 0::2]
    b1l = b1[:, None, 1::2]
    eo_rows = _moe_ffn(block_exp, n_used, xs, exp_w1[l], b1g, b1l, exp_w2[l], exp_b2[l][:, None, :])
    eo = eo_rows[pos.T]
    h = _combine_ln(h1, eo, gates, row(ln2_w[l]), row(ln2_b[l]), alpha)
    return h.reshape(B, S, D)
```

```python
import functools

import jax
import jax.numpy as jnp
from jax import lax
from jax.experimental import pallas as pl
from jax.experimental.pallas import tpu as pltpu

F32 = jnp.float32
BF16 = jnp.bfloat16
HIGHEST = lax.Precision.HIGHEST

HEAD_DIM = 64
PAIR = 2 * HEAD_DIM
WKV_CHUNK = 64
RW_GN_EPS = 64e-5
QK_EPS = 1e-6
LN_EPS = 1e-5
TOP_K = 4
SWIGLU_ALPHA = 1.702
SWIGLU_LIMIT = 7.0
NEG_BIG = -1e30
LOG2E = 1.4426950408889634

ROW_TILE = 256
ATTN_TQ = 512
ATTN_TK = 512
ATTN_HEADS = 8
MOE_TILE = 512
WKV_BATCH = 4
VMEM_LIMIT = 48 * 1024 * 1024


def _cparams(sem):
    return pltpu.CompilerParams(dimension_semantics=sem, vmem_limit_bytes=VMEM_LIMIT)


def _dot(a, b):
    return jnp.dot(a.astype(BF16), b.astype(BF16), preferred_element_type=F32)


def _dot_t(a, b):
    return lax.dot_general(a.astype(BF16), b.astype(BF16), (((1,), (1,)), ((), ())),
                           preferred_element_type=F32)


def _segsum(x, seg):
    hi = x.astype(BF16)
    lo = (x - hi.astype(F32)).astype(BF16)
    return (jnp.dot(hi, seg, preferred_element_type=F32) + jnp.dot(lo, seg, preferred_element_type=F32))


def _sigmoid(x):
    return 1.0 / (1.0 + jnp.exp(-x))


def _softplus(x):
    return jnp.maximum(x, 0.0) + jnp.log(1.0 + jnp.exp(-jnp.abs(x)))


def _layer_norm(x, w, b):
    mu = jnp.mean(x, axis=-1, keepdims=True)
    xc = x - mu
    var = jnp.mean(xc * xc, axis=-1, keepdims=True)
    return xc * lax.rsqrt(var + LN_EPS) * w + b


def _pack_bf16_pairs(x):
    n = x.shape[1] // 2
    bits = pltpu.bitcast(x.astype(BF16).astype(F32), jnp.uint32)
    return lax.shift_right_logical(bits[:, :n], jnp.uint32(16)) | bits[:, n:]


def _unpack_bf16_pairs(u):
    lo = pltpu.bitcast(lax.shift_left(u, jnp.uint32(16)), F32)
    hi = pltpu.bitcast(u & jnp.uint32(0xFFFF0000), F32)
    return jnp.concatenate([lo, hi], axis=1)


def _ln_proj_kernel(x_ref, lnw_ref, lnb_ref, w_ref, wfz_ref, h_ref, prw_ref, pfx_ref, fzt_ref, *, n_rw):
    h = _layer_norm(x_ref[...], lnw_ref[...], lnb_ref[...])
    h_ref[...] = h
    p = jnp.dot(h.astype(BF16), w_ref[...], preferred_element_type=F32)
    prw_ref[...] = p[:, :n_rw]
    pfx_ref[...] = p[:, n_rw:]
    fzt_ref[...] = lax.dot_general(wfz_ref[...], h, (((1,), (1,)), ((), ())),
                                   precision=HIGHEST, preferred_element_type=F32)


def _ln_proj(x2, lnw, lnb, w_main, wfz_t, n_rw):
    T, D = x2.shape
    n_all = w_main.shape[1]
    n_fx = n_all - n_rw
    nh = wfz_t.shape[0]
    tm = ROW_TILE
    return pl.pallas_call(
        functools.partial(_ln_proj_kernel, n_rw=n_rw),
        grid=(T // tm,),
        in_specs=[
            pl.BlockSpec((tm, D), lambda i: (i, 0)),
            pl.BlockSpec((1, D), lambda i: (0, 0)),
            pl.BlockSpec((1, D), lambda i: (0, 0)),
            pl.BlockSpec((D, n_all), lambda i: (0, 0)),
            pl.BlockSpec((nh, D), lambda i: (0, 0)),
        ],
        out_specs=[
            pl.BlockSpec((tm, D), lambda i: (i, 0)),
            pl.BlockSpec((tm, n_rw), lambda i: (i, 0)),
            pl.BlockSpec((tm, n_fx), lambda i: (i, 0)),
            pl.BlockSpec((nh, tm), lambda i: (0, i)),
        ],
        out_shape=[
            jax.ShapeDtypeStruct((T, D), F32),
            jax.ShapeDtypeStruct((T, n_rw), F32),
            jax.ShapeDtypeStruct((T, n_fx), F32),
            jax.ShapeDtypeStruct((nh, T), F32),
        ],
        compiler_params=_cparams(("parallel",)),
        name="ln_proj",
    )(x2, lnw, lnb, w_main, wfz_t)


def _rw_prep_kernel(p_ref, mu_ref, w0_ref, w2_ref, a0_ref, a2_ref, g2_ref, kkw_ref, ka_ref, rk_ref, seg_ref,
                    r_ref, lw_ref, k_ref, v_ref, kk_ref, alr_ref, g_ref, bonus_ref, carry_ref, *, width):
    @pl.when(pl.program_id(1) == 0)
    def _():
        carry_ref[...] = jnp.zeros_like(carry_ref)

    p = p_ref[...]
    tm = p.shape[0]
    prev = pltpu.roll(p, 1, axis=0)
    first_row = lax.broadcasted_iota(jnp.int32, p.shape, 0) == 0
    prev = jnp.where(first_row, carry_ref[...], prev)
    carry_ref[...] = p[tm - 1:tm, :]
    ps = p + mu_ref[...] * (prev - p)

    r = ps[:, 0:width]
    k = ps[:, width:2 * width]
    v = ps[:, 2 * width:3 * width]
    lora = ps[:, 3 * width:]
    seg = seg_ref[...]

    wl = w0_ref[...] + _dot(jnp.tanh(lora), w2_ref[...])
    w_raw = -_softplus(-wl) - 0.5
    lw_ref[...] = -jnp.exp(w_raw)
    alr = _sigmoid(a0_ref[...] + _dot(lora, a2_ref[...]))
    g_ref[...] = _dot(_sigmoid(lora), g2_ref[...])
    kkp = k * kkw_ref[...]
    nrm = jnp.sqrt(_segsum(kkp * kkp, seg))
    kk_ref[...] = kkp / jnp.maximum(nrm, 1e-12)
    k2 = k * (1.0 + (alr - 1.0) * ka_ref[...])
    bonus_ref[...] = _segsum(r * k2 * rk_ref[...], seg) * v
    r_ref[...] = r
    k_ref[...] = k2
    v_ref[...] = v
    alr_ref[...] = alr


def _rw_prep(p_rw, B, S, width, mu, w0, w2p, a0, a2p, g2p, kkw, ka, rk, seg):
    T, n_rw = p_rw.shape
    tm = ROW_TILE
    ns = S // tm
    n_lora = n_rw - 3 * width
    row = lambda b, s: (b * ns + s, 0)
    fixed = lambda b, s: (0, 0)
    vec = pl.BlockSpec((1, width), fixed)
    out = pl.BlockSpec((tm, width), row)
    return pl.pallas_call(
        functools.partial(_rw_prep_kernel, width=width),
        grid=(B, ns),
        in_specs=[
            pl.BlockSpec((tm, n_rw), row),
            pl.BlockSpec((1, n_rw), fixed),
            vec, pl.BlockSpec((n_lora, width), fixed),
            vec, pl.BlockSpec((n_lora, width), fixed),
            pl.BlockSpec((n_lora, width), fixed),
            vec, vec, vec,
            pl.BlockSpec((width, width), fixed),
        ],
        out_specs=[out] * 8,
        out_shape=[jax.ShapeDtypeStruct((T, width), F32)] * 8,
        scratch_shapes=[pltpu.VMEM((1, n_rw), F32)],
        compiler_params=_cparams(("parallel", "arbitrary")),
        name="rw_prep",
    )(p_rw, mu, w0, w2p, a0, a2p, g2p, kkw, ka, rk, seg)


def _stack_heads(x):
    lane = lax.broadcasted_iota(jnp.int32, x.shape, 1)
    return jnp.concatenate([jnp.where(lane < HEAD_DIM, x, 0.0), jnp.where(lane >= HEAD_DIM, x, 0.0)], axis=0)


def _wkv_kernel(r_ref, lw_ref, k_ref, v_ref, kk_ref, alr_ref, g_ref, bonus_ref, gnw_ref, gnb_ref, tri_ref,
                segm_ref, y_ref, state_ref):
    C = WKV_CHUNK
    nb, _, width = lw_ref.shape
    npair = width // PAIR

    @pl.when(pl.program_id(1) == 0)
    def _():
        state_ref[...] = jnp.zeros_like(state_ref)

    ri = lax.broadcasted_iota(jnp.int32, (2 * C, 2 * C), 0)
    ci = lax.broadcasted_iota(jnp.int32, (2 * C, 2 * C), 1)
    same = (ri // C) == (ci // C)
    strict = same & ((ci % C) < (ri % C))
    incl = same & ((ci % C) <= (ri % C))
    eye = (ri == ci).astype(F32)

    lhs, rhs, a2s, r2s, v2s, bhts, bkts, ptots = [], [], [], [], [], [], [], []
    for bi in range(nb):
        lw = lw_ref[bi]
        cum = jnp.dot(tri_ref[...], lw, precision=HIGHEST, preferred_element_type=F32)
        total = cum[C - 1:C, :]
        p_inv = jnp.exp(-cum)
        p_rem = jnp.exp(total - cum)
        p_tot = jnp.exp(total)
        kk = kk_ref[bi]
        k2 = k_ref[bi]
        b = kk * alr_ref[bi]
        a_t = -kk * jnp.exp(cum - lw)
        r_t = r_ref[bi] * jnp.exp(cum)
        b_t = b * p_inv
        k_t = k2 * p_inv
        b_h = b * p_rem
        k_h = k2 * p_rem
        v = v_ref[bi]
        for p in range(npair):
            sl = slice(p * PAIR, (p + 1) * PAIR)
            a2, r2, b2, kt2 = (_stack_heads(t[:, sl]) for t in (a_t, r_t, b_t, k_t))
            bh2, kh2, v2 = (_stack_heads(t[:, sl]) for t in (b_h, k_h, v))
            lhs.append(jnp.concatenate([a2, r2], axis=0))
            rhs.append(jnp.concatenate([b2, kt2], axis=0))
            a2s.append(a2)
            r2s.append(r2)
            v2s.append(v2)
            bhts.append(bh2.T)
            bkts.append(jnp.concatenate([bh2.T, kh2.T], axis=1))
            ptots.append(p_tot[:, sl])

    chains = range(nb * npair)
    m = [_dot_t(lhs[c], rhs[c]) for c in chains]
    n_ab = [jnp.where(strict, m[c][:2 * C, :2 * C], 0.0) for c in chains]
    m_ak = [jnp.where(strict, m[c][:2 * C, 2 * C:], 0.0) for c in chains]
    m_rb = [jnp.where(incl, m[c][2 * C:, :2 * C], 0.0) for c in chains]
    m_rk = [jnp.where(incl, m[c][2 * C:, 2 * C:], 0.0) for c in chains]
    mv = [_dot(m_ak[c], v2s[c]) for c in chains]
    mrkv = [_dot(m_rk[c], v2s[c]) for c in chains]
    inv = [eye + n_ab[c] for c in chains]
    pw = n_ab
    for _ in range(C.bit_length() - 2):
        pw = [_dot(pw[c], pw[c]) for c in chains]
        inv = [inv[c] + _dot(inv[c], pw[c]) for c in chains]
    wu = [_dot(inv[c], jnp.concatenate([a2s[c], mv[c]], axis=1)) for c in chains]
    qy = [_dot(m_rb[c], wu[c]) + jnp.concatenate([r2s[c], mrkv[c]], axis=1) for c in chains]
    g_t = [_dot(bhts[c], wu[c][:, :PAIR]) + eye * ptots[c] for c in chains]
    h_t = [_dot(bkts[c], jnp.concatenate([wu[c][:, PAIR:], v2s[c]], axis=0)) for c in chains]
    s0 = [state_ref[c] for c in chains]
    y2 = [_dot(qy[c][:, :PAIR], s0[c]) + qy[c][:, PAIR:] for c in chains]
    for c in chains:
        state_ref[c] = _dot(g_t[c], s0[c]) + h_t[c]

    segm = segm_ref[...]
    for bi in range(nb):
        y = jnp.concatenate([y2[bi * npair + p][:C] + y2[bi * npair + p][C:] for p in range(npair)], axis=1)
        mean = _segsum(y, segm)
        yc = y - mean
        var = _segsum(yc * yc, segm)
        yn = yc * lax.rsqrt(var + RW_GN_EPS) * gnw_ref[...] + gnb_ref[...]
        y_ref[bi] = ((yn + bonus_ref[bi]) * g_ref[bi]).astype(y_ref.dtype)


def _wkv(r, lw, k2, v, kk, alr, g, bonus, gnw, gnb, tri, segm, B, S):
    T, width = r.shape
    C = WKV_CHUNK
    nb = WKV_BATCH
    nc = S // C
    fixed = lambda b, c: (0, 0)
    blk = pl.BlockSpec((nb, C, width), lambda b, c: (b, c, 0))
    vec = pl.BlockSpec((1, width), fixed)
    ins = [t.reshape(B, S, width) for t in (r, lw, k2, v, kk, alr, g, bonus)]
    y = pl.pallas_call(
        _wkv_kernel,
        grid=(B // nb, nc),
        in_specs=[blk] * 8 + [vec, vec, pl.BlockSpec((C, C), fixed), pl.BlockSpec((width, width), fixed)],
        out_specs=blk,
        out_shape=jax.ShapeDtypeStruct((B, S, width), BF16),
        scratch_shapes=[pltpu.VMEM((nb * width // PAIR, PAIR, PAIR), F32)],
        compiler_params=_cparams(("parallel", "arbitrary")),
        name="wkv",
    )(*ins, gnw, gnb, tri, segm)
    return y.reshape(T, width)


def _split3(x):
    hi = x.astype(BF16).astype(F32)
    mid = (x - hi).astype(BF16).astype(F32)
    return hi, mid, x - hi - mid


def _spread_heads(x):
    lane = lax.broadcasted_iota(jnp.int32, (x.shape[0], PAIR), 1)
    groups = []
    for p in range(x.shape[1] // PAIR):
        blk = x[:, p * PAIR:(p + 1) * PAIR]
        groups.append(jnp.where(lane < HEAD_DIM, blk, 0.0))
        groups.append(jnp.where(lane < HEAD_DIM, pltpu.roll(blk, HEAD_DIM, axis=1), 0.0))
    return jnp.concatenate(groups, axis=1)


def _fx_prep_kernel(q_ref, k_ref, v_ref, fzt_ref, bf_ref, qw_ref, kw_ref, segm_ref, tril_ref,
                    place_ref, oneq_ref, onek_ref, onev_ref, qa_ref, ka_ref, va_ref, carry_ref):
    @pl.when(pl.program_id(1) == 0)
    def _():
        carry_ref[...] = jnp.zeros_like(carry_ref)

    segm = segm_ref[...]
    q = q_ref[...]
    k = k_ref[...]
    tm = q.shape[0]
    qn = q * lax.rsqrt(_segsum(q * q, segm) + QK_EPS) * (qw_ref[...] * (HEAD_DIM ** -0.5 * LOG2E))
    kn = k * lax.rsqrt(_segsum(k * k, segm) + QK_EPS) * kw_ref[...]

    lf = -_softplus(-(fzt_ref[...] + bf_ref[...]))
    c = lax.dot_general(tril_ref[...], lf, (((1,), (1,)), ((), ())), precision=HIGHEST,
                        preferred_element_type=F32) + carry_ref[...]
    carry_ref[...] = c[tm - 1:tm, :]
    placed = jnp.dot(c * LOG2E, place_ref[...], precision=HIGHEST, preferred_element_type=F32)
    hi, mid, lo = _split3(placed)
    lane = lax.broadcasted_iota(jnp.int32, placed.shape, 1) % PAIR
    c_q = jnp.where(lane == HEAD_DIM + 3, hi, jnp.where(lane == HEAD_DIM + 4, mid,
                                                        jnp.where(lane == HEAD_DIM + 5, lo, 0.0)))
    c_k = jnp.where(lane == HEAD_DIM, hi, jnp.where(lane == HEAD_DIM + 1, mid,
                                                    jnp.where(lane == HEAD_DIM + 2, lo, 0.0)))
    qa_ref[...] = (_spread_heads(qn) + c_q + oneq_ref[...]).astype(qa_ref.dtype)
    ka_ref[...] = (_spread_heads(kn) - c_k + onek_ref[...]).astype(ka_ref.dtype)
    va_ref[...] = (_spread_heads(v_ref[...]) + onev_ref[...]).astype(va_ref.dtype)


def _fx_prep(p_fx, fzt, bf, qw, kw, segm, tril, B, S, width):
    T = p_fx.shape[0]
    nh = fzt.shape[0]
    wide = nh * PAIR
    tm = ROW_TILE
    ns = S // tm
    fixed = lambda b, s: (0, 0)
    col = lambda j: pl.BlockSpec((tm, width), lambda b, s: (b * ns + s, j))
    out = pl.BlockSpec((tm, wide), lambda b, s: (b * ns + s, 0))
    vec = pl.BlockSpec((1, width), fixed)
    wvec = pl.BlockSpec((1, wide), fixed)

    lane = jnp.arange(wide) % PAIR
    head = jnp.arange(wide) // PAIR
    place = ((head[None, :] == jnp.arange(nh)[:, None]) & (lane >= HEAD_DIM) & (lane < HEAD_DIM + 6)).astype(F32)
    one_q = ((lane >= HEAD_DIM) & (lane < HEAD_DIM + 3)).astype(F32).reshape(1, wide)
    one_k = ((lane >= HEAD_DIM + 3) & (lane < HEAD_DIM + 6)).astype(F32).reshape(1, wide)
    one_v = (lane >= HEAD_DIM).astype(F32).reshape(1, wide)
    return pl.pallas_call(
        _fx_prep_kernel,
        grid=(B, ns),
        in_specs=[
            col(0), col(1), col(2),
            pl.BlockSpec((nh, tm), lambda b, s: (0, b * ns + s)),
            pl.BlockSpec((nh, 1), fixed),
            vec, vec,
            pl.BlockSpec((width, width), fixed),
            pl.BlockSpec((tm, tm), fixed),
            pl.BlockSpec((nh, wide), fixed),
            wvec, wvec, wvec,
        ],
        out_specs=[out, out, out],
        out_shape=[jax.ShapeDtypeStruct((T, wide), BF16)] * 3,
        scratch_shapes=[pltpu.VMEM((1, nh), F32)],
        compiler_params=_cparams(("parallel", "arbitrary")),
        name="fx_prep",
    )(p_fx, p_fx, p_fx, fzt, bf, qw, kw, segm, tril, place, one_q, one_k, one_v)


def _fox_attn_kernel(qi_ref, kj_ref, q_ref, k_ref, v_ref, og_ref, o_ref, m_ref, acc_ref):
    i = qi_ref[pl.program_id(2)]
    j = kj_ref[pl.program_id(2)]
    tq = q_ref.shape[0]
    tk = k_ref.shape[0]
    heads = range(ATTN_HEADS)
    grp = lambda ref, h: ref[:, h * PAIR:(h + 1) * PAIR]

    @pl.when(j == 0)
    def _():
        m_ref[...] = jnp.full_like(m_ref, NEG_BIG)
        acc_ref[...] = jnp.zeros_like(acc_ref)

    def step(masked):
        s = [lax.dot_general(grp(q_ref, h), grp(k_ref, h), (((1,), (1,)), ((), ())), preferred_element_type=F32)
             for h in heads]
        if masked:
            row = lax.broadcasted_iota(jnp.int32, (tq, tk), 0)
            col = lax.broadcasted_iota(jnp.int32, (tq, tk), 1)
            s = [jnp.where(col <= row, s[h], NEG_BIG) for h in heads]
        m_old = [m_ref[h] for h in heads]
        m_new = [jnp.maximum(m_old[h], jnp.max(s[h], axis=1, keepdims=True)) for h in heads]
        alpha = [jnp.exp2(m_old[h] - m_new[h]) for h in heads]
        pr = [jnp.exp2(s[h] - jnp.concatenate([m_new[h]] * (tk // PAIR), axis=1)).astype(BF16) for h in heads]
        pv = [jnp.dot(pr[h], grp(v_ref, h), preferred_element_type=F32) for h in heads]
        for h in heads:
            acc_ref[h] = alpha[h] * acc_ref[h] + pv[h]
            m_ref[h] = m_new[h]

    @pl.when(j < i)
    def _():
        step(False)

    @pl.when(j == i)
    def _():
        step(True)
        lane = lax.broadcasted_iota(jnp.int32, (tq, PAIR), 1)
        outs = []
        for p in range(ATTN_HEADS // 2):
            a0 = acc_ref[2 * p]
            a1 = acc_ref[2 * p + 1]
            o0 = a0 * pltpu.roll(1.0 / a0, HEAD_DIM, axis=1)
            o1 = pltpu.roll(a1, HEAD_DIM, axis=1) * (1.0 / a1)
            outs.append(jnp.where(lane < HEAD_DIM, o0, o1))
        o = jnp.concatenate(outs, axis=1)
        o_ref[...] = (o * _sigmoid(og_ref[...])).astype(o_ref.dtype)


def _fox_attn(qa, ka, va, p_fx, B, S, width):
    T = qa.shape[0]
    tq, tk = ATTN_TQ, ATTN_TK
    assert tq == tk
    nq = S // tq
    nh = ATTN_HEADS
    ow = nh * HEAD_DIM
    ngroup = width // ow
    og_col0 = 3 * width // ow
    pairs = [(i, j) for i in range(nq) for j in range(i + 1)]
    qi = jnp.array([ij[0] for ij in pairs], jnp.int32)
    kj = jnp.array([ij[1] for ij in pairs], jnp.int32)
    kv = pl.BlockSpec((tk, nh * PAIR), lambda b, p, t, qi, kj: (b * nq + kj[t], p))
    grid_spec = pltpu.PrefetchScalarGridSpec(
        num_scalar_prefetch=2,
        grid=(B, ngroup, len(pairs)),
        in_specs=[
            pl.BlockSpec((tq, nh * PAIR), lambda b, p, t, qi, kj: (b * nq + qi[t], p)), kv, kv,
            pl.BlockSpec((tq, ow), lambda b, p, t, qi, kj: (b * nq + qi[t], og_col0 + p)),
        ],
        out_specs=pl.BlockSpec((tq, ow), lambda b, p, t, qi, kj: (b * nq + qi[t], p)),
        scratch_shapes=[pltpu.VMEM((nh, tq, PAIR), F32), pltpu.VMEM((nh, tq, PAIR), F32)],
    )
    return pl.pallas_call(
        _fox_attn_kernel,
        grid_spec=grid_spec,
        out_shape=jax.ShapeDtypeStruct((T, width), BF16),
        compiler_params=_cparams(("parallel", "parallel", "arbitrary")),
        name="fox_attn",
    )(qi, kj, qa, ka, va, p_fx)


def _out_ln_kernel(yrw_ref, yfx_ref, h_ref, wo1_ref, wo2_ref, lnw_ref, lnb_ref, rw_ref, rb_ref,
                   h1_ref, h1p_ref, logit_ref, *, alpha):
    mix = (jnp.dot(yrw_ref[...], wo1_ref[...], preferred_element_type=F32)
           + jnp.dot(yfx_ref[...], wo2_ref[...], preferred_element_type=F32))
    h1 = _layer_norm(alpha * h_ref[...] + mix, lnw_ref[...], lnb_ref[...])
    h1_ref[...] = h1
    h1p_ref[...] = _pack_bf16_pairs(h1)
    ne = rb_ref.shape[1]
    h_hi = h1.astype(BF16)
    h_lo = (h1 - h_hi.astype(F32)).astype(BF16)
    rw = rw_ref[...]
    first = jnp.dot(h_hi, rw, preferred_element_type=F32)
    second = jnp.dot(h_lo, rw[:, :ne], preferred_element_type=F32)
    logit_ref[...] = first[:, :ne] + first[:, ne:] + second + rb_ref[...]


def _out_ln(y_rw, y_fx, h0, wo1, wo2, lnw, lnb, rw, rb, alpha):
    T, D = h0.shape
    width = y_rw.shape[1]
    ne = rb.shape[1]
    tm = ROW_TILE
    row = lambda i: (i, 0)
    fixed = lambda i: (0, 0)
    rw_hi = rw.astype(BF16)
    rw_lo = (rw - rw_hi.astype(F32)).astype(BF16)
    rw = jnp.concatenate([rw_hi, rw_lo], axis=1)
    return pl.pallas_call(
        functools.partial(_out_ln_kernel, alpha=alpha),
        grid=(T // tm,),
        in_specs=[
            pl.BlockSpec((tm, width), row), pl.BlockSpec((tm, width), row), pl.BlockSpec((tm, D), row),
            pl.BlockSpec((width, D), fixed), pl.BlockSpec((width, D), fixed),
            pl.BlockSpec((1, D), fixed), pl.BlockSpec((1, D), fixed),
            pl.BlockSpec((D, 2 * ne), fixed), pl.BlockSpec((1, ne), fixed),
        ],
        out_specs=[pl.BlockSpec((tm, D), row), pl.BlockSpec((tm, D // 2), row), pl.BlockSpec((tm, ne), row)],
        out_shape=[jax.ShapeDtypeStruct((T, D), F32), jax.ShapeDtypeStruct((T, D // 2), jnp.uint32),
                   jax.ShapeDtypeStruct((T, ne), F32)],
        compiler_params=_cparams(("parallel",)),
        name="out_ln",
    )(y_rw, y_fx, h0, wo1, wo2, lnw, lnb, rw, rb)


DEINT_COLS = 256
MOE_VMEM_LIMIT = 56 * 1024 * 1024


def _moe_kernel(bexp_ref, nused_ref, first_ref, nexte_ref, slot_ref,
                x_ref, w1_hbm, w2_hbm, b1g_ref, b1l_ref, b2_ref, perm_ref, o_ref,
                w1_stage, w2_stage, w1g_bf, w1l_bf, w2_bf, sem):
    i = pl.program_id(0)
    live = i < nused_ref[0]

    def weight_copies(e, s):
        return (pltpu.make_async_copy(w1_hbm.at[e], w1_stage.at[s], sem.at[0, s]),
                pltpu.make_async_copy(w2_hbm.at[e], w2_stage.at[s], sem.at[1, s]))

    @pl.when(live & (i == 0))
    def _():
        for cp in weight_copies(bexp_ref[0], 0):
            cp.start()

    @pl.when(live & (first_ref[i] == 1))
    def _():
        s = slot_ref[i]
        for cp in weight_copies(bexp_ref[i], s):
            cp.wait()

        @pl.when(nexte_ref[i] >= 0)
        def _():
            for cp in weight_copies(nexte_ref[i], 1 - s):
                cp.start()

        half = DEINT_COLS // 2
        for c in range(w1_stage.shape[2] // DEINT_COLS):
            blk = w1_stage[s, :, c * DEINT_COLS:(c + 1) * DEINT_COLS].astype(BF16)
            out = jnp.dot(blk, perm_ref[...], preferred_element_type=F32).astype(BF16)
            w1g_bf[:, c * half:(c + 1) * half] = out[:, :half]
            w1l_bf[:, c * half:(c + 1) * half] = out[:, half:]
        w2_bf[...] = w2_stage[s].astype(BF16)

    @pl.when(live)
    def _():
        x = _unpack_bf16_pairs(x_ref[...]).astype(BF16)
        x_glu = jnp.minimum(jnp.dot(x, w1g_bf[...], preferred_element_type=F32) + b1g_ref[0], SWIGLU_LIMIT)
        x_lin = jnp.clip(jnp.dot(x, w1l_bf[...], preferred_element_type=F32) + b1l_ref[0],
                         -SWIGLU_LIMIT, SWIGLU_LIMIT)
        act = x_glu * _sigmoid(SWIGLU_ALPHA * x_glu) * (x_lin + 1.0)
        o = jnp.dot(act.astype(BF16), w2_bf[...], preferred_element_type=F32) + b2_ref[0]
        o_ref[...] = _pack_bf16_pairs(o)

    @pl.when(jnp.logical_not(live))
    def _():
        o_ref[...] = jnp.zeros_like(o_ref)


def _moe_ffn(block_exp, n_used, xs, w1, b1g, b1l, w2, b2):
    n_rows = xs.shape[0]
    E, D, F2 = w1.shape
    F = F2 // 2
    tm = MOE_TILE
    n_blocks = n_rows // tm

    idx = jnp.arange(n_blocks, dtype=jnp.int32)
    first = jnp.concatenate([jnp.ones((1,), jnp.bool_), block_exp[1:] != block_exp[:-1]])
    slot = ((jnp.cumsum(first.astype(jnp.int32)) - 1) % 2).astype(jnp.int32)
    cand = jnp.where(first & (idx < n_used[0]), idx, n_blocks)
    next_first = jnp.min(jnp.where(idx[None, :] > idx[:, None], cand[None, :], n_blocks), axis=1)
    next_e = jnp.where(next_first < n_blocks, block_exp[jnp.minimum(next_first, n_blocks - 1)], -1).astype(jnp.int32)

    half = DEINT_COLS // 2
    src = jnp.arange(DEINT_COLS)
    dst = jnp.where(src % 2 == 0, src // 2, half + src // 2)
    perm = (dst[:, None] == jnp.arange(DEINT_COLS)[None, :]).astype(BF16)

    live = lambda i, be, nu: jnp.minimum(i, nu[0] - 1)
    bspec = lambda n: pl.BlockSpec((1, 1, n), lambda i, be, nu, fi, ne, sl: (be[live(i, be, nu)], 0, 0))
    grid_spec = pltpu.PrefetchScalarGridSpec(
        num_scalar_prefetch=5,
        grid=(n_blocks,),
        in_specs=[
            pl.BlockSpec((tm, D // 2), lambda i, be, nu, fi, ne, sl: (live(i, be, nu), 0)),
            pl.BlockSpec(memory_space=pl.ANY),
            pl.BlockSpec(memory_space=pl.ANY),
            bspec(F), bspec(F), bspec(D),
            pl.BlockSpec((DEINT_COLS, DEINT_COLS), lambda i, be, nu, fi, ne, sl: (0, 0)),
        ],
        out_specs=pl.BlockSpec((tm, D // 2), lambda i, be, nu, fi, ne, sl: (i, 0)),
        scratch_shapes=[
            pltpu.VMEM((2, D, F2), F32), pltpu.VMEM((2, F, D), F32),
            pltpu.VMEM((D, F), BF16), pltpu.VMEM((D, F), BF16), pltpu.VMEM((F, D), BF16),
            pltpu.SemaphoreType.DMA((2, 2)),
        ],
    )
    return pl.pallas_call(
        _moe_kernel,
        grid_spec=grid_spec,
        out_shape=jax.ShapeDtypeStruct((n_rows, D // 2), jnp.uint32),
        compiler_params=pltpu.CompilerParams(dimension_semantics=("arbitrary",), vmem_limit_bytes=MOE_VMEM_LIMIT),
        name="moe_ffn",
    )(block_exp, n_used, first.astype(jnp.int32), next_e, slot, xs, w1, w2, b1g, b1l, b2, perm)


def _combine_ln_kernel(h_ref, eo_ref, gate_ref, lnw_ref, lnb_ref, o_ref, *, alpha):
    gates = gate_ref[...]
    ffn = _unpack_bf16_pairs(eo_ref[0]) * gates[:, 0:1]
    for j in range(1, TOP_K):
        ffn = ffn + _unpack_bf16_pairs(eo_ref[j]) * gates[:, j:j + 1]
    o_ref[...] = _layer_norm(alpha * h_ref[...] + ffn, lnw_ref[...], lnb_ref[...])


def _combine_ln(h1, eo, gates, lnw, lnb, alpha):
    T, D = h1.shape
    tm = ROW_TILE
    return pl.pallas_call(
        functools.partial(_combine_ln_kernel, alpha=alpha),
        grid=(T // tm,),
        in_specs=[
            pl.BlockSpec((tm, D), lambda i: (i, 0)),
            pl.BlockSpec((TOP_K, tm, D // 2), lambda i: (0, i, 0)),
            pl.BlockSpec((tm, TOP_K), lambda i: (i, 0)),
            pl.BlockSpec((1, D), lambda i: (0, 0)),
            pl.BlockSpec((1, D), lambda i: (0, 0)),
        ],
        out_specs=pl.BlockSpec((tm, D), lambda i: (i, 0)),
        out_shape=jax.ShapeDtypeStruct((T, D), F32),
        compiler_params=_cparams(("parallel",)),
        name="combine_ln",
    )(h1, eo, gates, lnw, lnb)


def _pad_to(x, n, axis):
    pad = [(0, 0)] * x.ndim
    pad[axis] = (0, n - x.shape[axis])
    return jnp.pad(x, pad)


def _block_diag_ones(width, value=1.0):
    idx = jnp.arange(width) // HEAD_DIM
    return jnp.where(idx[:, None] == idx[None, :], value, 0.0).astype(BF16)


def _route(logits, n_experts, tile):
    T = logits.shape[0]
    top_val, top_idx = lax.top_k(logits[:, :n_experts], TOP_K)
    gates = jax.nn.softmax(top_val, axis=-1)
    e_flat = top_idx.reshape(-1).astype(jnp.int32)
    n_assign = T * TOP_K
    n_rows = n_assign + n_experts * tile
    n_blocks = n_rows // tile
    assert n_experts * n_assign < 2 ** 31
    eids = jnp.arange(n_experts, dtype=jnp.int32)
    aids = jnp.arange(n_assign, dtype=jnp.int32)
    skeys = lax.sort(e_flat * n_assign + aids)
    order = skeys % n_assign
    e_sorted = skeys // n_assign
    counts = jnp.sum((e_flat[:, None] == eids[None, :]).astype(jnp.int32), axis=0)
    starts = jnp.cumsum(counts) - counts
    padded = (counts + tile - 1) // tile * tile
    pends = jnp.cumsum(padded)
    pstarts = pends - padded
    shift = pstarts - starts
    dest_sorted = aids + jnp.sum(jnp.where(e_sorted[:, None] == eids[None, :], shift[None, :], 0), axis=1)
    _, dest = lax.sort((order, dest_sorted), num_keys=1)
    block_start = jnp.arange(n_blocks, dtype=jnp.int32) * tile
    block_exp = jnp.minimum(jnp.sum((pends[None, :] <= block_start[:, None]).astype(jnp.int32), axis=1),
                            n_experts - 1).astype(jnp.int32)
    n_used = (pends[-1] // tile).astype(jnp.int32).reshape(1)
    of_block = lambda per_expert: jnp.sum(
        jnp.where(block_exp[:, None] == eids[None, :], per_expert[None, :], 0), axis=1)
    in_exp = block_start - of_block(pstarts)
    n_valid = of_block(counts) - in_exp
    within = jnp.arange(tile, dtype=jnp.int32)[None, :]
    src = jnp.clip((of_block(starts) + in_exp)[:, None] + within, 0, n_assign - 1)
    row_tok = jnp.where(within < n_valid[:, None], order[src.reshape(-1)].reshape(n_blocks, tile) // TOP_K, 0)
    return gates, dest.reshape(T, TOP_K), row_tok.reshape(-1).astype(jnp.int32), block_exp, n_used


def kernel(x, ln_in_w, ln_in_b, w_in, rw_mu, rw_w0, rw_w2, rw_a0, rw_a2, rw_g2, rw_k_k, rw_k_a, rw_r_k,
           rw_gn_w, rw_gn_b, fx_b_f, fx_q_norm, fx_k_norm, w_o, ln1_w, ln1_b, router_w, router_b,
           exp_w1, exp_b1, exp_w2, exp_b2, ln2_w, ln2_b):
    B, S, D = x.shape
    T = B * S
    depth = w_in.shape[0]
    alpha = (2 * depth) ** 0.25
    rw_w = rw_w0.shape[1]
    fx_heads = fx_b_f.shape[1]
    fx_w = fx_heads * HEAD_DIM
    d_lora, a_lora, g_lora = rw_w2.shape[1], rw_a2.shape[1], rw_g2.shape[1]
    n_lora = d_lora + a_lora + g_lora
    lora_pad = -(-n_lora // 128) * 128
    rw_cols = 3 * rw_w + n_lora
    n_rw = 3 * rw_w + lora_pad
    n_experts = router_w.shape[2]
    ne_pad = -(-n_experts // 128) * 128
    row = lambda a: a.reshape(1, -1)

    seg_rw = _block_diag_ones(rw_w)
    segm_rw = _block_diag_ones(rw_w, 1.0 / HEAD_DIM)
    segm_fx = _block_diag_ones(fx_w, 1.0 / HEAD_DIM)
    tidx = jnp.arange(ROW_TILE)
    tril = (tidx[:, None] >= tidx[None, :]).astype(F32)
    cidx = jnp.arange(WKV_CHUNK)
    tri_c = (cidx[:, None] >= cidx[None, :]).astype(F32)

    assert depth == 1, "single-layer block"
    l = 0
    x2 = x.reshape(T, D)
    w_l = w_in[l]
    w_main = jnp.concatenate(
        [_pad_to(w_l[:, :rw_cols], n_rw, 1), w_l[:, rw_cols:rw_cols + 4 * fx_w]], axis=1).astype(BF16)
    wfz_t = w_l[:, rw_cols + 4 * fx_w:].T
    h0, p_rw, p_fx, fzt = _ln_proj(x2, row(ln_in_w), row(ln_in_b), w_main, wfz_t, n_rw)

    mu = _pad_to(row(rw_mu[l]), n_rw, 1)
    w2p = _pad_to(rw_w2[l], lora_pad, 0).astype(BF16)
    a2p = _pad_to(jnp.pad(rw_a2[l], ((d_lora, 0), (0, 0))), lora_pad, 0).astype(BF16)
    g2p = _pad_to(jnp.pad(rw_g2[l], ((d_lora + a_lora, 0), (0, 0))), lora_pad, 0).astype(BF16)
    r, lw, k2, v, kk, alr, g, bonus = _rw_prep(
        p_rw, B, S, rw_w, mu, row(rw_w0[l]), w2p, row(rw_a0[l]), a2p, g2p,
        row(rw_k_k[l]), row(rw_k_a[l]), row(rw_r_k[l]), seg_rw)
    y_rw = _wkv(r, lw, k2, v, kk, alr, g, bonus, row(rw_gn_w[l]), row(rw_gn_b[l]), tri_c, segm_rw, B, S)

    qw = row(jnp.tile(fx_q_norm[l], fx_heads))
    kw = row(jnp.tile(fx_k_norm[l], fx_heads))
    qa, ka, va = _fx_prep(p_fx, fzt, fx_b_f[l].reshape(-1, 1), qw, kw, segm_fx, tril, B, S, fx_w)
    y_fx = _fox_attn(qa, ka, va, p_fx, B, S, fx_w)

    wo = w_o[l].astype(BF16)
    rw_pad = _pad_to(router_w[l], ne_pad, 1)
    rb_pad = _pad_to(row(router_b[l]), ne_pad, 1)
    h1, h1p, logits = _out_ln(y_rw, y_fx, h0, wo[:rw_w], wo[rw_w:], row(ln1_w[l]), row(ln1_b[l]),
                              rw_pad, rb_pad, alpha)

    gates, pos, row_tok, block_exp, n_used = _route(logits, n_experts, MOE_TILE)
    xs = h1p[row_tok]
    b1 = exp_b1[l]
    b1g = b1[:, None, 0::2]
    b1l = b1[:, None, 1::2]
    eo_rows = _moe_ffn(block_exp, n_used, xs, exp_w1[l], b1g, b1l, exp_w2[l], exp_b2[l][:, None, :])
    eo = eo_rows[pos.T]
    h = _combine_ln(h1, eo, gates, row(ln2_w[l]), row(ln2_b[l]), alpha)
    return h.reshape(B, S, D)
```

```python
import functools

import jax
import jax.numpy as jnp
from jax import lax
from jax.experimental import pallas as pl
from jax.experimental.pallas import tpu as pltpu

F32 = jnp.float32
BF16 = jnp.bfloat16
HIGHEST = lax.Precision.HIGHEST

HEAD_DIM = 64
PAIR = 2 * HEAD_DIM
WKV_CHUNK = 64
RW_GN_EPS = 64e-5
QK_EPS = 1e-6
LN_EPS = 1e-5
TOP_K = 4
SWIGLU_ALPHA = 1.702
SWIGLU_LIMIT = 7.0
NEG_BIG = -1e30
LOG2E = 1.4426950408889634

ROW_TILE = 256
ATTN_TQ = 512
ATTN_TK = 512
ATTN_HEADS = 8
MOE_TILE = 512
WKV_BATCH = 4
VMEM_LIMIT = 48 * 1024 * 1024
FRONT_VMEM_LIMIT = 56 * 1024 * 1024


def _cparams(sem):
    return pltpu.CompilerParams(dimension_semantics=sem, vmem_limit_bytes=VMEM_LIMIT)


def _dot(a, b):
    return jnp.dot(a.astype(BF16), b.astype(BF16), preferred_element_type=F32)


def _dot_t(a, b):
    return lax.dot_general(a.astype(BF16), b.astype(BF16), (((1,), (1,)), ((), ())),
                           preferred_element_type=F32)


def _segsum(x, seg):
    hi = x.astype(BF16)
    lo = (x - hi.astype(F32)).astype(BF16)
    return (jnp.dot(hi, seg, preferred_element_type=F32) + jnp.dot(lo, seg, preferred_element_type=F32))


def _sigmoid(x):
    return 1.0 / (1.0 + jnp.exp(-x))


def _softplus(x):
    return jnp.maximum(x, 0.0) + jnp.log(1.0 + jnp.exp(-jnp.abs(x)))


def _layer_norm(x, w, b):
    mu = jnp.mean(x, axis=-1, keepdims=True)
    xc = x - mu
    var = jnp.mean(xc * xc, axis=-1, keepdims=True)
    return xc * lax.rsqrt(var + LN_EPS) * w + b


def _pack_bf16_pairs(x):
    n = x.shape[1] // 2
    bits = pltpu.bitcast(x.astype(BF16).astype(F32), jnp.uint32)
    return lax.shift_right_logical(bits[:, :n], jnp.uint32(16)) | bits[:, n:]


def _unpack_bf16_pairs(u):
    lo = pltpu.bitcast(lax.shift_left(u, jnp.uint32(16)), F32)
    hi = pltpu.bitcast(u & jnp.uint32(0xFFFF0000), F32)
    return jnp.concatenate([lo, hi], axis=1)


def _ln_proj_kernel(x_ref, lnw_ref, lnb_ref, w_ref, wfz_ref, h_ref, prw_ref, pfx_ref, fzt_ref, *, n_rw):
    h = _layer_norm(x_ref[...], lnw_ref[...], lnb_ref[...])
    h_ref[...] = h
    p = jnp.dot(h.astype(BF16), w_ref[...], preferred_element_type=F32)
    prw_ref[...] = p[:, :n_rw]
    pfx_ref[...] = p[:, n_rw:]
    fzt_ref[...] = lax.dot_general(wfz_ref[...], h, (((1,), (1,)), ((), ())),
                                   precision=HIGHEST, preferred_element_type=F32)


def _ln_proj(x2, lnw, lnb, w_main, wfz_t, n_rw):
    T, D = x2.shape
    n_all = w_main.shape[1]
    n_fx = n_all - n_rw
    nh = wfz_t.shape[0]
    tm = ROW_TILE
    return pl.pallas_call(
        functools.partial(_ln_proj_kernel, n_rw=n_rw),
        grid=(T // tm,),
        in_specs=[
            pl.BlockSpec((tm, D), lambda i: (i, 0)),
            pl.BlockSpec((1, D), lambda i: (0, 0)),
            pl.BlockSpec((1, D), lambda i: (0, 0)),
            pl.BlockSpec((D, n_all), lambda i: (0, 0)),
            pl.BlockSpec((nh, D), lambda i: (0, 0)),
        ],
        out_specs=[
            pl.BlockSpec((tm, D), lambda i: (i, 0)),
            pl.BlockSpec((tm, n_rw), lambda i: (i, 0)),
            pl.BlockSpec((tm, n_fx), lambda i: (i, 0)),
            pl.BlockSpec((nh, tm), lambda i: (0, i)),
        ],
        out_shape=[
            jax.ShapeDtypeStruct((T, D), F32),
            jax.ShapeDtypeStruct((T, n_rw), F32),
            jax.ShapeDtypeStruct((T, n_fx), F32),
            jax.ShapeDtypeStruct((nh, T), F32),
        ],
        compiler_params=_cparams(("parallel",)),
        name="ln_proj",
    )(x2, lnw, lnb, w_main, wfz_t)


def _rw_prep_kernel(p_ref, mu_ref, w0_ref, w2_ref, a0_ref, a2_ref, g2_ref, kkw_ref, ka_ref, rk_ref, seg_ref,
                    r_ref, lw_ref, k_ref, v_ref, kk_ref, alr_ref, g_ref, bonus_ref, carry_ref, *, width):
    @pl.when(pl.program_id(1) == 0)
    def _():
        carry_ref[...] = jnp.zeros_like(carry_ref)

    p = p_ref[...]
    tm = p.shape[0]
    prev = pltpu.roll(p, 1, axis=0)
    first_row = lax.broadcasted_iota(jnp.int32, p.shape, 0) == 0
    prev = jnp.where(first_row, carry_ref[...], prev)
    carry_ref[...] = p[tm - 1:tm, :]
    ps = p + mu_ref[...] * (prev - p)

    r = ps[:, 0:width]
    k = ps[:, width:2 * width]
    v = ps[:, 2 * width:3 * width]
    lora = ps[:, 3 * width:]
    seg = seg_ref[...]

    wl = w0_ref[...] + _dot(jnp.tanh(lora), w2_ref[...])
    w_raw = -_softplus(-wl) - 0.5
    lw_ref[...] = -jnp.exp(w_raw)
    alr = _sigmoid(a0_ref[...] + _dot(lora, a2_ref[...]))
    g_ref[...] = _dot(_sigmoid(lora), g2_ref[...])
    kkp = k * kkw_ref[...]
    nrm = jnp.sqrt(_segsum(kkp * kkp, seg))
    kk_ref[...] = kkp / jnp.maximum(nrm, 1e-12)
    k2 = k * (1.0 + (alr - 1.0) * ka_ref[...])
    bonus_ref[...] = _segsum(r * k2 * rk_ref[...], seg) * v
    r_ref[...] = r
    k_ref[...] = k2
    v_ref[...] = v
    alr_ref[...] = alr


def _rw_prep(p_rw, B, S, width, mu, w0, w2p, a0, a2p, g2p, kkw, ka, rk, seg):
    T, n_rw = p_rw.shape
    tm = ROW_TILE
    ns = S // tm
    n_lora = n_rw - 3 * width
    row = lambda b, s: (b * ns + s, 0)
    fixed = lambda b, s: (0, 0)
    vec = pl.BlockSpec((1, width), fixed)
    out = pl.BlockSpec((tm, width), row)
    return pl.pallas_call(
        functools.partial(_rw_prep_kernel, width=width),
        grid=(B, ns),
        in_specs=[
            pl.BlockSpec((tm, n_rw), row),
            pl.BlockSpec((1, n_rw), fixed),
            vec, pl.BlockSpec((n_lora, width), fixed),
            vec, pl.BlockSpec((n_lora, width), fixed),
            pl.BlockSpec((n_lora, width), fixed),
            vec, vec, vec,
            pl.BlockSpec((width, width), fixed),
        ],
        out_specs=[out] * 8,
        out_shape=[jax.ShapeDtypeStruct((T, width), F32)] * 8,
        scratch_shapes=[pltpu.VMEM((1, n_rw), F32)],
        compiler_params=_cparams(("parallel", "arbitrary")),
        name="rw_prep",
    )(p_rw, mu, w0, w2p, a0, a2p, g2p, kkw, ka, rk, seg)


def _stack_heads(x):
    lane = lax.broadcasted_iota(jnp.int32, x.shape, 1)
    return jnp.concatenate([jnp.where(lane < HEAD_DIM, x, 0.0), jnp.where(lane >= HEAD_DIM, x, 0.0)], axis=0)


def _wkv_kernel(r_ref, lw_ref, k_ref, v_ref, kk_ref, alr_ref, g_ref, bonus_ref, gnw_ref, gnb_ref, tri_ref,
                segm_ref, y_ref, state_ref):
    C = WKV_CHUNK
    nb, _, width = lw_ref.shape
    npair = width // PAIR

    @pl.when(pl.program_id(1) == 0)
    def _():
        state_ref[...] = jnp.zeros_like(state_ref)

    ri = lax.broadcasted_iota(jnp.int32, (2 * C, 2 * C), 0)
    ci = lax.broadcasted_iota(jnp.int32, (2 * C, 2 * C), 1)
    same = (ri // C) == (ci // C)
    strict = same & ((ci % C) < (ri % C))
    incl = same & ((ci % C) <= (ri % C))
    eye = (ri == ci).astype(F32)

    lhs, rhs, a2s, r2s, v2s, bhts, bkts, ptots = [], [], [], [], [], [], [], []
    for bi in range(nb):
        lw = lw_ref[bi]
        cum = jnp.dot(tri_ref[...], lw, precision=HIGHEST, preferred_element_type=F32)
        total = cum[C - 1:C, :]
        p_inv = jnp.exp(-cum)
        p_rem = jnp.exp(total - cum)
        p_tot = jnp.exp(total)
        kk = kk_ref[bi]
        k2 = k_ref[bi]
        b = kk * alr_ref[bi]
        a_t = -kk * jnp.exp(cum - lw)
        r_t = r_ref[bi] * jnp.exp(cum)
        b_t = b * p_inv
        k_t = k2 * p_inv
        b_h = b * p_rem
        k_h = k2 * p_rem
        v = v_ref[bi]
        for p in range(npair):
            sl = slice(p * PAIR, (p + 1) * PAIR)
            a2, r2, b2, kt2 = (_stack_heads(t[:, sl]) for t in (a_t, r_t, b_t, k_t))
            bh2, kh2, v2 = (_stack_heads(t[:, sl]) for t in (b_h, k_h, v))
            lhs.append(jnp.concatenate([a2, r2], axis=0))
            rhs.append(jnp.concatenate([b2, kt2], axis=0))
            a2s.append(a2)
            r2s.append(r2)
            v2s.append(v2)
            bhts.append(bh2.T)
            bkts.append(jnp.concatenate([bh2.T, kh2.T], axis=1))
            ptots.append(p_tot[:, sl])

    chains = range(nb * npair)
    m = [_dot_t(lhs[c], rhs[c]) for c in chains]
    n_ab = [jnp.where(strict, m[c][:2 * C, :2 * C], 0.0) for c in chains]
    m_ak = [jnp.where(strict, m[c][:2 * C, 2 * C:], 0.0) for c in chains]
    m_rb = [jnp.where(incl, m[c][2 * C:, :2 * C], 0.0) for c in chains]
    m_rk = [jnp.where(incl, m[c][2 * C:, 2 * C:], 0.0) for c in chains]
    mv = [_dot(m_ak[c], v2s[c]) for c in chains]
    mrkv = [_dot(m_rk[c], v2s[c]) for c in chains]
    inv = [eye + n_ab[c] for c in chains]
    pw = n_ab
    for _ in range(C.bit_length() - 2):
        pw = [_dot(pw[c], pw[c]) for c in chains]
        inv = [inv[c] + _dot(inv[c], pw[c]) for c in chains]
    wu = [_dot(inv[c], jnp.concatenate([a2s[c], mv[c]], axis=1)) for c in chains]
    qy = [_dot(m_rb[c], wu[c]) + jnp.concatenate([r2s[c], mrkv[c]], axis=1) for c in chains]
    g_t = [_dot(bhts[c], wu[c][:, :PAIR]) + eye * ptots[c] for c in chains]
    h_t = [_dot(bkts[c], jnp.concatenate([wu[c][:, PAIR:], v2s[c]], axis=0)) for c in chains]
    s0 = [state_ref[c] for c in chains]
    y2 = [_dot(qy[c][:, :PAIR], s0[c]) + qy[c][:, PAIR:] for c in chains]
    for c in chains:
        state_ref[c] = _dot(g_t[c], s0[c]) + h_t[c]

    segm = segm_ref[...]
    for bi in range(nb):
        y = jnp.concatenate([y2[bi * npair + p][:C] + y2[bi * npair + p][C:] for p in range(npair)], axis=1)
        mean = _segsum(y, segm)
        yc = y - mean
        var = _segsum(yc * yc, segm)
        yn = yc * lax.rsqrt(var + RW_GN_EPS) * gnw_ref[...] + gnb_ref[...]
        y_ref[bi] = ((yn + bonus_ref[bi]) * g_ref[bi]).astype(y_ref.dtype)


def _wkv(r, lw, k2, v, kk, alr, g, bonus, gnw, gnb, tri, segm, B, S):
    T, width = r.shape
    C = WKV_CHUNK
    nb = WKV_BATCH
    nc = S // C
    fixed = lambda b, c: (0, 0)
    blk = pl.BlockSpec((nb, C, width), lambda b, c: (b, c, 0))
    vec = pl.BlockSpec((1, width), fixed)
    ins = [t.reshape(B, S, width) for t in (r, lw, k2, v, kk, alr, g, bonus)]
    y = pl.pallas_call(
        _wkv_kernel,
        grid=(B // nb, nc),
        in_specs=[blk] * 8 + [vec, vec, pl.BlockSpec((C, C), fixed), pl.BlockSpec((width, width), fixed)],
        out_specs=blk,
        out_shape=jax.ShapeDtypeStruct((B, S, width), BF16),
        scratch_shapes=[pltpu.VMEM((nb * width // PAIR, PAIR, PAIR), F32)],
        compiler_params=_cparams(("parallel", "arbitrary")),
        name="wkv",
    )(*ins, gnw, gnb, tri, segm)
    return y.reshape(T, width)


def _split3(x):
    hi = x.astype(BF16).astype(F32)
    mid = (x - hi).astype(BF16).astype(F32)
    return hi, mid, x - hi - mid


def _spread_heads(x):
    lane = lax.broadcasted_iota(jnp.int32, (x.shape[0], PAIR), 1)
    groups = []
    for p in range(x.shape[1] // PAIR):
        blk = x[:, p * PAIR:(p + 1) * PAIR]
        groups.append(jnp.where(lane < HEAD_DIM, blk, 0.0))
        groups.append(jnp.where(lane < HEAD_DIM, pltpu.roll(blk, HEAD_DIM, axis=1), 0.0))
    return jnp.concatenate(groups, axis=1)


def _fx_prep_kernel(q_ref, k_ref, v_ref, fzt_ref, bf_ref, qw_ref, kw_ref, segm_ref, tril_ref,
                    place_ref, oneq_ref, onek_ref, onev_ref, qa_ref, ka_ref, va_ref, carry_ref):
    @pl.when(pl.program_id(1) == 0)
    def _():
        carry_ref[...] = jnp.zeros_like(carry_ref)

    segm = segm_ref[...]
    q = q_ref[...]
    k = k_ref[...]
    tm = q.shape[0]
    qn = q * lax.rsqrt(_segsum(q * q, segm) + QK_EPS) * (qw_ref[...] * (HEAD_DIM ** -0.5 * LOG2E))
    kn = k * lax.rsqrt(_segsum(k * k, segm) + QK_EPS) * kw_ref[...]

    lf = -_softplus(-(fzt_ref[...] + bf_ref[...]))
    c = lax.dot_general(tril_ref[...], lf, (((1,), (1,)), ((), ())), precision=HIGHEST,
                        preferred_element_type=F32) + carry_ref[...]
    carry_ref[...] = c[tm - 1:tm, :]
    placed = jnp.dot(c * LOG2E, place_ref[...], precision=HIGHEST, preferred_element_type=F32)
    hi, mid, lo = _split3(placed)
    lane = lax.broadcasted_iota(jnp.int32, placed.shape, 1) % PAIR
    c_q = jnp.where(lane == HEAD_DIM + 3, hi, jnp.where(lane == HEAD_DIM + 4, mid,
                                                        jnp.where(lane == HEAD_DIM + 5, lo, 0.0)))
    c_k = jnp.where(lane == HEAD_DIM, hi, jnp.where(lane == HEAD_DIM + 1, mid,
                                                    jnp.where(lane == HEAD_DIM + 2, lo, 0.0)))
    qa_ref[...] = (_spread_heads(qn) + c_q + oneq_ref[...]).astype(qa_ref.dtype)
    ka_ref[...] = (_spread_heads(kn) - c_k + onek_ref[...]).astype(ka_ref.dtype)
    va_ref[...] = (_spread_heads(v_ref[...]) + onev_ref[...]).astype(va_ref.dtype)


def _fx_prep(p_fx, fzt, bf, qw, kw, segm, tril, B, S, width):
    T = p_fx.shape[0]
    nh = fzt.shape[0]
    wide = nh * PAIR
    tm = ROW_TILE
    ns = S // tm
    fixed = lambda b, s: (0, 0)
    col = lambda j: pl.BlockSpec((tm, width), lambda b, s: (b * ns + s, j))
    out = pl.BlockSpec((tm, wide), lambda b, s: (b * ns + s, 0))
    vec = pl.BlockSpec((1, width), fixed)
    wvec = pl.BlockSpec((1, wide), fixed)

    lane = jnp.arange(wide) % PAIR
    head = jnp.arange(wide) // PAIR
    place = ((head[None, :] == jnp.arange(nh)[:, None]) & (lane >= HEAD_DIM) & (lane < HEAD_DIM + 6)).astype(F32)
    one_q = ((lane >= HEAD_DIM) & (lane < HEAD_DIM + 3)).astype(F32).reshape(1, wide)
    one_k = ((lane >= HEAD_DIM + 3) & (lane < HEAD_DIM + 6)).astype(F32).reshape(1, wide)
    one_v = (lane >= HEAD_DIM).astype(F32).reshape(1, wide)
    return pl.pallas_call(
        _fx_prep_kernel,
        grid=(B, ns),
        in_specs=[
            col(0), col(1), col(2),
            pl.BlockSpec((nh, tm), lambda b, s: (0, b * ns + s)),
            pl.BlockSpec((nh, 1), fixed),
            vec, vec,
            pl.BlockSpec((width, width), fixed),
            pl.BlockSpec((tm, tm), fixed),
            pl.BlockSpec((nh, wide), fixed),
            wvec, wvec, wvec,
        ],
        out_specs=[out, out, out],
        out_shape=[jax.ShapeDtypeStruct((T, wide), BF16)] * 3,
        scratch_shapes=[pltpu.VMEM((1, nh), F32)],
        compiler_params=_cparams(("parallel", "arbitrary")),
        name="fx_prep",
    )(p_fx, p_fx, p_fx, fzt, bf, qw, kw, segm, tril, place, one_q, one_k, one_v)


def _front_kernel(x_ref, lnw_ref, lnb_ref, w_ref, wfz_ref,
                  mu_ref, w0_ref, w2_ref, a0_ref, a2_ref, g2_ref, kkw_ref, kaw_ref, rk_ref, seg_ref,
                  bf_ref, qw_ref, kw_ref, segm_ref, tril_ref, place_ref, oneq_ref, onek_ref, onev_ref,
                  h_ref, r_ref, lw_ref, k_ref, v_ref, kk_ref, alr_ref, g_ref, bonus_ref,
                  qa_ref, ka_ref, va_ref, og_ref, shift_carry, c_carry, *, width):
    @pl.when(pl.program_id(1) == 0)
    def _():
        shift_carry[...] = jnp.zeros_like(shift_carry)
        c_carry[...] = jnp.zeros_like(c_carry)

    h = _layer_norm(x_ref[...], lnw_ref[...], lnb_ref[...])
    h_ref[...] = h
    proj = jnp.dot(h.astype(BF16), w_ref[...], preferred_element_type=F32)
    tm = proj.shape[0]
    n_rw = mu_ref.shape[1]

    p = proj[:, :n_rw]
    prev = pltpu.roll(p, 1, axis=0)
    first_row = lax.broadcasted_iota(jnp.int32, p.shape, 0) == 0
    prev = jnp.where(first_row, shift_carry[...], prev)
    shift_carry[...] = p[tm - 1:tm, :]
    ps = p + mu_ref[...] * (prev - p)
    r = ps[:, 0:width]
    k = ps[:, width:2 * width]
    v = ps[:, 2 * width:3 * width]
    lora = ps[:, 3 * width:]
    seg = seg_ref[...]
    wl = w0_ref[...] + _dot(jnp.tanh(lora), w2_ref[...])
    w_raw = -_softplus(-wl) - 0.5
    lw_ref[...] = -jnp.exp(w_raw)
    alr = _sigmoid(a0_ref[...] + _dot(lora, a2_ref[...]))
    g_ref[...] = _dot(_sigmoid(lora), g2_ref[...])
    kkp = k * kkw_ref[...]
    nrm = jnp.sqrt(_segsum(kkp * kkp, seg))
    kk_ref[...] = kkp / jnp.maximum(nrm, 1e-12)
    k2 = k * (1.0 + (alr - 1.0) * kaw_ref[...])
    bonus_ref[...] = _segsum(r * k2 * rk_ref[...], seg) * v
    r_ref[...] = r
    k_ref[...] = k2
    v_ref[...] = v
    alr_ref[...] = alr

    fw = qw_ref.shape[1]
    q = proj[:, n_rw:n_rw + fw]
    kx = proj[:, n_rw + fw:n_rw + 2 * fw]
    vx = proj[:, n_rw + 2 * fw:n_rw + 3 * fw]
    og_ref[...] = proj[:, n_rw + 3 * fw:]
    segm = segm_ref[...]
    qn = q * lax.rsqrt(_segsum(q * q, segm) + QK_EPS) * (qw_ref[...] * (HEAD_DIM ** -0.5 * LOG2E))
    kn = kx * lax.rsqrt(_segsum(kx * kx, segm) + QK_EPS) * kw_ref[...]
    fz = lax.dot_general(wfz_ref[...], h, (((1,), (1,)), ((), ())), precision=HIGHEST, preferred_element_type=F32)
    lf = -_softplus(-(fz + bf_ref[...]))
    c = lax.dot_general(tril_ref[...], lf, (((1,), (1,)), ((), ())), precision=HIGHEST,
                        preferred_element_type=F32) + c_carry[...]
    c_carry[...] = c[tm - 1:tm, :]
    placed = jnp.dot(c * LOG2E, place_ref[...], precision=HIGHEST, preferred_element_type=F32)
    hi, mid, lo = _split3(placed)
    lane = lax.broadcasted_iota(jnp.int32, placed.shape, 1) % PAIR
    c_q = jnp.where(lane == HEAD_DIM + 3, hi, jnp.where(lane == HEAD_DIM + 4, mid,
                                                        jnp.where(lane == HEAD_DIM + 5, lo, 0.0)))
    c_k = jnp.where(lane == HEAD_DIM, hi, jnp.where(lane == HEAD_DIM + 1, mid,
                                                    jnp.where(lane == HEAD_DIM + 2, lo, 0.0)))
    qa_ref[...] = (_spread_heads(qn) + c_q + oneq_ref[...]).astype(qa_ref.dtype)
    ka_ref[...] = (_spread_heads(kn) - c_k + onek_ref[...]).astype(ka_ref.dtype)
    va_ref[...] = (_spread_heads(vx) + onev_ref[...]).astype(va_ref.dtype)


def _front(x2, B, S, lnw, lnb, w_main, wfz_t, rw_params, fx_params, rw_w, fx_w):
    T, D = x2.shape
    n_all = w_main.shape[1]
    nh = wfz_t.shape[0]
    wide = nh * PAIR
    tm = ROW_TILE
    ns = S // tm
    row = lambda b, s: (b * ns + s, 0)
    fixed = lambda b, s: (0, 0)
    const = lambda a: pl.BlockSpec(a.shape, fixed)

    lane = jnp.arange(wide) % PAIR
    head = jnp.arange(wide) // PAIR
    place = ((head[None, :] == jnp.arange(nh)[:, None]) & (lane >= HEAD_DIM) & (lane < HEAD_DIM + 6)).astype(F32)
    one_q = ((lane >= HEAD_DIM) & (lane < HEAD_DIM + 3)).astype(F32).reshape(1, wide)
    one_k = ((lane >= HEAD_DIM + 3) & (lane < HEAD_DIM + 6)).astype(F32).reshape(1, wide)
    one_v = (lane >= HEAD_DIM).astype(F32).reshape(1, wide)
    consts = [lnw, lnb, w_main, wfz_t, *rw_params, *fx_params, place, one_q, one_k, one_v]

    f32_out = lambda n: (pl.BlockSpec((tm, n), row), jax.ShapeDtypeStruct((T, n), F32))
    bf_out = lambda n: (pl.BlockSpec((tm, n), row), jax.ShapeDtypeStruct((T, n), BF16))
    outs = [f32_out(D)] + [f32_out(rw_w)] * 8 + [bf_out(wide)] * 3 + [f32_out(fx_w)]
    return pl.pallas_call(
        functools.partial(_front_kernel, width=rw_w),
        grid=(B, ns),
        in_specs=[pl.BlockSpec((tm, D), row)] + [const(a) for a in consts],
        out_specs=[o[0] for o in outs],
        out_shape=[o[1] for o in outs],
        scratch_shapes=[pltpu.VMEM((1, rw_params[0].shape[1]), F32), pltpu.VMEM((1, nh), F32)],
        compiler_params=pltpu.CompilerParams(dimension_semantics=("parallel", "arbitrary"),
                                             vmem_limit_bytes=FRONT_VMEM_LIMIT),
        name="front",
    )(x2, *consts)


def _fox_attn_kernel(qi_ref, kj_ref, q_ref, k_ref, v_ref, og_ref, o_ref, m_ref, acc_ref):
    i = qi_ref[pl.program_id(2)]
    j = kj_ref[pl.program_id(2)]
    tq = q_ref.shape[0]
    tk = k_ref.shape[0]
    heads = range(ATTN_HEADS)
    grp = lambda ref, h: ref[:, h * PAIR:(h + 1) * PAIR]

    @pl.when(j == 0)
    def _():
        m_ref[...] = jnp.full_like(m_ref, NEG_BIG)
        acc_ref[...] = jnp.zeros_like(acc_ref)

    def step(masked):
        s = [lax.dot_general(grp(q_ref, h), grp(k_ref, h), (((1,), (1,)), ((), ())), preferred_element_type=F32)
             for h in heads]
        if masked:
            row = lax.broadcasted_iota(jnp.int32, (tq, tk), 0)
            col = lax.broadcasted_iota(jnp.int32, (tq, tk), 1)
            s = [jnp.where(col <= row, s[h], NEG_BIG) for h in heads]
        m_old = [m_ref[h] for h in heads]
        m_new = [jnp.maximum(m_old[h], jnp.max(s[h], axis=1, keepdims=True)) for h in heads]
        alpha = [jnp.exp2(m_old[h] - m_new[h]) for h in heads]
        pr = [jnp.exp2(s[h] - jnp.concatenate([m_new[h]] * (tk // PAIR), axis=1)).astype(BF16) for h in heads]
        pv = [jnp.dot(pr[h], grp(v_ref, h), preferred_element_type=F32) for h in heads]
        for h in heads:
            acc_ref[h] = alpha[h] * acc_ref[h] + pv[h]
            m_ref[h] = m_new[h]

    @pl.when(j < i)
    def _():
        step(False)

    @pl.when(j == i)
    def _():
        step(True)
        lane = lax.broadcasted_iota(jnp.int32, (tq, PAIR), 1)
        outs = []
        for p in range(ATTN_HEADS // 2):
            a0 = acc_ref[2 * p]
            a1 = acc_ref[2 * p + 1]
            o0 = a0 * pltpu.roll(1.0 / a0, HEAD_DIM, axis=1)
            o1 = pltpu.roll(a1, HEAD_DIM, axis=1) * (1.0 / a1)
            outs.append(jnp.where(lane < HEAD_DIM, o0, o1))
        o = jnp.concatenate(outs, axis=1)
        o_ref[...] = (o * _sigmoid(og_ref[...])).astype(o_ref.dtype)


def _fox_attn(qa, ka, va, og, B, S, width):
    T = qa.shape[0]
    tq, tk = ATTN_TQ, ATTN_TK
    assert tq == tk
    nq = S // tq
    nh = ATTN_HEADS
    ow = nh * HEAD_DIM
    ngroup = width // ow
    pairs = [(i, j) for i in range(nq) for j in range(i + 1)]
    qi = jnp.array([ij[0] for ij in pairs], jnp.int32)
    kj = jnp.array([ij[1] for ij in pairs], jnp.int32)
    kv = pl.BlockSpec((tk, nh * PAIR), lambda b, p, t, qi, kj: (b * nq + kj[t], p))
    grid_spec = pltpu.PrefetchScalarGridSpec(
        num_scalar_prefetch=2,
        grid=(B, ngroup, len(pairs)),
        in_specs=[
            pl.BlockSpec((tq, nh * PAIR), lambda b, p, t, qi, kj: (b * nq + qi[t], p)), kv, kv,
            pl.BlockSpec((tq, ow), lambda b, p, t, qi, kj: (b * nq + qi[t], p)),
        ],
        out_specs=pl.BlockSpec((tq, ow), lambda b, p, t, qi, kj: (b * nq + qi[t], p)),
        scratch_shapes=[pltpu.VMEM((nh, tq, PAIR), F32), pltpu.VMEM((nh, tq, PAIR), F32)],
    )
    return pl.pallas_call(
        _fox_attn_kernel,
        grid_spec=grid_spec,
        out_shape=jax.ShapeDtypeStruct((T, width), BF16),
        compiler_params=_cparams(("parallel", "parallel", "arbitrary")),
        name="fox_attn",
    )(qi, kj, qa, ka, va, og)


def _out_ln_kernel(yrw_ref, yfx_ref, h_ref, wo1_ref, wo2_ref, lnw_ref, lnb_ref, rw_ref, rb_ref,
                   h1_ref, logit_ref, *, alpha):
    mix = (jnp.dot(yrw_ref[...], wo1_ref[...], preferred_element_type=F32)
           + jnp.dot(yfx_ref[...], wo2_ref[...], preferred_element_type=F32))
    h1 = _layer_norm(alpha * h_ref[...] + mix, lnw_ref[...], lnb_ref[...])
    h1_ref[...] = h1
    ne = rb_ref.shape[1]
    h_hi = h1.astype(BF16)
    h_lo = (h1 - h_hi.astype(F32)).astype(BF16)
    rw = rw_ref[...]
    first = jnp.dot(h_hi, rw, preferred_element_type=F32)
    second = jnp.dot(h_lo, rw[:, :ne], preferred_element_type=F32)
    logit_ref[...] = first[:, :ne] + first[:, ne:] + second + rb_ref[...]


def _out_ln(y_rw, y_fx, h0, wo1, wo2, lnw, lnb, rw, rb, alpha):
    T, D = h0.shape
    width = y_rw.shape[1]
    ne = rb.shape[1]
    tm = ROW_TILE
    row = lambda i: (i, 0)
    fixed = lambda i: (0, 0)
    rw_hi = rw.astype(BF16)
    rw_lo = (rw - rw_hi.astype(F32)).astype(BF16)
    rw = jnp.concatenate([rw_hi, rw_lo], axis=1)
    return pl.pallas_call(
        functools.partial(_out_ln_kernel, alpha=alpha),
        grid=(T // tm,),
        in_specs=[
            pl.BlockSpec((tm, width), row), pl.BlockSpec((tm, width), row), pl.BlockSpec((tm, D), row),
            pl.BlockSpec((width, D), fixed), pl.BlockSpec((width, D), fixed),
            pl.BlockSpec((1, D), fixed), pl.BlockSpec((1, D), fixed),
            pl.BlockSpec((D, 2 * ne), fixed), pl.BlockSpec((1, ne), fixed),
        ],
        out_specs=[pl.BlockSpec((tm, D), row), pl.BlockSpec((tm, ne), row)],
        out_shape=[jax.ShapeDtypeStruct((T, D), F32), jax.ShapeDtypeStruct((T, ne), F32)],
        compiler_params=_cparams(("parallel",)),
        name="out_ln",
    )(y_rw, y_fx, h0, wo1, wo2, lnw, lnb, rw, rb)


DEINT_COLS = 256
MOE_VMEM_LIMIT = 56 * 1024 * 1024


def _moe_kernel(bexp_ref, nused_ref, first_ref, nexte_ref, slot_ref,
                x_ref, w1_hbm, w2_hbm, b1g_ref, b1l_ref, b2_ref, perm_ref, o_ref,
                w1_stage, w2_stage, w1g_bf, w1l_bf, w2_bf, sem):
    i = pl.program_id(0)
    live = i < nused_ref[0]

    def weight_copies(e, s):
        return (pltpu.make_async_copy(w1_hbm.at[e], w1_stage.at[s], sem.at[0, s]),
                pltpu.make_async_copy(w2_hbm.at[e], w2_stage.at[s], sem.at[1, s]))

    @pl.when(live & (i == 0))
    def _():
        for cp in weight_copies(bexp_ref[0], 0):
            cp.start()

    @pl.when(live & (first_ref[i] == 1))
    def _():
        s = slot_ref[i]
        for cp in weight_copies(bexp_ref[i], s):
            cp.wait()

        @pl.when(nexte_ref[i] >= 0)
        def _():
            for cp in weight_copies(nexte_ref[i], 1 - s):
                cp.start()

        half = DEINT_COLS // 2
        for c in range(w1_stage.shape[2] // DEINT_COLS):
            blk = w1_stage[s, :, c * DEINT_COLS:(c + 1) * DEINT_COLS].astype(BF16)
            out = jnp.dot(blk, perm_ref[...], preferred_element_type=F32).astype(BF16)
            w1g_bf[:, c * half:(c + 1) * half] = out[:, :half]
            w1l_bf[:, c * half:(c + 1) * half] = out[:, half:]
        w2_bf[...] = w2_stage[s].astype(BF16)

    @pl.when(live)
    def _():
        x = x_ref[...].astype(BF16)
        x_glu = jnp.minimum(jnp.dot(x, w1g_bf[...], preferred_element_type=F32) + b1g_ref[0], SWIGLU_LIMIT)
        x_lin = jnp.clip(jnp.dot(x, w1l_bf[...], preferred_element_type=F32) + b1l_ref[0],
                         -SWIGLU_LIMIT, SWIGLU_LIMIT)
        act = x_glu * _sigmoid(SWIGLU_ALPHA * x_glu) * (x_lin + 1.0)
        o = jnp.dot(act.astype(BF16), w2_bf[...], preferred_element_type=F32) + b2_ref[0]
        o_ref[...] = _pack_bf16_pairs(o)

    @pl.when(jnp.logical_not(live))
    def _():
        o_ref[...] = jnp.zeros_like(o_ref)


def _moe_ffn(block_exp, n_used, xs, w1, b1g, b1l, w2, b2):
    n_rows = xs.shape[0]
    E, D, F2 = w1.shape
    F = F2 // 2
    tm = MOE_TILE
    n_blocks = n_rows // tm

    idx = jnp.arange(n_blocks, dtype=jnp.int32)
    first = jnp.concatenate([jnp.ones((1,), jnp.bool_), block_exp[1:] != block_exp[:-1]])
    slot = ((jnp.cumsum(first.astype(jnp.int32)) - 1) % 2).astype(jnp.int32)
    cand = jnp.where(first & (idx < n_used[0]), idx, n_blocks)
    next_first = jnp.min(jnp.where(idx[None, :] > idx[:, None], cand[None, :], n_blocks), axis=1)
    next_e = jnp.where(next_first < n_blocks, block_exp[jnp.minimum(next_first, n_blocks - 1)], -1).astype(jnp.int32)

    half = DEINT_COLS // 2
    src = jnp.arange(DEINT_COLS)
    dst = jnp.where(src % 2 == 0, src // 2, half + src // 2)
    perm = (dst[:, None] == jnp.arange(DEINT_COLS)[None, :]).astype(BF16)

    live = lambda i, be, nu: jnp.minimum(i, nu[0] - 1)
    bspec = lambda n: pl.BlockSpec((1, 1, n), lambda i, be, nu, fi, ne, sl: (be[live(i, be, nu)], 0, 0))
    grid_spec = pltpu.PrefetchScalarGridSpec(
        num_scalar_prefetch=5,
        grid=(n_blocks,),
        in_specs=[
            pl.BlockSpec((tm, D), lambda i, be, nu, fi, ne, sl: (live(i, be, nu), 0)),
            pl.BlockSpec(memory_space=pl.ANY),
            pl.BlockSpec(memory_space=pl.ANY),
            bspec(F), bspec(F), bspec(D),
            pl.BlockSpec((DEINT_COLS, DEINT_COLS), lambda i, be, nu, fi, ne, sl: (0, 0)),
        ],
        out_specs=pl.BlockSpec((tm, D // 2), lambda i, be, nu, fi, ne, sl: (i, 0)),
        scratch_shapes=[
            pltpu.VMEM((2, D, F2), F32), pltpu.VMEM((2, F, D), F32),
            pltpu.VMEM((D, F), BF16), pltpu.VMEM((D, F), BF16), pltpu.VMEM((F, D), BF16),
            pltpu.SemaphoreType.DMA((2, 2)),
        ],
    )
    return pl.pallas_call(
        _moe_kernel,
        grid_spec=grid_spec,
        out_shape=jax.ShapeDtypeStruct((n_rows, D // 2), jnp.uint32),
        compiler_params=pltpu.CompilerParams(dimension_semantics=("arbitrary",), vmem_limit_bytes=MOE_VMEM_LIMIT),
        name="moe_ffn",
    )(block_exp, n_used, first.astype(jnp.int32), next_e, slot, xs, w1, w2, b1g, b1l, b2, perm)


def _combine_ln_kernel(h_ref, eo_ref, gate_ref, lnw_ref, lnb_ref, o_ref, *, alpha):
    gates = gate_ref[...]
    ffn = _unpack_bf16_pairs(eo_ref[0]) * gates[:, 0:1]
    for j in range(1, TOP_K):
        ffn = ffn + _unpack_bf16_pairs(eo_ref[j]) * gates[:, j:j + 1]
    o_ref[...] = _layer_norm(alpha * h_ref[...] + ffn, lnw_ref[...], lnb_ref[...])


def _combine_ln(h1, eo, gates, lnw, lnb, alpha):
    T, D = h1.shape
    tm = ROW_TILE
    return pl.pallas_call(
        functools.partial(_combine_ln_kernel, alpha=alpha),
        grid=(T // tm,),
        in_specs=[
            pl.BlockSpec((tm, D), lambda i: (i, 0)),
            pl.BlockSpec((TOP_K, tm, D // 2), lambda i: (0, i, 0)),
            pl.BlockSpec((tm, TOP_K), lambda i: (i, 0)),
            pl.BlockSpec((1, D), lambda i: (0, 0)),
            pl.BlockSpec((1, D), lambda i: (0, 0)),
        ],
        out_specs=pl.BlockSpec((tm, D), lambda i: (i, 0)),
        out_shape=jax.ShapeDtypeStruct((T, D), F32),
        compiler_params=_cparams(("parallel",)),
        name="combine_ln",
    )(h1, eo, gates, lnw, lnb)


def _pad_to(x, n, axis):
    pad = [(0, 0)] * x.ndim
    pad[axis] = (0, n - x.shape[axis])
    return jnp.pad(x, pad)


def _block_diag_ones(width, value=1.0):
    idx = jnp.arange(width) // HEAD_DIM
    return jnp.where(idx[:, None] == idx[None, :], value, 0.0).astype(BF16)


def _route(logits, n_experts, tile):
    T = logits.shape[0]
    top_val, top_idx = lax.top_k(logits[:, :n_experts], TOP_K)
    gates = jax.nn.softmax(top_val, axis=-1)
    e_flat = top_idx.reshape(-1).astype(jnp.int32)
    n_assign = T * TOP_K
    n_rows = n_assign + n_experts * tile
    n_blocks = n_rows // tile
    assert n_experts * n_assign < 2 ** 31
    eids = jnp.arange(n_experts, dtype=jnp.int32)
    aids = jnp.arange(n_assign, dtype=jnp.int32)
    skeys = lax.sort(e_flat * n_assign + aids)
    order = skeys % n_assign
    e_sorted = skeys // n_assign
    counts = jnp.sum((e_flat[:, None] == eids[None, :]).astype(jnp.int32), axis=0)
    starts = jnp.cumsum(counts) - counts
    padded = (counts + tile - 1) // tile * tile
    pends = jnp.cumsum(padded)
    pstarts = pends - padded
    shift = pstarts - starts
    dest_sorted = aids + jnp.sum(jnp.where(e_sorted[:, None] == eids[None, :], shift[None, :], 0), axis=1)
    _, dest = lax.sort((order, dest_sorted), num_keys=1)
    block_start = jnp.arange(n_blocks, dtype=jnp.int32) * tile
    block_exp = jnp.minimum(jnp.sum((pends[None, :] <= block_start[:, None]).astype(jnp.int32), axis=1),
                            n_experts - 1).astype(jnp.int32)
    n_used = (pends[-1] // tile).astype(jnp.int32).reshape(1)
    of_block = lambda per_expert: jnp.sum(
        jnp.where(block_exp[:, None] == eids[None, :], per_expert[None, :], 0), axis=1)
    in_exp = block_start - of_block(pstarts)
    n_valid = of_block(counts) - in_exp
    within = jnp.arange(tile, dtype=jnp.int32)[None, :]
    src = jnp.clip((of_block(starts) + in_exp)[:, None] + within, 0, n_assign - 1)
    row_tok = jnp.where(within < n_valid[:, None], order[src.reshape(-1)].reshape(n_blocks, tile) // TOP_K, 0)
    return gates, dest.reshape(T, TOP_K), row_tok.reshape(-1).astype(jnp.int32), block_exp, n_used


def kernel(x, ln_in_w, ln_in_b, w_in, rw_mu, rw_w0, rw_w2, rw_a0, rw_a2, rw_g2, rw_k_k, rw_k_a, rw_r_k,
           rw_gn_w, rw_gn_b, fx_b_f, fx_q_norm, fx_k_norm, w_o, ln1_w, ln1_b, router_w, router_b,
           exp_w1, exp_b1, exp_w2, exp_b2, ln2_w, ln2_b):
    B, S, D = x.shape
    T = B * S
    depth = w_in.shape[0]
    alpha = (2 * depth) ** 0.25
    rw_w = rw_w0.shape[1]
    fx_heads = fx_b_f.shape[1]
    fx_w = fx_heads * HEAD_DIM
    d_lora, a_lora, g_lora = rw_w2.shape[1], rw_a2.shape[1], rw_g2.shape[1]
    n_lora = d_lora + a_lora + g_lora
    lora_pad = -(-n_lora // 128) * 128
    rw_cols = 3 * rw_w + n_lora
    n_rw = 3 * rw_w + lora_pad
    n_experts = router_w.shape[2]
    ne_pad = -(-n_experts // 128) * 128
    row = lambda a: a.reshape(1, -1)

    seg_rw = _block_diag_ones(rw_w)
    segm_rw = _block_diag_ones(rw_w, 1.0 / HEAD_DIM)
    segm_fx = _block_diag_ones(fx_w, 1.0 / HEAD_DIM)
    tidx = jnp.arange(ROW_TILE)
    tril = (tidx[:, None] >= tidx[None, :]).astype(F32)
    cidx = jnp.arange(WKV_CHUNK)
    tri_c = (cidx[:, None] >= cidx[None, :]).astype(F32)

    assert depth == 1, "single-layer block"
    l = 0
    x2 = x.reshape(T, D)
    w_l = w_in[l]
    w_main = jnp.concatenate(
        [_pad_to(w_l[:, :rw_cols], n_rw, 1), w_l[:, rw_cols:rw_cols + 4 * fx_w]], axis=1).astype(BF16)
    wfz_t = w_l[:, rw_cols + 4 * fx_w:].T
    mu = _pad_to(row(rw_mu[l]), n_rw, 1)
    w2p = _pad_to(rw_w2[l], lora_pad, 0).astype(BF16)
    a2p = _pad_to(jnp.pad(rw_a2[l], ((d_lora, 0), (0, 0))), lora_pad, 0).astype(BF16)
    g2p = _pad_to(jnp.pad(rw_g2[l], ((d_lora + a_lora, 0), (0, 0))), lora_pad, 0).astype(BF16)
    rw_params = [mu, row(rw_w0[l]), w2p, row(rw_a0[l]), a2p, g2p,
                 row(rw_k_k[l]), row(rw_k_a[l]), row(rw_r_k[l]), seg_rw]
    qw = row(jnp.tile(fx_q_norm[l], fx_heads))
    kw = row(jnp.tile(fx_k_norm[l], fx_heads))
    fx_params = [fx_b_f[l].reshape(-1, 1), qw, kw, segm_fx, tril]
    h0, r, lw, k2, v, kk, alr, g, bonus, qa, ka, va, og = _front(
        x2, B, S, row(ln_in_w), row(ln_in_b), w_main, wfz_t, rw_params, fx_params, rw_w, fx_w)

    y_rw = _wkv(r, lw, k2, v, kk, alr, g, bonus, row(rw_gn_w[l]), row(rw_gn_b[l]), tri_c, segm_rw, B, S)
    y_fx = _fox_attn(qa, ka, va, og, B, S, fx_w)

    wo = w_o[l].astype(BF16)
    rw_pad = _pad_to(router_w[l], ne_pad, 1)
    rb_pad = _pad_to(row(router_b[l]), ne_pad, 1)
    h1, logits = _out_ln(y_rw, y_fx, h0, wo[:rw_w], wo[rw_w:], row(ln1_w[l]), row(ln1_b[l]),
                         rw_pad, rb_pad, alpha)

    gates, pos, row_tok, block_exp, n_used = _route(logits, n_experts, MOE_TILE)
    xs = h1[row_tok]
    b1 = exp_b1[l]
    b1g = b1[:, None, 0::2]
    b1l = b1[:, None, 1::2]
    eo_rows = _moe_ffn(block_exp, n_used, xs, exp_w1[l], b1g, b1l, exp_w2[l], exp_b2[l][:, None, :])
    eo = eo_rows[pos.T]
    h = _combine_ln(h1, eo, gates, row(ln2_w[l]), row(ln2_b[l]), alpha)
    return h.reshape(B, S, D)
```

```python
import functools

import jax
import jax.numpy as jnp
from jax import lax
from jax.experimental import pallas as pl
from jax.experimental.pallas import tpu as pltpu

F32 = jnp.float32
BF16 = jnp.bfloat16
HIGHEST = lax.Precision.HIGHEST

HEAD_DIM = 64
PAIR = 2 * HEAD_DIM
WKV_CHUNK = 64
RW_GN_EPS = 64e-5
QK_EPS = 1e-6
LN_EPS = 1e-5
TOP_K = 4
SWIGLU_ALPHA = 1.702
SWIGLU_LIMIT = 7.0
NEG_BIG = -1e30
LOG2E = 1.4426950408889634

ROW_TILE = 256
ATTN_TQ = 512
ATTN_TK = 512
ATTN_HEADS = 8
MOE_TILE = 512
WKV_BATCH = 4
VMEM_LIMIT = 48 * 1024 * 1024
FRONT_VMEM_LIMIT = 56 * 1024 * 1024


def _cparams(sem):
    return pltpu.CompilerParams(dimension_semantics=sem, vmem_limit_bytes=VMEM_LIMIT)


def _dot(a, b):
    return jnp.dot(a.astype(BF16), b.astype(BF16), preferred_element_type=F32)


def _dot_t(a, b):
    return lax.dot_general(a.astype(BF16), b.astype(BF16), (((1,), (1,)), ((), ())),
                           preferred_element_type=F32)


def _segsum(x, seg):
    return jnp.dot(x.astype(BF16), seg, preferred_element_type=F32)


def _sigmoid(x):
    return 1.0 / (1.0 + jnp.exp(-x))


def _softplus(x):
    return jnp.maximum(x, 0.0) + jnp.log(1.0 + jnp.exp(-jnp.abs(x)))


def _layer_norm(x, w, b):
    mu = jnp.mean(x, axis=-1, keepdims=True)
    xc = x - mu
    var = jnp.mean(xc * xc, axis=-1, keepdims=True)
    return xc * lax.rsqrt(var + LN_EPS) * w + b


def _pack_bf16_pairs(x):
    n = x.shape[1] // 2
    bits = pltpu.bitcast(x.astype(BF16).astype(F32), jnp.uint32)
    return lax.shift_right_logical(bits[:, :n], jnp.uint32(16)) | bits[:, n:]


def _unpack_bf16_pairs(u):
    lo = pltpu.bitcast(lax.shift_left(u, jnp.uint32(16)), F32)
    hi = pltpu.bitcast(u & jnp.uint32(0xFFFF0000), F32)
    return jnp.concatenate([lo, hi], axis=1)


def _stack_heads(x):
    lane = lax.broadcasted_iota(jnp.int32, x.shape, 1)
    return jnp.concatenate([jnp.where(lane < HEAD_DIM, x, 0.0), jnp.where(lane >= HEAD_DIM, x, 0.0)], axis=0)


def _wkv_kernel(r_ref, lw_ref, k_ref, v_ref, kk_ref, alr_ref, g_ref, bonus_ref, gnw_ref, gnb_ref, tri_ref,
                segm_ref, y_ref, state_ref):
    C = WKV_CHUNK
    nb, _, width = lw_ref.shape
    npair = width // PAIR

    @pl.when(pl.program_id(1) == 0)
    def _():
        state_ref[...] = jnp.zeros_like(state_ref)

    ri = lax.broadcasted_iota(jnp.int32, (2 * C, 2 * C), 0)
    ci = lax.broadcasted_iota(jnp.int32, (2 * C, 2 * C), 1)
    same = (ri // C) == (ci // C)
    strict = same & ((ci % C) < (ri % C))
    incl = same & ((ci % C) <= (ri % C))
    eye = (ri == ci).astype(F32)

    lhs, rhs, a2s, r2s, v2s, bhts, bkts, ptots = [], [], [], [], [], [], [], []
    for bi in range(nb):
        lw = lw_ref[bi]
        cum = jnp.dot(tri_ref[...], lw, precision=HIGHEST, preferred_element_type=F32)
        total = cum[C - 1:C, :]
        p_inv = jnp.exp(-cum)
        p_rem = jnp.exp(total - cum)
        p_tot = jnp.exp(total)
        kk = kk_ref[bi]
        k2 = k_ref[bi]
        b = kk * alr_ref[bi]
        a_t = -kk * jnp.exp(cum - lw)
        r_t = r_ref[bi] * jnp.exp(cum)
        b_t = b * p_inv
        k_t = k2 * p_inv
        b_h = b * p_rem
        k_h = k2 * p_rem
        v = v_ref[bi]
        for p in range(npair):
            sl = slice(p * PAIR, (p + 1) * PAIR)
            a2, r2, b2, kt2 = (_stack_heads(t[:, sl]) for t in (a_t, r_t, b_t, k_t))
            bh2, kh2, v2 = (_stack_heads(t[:, sl]) for t in (b_h, k_h, v))
            lhs.append(jnp.concatenate([a2, r2], axis=0))
            rhs.append(jnp.concatenate([b2, kt2], axis=0))
            a2s.append(a2)
            r2s.append(r2)
            v2s.append(v2)
            bhts.append(bh2.T)
            bkts.append(jnp.concatenate([bh2.T, kh2.T], axis=1))
            ptots.append(p_tot[:, sl])

    chains = range(nb * npair)
    m = [_dot_t(lhs[c], rhs[c]) for c in chains]
    n_ab = [jnp.where(strict, m[c][:2 * C, :2 * C], 0.0) for c in chains]
    m_ak = [jnp.where(strict, m[c][:2 * C, 2 * C:], 0.0) for c in chains]
    m_rb = [jnp.where(incl, m[c][2 * C:, :2 * C], 0.0) for c in chains]
    m_rk = [jnp.where(incl, m[c][2 * C:, 2 * C:], 0.0) for c in chains]
    mv = [_dot(m_ak[c], v2s[c]) for c in chains]
    mrkv = [_dot(m_rk[c], v2s[c]) for c in chains]
    inv = [eye + n_ab[c] for c in chains]
    pw = n_ab
    for _ in range(C.bit_length() - 2):
        pw = [_dot(pw[c], pw[c]) for c in chains]
        inv = [inv[c] + _dot(inv[c], pw[c]) for c in chains]
    wu = [_dot(inv[c], jnp.concatenate([a2s[c], mv[c]], axis=1)) for c in chains]
    qy = [_dot(m_rb[c], wu[c]) + jnp.concatenate([r2s[c], mrkv[c]], axis=1) for c in chains]
    g_t = [_dot(bhts[c], wu[c][:, :PAIR]) + eye * ptots[c] for c in chains]
    h_t = [_dot(bkts[c], jnp.concatenate([wu[c][:, PAIR:], v2s[c]], axis=0)) for c in chains]
    s0 = [state_ref[c] for c in chains]
    y2 = [_dot(qy[c][:, :PAIR], s0[c]) + qy[c][:, PAIR:] for c in chains]
    for c in chains:
        state_ref[c] = _dot(g_t[c], s0[c]) + h_t[c]

    segm = segm_ref[...]
    for bi in range(nb):
        y = jnp.concatenate([y2[bi * npair + p][:C] + y2[bi * npair + p][C:] for p in range(npair)], axis=1)
        mean = _segsum(y, segm)
        yc = y - mean
        var = _segsum(yc * yc, segm)
        yn = yc * lax.rsqrt(var + RW_GN_EPS) * gnw_ref[...] + gnb_ref[...]
        y_ref[bi] = ((yn + bonus_ref[bi]) * g_ref[bi]).astype(y_ref.dtype)


def _wkv(r, lw, k2, v, kk, alr, g, bonus, gnw, gnb, tri, segm, B, S):
    T, width = r.shape
    C = WKV_CHUNK
    nb = WKV_BATCH
    nc = S // C
    fixed = lambda b, c: (0, 0)
    blk = pl.BlockSpec((nb, C, width), lambda b, c: (b, c, 0))
    vec = pl.BlockSpec((1, width), fixed)
    ins = [t.reshape(B, S, width) for t in (r, lw, k2, v, kk, alr, g, bonus)]
    y = pl.pallas_call(
        _wkv_kernel,
        grid=(B // nb, nc),
        in_specs=[blk] * 8 + [vec, vec, pl.BlockSpec((C, C), fixed), pl.BlockSpec((width, width), fixed)],
        out_specs=blk,
        out_shape=jax.ShapeDtypeStruct((B, S, width), BF16),
        scratch_shapes=[pltpu.VMEM((nb * width // PAIR, PAIR, PAIR), F32)],
        compiler_params=_cparams(("parallel", "arbitrary")),
        name="wkv",
    )(*ins, gnw, gnb, tri, segm)
    return y.reshape(T, width)


def _split3(x):
    hi = x.astype(BF16).astype(F32)
    mid = (x - hi).astype(BF16).astype(F32)
    return hi, mid, x - hi - mid


def _pieces(x, n):
    hi, mid, lo = _split3(x)
    return hi + pltpu.roll(mid, n, axis=1) + pltpu.roll(lo, 2 * n, axis=1)


def _spread_heads(x):
    lane = lax.broadcasted_iota(jnp.int32, (x.shape[0], PAIR), 1)
    groups = []
    for p in range(x.shape[1] // PAIR):
        blk = x[:, p * PAIR:(p + 1) * PAIR]
        groups.append(jnp.where(lane < HEAD_DIM, blk, 0.0))
        groups.append(jnp.where(lane < HEAD_DIM, pltpu.roll(blk, HEAD_DIM, axis=1), 0.0))
    return jnp.concatenate(groups, axis=1)


def _front_kernel(x_ref, lnw_ref, lnb_ref, w_ref, wfz_ref,
                  mu_ref, w0_ref, w2_ref, a0_ref, a2_ref, g2_ref, kkw_ref, kaw_ref, rk_ref, seg_ref,
                  bf_ref, qw_ref, kw_ref, segm_ref, tril_ref, place_ref, maskq_ref, maskk_ref,
                  oneq_ref, onek_ref, onev_ref,
                  h_ref, r_ref, lw_ref, k_ref, v_ref, kk_ref, alr_ref, g_ref, bonus_ref,
                  qa_ref, ka_ref, va_ref, og_ref, shift_carry, c_carry, *, width, nh):
    @pl.when(pl.program_id(1) == 0)
    def _():
        shift_carry[...] = jnp.zeros_like(shift_carry)
        c_carry[...] = jnp.zeros_like(c_carry)

    h = _layer_norm(x_ref[...], lnw_ref[...], lnb_ref[...])
    h_ref[...] = h
    proj = jnp.dot(h.astype(BF16), w_ref[...], preferred_element_type=F32)
    tm = proj.shape[0]
    n_rw = mu_ref.shape[1]

    p = proj[:, :n_rw]
    prev = pltpu.roll(p, 1, axis=0)
    first_row = lax.broadcasted_iota(jnp.int32, p.shape, 0) == 0
    prev = jnp.where(first_row, shift_carry[...], prev)
    shift_carry[...] = p[tm - 1:tm, :]
    ps = p + mu_ref[...] * (prev - p)
    r = ps[:, 0:width]
    k = ps[:, width:2 * width]
    v = ps[:, 2 * width:3 * width]
    lora = ps[:, 3 * width:]
    seg = seg_ref[...]
    wl = w0_ref[...] + _dot(jnp.tanh(lora), w2_ref[...])
    w_raw = -_softplus(-wl) - 0.5
    lw_ref[...] = -jnp.exp(w_raw)
    alr = _sigmoid(a0_ref[...] + _dot(lora, a2_ref[...]))
    g_ref[...] = _dot(_sigmoid(lora), g2_ref[...])
    kkp = k * kkw_ref[...]
    nrm = jnp.sqrt(_segsum(kkp * kkp, seg))
    kk_ref[...] = kkp / jnp.maximum(nrm, 1e-12)
    k2 = k * (1.0 + (alr - 1.0) * kaw_ref[...])
    bonus_ref[...] = _segsum(r * k2 * rk_ref[...], seg) * v
    r_ref[...] = r
    k_ref[...] = k2
    v_ref[...] = v
    alr_ref[...] = alr

    fw = qw_ref.shape[1]
    q = proj[:, n_rw:n_rw + fw]
    kx = proj[:, n_rw + fw:n_rw + 2 * fw]
    vx = proj[:, n_rw + 2 * fw:n_rw + 3 * fw]
    og_ref[...] = proj[:, n_rw + 3 * fw:n_rw + 4 * fw]
    segm = segm_ref[...]
    qn = q * lax.rsqrt(_segsum(q * q, segm) + QK_EPS) * (qw_ref[...] * (HEAD_DIM ** -0.5 * LOG2E))
    kn = kx * lax.rsqrt(_segsum(kx * kx, segm) + QK_EPS) * kw_ref[...]
    fblk = proj[:, n_rw + 4 * fw:]
    h_lo = (h - h.astype(BF16).astype(F32)).astype(BF16)
    fz = fblk + pltpu.roll(fblk, PAIR - nh, axis=1) + jnp.dot(h_lo, wfz_ref[...], preferred_element_type=F32)
    in_heads = lax.broadcasted_iota(jnp.int32, fz.shape, 1) < nh
    lf = jnp.where(in_heads, -_softplus(-(fz + bf_ref[...])), 0.0)
    c3 = jnp.dot(tril_ref[...], _pieces(lf, nh).astype(BF16), preferred_element_type=F32)
    c = c3 + pltpu.roll(c3, PAIR - nh, axis=1) + pltpu.roll(c3, PAIR - 2 * nh, axis=1)
    c = jnp.where(in_heads, c, 0.0) + c_carry[...]
    c_carry[...] = c[tm - 1:tm, :]
    placed = jnp.dot(_pieces(c * LOG2E, nh).astype(BF16), place_ref[...], preferred_element_type=F32)
    qa_ref[...] = (_spread_heads(qn) + placed * maskq_ref[...] + oneq_ref[...]).astype(qa_ref.dtype)
    ka_ref[...] = (_spread_heads(kn) - placed * maskk_ref[...] + onek_ref[...]).astype(ka_ref.dtype)
    va_ref[...] = (_spread_heads(vx) + onev_ref[...]).astype(va_ref.dtype)


def _front(x2, B, S, lnw, lnb, w_main, wfz_hi, rw_params, fx_params, rw_w, fx_w, nh):
    T, D = x2.shape
    wide = nh * PAIR
    tm = ROW_TILE
    ns = S // tm
    row = lambda b, s: (b * ns + s, 0)
    fixed = lambda b, s: (0, 0)
    const = lambda a: pl.BlockSpec(a.shape, fixed)

    lane = jnp.arange(wide) % PAIR
    head = jnp.arange(wide) // PAIR
    src = jnp.arange(PAIR)
    piece = (lane - HEAD_DIM) % 3
    place = ((src[:, None] == (piece * nh + head)[None, :]) & (src[:, None] < 3 * nh)
             & (lane >= HEAD_DIM)[None, :] & (lane < HEAD_DIM + 6)[None, :]).astype(BF16)
    mask_k = ((lane >= HEAD_DIM) & (lane < HEAD_DIM + 3)).astype(F32).reshape(1, wide)
    mask_q = ((lane >= HEAD_DIM + 3) & (lane < HEAD_DIM + 6)).astype(F32).reshape(1, wide)
    one_v = (lane >= HEAD_DIM).astype(F32).reshape(1, wide)
    consts = [lnw, lnb, w_main, wfz_hi, *rw_params, *fx_params, place, mask_q, mask_k, mask_k, mask_q, one_v]

    f32_out = lambda n: (pl.BlockSpec((tm, n), row), jax.ShapeDtypeStruct((T, n), F32))
    bf_out = lambda n: (pl.BlockSpec((tm, n), row), jax.ShapeDtypeStruct((T, n), BF16))
    outs = [f32_out(D)] + [f32_out(rw_w)] * 8 + [bf_out(wide)] * 3 + [f32_out(fx_w)]
    return pl.pallas_call(
        functools.partial(_front_kernel, width=rw_w, nh=nh),
        grid=(B, ns),
        in_specs=[pl.BlockSpec((tm, D), row)] + [const(a) for a in consts],
        out_specs=[o[0] for o in outs],
        out_shape=[o[1] for o in outs],
        scratch_shapes=[pltpu.VMEM((1, rw_params[0].shape[1]), F32), pltpu.VMEM((1, PAIR), F32)],
        compiler_params=pltpu.CompilerParams(dimension_semantics=("parallel", "arbitrary"),
                                             vmem_limit_bytes=FRONT_VMEM_LIMIT),
        name="front",
    )(x2, *consts)


def _fox_attn_kernel(qi_ref, kj_ref, q_ref, k_ref, v_ref, og_ref, o_ref, m_ref, acc_ref):
    i = qi_ref[pl.program_id(2)]
    j = kj_ref[pl.program_id(2)]
    tq = q_ref.shape[0]
    tk = k_ref.shape[0]
    heads = range(ATTN_HEADS)
    grp = lambda ref, h: ref[:, h * PAIR:(h + 1) * PAIR]

    @pl.when(j == 0)
    def _():
        m_ref[...] = jnp.full_like(m_ref, NEG_BIG)
        acc_ref[...] = jnp.zeros_like(acc_ref)

    def step(masked):
        s = [lax.dot_general(grp(q_ref, h), grp(k_ref, h), (((1,), (1,)), ((), ())), preferred_element_type=F32)
             for h in heads]
        if masked:
            row = lax.broadcasted_iota(jnp.int32, (tq, tk), 0)
            col = lax.broadcasted_iota(jnp.int32, (tq, tk), 1)
            s = [jnp.where(col <= row, s[h], NEG_BIG) for h in heads]
        m_old = [m_ref[h] for h in heads]
        m_new = [jnp.maximum(m_old[h], jnp.max(s[h], axis=1, keepdims=True)) for h in heads]
        alpha = [jnp.exp2(m_old[h] - m_new[h]) for h in heads]
        pr = [jnp.exp2(s[h] - jnp.concatenate([m_new[h]] * (tk // PAIR), axis=1)).astype(BF16) for h in heads]
        pv = [jnp.dot(pr[h], grp(v_ref, h), preferred_element_type=F32) for h in heads]
        for h in heads:
            acc_ref[h] = alpha[h] * acc_ref[h] + pv[h]
            m_ref[h] = m_new[h]

    @pl.when(j < i)
    def _():
        step(False)

    @pl.when(j == i)
    def _():
        step(True)
        lane = lax.broadcasted_iota(jnp.int32, (tq, PAIR), 1)
        outs = []
        for p in range(ATTN_HEADS // 2):
            a0 = acc_ref[2 * p]
            a1 = acc_ref[2 * p + 1]
            o0 = a0 * pltpu.roll(1.0 / a0, HEAD_DIM, axis=1)
            o1 = pltpu.roll(a1, HEAD_DIM, axis=1) * (1.0 / a1)
            outs.append(jnp.where(lane < HEAD_DIM, o0, o1))
        o = jnp.concatenate(outs, axis=1)
        o_ref[...] = (o * _sigmoid(og_ref[...])).astype(o_ref.dtype)


def _fox_attn(qa, ka, va, og, B, S, width):
    T = qa.shape[0]
    tq, tk = ATTN_TQ, ATTN_TK
    assert tq == tk
    nq = S // tq
    nh = ATTN_HEADS
    ow = nh * HEAD_DIM
    ngroup = width // ow
    pairs = [(i, j) for i in range(nq) for j in range(i + 1)]
    qi = jnp.array([ij[0] for ij in pairs], jnp.int32)
    kj = jnp.array([ij[1] for ij in pairs], jnp.int32)
    kv = pl.BlockSpec((tk, nh * PAIR), lambda b, p, t, qi, kj: (b * nq + kj[t], p))
    grid_spec = pltpu.PrefetchScalarGridSpec(
        num_scalar_prefetch=2,
        grid=(B, ngroup, len(pairs)),
        in_specs=[
            pl.BlockSpec((tq, nh * PAIR), lambda b, p, t, qi, kj: (b * nq + qi[t], p)), kv, kv,
            pl.BlockSpec((tq, ow), lambda b, p, t, qi, kj: (b * nq + qi[t], p)),
        ],
        out_specs=pl.BlockSpec((tq, ow), lambda b, p, t, qi, kj: (b * nq + qi[t], p)),
        scratch_shapes=[pltpu.VMEM((nh, tq, PAIR), F32), pltpu.VMEM((nh, tq, PAIR), F32)],
    )
    return pl.pallas_call(
        _fox_attn_kernel,
        grid_spec=grid_spec,
        out_shape=jax.ShapeDtypeStruct((T, width), BF16),
        compiler_params=_cparams(("parallel", "parallel", "arbitrary")),
        name="fox_attn",
    )(qi, kj, qa, ka, va, og)


def _out_ln_kernel(yrw_ref, yfx_ref, h_ref, wo1_ref, wo2_ref, lnw_ref, lnb_ref, rw_ref, rb_ref,
                   h1_ref, logit_ref, *, alpha):
    mix = (jnp.dot(yrw_ref[...], wo1_ref[...], preferred_element_type=F32)
           + jnp.dot(yfx_ref[...], wo2_ref[...], preferred_element_type=F32))
    h1 = _layer_norm(alpha * h_ref[...] + mix, lnw_ref[...], lnb_ref[...])
    h1_ref[...] = h1
    ne = rb_ref.shape[1]
    h_hi = h1.astype(BF16)
    h_lo = (h1 - h_hi.astype(F32)).astype(BF16)
    rw = rw_ref[...]
    first = jnp.dot(h_hi, rw, preferred_element_type=F32)
    second = jnp.dot(h_lo, rw[:, :ne], preferred_element_type=F32)
    logit_ref[...] = first[:, :ne] + first[:, ne:] + second + rb_ref[...]


def _out_ln(y_rw, y_fx, h0, wo1, wo2, lnw, lnb, rw, rb, alpha):
    T, D = h0.shape
    width = y_rw.shape[1]
    ne = rb.shape[1]
    tm = ROW_TILE
    row = lambda i: (i, 0)
    fixed = lambda i: (0, 0)
    rw_hi = rw.astype(BF16)
    rw_lo = (rw - rw_hi.astype(F32)).astype(BF16)
    rw = jnp.concatenate([rw_hi, rw_lo], axis=1)
    return pl.pallas_call(
        functools.partial(_out_ln_kernel, alpha=alpha),
        grid=(T // tm,),
        in_specs=[
            pl.BlockSpec((tm, width), row), pl.BlockSpec((tm, width), row), pl.BlockSpec((tm, D), row),
            pl.BlockSpec((width, D), fixed), pl.BlockSpec((width, D), fixed),
            pl.BlockSpec((1, D), fixed), pl.BlockSpec((1, D), fixed),
            pl.BlockSpec((D, 2 * ne), fixed), pl.BlockSpec((1, ne), fixed),
        ],
        out_specs=[pl.BlockSpec((tm, D), row), pl.BlockSpec((tm, ne), row)],
        out_shape=[jax.ShapeDtypeStruct((T, D), F32), jax.ShapeDtypeStruct((T, ne), F32)],
        compiler_params=_cparams(("parallel",)),
        name="out_ln",
    )(y_rw, y_fx, h0, wo1, wo2, lnw, lnb, rw, rb)


DEINT_COLS = 256
MOE_VMEM_LIMIT = 56 * 1024 * 1024


def _moe_kernel(bexp_ref, nused_ref, first_ref, nexte_ref, slot_ref,
                x_ref, w1_hbm, w2_hbm, b1g_ref, b1l_ref, b2_ref, perm_ref, o_ref,
                w1_stage, w2_stage, w1g_bf, w1l_bf, w2_bf, sem):
    i = pl.program_id(0)
    live = i < nused_ref[0]

    def weight_copies(e, s):
        return (pltpu.make_async_copy(w1_hbm.at[e], w1_stage.at[s], sem.at[0, s]),
                pltpu.make_async_copy(w2_hbm.at[e], w2_stage.at[s], sem.at[1, s]))

    @pl.when(live & (i == 0))
    def _():
        for cp in weight_copies(bexp_ref[0], 0):
            cp.start()

    @pl.when(live & (first_ref[i] == 1))
    def _():
        s = slot_ref[i]
        for cp in weight_copies(bexp_ref[i], s):
            cp.wait()

        @pl.when(nexte_ref[i] >= 0)
        def _():
            for cp in weight_copies(nexte_ref[i], 1 - s):
                cp.start()

        half = DEINT_COLS // 2
        for c in range(w1_stage.shape[2] // DEINT_COLS):
            blk = w1_stage[s, :, c * DEINT_COLS:(c + 1) * DEINT_COLS].astype(BF16)
            out = jnp.dot(blk, perm_ref[...], preferred_element_type=F32).astype(BF16)
            w1g_bf[:, c * half:(c + 1) * half] = out[:, :half]
            w1l_bf[:, c * half:(c + 1) * half] = out[:, half:]
        w2_bf[...] = w2_stage[s].astype(BF16)

    @pl.when(live)
    def _():
        x = x_ref[...].astype(BF16)
        x_glu = jnp.minimum(jnp.dot(x, w1g_bf[...], preferred_element_type=F32) + b1g_ref[0], SWIGLU_LIMIT)
        x_lin = jnp.clip(jnp.dot(x, w1l_bf[...], preferred_element_type=F32) + b1l_ref[0],
                         -SWIGLU_LIMIT, SWIGLU_LIMIT)
        act = x_glu * _sigmoid(SWIGLU_ALPHA * x_glu) * (x_lin + 1.0)
        o = jnp.dot(act.astype(BF16), w2_bf[...], preferred_element_type=F32) + b2_ref[0]
        o_ref[...] = _pack_bf16_pairs(o)

    @pl.when(jnp.logical_not(live))
    def _():
        o_ref[...] = jnp.zeros_like(o_ref)


def _moe_ffn(block_exp, n_used, xs, w1, b1g, b1l, w2, b2):
    n_rows = xs.shape[0]
    E, D, F2 = w1.shape
    F = F2 // 2
    tm = MOE_TILE
    n_blocks = n_rows // tm

    idx = jnp.arange(n_blocks, dtype=jnp.int32)
    first = jnp.concatenate([jnp.ones((1,), jnp.bool_), block_exp[1:] != block_exp[:-1]])
    slot = ((jnp.cumsum(first.astype(jnp.int32)) - 1) % 2).astype(jnp.int32)
    cand = jnp.where(first & (idx < n_used[0]), idx, n_blocks)
    next_first = jnp.min(jnp.where(idx[None, :] > idx[:, None], cand[None, :], n_blocks), axis=1)
    next_e = jnp.where(next_first < n_blocks, block_exp[jnp.minimum(next_first, n_blocks - 1)], -1).astype(jnp.int32)

    half = DEINT_COLS // 2
    src = jnp.arange(DEINT_COLS)
    dst = jnp.where(src % 2 == 0, src // 2, half + src // 2)
    perm = (dst[:, None] == jnp.arange(DEINT_COLS)[None, :]).astype(BF16)

    live = lambda i, be, nu: jnp.minimum(i, nu[0] - 1)
    bspec = lambda n: pl.BlockSpec((1, 1, n), lambda i, be, nu, fi, ne, sl: (be[live(i, be, nu)], 0, 0))
    grid_spec = pltpu.PrefetchScalarGridSpec(
        num_scalar_prefetch=5,
        grid=(n_blocks,),
        in_specs=[
            pl.BlockSpec((tm, D), lambda i, be, nu, fi, ne, sl: (live(i, be, nu), 0)),
            pl.BlockSpec(memory_space=pl.ANY),
            pl.BlockSpec(memory_space=pl.ANY),
            bspec(F), bspec(F), bspec(D),
            pl.BlockSpec((DEINT_COLS, DEINT_COLS), lambda i, be, nu, fi, ne, sl: (0, 0)),
        ],
        out_specs=pl.BlockSpec((tm, D // 2), lambda i, be, nu, fi, ne, sl: (i, 0)),
        scratch_shapes=[
            pltpu.VMEM((2, D, F2), F32), pltpu.VMEM((2, F, D), F32),
            pltpu.VMEM((D, F), BF16), pltpu.VMEM((D, F), BF16), pltpu.VMEM((F, D), BF16),
            pltpu.SemaphoreType.DMA((2, 2)),
        ],
    )
    return pl.pallas_call(
        _moe_kernel,
        grid_spec=grid_spec,
        out_shape=jax.ShapeDtypeStruct((n_rows, D // 2), jnp.uint32),
        compiler_params=pltpu.CompilerParams(dimension_semantics=("arbitrary",), vmem_limit_bytes=MOE_VMEM_LIMIT),
        name="moe_ffn",
    )(block_exp, n_used, first.astype(jnp.int32), next_e, slot, xs, w1, w2, b1g, b1l, b2, perm)


def _combine_ln_kernel(h_ref, eo_ref, gate_ref, lnw_ref, lnb_ref, o_ref, *, alpha):
    gates = gate_ref[...]
    ffn = _unpack_bf16_pairs(eo_ref[0]) * gates[:, 0:1]
    for j in range(1, TOP_K):
        ffn = ffn + _unpack_bf16_pairs(eo_ref[j]) * gates[:, j:j + 1]
    o_ref[...] = _layer_norm(alpha * h_ref[...] + ffn, lnw_ref[...], lnb_ref[...])


def _combine_ln(h1, eo, gates, lnw, lnb, alpha):
    T, D = h1.shape
    tm = ROW_TILE
    return pl.pallas_call(
        functools.partial(_combine_ln_kernel, alpha=alpha),
        grid=(T // tm,),
        in_specs=[
            pl.BlockSpec((tm, D), lambda i: (i, 0)),
            pl.BlockSpec((TOP_K, tm, D // 2), lambda i: (0, i, 0)),
            pl.BlockSpec((tm, TOP_K), lambda i: (i, 0)),
            pl.BlockSpec((1, D), lambda i: (0, 0)),
            pl.BlockSpec((1, D), lambda i: (0, 0)),
        ],
        out_specs=pl.BlockSpec((tm, D), lambda i: (i, 0)),
        out_shape=jax.ShapeDtypeStruct((T, D), F32),
        compiler_params=_cparams(("parallel",)),
        name="combine_ln",
    )(h1, eo, gates, lnw, lnb)


def _pad_to(x, n, axis):
    pad = [(0, 0)] * x.ndim
    pad[axis] = (0, n - x.shape[axis])
    return jnp.pad(x, pad)


def _block_diag_ones(width, value=1.0):
    idx = jnp.arange(width) // HEAD_DIM
    return jnp.where(idx[:, None] == idx[None, :], value, 0.0).astype(BF16)


def _route(logits, n_experts, tile):
    T = logits.shape[0]
    top_val, top_idx = lax.top_k(logits[:, :n_experts], TOP_K)
    gates = jax.nn.softmax(top_val, axis=-1)
    e_flat = top_idx.reshape(-1).astype(jnp.int32)
    n_assign = T * TOP_K
    n_rows = n_assign + n_experts * tile
    n_blocks = n_rows // tile
    assert n_experts * n_assign < 2 ** 31
    eids = jnp.arange(n_experts, dtype=jnp.int32)
    aids = jnp.arange(n_assign, dtype=jnp.int32)
    skeys = lax.sort(e_flat * n_assign + aids)
    order = skeys % n_assign
    e_sorted = skeys // n_assign
    counts = jnp.sum((e_flat[:, None] == eids[None, :]).astype(jnp.int32), axis=0)
    starts = jnp.cumsum(counts) - counts
    padded = (counts + tile - 1) // tile * tile
    pends = jnp.cumsum(padded)
    pstarts = pends - padded
    shift = pstarts - starts
    dest_sorted = aids + jnp.sum(jnp.where(e_sorted[:, None] == eids[None, :], shift[None, :], 0), axis=1)
    _, dest = lax.sort((order, dest_sorted), num_keys=1)
    block_start = jnp.arange(n_blocks, dtype=jnp.int32) * tile
    block_exp = jnp.minimum(jnp.sum((pends[None, :] <= block_start[:, None]).astype(jnp.int32), axis=1),
                            n_experts - 1).astype(jnp.int32)
    n_used = (pends[-1] // tile).astype(jnp.int32).reshape(1)
    of_block = lambda per_expert: jnp.sum(
        jnp.where(block_exp[:, None] == eids[None, :], per_expert[None, :], 0), axis=1)
    in_exp = block_start - of_block(pstarts)
    n_valid = of_block(counts) - in_exp
    within = jnp.arange(tile, dtype=jnp.int32)[None, :]
    src = jnp.clip((of_block(starts) + in_exp)[:, None] + within, 0, n_assign - 1)
    row_tok = jnp.where(within < n_valid[:, None], order[src.reshape(-1)].reshape(n_blocks, tile) // TOP_K, 0)
    return gates, dest.reshape(T, TOP_K), row_tok.reshape(-1).astype(jnp.int32), block_exp, n_used


def kernel(x, ln_in_w, ln_in_b, w_in, rw_mu, rw_w0, rw_w2, rw_a0, rw_a2, rw_g2, rw_k_k, rw_k_a, rw_r_k,
           rw_gn_w, rw_gn_b, fx_b_f, fx_q_norm, fx_k_norm, w_o, ln1_w, ln1_b, router_w, router_b,
           exp_w1, exp_b1, exp_w2, exp_b2, ln2_w, ln2_b):
    B, S, D = x.shape
    T = B * S
    depth = w_in.shape[0]
    alpha = (2 * depth) ** 0.25
    rw_w = rw_w0.shape[1]
    fx_heads = fx_b_f.shape[1]
    fx_w = fx_heads * HEAD_DIM
    d_lora, a_lora, g_lora = rw_w2.shape[1], rw_a2.shape[1], rw_g2.shape[1]
    n_lora = d_lora + a_lora + g_lora
    lora_pad = -(-n_lora // 128) * 128
    rw_cols = 3 * rw_w + n_lora
    n_rw = 3 * rw_w + lora_pad
    n_experts = router_w.shape[2]
    ne_pad = -(-n_experts // 128) * 128
    row = lambda a: a.reshape(1, -1)

    seg_rw = _block_diag_ones(rw_w)
    segm_rw = _block_diag_ones(rw_w, 1.0 / HEAD_DIM)
    segm_fx = _block_diag_ones(fx_w, 1.0 / HEAD_DIM)
    tidx = jnp.arange(ROW_TILE)
    tril = (tidx[:, None] >= tidx[None, :]).astype(BF16)
    cidx = jnp.arange(WKV_CHUNK)
    tri_c = (cidx[:, None] >= cidx[None, :]).astype(F32)

    assert depth == 1, "single-layer block"
    l = 0
    x2 = x.reshape(T, D)
    w_l = w_in[l]
    wfz = w_l[:, rw_cols + 4 * fx_w:]
    wfz_hi = wfz.astype(BF16)
    wfz_lo = (wfz - wfz_hi.astype(F32)).astype(BF16)
    w_main = jnp.concatenate(
        [_pad_to(w_l[:, :rw_cols], n_rw, 1).astype(BF16), w_l[:, rw_cols:rw_cols + 4 * fx_w].astype(BF16),
         _pad_to(jnp.concatenate([wfz_hi, wfz_lo], axis=1), PAIR, 1)], axis=1)
    mu = _pad_to(row(rw_mu[l]), n_rw, 1)
    w2p = _pad_to(rw_w2[l], lora_pad, 0).astype(BF16)
    a2p = _pad_to(jnp.pad(rw_a2[l], ((d_lora, 0), (0, 0))), lora_pad, 0).astype(BF16)
    g2p = _pad_to(jnp.pad(rw_g2[l], ((d_lora + a_lora, 0), (0, 0))), lora_pad, 0).astype(BF16)
    rw_params = [mu, row(rw_w0[l]), w2p, row(rw_a0[l]), a2p, g2p,
                 row(rw_k_k[l]), row(rw_k_a[l]), row(rw_r_k[l]), seg_rw]
    qw = row(jnp.tile(fx_q_norm[l], fx_heads))
    kw = row(jnp.tile(fx_k_norm[l], fx_heads))
    fx_params = [_pad_to(row(fx_b_f[l]), PAIR, 1), qw, kw, segm_fx, tril]
    h0, r, lw, k2, v, kk, alr, g, bonus, qa, ka, va, og = _front(
        x2, B, S, row(ln_in_w), row(ln_in_b), w_main, _pad_to(wfz_hi, PAIR, 1), rw_params, fx_params,
        rw_w, fx_w, fx_heads)

    y_rw = _wkv(r, lw, k2, v, kk, alr, g, bonus, row(rw_gn_w[l]), row(rw_gn_b[l]), tri_c, segm_rw, B, S)
    y_fx = _fox_attn(qa, ka, va, og, B, S, fx_w)

    wo = w_o[l].astype(BF16)
    rw_pad = _pad_to(router_w[l], ne_pad, 1)
    rb_pad = _pad_to(row(router_b[l]), ne_pad, 1)
    h1, logits = _out_ln(y_rw, y_fx, h0, wo[:rw_w], wo[rw_w:], row(ln1_w[l]), row(ln1_b[l]),
                         rw_pad, rb_pad, alpha)

    gates, pos, row_tok, block_exp, n_used = _route(logits, n_experts, MOE_TILE)
    xs = h1[row_tok]
    b1 = exp_b1[l]
    b1g = b1[:, None, 0::2]
    b1l = b1[:, None, 1::2]
    eo_rows = _moe_ffn(block_exp, n_used, xs, exp_w1[l], b1g, b1l, exp_w2[l], exp_b2[l][:, None, :])
    eo = eo_rows[pos.T]
    h = _combine_ln(h1, eo, gates, row(ln2_w[l]), row(ln2_b[l]), alpha)
    return h.reshape(B, S, D)
```

```python
import functools

import jax
import jax.numpy as jnp
from jax import lax
from jax.experimental import pallas as pl
from jax.experimental.pallas import tpu as pltpu

F32 = jnp.float32
BF16 = jnp.bfloat16
HIGHEST = lax.Precision.HIGHEST

HEAD_DIM = 64
PAIR = 2 * HEAD_DIM
WKV_CHUNK = 64
RW_GN_EPS = 64e-5
QK_EPS = 1e-6
LN_EPS = 1e-5
TOP_K = 4
SWIGLU_ALPHA = 1.702
SWIGLU_LIMIT = 7.0
NEG_BIG = -1e30
LOG2E = 1.4426950408889634

ROW_TILE = 256
ATTN_TQ = 512
ATTN_TK = 512
ATTN_HEADS = 8
MOE_TILE = 512
ROW_ALIGN = 8
WKV_BATCH = 4
VMEM_LIMIT = 48 * 1024 * 1024
FRONT_VMEM_LIMIT = 56 * 1024 * 1024


def _cparams(sem):
    return pltpu.CompilerParams(dimension_semantics=sem, vmem_limit_bytes=VMEM_LIMIT)


def _dot(a, b):
    return jnp.dot(a.astype(BF16), b.astype(BF16), preferred_element_type=F32)


def _dot_t(a, b):
    return lax.dot_general(a.astype(BF16), b.astype(BF16), (((1,), (1,)), ((), ())),
                           preferred_element_type=F32)


def _segsum(x, seg):
    return jnp.dot(x.astype(BF16), seg, preferred_element_type=F32)


def _sigmoid(x):
    return 1.0 / (1.0 + jnp.exp(-x))


def _softplus(x):
    return jnp.maximum(x, 0.0) + jnp.log(1.0 + jnp.exp(-jnp.abs(x)))


def _layer_norm(x, w, b):
    mu = jnp.mean(x, axis=-1, keepdims=True)
    xc = x - mu
    var = jnp.mean(xc * xc, axis=-1, keepdims=True)
    return xc * lax.rsqrt(var + LN_EPS) * w + b


def _pack_bf16_pairs(x):
    n = x.shape[1] // 2
    bits = pltpu.bitcast(x.astype(BF16).astype(F32), jnp.uint32)
    return lax.shift_right_logical(bits[:, :n], jnp.uint32(16)) | bits[:, n:]


def _unpack_bf16_pairs(u):
    lo = pltpu.bitcast(lax.shift_left(u, jnp.uint32(16)), F32)
    hi = pltpu.bitcast(u & jnp.uint32(0xFFFF0000), F32)
    return jnp.concatenate([lo, hi], axis=1)


def _stack_heads(x):
    lane = lax.broadcasted_iota(jnp.int32, x.shape, 1)
    return jnp.concatenate([jnp.where(lane < HEAD_DIM, x, 0.0), jnp.where(lane >= HEAD_DIM, x, 0.0)], axis=0)


def _wkv_kernel(r_ref, lw_ref, k_ref, v_ref, kk_ref, alr_ref, g_ref, bonus_ref, gnw_ref, gnb_ref, tri_ref,
                segm_ref, y_ref, state_ref):
    C = WKV_CHUNK
    nb, _, width = lw_ref.shape
    npair = width // PAIR

    @pl.when(pl.program_id(1) == 0)
    def _():
        state_ref[...] = jnp.zeros_like(state_ref)

    ri = lax.broadcasted_iota(jnp.int32, (2 * C, 2 * C), 0)
    ci = lax.broadcasted_iota(jnp.int32, (2 * C, 2 * C), 1)
    same = (ri // C) == (ci // C)
    strict = same & ((ci % C) < (ri % C))
    incl = same & ((ci % C) <= (ri % C))
    eye = (ri == ci).astype(F32)

    lhs, rhs, a2s, r2s, v2s, bhts, bkts, ptots = [], [], [], [], [], [], [], []
    for bi in range(nb):
        lw = lw_ref[bi]
        cum = jnp.dot(tri_ref[...], lw, precision=HIGHEST, preferred_element_type=F32)
        total = cum[C - 1:C, :]
        p_inv = jnp.exp(-cum)
        p_rem = jnp.exp(total - cum)
        p_tot = jnp.exp(total)
        kk = kk_ref[bi]
        k2 = k_ref[bi]
        b = kk * alr_ref[bi]
        a_t = -kk * jnp.exp(cum - lw)
        r_t = r_ref[bi] * jnp.exp(cum)
        b_t = b * p_inv
        k_t = k2 * p_inv
        b_h = b * p_rem
        k_h = k2 * p_rem
        v = v_ref[bi]
        for p in range(npair):
            sl = slice(p * PAIR, (p + 1) * PAIR)
            a2, r2, b2, kt2 = (_stack_heads(t[:, sl]) for t in (a_t, r_t, b_t, k_t))
            bh2, kh2, v2 = (_stack_heads(t[:, sl]) for t in (b_h, k_h, v))
            lhs.append(jnp.concatenate([a2, r2], axis=0))
            rhs.append(jnp.concatenate([b2, kt2], axis=0))
            a2s.append(a2)
            r2s.append(r2)
            v2s.append(v2)
            bhts.append(bh2.T)
            bkts.append(jnp.concatenate([bh2.T, kh2.T], axis=1))
            ptots.append(p_tot[:, sl])

    chains = range(nb * npair)
    m = [_dot_t(lhs[c], rhs[c]) for c in chains]
    n_ab = [jnp.where(strict, m[c][:2 * C, :2 * C], 0.0) for c in chains]
    m_ak = [jnp.where(strict, m[c][:2 * C, 2 * C:], 0.0) for c in chains]
    m_rb = [jnp.where(incl, m[c][2 * C:, :2 * C], 0.0) for c in chains]
    m_rk = [jnp.where(incl, m[c][2 * C:, 2 * C:], 0.0) for c in chains]
    mv = [_dot(m_ak[c], v2s[c]) for c in chains]
    mrkv = [_dot(m_rk[c], v2s[c]) for c in chains]
    inv = [eye + n_ab[c] for c in chains]
    pw = n_ab
    for _ in range(C.bit_length() - 2):
        pw = [_dot(pw[c], pw[c]) for c in chains]
        inv = [inv[c] + _dot(inv[c], pw[c]) for c in chains]
    wu = [_dot(inv[c], jnp.concatenate([a2s[c], mv[c]], axis=1)) for c in chains]
    qy = [_dot(m_rb[c], wu[c]) + jnp.concatenate([r2s[c], mrkv[c]], axis=1) for c in chains]
    g_t = [_dot(bhts[c], wu[c][:, :PAIR]) + eye * ptots[c] for c in chains]
    h_t = [_dot(bkts[c], jnp.concatenate([wu[c][:, PAIR:], v2s[c]], axis=0)) for c in chains]
    s0 = [state_ref[c] for c in chains]
    y2 = [_dot(qy[c][:, :PAIR], s0[c]) + qy[c][:, PAIR:] for c in chains]
    for c in chains:
        state_ref[c] = _dot(g_t[c], s0[c]) + h_t[c]

    segm = segm_ref[...]
    for bi in range(nb):
        y = jnp.concatenate([y2[bi * npair + p][:C] + y2[bi * npair + p][C:] for p in range(npair)], axis=1)
        mean = _segsum(y, segm)
        yc = y - mean
        var = _segsum(yc * yc, segm)
        yn = yc * lax.rsqrt(var + RW_GN_EPS) * gnw_ref[...] + gnb_ref[...]
        y_ref[bi] = ((yn + bonus_ref[bi]) * g_ref[bi]).astype(y_ref.dtype)


def _wkv(r, lw, k2, v, kk, alr, g, bonus, gnw, gnb, tri, segm, B, S):
    T, width = r.shape
    C = WKV_CHUNK
    nb = WKV_BATCH
    nc = S // C
    fixed = lambda b, c: (0, 0)
    blk = pl.BlockSpec((nb, C, width), lambda b, c: (b, c, 0))
    vec = pl.BlockSpec((1, width), fixed)
    ins = [t.reshape(B, S, width) for t in (r, lw, k2, v, kk, alr, g, bonus)]
    y = pl.pallas_call(
        _wkv_kernel,
        grid=(B // nb, nc),
        in_specs=[blk] * 8 + [vec, vec, pl.BlockSpec((C, C), fixed), pl.BlockSpec((width, width), fixed)],
        out_specs=blk,
        out_shape=jax.ShapeDtypeStruct((B, S, width), BF16),
        scratch_shapes=[pltpu.VMEM((nb * width // PAIR, PAIR, PAIR), F32)],
        compiler_params=_cparams(("parallel", "arbitrary")),
        name="wkv",
    )(*ins, gnw, gnb, tri, segm)
    return y.reshape(T, width)


def _split3(x):
    hi = x.astype(BF16).astype(F32)
    mid = (x - hi).astype(BF16).astype(F32)
    return hi, mid, x - hi - mid


def _pieces(x, n):
    hi, mid, lo = _split3(x)
    return hi + pltpu.roll(mid, n, axis=1) + pltpu.roll(lo, 2 * n, axis=1)


def _spread_heads(x):
    lane = lax.broadcasted_iota(jnp.int32, (x.shape[0], PAIR), 1)
    groups = []
    for p in range(x.shape[1] // PAIR):
        blk = x[:, p * PAIR:(p + 1) * PAIR]
        groups.append(jnp.where(lane < HEAD_DIM, blk, 0.0))
        groups.append(jnp.where(lane < HEAD_DIM, pltpu.roll(blk, HEAD_DIM, axis=1), 0.0))
    return jnp.concatenate(groups, axis=1)


def _front_kernel(x_ref, lnw_ref, lnb_ref, w_ref, wfz_ref,
                  mu_ref, w0_ref, w2_ref, a0_ref, a2_ref, g2_ref, kkw_ref, kaw_ref, rk_ref, seg_ref,
                  bf_ref, qw_ref, kw_ref, segm_ref, tril_ref, place_ref, maskq_ref, maskk_ref,
                  oneq_ref, onek_ref, onev_ref,
                  h_ref, r_ref, lw_ref, k_ref, v_ref, kk_ref, alr_ref, g_ref, bonus_ref,
                  qa_ref, ka_ref, va_ref, og_ref, shift_carry, c_carry, *, width, nh):
    @pl.when(pl.program_id(1) == 0)
    def _():
        shift_carry[...] = jnp.zeros_like(shift_carry)
        c_carry[...] = jnp.zeros_like(c_carry)

    h = _layer_norm(x_ref[...], lnw_ref[...], lnb_ref[...])
    h_ref[...] = h
    proj = jnp.dot(h.astype(BF16), w_ref[...], preferred_element_type=F32)
    tm = proj.shape[0]
    n_rw = mu_ref.shape[1]

    p = proj[:, :n_rw]
    prev = pltpu.roll(p, 1, axis=0)
    first_row = lax.broadcasted_iota(jnp.int32, p.shape, 0) == 0
    prev = jnp.where(first_row, shift_carry[...], prev)
    shift_carry[...] = p[tm - 1:tm, :]
    ps = p + mu_ref[...] * (prev - p)
    r = ps[:, 0:width]
    k = ps[:, width:2 * width]
    v = ps[:, 2 * width:3 * width]
    lora = ps[:, 3 * width:]
    seg = seg_ref[...]
    wl = w0_ref[...] + _dot(jnp.tanh(lora), w2_ref[...])
    w_raw = -_softplus(-wl) - 0.5
    lw_ref[...] = -jnp.exp(w_raw)
    alr = _sigmoid(a0_ref[...] + _dot(lora, a2_ref[...]))
    g_ref[...] = _dot(_sigmoid(lora), g2_ref[...])
    kkp = k * kkw_ref[...]
    nrm = jnp.sqrt(_segsum(kkp * kkp, seg))
    kk_ref[...] = kkp / jnp.maximum(nrm, 1e-12)
    k2 = k * (1.0 + (alr - 1.0) * kaw_ref[...])
    bonus_ref[...] = _segsum(r * k2 * rk_ref[...], seg) * v
    r_ref[...] = r
    k_ref[...] = k2
    v_ref[...] = v
    alr_ref[...] = alr

    fw = qw_ref.shape[1]
    q = proj[:, n_rw:n_rw + fw]
    kx = proj[:, n_rw + fw:n_rw + 2 * fw]
    vx = proj[:, n_rw + 2 * fw:n_rw + 3 * fw]
    og_ref[...] = proj[:, n_rw + 3 * fw:n_rw + 4 * fw]
    segm = segm_ref[...]
    qn = q * lax.rsqrt(_segsum(q * q, segm) + QK_EPS) * (qw_ref[...] * (HEAD_DIM ** -0.5 * LOG2E))
    kn = kx * lax.rsqrt(_segsum(kx * kx, segm) + QK_EPS) * kw_ref[...]
    fblk = proj[:, n_rw + 4 * fw:]
    h_lo = (h - h.astype(BF16).astype(F32)).astype(BF16)
    fz = fblk + pltpu.roll(fblk, PAIR - nh, axis=1) + jnp.dot(h_lo, wfz_ref[...], preferred_element_type=F32)
    in_heads = lax.broadcasted_iota(jnp.int32, fz.shape, 1) < nh
    lf = jnp.where(in_heads, -_softplus(-(fz + bf_ref[...])), 0.0)
    c3 = jnp.dot(tril_ref[...], _pieces(lf, nh).astype(BF16), preferred_element_type=F32)
    c = c3 + pltpu.roll(c3, PAIR - nh, axis=1) + pltpu.roll(c3, PAIR - 2 * nh, axis=1)
    c = jnp.where(in_heads, c, 0.0) + c_carry[...]
    c_carry[...] = c[tm - 1:tm, :]
    placed = jnp.dot(_pieces(c * LOG2E, nh).astype(BF16), place_ref[...], preferred_element_type=F32)
    qa_ref[...] = (_spread_heads(qn) + placed * maskq_ref[...] + oneq_ref[...]).astype(qa_ref.dtype)
    ka_ref[...] = (_spread_heads(kn) - placed * maskk_ref[...] + onek_ref[...]).astype(ka_ref.dtype)
    va_ref[...] = (_spread_heads(vx) + onev_ref[...]).astype(va_ref.dtype)


def _front(x2, B, S, lnw, lnb, w_main, wfz_hi, rw_params, fx_params, rw_w, fx_w, nh):
    T, D = x2.shape
    wide = nh * PAIR
    tm = ROW_TILE
    ns = S // tm
    row = lambda b, s: (b * ns + s, 0)
    fixed = lambda b, s: (0, 0)
    const = lambda a: pl.BlockSpec(a.shape, fixed)

    lane = jnp.arange(wide) % PAIR
    head = jnp.arange(wide) // PAIR
    src = jnp.arange(PAIR)
    piece = (lane - HEAD_DIM) % 3
    place = ((src[:, None] == (piece * nh + head)[None, :]) & (src[:, None] < 3 * nh)
             & (lane >= HEAD_DIM)[None, :] & (lane < HEAD_DIM + 6)[None, :]).astype(BF16)
    mask_k = ((lane >= HEAD_DIM) & (lane < HEAD_DIM + 3)).astype(F32).reshape(1, wide)
    mask_q = ((lane >= HEAD_DIM + 3) & (lane < HEAD_DIM + 6)).astype(F32).reshape(1, wide)
    one_v = (lane >= HEAD_DIM).astype(F32).reshape(1, wide)
    consts = [lnw, lnb, w_main, wfz_hi, *rw_params, *fx_params, place, mask_q, mask_k, mask_k, mask_q, one_v]

    f32_out = lambda n: (pl.BlockSpec((tm, n), row), jax.ShapeDtypeStruct((T, n), F32))
    bf_out = lambda n: (pl.BlockSpec((tm, n), row), jax.ShapeDtypeStruct((T, n), BF16))
    outs = [f32_out(D)] + [f32_out(rw_w)] * 8 + [bf_out(wide)] * 3 + [f32_out(fx_w)]
    return pl.pallas_call(
        functools.partial(_front_kernel, width=rw_w, nh=nh),
        grid=(B, ns),
        in_specs=[pl.BlockSpec((tm, D), row)] + [const(a) for a in consts],
        out_specs=[o[0] for o in outs],
        out_shape=[o[1] for o in outs],
        scratch_shapes=[pltpu.VMEM((1, rw_params[0].shape[1]), F32), pltpu.VMEM((1, PAIR), F32)],
        compiler_params=pltpu.CompilerParams(dimension_semantics=("parallel", "arbitrary"),
                                             vmem_limit_bytes=FRONT_VMEM_LIMIT),
        name="front",
    )(x2, *consts)


def _fox_attn_kernel(qi_ref, kj_ref, q_ref, k_ref, v_ref, og_ref, o_ref, m_ref, acc_ref):
    i = qi_ref[pl.program_id(2)]
    j = kj_ref[pl.program_id(2)]
    tq = q_ref.shape[0]
    tk = k_ref.shape[0]
    heads = range(ATTN_HEADS)
    grp = lambda ref, h: ref[:, h * PAIR:(h + 1) * PAIR]

    @pl.when(j == 0)
    def _():
        m_ref[...] = jnp.full_like(m_ref, NEG_BIG)
        acc_ref[...] = jnp.zeros_like(acc_ref)

    def step(masked):
        s = [lax.dot_general(grp(q_ref, h), grp(k_ref, h), (((1,), (1,)), ((), ())), preferred_element_type=F32)
             for h in heads]
        if masked:
            row = lax.broadcasted_iota(jnp.int32, (tq, tk), 0)
            col = lax.broadcasted_iota(jnp.int32, (tq, tk), 1)
            s = [jnp.where(col <= row, s[h], NEG_BIG) for h in heads]
        m_old = [m_ref[h] for h in heads]
        m_new = [jnp.maximum(m_old[h], jnp.max(s[h], axis=1, keepdims=True)) for h in heads]
        alpha = [jnp.exp2(m_old[h] - m_new[h]) for h in heads]
        pr = [jnp.exp2(s[h] - jnp.concatenate([m_new[h]] * (tk // PAIR), axis=1)).astype(BF16) for h in heads]
        pv = [jnp.dot(pr[h], grp(v_ref, h), preferred_element_type=F32) for h in heads]
        for h in heads:
            acc_ref[h] = alpha[h] * acc_ref[h] + pv[h]
            m_ref[h] = m_new[h]

    @pl.when(j < i)
    def _():
        step(False)

    @pl.when(j == i)
    def _():
        step(True)
        lane = lax.broadcasted_iota(jnp.int32, (tq, PAIR), 1)
        outs = []
        for p in range(ATTN_HEADS // 2):
            a0 = acc_ref[2 * p]
            a1 = acc_ref[2 * p + 1]
            o0 = a0 * pltpu.roll(1.0 / a0, HEAD_DIM, axis=1)
            o1 = pltpu.roll(a1, HEAD_DIM, axis=1) * (1.0 / a1)
            outs.append(jnp.where(lane < HEAD_DIM, o0, o1))
        o = jnp.concatenate(outs, axis=1)
        o_ref[...] = (o * _sigmoid(og_ref[...])).astype(o_ref.dtype)


def _fox_attn(qa, ka, va, og, B, S, width):
    T = qa.shape[0]
    tq, tk = ATTN_TQ, ATTN_TK
    assert tq == tk
    nq = S // tq
    nh = ATTN_HEADS
    ow = nh * HEAD_DIM
    ngroup = width // ow
    pairs = [(i, j) for i in range(nq) for j in range(i + 1)]
    qi = jnp.array([ij[0] for ij in pairs], jnp.int32)
    kj = jnp.array([ij[1] for ij in pairs], jnp.int32)
    kv = pl.BlockSpec((tk, nh * PAIR), lambda b, p, t, qi, kj: (b * nq + kj[t], p))
    grid_spec = pltpu.PrefetchScalarGridSpec(
        num_scalar_prefetch=2,
        grid=(B, ngroup, len(pairs)),
        in_specs=[
            pl.BlockSpec((tq, nh * PAIR), lambda b, p, t, qi, kj: (b * nq + qi[t], p)), kv, kv,
            pl.BlockSpec((tq, ow), lambda b, p, t, qi, kj: (b * nq + qi[t], p)),
        ],
        out_specs=pl.BlockSpec((tq, ow), lambda b, p, t, qi, kj: (b * nq + qi[t], p)),
        scratch_shapes=[pltpu.VMEM((nh, tq, PAIR), F32), pltpu.VMEM((nh, tq, PAIR), F32)],
    )
    return pl.pallas_call(
        _fox_attn_kernel,
        grid_spec=grid_spec,
        out_shape=jax.ShapeDtypeStruct((T, width), BF16),
        compiler_params=_cparams(("parallel", "parallel", "arbitrary")),
        name="fox_attn",
    )(qi, kj, qa, ka, va, og)


def _out_ln_kernel(yrw_ref, yfx_ref, h_ref, wo1_ref, wo2_ref, lnw_ref, lnb_ref, rw_ref, rb_ref,
                   h1_ref, logit_ref, *, alpha):
    mix = (jnp.dot(yrw_ref[...], wo1_ref[...], preferred_element_type=F32)
           + jnp.dot(yfx_ref[...], wo2_ref[...], preferred_element_type=F32))
    h1 = _layer_norm(alpha * h_ref[...] + mix, lnw_ref[...], lnb_ref[...])
    h1_ref[...] = h1
    ne = rb_ref.shape[1]
    h_hi = h1.astype(BF16)
    h_lo = (h1 - h_hi.astype(F32)).astype(BF16)
    rw = rw_ref[...]
    first = jnp.dot(h_hi, rw, preferred_element_type=F32)
    second = jnp.dot(h_lo, rw[:, :ne], preferred_element_type=F32)
    logit_ref[...] = first[:, :ne] + first[:, ne:] + second + rb_ref[...]


def _out_ln(y_rw, y_fx, h0, wo1, wo2, lnw, lnb, rw, rb, alpha):
    T, D = h0.shape
    width = y_rw.shape[1]
    ne = rb.shape[1]
    tm = ROW_TILE
    row = lambda i: (i, 0)
    fixed = lambda i: (0, 0)
    rw_hi = rw.astype(BF16)
    rw_lo = (rw - rw_hi.astype(F32)).astype(BF16)
    rw = jnp.concatenate([rw_hi, rw_lo], axis=1)
    return pl.pallas_call(
        functools.partial(_out_ln_kernel, alpha=alpha),
        grid=(T // tm,),
        in_specs=[
            pl.BlockSpec((tm, width), row), pl.BlockSpec((tm, width), row), pl.BlockSpec((tm, D), row),
            pl.BlockSpec((width, D), fixed), pl.BlockSpec((width, D), fixed),
            pl.BlockSpec((1, D), fixed), pl.BlockSpec((1, D), fixed),
            pl.BlockSpec((D, 2 * ne), fixed), pl.BlockSpec((1, ne), fixed),
        ],
        out_specs=[pl.BlockSpec((tm, D), row), pl.BlockSpec((tm, ne), row)],
        out_shape=[jax.ShapeDtypeStruct((T, D), F32), jax.ShapeDtypeStruct((T, ne), F32)],
        compiler_params=_cparams(("parallel",)),
        name="out_ln",
    )(y_rw, y_fx, h0, wo1, wo2, lnw, lnb, rw, rb)


DEINT_COLS = 256
MOE_VMEM_LIMIT = 56 * 1024 * 1024


def _moe_kernel(bexp_ref, nused_ref, first_ref, nexte_ref, slot_ref, xbase_ref,
                x_hbm, w1_hbm, w2_hbm, b1g_ref, b1l_ref, b2_ref, perm_ref, o_ref,
                w1_stage, w2_stage, w1g_bf, w1l_bf, w2_bf, x_buf, sem, sem_x):
    i = pl.program_id(0)
    live = i < nused_ref[0]
    tm = x_buf.shape[1]

    def x_copy(blk, s):
        row0 = pl.multiple_of(xbase_ref[blk], ROW_ALIGN)
        return pltpu.make_async_copy(x_hbm.at[pl.ds(row0, tm), :], x_buf.at[s], sem_x.at[s])

    @pl.when(live & (i == 0))
    def _():
        x_copy(0, 0).start()

    def weight_copies(e, s):
        return (pltpu.make_async_copy(w1_hbm.at[e], w1_stage.at[s], sem.at[0, s]),
                pltpu.make_async_copy(w2_hbm.at[e], w2_stage.at[s], sem.at[1, s]))

    @pl.when(live & (i == 0))
    def _():
        for cp in weight_copies(bexp_ref[0], 0):
            cp.start()

    @pl.when(live & (first_ref[i] == 1))
    def _():
        s = slot_ref[i]
        for cp in weight_copies(bexp_ref[i], s):
            cp.wait()

        @pl.when(nexte_ref[i] >= 0)
        def _():
            for cp in weight_copies(nexte_ref[i], 1 - s):
                cp.start()

        half = DEINT_COLS // 2
        for c in range(w1_stage.shape[2] // DEINT_COLS):
            blk = w1_stage[s, :, c * DEINT_COLS:(c + 1) * DEINT_COLS].astype(BF16)
            out = jnp.dot(blk, perm_ref[...], preferred_element_type=F32).astype(BF16)
            w1g_bf[:, c * half:(c + 1) * half] = out[:, :half]
            w1l_bf[:, c * half:(c + 1) * half] = out[:, half:]
        w2_bf[...] = w2_stage[s].astype(BF16)

    @pl.when(live)
    def _():
        xs = i % 2
        x_copy(i, xs).wait()

        @pl.when(i + 1 < nused_ref[0])
        def _():
            x_copy(i + 1, 1 - xs).start()

        x = x_buf[xs].astype(BF16)
        x_glu = jnp.minimum(jnp.dot(x, w1g_bf[...], preferred_element_type=F32) + b1g_ref[0], SWIGLU_LIMIT)
        x_lin = jnp.clip(jnp.dot(x, w1l_bf[...], preferred_element_type=F32) + b1l_ref[0],
                         -SWIGLU_LIMIT, SWIGLU_LIMIT)
        act = x_glu * _sigmoid(SWIGLU_ALPHA * x_glu) * (x_lin + 1.0)
        o = jnp.dot(act.astype(BF16), w2_bf[...], preferred_element_type=F32) + b2_ref[0]
        o_ref[...] = _pack_bf16_pairs(o)

    @pl.when(jnp.logical_not(live))
    def _():
        o_ref[...] = jnp.zeros_like(o_ref)


def _moe_ffn(block_exp, n_used, x_base, n_blocks, xs, w1, b1g, b1l, w2, b2):
    E, D, F2 = w1.shape
    F = F2 // 2
    tm = MOE_TILE
    n_rows = n_blocks * tm

    idx = jnp.arange(n_blocks, dtype=jnp.int32)
    first = jnp.concatenate([jnp.ones((1,), jnp.bool_), block_exp[1:] != block_exp[:-1]])
    slot = ((jnp.cumsum(first.astype(jnp.int32)) - 1) % 2).astype(jnp.int32)
    cand = jnp.where(first & (idx < n_used[0]), idx, n_blocks)
    next_first = jnp.min(jnp.where(idx[None, :] > idx[:, None], cand[None, :], n_blocks), axis=1)
    next_e = jnp.where(next_first < n_blocks, block_exp[jnp.minimum(next_first, n_blocks - 1)], -1).astype(jnp.int32)

    half = DEINT_COLS // 2
    src = jnp.arange(DEINT_COLS)
    dst = jnp.where(src % 2 == 0, src // 2, half + src // 2)
    perm = (dst[:, None] == jnp.arange(DEINT_COLS)[None, :]).astype(BF16)

    live = lambda i, be, nu: jnp.minimum(i, nu[0] - 1)
    bspec = lambda n: pl.BlockSpec((1, 1, n), lambda i, be, nu, fi, ne, sl, xb: (be[live(i, be, nu)], 0, 0))
    grid_spec = pltpu.PrefetchScalarGridSpec(
        num_scalar_prefetch=6,
        grid=(n_blocks,),
        in_specs=[
            pl.BlockSpec(memory_space=pl.ANY),
            pl.BlockSpec(memory_space=pl.ANY),
            pl.BlockSpec(memory_space=pl.ANY),
            bspec(F), bspec(F), bspec(D),
            pl.BlockSpec((DEINT_COLS, DEINT_COLS), lambda i, be, nu, fi, ne, sl, xb: (0, 0)),
        ],
        out_specs=pl.BlockSpec((tm, D // 2), lambda i, be, nu, fi, ne, sl, xb: (i, 0)),
        scratch_shapes=[
            pltpu.VMEM((2, D, F2), F32), pltpu.VMEM((2, F, D), F32),
            pltpu.VMEM((D, F), BF16), pltpu.VMEM((D, F), BF16), pltpu.VMEM((F, D), BF16),
            pltpu.VMEM((2, tm, D), F32),
            pltpu.SemaphoreType.DMA((2, 2)), pltpu.SemaphoreType.DMA((2,)),
        ],
    )
    return pl.pallas_call(
        _moe_kernel,
        grid_spec=grid_spec,
        out_shape=jax.ShapeDtypeStruct((n_rows, D // 2), jnp.uint32),
        compiler_params=pltpu.CompilerParams(dimension_semantics=("arbitrary",), vmem_limit_bytes=MOE_VMEM_LIMIT),
        name="moe_ffn",
    )(block_exp, n_used, first.astype(jnp.int32), next_e, slot, x_base, xs, w1, w2, b1g, b1l, b2, perm)


def _combine_ln_kernel(h_ref, eo_ref, gate_ref, lnw_ref, lnb_ref, o_ref, *, alpha):
    gates = gate_ref[...]
    ffn = _unpack_bf16_pairs(eo_ref[0]) * gates[:, 0:1]
    for j in range(1, TOP_K):
        ffn = ffn + _unpack_bf16_pairs(eo_ref[j]) * gates[:, j:j + 1]
    o_ref[...] = _layer_norm(alpha * h_ref[...] + ffn, lnw_ref[...], lnb_ref[...])


def _combine_ln(h1, eo, gates, lnw, lnb, alpha):
    T, D = h1.shape
    tm = ROW_TILE
    return pl.pallas_call(
        functools.partial(_combine_ln_kernel, alpha=alpha),
        grid=(T // tm,),
        in_specs=[
            pl.BlockSpec((tm, D), lambda i: (i, 0)),
            pl.BlockSpec((TOP_K, tm, D // 2), lambda i: (0, i, 0)),
            pl.BlockSpec((tm, TOP_K), lambda i: (i, 0)),
            pl.BlockSpec((1, D), lambda i: (0, 0)),
            pl.BlockSpec((1, D), lambda i: (0, 0)),
        ],
        out_specs=pl.BlockSpec((tm, D), lambda i: (i, 0)),
        out_shape=jax.ShapeDtypeStruct((T, D), F32),
        compiler_params=_cparams(("parallel",)),
        name="combine_ln",
    )(h1, eo, gates, lnw, lnb)


def _pad_to(x, n, axis):
    pad = [(0, 0)] * x.ndim
    pad[axis] = (0, n - x.shape[axis])
    return jnp.pad(x, pad)


def _block_diag_ones(width, value=1.0):
    idx = jnp.arange(width) // HEAD_DIM
    return jnp.where(idx[:, None] == idx[None, :], value, 0.0).astype(BF16)


def _route(logits, n_experts, tile):
    T = logits.shape[0]
    top_val, top_idx = lax.top_k(logits[:, :n_experts], TOP_K)
    gates = jax.nn.softmax(top_val, axis=-1)
    e_flat = top_idx.reshape(-1).astype(jnp.int32)
    n_assign = T * TOP_K
    n_sorted = n_assign + ROW_ALIGN * n_experts
    key_base = 1 << (n_sorted - 1).bit_length()
    assert n_experts * key_base < 2 ** 31
    n_blocks = n_assign // tile + n_experts + 1
    eids = jnp.arange(n_experts, dtype=jnp.int32)
    aids = jnp.arange(n_assign, dtype=jnp.int32)
    dummy = (eids[:, None] * key_base + (key_base - 1 - jnp.arange(ROW_ALIGN, dtype=jnp.int32))[None, :]).reshape(-1)
    skeys = lax.sort(jnp.concatenate([e_flat * key_base + aids, dummy]))
    e_sorted = skeys // key_base
    a_sorted = skeys % key_base
    tok_sorted = jnp.where(a_sorted < n_assign, a_sorted // TOP_K, 0)
    counts = jnp.sum((e_flat[:, None] == eids[None, :]).astype(jnp.int32), axis=0)
    seg_start = jnp.cumsum(counts) - counts + ROW_ALIGN * eids
    base = seg_start // ROW_ALIGN * ROW_ALIGN
    rows = seg_start - base + counts
    padded = (rows + tile - 1) // tile * tile
    pends = jnp.cumsum(padded)
    pstarts = pends - padded
    shift = pstarts - base
    qids = jnp.arange(n_sorted, dtype=jnp.int32)
    dest_sorted = qids + jnp.sum(jnp.where(e_sorted[:, None] == eids[None, :], shift[None, :], 0), axis=1)
    _, dest = lax.sort((a_sorted, dest_sorted), num_keys=1)
    block_start = jnp.arange(n_blocks, dtype=jnp.int32) * tile
    block_exp = jnp.minimum(jnp.sum((pends[None, :] <= block_start[:, None]).astype(jnp.int32), axis=1),
                            n_experts - 1).astype(jnp.int32)
    n_used = (pends[-1] // tile).astype(jnp.int32).reshape(1)
    of_block = lambda per_expert: jnp.sum(
        jnp.where(block_exp[:, None] == eids[None, :], per_expert[None, :], 0), axis=1)
    x_base = jnp.clip(block_start - of_block(shift), 0, n_sorted).astype(jnp.int32)
    gather_tok = _pad_to(tok_sorted, n_sorted + tile, 0).astype(jnp.int32)
    return gates, dest[:n_assign].reshape(T, TOP_K), gather_tok, block_exp, n_used, x_base, n_blocks


def kernel(x, ln_in_w, ln_in_b, w_in, rw_mu, rw_w0, rw_w2, rw_a0, rw_a2, rw_g2, rw_k_k, rw_k_a, rw_r_k,
           rw_gn_w, rw_gn_b, fx_b_f, fx_q_norm, fx_k_norm, w_o, ln1_w, ln1_b, router_w, router_b,
           exp_w1, exp_b1, exp_w2, exp_b2, ln2_w, ln2_b):
    B, S, D = x.shape
    T = B * S
    depth = w_in.shape[0]
    alpha = (2 * depth) ** 0.25
    rw_w = rw_w0.shape[1]
    fx_heads = fx_b_f.shape[1]
    fx_w = fx_heads * HEAD_DIM
    d_lora, a_lora, g_lora = rw_w2.shape[1], rw_a2.shape[1], rw_g2.shape[1]
    n_lora = d_lora + a_lora + g_lora
    lora_pad = -(-n_lora // 128) * 128
    rw_cols = 3 * rw_w + n_lora
    n_rw = 3 * rw_w + lora_pad
    n_experts = router_w.shape[2]
    ne_pad = -(-n_experts // 128) * 128
    row = lambda a: a.reshape(1, -1)

    seg_rw = _block_diag_ones(rw_w)
    segm_rw = _block_diag_ones(rw_w, 1.0 / HEAD_DIM)
    segm_fx = _block_diag_ones(fx_w, 1.0 / HEAD_DIM)
    tidx = jnp.arange(ROW_TILE)
    tril = (tidx[:, None] >= tidx[None, :]).astype(BF16)
    cidx = jnp.arange(WKV_CHUNK)
    tri_c = (cidx[:, None] >= cidx[None, :]).astype(F32)

    assert depth == 1, "single-layer block"
    l = 0
    x2 = x.reshape(T, D)
    w_l = w_in[l]
    wfz = w_l[:, rw_cols + 4 * fx_w:]
    wfz_hi = wfz.astype(BF16)
    wfz_lo = (wfz - wfz_hi.astype(F32)).astype(BF16)
    w_main = jnp.concatenate(
        [_pad_to(w_l[:, :rw_cols], n_rw, 1).astype(BF16), w_l[:, rw_cols:rw_cols + 4 * fx_w].astype(BF16),
         _pad_to(jnp.concatenate([wfz_hi, wfz_lo], axis=1), PAIR, 1)], axis=1)
    mu = _pad_to(row(rw_mu[l]), n_rw, 1)
    w2p = _pad_to(rw_w2[l], lora_pad, 0).astype(BF16)
    a2p = _pad_to(jnp.pad(rw_a2[l], ((d_lora, 0), (0, 0))), lora_pad, 0).astype(BF16)
    g2p = _pad_to(jnp.pad(rw_g2[l], ((d_lora + a_lora, 0), (0, 0))), lora_pad, 0).astype(BF16)
    rw_params = [mu, row(rw_w0[l]), w2p, row(rw_a0[l]), a2p, g2p,
                 row(rw_k_k[l]), row(rw_k_a[l]), row(rw_r_k[l]), seg_rw]
    qw = row(jnp.tile(fx_q_norm[l], fx_heads))
    kw = row(jnp.tile(fx_k_norm[l], fx_heads))
    fx_params = [_pad_to(row(fx_b_f[l]), PAIR, 1), qw, kw, segm_fx, tril]
    h0, r, lw, k2, v, kk, alr, g, bonus, qa, ka, va, og = _front(
        x2, B, S, row(ln_in_w), row(ln_in_b), w_main, _pad_to(wfz_hi, PAIR, 1), rw_params, fx_params,
        rw_w, fx_w, fx_heads)

    y_rw = _wkv(r, lw, k2, v, kk, alr, g, bonus, row(rw_gn_w[l]), row(rw_gn_b[l]), tri_c, segm_rw, B, S)
    y_fx = _fox_attn(qa, ka, va, og, B, S, fx_w)

    wo = w_o[l].astype(BF16)
    rw_pad = _pad_to(router_w[l], ne_pad, 1)
    rb_pad = _pad_to(row(router_b[l]), ne_pad, 1)
    h1, logits = _out_ln(y_rw, y_fx, h0, wo[:rw_w], wo[rw_w:], row(ln1_w[l]), row(ln1_b[l]),
                         rw_pad, rb_pad, alpha)

    gates, pos, gather_tok, block_exp, n_used, x_base, n_blocks = _route(logits, n_experts, MOE_TILE)
    xs = h1[gather_tok]
    b1 = exp_b1[l]
    b1g = b1[:, None, 0::2]
    b1l = b1[:, None, 1::2]
    eo_rows = _moe_ffn(block_exp, n_used, x_base, n_blocks, xs, exp_w1[l], b1g, b1l, exp_w2[l],
                       exp_b2[l][:, None, :])
    eo = eo_rows[pos.T]
    h = _combine_ln(h1, eo, gates, row(ln2_w[l]), row(ln2_b[l]), alpha)
    return h.reshape(B, S, D)
```

```python
import functools

import jax
import jax.numpy as jnp
from jax import lax
from jax.experimental import pallas as pl
from jax.experimental.pallas import tpu as pltpu

F32 = jnp.float32
BF16 = jnp.bfloat16
HIGHEST = lax.Precision.HIGHEST

HEAD_DIM = 64
PAIR = 2 * HEAD_DIM
WKV_CHUNK = 64
RW_GN_EPS = 64e-5
QK_EPS = 1e-6
LN_EPS = 1e-5
TOP_K = 4
SWIGLU_ALPHA = 1.702
SWIGLU_LIMIT = 7.0
NEG_BIG = -1e30
LOG2E = 1.4426950408889634

ROW_TILE = 256
ATTN_TQ = 512
ATTN_TK = 512
ATTN_HEADS = 8
MOE_TILE = 512
ROW_ALIGN = 8
WKV_BATCH = 4
VMEM_LIMIT = 48 * 1024 * 1024
FRONT_VMEM_LIMIT = 56 * 1024 * 1024


def _cparams(sem):
    return pltpu.CompilerParams(dimension_semantics=sem, vmem_limit_bytes=VMEM_LIMIT)


def _dot(a, b):
    return jnp.dot(a.astype(BF16), b.astype(BF16), preferred_element_type=F32)


def _dot_t(a, b):
    return lax.dot_general(a.astype(BF16), b.astype(BF16), (((1,), (1,)), ((), ())),
                           preferred_element_type=F32)


def _segsum(x, seg):
    return jnp.dot(x.astype(BF16), seg, preferred_element_type=F32)


def _sigmoid(x):
    return 1.0 / (1.0 + jnp.exp(-x))


def _softplus(x):
    return jnp.maximum(x, 0.0) + jnp.log(1.0 + jnp.exp(-jnp.abs(x)))


def _layer_norm(x, w, b):
    mu = jnp.mean(x, axis=-1, keepdims=True)
    xc = x - mu
    var = jnp.mean(xc * xc, axis=-1, keepdims=True)
    return xc * lax.rsqrt(var + LN_EPS) * w + b


def _pack_bf16_pairs(x):
    n = x.shape[1] // 2
    bits = pltpu.bitcast(x.astype(BF16).astype(F32), jnp.uint32)
    return lax.shift_right_logical(bits[:, :n], jnp.uint32(16)) | bits[:, n:]


def _unpack_bf16_pairs(u):
    lo = pltpu.bitcast(lax.shift_left(u, jnp.uint32(16)), F32)
    hi = pltpu.bitcast(u & jnp.uint32(0xFFFF0000), F32)
    return jnp.concatenate([lo, hi], axis=1)


def _stack_heads(x):
    lane = lax.broadcasted_iota(jnp.int32, x.shape, 1)
    return jnp.concatenate([jnp.where(lane < HEAD_DIM, x, 0.0), jnp.where(lane >= HEAD_DIM, x, 0.0)], axis=0)


def _wkv_kernel(r_ref, lw_ref, k_ref, v_ref, kk_ref, alr_ref, g_ref, bonus_ref, gnw_ref, gnb_ref, tri_ref,
                segm_ref, y_ref, state_ref):
    C = WKV_CHUNK
    nb, _, width = lw_ref.shape
    npair = width // PAIR

    @pl.when(pl.program_id(1) == 0)
    def _():
        state_ref[...] = jnp.zeros_like(state_ref)

    ri = lax.broadcasted_iota(jnp.int32, (2 * C, 2 * C), 0)
    ci = lax.broadcasted_iota(jnp.int32, (2 * C, 2 * C), 1)
    same = (ri // C) == (ci // C)
    strict = same & ((ci % C) < (ri % C))
    incl = same & ((ci % C) <= (ri % C))
    eye = (ri == ci).astype(F32)

    lhs, rhs, a2s, r2s, v2s, bhts, bkts, ptots = [], [], [], [], [], [], [], []
    for bi in range(nb):
        lw = lw_ref[bi]
        cum = jnp.dot(tri_ref[...], lw, precision=HIGHEST, preferred_element_type=F32)
        total = cum[C - 1:C, :]
        p_inv = jnp.exp(-cum)
        p_rem = jnp.exp(total - cum)
        p_tot = jnp.exp(total)
        kk = kk_ref[bi]
        k2 = k_ref[bi]
        b = kk * alr_ref[bi]
        a_t = -kk * jnp.exp(cum - lw)
        r_t = r_ref[bi] * jnp.exp(cum)
        b_t = b * p_inv
        k_t = k2 * p_inv
        b_h = b * p_rem
        k_h = k2 * p_rem
        v = v_ref[bi]
        for p in range(npair):
            sl = slice(p * PAIR, (p + 1) * PAIR)
            a2, r2, b2, kt2 = (_stack_heads(t[:, sl]) for t in (a_t, r_t, b_t, k_t))
            bh2, kh2, v2 = (_stack_heads(t[:, sl]) for t in (b_h, k_h, v))
            lhs.append(jnp.concatenate([a2, r2], axis=0))
            rhs.append(jnp.concatenate([b2, kt2], axis=0))
            a2s.append(a2)
            r2s.append(r2)
            v2s.append(v2)
            bhts.append(bh2.T)
            bkts.append(jnp.concatenate([bh2.T, kh2.T], axis=1))
            ptots.append(p_tot[:, sl])

    chains = range(nb * npair)
    m = [_dot_t(lhs[c], rhs[c]) for c in chains]
    n_ab = [jnp.where(strict, m[c][:2 * C, :2 * C], 0.0) for c in chains]
    m_ak = [jnp.where(strict, m[c][:2 * C, 2 * C:], 0.0) for c in chains]
    m_rb = [jnp.where(incl, m[c][2 * C:, :2 * C], 0.0) for c in chains]
    m_rk = [jnp.where(incl, m[c][2 * C:, 2 * C:], 0.0) for c in chains]
    mv = [_dot(m_ak[c], v2s[c]) for c in chains]
    mrkv = [_dot(m_rk[c], v2s[c]) for c in chains]
    inv = [eye + n_ab[c] for c in chains]
    pw = n_ab
    for _ in range(C.bit_length() - 2):
        pw = [_dot(pw[c], pw[c]) for c in chains]
        inv = [inv[c] + _dot(inv[c], pw[c]) for c in chains]
    wu = [_dot(inv[c], jnp.concatenate([a2s[c], mv[c]], axis=1)) for c in chains]
    qy = [_dot(m_rb[c], wu[c]) + jnp.concatenate([r2s[c], mrkv[c]], axis=1) for c in chains]
    g_t = [_dot(bhts[c], wu[c][:, :PAIR]) + eye * ptots[c] for c in chains]
    h_t = [_dot(bkts[c], jnp.concatenate([wu[c][:, PAIR:], v2s[c]], axis=0)) for c in chains]
    s0 = [state_ref[c] for c in chains]
    y2 = [_dot(qy[c][:, :PAIR], s0[c]) + qy[c][:, PAIR:] for c in chains]
    for c in chains:
        state_ref[c] = _dot(g_t[c], s0[c]) + h_t[c]

    segm = segm_ref[...]
    for bi in range(nb):
        y = jnp.concatenate([y2[bi * npair + p][:C] + y2[bi * npair + p][C:] for p in range(npair)], axis=1)
        mean = _segsum(y, segm)
        yc = y - mean
        var = _segsum(yc * yc, segm)
        yn = yc * lax.rsqrt(var + RW_GN_EPS) * gnw_ref[...] + gnb_ref[...]
        y_ref[bi] = ((yn + bonus_ref[bi]) * g_ref[bi]).astype(y_ref.dtype)


def _wkv(r, lw, k2, v, kk, alr, g, bonus, gnw, gnb, tri, segm, B, S):
    T, width = r.shape
    C = WKV_CHUNK
    nb = WKV_BATCH
    nc = S // C
    fixed = lambda b, c: (0, 0)
    blk = pl.BlockSpec((nb, C, width), lambda b, c: (b, c, 0))
    vec = pl.BlockSpec((1, width), fixed)
    ins = [t.reshape(B, S, width) for t in (r, lw, k2, v, kk, alr, g, bonus)]
    y = pl.pallas_call(
        _wkv_kernel,
        grid=(B // nb, nc),
        in_specs=[blk] * 8 + [vec, vec, pl.BlockSpec((C, C), fixed), pl.BlockSpec((width, width), fixed)],
        out_specs=blk,
        out_shape=jax.ShapeDtypeStruct((B, S, width), BF16),
        scratch_shapes=[pltpu.VMEM((nb * width // PAIR, PAIR, PAIR), F32)],
        compiler_params=_cparams(("parallel", "arbitrary")),
        name="wkv",
    )(*ins, gnw, gnb, tri, segm)
    return y.reshape(T, width)


def _split3(x):
    hi = x.astype(BF16).astype(F32)
    mid = (x - hi).astype(BF16).astype(F32)
    return hi, mid, x - hi - mid


def _pieces(x, n):
    hi, mid, lo = _split3(x)
    return hi + pltpu.roll(mid, n, axis=1) + pltpu.roll(lo, 2 * n, axis=1)


def _spread_heads(x):
    lane = lax.broadcasted_iota(jnp.int32, (x.shape[0], PAIR), 1)
    groups = []
    for p in range(x.shape[1] // PAIR):
        blk = x[:, p * PAIR:(p + 1) * PAIR]
        groups.append(jnp.where(lane < HEAD_DIM, blk, 0.0))
        groups.append(jnp.where(lane < HEAD_DIM, pltpu.roll(blk, HEAD_DIM, axis=1), 0.0))
    return jnp.concatenate(groups, axis=1)


def _front_kernel(x_ref, lnw_ref, lnb_ref, w_ref, wfz_ref,
                  mu_ref, w0_ref, w2_ref, a0_ref, a2_ref, g2_ref, kkw_ref, kaw_ref, rk_ref, seg_ref,
                  bf_ref, qw_ref, kw_ref, segm_ref, tril_ref, place_ref, maskq_ref, maskk_ref,
                  oneq_ref, onek_ref, onev_ref,
                  h_ref, r_ref, lw_ref, k_ref, v_ref, kk_ref, alr_ref, g_ref, bonus_ref,
                  qa_ref, ka_ref, va_ref, og_ref, shift_carry, c_carry, *, width, nh):
    @pl.when(pl.program_id(1) == 0)
    def _():
        shift_carry[...] = jnp.zeros_like(shift_carry)
        c_carry[...] = jnp.zeros_like(c_carry)

    h = _layer_norm(x_ref[...], lnw_ref[...], lnb_ref[...])
    h_ref[...] = h
    proj = jnp.dot(h.astype(BF16), w_ref[...], preferred_element_type=F32)
    tm = proj.shape[0]
    n_rw = mu_ref.shape[1]

    p = proj[:, :n_rw]
    prev = pltpu.roll(p, 1, axis=0)
    first_row = lax.broadcasted_iota(jnp.int32, p.shape, 0) == 0
    prev = jnp.where(first_row, shift_carry[...], prev)
    shift_carry[...] = p[tm - 1:tm, :]
    ps = p + mu_ref[...] * (prev - p)
    r = ps[:, 0:width]
    k = ps[:, width:2 * width]
    v = ps[:, 2 * width:3 * width]
    lora = ps[:, 3 * width:]
    seg = seg_ref[...]
    wl = w0_ref[...] + _dot(jnp.tanh(lora), w2_ref[...])
    w_raw = -_softplus(-wl) - 0.5
    lw_ref[...] = -jnp.exp(w_raw)
    alr = _sigmoid(a0_ref[...] + _dot(lora, a2_ref[...]))
    g_ref[...] = _dot(_sigmoid(lora), g2_ref[...])
    kkp = k * kkw_ref[...]
    nrm = jnp.sqrt(_segsum(kkp * kkp, seg))
    kk_ref[...] = kkp / jnp.maximum(nrm, 1e-12)
    k2 = k * (1.0 + (alr - 1.0) * kaw_ref[...])
    bonus_ref[...] = _segsum(r * k2 * rk_ref[...], seg) * v
    r_ref[...] = r
    k_ref[...] = k2
    v_ref[...] = v
    alr_ref[...] = alr

    fw = qw_ref.shape[1]
    q = proj[:, n_rw:n_rw + fw]
    kx = proj[:, n_rw + fw:n_rw + 2 * fw]
    vx = proj[:, n_rw + 2 * fw:n_rw + 3 * fw]
    og_ref[...] = proj[:, n_rw + 3 * fw:n_rw + 4 * fw]
    segm = segm_ref[...]
    qn = q * lax.rsqrt(_segsum(q * q, segm) + QK_EPS) * (qw_ref[...] * (HEAD_DIM ** -0.5 * LOG2E))
    kn = kx * lax.rsqrt(_segsum(kx * kx, segm) + QK_EPS) * kw_ref[...]
    fblk = proj[:, n_rw + 4 * fw:]
    h_lo = (h - h.astype(BF16).astype(F32)).astype(BF16)
    fz = fblk + pltpu.roll(fblk, PAIR - nh, axis=1) + jnp.dot(h_lo, wfz_ref[...], preferred_element_type=F32)
    in_heads = lax.broadcasted_iota(jnp.int32, fz.shape, 1) < nh
    lf = jnp.where(in_heads, -_softplus(-(fz + bf_ref[...])), 0.0)
    c3 = jnp.dot(tril_ref[...], _pieces(lf, nh).astype(BF16), preferred_element_type=F32)
    c = c3 + pltpu.roll(c3, PAIR - nh, axis=1) + pltpu.roll(c3, PAIR - 2 * nh, axis=1)
    c = jnp.where(in_heads, c, 0.0) + c_carry[...]
    c_carry[...] = c[tm - 1:tm, :]
    placed = jnp.dot(_pieces(c * LOG2E, nh).astype(BF16), place_ref[...], preferred_element_type=F32)
    qa_ref[...] = (_spread_heads(qn) + placed * maskq_ref[...] + oneq_ref[...]).astype(qa_ref.dtype)
    ka_ref[...] = (_spread_heads(kn) - placed * maskk_ref[...] + onek_ref[...]).astype(ka_ref.dtype)
    va_ref[...] = (_spread_heads(vx) + onev_ref[...]).astype(va_ref.dtype)


def _front(x2, B, S, lnw, lnb, w_main, wfz_hi, rw_params, fx_params, rw_w, fx_w, nh):
    T, D = x2.shape
    wide = nh * PAIR
    tm = ROW_TILE
    ns = S // tm
    row = lambda b, s: (b * ns + s, 0)
    fixed = lambda b, s: (0, 0)
    const = lambda a: pl.BlockSpec(a.shape, fixed)

    lane = jnp.arange(wide) % PAIR
    head = jnp.arange(wide) // PAIR
    src = jnp.arange(PAIR)
    piece = (lane - HEAD_DIM) % 3
    place = ((src[:, None] == (piece * nh + head)[None, :]) & (src[:, None] < 3 * nh)
             & (lane >= HEAD_DIM)[None, :] & (lane < HEAD_DIM + 6)[None, :]).astype(BF16)
    mask_k = ((lane >= HEAD_DIM) & (lane < HEAD_DIM + 3)).astype(F32).reshape(1, wide)
    mask_q = ((lane >= HEAD_DIM + 3) & (lane < HEAD_DIM + 6)).astype(F32).reshape(1, wide)
    one_v = (lane >= HEAD_DIM).astype(F32).reshape(1, wide)
    consts = [lnw, lnb, w_main, wfz_hi, *rw_params, *fx_params, place, mask_q, mask_k, mask_k, mask_q, one_v]

    f32_out = lambda n: (pl.BlockSpec((tm, n), row), jax.ShapeDtypeStruct((T, n), F32))
    bf_out = lambda n: (pl.BlockSpec((tm, n), row), jax.ShapeDtypeStruct((T, n), BF16))
    outs = [f32_out(D)] + [f32_out(rw_w)] * 8 + [bf_out(wide)] * 3 + [f32_out(fx_w)]
    return pl.pallas_call(
        functools.partial(_front_kernel, width=rw_w, nh=nh),
        grid=(B, ns),
        in_specs=[pl.BlockSpec((tm, D), row)] + [const(a) for a in consts],
        out_specs=[o[0] for o in outs],
        out_shape=[o[1] for o in outs],
        scratch_shapes=[pltpu.VMEM((1, rw_params[0].shape[1]), F32), pltpu.VMEM((1, PAIR), F32)],
        compiler_params=pltpu.CompilerParams(dimension_semantics=("parallel", "arbitrary"),
                                             vmem_limit_bytes=FRONT_VMEM_LIMIT),
        name="front",
    )(x2, *consts)


def _fox_attn_kernel(qi_ref, kj_ref, q_ref, k_ref, v_ref, og_ref, o_ref, m_ref, acc_ref):
    i = qi_ref[pl.program_id(2)]
    j = kj_ref[pl.program_id(2)]
    tq = q_ref.shape[0]
    tk = k_ref.shape[0]
    heads = range(ATTN_HEADS)
    grp = lambda ref, h: ref[:, h * PAIR:(h + 1) * PAIR]

    @pl.when(j == 0)
    def _():
        m_ref[...] = jnp.full_like(m_ref, NEG_BIG)
        acc_ref[...] = jnp.zeros_like(acc_ref)

    def step(masked):
        s = [lax.dot_general(grp(q_ref, h), grp(k_ref, h), (((1,), (1,)), ((), ())), preferred_element_type=F32)
             for h in heads]
        if masked:
            row = lax.broadcasted_iota(jnp.int32, (tq, tk), 0)
            col = lax.broadcasted_iota(jnp.int32, (tq, tk), 1)
            s = [jnp.where(col <= row, s[h], NEG_BIG) for h in heads]
        m_old = [m_ref[h] for h in heads]
        m_new = [jnp.maximum(m_old[h], jnp.max(s[h], axis=1, keepdims=True)) for h in heads]
        alpha = [jnp.exp2(m_old[h] - m_new[h]) for h in heads]
        pr = [jnp.exp2(s[h] - jnp.concatenate([m_new[h]] * (tk // PAIR), axis=1)).astype(BF16) for h in heads]
        pv = [jnp.dot(pr[h], grp(v_ref, h), preferred_element_type=F32) for h in heads]
        for h in heads:
            acc_ref[h] = alpha[h] * acc_ref[h] + pv[h]
            m_ref[h] = m_new[h]

    @pl.when(j < i)
    def _():
        step(False)

    @pl.when(j == i)
    def _():
        step(True)
        lane = lax.broadcasted_iota(jnp.int32, (tq, PAIR), 1)
        outs = []
        for p in range(ATTN_HEADS // 2):
            a0 = acc_ref[2 * p]
            a1 = acc_ref[2 * p + 1]
            o0 = a0 * pltpu.roll(1.0 / a0, HEAD_DIM, axis=1)
            o1 = pltpu.roll(a1, HEAD_DIM, axis=1) * (1.0 / a1)
            outs.append(jnp.where(lane < HEAD_DIM, o0, o1))
        o = jnp.concatenate(outs, axis=1)
        o_ref[...] = (o * _sigmoid(og_ref[...])).astype(o_ref.dtype)


def _fox_attn(qa, ka, va, og, B, S, width):
    T = qa.shape[0]
    tq, tk = ATTN_TQ, ATTN_TK
    assert tq == tk
    nq = S // tq
    nh = ATTN_HEADS
    ow = nh * HEAD_DIM
    ngroup = width // ow
    pairs = [(i, j) for i in range(nq) for j in range(i + 1)]
    qi = jnp.array([ij[0] for ij in pairs], jnp.int32)
    kj = jnp.array([ij[1] for ij in pairs], jnp.int32)
    kv = pl.BlockSpec((tk, nh * PAIR), lambda b, p, t, qi, kj: (b * nq + kj[t], p))
    grid_spec = pltpu.PrefetchScalarGridSpec(
        num_scalar_prefetch=2,
        grid=(B, ngroup, len(pairs)),
        in_specs=[
            pl.BlockSpec((tq, nh * PAIR), lambda b, p, t, qi, kj: (b * nq + qi[t], p)), kv, kv,
            pl.BlockSpec((tq, ow), lambda b, p, t, qi, kj: (b * nq + qi[t], p)),
        ],
        out_specs=pl.BlockSpec((tq, ow), lambda b, p, t, qi, kj: (b * nq + qi[t], p)),
        scratch_shapes=[pltpu.VMEM((nh, tq, PAIR), F32), pltpu.VMEM((nh, tq, PAIR), F32)],
    )
    return pl.pallas_call(
        _fox_attn_kernel,
        grid_spec=grid_spec,
        out_shape=jax.ShapeDtypeStruct((T, width), BF16),
        compiler_params=_cparams(("parallel", "parallel", "arbitrary")),
        name="fox_attn",
    )(qi, kj, qa, ka, va, og)


def _out_ln_kernel(yrw_ref, yfx_ref, h_ref, wo1_ref, wo2_ref, lnw_ref, lnb_ref, rw_ref, rb_ref,
                   h1_ref, logit_ref, *, alpha):
    mix = (jnp.dot(yrw_ref[...], wo1_ref[...], preferred_element_type=F32)
           + jnp.dot(yfx_ref[...], wo2_ref[...], preferred_element_type=F32))
    h1 = _layer_norm(alpha * h_ref[...] + mix, lnw_ref[...], lnb_ref[...])
    h1_ref[...] = h1
    ne = rb_ref.shape[1]
    h_hi = h1.astype(BF16)
    h_lo = (h1 - h_hi.astype(F32)).astype(BF16)
    rw = rw_ref[...]
    first = jnp.dot(h_hi, rw, preferred_element_type=F32)
    second = jnp.dot(h_lo, rw[:, :ne], preferred_element_type=F32)
    logit_ref[...] = first[:, :ne] + first[:, ne:] + second + rb_ref[...]


def _out_ln(y_rw, y_fx, h0, wo1, wo2, lnw, lnb, rw, rb, alpha):
    T, D = h0.shape
    width = y_rw.shape[1]
    ne = rb.shape[1]
    tm = ROW_TILE
    row = lambda i: (i, 0)
    fixed = lambda i: (0, 0)
    rw_hi = rw.astype(BF16)
    rw_lo = (rw - rw_hi.astype(F32)).astype(BF16)
    rw = jnp.concatenate([rw_hi, rw_lo], axis=1)
    return pl.pallas_call(
        functools.partial(_out_ln_kernel, alpha=alpha),
        grid=(T // tm,),
        in_specs=[
            pl.BlockSpec((tm, width), row), pl.BlockSpec((tm, width), row), pl.BlockSpec((tm, D), row),
            pl.BlockSpec((width, D), fixed), pl.BlockSpec((width, D), fixed),
            pl.BlockSpec((1, D), fixed), pl.BlockSpec((1, D), fixed),
            pl.BlockSpec((D, 2 * ne), fixed), pl.BlockSpec((1, ne), fixed),
        ],
        out_specs=[pl.BlockSpec((tm, D), row), pl.BlockSpec((tm, ne), row)],
        out_shape=[jax.ShapeDtypeStruct((T, D), F32), jax.ShapeDtypeStruct((T, ne), F32)],
        compiler_params=_cparams(("parallel",)),
        name="out_ln",
    )(y_rw, y_fx, h0, wo1, wo2, lnw, lnb, rw, rb)


DEINT_COLS = 256
MOE_VMEM_LIMIT = 56 * 1024 * 1024


def _moe_kernel(bexp_ref, nused_ref, first_ref, nexte_ref, slot_ref, xbase_ref,
                x_hbm, w1_hbm, w2_hbm, b1g_ref, b1l_ref, b2_ref, perm_ref, o_ref,
                w1_stage, w2_stage, w1g_bf, w1l_bf, w2_bf, x_buf, sem, sem_x):
    i = pl.program_id(0)
    live = i < nused_ref[0]
    tm = x_buf.shape[1]

    def x_copy(blk, s):
        row0 = pl.multiple_of(xbase_ref[blk], ROW_ALIGN)
        return pltpu.make_async_copy(x_hbm.at[pl.ds(row0, tm), :], x_buf.at[s], sem_x.at[s])

    @pl.when(live & (i == 0))
    def _():
        x_copy(0, 0).start()

    def weight_copies(e, s):
        return (pltpu.make_async_copy(w1_hbm.at[e], w1_stage.at[s], sem.at[0, s]),
                pltpu.make_async_copy(w2_hbm.at[e], w2_stage.at[s], sem.at[1, s]))

    @pl.when(live & (i == 0))
    def _():
        for cp in weight_copies(bexp_ref[0], 0):
            cp.start()

    @pl.when(live & (first_ref[i] == 1))
    def _():
        s = slot_ref[i]
        for cp in weight_copies(bexp_ref[i], s):
            cp.wait()

        @pl.when(nexte_ref[i] >= 0)
        def _():
            for cp in weight_copies(nexte_ref[i], 1 - s):
                cp.start()

        half = DEINT_COLS // 2
        for c in range(w1_stage.shape[2] // DEINT_COLS):
            blk = w1_stage[s, :, c * DEINT_COLS:(c + 1) * DEINT_COLS].astype(BF16)
            out = jnp.dot(blk, perm_ref[...], preferred_element_type=F32).astype(BF16)
            w1g_bf[:, c * half:(c + 1) * half] = out[:, :half]
            w1l_bf[:, c * half:(c + 1) * half] = out[:, half:]
        w2_bf[...] = w2_stage[s].astype(BF16)

    @pl.when(live)
    def _():
        xs = i % 2
        x_copy(i, xs).wait()

        @pl.when(i + 1 < nused_ref[0])
        def _():
            x_copy(i + 1, 1 - xs).start()

        x = x_buf[xs].astype(BF16)
        x_glu = jnp.minimum(jnp.dot(x, w1g_bf[...], preferred_element_type=F32) + b1g_ref[0], SWIGLU_LIMIT)
        x_lin = jnp.clip(jnp.dot(x, w1l_bf[...], preferred_element_type=F32) + b1l_ref[0],
                         -SWIGLU_LIMIT, SWIGLU_LIMIT)
        act = x_glu * _sigmoid(SWIGLU_ALPHA * x_glu) * (x_lin + 1.0)
        o = jnp.dot(act.astype(BF16), w2_bf[...], preferred_element_type=F32) + b2_ref[0]
        o_ref[...] = _pack_bf16_pairs(o)

    @pl.when(jnp.logical_not(live))
    def _():
        o_ref[...] = jnp.zeros_like(o_ref)


def _moe_ffn(block_exp, n_used, x_base, n_blocks, xs, w1, b1g, b1l, w2, b2):
    E, D, F2 = w1.shape
    F = F2 // 2
    tm = MOE_TILE
    n_rows = n_blocks * tm

    idx = jnp.arange(n_blocks, dtype=jnp.int32)
    first = jnp.concatenate([jnp.ones((1,), jnp.bool_), block_exp[1:] != block_exp[:-1]])
    slot = ((jnp.cumsum(first.astype(jnp.int32)) - 1) % 2).astype(jnp.int32)
    cand = jnp.where(first & (idx < n_used[0]), idx, n_blocks)
    next_first = jnp.min(jnp.where(idx[None, :] > idx[:, None], cand[None, :], n_blocks), axis=1)
    next_e = jnp.where(next_first < n_blocks, block_exp[jnp.minimum(next_first, n_blocks - 1)], -1).astype(jnp.int32)

    half = DEINT_COLS // 2
    src = jnp.arange(DEINT_COLS)
    dst = jnp.where(src % 2 == 0, src // 2, half + src // 2)
    perm = (dst[:, None] == jnp.arange(DEINT_COLS)[None, :]).astype(BF16)

    live = lambda i, be, nu: jnp.minimum(i, nu[0] - 1)
    bspec = lambda n: pl.BlockSpec((1, 1, n), lambda i, be, nu, fi, ne, sl, xb: (be[live(i, be, nu)], 0, 0))
    grid_spec = pltpu.PrefetchScalarGridSpec(
        num_scalar_prefetch=6,
        grid=(n_blocks,),
        in_specs=[
            pl.BlockSpec(memory_space=pl.ANY),
            pl.BlockSpec(memory_space=pl.ANY),
            pl.BlockSpec(memory_space=pl.ANY),
            bspec(F), bspec(F), bspec(D),
            pl.BlockSpec((DEINT_COLS, DEINT_COLS), lambda i, be, nu, fi, ne, sl, xb: (0, 0)),
        ],
        out_specs=pl.BlockSpec((tm, D // 2), lambda i, be, nu, fi, ne, sl, xb: (i, 0)),
        scratch_shapes=[
            pltpu.VMEM((2, D, F2), F32), pltpu.VMEM((2, F, D), F32),
            pltpu.VMEM((D, F), BF16), pltpu.VMEM((D, F), BF16), pltpu.VMEM((F, D), BF16),
            pltpu.VMEM((2, tm, D), F32),
            pltpu.SemaphoreType.DMA((2, 2)), pltpu.SemaphoreType.DMA((2,)),
        ],
    )
    return pl.pallas_call(
        _moe_kernel,
        grid_spec=grid_spec,
        out_shape=jax.ShapeDtypeStruct((n_rows, D // 2), jnp.uint32),
        compiler_params=pltpu.CompilerParams(dimension_semantics=("arbitrary",), vmem_limit_bytes=MOE_VMEM_LIMIT),
        name="moe_ffn",
    )(block_exp, n_used, first.astype(jnp.int32), next_e, slot, x_base, xs, w1, w2, b1g, b1l, b2, perm)


def _combine_ln_kernel(h_ref, eo_ref, gate_ref, lnw_ref, lnb_ref, o_ref, *, alpha):
    gates = gate_ref[...]
    ffn = _unpack_bf16_pairs(eo_ref[0]) * gates[:, 0:1]
    for j in range(1, TOP_K):
        ffn = ffn + _unpack_bf16_pairs(eo_ref[j]) * gates[:, j:j + 1]
    o_ref[...] = _layer_norm(alpha * h_ref[...] + ffn, lnw_ref[...], lnb_ref[...])


def _combine_ln(h1, eo, gates, lnw, lnb, alpha):
    T, D = h1.shape
    tm = ROW_TILE
    return pl.pallas_call(
        functools.partial(_combine_ln_kernel, alpha=alpha),
        grid=(T // tm,),
        in_specs=[
            pl.BlockSpec((tm, D), lambda i: (i, 0)),
            pl.BlockSpec((TOP_K, tm, D // 2), lambda i: (0, i, 0)),
            pl.BlockSpec((tm, TOP_K), lambda i: (i, 0)),
            pl.BlockSpec((1, D), lambda i: (0, 0)),
            pl.BlockSpec((1, D), lambda i: (0, 0)),
        ],
        out_specs=pl.BlockSpec((tm, D), lambda i: (i, 0)),
        out_shape=jax.ShapeDtypeStruct((T, D), F32),
        compiler_params=_cparams(("parallel",)),
        name="combine_ln",
    )(h1, eo, gates, lnw, lnb)


def _pad_to(x, n, axis):
    pad = [(0, 0)] * x.ndim
    pad[axis] = (0, n - x.shape[axis])
    return jnp.pad(x, pad)


def _block_diag_ones(width, value=1.0):
    idx = jnp.arange(width) // HEAD_DIM
    return jnp.where(idx[:, None] == idx[None, :], value, 0.0).astype(BF16)


def _route(logits, n_experts, tile):
    T = logits.shape[0]
    top_val, top_idx = lax.top_k(logits[:, :n_experts], TOP_K)
    gates = jax.nn.softmax(top_val, axis=-1)
    e_flat = top_idx.reshape(-1).astype(jnp.int32)
    n_assign = T * TOP_K
    n_blocks = n_assign // tile + n_experts
    assert n_experts * n_assign < 2 ** 31
    eids = jnp.arange(n_experts, dtype=jnp.int32)
    aids = jnp.arange(n_assign, dtype=jnp.int32)
    skeys = lax.sort(e_flat * n_assign + aids)
    order = skeys % n_assign
    e_sorted = skeys // n_assign
    counts = jnp.sum((e_flat[:, None] == eids[None, :]).astype(jnp.int32), axis=0)
    starts = jnp.cumsum(counts) - counts
    padded = (counts + tile - 1) // tile * tile
    pends = jnp.cumsum(padded)
    pstarts = pends - padded
    shift = pstarts - starts
    dest_sorted = aids + jnp.sum(jnp.where(e_sorted[:, None] == eids[None, :], shift[None, :], 0), axis=1)
    _, dest = lax.sort((order, dest_sorted), num_keys=1)
    block_start = jnp.arange(n_blocks, dtype=jnp.int32) * tile
    block_exp = jnp.minimum(jnp.sum((pends[None, :] <= block_start[:, None]).astype(jnp.int32), axis=1),
                            n_experts - 1).astype(jnp.int32)
    n_used = (pends[-1] // tile).astype(jnp.int32).reshape(1)
    c_len = (counts + ROW_ALIGN - 1) // ROW_ALIGN * ROW_ALIGN
    c_ends = jnp.cumsum(c_len)
    c_starts = c_ends - c_len
    n_compact = n_assign + ROW_ALIGN * n_experts + tile
    of_block = lambda per_expert: jnp.sum(
        jnp.where(block_exp[:, None] == eids[None, :], per_expert[None, :], 0), axis=1)
    x_base = jnp.clip(of_block(c_starts - pstarts) + block_start, 0, n_compact - tile).astype(jnp.int32)
    rows = jnp.arange(n_compact, dtype=jnp.int32)
    row_exp = jnp.minimum(jnp.sum((c_ends[None, :] <= rows[:, None]).astype(jnp.int32), axis=1), n_experts - 1)
    of_row = lambda per_expert: jnp.sum(
        jnp.where(row_exp[:, None] == eids[None, :], per_expert[None, :], 0), axis=1)
    in_exp = rows - of_row(c_starts)
    src = jnp.clip(of_row(starts) + in_exp, 0, n_assign - 1)
    gather_tok = jnp.where(in_exp < of_row(counts), order[src] // TOP_K, 0).astype(jnp.int32)
    return gates, dest.reshape(T, TOP_K), gather_tok, block_exp, n_used, x_base, n_blocks


def kernel(x, ln_in_w, ln_in_b, w_in, rw_mu, rw_w0, rw_w2, rw_a0, rw_a2, rw_g2, rw_k_k, rw_k_a, rw_r_k,
           rw_gn_w, rw_gn_b, fx_b_f, fx_q_norm, fx_k_norm, w_o, ln1_w, ln1_b, router_w, router_b,
           exp_w1, exp_b1, exp_w2, exp_b2, ln2_w, ln2_b):
    B, S, D = x.shape
    T = B * S
    depth = w_in.shape[0]
    alpha = (2 * depth) ** 0.25
    rw_w = rw_w0.shape[1]
    fx_heads = fx_b_f.shape[1]
    fx_w = fx_heads * HEAD_DIM
    d_lora, a_lora, g_lora = rw_w2.shape[1], rw_a2.shape[1], rw_g2.shape[1]
    n_lora = d_lora + a_lora + g_lora
    lora_pad = -(-n_lora // 128) * 128
    rw_cols = 3 * rw_w + n_lora
    n_rw = 3 * rw_w + lora_pad
    n_experts = router_w.shape[2]
    ne_pad = -(-n_experts // 128) * 128
    row = lambda a: a.reshape(1, -1)

    seg_rw = _block_diag_ones(rw_w)
    segm_rw = _block_diag_ones(rw_w, 1.0 / HEAD_DIM)
    segm_fx = _block_diag_ones(fx_w, 1.0 / HEAD_DIM)
    tidx = jnp.arange(ROW_TILE)
    tril = (tidx[:, None] >= tidx[None, :]).astype(BF16)
    cidx = jnp.arange(WKV_CHUNK)
    tri_c = (cidx[:, None] >= cidx[None, :]).astype(F32)

    assert depth == 1, "single-layer block"
    l = 0
    x2 = x.reshape(T, D)
    w_l = w_in[l]
    wfz = w_l[:, rw_cols + 4 * fx_w:]
    wfz_hi = wfz.astype(BF16)
    wfz_lo = (wfz - wfz_hi.astype(F32)).astype(BF16)
    w_main = jnp.concatenate(
        [_pad_to(w_l[:, :rw_cols], n_rw, 1).astype(BF16), w_l[:, rw_cols:rw_cols + 4 * fx_w].astype(BF16),
         _pad_to(jnp.concatenate([wfz_hi, wfz_lo], axis=1), PAIR, 1)], axis=1)
    mu = _pad_to(row(rw_mu[l]), n_rw, 1)
    w2p = _pad_to(rw_w2[l], lora_pad, 0).astype(BF16)
    a2p = _pad_to(jnp.pad(rw_a2[l], ((d_lora, 0), (0, 0))), lora_pad, 0).astype(BF16)
    g2p = _pad_to(jnp.pad(rw_g2[l], ((d_lora + a_lora, 0), (0, 0))), lora_pad, 0).astype(BF16)
    rw_params = [mu, row(rw_w0[l]), w2p, row(rw_a0[l]), a2p, g2p,
                 row(rw_k_k[l]), row(rw_k_a[l]), row(rw_r_k[l]), seg_rw]
    qw = row(jnp.tile(fx_q_norm[l], fx_heads))
    kw = row(jnp.tile(fx_k_norm[l], fx_heads))
    fx_params = [_pad_to(row(fx_b_f[l]), PAIR, 1), qw, kw, segm_fx, tril]
    h0, r, lw, k2, v, kk, alr, g, bonus, qa, ka, va, og = _front(
        x2, B, S, row(ln_in_w), row(ln_in_b), w_main, _pad_to(wfz_hi, PAIR, 1), rw_params, fx_params,
        rw_w, fx_w, fx_heads)

    y_rw = _wkv(r, lw, k2, v, kk, alr, g, bonus, row(rw_gn_w[l]), row(rw_gn_b[l]), tri_c, segm_rw, B, S)
    y_fx = _fox_attn(qa, ka, va, og, B, S, fx_w)

    wo = w_o[l].astype(BF16)
    rw_pad = _pad_to(router_w[l], ne_pad, 1)
    rb_pad = _pad_to(row(router_b[l]), ne_pad, 1)
    h1, logits = _out_ln(y_rw, y_fx, h0, wo[:rw_w], wo[rw_w:], row(ln1_w[l]), row(ln1_b[l]),
                         rw_pad, rb_pad, alpha)

    gates, pos, gather_tok, block_exp, n_used, x_base, n_blocks = _route(logits, n_experts, MOE_TILE)
    xs = h1[gather_tok]
    b1 = exp_b1[l]
    b1g = b1[:, None, 0::2]
    b1l = b1[:, None, 1::2]
    eo_rows = _moe_ffn(block_exp, n_used, x_base, n_blocks, xs, exp_w1[l], b1g, b1l, exp_w2[l],
                       exp_b2[l][:, None, :])
    eo = eo_rows[pos.T]
    h = _combine_ln(h1, eo, gates, row(ln2_w[l]), row(ln2_b[l]), alpha)
    return h.reshape(B, S, D)
```

```python
import functools

import jax
import jax.numpy as jnp
from jax import lax
from jax.experimental import pallas as pl
from jax.experimental.pallas import tpu as pltpu

F32 = jnp.float32
BF16 = jnp.bfloat16
HIGHEST = lax.Precision.HIGHEST

HEAD_DIM = 64
PAIR = 2 * HEAD_DIM
WKV_CHUNK = 64
RW_GN_EPS = 64e-5
QK_EPS = 1e-6
LN_EPS = 1e-5
TOP_K = 4
SWIGLU_ALPHA = 1.702
SWIGLU_LIMIT = 7.0
NEG_BIG = -1e30
LOG2E = 1.4426950408889634

ROW_TILE = 256
LN_ROW_TILE = 512
ATTN_TQ = 512
ATTN_TK = 512
ATTN_HEADS = 8
MOE_TILE = 512
ROW_ALIGN = 8
WKV_BATCH = 4
VMEM_LIMIT = 48 * 1024 * 1024
FRONT_VMEM_LIMIT = 56 * 1024 * 1024


def _cparams(sem):
    return pltpu.CompilerParams(dimension_semantics=sem, vmem_limit_bytes=VMEM_LIMIT)


def _dot(a, b):
    return jnp.dot(a.astype(BF16), b.astype(BF16), preferred_element_type=F32)


def _dot_t(a, b):
    return lax.dot_general(a.astype(BF16), b.astype(BF16), (((1,), (1,)), ((), ())),
                           preferred_element_type=F32)


def _segsum(x, seg):
    return jnp.dot(x.astype(BF16), seg, preferred_element_type=F32)


def _sigmoid(x):
    return 1.0 / (1.0 + jnp.exp(-x))


def _softplus(x):
    return jnp.maximum(x, 0.0) + jnp.log(1.0 + jnp.exp(-jnp.abs(x)))


def _layer_norm(x, w, b):
    mu = jnp.mean(x, axis=-1, keepdims=True)
    xc = x - mu
    var = jnp.mean(xc * xc, axis=-1, keepdims=True)
    return xc * lax.rsqrt(var + LN_EPS) * w + b


def _pack_bf16_pairs(x):
    n = x.shape[1] // 2
    bits = pltpu.bitcast(x.astype(BF16).astype(F32), jnp.uint32)
    return lax.shift_right_logical(bits[:, :n], jnp.uint32(16)) | bits[:, n:]


def _unpack_bf16_pairs(u):
    lo = pltpu.bitcast(lax.shift_left(u, jnp.uint32(16)), F32)
    hi = pltpu.bitcast(u & jnp.uint32(0xFFFF0000), F32)
    return jnp.concatenate([lo, hi], axis=1)


def _stack_heads(x):
    lane = lax.broadcasted_iota(jnp.int32, x.shape, 1)
    return jnp.concatenate([jnp.where(lane < HEAD_DIM, x, 0.0), jnp.where(lane >= HEAD_DIM, x, 0.0)], axis=0)


def _wkv_kernel(r_ref, lw_ref, k_ref, v_ref, kk_ref, alr_ref, g_ref, bonus_ref, gnw_ref, gnb_ref, tri_ref,
                segm_ref, y_ref, state_ref):
    C = WKV_CHUNK
    nb, _, width = lw_ref.shape
    npair = width // PAIR

    @pl.when(pl.program_id(1) == 0)
    def _():
        state_ref[...] = jnp.zeros_like(state_ref)

    ri = lax.broadcasted_iota(jnp.int32, (2 * C, 2 * C), 0)
    ci = lax.broadcasted_iota(jnp.int32, (2 * C, 2 * C), 1)
    same = (ri // C) == (ci // C)
    strict = same & ((ci % C) < (ri % C))
    incl = same & ((ci % C) <= (ri % C))
    eye = (ri == ci).astype(F32)

    lhs, rhs, a2s, r2s, v2s, bhts, bkts, ptots = [], [], [], [], [], [], [], []
    for bi in range(nb):
        lw = lw_ref[bi]
        cum = jnp.dot(tri_ref[...], lw, precision=HIGHEST, preferred_element_type=F32)
        total = cum[C - 1:C, :]
        p_inv = jnp.exp(-cum)
        p_rem = jnp.exp(total - cum)
        p_tot = jnp.exp(total)
        kk = kk_ref[bi]
        k2 = k_ref[bi]
        b = kk * alr_ref[bi]
        a_t = -kk * jnp.exp(cum - lw)
        r_t = r_ref[bi] * jnp.exp(cum)
        b_t = b * p_inv
        k_t = k2 * p_inv
        b_h = b * p_rem
        k_h = k2 * p_rem
        v = v_ref[bi]
        for p in range(npair):
            sl = slice(p * PAIR, (p + 1) * PAIR)
            a2, r2, b2, kt2 = (_stack_heads(t[:, sl]) for t in (a_t, r_t, b_t, k_t))
            bh2, kh2, v2 = (_stack_heads(t[:, sl]) for t in (b_h, k_h, v))
            lhs.append(jnp.concatenate([a2, r2], axis=0))
            rhs.append(jnp.concatenate([b2, kt2], axis=0))
            a2s.append(a2)
            r2s.append(r2)
            v2s.append(v2)
            bhts.append(bh2.T)
            bkts.append(jnp.concatenate([bh2.T, kh2.T], axis=1))
            ptots.append(p_tot[:, sl])

    chains = range(nb * npair)
    m = [_dot_t(lhs[c], rhs[c]) for c in chains]
    n_ab = [jnp.where(strict, m[c][:2 * C, :2 * C], 0.0) for c in chains]
    m_ak = [jnp.where(strict, m[c][:2 * C, 2 * C:], 0.0) for c in chains]
    m_rb = [jnp.where(incl, m[c][2 * C:, :2 * C], 0.0) for c in chains]
    m_rk = [jnp.where(incl, m[c][2 * C:, 2 * C:], 0.0) for c in chains]
    mv = [_dot(m_ak[c], v2s[c]) for c in chains]
    mrkv = [_dot(m_rk[c], v2s[c]) for c in chains]
    inv = [eye + n_ab[c] for c in chains]
    pw = n_ab
    for _ in range(C.bit_length() - 2):
        pw = [_dot(pw[c], pw[c]) for c in chains]
        inv = [inv[c] + _dot(inv[c], pw[c]) for c in chains]
    wu = [_dot(inv[c], jnp.concatenate([a2s[c], mv[c]], axis=1)) for c in chains]
    qy = [_dot(m_rb[c], wu[c]) + jnp.concatenate([r2s[c], mrkv[c]], axis=1) for c in chains]
    g_t = [_dot(bhts[c], wu[c][:, :PAIR]) + eye * ptots[c] for c in chains]
    h_t = [_dot(bkts[c], jnp.concatenate([wu[c][:, PAIR:], v2s[c]], axis=0)) for c in chains]
    s0 = [state_ref[c] for c in chains]
    y2 = [_dot(qy[c][:, :PAIR], s0[c]) + qy[c][:, PAIR:] for c in chains]
    for c in chains:
        state_ref[c] = _dot(g_t[c], s0[c]) + h_t[c]

    segm = segm_ref[...]
    for bi in range(nb):
        y = jnp.concatenate([y2[bi * npair + p][:C] + y2[bi * npair + p][C:] for p in range(npair)], axis=1)
        mean = _segsum(y, segm)
        yc = y - mean
        var = _segsum(yc * yc, segm)
        yn = yc * lax.rsqrt(var + RW_GN_EPS) * gnw_ref[...] + gnb_ref[...]
        y_ref[bi] = ((yn + bonus_ref[bi]) * g_ref[bi]).astype(y_ref.dtype)


def _wkv(r, lw, k2, v, kk, alr, g, bonus, gnw, gnb, tri, segm, B, S):
    T, width = r.shape
    C = WKV_CHUNK
    nb = WKV_BATCH
    nc = S // C
    fixed = lambda b, c: (0, 0)
    blk = pl.BlockSpec((nb, C, width), lambda b, c: (b, c, 0))
    vec = pl.BlockSpec((1, width), fixed)
    ins = [t.reshape(B, S, width) for t in (r, lw, k2, v, kk, alr, g, bonus)]
    y = pl.pallas_call(
        _wkv_kernel,
        grid=(B // nb, nc),
        in_specs=[blk] * 8 + [vec, vec, pl.BlockSpec((C, C), fixed), pl.BlockSpec((width, width), fixed)],
        out_specs=blk,
        out_shape=jax.ShapeDtypeStruct((B, S, width), BF16),
        scratch_shapes=[pltpu.VMEM((nb * width // PAIR, PAIR, PAIR), F32)],
        compiler_params=_cparams(("parallel", "arbitrary")),
        name="wkv",
    )(*ins, gnw, gnb, tri, segm)
    return y.reshape(T, width)


def _split3(x):
    hi = x.astype(BF16).astype(F32)
    mid = (x - hi).astype(BF16).astype(F32)
    return hi, mid, x - hi - mid


def _pieces(x, n):
    hi, mid, lo = _split3(x)
    return hi + pltpu.roll(mid, n, axis=1) + pltpu.roll(lo, 2 * n, axis=1)


def _spread_heads(x):
    lane = lax.broadcasted_iota(jnp.int32, (x.shape[0], PAIR), 1)
    groups = []
    for p in range(x.shape[1] // PAIR):
        blk = x[:, p * PAIR:(p + 1) * PAIR]
        groups.append(jnp.where(lane < HEAD_DIM, blk, 0.0))
        groups.append(jnp.where(lane < HEAD_DIM, pltpu.roll(blk, HEAD_DIM, axis=1), 0.0))
    return jnp.concatenate(groups, axis=1)


def _front_kernel(x_ref, lnw_ref, lnb_ref, w_ref, wfz_ref,
                  mu_ref, w0_ref, w2_ref, a0_ref, a2_ref, g2_ref, kkw_ref, kaw_ref, rk_ref, seg_ref,
                  bf_ref, qw_ref, kw_ref, segm_ref, tril_ref, place_ref, maskq_ref, maskk_ref,
                  oneq_ref, onek_ref, onev_ref,
                  h_ref, r_ref, lw_ref, k_ref, v_ref, kk_ref, alr_ref, g_ref, bonus_ref,
                  qa_ref, ka_ref, va_ref, og_ref, shift_carry, c_carry, *, width, nh):
    @pl.when(pl.program_id(1) == 0)
    def _():
        shift_carry[...] = jnp.zeros_like(shift_carry)
        c_carry[...] = jnp.zeros_like(c_carry)

    h = _layer_norm(x_ref[...], lnw_ref[...], lnb_ref[...])
    h_ref[...] = h
    proj = jnp.dot(h.astype(BF16), w_ref[...], preferred_element_type=F32)
    tm = proj.shape[0]
    n_rw = mu_ref.shape[1]

    p = proj[:, :n_rw]
    prev = pltpu.roll(p, 1, axis=0)
    first_row = lax.broadcasted_iota(jnp.int32, p.shape, 0) == 0
    prev = jnp.where(first_row, shift_carry[...], prev)
    shift_carry[...] = p[tm - 1:tm, :]
    ps = p + mu_ref[...] * (prev - p)
    r = ps[:, 0:width]
    k = ps[:, width:2 * width]
    v = ps[:, 2 * width:3 * width]
    lora = ps[:, 3 * width:]
    seg = seg_ref[...]
    wl = w0_ref[...] + _dot(jnp.tanh(lora), w2_ref[...])
    w_raw = -_softplus(-wl) - 0.5
    lw_ref[...] = -jnp.exp(w_raw)
    alr = _sigmoid(a0_ref[...] + _dot(lora, a2_ref[...]))
    g_ref[...] = _dot(_sigmoid(lora), g2_ref[...])
    kkp = k * kkw_ref[...]
    nrm = jnp.sqrt(_segsum(kkp * kkp, seg))
    kk_ref[...] = kkp / jnp.maximum(nrm, 1e-12)
    k2 = k * (1.0 + (alr - 1.0) * kaw_ref[...])
    bonus_ref[...] = _segsum(r * k2 * rk_ref[...], seg) * v
    r_ref[...] = r
    k_ref[...] = k2
    v_ref[...] = v
    alr_ref[...] = alr

    fw = qw_ref.shape[1]
    q = proj[:, n_rw:n_rw + fw]
    kx = proj[:, n_rw + fw:n_rw + 2 * fw]
    vx = proj[:, n_rw + 2 * fw:n_rw + 3 * fw]
    og_ref[...] = proj[:, n_rw + 3 * fw:n_rw + 4 * fw]
    segm = segm_ref[...]
    qn = q * lax.rsqrt(_segsum(q * q, segm) + QK_EPS) * (qw_ref[...] * (HEAD_DIM ** -0.5 * LOG2E))
    kn = kx * lax.rsqrt(_segsum(kx * kx, segm) + QK_EPS) * kw_ref[...]
    fblk = proj[:, n_rw + 4 * fw:]
    h_lo = (h - h.astype(BF16).astype(F32)).astype(BF16)
    fz = fblk + pltpu.roll(fblk, PAIR - nh, axis=1) + jnp.dot(h_lo, wfz_ref[...], preferred_element_type=F32)
    in_heads = lax.broadcasted_iota(jnp.int32, fz.shape, 1) < nh
    lf = jnp.where(in_heads, -_softplus(-(fz + bf_ref[...])), 0.0)
    c3 = jnp.dot(tril_ref[...], _pieces(lf, nh).astype(BF16), preferred_element_type=F32)
    c = c3 + pltpu.roll(c3, PAIR - nh, axis=1) + pltpu.roll(c3, PAIR - 2 * nh, axis=1)
    c = jnp.where(in_heads, c, 0.0) + c_carry[...]
    c_carry[...] = c[tm - 1:tm, :]
    placed = jnp.dot(_pieces(c * LOG2E, nh).astype(BF16), place_ref[...], preferred_element_type=F32)
    qa_ref[...] = (_spread_heads(qn) + placed * maskq_ref[...] + oneq_ref[...]).astype(qa_ref.dtype)
    ka_ref[...] = (_spread_heads(kn) - placed * maskk_ref[...] + onek_ref[...]).astype(ka_ref.dtype)
    va_ref[...] = (_spread_heads(vx) + onev_ref[...]).astype(va_ref.dtype)


def _front(x2, B, S, lnw, lnb, w_main, wfz_hi, rw_params, fx_params, rw_w, fx_w, nh):
    T, D = x2.shape
    wide = nh * PAIR
    tm = ROW_TILE
    ns = S // tm
    row = lambda b, s: (b * ns + s, 0)
    fixed = lambda b, s: (0, 0)
    const = lambda a: pl.BlockSpec(a.shape, fixed)

    lane = jnp.arange(wide) % PAIR
    head = jnp.arange(wide) // PAIR
    src = jnp.arange(PAIR)
    piece = (lane - HEAD_DIM) % 3
    place = ((src[:, None] == (piece * nh + head)[None, :]) & (src[:, None] < 3 * nh)
             & (lane >= HEAD_DIM)[None, :] & (lane < HEAD_DIM + 6)[None, :]).astype(BF16)
    mask_k = ((lane >= HEAD_DIM) & (lane < HEAD_DIM + 3)).astype(F32).reshape(1, wide)
    mask_q = ((lane >= HEAD_DIM + 3) & (lane < HEAD_DIM + 6)).astype(F32).reshape(1, wide)
    one_v = (lane >= HEAD_DIM).astype(F32).reshape(1, wide)
    consts = [lnw, lnb, w_main, wfz_hi, *rw_params, *fx_params, place, mask_q, mask_k, mask_k, mask_q, one_v]

    f32_out = lambda n: (pl.BlockSpec((tm, n), row), jax.ShapeDtypeStruct((T, n), F32))
    bf_out = lambda n: (pl.BlockSpec((tm, n), row), jax.ShapeDtypeStruct((T, n), BF16))
    outs = [f32_out(D)] + [f32_out(rw_w)] * 8 + [bf_out(wide)] * 3 + [f32_out(fx_w)]
    return pl.pallas_call(
        functools.partial(_front_kernel, width=rw_w, nh=nh),
        grid=(B, ns),
        in_specs=[pl.BlockSpec((tm, D), row)] + [const(a) for a in consts],
        out_specs=[o[0] for o in outs],
        out_shape=[o[1] for o in outs],
        scratch_shapes=[pltpu.VMEM((1, rw_params[0].shape[1]), F32), pltpu.VMEM((1, PAIR), F32)],
        compiler_params=pltpu.CompilerParams(dimension_semantics=("parallel", "arbitrary"),
                                             vmem_limit_bytes=FRONT_VMEM_LIMIT),
        name="front",
    )(x2, *consts)


def _fox_attn_kernel(qi_ref, kj_ref, q_ref, k_ref, v_ref, og_ref, o_ref, m_ref, acc_ref):
    i = qi_ref[pl.program_id(2)]
    j = kj_ref[pl.program_id(2)]
    tq = q_ref.shape[0]
    tk = k_ref.shape[0]
    heads = range(ATTN_HEADS)
    grp = lambda ref, h: ref[:, h * PAIR:(h + 1) * PAIR]

    @pl.when(j == 0)
    def _():
        m_ref[...] = jnp.full_like(m_ref, NEG_BIG)
        acc_ref[...] = jnp.zeros_like(acc_ref)

    def step(masked):
        s = [lax.dot_general(grp(q_ref, h), grp(k_ref, h), (((1,), (1,)), ((), ())), preferred_element_type=F32)
             for h in heads]
        if masked:
            row = lax.broadcasted_iota(jnp.int32, (tq, tk), 0)
            col = lax.broadcasted_iota(jnp.int32, (tq, tk), 1)
            s = [jnp.where(col <= row, s[h], NEG_BIG) for h in heads]
        m_old = [m_ref[h] for h in heads]
        m_new = [jnp.maximum(m_old[h], jnp.max(s[h], axis=1, keepdims=True)) for h in heads]
        alpha = [jnp.exp2(m_old[h] - m_new[h]) for h in heads]
        pr = [jnp.exp2(s[h] - jnp.concatenate([m_new[h]] * (tk // PAIR), axis=1)).astype(BF16) for h in heads]
        pv = [jnp.dot(pr[h], grp(v_ref, h), preferred_element_type=F32) for h in heads]
        for h in heads:
            acc_ref[h] = alpha[h] * acc_ref[h] + pv[h]
            m_ref[h] = m_new[h]

    @pl.when(j < i)
    def _():
        step(False)

    @pl.when(j == i)
    def _():
        step(True)
        lane = lax.broadcasted_iota(jnp.int32, (tq, PAIR), 1)
        outs = []
        for p in range(ATTN_HEADS // 2):
            a0 = acc_ref[2 * p]
            a1 = acc_ref[2 * p + 1]
            o0 = a0 * pltpu.roll(1.0 / a0, HEAD_DIM, axis=1)
            o1 = pltpu.roll(a1, HEAD_DIM, axis=1) * (1.0 / a1)
            outs.append(jnp.where(lane < HEAD_DIM, o0, o1))
        o = jnp.concatenate(outs, axis=1)
        o_ref[...] = (o * _sigmoid(og_ref[...])).astype(o_ref.dtype)


def _fox_attn(qa, ka, va, og, B, S, width):
    T = qa.shape[0]
    tq, tk = ATTN_TQ, ATTN_TK
    assert tq == tk
    nq = S // tq
    nh = ATTN_HEADS
    ow = nh * HEAD_DIM
    ngroup = width // ow
    pairs = [(i, j) for i in range(nq) for j in range(i + 1)]
    qi = jnp.array([ij[0] for ij in pairs], jnp.int32)
    kj = jnp.array([ij[1] for ij in pairs], jnp.int32)
    kv = pl.BlockSpec((tk, nh * PAIR), lambda b, p, t, qi, kj: (b * nq + kj[t], p))
    grid_spec = pltpu.PrefetchScalarGridSpec(
        num_scalar_prefetch=2,
        grid=(B, ngroup, len(pairs)),
        in_specs=[
            pl.BlockSpec((tq, nh * PAIR), lambda b, p, t, qi, kj: (b * nq + qi[t], p)), kv, kv,
            pl.BlockSpec((tq, ow), lambda b, p, t, qi, kj: (b * nq + qi[t], p)),
        ],
        out_specs=pl.BlockSpec((tq, ow), lambda b, p, t, qi, kj: (b * nq + qi[t], p)),
        scratch_shapes=[pltpu.VMEM((nh, tq, PAIR), F32), pltpu.VMEM((nh, tq, PAIR), F32)],
    )
    return pl.pallas_call(
        _fox_attn_kernel,
        grid_spec=grid_spec,
        out_shape=jax.ShapeDtypeStruct((T, width), BF16),
        compiler_params=_cparams(("parallel", "parallel", "arbitrary")),
        name="fox_attn",
    )(qi, kj, qa, ka, va, og)


def _out_ln_kernel(yrw_ref, yfx_ref, h_ref, wo1_ref, wo2_ref, lnw_ref, lnb_ref, rw_ref, rb_ref,
                   h1_ref, logit_ref, *, alpha):
    mix = (jnp.dot(yrw_ref[...], wo1_ref[...], preferred_element_type=F32)
           + jnp.dot(yfx_ref[...], wo2_ref[...], preferred_element_type=F32))
    h1 = _layer_norm(alpha * h_ref[...] + mix, lnw_ref[...], lnb_ref[...])
    h1_ref[...] = h1
    ne = rb_ref.shape[1]
    h_hi = h1.astype(BF16)
    h_lo = (h1 - h_hi.astype(F32)).astype(BF16)
    rw = rw_ref[...]
    first = jnp.dot(h_hi, rw, preferred_element_type=F32)
    second = jnp.dot(h_lo, rw[:, :ne], preferred_element_type=F32)
    logit_ref[...] = first[:, :ne] + first[:, ne:] + second + rb_ref[...]


def _out_ln(y_rw, y_fx, h0, wo1, wo2, lnw, lnb, rw, rb, alpha):
    T, D = h0.shape
    width = y_rw.shape[1]
    ne = rb.shape[1]
    tm = LN_ROW_TILE
    row = lambda i: (i, 0)
    fixed = lambda i: (0, 0)
    rw_hi = rw.astype(BF16)
    rw_lo = (rw - rw_hi.astype(F32)).astype(BF16)
    rw = jnp.concatenate([rw_hi, rw_lo], axis=1)
    return pl.pallas_call(
        functools.partial(_out_ln_kernel, alpha=alpha),
        grid=(T // tm,),
        in_specs=[
            pl.BlockSpec((tm, width), row), pl.BlockSpec((tm, width), row), pl.BlockSpec((tm, D), row),
            pl.BlockSpec((width, D), fixed), pl.BlockSpec((width, D), fixed),
            pl.BlockSpec((1, D), fixed), pl.BlockSpec((1, D), fixed),
            pl.BlockSpec((D, 2 * ne), fixed), pl.BlockSpec((1, ne), fixed),
        ],
        out_specs=[pl.BlockSpec((tm, D), row), pl.BlockSpec((tm, ne), row)],
        out_shape=[jax.ShapeDtypeStruct((T, D), F32), jax.ShapeDtypeStruct((T, ne), F32)],
        compiler_params=_cparams(("parallel",)),
        name="out_ln",
    )(y_rw, y_fx, h0, wo1, wo2, lnw, lnb, rw, rb)


DEINT_COLS = 256
MOE_VMEM_LIMIT = 56 * 1024 * 1024


def _moe_kernel(bexp_ref, nused_ref, first_ref, nexte_ref, slot_ref, xbase_ref,
                x_hbm, w1_hbm, w2_hbm, b1g_ref, b1l_ref, b2_ref, perm_ref, o_ref,
                w1_stage, w2_stage, w1g_bf, w1l_bf, w2_bf, x_buf, sem, sem_x):
    i = pl.program_id(0)
    live = i < nused_ref[0]
    tm = x_buf.shape[1]

    def x_copy(blk, s):
        row0 = pl.multiple_of(xbase_ref[blk], ROW_ALIGN)
        return pltpu.make_async_copy(x_hbm.at[pl.ds(row0, tm), :], x_buf.at[s], sem_x.at[s])

    @pl.when(live & (i == 0))
    def _():
        x_copy(0, 0).start()

    def weight_copies(e, s):
        return (pltpu.make_async_copy(w1_hbm.at[e], w1_stage.at[s], sem.at[0, s]),
                pltpu.make_async_copy(w2_hbm.at[e], w2_stage.at[s], sem.at[1, s]))

    @pl.when(live & (i == 0))
    def _():
        for cp in weight_copies(bexp_ref[0], 0):
            cp.start()

    @pl.when(live & (first_ref[i] == 1))
    def _():
        s = slot_ref[i]
        for cp in weight_copies(bexp_ref[i], s):
            cp.wait()

        @pl.when(nexte_ref[i] >= 0)
        def _():
            for cp in weight_copies(nexte_ref[i], 1 - s):
                cp.start()

        half = DEINT_COLS // 2
        for c in range(w1_stage.shape[2] // DEINT_COLS):
            blk = w1_stage[s, :, c * DEINT_COLS:(c + 1) * DEINT_COLS].astype(BF16)
            out = jnp.dot(blk, perm_ref[...], preferred_element_type=F32).astype(BF16)
            w1g_bf[:, c * half:(c + 1) * half] = out[:, :half]
            w1l_bf[:, c * half:(c + 1) * half] = out[:, half:]
        w2_bf[...] = w2_stage[s].astype(BF16)

    @pl.when(live)
    def _():
        xs = i % 2
        x_copy(i, xs).wait()

        @pl.when(i + 1 < nused_ref[0])
        def _():
            x_copy(i + 1, 1 - xs).start()

        x = x_buf[xs].astype(BF16)
        x_glu = jnp.minimum(jnp.dot(x, w1g_bf[...], preferred_element_type=F32) + b1g_ref[0], SWIGLU_LIMIT)
        x_lin = jnp.clip(jnp.dot(x, w1l_bf[...], preferred_element_type=F32) + b1l_ref[0],
                         -SWIGLU_LIMIT, SWIGLU_LIMIT)
        act = x_glu * _sigmoid(SWIGLU_ALPHA * x_glu) * (x_lin + 1.0)
        o = jnp.dot(act.astype(BF16), w2_bf[...], preferred_element_type=F32) + b2_ref[0]
        o_ref[...] = _pack_bf16_pairs(o)

    @pl.when(jnp.logical_not(live))
    def _():
        o_ref[...] = jnp.zeros_like(o_ref)


def _moe_ffn(block_exp, n_used, x_base, n_blocks, xs, w1, b1g, b1l, w2, b2):
    E, D, F2 = w1.shape
    F = F2 // 2
    tm = MOE_TILE
    n_rows = n_blocks * tm

    idx = jnp.arange(n_blocks, dtype=jnp.int32)
    first = jnp.concatenate([jnp.ones((1,), jnp.bool_), block_exp[1:] != block_exp[:-1]])
    slot = ((jnp.cumsum(first.astype(jnp.int32)) - 1) % 2).astype(jnp.int32)
    cand = jnp.where(first & (idx < n_used[0]), idx, n_blocks)
    next_first = jnp.min(jnp.where(idx[None, :] > idx[:, None], cand[None, :], n_blocks), axis=1)
    next_e = jnp.where(next_first < n_blocks, block_exp[jnp.minimum(next_first, n_blocks - 1)], -1).astype(jnp.int32)

    half = DEINT_COLS // 2
    src = jnp.arange(DEINT_COLS)
    dst = jnp.where(src % 2 == 0, src // 2, half + src // 2)
    perm = (dst[:, None] == jnp.arange(DEINT_COLS)[None, :]).astype(BF16)

    live = lambda i, be, nu: jnp.minimum(i, nu[0] - 1)
    bspec = lambda n: pl.BlockSpec((1, 1, n), lambda i, be, nu, fi, ne, sl, xb: (be[live(i, be, nu)], 0, 0))
    grid_spec = pltpu.PrefetchScalarGridSpec(
        num_scalar_prefetch=6,
        grid=(n_blocks,),
        in_specs=[
            pl.BlockSpec(memory_space=pl.ANY),
            pl.BlockSpec(memory_space=pl.ANY),
            pl.BlockSpec(memory_space=pl.ANY),
            bspec(F), bspec(F), bspec(D),
            pl.BlockSpec((DEINT_COLS, DEINT_COLS), lambda i, be, nu, fi, ne, sl, xb: (0, 0)),
        ],
        out_specs=pl.BlockSpec((tm, D // 2), lambda i, be, nu, fi, ne, sl, xb: (i, 0)),
        scratch_shapes=[
            pltpu.VMEM((2, D, F2), F32), pltpu.VMEM((2, F, D), F32),
            pltpu.VMEM((D, F), BF16), pltpu.VMEM((D, F), BF16), pltpu.VMEM((F, D), BF16),
            pltpu.VMEM((2, tm, D), F32),
            pltpu.SemaphoreType.DMA((2, 2)), pltpu.SemaphoreType.DMA((2,)),
        ],
    )
    return pl.pallas_call(
        _moe_kernel,
        grid_spec=grid_spec,
        out_shape=jax.ShapeDtypeStruct((n_rows, D // 2), jnp.uint32),
        compiler_params=pltpu.CompilerParams(dimension_semantics=("arbitrary",), vmem_limit_bytes=MOE_VMEM_LIMIT),
        name="moe_ffn",
    )(block_exp, n_used, first.astype(jnp.int32), next_e, slot, x_base, xs, w1, w2, b1g, b1l, b2, perm)


def _combine_ln_kernel(h_ref, eo_ref, gate_ref, lnw_ref, lnb_ref, o_ref, *, alpha):
    gates = gate_ref[...]
    ffn = _unpack_bf16_pairs(eo_ref[0]) * gates[:, 0:1]
    for j in range(1, TOP_K):
        ffn = ffn + _unpack_bf16_pairs(eo_ref[j]) * gates[:, j:j + 1]
    o_ref[...] = _layer_norm(alpha * h_ref[...] + ffn, lnw_ref[...], lnb_ref[...])


def _combine_ln(h1, eo, gates, lnw, lnb, alpha):
    T, D = h1.shape
    tm = LN_ROW_TILE
    return pl.pallas_call(
        functools.partial(_combine_ln_kernel, alpha=alpha),
        grid=(T // tm,),
        in_specs=[
            pl.BlockSpec((tm, D), lambda i: (i, 0)),
            pl.BlockSpec((TOP_K, tm, D // 2), lambda i: (0, i, 0)),
            pl.BlockSpec((tm, TOP_K), lambda i: (i, 0)),
            pl.BlockSpec((1, D), lambda i: (0, 0)),
            pl.BlockSpec((1, D), lambda i: (0, 0)),
        ],
        out_specs=pl.BlockSpec((tm, D), lambda i: (i, 0)),
        out_shape=jax.ShapeDtypeStruct((T, D), F32),
        compiler_params=_cparams(("parallel",)),
        name="combine_ln",
    )(h1, eo, gates, lnw, lnb)


def _pad_to(x, n, axis):
    pad = [(0, 0)] * x.ndim
    pad[axis] = (0, n - x.shape[axis])
    return jnp.pad(x, pad)


def _block_diag_ones(width, value=1.0):
    idx = jnp.arange(width) // HEAD_DIM
    return jnp.where(idx[:, None] == idx[None, :], value, 0.0).astype(BF16)


def _route(logits, n_experts, tile):
    T = logits.shape[0]
    top_val, top_idx = lax.top_k(logits[:, :n_experts], TOP_K)
    gates = jax.nn.softmax(top_val, axis=-1)
    e_flat = top_idx.reshape(-1).astype(jnp.int32)
    n_assign = T * TOP_K
    n_blocks = n_assign // tile + n_experts
    assert n_experts * n_assign < 2 ** 31
    eids = jnp.arange(n_experts, dtype=jnp.int32)
    aids = jnp.arange(n_assign, dtype=jnp.int32)
    skeys = lax.sort(e_flat * n_assign + aids)
    order = skeys % n_assign
    e_sorted = skeys // n_assign
    counts = jnp.sum((e_flat[:, None] == eids[None, :]).astype(jnp.int32), axis=0)
    starts = jnp.cumsum(counts) - counts
    padded = (counts + tile - 1) // tile * tile
    pends = jnp.cumsum(padded)
    pstarts = pends - padded
    shift = pstarts - starts
    dest_sorted = aids + jnp.sum(jnp.where(e_sorted[:, None] == eids[None, :], shift[None, :], 0), axis=1)
    _, dest = lax.sort((order, dest_sorted), num_keys=1)
    block_start = jnp.arange(n_blocks, dtype=jnp.int32) * tile
    block_exp = jnp.minimum(jnp.sum((pends[None, :] <= block_start[:, None]).astype(jnp.int32), axis=1),
                            n_experts - 1).astype(jnp.int32)
    n_used = (pends[-1] // tile).astype(jnp.int32).reshape(1)
    c_len = (counts + ROW_ALIGN - 1) // ROW_ALIGN * ROW_ALIGN
    c_ends = jnp.cumsum(c_len)
    c_starts = c_ends - c_len
    n_compact = n_assign + ROW_ALIGN * n_experts + tile
    of_block = lambda per_expert: jnp.sum(
        jnp.where(block_exp[:, None] == eids[None, :], per_expert[None, :], 0), axis=1)
    x_base = jnp.clip(of_block(c_starts - pstarts) + block_start, 0, n_compact - tile).astype(jnp.int32)
    rows = jnp.arange(n_compact, dtype=jnp.int32)
    row_exp = jnp.minimum(jnp.sum((c_ends[None, :] <= rows[:, None]).astype(jnp.int32), axis=1), n_experts - 1)
    of_row = lambda per_expert: jnp.sum(
        jnp.where(row_exp[:, None] == eids[None, :], per_expert[None, :], 0), axis=1)
    in_exp = rows - of_row(c_starts)
    src = jnp.clip(of_row(starts) + in_exp, 0, n_assign - 1)
    gather_tok = jnp.where(in_exp < of_row(counts), order[src] // TOP_K, 0).astype(jnp.int32)
    return gates, dest.reshape(T, TOP_K), gather_tok, block_exp, n_used, x_base, n_blocks


def kernel(x, ln_in_w, ln_in_b, w_in, rw_mu, rw_w0, rw_w2, rw_a0, rw_a2, rw_g2, rw_k_k, rw_k_a, rw_r_k,
           rw_gn_w, rw_gn_b, fx_b_f, fx_q_norm, fx_k_norm, w_o, ln1_w, ln1_b, router_w, router_b,
           exp_w1, exp_b1, exp_w2, exp_b2, ln2_w, ln2_b):
    B, S, D = x.shape
    T = B * S
    depth = w_in.shape[0]
    alpha = (2 * depth) ** 0.25
    rw_w = rw_w0.shape[1]
    fx_heads = fx_b_f.shape[1]
    fx_w = fx_heads * HEAD_DIM
    d_lora, a_lora, g_lora = rw_w2.shape[1], rw_a2.shape[1], rw_g2.shape[1]
    n_lora = d_lora + a_lora + g_lora
    lora_pad = -(-n_lora // 128) * 128
    rw_cols = 3 * rw_w + n_lora
    n_rw = 3 * rw_w + lora_pad
    n_experts = router_w.shape[2]
    ne_pad = -(-n_experts // 128) * 128
    row = lambda a: a.reshape(1, -1)

    seg_rw = _block_diag_ones(rw_w)
    segm_rw = _block_diag_ones(rw_w, 1.0 / HEAD_DIM)
    segm_fx = _block_diag_ones(fx_w, 1.0 / HEAD_DIM)
    tidx = jnp.arange(ROW_TILE)
    tril = (tidx[:, None] >= tidx[None, :]).astype(BF16)
    cidx = jnp.arange(WKV_CHUNK)
    tri_c = (cidx[:, None] >= cidx[None, :]).astype(F32)

    assert depth == 1, "single-layer block"
    l = 0
    x2 = x.reshape(T, D)
    w_l = w_in[l]
    wfz = w_l[:, rw_cols + 4 * fx_w:]
    wfz_hi = wfz.astype(BF16)
    wfz_lo = (wfz - wfz_hi.astype(F32)).astype(BF16)
    w_main = jnp.concatenate(
        [_pad_to(w_l[:, :rw_cols], n_rw, 1).astype(BF16), w_l[:, rw_cols:rw_cols + 4 * fx_w].astype(BF16),
         _pad_to(jnp.concatenate([wfz_hi, wfz_lo], axis=1), PAIR, 1)], axis=1)
    mu = _pad_to(row(rw_mu[l]), n_rw, 1)
    w2p = _pad_to(rw_w2[l], lora_pad, 0).astype(BF16)
    a2p = _pad_to(jnp.pad(rw_a2[l], ((d_lora, 0), (0, 0))), lora_pad, 0).astype(BF16)
    g2p = _pad_to(jnp.pad(rw_g2[l], ((d_lora + a_lora, 0), (0, 0))), lora_pad, 0).astype(BF16)
    rw_params = [mu, row(rw_w0[l]), w2p, row(rw_a0[l]), a2p, g2p,
                 row(rw_k_k[l]), row(rw_k_a[l]), row(rw_r_k[l]), seg_rw]
    qw = row(jnp.tile(fx_q_norm[l], fx_heads))
    kw = row(jnp.tile(fx_k_norm[l], fx_heads))
    fx_params = [_pad_to(row(fx_b_f[l]), PAIR, 1), qw, kw, segm_fx, tril]
    h0, r, lw, k2, v, kk, alr, g, bonus, qa, ka, va, og = _front(
        x2, B, S, row(ln_in_w), row(ln_in_b), w_main, _pad_to(wfz_hi, PAIR, 1), rw_params, fx_params,
        rw_w, fx_w, fx_heads)

    y_rw = _wkv(r, lw, k2, v, kk, alr, g, bonus, row(rw_gn_w[l]), row(rw_gn_b[l]), tri_c, segm_rw, B, S)
    y_fx = _fox_attn(qa, ka, va, og, B, S, fx_w)

    wo = w_o[l].astype(BF16)
    rw_pad = _pad_to(router_w[l], ne_pad, 1)
    rb_pad = _pad_to(row(router_b[l]), ne_pad, 1)
    h1, logits = _out_ln(y_rw, y_fx, h0, wo[:rw_w], wo[rw_w:], row(ln1_w[l]), row(ln1_b[l]),
                         rw_pad, rb_pad, alpha)

    gates, pos, gather_tok, block_exp, n_used, x_base, n_blocks = _route(logits, n_experts, MOE_TILE)
    xs = h1[gather_tok]
    b1 = exp_b1[l]
    b1g = b1[:, None, 0::2]
    b1l = b1[:, None, 1::2]
    eo_rows = _moe_ffn(block_exp, n_used, x_base, n_blocks, xs, exp_w1[l], b1g, b1l, exp_w2[l],
                       exp_b2[l][:, None, :])
    eo = eo_rows[pos.T]
    h = _combine_ln(h1, eo, gates, row(ln2_w[l]), row(ln2_b[l]), alpha)
    return h.reshape(B, S, D)
```

```python
import functools

import jax
import jax.numpy as jnp
from jax import lax
from jax.experimental import pallas as pl
from jax.experimental.pallas import tpu as pltpu

F32 = jnp.float32
BF16 = jnp.bfloat16
HIGHEST = lax.Precision.HIGHEST

HEAD_DIM = 64
PAIR = 2 * HEAD_DIM
WKV_CHUNK = 64
RW_GN_EPS = 64e-5
QK_EPS = 1e-6
LN_EPS = 1e-5
TOP_K = 4
SWIGLU_ALPHA = 1.702
SWIGLU_LIMIT = 7.0
NEG_BIG = -1e30
LOG2E = 1.4426950408889634

ROW_TILE = 256
LN_ROW_TILE = 1024
ATTN_TQ = 512
ATTN_TK = 512
ATTN_HEADS = 8
MOE_TILE = 512
ROW_ALIGN = 8
WKV_BATCH = 4
VMEM_LIMIT = 48 * 1024 * 1024
FRONT_VMEM_LIMIT = 56 * 1024 * 1024


def _cparams(sem):
    return pltpu.CompilerParams(dimension_semantics=sem, vmem_limit_bytes=VMEM_LIMIT)


def _dot(a, b):
    return jnp.dot(a.astype(BF16), b.astype(BF16), preferred_element_type=F32)


def _dot_t(a, b):
    return lax.dot_general(a.astype(BF16), b.astype(BF16), (((1,), (1,)), ((), ())),
                           preferred_element_type=F32)


def _segsum(x, seg):
    return jnp.dot(x.astype(BF16), seg, preferred_element_type=F32)


def _sigmoid(x):
    return 1.0 / (1.0 + jnp.exp(-x))


def _softplus(x):
    return jnp.maximum(x, 0.0) + jnp.log(1.0 + jnp.exp(-jnp.abs(x)))


def _layer_norm(x, w, b):
    mu = jnp.mean(x, axis=-1, keepdims=True)
    xc = x - mu
    var = jnp.mean(xc * xc, axis=-1, keepdims=True)
    return xc * lax.rsqrt(var + LN_EPS) * w + b


def _pack_bf16_pairs(x):
    n = x.shape[1] // 2
    bits = pltpu.bitcast(x.astype(BF16).astype(F32), jnp.uint32)
    return lax.shift_right_logical(bits[:, :n], jnp.uint32(16)) | bits[:, n:]


def _unpack_bf16_pairs(u):
    lo = pltpu.bitcast(lax.shift_left(u, jnp.uint32(16)), F32)
    hi = pltpu.bitcast(u & jnp.uint32(0xFFFF0000), F32)
    return jnp.concatenate([lo, hi], axis=1)


def _stack_heads(x):
    lane = lax.broadcasted_iota(jnp.int32, x.shape, 1)
    return jnp.concatenate([jnp.where(lane < HEAD_DIM, x, 0.0), jnp.where(lane >= HEAD_DIM, x, 0.0)], axis=0)


def _wkv_kernel(r_ref, lw_ref, k_ref, v_ref, kk_ref, alr_ref, g_ref, bonus_ref, gnw_ref, gnb_ref, tri_ref,
                segm_ref, y_ref, state_ref):
    C = WKV_CHUNK
    nb, _, width = lw_ref.shape
    npair = width // PAIR

    @pl.when(pl.program_id(1) == 0)
    def _():
        state_ref[...] = jnp.zeros_like(state_ref)

    ri = lax.broadcasted_iota(jnp.int32, (2 * C, 2 * C), 0)
    ci = lax.broadcasted_iota(jnp.int32, (2 * C, 2 * C), 1)
    same = (ri // C) == (ci // C)
    strict = same & ((ci % C) < (ri % C))
    incl = same & ((ci % C) <= (ri % C))
    eye = (ri == ci).astype(F32)

    lhs, rhs, a2s, r2s, v2s, bhts, bkts, ptots = [], [], [], [], [], [], [], []
    for bi in range(nb):
        lw = lw_ref[bi]
        cum = jnp.dot(tri_ref[...], lw, precision=HIGHEST, preferred_element_type=F32)
        total = cum[C - 1:C, :]
        p_inv = jnp.exp(-cum)
        p_rem = jnp.exp(total - cum)
        p_tot = jnp.exp(total)
        kk = kk_ref[bi]
        k2 = k_ref[bi]
        b = kk * alr_ref[bi]
        a_t = -kk * jnp.exp(cum - lw)
        r_t = r_ref[bi] * jnp.exp(cum)
        b_t = b * p_inv
        k_t = k2 * p_inv
        b_h = b * p_rem
        k_h = k2 * p_rem
        v = v_ref[bi]
        for p in range(npair):
            sl = slice(p * PAIR, (p + 1) * PAIR)
            a2, r2, b2, kt2 = (_stack_heads(t[:, sl]) for t in (a_t, r_t, b_t, k_t))
            bh2, kh2, v2 = (_stack_heads(t[:, sl]) for t in (b_h, k_h, v))
            lhs.append(jnp.concatenate([a2, r2], axis=0))
            rhs.append(jnp.concatenate([b2, kt2], axis=0))
            a2s.append(a2)
            r2s.append(r2)
            v2s.append(v2)
            bhts.append(bh2.T)
            bkts.append(jnp.concatenate([bh2.T, kh2.T], axis=1))
            ptots.append(p_tot[:, sl])

    chains = range(nb * npair)
    m = [_dot_t(lhs[c], rhs[c]) for c in chains]
    n_ab = [jnp.where(strict, m[c][:2 * C, :2 * C], 0.0) for c in chains]
    m_ak = [jnp.where(strict, m[c][:2 * C, 2 * C:], 0.0) for c in chains]
    m_rb = [jnp.where(incl, m[c][2 * C:, :2 * C], 0.0) for c in chains]
    m_rk = [jnp.where(incl, m[c][2 * C:, 2 * C:], 0.0) for c in chains]
    mv = [_dot(m_ak[c], v2s[c]) for c in chains]
    mrkv = [_dot(m_rk[c], v2s[c]) for c in chains]
    inv = [eye + n_ab[c] for c in chains]
    pw = n_ab
    for _ in range(C.bit_length() - 2):
        pw = [_dot(pw[c], pw[c]) for c in chains]
        inv = [inv[c] + _dot(inv[c], pw[c]) for c in chains]
    wu = [_dot(inv[c], jnp.concatenate([a2s[c], mv[c]], axis=1)) for c in chains]
    qy = [_dot(m_rb[c], wu[c]) + jnp.concatenate([r2s[c], mrkv[c]], axis=1) for c in chains]
    g_t = [_dot(bhts[c], wu[c][:, :PAIR]) + eye * ptots[c] for c in chains]
    h_t = [_dot(bkts[c], jnp.concatenate([wu[c][:, PAIR:], v2s[c]], axis=0)) for c in chains]
    s0 = [state_ref[c] for c in chains]
    y2 = [_dot(qy[c][:, :PAIR], s0[c]) + qy[c][:, PAIR:] for c in chains]
    for c in chains:
        state_ref[c] = _dot(g_t[c], s0[c]) + h_t[c]

    segm = segm_ref[...]
    for bi in range(nb):
        y = jnp.concatenate([y2[bi * npair + p][:C] + y2[bi * npair + p][C:] for p in range(npair)], axis=1)
        mean = _segsum(y, segm)
        yc = y - mean
        var = _segsum(yc * yc, segm)
        yn = yc * lax.rsqrt(var + RW_GN_EPS) * gnw_ref[...] + gnb_ref[...]
        y_ref[bi] = ((yn + bonus_ref[bi]) * g_ref[bi]).astype(y_ref.dtype)


def _wkv(r, lw, k2, v, kk, alr, g, bonus, gnw, gnb, tri, segm, B, S):
    T, width = r.shape
    C = WKV_CHUNK
    nb = WKV_BATCH
    nc = S // C
    fixed = lambda b, c: (0, 0)
    blk = pl.BlockSpec((nb, C, width), lambda b, c: (b, c, 0))
    vec = pl.BlockSpec((1, width), fixed)
    ins = [t.reshape(B, S, width) for t in (r, lw, k2, v, kk, alr, g, bonus)]
    y = pl.pallas_call(
        _wkv_kernel,
        grid=(B // nb, nc),
        in_specs=[blk] * 8 + [vec, vec, pl.BlockSpec((C, C), fixed), pl.BlockSpec((width, width), fixed)],
        out_specs=blk,
        out_shape=jax.ShapeDtypeStruct((B, S, width), BF16),
        scratch_shapes=[pltpu.VMEM((nb * width // PAIR, PAIR, PAIR), F32)],
        compiler_params=_cparams(("parallel", "arbitrary")),
        name="wkv",
    )(*ins, gnw, gnb, tri, segm)
    return y.reshape(T, width)


def _split3(x):
    hi = x.astype(BF16).astype(F32)
    mid = (x - hi).astype(BF16).astype(F32)
    return hi, mid, x - hi - mid


def _pieces(x, n):
    hi, mid, lo = _split3(x)
    return hi + pltpu.roll(mid, n, axis=1) + pltpu.roll(lo, 2 * n, axis=1)


def _spread_heads(x):
    lane = lax.broadcasted_iota(jnp.int32, (x.shape[0], PAIR), 1)
    groups = []
    for p in range(x.shape[1] // PAIR):
        blk = x[:, p * PAIR:(p + 1) * PAIR]
        groups.append(jnp.where(lane < HEAD_DIM, blk, 0.0))
        groups.append(jnp.where(lane < HEAD_DIM, pltpu.roll(blk, HEAD_DIM, axis=1), 0.0))
    return jnp.concatenate(groups, axis=1)


def _front_kernel(x_ref, lnw_ref, lnb_ref, w_ref, wfz_ref,
                  mu_ref, w0_ref, w2_ref, a0_ref, a2_ref, g2_ref, kkw_ref, kaw_ref, rk_ref, seg_ref,
                  bf_ref, qw_ref, kw_ref, segm_ref, tril_ref, place_ref, maskq_ref, maskk_ref,
                  oneq_ref, onek_ref, onev_ref,
                  h_ref, r_ref, lw_ref, k_ref, v_ref, kk_ref, alr_ref, g_ref, bonus_ref,
                  qa_ref, ka_ref, va_ref, og_ref, shift_carry, c_carry, *, width, nh):
    @pl.when(pl.program_id(1) == 0)
    def _():
        shift_carry[...] = jnp.zeros_like(shift_carry)
        c_carry[...] = jnp.zeros_like(c_carry)

    h = _layer_norm(x_ref[...], lnw_ref[...], lnb_ref[...])
    h_ref[...] = h
    proj = jnp.dot(h.astype(BF16), w_ref[...], preferred_element_type=F32)
    tm = proj.shape[0]
    n_rw = mu_ref.shape[1]

    p = proj[:, :n_rw]
    prev = pltpu.roll(p, 1, axis=0)
    first_row = lax.broadcasted_iota(jnp.int32, p.shape, 0) == 0
    prev = jnp.where(first_row, shift_carry[...], prev)
    shift_carry[...] = p[tm - 1:tm, :]
    ps = p + mu_ref[...] * (prev - p)
    r = ps[:, 0:width]
    k = ps[:, width:2 * width]
    v = ps[:, 2 * width:3 * width]
    lora = ps[:, 3 * width:]
    seg = seg_ref[...]
    wl = w0_ref[...] + _dot(jnp.tanh(lora), w2_ref[...])
    w_raw = -_softplus(-wl) - 0.5
    lw_ref[...] = -jnp.exp(w_raw)
    alr = _sigmoid(a0_ref[...] + _dot(lora, a2_ref[...]))
    g_ref[...] = _dot(_sigmoid(lora), g2_ref[...])
    kkp = k * kkw_ref[...]
    nrm = jnp.sqrt(_segsum(kkp * kkp, seg))
    kk_ref[...] = kkp / jnp.maximum(nrm, 1e-12)
    k2 = k * (1.0 + (alr - 1.0) * kaw_ref[...])
    bonus_ref[...] = _segsum(r * k2 * rk_ref[...], seg) * v
    r_ref[...] = r
    k_ref[...] = k2
    v_ref[...] = v
    alr_ref[...] = alr

    fw = qw_ref.shape[1]
    q = proj[:, n_rw:n_rw + fw]
    kx = proj[:, n_rw + fw:n_rw + 2 * fw]
    vx = proj[:, n_rw + 2 * fw:n_rw + 3 * fw]
    og_ref[...] = proj[:, n_rw + 3 * fw:n_rw + 4 * fw]
    segm = segm_ref[...]
    qn = q * lax.rsqrt(_segsum(q * q, segm) + QK_EPS) * (qw_ref[...] * (HEAD_DIM ** -0.5 * LOG2E))
    kn = kx * lax.rsqrt(_segsum(kx * kx, segm) + QK_EPS) * kw_ref[...]
    fblk = proj[:, n_rw + 4 * fw:]
    h_lo = (h - h.astype(BF16).astype(F32)).astype(BF16)
    fz = fblk + pltpu.roll(fblk, PAIR - nh, axis=1) + jnp.dot(h_lo, wfz_ref[...], preferred_element_type=F32)
    in_heads = lax.broadcasted_iota(jnp.int32, fz.shape, 1) < nh
    lf = jnp.where(in_heads, -_softplus(-(fz + bf_ref[...])), 0.0)
    c3 = jnp.dot(tril_ref[...], _pieces(lf, nh).astype(BF16), preferred_element_type=F32)
    c = c3 + pltpu.roll(c3, PAIR - nh, axis=1) + pltpu.roll(c3, PAIR - 2 * nh, axis=1)
    c = jnp.where(in_heads, c, 0.0) + c_carry[...]
    c_carry[...] = c[tm - 1:tm, :]
    placed = jnp.dot(_pieces(c * LOG2E, nh).astype(BF16), place_ref[...], preferred_element_type=F32)
    qa_ref[...] = (_spread_heads(qn) + placed * maskq_ref[...] + oneq_ref[...]).astype(qa_ref.dtype)
    ka_ref[...] = (_spread_heads(kn) - placed * maskk_ref[...] + onek_ref[...]).astype(ka_ref.dtype)
    va_ref[...] = (_spread_heads(vx) + onev_ref[...]).astype(va_ref.dtype)


def _front(x2, B, S, lnw, lnb, w_main, wfz_hi, rw_params, fx_params, rw_w, fx_w, nh):
    T, D = x2.shape
    wide = nh * PAIR
    tm = ROW_TILE
    ns = S // tm
    row = lambda b, s: (b * ns + s, 0)
    fixed = lambda b, s: (0, 0)
    const = lambda a: pl.BlockSpec(a.shape, fixed)

    lane = jnp.arange(wide) % PAIR
    head = jnp.arange(wide) // PAIR
    src = jnp.arange(PAIR)
    piece = (lane - HEAD_DIM) % 3
    place = ((src[:, None] == (piece * nh + head)[None, :]) & (src[:, None] < 3 * nh)
             & (lane >= HEAD_DIM)[None, :] & (lane < HEAD_DIM + 6)[None, :]).astype(BF16)
    mask_k = ((lane >= HEAD_DIM) & (lane < HEAD_DIM + 3)).astype(F32).reshape(1, wide)
    mask_q = ((lane >= HEAD_DIM + 3) & (lane < HEAD_DIM + 6)).astype(F32).reshape(1, wide)
    one_v = (lane >= HEAD_DIM).astype(F32).reshape(1, wide)
    consts = [lnw, lnb, w_main, wfz_hi, *rw_params, *fx_params, place, mask_q, mask_k, mask_k, mask_q, one_v]

    f32_out = lambda n: (pl.BlockSpec((tm, n), row), jax.ShapeDtypeStruct((T, n), F32))
    bf_out = lambda n: (pl.BlockSpec((tm, n), row), jax.ShapeDtypeStruct((T, n), BF16))
    outs = [f32_out(D)] + [f32_out(rw_w)] * 8 + [bf_out(wide)] * 3 + [f32_out(fx_w)]
    return pl.pallas_call(
        functools.partial(_front_kernel, width=rw_w, nh=nh),
        grid=(B, ns),
        in_specs=[pl.BlockSpec((tm, D), row)] + [const(a) for a in consts],
        out_specs=[o[0] for o in outs],
        out_shape=[o[1] for o in outs],
        scratch_shapes=[pltpu.VMEM((1, rw_params[0].shape[1]), F32), pltpu.VMEM((1, PAIR), F32)],
        compiler_params=pltpu.CompilerParams(dimension_semantics=("parallel", "arbitrary"),
                                             vmem_limit_bytes=FRONT_VMEM_LIMIT),
        name="front",
    )(x2, *consts)


def _fox_attn_kernel(qi_ref, kj_ref, q_ref, k_ref, v_ref, og_ref, o_ref, m_ref, acc_ref):
    i = qi_ref[pl.program_id(2)]
    j = kj_ref[pl.program_id(2)]
    tq = q_ref.shape[0]
    tk = k_ref.shape[0]
    heads = range(ATTN_HEADS)
    grp = lambda ref, h: ref[:, h * PAIR:(h + 1) * PAIR]

    @pl.when(j == 0)
    def _():
        m_ref[...] = jnp.full_like(m_ref, NEG_BIG)
        acc_ref[...] = jnp.zeros_like(acc_ref)

    def step(masked):
        s = [lax.dot_general(grp(q_ref, h), grp(k_ref, h), (((1,), (1,)), ((), ())), preferred_element_type=F32)
             for h in heads]
        if masked:
            row = lax.broadcasted_iota(jnp.int32, (tq, tk), 0)
            col = lax.broadcasted_iota(jnp.int32, (tq, tk), 1)
            s = [jnp.where(col <= row, s[h], NEG_BIG) for h in heads]
        m_old = [m_ref[h] for h in heads]
        m_new = [jnp.maximum(m_old[h], jnp.max(s[h], axis=1, keepdims=True)) for h in heads]
        alpha = [jnp.exp2(m_old[h] - m_new[h]) for h in heads]
        pr = [jnp.exp2(s[h] - jnp.concatenate([m_new[h]] * (tk // PAIR), axis=1)).astype(BF16) for h in heads]
        pv = [jnp.dot(pr[h], grp(v_ref, h), preferred_element_type=F32) for h in heads]
        for h in heads:
            acc_ref[h] = alpha[h] * acc_ref[h] + pv[h]
            m_ref[h] = m_new[h]

    @pl.when(j < i)
    def _():
        step(False)

    @pl.when(j == i)
    def _():
        step(True)
        lane = lax.broadcasted_iota(jnp.int32, (tq, PAIR), 1)
        outs = []
        for p in range(ATTN_HEADS // 2):
            a0 = acc_ref[2 * p]
            a1 = acc_ref[2 * p + 1]
            o0 = a0 * pltpu.roll(1.0 / a0, HEAD_DIM, axis=1)
            o1 = pltpu.roll(a1, HEAD_DIM, axis=1) * (1.0 / a1)
            outs.append(jnp.where(lane < HEAD_DIM, o0, o1))
        o = jnp.concatenate(outs, axis=1)
        o_ref[...] = (o * _sigmoid(og_ref[...])).astype(o_ref.dtype)


def _fox_attn(qa, ka, va, og, B, S, width):
    T = qa.shape[0]
    tq, tk = ATTN_TQ, ATTN_TK
    assert tq == tk
    nq = S // tq
    nh = ATTN_HEADS
    ow = nh * HEAD_DIM
    ngroup = width // ow
    pairs = [(i, j) for i in range(nq) for j in range(i + 1)]
    qi = jnp.array([ij[0] for ij in pairs], jnp.int32)
    kj = jnp.array([ij[1] for ij in pairs], jnp.int32)
    kv = pl.BlockSpec((tk, nh * PAIR), lambda b, p, t, qi, kj: (b * nq + kj[t], p))
    grid_spec = pltpu.PrefetchScalarGridSpec(
        num_scalar_prefetch=2,
        grid=(B, ngroup, len(pairs)),
        in_specs=[
            pl.BlockSpec((tq, nh * PAIR), lambda b, p, t, qi, kj: (b * nq + qi[t], p)), kv, kv,
            pl.BlockSpec((tq, ow), lambda b, p, t, qi, kj: (b * nq + qi[t], p)),
        ],
        out_specs=pl.BlockSpec((tq, ow), lambda b, p, t, qi, kj: (b * nq + qi[t], p)),
        scratch_shapes=[pltpu.VMEM((nh, tq, PAIR), F32), pltpu.VMEM((nh, tq, PAIR), F32)],
    )
    return pl.pallas_call(
        _fox_attn_kernel,
        grid_spec=grid_spec,
        out_shape=jax.ShapeDtypeStruct((T, width), BF16),
        compiler_params=_cparams(("parallel", "parallel", "arbitrary")),
        name="fox_attn",
    )(qi, kj, qa, ka, va, og)


def _out_ln_kernel(yrw_ref, yfx_ref, h_ref, wo1_ref, wo2_ref, lnw_ref, lnb_ref, rw_ref, rb_ref,
                   h1_ref, logit_ref, *, alpha):
    mix = (jnp.dot(yrw_ref[...], wo1_ref[...], preferred_element_type=F32)
           + jnp.dot(yfx_ref[...], wo2_ref[...], preferred_element_type=F32))
    h1 = _layer_norm(alpha * h_ref[...] + mix, lnw_ref[...], lnb_ref[...])
    h1_ref[...] = h1
    ne = rb_ref.shape[1]
    h_hi = h1.astype(BF16)
    h_lo = (h1 - h_hi.astype(F32)).astype(BF16)
    rw = rw_ref[...]
    first = jnp.dot(h_hi, rw, preferred_element_type=F32)
    second = jnp.dot(h_lo, rw[:, :ne], preferred_element_type=F32)
    logit_ref[...] = first[:, :ne] + first[:, ne:] + second + rb_ref[...]


def _out_ln(y_rw, y_fx, h0, wo1, wo2, lnw, lnb, rw, rb, alpha):
    T, D = h0.shape
    width = y_rw.shape[1]
    ne = rb.shape[1]
    tm = LN_ROW_TILE
    row = lambda i: (i, 0)
    fixed = lambda i: (0, 0)
    rw_hi = rw.astype(BF16)
    rw_lo = (rw - rw_hi.astype(F32)).astype(BF16)
    rw = jnp.concatenate([rw_hi, rw_lo], axis=1)
    return pl.pallas_call(
        functools.partial(_out_ln_kernel, alpha=alpha),
        grid=(T // tm,),
        in_specs=[
            pl.BlockSpec((tm, width), row), pl.BlockSpec((tm, width), row), pl.BlockSpec((tm, D), row),
            pl.BlockSpec((width, D), fixed), pl.BlockSpec((width, D), fixed),
            pl.BlockSpec((1, D), fixed), pl.BlockSpec((1, D), fixed),
            pl.BlockSpec((D, 2 * ne), fixed), pl.BlockSpec((1, ne), fixed),
        ],
        out_specs=[pl.BlockSpec((tm, D), row), pl.BlockSpec((tm, ne), row)],
        out_shape=[jax.ShapeDtypeStruct((T, D), F32), jax.ShapeDtypeStruct((T, ne), F32)],
        compiler_params=_cparams(("parallel",)),
        name="out_ln",
    )(y_rw, y_fx, h0, wo1, wo2, lnw, lnb, rw, rb)


DEINT_COLS = 256
MOE_VMEM_LIMIT = 56 * 1024 * 1024


def _moe_kernel(bexp_ref, nused_ref, first_ref, nexte_ref, slot_ref, xbase_ref,
                x_hbm, w1_hbm, w2_hbm, b1g_ref, b1l_ref, b2_ref, perm_ref, o_ref,
                w1_stage, w2_stage, w1g_bf, w1l_bf, w2_bf, x_buf, sem, sem_x):
    i = pl.program_id(0)
    live = i < nused_ref[0]
    tm = x_buf.shape[1]

    def x_copy(blk, s):
        row0 = pl.multiple_of(xbase_ref[blk], ROW_ALIGN)
        return pltpu.make_async_copy(x_hbm.at[pl.ds(row0, tm), :], x_buf.at[s], sem_x.at[s])

    @pl.when(live & (i == 0))
    def _():
        x_copy(0, 0).start()

    def weight_copies(e, s):
        return (pltpu.make_async_copy(w1_hbm.at[e], w1_stage.at[s], sem.at[0, s]),
                pltpu.make_async_copy(w2_hbm.at[e], w2_stage.at[s], sem.at[1, s]))

    @pl.when(live & (i == 0))
    def _():
        for cp in weight_copies(bexp_ref[0], 0):
            cp.start()

    @pl.when(live & (first_ref[i] == 1))
    def _():
        s = slot_ref[i]
        for cp in weight_copies(bexp_ref[i], s):
            cp.wait()

        @pl.when(nexte_ref[i] >= 0)
        def _():
            for cp in weight_copies(nexte_ref[i], 1 - s):
                cp.start()

        half = DEINT_COLS // 2
        for c in range(w1_stage.shape[2] // DEINT_COLS):
            blk = w1_stage[s, :, c * DEINT_COLS:(c + 1) * DEINT_COLS].astype(BF16)
            out = jnp.dot(blk, perm_ref[...], preferred_element_type=F32).astype(BF16)
            w1g_bf[:, c * half:(c + 1) * half] = out[:, :half]
            w1l_bf[:, c * half:(c + 1) * half] = out[:, half:]
        w2_bf[...] = w2_stage[s].astype(BF16)

    @pl.when(live)
    def _():
        xs = i % 2
        x_copy(i, xs).wait()

        @pl.when(i + 1 < nused_ref[0])
        def _():
            x_copy(i + 1, 1 - xs).start()

        x = x_buf[xs].astype(BF16)
        x_glu = jnp.minimum(jnp.dot(x, w1g_bf[...], preferred_element_type=F32) + b1g_ref[0], SWIGLU_LIMIT)
        x_lin = jnp.clip(jnp.dot(x, w1l_bf[...], preferred_element_type=F32) + b1l_ref[0],
                         -SWIGLU_LIMIT, SWIGLU_LIMIT)
        act = x_glu * _sigmoid(SWIGLU_ALPHA * x_glu) * (x_lin + 1.0)
        o = jnp.dot(act.astype(BF16), w2_bf[...], preferred_element_type=F32) + b2_ref[0]
        o_ref[...] = _pack_bf16_pairs(o)

    @pl.when(jnp.logical_not(live))
    def _():
        o_ref[...] = jnp.zeros_like(o_ref)


def _moe_ffn(block_exp, n_used, x_base, n_blocks, xs, w1, b1g, b1l, w2, b2):
    E, D, F2 = w1.shape
    F = F2 // 2
    tm = MOE_TILE
    n_rows = n_blocks * tm

    idx = jnp.arange(n_blocks, dtype=jnp.int32)
    first = jnp.concatenate([jnp.ones((1,), jnp.bool_), block_exp[1:] != block_exp[:-1]])
    slot = ((jnp.cumsum(first.astype(jnp.int32)) - 1) % 2).astype(jnp.int32)
    cand = jnp.where(first & (idx < n_used[0]), idx, n_blocks)
    next_first = jnp.min(jnp.where(idx[None, :] > idx[:, None], cand[None, :], n_blocks), axis=1)
    next_e = jnp.where(next_first < n_blocks, block_exp[jnp.minimum(next_first, n_blocks - 1)], -1).astype(jnp.int32)

    half = DEINT_COLS // 2
    src = jnp.arange(DEINT_COLS)
    dst = jnp.where(src % 2 == 0, src // 2, half + src // 2)
    perm = (dst[:, None] == jnp.arange(DEINT_COLS)[None, :]).astype(BF16)

    live = lambda i, be, nu: jnp.minimum(i, nu[0] - 1)
    bspec = lambda n: pl.BlockSpec((1, 1, n), lambda i, be, nu, fi, ne, sl, xb: (be[live(i, be, nu)], 0, 0))
    grid_spec = pltpu.PrefetchScalarGridSpec(
        num_scalar_prefetch=6,
        grid=(n_blocks,),
        in_specs=[
            pl.BlockSpec(memory_space=pl.ANY),
            pl.BlockSpec(memory_space=pl.ANY),
            pl.BlockSpec(memory_space=pl.ANY),
            bspec(F), bspec(F), bspec(D),
            pl.BlockSpec((DEINT_COLS, DEINT_COLS), lambda i, be, nu, fi, ne, sl, xb: (0, 0)),
        ],
        out_specs=pl.BlockSpec((tm, D // 2), lambda i, be, nu, fi, ne, sl, xb: (i, 0)),
        scratch_shapes=[
            pltpu.VMEM((2, D, F2), F32), pltpu.VMEM((2, F, D), F32),
            pltpu.VMEM((D, F), BF16), pltpu.VMEM((D, F), BF16), pltpu.VMEM((F, D), BF16),
            pltpu.VMEM((2, tm, D), F32),
            pltpu.SemaphoreType.DMA((2, 2)), pltpu.SemaphoreType.DMA((2,)),
        ],
    )
    return pl.pallas_call(
        _moe_kernel,
        grid_spec=grid_spec,
        out_shape=jax.ShapeDtypeStruct((n_rows, D // 2), jnp.uint32),
        compiler_params=pltpu.CompilerParams(dimension_semantics=("arbitrary",), vmem_limit_bytes=MOE_VMEM_LIMIT),
        name="moe_ffn",
    )(block_exp, n_used, first.astype(jnp.int32), next_e, slot, x_base, xs, w1, w2, b1g, b1l, b2, perm)


def _combine_ln_kernel(h_ref, eo_ref, gate_ref, lnw_ref, lnb_ref, o_ref, *, alpha):
    gates = gate_ref[...]
    ffn = _unpack_bf16_pairs(eo_ref[0]) * gates[:, 0:1]
    for j in range(1, TOP_K):
        ffn = ffn + _unpack_bf16_pairs(eo_ref[j]) * gates[:, j:j + 1]
    o_ref[...] = _layer_norm(alpha * h_ref[...] + ffn, lnw_ref[...], lnb_ref[...])


def _combine_ln(h1, eo, gates, lnw, lnb, alpha):
    T, D = h1.shape
    tm = LN_ROW_TILE
    return pl.pallas_call(
        functools.partial(_combine_ln_kernel, alpha=alpha),
        grid=(T // tm,),
        in_specs=[
            pl.BlockSpec((tm, D), lambda i: (i, 0)),
            pl.BlockSpec((TOP_K, tm, D // 2), lambda i: (0, i, 0)),
            pl.BlockSpec((tm, TOP_K), lambda i: (i, 0)),
            pl.BlockSpec((1, D), lambda i: (0, 0)),
            pl.BlockSpec((1, D), lambda i: (0, 0)),
        ],
        out_specs=pl.BlockSpec((tm, D), lambda i: (i, 0)),
        out_shape=jax.ShapeDtypeStruct((T, D), F32),
        compiler_params=_cparams(("parallel",)),
        name="combine_ln",
    )(h1, eo, gates, lnw, lnb)


def _pad_to(x, n, axis):
    pad = [(0, 0)] * x.ndim
    pad[axis] = (0, n - x.shape[axis])
    return jnp.pad(x, pad)


def _block_diag_ones(width, value=1.0):
    idx = jnp.arange(width) // HEAD_DIM
    return jnp.where(idx[:, None] == idx[None, :], value, 0.0).astype(BF16)


def _route(logits, n_experts, tile):
    T = logits.shape[0]
    top_val, top_idx = lax.top_k(logits[:, :n_experts], TOP_K)
    gates = jax.nn.softmax(top_val, axis=-1)
    e_flat = top_idx.reshape(-1).astype(jnp.int32)
    n_assign = T * TOP_K
    n_blocks = n_assign // tile + n_experts
    assert n_experts * n_assign < 2 ** 31
    eids = jnp.arange(n_experts, dtype=jnp.int32)
    aids = jnp.arange(n_assign, dtype=jnp.int32)
    skeys = lax.sort(e_flat * n_assign + aids)
    order = skeys % n_assign
    e_sorted = skeys // n_assign
    counts = jnp.sum((e_flat[:, None] == eids[None, :]).astype(jnp.int32), axis=0)
    starts = jnp.cumsum(counts) - counts
    padded = (counts + tile - 1) // tile * tile
    pends = jnp.cumsum(padded)
    pstarts = pends - padded
    shift = pstarts - starts
    dest_sorted = aids + jnp.sum(jnp.where(e_sorted[:, None] == eids[None, :], shift[None, :], 0), axis=1)
    _, dest = lax.sort((order, dest_sorted), num_keys=1)
    block_start = jnp.arange(n_blocks, dtype=jnp.int32) * tile
    block_exp = jnp.minimum(jnp.sum((pends[None, :] <= block_start[:, None]).astype(jnp.int32), axis=1),
                            n_experts - 1).astype(jnp.int32)
    n_used = (pends[-1] // tile).astype(jnp.int32).reshape(1)
    c_len = (counts + ROW_ALIGN - 1) // ROW_ALIGN * ROW_ALIGN
    c_ends = jnp.cumsum(c_len)
    c_starts = c_ends - c_len
    n_compact = n_assign + ROW_ALIGN * n_experts + tile
    of_block = lambda per_expert: jnp.sum(
        jnp.where(block_exp[:, None] == eids[None, :], per_expert[None, :], 0), axis=1)
    x_base = jnp.clip(of_block(c_starts - pstarts) + block_start, 0, n_compact - tile).astype(jnp.int32)
    rows = jnp.arange(n_compact, dtype=jnp.int32)
    row_exp = jnp.minimum(jnp.sum((c_ends[None, :] <= rows[:, None]).astype(jnp.int32), axis=1), n_experts - 1)
    of_row = lambda per_expert: jnp.sum(
        jnp.where(row_exp[:, None] == eids[None, :], per_expert[None, :], 0), axis=1)
    in_exp = rows - of_row(c_starts)
    src = jnp.clip(of_row(starts) + in_exp, 0, n_assign - 1)
    gather_tok = jnp.where(in_exp < of_row(counts), order[src] // TOP_K, 0).astype(jnp.int32)
    return gates, dest.reshape(T, TOP_K), gather_tok, block_exp, n_used, x_base, n_blocks


def kernel(x, ln_in_w, ln_in_b, w_in, rw_mu, rw_w0, rw_w2, rw_a0, rw_a2, rw_g2, rw_k_k, rw_k_a, rw_r_k,
           rw_gn_w, rw_gn_b, fx_b_f, fx_q_norm, fx_k_norm, w_o, ln1_w, ln1_b, router_w, router_b,
           exp_w1, exp_b1, exp_w2, exp_b2, ln2_w, ln2_b):
    B, S, D = x.shape
    T = B * S
    depth = w_in.shape[0]
    alpha = (2 * depth) ** 0.25
    rw_w = rw_w0.shape[1]
    fx_heads = fx_b_f.shape[1]
    fx_w = fx_heads * HEAD_DIM
    d_lora, a_lora, g_lora = rw_w2.shape[1], rw_a2.shape[1], rw_g2.shape[1]
    n_lora = d_lora + a_lora + g_lora
    lora_pad = -(-n_lora // 128) * 128
    rw_cols = 3 * rw_w + n_lora
    n_rw = 3 * rw_w + lora_pad
    n_experts = router_w.shape[2]
    ne_pad = -(-n_experts // 128) * 128
    row = lambda a: a.reshape(1, -1)

    seg_rw = _block_diag_ones(rw_w)
    segm_rw = _block_diag_ones(rw_w, 1.0 / HEAD_DIM)
    segm_fx = _block_diag_ones(fx_w, 1.0 / HEAD_DIM)
    tidx = jnp.arange(ROW_TILE)
    tril = (tidx[:, None] >= tidx[None, :]).astype(BF16)
    cidx = jnp.arange(WKV_CHUNK)
    tri_c = (cidx[:, None] >= cidx[None, :]).astype(F32)

    assert depth == 1, "single-layer block"
    l = 0
    x2 = x.reshape(T, D)
    w_l = w_in[l]
    wfz = w_l[:, rw_cols + 4 * fx_w:]
    wfz_hi = wfz.astype(BF16)
    wfz_lo = (wfz - wfz_hi.astype(F32)).astype(BF16)
    w_main = jnp.concatenate(
        [_pad_to(w_l[:, :rw_cols], n_rw, 1).astype(BF16), w_l[:, rw_cols:rw_cols + 4 * fx_w].astype(BF16),
         _pad_to(jnp.concatenate([wfz_hi, wfz_lo], axis=1), PAIR, 1)], axis=1)
    mu = _pad_to(row(rw_mu[l]), n_rw, 1)
    w2p = _pad_to(rw_w2[l], lora_pad, 0).astype(BF16)
    a2p = _pad_to(jnp.pad(rw_a2[l], ((d_lora, 0), (0, 0))), lora_pad, 0).astype(BF16)
    g2p = _pad_to(jnp.pad(rw_g2[l], ((d_lora + a_lora, 0), (0, 0))), lora_pad, 0).astype(BF16)
    rw_params = [mu, row(rw_w0[l]), w2p, row(rw_a0[l]), a2p, g2p,
                 row(rw_k_k[l]), row(rw_k_a[l]), row(rw_r_k[l]), seg_rw]
    qw = row(jnp.tile(fx_q_norm[l], fx_heads))
    kw = row(jnp.tile(fx_k_norm[l], fx_heads))
    fx_params = [_pad_to(row(fx_b_f[l]), PAIR, 1), qw, kw, segm_fx, tril]
    h0, r, lw, k2, v, kk, alr, g, bonus, qa, ka, va, og = _front(
        x2, B, S, row(ln_in_w), row(ln_in_b), w_main, _pad_to(wfz_hi, PAIR, 1), rw_params, fx_params,
        rw_w, fx_w, fx_heads)

    y_rw = _wkv(r, lw, k2, v, kk, alr, g, bonus, row(rw_gn_w[l]), row(rw_gn_b[l]), tri_c, segm_rw, B, S)
    y_fx = _fox_attn(qa, ka, va, og, B, S, fx_w)

    wo = w_o[l].astype(BF16)
    rw_pad = _pad_to(router_w[l], ne_pad, 1)
    rb_pad = _pad_to(row(router_b[l]), ne_pad, 1)
    h1, logits = _out_ln(y_rw, y_fx, h0, wo[:rw_w], wo[rw_w:], row(ln1_w[l]), row(ln1_b[l]),
                         rw_pad, rb_pad, alpha)

    gates, pos, gather_tok, block_exp, n_used, x_base, n_blocks = _route(logits, n_experts, MOE_TILE)
    xs = h1[gather_tok]
    b1 = exp_b1[l]
    b1g = b1[:, None, 0::2]
    b1l = b1[:, None, 1::2]
    eo_rows = _moe_ffn(block_exp, n_used, x_base, n_blocks, xs, exp_w1[l], b1g, b1l, exp_w2[l],
                       exp_b2[l][:, None, :])
    eo = eo_rows[pos.T]
    h = _combine_ln(h1, eo, gates, row(ln2_w[l]), row(ln2_b[l]), alpha)
    return h.reshape(B, S, D)
```

```python
import functools

import jax
import jax.numpy as jnp
from jax import lax
from jax.experimental import pallas as pl
from jax.experimental.pallas import tpu as pltpu

F32 = jnp.float32
BF16 = jnp.bfloat16
HIGHEST = lax.Precision.HIGHEST

HEAD_DIM = 64
PAIR = 2 * HEAD_DIM
WKV_CHUNK = 64
RW_GN_EPS = 64e-5
QK_EPS = 1e-6
LN_EPS = 1e-5
TOP_K = 4
SWIGLU_ALPHA = 1.702
SWIGLU_LIMIT = 7.0
NEG_BIG = -1e30
LOG2E = 1.4426950408889634

ROW_TILE = 256
LN_ROW_TILE = 1024
ATTN_TQ = 512
ATTN_TK = 512
ATTN_HEADS = 8
MOE_TILE = 512
ROW_ALIGN = 8
WKV_BATCH = 4
VMEM_LIMIT = 48 * 1024 * 1024
FRONT_VMEM_LIMIT = 56 * 1024 * 1024


def _cparams(sem):
    return pltpu.CompilerParams(dimension_semantics=sem, vmem_limit_bytes=VMEM_LIMIT)


def _dot(a, b):
    return jnp.dot(a.astype(BF16), b.astype(BF16), preferred_element_type=F32)


def _dot_t(a, b):
    return lax.dot_general(a.astype(BF16), b.astype(BF16), (((1,), (1,)), ((), ())),
                           preferred_element_type=F32)


def _segsum(x, seg):
    return jnp.dot(x.astype(BF16), seg, preferred_element_type=F32)


def _sigmoid(x):
    return 1.0 / (1.0 + jnp.exp(-x))


def _softplus(x):
    return jnp.maximum(x, 0.0) + jnp.log(1.0 + jnp.exp(-jnp.abs(x)))


def _layer_norm(x, w, b):
    mu = jnp.mean(x, axis=-1, keepdims=True)
    xc = x - mu
    var = jnp.mean(xc * xc, axis=-1, keepdims=True)
    return xc * lax.rsqrt(var + LN_EPS) * w + b


def _pack_bf16_pairs(x):
    n = x.shape[1] // 2
    bits = pltpu.bitcast(x.astype(BF16).astype(F32), jnp.uint32)
    return lax.shift_right_logical(bits[:, :n], jnp.uint32(16)) | bits[:, n:]


def _unpack_bf16_pairs(u):
    lo = pltpu.bitcast(lax.shift_left(u, jnp.uint32(16)), F32)
    hi = pltpu.bitcast(u & jnp.uint32(0xFFFF0000), F32)
    return jnp.concatenate([lo, hi], axis=1)


def _stack_heads(x):
    lane = lax.broadcasted_iota(jnp.int32, x.shape, 1)
    return jnp.concatenate([jnp.where(lane < HEAD_DIM, x, 0.0), jnp.where(lane >= HEAD_DIM, x, 0.0)], axis=0)


def _wkv_kernel(r_ref, lw_ref, k_ref, v_ref, kk_ref, alr_ref, g_ref, bonus_ref, gnw_ref, gnb_ref, tri_ref,
                segm_ref, y_ref, state_ref):
    C = WKV_CHUNK
    nb, _, width = lw_ref.shape
    npair = width // PAIR

    @pl.when(pl.program_id(1) == 0)
    def _():
        state_ref[...] = jnp.zeros_like(state_ref)

    ri = lax.broadcasted_iota(jnp.int32, (2 * C, 2 * C), 0)
    ci = lax.broadcasted_iota(jnp.int32, (2 * C, 2 * C), 1)
    same = (ri // C) == (ci // C)
    strict = same & ((ci % C) < (ri % C))
    incl = same & ((ci % C) <= (ri % C))
    eye = (ri == ci).astype(F32)

    lhs, rhs, a2s, r2s, v2s, bhts, bkts, ptots = [], [], [], [], [], [], [], []
    for bi in range(nb):
        lw = lw_ref[bi]
        cum = jnp.dot(tri_ref[...], lw, precision=HIGHEST, preferred_element_type=F32)
        total = cum[C - 1:C, :]
        p_inv = jnp.exp(-cum)
        p_rem = jnp.exp(total - cum)
        p_tot = jnp.exp(total)
        kk = kk_ref[bi]
        k2 = k_ref[bi]
        b = kk * alr_ref[bi]
        a_t = -kk * jnp.exp(cum - lw)
        r_t = r_ref[bi] * jnp.exp(cum)
        b_t = b * p_inv
        k_t = k2 * p_inv
        b_h = b * p_rem
        k_h = k2 * p_rem
        v = v_ref[bi]
        for p in range(npair):
            sl = slice(p * PAIR, (p + 1) * PAIR)
            a2, r2, b2, kt2 = (_stack_heads(t[:, sl]) for t in (a_t, r_t, b_t, k_t))
            bh2, kh2, v2 = (_stack_heads(t[:, sl]) for t in (b_h, k_h, v))
            lhs.append(jnp.concatenate([a2, r2], axis=0))
            rhs.append(jnp.concatenate([b2, kt2], axis=0))
            a2s.append(a2)
            r2s.append(r2)
            v2s.append(v2)
            bhts.append(bh2.T)
            bkts.append(jnp.concatenate([bh2.T, kh2.T], axis=1))
            ptots.append(p_tot[:, sl])

    chains = range(nb * npair)
    m = [_dot_t(lhs[c], rhs[c]) for c in chains]
    n_ab = [jnp.where(strict, m[c][:2 * C, :2 * C], 0.0) for c in chains]
    m_ak = [jnp.where(strict, m[c][:2 * C, 2 * C:], 0.0) for c in chains]
    m_rb = [jnp.where(incl, m[c][2 * C:, :2 * C], 0.0) for c in chains]
    m_rk = [jnp.where(incl, m[c][2 * C:, 2 * C:], 0.0) for c in chains]
    mv = [_dot(m_ak[c], v2s[c]) for c in chains]
    mrkv = [_dot(m_rk[c], v2s[c]) for c in chains]
    inv = [eye + n_ab[c] for c in chains]
    pw = n_ab
    for _ in range(C.bit_length() - 2):
        pw = [_dot(pw[c], pw[c]) for c in chains]
        inv = [inv[c] + _dot(inv[c], pw[c]) for c in chains]
    wu = [_dot(inv[c], jnp.concatenate([a2s[c], mv[c]], axis=1)) for c in chains]
    qy = [_dot(m_rb[c], wu[c]) + jnp.concatenate([r2s[c], mrkv[c]], axis=1) for c in chains]
    g_t = [_dot(bhts[c], wu[c][:, :PAIR]) + eye * ptots[c] for c in chains]
    h_t = [_dot(bkts[c], jnp.concatenate([wu[c][:, PAIR:], v2s[c]], axis=0)) for c in chains]
    s0 = [state_ref[c] for c in chains]
    y2 = [_dot(qy[c][:, :PAIR], s0[c]) + qy[c][:, PAIR:] for c in chains]
    for c in chains:
        state_ref[c] = _dot(g_t[c], s0[c]) + h_t[c]

    segm = segm_ref[...]
    for bi in range(nb):
        y = jnp.concatenate([y2[bi * npair + p][:C] + y2[bi * npair + p][C:] for p in range(npair)], axis=1)
        mean = _segsum(y, segm)
        yc = y - mean
        var = _segsum(yc * yc, segm)
        yn = yc * lax.rsqrt(var + RW_GN_EPS) * gnw_ref[...] + gnb_ref[...]
        y_ref[bi] = ((yn + bonus_ref[bi]) * g_ref[bi]).astype(y_ref.dtype)


def _wkv(r, lw, k2, v, kk, alr, g, bonus, gnw, gnb, tri, segm, B, S):
    T, width = r.shape
    C = WKV_CHUNK
    nb = WKV_BATCH
    nc = S // C
    fixed = lambda b, c: (0, 0)
    blk = pl.BlockSpec((nb, C, width), lambda b, c: (b, c, 0))
    vec = pl.BlockSpec((1, width), fixed)
    ins = [t.reshape(B, S, width) for t in (r, lw, k2, v, kk, alr, g, bonus)]
    y = pl.pallas_call(
        _wkv_kernel,
        grid=(B // nb, nc),
        in_specs=[blk] * 8 + [vec, vec, pl.BlockSpec((C, C), fixed), pl.BlockSpec((width, width), fixed)],
        out_specs=blk,
        out_shape=jax.ShapeDtypeStruct((B, S, width), BF16),
        scratch_shapes=[pltpu.VMEM((nb * width // PAIR, PAIR, PAIR), F32)],
        compiler_params=_cparams(("parallel", "arbitrary")),
        name="wkv",
    )(*ins, gnw, gnb, tri, segm)
    return y.reshape(T, width)


def _split3(x):
    hi = x.astype(BF16).astype(F32)
    mid = (x - hi).astype(BF16).astype(F32)
    return hi, mid, x - hi - mid


def _pieces(x, n):
    hi, mid, lo = _split3(x)
    return hi + pltpu.roll(mid, n, axis=1) + pltpu.roll(lo, 2 * n, axis=1)


def _spread_heads(x):
    lane = lax.broadcasted_iota(jnp.int32, (x.shape[0], PAIR), 1)
    groups = []
    for p in range(x.shape[1] // PAIR):
        blk = x[:, p * PAIR:(p + 1) * PAIR]
        groups.append(jnp.where(lane < HEAD_DIM, blk, 0.0))
        groups.append(jnp.where(lane < HEAD_DIM, pltpu.roll(blk, HEAD_DIM, axis=1), 0.0))
    return jnp.concatenate(groups, axis=1)


def _front_kernel(x_ref, lnw_ref, lnb_ref, w_ref, wfz_ref,
                  mu_ref, w0_ref, w2_ref, a0_ref, a2_ref, g2_ref, kkw_ref, kaw_ref, rk_ref, seg_ref,
                  bf_ref, qw_ref, kw_ref, segm_ref, tril_ref, place_ref, maskq_ref, maskk_ref,
                  oneq_ref, onek_ref, onev_ref,
                  h_ref, r_ref, lw_ref, k_ref, v_ref, kk_ref, alr_ref, g_ref, bonus_ref,
                  qa_ref, ka_ref, va_ref, og_ref, shift_carry, c_carry, *, width, nh):
    @pl.when(pl.program_id(1) == 0)
    def _():
        shift_carry[...] = jnp.zeros_like(shift_carry)
        c_carry[...] = jnp.zeros_like(c_carry)

    h = _layer_norm(x_ref[...], lnw_ref[...], lnb_ref[...])
    h_ref[...] = h
    proj = jnp.dot(h.astype(BF16), w_ref[...], preferred_element_type=F32)
    tm = proj.shape[0]
    n_rw = mu_ref.shape[1]

    p = proj[:, :n_rw]
    prev = pltpu.roll(p, 1, axis=0)
    first_row = lax.broadcasted_iota(jnp.int32, p.shape, 0) == 0
    prev = jnp.where(first_row, shift_carry[...], prev)
    shift_carry[...] = p[tm - 1:tm, :]
    ps = p + mu_ref[...] * (prev - p)
    r = ps[:, 0:width]
    k = ps[:, width:2 * width]
    v = ps[:, 2 * width:3 * width]
    lora = ps[:, 3 * width:]
    seg = seg_ref[...]
    wl = w0_ref[...] + _dot(jnp.tanh(lora), w2_ref[...])
    w_raw = -_softplus(-wl) - 0.5
    lw_ref[...] = -jnp.exp(w_raw)
    alr = _sigmoid(a0_ref[...] + _dot(lora, a2_ref[...]))
    g_ref[...] = _dot(_sigmoid(lora), g2_ref[...])
    kkp = k * kkw_ref[...]
    nrm = jnp.sqrt(_segsum(kkp * kkp, seg))
    kk_ref[...] = kkp / jnp.maximum(nrm, 1e-12)
    k2 = k * (1.0 + (alr - 1.0) * kaw_ref[...])
    bonus_ref[...] = _segsum(r * k2 * rk_ref[...], seg) * v
    r_ref[...] = r
    k_ref[...] = k2
    v_ref[...] = v
    alr_ref[...] = alr

    fw = qw_ref.shape[1]
    q = proj[:, n_rw:n_rw + fw]
    kx = proj[:, n_rw + fw:n_rw + 2 * fw]
    vx = proj[:, n_rw + 2 * fw:n_rw + 3 * fw]
    og_ref[...] = proj[:, n_rw + 3 * fw:n_rw + 4 * fw]
    segm = segm_ref[...]
    qn = q * lax.rsqrt(_segsum(q * q, segm) + QK_EPS) * (qw_ref[...] * (HEAD_DIM ** -0.5 * LOG2E))
    kn = kx * lax.rsqrt(_segsum(kx * kx, segm) + QK_EPS) * kw_ref[...]
    fblk = proj[:, n_rw + 4 * fw:]
    h_lo = (h - h.astype(BF16).astype(F32)).astype(BF16)
    fz = fblk + pltpu.roll(fblk, PAIR - nh, axis=1) + jnp.dot(h_lo, wfz_ref[...], preferred_element_type=F32)
    in_heads = lax.broadcasted_iota(jnp.int32, fz.shape, 1) < nh
    lf = jnp.where(in_heads, -_softplus(-(fz + bf_ref[...])), 0.0)
    c3 = jnp.dot(tril_ref[...], _pieces(lf, nh).astype(BF16), preferred_element_type=F32)
    c = c3 + pltpu.roll(c3, PAIR - nh, axis=1) + pltpu.roll(c3, PAIR - 2 * nh, axis=1)
    c = jnp.where(in_heads, c, 0.0) + c_carry[...]
    c_carry[...] = c[tm - 1:tm, :]
    placed = jnp.dot(_pieces(c * LOG2E, nh).astype(BF16), place_ref[...], preferred_element_type=F32)
    qa_ref[...] = (_spread_heads(qn) + placed * maskq_ref[...] + oneq_ref[...]).astype(qa_ref.dtype)
    ka_ref[...] = (_spread_heads(kn) - placed * maskk_ref[...] + onek_ref[...]).astype(ka_ref.dtype)
    va_ref[...] = (_spread_heads(vx) + onev_ref[...]).astype(va_ref.dtype)


def _front(x2, B, S, lnw, lnb, w_main, wfz_hi, rw_params, fx_params, rw_w, fx_w, nh):
    T, D = x2.shape
    wide = nh * PAIR
    tm = ROW_TILE
    ns = S // tm
    row = lambda b, s: (b * ns + s, 0)
    fixed = lambda b, s: (0, 0)
    const = lambda a: pl.BlockSpec(a.shape, fixed)

    lane = jnp.arange(wide) % PAIR
    head = jnp.arange(wide) // PAIR
    src = jnp.arange(PAIR)
    piece = (lane - HEAD_DIM) % 3
    place = ((src[:, None] == (piece * nh + head)[None, :]) & (src[:, None] < 3 * nh)
             & (lane >= HEAD_DIM)[None, :] & (lane < HEAD_DIM + 6)[None, :]).astype(BF16)
    mask_k = ((lane >= HEAD_DIM) & (lane < HEAD_DIM + 3)).astype(F32).reshape(1, wide)
    mask_q = ((lane >= HEAD_DIM + 3) & (lane < HEAD_DIM + 6)).astype(F32).reshape(1, wide)
    one_v = (lane >= HEAD_DIM).astype(F32).reshape(1, wide)
    consts = [lnw, lnb, w_main, wfz_hi, *rw_params, *fx_params, place, mask_q, mask_k, mask_k, mask_q, one_v]

    f32_out = lambda n: (pl.BlockSpec((tm, n), row), jax.ShapeDtypeStruct((T, n), F32))
    bf_out = lambda n: (pl.BlockSpec((tm, n), row), jax.ShapeDtypeStruct((T, n), BF16))
    outs = [f32_out(D)] + [f32_out(rw_w)] * 8 + [bf_out(wide)] * 3 + [f32_out(fx_w)]
    return pl.pallas_call(
        functools.partial(_front_kernel, width=rw_w, nh=nh),
        grid=(B, ns),
        in_specs=[pl.BlockSpec((tm, D), row)] + [const(a) for a in consts],
        out_specs=[o[0] for o in outs],
        out_shape=[o[1] for o in outs],
        scratch_shapes=[pltpu.VMEM((1, rw_params[0].shape[1]), F32), pltpu.VMEM((1, PAIR), F32)],
        compiler_params=pltpu.CompilerParams(dimension_semantics=("parallel", "arbitrary"),
                                             vmem_limit_bytes=FRONT_VMEM_LIMIT),
        name="front",
    )(x2, *consts)


def _fox_attn_kernel(qi_ref, kj_ref, q_ref, k_ref, v_ref, og_ref, o_ref, m_ref, acc_ref):
    i = qi_ref[pl.program_id(2)]
    j = kj_ref[pl.program_id(2)]
    tq = q_ref.shape[0]
    tk = k_ref.shape[0]
    heads = range(ATTN_HEADS)
    grp = lambda ref, h: ref[:, h * PAIR:(h + 1) * PAIR]

    @pl.when(j == 0)
    def _():
        m_ref[...] = jnp.full_like(m_ref, NEG_BIG)
        acc_ref[...] = jnp.zeros_like(acc_ref)

    def step(masked):
        s = [lax.dot_general(grp(q_ref, h), grp(k_ref, h), (((1,), (1,)), ((), ())), preferred_element_type=F32)
             for h in heads]
        if masked:
            row = lax.broadcasted_iota(jnp.int32, (tq, tk), 0)
            col = lax.broadcasted_iota(jnp.int32, (tq, tk), 1)
            s = [jnp.where(col <= row, s[h], NEG_BIG) for h in heads]
        m_old = [m_ref[h] for h in heads]
        m_new = [jnp.maximum(m_old[h], jnp.max(s[h], axis=1, keepdims=True)) for h in heads]
        alpha = [jnp.exp2(m_old[h] - m_new[h]) for h in heads]
        pr = [jnp.exp2(s[h] - jnp.concatenate([m_new[h]] * (tk // PAIR), axis=1)).astype(BF16) for h in heads]
        pv = [jnp.dot(pr[h], grp(v_ref, h), preferred_element_type=F32) for h in heads]
        for h in heads:
            acc_ref[h] = alpha[h] * acc_ref[h] + pv[h]
            m_ref[h] = m_new[h]

    @pl.when(j < i)
    def _():
        step(False)

    @pl.when(j == i)
    def _():
        step(True)
        lane = lax.broadcasted_iota(jnp.int32, (tq, PAIR), 1)
        outs = []
        for p in range(ATTN_HEADS // 2):
            a0 = acc_ref[2 * p]
            a1 = acc_ref[2 * p + 1]
            o0 = a0 * pltpu.roll(1.0 / a0, HEAD_DIM, axis=1)
            o1 = pltpu.roll(a1, HEAD_DIM, axis=1) * (1.0 / a1)
            outs.append(jnp.where(lane < HEAD_DIM, o0, o1))
        o = jnp.concatenate(outs, axis=1)
        o_ref[...] = (o * _sigmoid(og_ref[...])).astype(o_ref.dtype)


def _fox_attn(qa, ka, va, og, B, S, width):
    T = qa.shape[0]
    tq, tk = ATTN_TQ, ATTN_TK
    assert tq == tk
    nq = S // tq
    nh = ATTN_HEADS
    ow = nh * HEAD_DIM
    ngroup = width // ow
    pairs = [(i, j) for i in range(nq) for j in range(i + 1)]
    qi = jnp.array([ij[0] for ij in pairs], jnp.int32)
    kj = jnp.array([ij[1] for ij in pairs], jnp.int32)
    kv = pl.BlockSpec((tk, nh * PAIR), lambda b, p, t, qi, kj: (b * nq + kj[t], p))
    grid_spec = pltpu.PrefetchScalarGridSpec(
        num_scalar_prefetch=2,
        grid=(B, ngroup, len(pairs)),
        in_specs=[
            pl.BlockSpec((tq, nh * PAIR), lambda b, p, t, qi, kj: (b * nq + qi[t], p)), kv, kv,
            pl.BlockSpec((tq, ow), lambda b, p, t, qi, kj: (b * nq + qi[t], p)),
        ],
        out_specs=pl.BlockSpec((tq, ow), lambda b, p, t, qi, kj: (b * nq + qi[t], p)),
        scratch_shapes=[pltpu.VMEM((nh, tq, PAIR), F32), pltpu.VMEM((nh, tq, PAIR), F32)],
    )
    return pl.pallas_call(
        _fox_attn_kernel,
        grid_spec=grid_spec,
        out_shape=jax.ShapeDtypeStruct((T, width), BF16),
        compiler_params=_cparams(("parallel", "parallel", "arbitrary")),
        name="fox_attn",
    )(qi, kj, qa, ka, va, og)


def _out_ln_kernel(yrw_ref, yfx_ref, h_ref, wo1_ref, wo2_ref, lnw_ref, lnb_ref, rw_ref, rb_ref,
                   h1_ref, logit_ref, *, alpha):
    mix = (jnp.dot(yrw_ref[...], wo1_ref[...], preferred_element_type=F32)
           + jnp.dot(yfx_ref[...], wo2_ref[...], preferred_element_type=F32))
    h1 = _layer_norm(alpha * h_ref[...] + mix, lnw_ref[...], lnb_ref[...])
    h1_ref[...] = h1
    ne = rb_ref.shape[1]
    h_hi = h1.astype(BF16)
    h_lo = (h1 - h_hi.astype(F32)).astype(BF16)
    rw = rw_ref[...]
    first = jnp.dot(h_hi, rw, preferred_element_type=F32)
    second = jnp.dot(h_lo, rw[:, :ne], preferred_element_type=F32)
    logit_ref[...] = first[:, :ne] + first[:, ne:] + second + rb_ref[...]


def _out_ln(y_rw, y_fx, h0, wo1, wo2, lnw, lnb, rw, rb, alpha):
    T, D = h0.shape
    width = y_rw.shape[1]
    ne = rb.shape[1]
    tm = LN_ROW_TILE
    row = lambda i: (i, 0)
    fixed = lambda i: (0, 0)
    rw_hi = rw.astype(BF16)
    rw_lo = (rw - rw_hi.astype(F32)).astype(BF16)
    rw = jnp.concatenate([rw_hi, rw_lo], axis=1)
    return pl.pallas_call(
        functools.partial(_out_ln_kernel, alpha=alpha),
        grid=(T // tm,),
        in_specs=[
            pl.BlockSpec((tm, width), row), pl.BlockSpec((tm, width), row), pl.BlockSpec((tm, D), row),
            pl.BlockSpec((width, D), fixed), pl.BlockSpec((width, D), fixed),
            pl.BlockSpec((1, D), fixed), pl.BlockSpec((1, D), fixed),
            pl.BlockSpec((D, 2 * ne), fixed), pl.BlockSpec((1, ne), fixed),
        ],
        out_specs=[pl.BlockSpec((tm, D), row), pl.BlockSpec((tm, ne), row)],
        out_shape=[jax.ShapeDtypeStruct((T, D), F32), jax.ShapeDtypeStruct((T, ne), F32)],
        compiler_params=_cparams(("parallel",)),
        name="out_ln",
    )(y_rw, y_fx, h0, wo1, wo2, lnw, lnb, rw, rb)


DEINT_COLS = 256
MOE_VMEM_LIMIT = 56 * 1024 * 1024


def _moe_kernel(bexp_ref, nused_ref, first_ref, nexte_ref, slot_ref, xbase_ref,
                x_hbm, w1_hbm, w2_hbm, b1g_ref, b1l_ref, b2_ref, perm_ref, o_ref,
                w1_stage, w2_stage, w1g_bf, w1l_bf, w2_bf, x_buf, sem, sem_x):
    i = pl.program_id(0)
    live = i < nused_ref[0]
    tm = x_buf.shape[1]

    def x_copy(blk, s):
        row0 = pl.multiple_of(xbase_ref[blk], ROW_ALIGN)
        return pltpu.make_async_copy(x_hbm.at[pl.ds(row0, tm), :], x_buf.at[s], sem_x.at[s])

    @pl.when(live & (i == 0))
    def _():
        x_copy(0, 0).start()

    def weight_copies(e, s):
        return (pltpu.make_async_copy(w1_hbm.at[e], w1_stage.at[s], sem.at[0, s]),
                pltpu.make_async_copy(w2_hbm.at[e], w2_stage.at[s], sem.at[1, s]))

    @pl.when(live & (i == 0))
    def _():
        for cp in weight_copies(bexp_ref[0], 0):
            cp.start()

    @pl.when(live & (first_ref[i] == 1))
    def _():
        s = slot_ref[i]
        for cp in weight_copies(bexp_ref[i], s):
            cp.wait()

        @pl.when(nexte_ref[i] >= 0)
        def _():
            for cp in weight_copies(nexte_ref[i], 1 - s):
                cp.start()

        half = DEINT_COLS // 2
        for c in range(w1_stage.shape[2] // DEINT_COLS):
            blk = w1_stage[s, :, c * DEINT_COLS:(c + 1) * DEINT_COLS].astype(BF16)
            out = jnp.dot(blk, perm_ref[...], preferred_element_type=F32).astype(BF16)
            w1g_bf[:, c * half:(c + 1) * half] = out[:, :half]
            w1l_bf[:, c * half:(c + 1) * half] = out[:, half:]
        w2_bf[...] = w2_stage[s].astype(BF16)

    @pl.when(live)
    def _():
        xs = i % 2
        x_copy(i, xs).wait()

        @pl.when(i + 1 < nused_ref[0])
        def _():
            x_copy(i + 1, 1 - xs).start()

        x = x_buf[xs].astype(BF16)
        x_glu = jnp.minimum(jnp.dot(x, w1g_bf[...], preferred_element_type=F32) + b1g_ref[0], SWIGLU_LIMIT)
        x_lin = jnp.clip(jnp.dot(x, w1l_bf[...], preferred_element_type=F32) + b1l_ref[0],
                         -SWIGLU_LIMIT, SWIGLU_LIMIT)
        act = x_glu * _sigmoid(SWIGLU_ALPHA * x_glu) * (x_lin + 1.0)
        o = jnp.dot(act.astype(BF16), w2_bf[...], preferred_element_type=F32) + b2_ref[0]
        o_ref[...] = _pack_bf16_pairs(o)

    @pl.when(jnp.logical_not(live))
    def _():
        o_ref[...] = jnp.zeros_like(o_ref)


def _moe_ffn(block_exp, n_used, x_base, n_blocks, xs, w1, b1g, b1l, w2, b2):
    E, D, F2 = w1.shape
    F = F2 // 2
    tm = MOE_TILE
    n_rows = n_blocks * tm

    idx = jnp.arange(n_blocks, dtype=jnp.int32)
    first = jnp.concatenate([jnp.ones((1,), jnp.bool_), block_exp[1:] != block_exp[:-1]])
    slot = ((jnp.cumsum(first.astype(jnp.int32)) - 1) % 2).astype(jnp.int32)
    cand = jnp.where(first & (idx < n_used[0]), idx, n_blocks)
    next_first = jnp.min(jnp.where(idx[None, :] > idx[:, None], cand[None, :], n_blocks), axis=1)
    next_e = jnp.where(next_first < n_blocks, block_exp[jnp.minimum(next_first, n_blocks - 1)], -1).astype(jnp.int32)

    half = DEINT_COLS // 2
    src = jnp.arange(DEINT_COLS)
    dst = jnp.where(src % 2 == 0, src // 2, half + src // 2)
    perm = (dst[:, None] == jnp.arange(DEINT_COLS)[None, :]).astype(BF16)

    live = lambda i, be, nu: jnp.minimum(i, nu[0] - 1)
    bspec = lambda n: pl.BlockSpec((1, 1, n), lambda i, be, nu, fi, ne, sl, xb: (be[live(i, be, nu)], 0, 0))
    grid_spec = pltpu.PrefetchScalarGridSpec(
        num_scalar_prefetch=6,
        grid=(n_blocks,),
        in_specs=[
            pl.BlockSpec(memory_space=pl.ANY),
            pl.BlockSpec(memory_space=pl.ANY),
            pl.BlockSpec(memory_space=pl.ANY),
            bspec(F), bspec(F), bspec(D),
            pl.BlockSpec((DEINT_COLS, DEINT_COLS), lambda i, be, nu, fi, ne, sl, xb: (0, 0)),
        ],
        out_specs=pl.BlockSpec((tm, D // 2), lambda i, be, nu, fi, ne, sl, xb: (i, 0)),
        scratch_shapes=[
            pltpu.VMEM((2, D, F2), F32), pltpu.VMEM((2, F, D), F32),
            pltpu.VMEM((D, F), BF16), pltpu.VMEM((D, F), BF16), pltpu.VMEM((F, D), BF16),
            pltpu.VMEM((2, tm, D), F32),
            pltpu.SemaphoreType.DMA((2, 2)), pltpu.SemaphoreType.DMA((2,)),
        ],
    )
    return pl.pallas_call(
        _moe_kernel,
        grid_spec=grid_spec,
        out_shape=jax.ShapeDtypeStruct((n_rows, D // 2), jnp.uint32),
        compiler_params=pltpu.CompilerParams(dimension_semantics=("arbitrary",), vmem_limit_bytes=MOE_VMEM_LIMIT),
        name="moe_ffn",
    )(block_exp, n_used, first.astype(jnp.int32), next_e, slot, x_base, xs, w1, w2, b1g, b1l, b2, perm)


def _combine_ln_kernel(h_ref, eo_ref, gate_ref, lnw_ref, lnb_ref, o_ref, *, alpha):
    gates = gate_ref[...]
    ffn = _unpack_bf16_pairs(eo_ref[0]) * gates[:, 0:1]
    for j in range(1, TOP_K):
        ffn = ffn + _unpack_bf16_pairs(eo_ref[j]) * gates[:, j:j + 1]
    o_ref[...] = _layer_norm(alpha * h_ref[...] + ffn, lnw_ref[...], lnb_ref[...])


def _combine_ln(h1, eo, gates, lnw, lnb, alpha):
    T, D = h1.shape
    tm = LN_ROW_TILE
    return pl.pallas_call(
        functools.partial(_combine_ln_kernel, alpha=alpha),
        grid=(T // tm,),
        in_specs=[
            pl.BlockSpec((tm, D), lambda i: (i, 0)),
            pl.BlockSpec((TOP_K, tm, D // 2), lambda i: (0, i, 0)),
            pl.BlockSpec((tm, TOP_K), lambda i: (i, 0)),
            pl.BlockSpec((1, D), lambda i: (0, 0)),
            pl.BlockSpec((1, D), lambda i: (0, 0)),
        ],
        out_specs=pl.BlockSpec((tm, D), lambda i: (i, 0)),
        out_shape=jax.ShapeDtypeStruct((T, D), F32),
        compiler_params=_cparams(("parallel",)),
        name="combine_ln",
    )(h1, eo, gates, lnw, lnb)


def _pad_to(x, n, axis):
    pad = [(0, 0)] * x.ndim
    pad[axis] = (0, n - x.shape[axis])
    return jnp.pad(x, pad)


def _block_diag_ones(width, value=1.0):
    idx = jnp.arange(width) // HEAD_DIM
    return jnp.where(idx[:, None] == idx[None, :], value, 0.0).astype(BF16)


def _route(logits, n_experts, tile):
    T = logits.shape[0]
    top_val, top_idx = lax.top_k(logits[:, :n_experts], TOP_K)
    gates = jax.nn.softmax(top_val, axis=-1)
    e_flat = top_idx.reshape(-1).astype(jnp.int32)
    n_assign = T * TOP_K
    assert n_assign % tile == 0 and tile % ROW_ALIGN == 0 and n_experts * n_assign < 2 ** 31
    n_blocks = n_assign // tile + n_experts + 1
    last_base = n_assign - tile
    eids = jnp.arange(n_experts, dtype=jnp.int32)
    aids = jnp.arange(n_assign, dtype=jnp.int32)
    skeys = lax.sort(e_flat * n_assign + aids)
    order = skeys % n_assign
    e_sorted = skeys // n_assign
    counts = jnp.sum((e_flat[:, None] == eids[None, :]).astype(jnp.int32), axis=0)
    starts = jnp.cumsum(counts) - counts
    base = starts // ROW_ALIGN * ROW_ALIGN
    padded = (starts - base + counts + tile - 1) // tile * tile
    pends = jnp.cumsum(padded)
    pstarts = pends - padded
    of_sorted = lambda per_expert: jnp.sum(
        jnp.where(e_sorted[:, None] == eids[None, :], per_expert[None, :], 0), axis=1)
    base_q = of_sorted(base)
    k_q = (aids - base_q) // tile
    window_q = jnp.minimum(base_q + k_q * tile, last_base)
    dest_sorted = of_sorted(pstarts) + k_q * tile + (aids - window_q)
    _, dest = lax.sort((order, dest_sorted), num_keys=1)
    block_start = jnp.arange(n_blocks, dtype=jnp.int32) * tile
    block_exp = jnp.minimum(jnp.sum((pends[None, :] <= block_start[:, None]).astype(jnp.int32), axis=1),
                            n_experts - 1).astype(jnp.int32)
    n_used = (pends[-1] // tile).astype(jnp.int32).reshape(1)
    of_block = lambda per_expert: jnp.sum(
        jnp.where(block_exp[:, None] == eids[None, :], per_expert[None, :], 0), axis=1)
    x_base = jnp.clip(of_block(base - pstarts) + block_start, 0, last_base).astype(jnp.int32)
    return gates, dest.reshape(T, TOP_K), order // TOP_K, block_exp, n_used, x_base, n_blocks


def kernel(x, ln_in_w, ln_in_b, w_in, rw_mu, rw_w0, rw_w2, rw_a0, rw_a2, rw_g2, rw_k_k, rw_k_a, rw_r_k,
           rw_gn_w, rw_gn_b, fx_b_f, fx_q_norm, fx_k_norm, w_o, ln1_w, ln1_b, router_w, router_b,
           exp_w1, exp_b1, exp_w2, exp_b2, ln2_w, ln2_b):
    B, S, D = x.shape
    T = B * S
    depth = w_in.shape[0]
    alpha = (2 * depth) ** 0.25
    rw_w = rw_w0.shape[1]
    fx_heads = fx_b_f.shape[1]
    fx_w = fx_heads * HEAD_DIM
    d_lora, a_lora, g_lora = rw_w2.shape[1], rw_a2.shape[1], rw_g2.shape[1]
    n_lora = d_lora + a_lora + g_lora
    lora_pad = -(-n_lora // 128) * 128
    rw_cols = 3 * rw_w + n_lora
    n_rw = 3 * rw_w + lora_pad
    n_experts = router_w.shape[2]
    ne_pad = -(-n_experts // 128) * 128
    row = lambda a: a.reshape(1, -1)

    seg_rw = _block_diag_ones(rw_w)
    segm_rw = _block_diag_ones(rw_w, 1.0 / HEAD_DIM)
    segm_fx = _block_diag_ones(fx_w, 1.0 / HEAD_DIM)
    tidx = jnp.arange(ROW_TILE)
    tril = (tidx[:, None] >= tidx[None, :]).astype(BF16)
    cidx = jnp.arange(WKV_CHUNK)
    tri_c = (cidx[:, None] >= cidx[None, :]).astype(F32)

    assert depth == 1, "single-layer block"
    l = 0
    x2 = x.reshape(T, D)
    w_l = w_in[l]
    wfz = w_l[:, rw_cols + 4 * fx_w:]
    wfz_hi = wfz.astype(BF16)
    wfz_lo = (wfz - wfz_hi.astype(F32)).astype(BF16)
    w_main = jnp.concatenate(
        [_pad_to(w_l[:, :rw_cols], n_rw, 1).astype(BF16), w_l[:, rw_cols:rw_cols + 4 * fx_w].astype(BF16),
         _pad_to(jnp.concatenate([wfz_hi, wfz_lo], axis=1), PAIR, 1)], axis=1)
    mu = _pad_to(row(rw_mu[l]), n_rw, 1)
    w2p = _pad_to(rw_w2[l], lora_pad, 0).astype(BF16)
    a2p = _pad_to(jnp.pad(rw_a2[l], ((d_lora, 0), (0, 0))), lora_pad, 0).astype(BF16)
    g2p = _pad_to(jnp.pad(rw_g2[l], ((d_lora + a_lora, 0), (0, 0))), lora_pad, 0).astype(BF16)
    rw_params = [mu, row(rw_w0[l]), w2p, row(rw_a0[l]), a2p, g2p,
                 row(rw_k_k[l]), row(rw_k_a[l]), row(rw_r_k[l]), seg_rw]
    qw = row(jnp.tile(fx_q_norm[l], fx_heads))
    kw = row(jnp.tile(fx_k_norm[l], fx_heads))
    fx_params = [_pad_to(row(fx_b_f[l]), PAIR, 1), qw, kw, segm_fx, tril]
    h0, r, lw, k2, v, kk, alr, g, bonus, qa, ka, va, og = _front(
        x2, B, S, row(ln_in_w), row(ln_in_b), w_main, _pad_to(wfz_hi, PAIR, 1), rw_params, fx_params,
        rw_w, fx_w, fx_heads)

    y_rw = _wkv(r, lw, k2, v, kk, alr, g, bonus, row(rw_gn_w[l]), row(rw_gn_b[l]), tri_c, segm_rw, B, S)
    y_fx = _fox_attn(qa, ka, va, og, B, S, fx_w)

    wo = w_o[l].astype(BF16)
    rw_pad = _pad_to(router_w[l], ne_pad, 1)
    rb_pad = _pad_to(row(router_b[l]), ne_pad, 1)
    h1, logits = _out_ln(y_rw, y_fx, h0, wo[:rw_w], wo[rw_w:], row(ln1_w[l]), row(ln1_b[l]),
                         rw_pad, rb_pad, alpha)

    gates, pos, gather_tok, block_exp, n_used, x_base, n_blocks = _route(logits, n_experts, MOE_TILE)
    xs = h1[gather_tok]
    b1 = exp_b1[l]
    b1g = b1[:, None, 0::2]
    b1l = b1[:, None, 1::2]
    eo_rows = _moe_ffn(block_exp, n_used, x_base, n_blocks, xs, exp_w1[l], b1g, b1l, exp_w2[l],
                       exp_b2[l][:, None, :])
    eo = eo_rows[pos.T]
    h = _combine_ln(h1, eo, gates, row(ln2_w[l]), row(ln2_b[l]), alpha)
    return h.reshape(B, S, D)
```

```python
import functools

import jax
import jax.numpy as jnp
from jax import lax
from jax.experimental import pallas as pl
from jax.experimental.pallas import tpu as pltpu

F32 = jnp.float32
BF16 = jnp.bfloat16
HIGHEST = lax.Precision.HIGHEST

HEAD_DIM = 64
PAIR = 2 * HEAD_DIM
WKV_CHUNK = 64
RW_GN_EPS = 64e-5
QK_EPS = 1e-6
LN_EPS = 1e-5
TOP_K = 4
SWIGLU_ALPHA = 1.702
SWIGLU_LIMIT = 7.0
NEG_BIG = -1e30
LOG2E = 1.4426950408889634

ROW_TILE = 256
LN_ROW_TILE = 1024
ATTN_TQ = 512
ATTN_TK = 512
ATTN_HEADS = 8
MOE_TILE = 512
ROW_ALIGN = 8
WKV_BATCH = 4
VMEM_LIMIT = 48 * 1024 * 1024
FRONT_VMEM_LIMIT = 56 * 1024 * 1024


def _cparams(sem):
    return pltpu.CompilerParams(dimension_semantics=sem, vmem_limit_bytes=VMEM_LIMIT)


def _dot(a, b):
    return jnp.dot(a.astype(BF16), b.astype(BF16), preferred_element_type=F32)


def _dot_t(a, b):
    return lax.dot_general(a.astype(BF16), b.astype(BF16), (((1,), (1,)), ((), ())),
                           preferred_element_type=F32)


def _segsum(x, seg):
    return jnp.dot(x.astype(BF16), seg, preferred_element_type=F32)


def _sigmoid(x):
    return 1.0 / (1.0 + jnp.exp(-x))


def _softplus(x):
    return jnp.maximum(x, 0.0) + jnp.log(1.0 + jnp.exp(-jnp.abs(x)))


def _layer_norm(x, w, b):
    mu = jnp.mean(x, axis=-1, keepdims=True)
    xc = x - mu
    var = jnp.mean(xc * xc, axis=-1, keepdims=True)
    return xc * lax.rsqrt(var + LN_EPS) * w + b


def _pack_bf16_pairs(x):
    n = x.shape[1] // 2
    bits = pltpu.bitcast(x.astype(BF16).astype(F32), jnp.uint32)
    return lax.shift_right_logical(bits[:, :n], jnp.uint32(16)) | bits[:, n:]


def _unpack_bf16_pairs(u):
    lo = pltpu.bitcast(lax.shift_left(u, jnp.uint32(16)), F32)
    hi = pltpu.bitcast(u & jnp.uint32(0xFFFF0000), F32)
    return jnp.concatenate([lo, hi], axis=1)


def _stack_heads(x):
    lane = lax.broadcasted_iota(jnp.int32, x.shape, 1)
    return jnp.concatenate([jnp.where(lane < HEAD_DIM, x, 0.0), jnp.where(lane >= HEAD_DIM, x, 0.0)], axis=0)


def _wkv_kernel(r_ref, lw_ref, k_ref, v_ref, kk_ref, alr_ref, g_ref, bonus_ref, gnw_ref, gnb_ref, tri_ref,
                segm_ref, y_ref, state_ref):
    C = WKV_CHUNK
    nb, _, width = lw_ref.shape
    npair = width // PAIR

    @pl.when(pl.program_id(1) == 0)
    def _():
        state_ref[...] = jnp.zeros_like(state_ref)

    ri = lax.broadcasted_iota(jnp.int32, (2 * C, 2 * C), 0)
    ci = lax.broadcasted_iota(jnp.int32, (2 * C, 2 * C), 1)
    same = (ri // C) == (ci // C)
    strict = same & ((ci % C) < (ri % C))
    incl = same & ((ci % C) <= (ri % C))
    eye = (ri == ci).astype(F32)

    lhs, rhs, a2s, r2s, v2s, bhts, bkts, ptots = [], [], [], [], [], [], [], []
    for bi in range(nb):
        lw = lw_ref[bi]
        cum = jnp.dot(tri_ref[...], lw, precision=HIGHEST, preferred_element_type=F32)
        total = cum[C - 1:C, :]
        p_inv = jnp.exp(-cum)
        p_rem = jnp.exp(total - cum)
        p_tot = jnp.exp(total)
        kk = kk_ref[bi]
        k2 = k_ref[bi]
        b = kk * alr_ref[bi]
        a_t = -kk * jnp.exp(cum - lw)
        r_t = r_ref[bi] * jnp.exp(cum)
        b_t = b * p_inv
        k_t = k2 * p_inv
        b_h = b * p_rem
        k_h = k2 * p_rem
        v = v_ref[bi]
        for p in range(npair):
            sl = slice(p * PAIR, (p + 1) * PAIR)
            a2, r2, b2, kt2 = (_stack_heads(t[:, sl]) for t in (a_t, r_t, b_t, k_t))
            bh2, kh2, v2 = (_stack_heads(t[:, sl]) for t in (b_h, k_h, v))
            lhs.append(jnp.concatenate([a2, r2], axis=0))
            rhs.append(jnp.concatenate([b2, kt2], axis=0))
            a2s.append(a2)
            r2s.append(r2)
            v2s.append(v2)
            bhts.append(bh2.T)
            bkts.append(jnp.concatenate([bh2.T, kh2.T], axis=1))
            ptots.append(p_tot[:, sl])

    chains = range(nb * npair)
    m = [_dot_t(lhs[c], rhs[c]) for c in chains]
    n_ab = [jnp.where(strict, m[c][:2 * C, :2 * C], 0.0) for c in chains]
    m_ak = [jnp.where(strict, m[c][:2 * C, 2 * C:], 0.0) for c in chains]
    m_rb = [jnp.where(incl, m[c][2 * C:, :2 * C], 0.0) for c in chains]
    m_rk = [jnp.where(incl, m[c][2 * C:, 2 * C:], 0.0) for c in chains]
    mv = [_dot(m_ak[c], v2s[c]) for c in chains]
    mrkv = [_dot(m_rk[c], v2s[c]) for c in chains]
    inv = [eye + n_ab[c] for c in chains]
    pw = n_ab
    for _ in range(C.bit_length() - 2):
        pw = [_dot(pw[c], pw[c]) for c in chains]
        inv = [inv[c] + _dot(inv[c], pw[c]) for c in chains]
    wu = [_dot(inv[c], jnp.concatenate([a2s[c], mv[c]], axis=1)) for c in chains]
    qy = [_dot(m_rb[c], wu[c]) + jnp.concatenate([r2s[c], mrkv[c]], axis=1) for c in chains]
    g_t = [_dot(bhts[c], wu[c][:, :PAIR]) + eye * ptots[c] for c in chains]
    h_t = [_dot(bkts[c], jnp.concatenate([wu[c][:, PAIR:], v2s[c]], axis=0)) for c in chains]
    s0 = [state_ref[c] for c in chains]
    y2 = [_dot(qy[c][:, :PAIR], s0[c]) + qy[c][:, PAIR:] for c in chains]
    for c in chains:
        state_ref[c] = _dot(g_t[c], s0[c]) + h_t[c]

    segm = segm_ref[...]
    for bi in range(nb):
        y = jnp.concatenate([y2[bi * npair + p][:C] + y2[bi * npair + p][C:] for p in range(npair)], axis=1)
        mean = _segsum(y, segm)
        yc = y - mean
        var = _segsum(yc * yc, segm)
        yn = yc * lax.rsqrt(var + RW_GN_EPS) * gnw_ref[...] + gnb_ref[...]
        y_ref[bi] = ((yn + bonus_ref[bi]) * g_ref[bi]).astype(y_ref.dtype)


def _wkv(r, lw, k2, v, kk, alr, g, bonus, gnw, gnb, tri, segm, B, S):
    T, width = r.shape
    C = WKV_CHUNK
    nb = WKV_BATCH
    nc = S // C
    fixed = lambda b, c: (0, 0)
    blk = pl.BlockSpec((nb, C, width), lambda b, c: (b, c, 0))
    vec = pl.BlockSpec((1, width), fixed)
    ins = [t.reshape(B, S, width) for t in (r, lw, k2, v, kk, alr, g, bonus)]
    y = pl.pallas_call(
        _wkv_kernel,
        grid=(B // nb, nc),
        in_specs=[blk] * 8 + [vec, vec, pl.BlockSpec((C, C), fixed), pl.BlockSpec((width, width), fixed)],
        out_specs=blk,
        out_shape=jax.ShapeDtypeStruct((B, S, width), BF16),
        scratch_shapes=[pltpu.VMEM((nb * width // PAIR, PAIR, PAIR), F32)],
        compiler_params=_cparams(("parallel", "arbitrary")),
        name="wkv",
    )(*ins, gnw, gnb, tri, segm)
    return y.reshape(T, width)


def _split3(x):
    hi = x.astype(BF16).astype(F32)
    mid = (x - hi).astype(BF16).astype(F32)
    return hi, mid, x - hi - mid


def _pieces(x, n):
    hi, mid, lo = _split3(x)
    return hi + pltpu.roll(mid, n, axis=1) + pltpu.roll(lo, 2 * n, axis=1)


def _spread_heads(x):
    lane = lax.broadcasted_iota(jnp.int32, (x.shape[0], PAIR), 1)
    groups = []
    for p in range(x.shape[1] // PAIR):
        blk = x[:, p * PAIR:(p + 1) * PAIR]
        groups.append(jnp.where(lane < HEAD_DIM, blk, 0.0))
        groups.append(jnp.where(lane < HEAD_DIM, pltpu.roll(blk, HEAD_DIM, axis=1), 0.0))
    return jnp.concatenate(groups, axis=1)


def _front_kernel(x_ref, lnw_ref, lnb_ref, w_ref, wfz_ref,
                  mu_ref, w0_ref, w2_ref, a0_ref, a2_ref, g2_ref, kkw_ref, kaw_ref, rk_ref, seg_ref,
                  bf_ref, qw_ref, kw_ref, segm_ref, tril_ref, place_ref, maskq_ref, maskk_ref,
                  oneq_ref, onek_ref, onev_ref,
                  h_ref, r_ref, lw_ref, k_ref, v_ref, kk_ref, alr_ref, g_ref, bonus_ref,
                  qa_ref, ka_ref, va_ref, og_ref, shift_carry, c_carry, *, width, nh):
    @pl.when(pl.program_id(1) == 0)
    def _():
        shift_carry[...] = jnp.zeros_like(shift_carry)
        c_carry[...] = jnp.zeros_like(c_carry)

    h = _layer_norm(x_ref[...], lnw_ref[...], lnb_ref[...])
    h_ref[...] = h
    proj = jnp.dot(h.astype(BF16), w_ref[...], preferred_element_type=F32)
    tm = proj.shape[0]
    n_rw = mu_ref.shape[1]

    p = proj[:, :n_rw]
    prev = pltpu.roll(p, 1, axis=0)
    first_row = lax.broadcasted_iota(jnp.int32, p.shape, 0) == 0
    prev = jnp.where(first_row, shift_carry[...], prev)
    shift_carry[...] = p[tm - 1:tm, :]
    ps = p + mu_ref[...] * (prev - p)
    r = ps[:, 0:width]
    k = ps[:, width:2 * width]
    v = ps[:, 2 * width:3 * width]
    lora = ps[:, 3 * width:]
    seg = seg_ref[...]
    wl = w0_ref[...] + _dot(jnp.tanh(lora), w2_ref[...])
    w_raw = -_softplus(-wl) - 0.5
    lw_ref[...] = -jnp.exp(w_raw)
    alr = _sigmoid(a0_ref[...] + _dot(lora, a2_ref[...]))
    g_ref[...] = _dot(_sigmoid(lora), g2_ref[...])
    kkp = k * kkw_ref[...]
    nrm = jnp.sqrt(_segsum(kkp * kkp, seg))
    kk_ref[...] = kkp / jnp.maximum(nrm, 1e-12)
    k2 = k * (1.0 + (alr - 1.0) * kaw_ref[...])
    bonus_ref[...] = _segsum(r * k2 * rk_ref[...], seg) * v
    r_ref[...] = r
    k_ref[...] = k2
    v_ref[...] = v
    alr_ref[...] = alr

    fw = qw_ref.shape[1]
    q = proj[:, n_rw:n_rw + fw]
    kx = proj[:, n_rw + fw:n_rw + 2 * fw]
    vx = proj[:, n_rw + 2 * fw:n_rw + 3 * fw]
    og_ref[...] = proj[:, n_rw + 3 * fw:n_rw + 4 * fw]
    segm = segm_ref[...]
    qn = q * lax.rsqrt(_segsum(q * q, segm) + QK_EPS) * (qw_ref[...] * (HEAD_DIM ** -0.5 * LOG2E))
    kn = kx * lax.rsqrt(_segsum(kx * kx, segm) + QK_EPS) * kw_ref[...]
    fblk = proj[:, n_rw + 4 * fw:]
    h_lo = (h - h.astype(BF16).astype(F32)).astype(BF16)
    fz = fblk + pltpu.roll(fblk, PAIR - nh, axis=1) + jnp.dot(h_lo, wfz_ref[...], preferred_element_type=F32)
    in_heads = lax.broadcasted_iota(jnp.int32, fz.shape, 1) < nh
    lf = jnp.where(in_heads, -_softplus(-(fz + bf_ref[...])), 0.0)
    c3 = jnp.dot(tril_ref[...], _pieces(lf, nh).astype(BF16), preferred_element_type=F32)
    c = c3 + pltpu.roll(c3, PAIR - nh, axis=1) + pltpu.roll(c3, PAIR - 2 * nh, axis=1)
    c = jnp.where(in_heads, c, 0.0) + c_carry[...]
    c_carry[...] = c[tm - 1:tm, :]
    placed = jnp.dot(_pieces(c * LOG2E, nh).astype(BF16), place_ref[...], preferred_element_type=F32)
    qa_ref[...] = (_spread_heads(qn) + placed * maskq_ref[...] + oneq_ref[...]).astype(qa_ref.dtype)
    ka_ref[...] = (_spread_heads(kn) - placed * maskk_ref[...] + onek_ref[...]).astype(ka_ref.dtype)
    va_ref[...] = (_spread_heads(vx) + onev_ref[...]).astype(va_ref.dtype)


def _front(x2, B, S, lnw, lnb, w_main, wfz_hi, rw_params, fx_params, rw_w, fx_w, nh):
    T, D = x2.shape
    wide = nh * PAIR
    tm = ROW_TILE
    ns = S // tm
    row = lambda b, s: (b * ns + s, 0)
    fixed = lambda b, s: (0, 0)
    const = lambda a: pl.BlockSpec(a.shape, fixed)

    lane = jnp.arange(wide) % PAIR
    head = jnp.arange(wide) // PAIR
    src = jnp.arange(PAIR)
    piece = (lane - HEAD_DIM) % 3
    place = ((src[:, None] == (piece * nh + head)[None, :]) & (src[:, None] < 3 * nh)
             & (lane >= HEAD_DIM)[None, :] & (lane < HEAD_DIM + 6)[None, :]).astype(BF16)
    mask_k = ((lane >= HEAD_DIM) & (lane < HEAD_DIM + 3)).astype(F32).reshape(1, wide)
    mask_q = ((lane >= HEAD_DIM + 3) & (lane < HEAD_DIM + 6)).astype(F32).reshape(1, wide)
    one_v = (lane >= HEAD_DIM).astype(F32).reshape(1, wide)
    consts = [lnw, lnb, w_main, wfz_hi, *rw_params, *fx_params, place, mask_q, mask_k, mask_k, mask_q, one_v]

    f32_out = lambda n: (pl.BlockSpec((tm, n), row), jax.ShapeDtypeStruct((T, n), F32))
    bf_out = lambda n: (pl.BlockSpec((tm, n), row), jax.ShapeDtypeStruct((T, n), BF16))
    outs = [f32_out(D)] + [f32_out(rw_w)] * 8 + [bf_out(wide)] * 3 + [f32_out(fx_w)]
    return pl.pallas_call(
        functools.partial(_front_kernel, width=rw_w, nh=nh),
        grid=(B, ns),
        in_specs=[pl.BlockSpec((tm, D), row)] + [const(a) for a in consts],
        out_specs=[o[0] for o in outs],
        out_shape=[o[1] for o in outs],
        scratch_shapes=[pltpu.VMEM((1, rw_params[0].shape[1]), F32), pltpu.VMEM((1, PAIR), F32)],
        compiler_params=pltpu.CompilerParams(dimension_semantics=("parallel", "arbitrary"),
                                             vmem_limit_bytes=FRONT_VMEM_LIMIT),
        name="front",
    )(x2, *consts)


def _fox_attn_kernel(qi_ref, kj_ref, q_ref, k_ref, v_ref, og_ref, o_ref, m_ref, acc_ref):
    i = qi_ref[pl.program_id(2)]
    j = kj_ref[pl.program_id(2)]
    tq = q_ref.shape[0]
    tk = k_ref.shape[0]
    heads = range(ATTN_HEADS)
    grp = lambda ref, h: ref[:, h * PAIR:(h + 1) * PAIR]

    @pl.when(j == 0)
    def _():
        m_ref[...] = jnp.full_like(m_ref, NEG_BIG)
        acc_ref[...] = jnp.zeros_like(acc_ref)

    def step(masked):
        s = [lax.dot_general(grp(q_ref, h), grp(k_ref, h), (((1,), (1,)), ((), ())), preferred_element_type=F32)
             for h in heads]
        if masked:
            row = lax.broadcasted_iota(jnp.int32, (tq, tk), 0)
            col = lax.broadcasted_iota(jnp.int32, (tq, tk), 1)
            s = [jnp.where(col <= row, s[h], NEG_BIG) for h in heads]
        m_old = [m_ref[h] for h in heads]
        m_new = [jnp.maximum(m_old[h], jnp.max(s[h], axis=1, keepdims=True)) for h in heads]
        alpha = [jnp.exp2(m_old[h] - m_new[h]) for h in heads]
        pr = [jnp.exp2(s[h] - jnp.concatenate([m_new[h]] * (tk // PAIR), axis=1)).astype(BF16) for h in heads]
        pv = [jnp.dot(pr[h], grp(v_ref, h), preferred_element_type=F32) for h in heads]
        for h in heads:
            acc_ref[h] = alpha[h] * acc_ref[h] + pv[h]
            m_ref[h] = m_new[h]

    @pl.when(j < i)
    def _():
        step(False)

    @pl.when(j == i)
    def _():
        step(True)
        lane = lax.broadcasted_iota(jnp.int32, (tq, PAIR), 1)
        outs = []
        for p in range(ATTN_HEADS // 2):
            a0 = acc_ref[2 * p]
            a1 = acc_ref[2 * p + 1]
            o0 = a0 * pltpu.roll(1.0 / a0, HEAD_DIM, axis=1)
            o1 = pltpu.roll(a1, HEAD_DIM, axis=1) * (1.0 / a1)
            outs.append(jnp.where(lane < HEAD_DIM, o0, o1))
        o = jnp.concatenate(outs, axis=1)
        o_ref[...] = (o * _sigmoid(og_ref[...])).astype(o_ref.dtype)


def _fox_attn(qa, ka, va, og, B, S, width):
    T = qa.shape[0]
    tq, tk = ATTN_TQ, ATTN_TK
    assert tq == tk
    nq = S // tq
    nh = ATTN_HEADS
    ow = nh * HEAD_DIM
    ngroup = width // ow
    pairs = [(i, j) for i in range(nq) for j in range(i + 1)]
    qi = jnp.array([ij[0] for ij in pairs], jnp.int32)
    kj = jnp.array([ij[1] for ij in pairs], jnp.int32)
    kv = pl.BlockSpec((tk, nh * PAIR), lambda b, p, t, qi, kj: (b * nq + kj[t], p))
    grid_spec = pltpu.PrefetchScalarGridSpec(
        num_scalar_prefetch=2,
        grid=(B, ngroup, len(pairs)),
        in_specs=[
            pl.BlockSpec((tq, nh * PAIR), lambda b, p, t, qi, kj: (b * nq + qi[t], p)), kv, kv,
            pl.BlockSpec((tq, ow), lambda b, p, t, qi, kj: (b * nq + qi[t], p)),
        ],
        out_specs=pl.BlockSpec((tq, ow), lambda b, p, t, qi, kj: (b * nq + qi[t], p)),
        scratch_shapes=[pltpu.VMEM((nh, tq, PAIR), F32), pltpu.VMEM((nh, tq, PAIR), F32)],
    )
    return pl.pallas_call(
        _fox_attn_kernel,
        grid_spec=grid_spec,
        out_shape=jax.ShapeDtypeStruct((T, width), BF16),
        compiler_params=_cparams(("parallel", "parallel", "arbitrary")),
        name="fox_attn",
    )(qi, kj, qa, ka, va, og)


def _out_ln_kernel(yrw_ref, yfx_ref, h_ref, wo1_ref, wo2_ref, lnw_ref, lnb_ref, rw_ref, rb_ref,
                   h1_ref, h1p_ref, logit_ref, *, alpha):
    mix = (jnp.dot(yrw_ref[...], wo1_ref[...], preferred_element_type=F32)
           + jnp.dot(yfx_ref[...], wo2_ref[...], preferred_element_type=F32))
    h1 = _layer_norm(alpha * h_ref[...] + mix, lnw_ref[...], lnb_ref[...])
    h1_ref[...] = h1
    h1p_ref[...] = pltpu.bitcast(_pack_bf16_pairs(h1), F32)
    ne = rb_ref.shape[1]
    h_hi = h1.astype(BF16)
    h_lo = (h1 - h_hi.astype(F32)).astype(BF16)
    rw = rw_ref[...]
    first = jnp.dot(h_hi, rw, preferred_element_type=F32)
    second = jnp.dot(h_lo, rw[:, :ne], preferred_element_type=F32)
    logit_ref[...] = first[:, :ne] + first[:, ne:] + second + rb_ref[...]


def _out_ln(y_rw, y_fx, h0, wo1, wo2, lnw, lnb, rw, rb, alpha):
    T, D = h0.shape
    width = y_rw.shape[1]
    ne = rb.shape[1]
    tm = LN_ROW_TILE
    row = lambda i: (i, 0)
    fixed = lambda i: (0, 0)
    rw_hi = rw.astype(BF16)
    rw_lo = (rw - rw_hi.astype(F32)).astype(BF16)
    rw = jnp.concatenate([rw_hi, rw_lo], axis=1)
    return pl.pallas_call(
        functools.partial(_out_ln_kernel, alpha=alpha),
        grid=(T // tm,),
        in_specs=[
            pl.BlockSpec((tm, width), row), pl.BlockSpec((tm, width), row), pl.BlockSpec((tm, D), row),
            pl.BlockSpec((width, D), fixed), pl.BlockSpec((width, D), fixed),
            pl.BlockSpec((1, D), fixed), pl.BlockSpec((1, D), fixed),
            pl.BlockSpec((D, 2 * ne), fixed), pl.BlockSpec((1, ne), fixed),
        ],
        out_specs=[pl.BlockSpec((tm, D), row), pl.BlockSpec((tm, D // 2), row), pl.BlockSpec((tm, ne), row)],
        out_shape=[jax.ShapeDtypeStruct((T, D), F32), jax.ShapeDtypeStruct((T, D // 2), F32),
                   jax.ShapeDtypeStruct((T, ne), F32)],
        compiler_params=_cparams(("parallel",)),
        name="out_ln",
    )(y_rw, y_fx, h0, wo1, wo2, lnw, lnb, rw, rb)


DEINT_COLS = 256
MOE_VMEM_LIMIT = 56 * 1024 * 1024


def _moe_kernel(bexp_ref, nused_ref, first_ref, nexte_ref, slot_ref, xbase_ref,
                x_hbm, w1_hbm, w2_hbm, b1g_ref, b1l_ref, b2_ref, perm_ref, o_ref,
                w1_stage, w2_stage, w1g_bf, w1l_bf, w2_bf, x_buf, sem, sem_x):
    i = pl.program_id(0)
    live = i < nused_ref[0]
    tm = x_buf.shape[1]

    def x_copy(blk, s):
        row0 = pl.multiple_of(xbase_ref[blk], ROW_ALIGN)
        return pltpu.make_async_copy(x_hbm.at[pl.ds(row0, tm), :], x_buf.at[s], sem_x.at[s])

    @pl.when(live & (i == 0))
    def _():
        x_copy(0, 0).start()

    def weight_copies(e, s):
        return (pltpu.make_async_copy(w1_hbm.at[e], w1_stage.at[s], sem.at[0, s]),
                pltpu.make_async_copy(w2_hbm.at[e], w2_stage.at[s], sem.at[1, s]))

    @pl.when(live & (i == 0))
    def _():
        for cp in weight_copies(bexp_ref[0], 0):
            cp.start()

    @pl.when(live & (first_ref[i] == 1))
    def _():
        s = slot_ref[i]
        for cp in weight_copies(bexp_ref[i], s):
            cp.wait()

        @pl.when(nexte_ref[i] >= 0)
        def _():
            for cp in weight_copies(nexte_ref[i], 1 - s):
                cp.start()

        half = DEINT_COLS // 2
        for c in range(w1_stage.shape[2] // DEINT_COLS):
            blk = w1_stage[s, :, c * DEINT_COLS:(c + 1) * DEINT_COLS].astype(BF16)
            out = jnp.dot(blk, perm_ref[...], preferred_element_type=F32).astype(BF16)
            w1g_bf[:, c * half:(c + 1) * half] = out[:, :half]
            w1l_bf[:, c * half:(c + 1) * half] = out[:, half:]
        w2_bf[...] = w2_stage[s].astype(BF16)

    @pl.when(live)
    def _():
        xs = i % 2
        x_copy(i, xs).wait()

        @pl.when(i + 1 < nused_ref[0])
        def _():
            x_copy(i + 1, 1 - xs).start()

        x = _unpack_bf16_pairs(pltpu.bitcast(x_buf[xs], jnp.uint32)).astype(BF16)
        x_glu = jnp.minimum(jnp.dot(x, w1g_bf[...], preferred_element_type=F32) + b1g_ref[0], SWIGLU_LIMIT)
        x_lin = jnp.clip(jnp.dot(x, w1l_bf[...], preferred_element_type=F32) + b1l_ref[0],
                         -SWIGLU_LIMIT, SWIGLU_LIMIT)
        act = x_glu * _sigmoid(SWIGLU_ALPHA * x_glu) * (x_lin + 1.0)
        o = jnp.dot(act.astype(BF16), w2_bf[...], preferred_element_type=F32) + b2_ref[0]
        o_ref[...] = _pack_bf16_pairs(o)

    @pl.when(jnp.logical_not(live))
    def _():
        o_ref[...] = jnp.zeros_like(o_ref)


def _moe_ffn(block_exp, n_used, x_base, n_blocks, xs, w1, b1g, b1l, w2, b2):
    E, D, F2 = w1.shape
    F = F2 // 2
    tm = MOE_TILE
    n_rows = n_blocks * tm

    idx = jnp.arange(n_blocks, dtype=jnp.int32)
    first = jnp.concatenate([jnp.ones((1,), jnp.bool_), block_exp[1:] != block_exp[:-1]])
    slot = ((jnp.cumsum(first.astype(jnp.int32)) - 1) % 2).astype(jnp.int32)
    cand = jnp.where(first & (idx < n_used[0]), idx, n_blocks)
    next_first = jnp.min(jnp.where(idx[None, :] > idx[:, None], cand[None, :], n_blocks), axis=1)
    next_e = jnp.where(next_first < n_blocks, block_exp[jnp.minimum(next_first, n_blocks - 1)], -1).astype(jnp.int32)

    half = DEINT_COLS // 2
    src = jnp.arange(DEINT_COLS)
    dst = jnp.where(src % 2 == 0, src // 2, half + src // 2)
    perm = (dst[:, None] == jnp.arange(DEINT_COLS)[None, :]).astype(BF16)

    live = lambda i, be, nu: jnp.minimum(i, nu[0] - 1)
    bspec = lambda n: pl.BlockSpec((1, 1, n), lambda i, be, nu, fi, ne, sl, xb: (be[live(i, be, nu)], 0, 0))
    grid_spec = pltpu.PrefetchScalarGridSpec(
        num_scalar_prefetch=6,
        grid=(n_blocks,),
        in_specs=[
            pl.BlockSpec(memory_space=pl.ANY),
            pl.BlockSpec(memory_space=pl.ANY),
            pl.BlockSpec(memory_space=pl.ANY),
            bspec(F), bspec(F), bspec(D),
            pl.BlockSpec((DEINT_COLS, DEINT_COLS), lambda i, be, nu, fi, ne, sl, xb: (0, 0)),
        ],
        out_specs=pl.BlockSpec((tm, D // 2), lambda i, be, nu, fi, ne, sl, xb: (i, 0)),
        scratch_shapes=[
            pltpu.VMEM((2, D, F2), F32), pltpu.VMEM((2, F, D), F32),
            pltpu.VMEM((D, F), BF16), pltpu.VMEM((D, F), BF16), pltpu.VMEM((F, D), BF16),
            pltpu.VMEM((2, tm, D // 2), F32),
            pltpu.SemaphoreType.DMA((2, 2)), pltpu.SemaphoreType.DMA((2,)),
        ],
    )
    return pl.pallas_call(
        _moe_kernel,
        grid_spec=grid_spec,
        out_shape=jax.ShapeDtypeStruct((n_rows, D // 2), jnp.uint32),
        compiler_params=pltpu.CompilerParams(dimension_semantics=("arbitrary",), vmem_limit_bytes=MOE_VMEM_LIMIT),
        name="moe_ffn",
    )(block_exp, n_used, first.astype(jnp.int32), next_e, slot, x_base, xs, w1, w2, b1g, b1l, b2, perm)


def _combine_ln_kernel(h_ref, eo_ref, gate_ref, lnw_ref, lnb_ref, o_ref, *, alpha):
    gates = gate_ref[...]
    ffn = _unpack_bf16_pairs(eo_ref[0]) * gates[:, 0:1]
    for j in range(1, TOP_K):
        ffn = ffn + _unpack_bf16_pairs(eo_ref[j]) * gates[:, j:j + 1]
    o_ref[...] = _layer_norm(alpha * h_ref[...] + ffn, lnw_ref[...], lnb_ref[...])


def _combine_ln(h1, eo, gates, lnw, lnb, alpha):
    T, D = h1.shape
    tm = LN_ROW_TILE
    return pl.pallas_call(
        functools.partial(_combine_ln_kernel, alpha=alpha),
        grid=(T // tm,),
        in_specs=[
            pl.BlockSpec((tm, D), lambda i: (i, 0)),
            pl.BlockSpec((TOP_K, tm, D // 2), lambda i: (0, i, 0)),
            pl.BlockSpec((tm, TOP_K), lambda i: (i, 0)),
            pl.BlockSpec((1, D), lambda i: (0, 0)),
            pl.BlockSpec((1, D), lambda i: (0, 0)),
        ],
        out_specs=pl.BlockSpec((tm, D), lambda i: (i, 0)),
        out_shape=jax.ShapeDtypeStruct((T, D), F32),
        compiler_params=_cparams(("parallel",)),
        name="combine_ln",
    )(h1, eo, gates, lnw, lnb)


def _pad_to(x, n, axis):
    pad = [(0, 0)] * x.ndim
    pad[axis] = (0, n - x.shape[axis])
    return jnp.pad(x, pad)


def _block_diag_ones(width, value=1.0):
    idx = jnp.arange(width) // HEAD_DIM
    return jnp.where(idx[:, None] == idx[None, :], value, 0.0).astype(BF16)


def _route(logits, n_experts, tile):
    T = logits.shape[0]
    top_val, top_idx = lax.top_k(logits[:, :n_experts], TOP_K)
    gates = jax.nn.softmax(top_val, axis=-1)
    e_flat = top_idx.reshape(-1).astype(jnp.int32)
    n_assign = T * TOP_K
    assert n_assign % tile == 0 and tile % ROW_ALIGN == 0 and n_experts * n_assign < 2 ** 31
    n_blocks = n_assign // tile + n_experts + 1
    last_base = n_assign - tile
    eids = jnp.arange(n_experts, dtype=jnp.int32)
    aids = jnp.arange(n_assign, dtype=jnp.int32)
    skeys = lax.sort(e_flat * n_assign + aids)
    order = skeys % n_assign
    e_sorted = skeys // n_assign
    counts = jnp.sum((e_flat[:, None] == eids[None, :]).astype(jnp.int32), axis=0)
    starts = jnp.cumsum(counts) - counts
    base = starts // ROW_ALIGN * ROW_ALIGN
    padded = (starts - base + counts + tile - 1) // tile * tile
    pends = jnp.cumsum(padded)
    pstarts = pends - padded
    of_sorted = lambda per_expert: jnp.sum(
        jnp.where(e_sorted[:, None] == eids[None, :], per_expert[None, :], 0), axis=1)
    base_q = of_sorted(base)
    k_q = (aids - base_q) // tile
    window_q = jnp.minimum(base_q + k_q * tile, last_base)
    dest_sorted = of_sorted(pstarts) + k_q * tile + (aids - window_q)
    _, dest = lax.sort((order, dest_sorted), num_keys=1)
    block_start = jnp.arange(n_blocks, dtype=jnp.int32) * tile
    block_exp = jnp.minimum(jnp.sum((pends[None, :] <= block_start[:, None]).astype(jnp.int32), axis=1),
                            n_experts - 1).astype(jnp.int32)
    n_used = (pends[-1] // tile).astype(jnp.int32).reshape(1)
    of_block = lambda per_expert: jnp.sum(
        jnp.where(block_exp[:, None] == eids[None, :], per_expert[None, :], 0), axis=1)
    x_base = jnp.clip(of_block(base - pstarts) + block_start, 0, last_base).astype(jnp.int32)
    return gates, dest.reshape(T, TOP_K), order // TOP_K, block_exp, n_used, x_base, n_blocks


def kernel(x, ln_in_w, ln_in_b, w_in, rw_mu, rw_w0, rw_w2, rw_a0, rw_a2, rw_g2, rw_k_k, rw_k_a, rw_r_k,
           rw_gn_w, rw_gn_b, fx_b_f, fx_q_norm, fx_k_norm, w_o, ln1_w, ln1_b, router_w, router_b,
           exp_w1, exp_b1, exp_w2, exp_b2, ln2_w, ln2_b):
    B, S, D = x.shape
    T = B * S
    depth = w_in.shape[0]
    alpha = (2 * depth) ** 0.25
    rw_w = rw_w0.shape[1]
    fx_heads = fx_b_f.shape[1]
    fx_w = fx_heads * HEAD_DIM
    d_lora, a_lora, g_lora = rw_w2.shape[1], rw_a2.shape[1], rw_g2.shape[1]
    n_lora = d_lora + a_lora + g_lora
    lora_pad = -(-n_lora // 128) * 128
    rw_cols = 3 * rw_w + n_lora
    n_rw = 3 * rw_w + lora_pad
    n_experts = router_w.shape[2]
    ne_pad = -(-n_experts // 128) * 128
    row = lambda a: a.reshape(1, -1)

    seg_rw = _block_diag_ones(rw_w)
    segm_rw = _block_diag_ones(rw_w, 1.0 / HEAD_DIM)
    segm_fx = _block_diag_ones(fx_w, 1.0 / HEAD_DIM)
    tidx = jnp.arange(ROW_TILE)
    tril = (tidx[:, None] >= tidx[None, :]).astype(BF16)
    cidx = jnp.arange(WKV_CHUNK)
    tri_c = (cidx[:, None] >= cidx[None, :]).astype(F32)

    assert depth == 1, "single-layer block"
    l = 0
    x2 = x.reshape(T, D)
    w_l = w_in[l]
    wfz = w_l[:, rw_cols + 4 * fx_w:]
    wfz_hi = wfz.astype(BF16)
    wfz_lo = (wfz - wfz_hi.astype(F32)).astype(BF16)
    w_main = jnp.concatenate(
        [_pad_to(w_l[:, :rw_cols], n_rw, 1).astype(BF16), w_l[:, rw_cols:rw_cols + 4 * fx_w].astype(BF16),
         _pad_to(jnp.concatenate([wfz_hi, wfz_lo], axis=1), PAIR, 1)], axis=1)
    mu = _pad_to(row(rw_mu[l]), n_rw, 1)
    w2p = _pad_to(rw_w2[l], lora_pad, 0).astype(BF16)
    a2p = _pad_to(jnp.pad(rw_a2[l], ((d_lora, 0), (0, 0))), lora_pad, 0).astype(BF16)
    g2p = _pad_to(jnp.pad(rw_g2[l], ((d_lora + a_lora, 0), (0, 0))), lora_pad, 0).astype(BF16)
    rw_params = [mu, row(rw_w0[l]), w2p, row(rw_a0[l]), a2p, g2p,
                 row(rw_k_k[l]), row(rw_k_a[l]), row(rw_r_k[l]), seg_rw]
    qw = row(jnp.tile(fx_q_norm[l], fx_heads))
    kw = row(jnp.tile(fx_k_norm[l], fx_heads))
    fx_params = [_pad_to(row(fx_b_f[l]), PAIR, 1), qw, kw, segm_fx, tril]
    h0, r, lw, k2, v, kk, alr, g, bonus, qa, ka, va, og = _front(
        x2, B, S, row(ln_in_w), row(ln_in_b), w_main, _pad_to(wfz_hi, PAIR, 1), rw_params, fx_params,
        rw_w, fx_w, fx_heads)

    y_rw = _wkv(r, lw, k2, v, kk, alr, g, bonus, row(rw_gn_w[l]), row(rw_gn_b[l]), tri_c, segm_rw, B, S)
    y_fx = _fox_attn(qa, ka, va, og, B, S, fx_w)

    wo = w_o[l].astype(BF16)
    rw_pad = _pad_to(router_w[l], ne_pad, 1)
    rb_pad = _pad_to(row(router_b[l]), ne_pad, 1)
    h1, h1p, logits = _out_ln(y_rw, y_fx, h0, wo[:rw_w], wo[rw_w:], row(ln1_w[l]), row(ln1_b[l]),
                              rw_pad, rb_pad, alpha)

    gates, pos, gather_tok, block_exp, n_used, x_base, n_blocks = _route(logits, n_experts, MOE_TILE)
    xs = h1p[gather_tok]
    b1 = exp_b1[l]
    b1g = b1[:, None, 0::2]
    b1l = b1[:, None, 1::2]
    eo_rows = _moe_ffn(block_exp, n_used, x_base, n_blocks, xs, exp_w1[l], b1g, b1l, exp_w2[l],
                       exp_b2[l][:, None, :])
    eo = eo_rows[pos.T]
    h = _combine_ln(h1, eo, gates, row(ln2_w[l]), row(ln2_b[l]), alpha)
    return h.reshape(B, S, D)
```

```python
import functools

import jax
import jax.numpy as jnp
from jax import lax
from jax.experimental import pallas as pl
from jax.experimental.pallas import tpu as pltpu

F32 = jnp.float32
BF16 = jnp.bfloat16
HIGHEST = lax.Precision.HIGHEST

HEAD_DIM = 64
PAIR = 2 * HEAD_DIM
WKV_CHUNK = 64
RW_GN_EPS = 64e-5
QK_EPS = 1e-6
LN_EPS = 1e-5
TOP_K = 4
SWIGLU_ALPHA = 1.702
SWIGLU_LIMIT = 7.0
NEG_BIG = -1e30
LOG2E = 1.4426950408889634

ROW_TILE = 256
LN_ROW_TILE = 1024
ATTN_TQ = 512
ATTN_TK = 512
ATTN_HEADS = 8
MOE_TILE = 512
ROW_ALIGN = 8
WKV_BATCH = 4
VMEM_LIMIT = 48 * 1024 * 1024
FRONT_VMEM_LIMIT = 56 * 1024 * 1024


def _cparams(sem):
    return pltpu.CompilerParams(dimension_semantics=sem, vmem_limit_bytes=VMEM_LIMIT)


def _dot(a, b):
    return jnp.dot(a.astype(BF16), b.astype(BF16), preferred_element_type=F32)


def _dot_t(a, b):
    return lax.dot_general(a.astype(BF16), b.astype(BF16), (((1,), (1,)), ((), ())),
                           preferred_element_type=F32)


def _segsum(x, seg):
    return jnp.dot(x.astype(BF16), seg, preferred_element_type=F32)


def _sigmoid(x):
    return 1.0 / (1.0 + jnp.exp(-x))


def _softplus(x):
    return jnp.maximum(x, 0.0) + jnp.log(1.0 + jnp.exp(-jnp.abs(x)))


def _layer_norm(x, w, b):
    mu = jnp.mean(x, axis=-1, keepdims=True)
    xc = x - mu
    var = jnp.mean(xc * xc, axis=-1, keepdims=True)
    return xc * lax.rsqrt(var + LN_EPS) * w + b


def _pack_bf16_pairs(x):
    n = x.shape[1] // 2
    bits = pltpu.bitcast(x.astype(BF16).astype(F32), jnp.uint32)
    return lax.shift_right_logical(bits[:, :n], jnp.uint32(16)) | bits[:, n:]


def _unpack_bf16_pairs(u):
    lo = pltpu.bitcast(lax.shift_left(u, jnp.uint32(16)), F32)
    hi = pltpu.bitcast(u & jnp.uint32(0xFFFF0000), F32)
    return jnp.concatenate([lo, hi], axis=1)


def _stack_heads(x):
    lane = lax.broadcasted_iota(jnp.int32, x.shape, 1)
    return jnp.concatenate([jnp.where(lane < HEAD_DIM, x, 0.0), jnp.where(lane >= HEAD_DIM, x, 0.0)], axis=0)


def _wkv_kernel(r_ref, lw_ref, k_ref, v_ref, kk_ref, alr_ref, g_ref, bonus_ref, gnw_ref, gnb_ref, tri_ref,
                segm_ref, y_ref, state_ref):
    C = WKV_CHUNK
    nb, _, width = lw_ref.shape
    npair = width // PAIR

    @pl.when(pl.program_id(1) == 0)
    def _():
        state_ref[...] = jnp.zeros_like(state_ref)

    ri = lax.broadcasted_iota(jnp.int32, (2 * C, 2 * C), 0)
    ci = lax.broadcasted_iota(jnp.int32, (2 * C, 2 * C), 1)
    same = (ri // C) == (ci // C)
    strict = same & ((ci % C) < (ri % C))
    incl = same & ((ci % C) <= (ri % C))
    eye = (ri == ci).astype(F32)

    lhs, rhs, a2s, r2s, v2s, bhts, bkts, ptots = [], [], [], [], [], [], [], []
    for bi in range(nb):
        lw = lw_ref[bi]
        cum = jnp.dot(tri_ref[...], lw, precision=HIGHEST, preferred_element_type=F32)
        total = cum[C - 1:C, :]
        p_inv = jnp.exp(-cum)
        p_rem = jnp.exp(total - cum)
        p_tot = jnp.exp(total)
        kk = kk_ref[bi]
        k2 = k_ref[bi]
        b = kk * alr_ref[bi]
        a_t = -kk * jnp.exp(cum - lw)
        r_t = r_ref[bi] * jnp.exp(cum)
        b_t = b * p_inv
        k_t = k2 * p_inv
        b_h = b * p_rem
        k_h = k2 * p_rem
        v = v_ref[bi]
        for p in range(npair):
            sl = slice(p * PAIR, (p + 1) * PAIR)
            a2, r2, b2, kt2 = (_stack_heads(t[:, sl]) for t in (a_t, r_t, b_t, k_t))
            bh2, kh2, v2 = (_stack_heads(t[:, sl]) for t in (b_h, k_h, v))
            lhs.append(jnp.concatenate([a2, r2], axis=0))
            rhs.append(jnp.concatenate([b2, kt2], axis=0))
            a2s.append(a2)
            r2s.append(r2)
            v2s.append(v2)
            bhts.append(bh2.T)
            bkts.append(jnp.concatenate([bh2.T, kh2.T], axis=1))
            ptots.append(p_tot[:, sl])

    chains = range(nb * npair)
    m = [_dot_t(lhs[c], rhs[c]) for c in chains]
    n_ab = [jnp.where(strict, m[c][:2 * C, :2 * C], 0.0) for c in chains]
    m_ak = [jnp.where(strict, m[c][:2 * C, 2 * C:], 0.0) for c in chains]
    m_rb = [jnp.where(incl, m[c][2 * C:, :2 * C], 0.0) for c in chains]
    m_rk = [jnp.where(incl, m[c][2 * C:, 2 * C:], 0.0) for c in chains]
    mv = [_dot(m_ak[c], v2s[c]) for c in chains]
    mrkv = [_dot(m_rk[c], v2s[c]) for c in chains]
    inv = [eye + n_ab[c] for c in chains]
    pw = n_ab
    for _ in range(C.bit_length() - 2):
        pw = [_dot(pw[c], pw[c]) for c in chains]
        inv = [inv[c] + _dot(inv[c], pw[c]) for c in chains]
    wu = [_dot(inv[c], jnp.concatenate([a2s[c], mv[c]], axis=1)) for c in chains]
    qy = [_dot(m_rb[c], wu[c]) + jnp.concatenate([r2s[c], mrkv[c]], axis=1) for c in chains]
    g_t = [_dot(bhts[c], wu[c][:, :PAIR]) + eye * ptots[c] for c in chains]
    h_t = [_dot(bkts[c], jnp.concatenate([wu[c][:, PAIR:], v2s[c]], axis=0)) for c in chains]
    s0 = [state_ref[c] for c in chains]
    y2 = [_dot(qy[c][:, :PAIR], s0[c]) + qy[c][:, PAIR:] for c in chains]
    for c in chains:
        state_ref[c] = _dot(g_t[c], s0[c]) + h_t[c]

    segm = segm_ref[...]
    for bi in range(nb):
        y = jnp.concatenate([y2[bi * npair + p][:C] + y2[bi * npair + p][C:] for p in range(npair)], axis=1)
        mean = _segsum(y, segm)
        yc = y - mean
        var = _segsum(yc * yc, segm)
        yn = yc * lax.rsqrt(var + RW_GN_EPS) * gnw_ref[...] + gnb_ref[...]
        y_ref[bi] = ((yn + bonus_ref[bi]) * g_ref[bi]).astype(y_ref.dtype)


def _wkv(r, lw, k2, v, kk, alr, g, bonus, gnw, gnb, tri, segm, B, S):
    T, width = r.shape
    C = WKV_CHUNK
    nb = WKV_BATCH
    nc = S // C
    fixed = lambda b, c: (0, 0)
    blk = pl.BlockSpec((nb, C, width), lambda b, c: (b, c, 0))
    vec = pl.BlockSpec((1, width), fixed)
    ins = [t.reshape(B, S, width) for t in (r, lw, k2, v, kk, alr, g, bonus)]
    y = pl.pallas_call(
        _wkv_kernel,
        grid=(B // nb, nc),
        in_specs=[blk] * 8 + [vec, vec, pl.BlockSpec((C, C), fixed), pl.BlockSpec((width, width), fixed)],
        out_specs=blk,
        out_shape=jax.ShapeDtypeStruct((B, S, width), BF16),
        scratch_shapes=[pltpu.VMEM((nb * width // PAIR, PAIR, PAIR), F32)],
        compiler_params=_cparams(("parallel", "arbitrary")),
        name="wkv",
    )(*ins, gnw, gnb, tri, segm)
    return y.reshape(T, width)


def _split3(x):
    hi = x.astype(BF16).astype(F32)
    mid = (x - hi).astype(BF16).astype(F32)
    return hi, mid, x - hi - mid


def _pieces(x, n):
    hi, mid, lo = _split3(x)
    return hi + pltpu.roll(mid, n, axis=1) + pltpu.roll(lo, 2 * n, axis=1)


def _spread_heads(x):
    lane = lax.broadcasted_iota(jnp.int32, (x.shape[0], PAIR), 1)
    groups = []
    for p in range(x.shape[1] // PAIR):
        blk = x[:, p * PAIR:(p + 1) * PAIR]
        groups.append(jnp.where(lane < HEAD_DIM, blk, 0.0))
        groups.append(jnp.where(lane < HEAD_DIM, pltpu.roll(blk, HEAD_DIM, axis=1), 0.0))
    return jnp.concatenate(groups, axis=1)


def _front_kernel(x_ref, lnw_ref, lnb_ref, w_ref, wfz_ref,
                  mu_ref, w0_ref, w2_ref, a0_ref, a2_ref, g2_ref, kkw_ref, kaw_ref, rk_ref, seg_ref,
                  bf_ref, qw_ref, kw_ref, segm_ref, tril_ref, place_ref, maskq_ref, maskk_ref,
                  oneq_ref, onek_ref, onev_ref,
                  h_ref, r_ref, lw_ref, k_ref, v_ref, kk_ref, alr_ref, g_ref, bonus_ref,
                  qa_ref, ka_ref, va_ref, og_ref, shift_carry, c_carry, *, width, nh):
    @pl.when(pl.program_id(1) == 0)
    def _():
        shift_carry[...] = jnp.zeros_like(shift_carry)
        c_carry[...] = jnp.zeros_like(c_carry)

    h = _layer_norm(x_ref[...], lnw_ref[...], lnb_ref[...])
    h_ref[...] = h
    proj = jnp.dot(h.astype(BF16), w_ref[...], preferred_element_type=F32)
    tm = proj.shape[0]
    n_rw = mu_ref.shape[1]

    p = proj[:, :n_rw]
    prev = pltpu.roll(p, 1, axis=0)
    first_row = lax.broadcasted_iota(jnp.int32, p.shape, 0) == 0
    prev = jnp.where(first_row, shift_carry[...], prev)
    shift_carry[...] = p[tm - 1:tm, :]
    ps = p + mu_ref[...] * (prev - p)
    r = ps[:, 0:width]
    k = ps[:, width:2 * width]
    v = ps[:, 2 * width:3 * width]
    lora = ps[:, 3 * width:]
    seg = seg_ref[...]
    wl = w0_ref[...] + _dot(jnp.tanh(lora), w2_ref[...])
    w_raw = -_softplus(-wl) - 0.5
    lw_ref[...] = -jnp.exp(w_raw)
    alr = _sigmoid(a0_ref[...] + _dot(lora, a2_ref[...]))
    g_ref[...] = _dot(_sigmoid(lora), g2_ref[...])
    kkp = k * kkw_ref[...]
    nrm = jnp.sqrt(_segsum(kkp * kkp, seg))
    kk_ref[...] = kkp / jnp.maximum(nrm, 1e-12)
    k2 = k * (1.0 + (alr - 1.0) * kaw_ref[...])
    bonus_ref[...] = _segsum(r * k2 * rk_ref[...], seg) * v
    r_ref[...] = r
    k_ref[...] = k2
    v_ref[...] = v
    alr_ref[...] = alr

    fw = qw_ref.shape[1]
    q = proj[:, n_rw:n_rw + fw]
    kx = proj[:, n_rw + fw:n_rw + 2 * fw]
    vx = proj[:, n_rw + 2 * fw:n_rw + 3 * fw]
    og_ref[...] = proj[:, n_rw + 3 * fw:n_rw + 4 * fw]
    segm = segm_ref[...]
    qn = q * lax.rsqrt(_segsum(q * q, segm) + QK_EPS) * (qw_ref[...] * (HEAD_DIM ** -0.5 * LOG2E))
    kn = kx * lax.rsqrt(_segsum(kx * kx, segm) + QK_EPS) * kw_ref[...]
    fblk = proj[:, n_rw + 4 * fw:]
    h_lo = (h - h.astype(BF16).astype(F32)).astype(BF16)
    fz = fblk + pltpu.roll(fblk, PAIR - nh, axis=1) + jnp.dot(h_lo, wfz_ref[...], preferred_element_type=F32)
    in_heads = lax.broadcasted_iota(jnp.int32, fz.shape, 1) < nh
    lf = jnp.where(in_heads, -_softplus(-(fz + bf_ref[...])), 0.0)
    c3 = jnp.dot(tril_ref[...], _pieces(lf, nh).astype(BF16), preferred_element_type=F32)
    c = c3 + pltpu.roll(c3, PAIR - nh, axis=1) + pltpu.roll(c3, PAIR - 2 * nh, axis=1)
    c = jnp.where(in_heads, c, 0.0) + c_carry[...]
    c_carry[...] = c[tm - 1:tm, :]
    placed = jnp.dot(_pieces(c * LOG2E, nh).astype(BF16), place_ref[...], preferred_element_type=F32)
    qa_ref[...] = (_spread_heads(qn) + placed * maskq_ref[...] + oneq_ref[...]).astype(qa_ref.dtype)
    ka_ref[...] = (_spread_heads(kn) - placed * maskk_ref[...] + onek_ref[...]).astype(ka_ref.dtype)
    va_ref[...] = (_spread_heads(vx) + onev_ref[...]).astype(va_ref.dtype)


def _front(x2, B, S, lnw, lnb, w_main, wfz_hi, rw_params, fx_params, rw_w, fx_w, nh):
    T, D = x2.shape
    wide = nh * PAIR
    tm = ROW_TILE
    ns = S // tm
    row = lambda b, s: (b * ns + s, 0)
    fixed = lambda b, s: (0, 0)
    const = lambda a: pl.BlockSpec(a.shape, fixed)

    lane = jnp.arange(wide) % PAIR
    head = jnp.arange(wide) // PAIR
    src = jnp.arange(PAIR)
    piece = (lane - HEAD_DIM) % 3
    place = ((src[:, None] == (piece * nh + head)[None, :]) & (src[:, None] < 3 * nh)
             & (lane >= HEAD_DIM)[None, :] & (lane < HEAD_DIM + 6)[None, :]).astype(BF16)
    mask_k = ((lane >= HEAD_DIM) & (lane < HEAD_DIM + 3)).astype(F32).reshape(1, wide)
    mask_q = ((lane >= HEAD_DIM + 3) & (lane < HEAD_DIM + 6)).astype(F32).reshape(1, wide)
    one_v = (lane >= HEAD_DIM).astype(F32).reshape(1, wide)
    consts = [lnw, lnb, w_main, wfz_hi, *rw_params, *fx_params, place, mask_q, mask_k, mask_k, mask_q, one_v]

    f32_out = lambda n: (pl.BlockSpec((tm, n), row), jax.ShapeDtypeStruct((T, n), F32))
    bf_out = lambda n: (pl.BlockSpec((tm, n), row), jax.ShapeDtypeStruct((T, n), BF16))
    outs = [f32_out(D)] + [f32_out(rw_w)] * 8 + [bf_out(wide)] * 3 + [f32_out(fx_w)]
    return pl.pallas_call(
        functools.partial(_front_kernel, width=rw_w, nh=nh),
        grid=(B, ns),
        in_specs=[pl.BlockSpec((tm, D), row)] + [const(a) for a in consts],
        out_specs=[o[0] for o in outs],
        out_shape=[o[1] for o in outs],
        scratch_shapes=[pltpu.VMEM((1, rw_params[0].shape[1]), F32), pltpu.VMEM((1, PAIR), F32)],
        compiler_params=pltpu.CompilerParams(dimension_semantics=("parallel", "arbitrary"),
                                             vmem_limit_bytes=FRONT_VMEM_LIMIT),
        name="front",
    )(x2, *consts)


def _fox_attn_kernel(qi_ref, kj_ref, q_ref, k_ref, v_ref, og_ref, o_ref, m_ref, acc_ref):
    i = qi_ref[pl.program_id(2)]
    j = kj_ref[pl.program_id(2)]
    tq = q_ref.shape[0]
    tk = k_ref.shape[0]
    heads = range(ATTN_HEADS)
    grp = lambda ref, h: ref[:, h * PAIR:(h + 1) * PAIR]

    @pl.when(j == 0)
    def _():
        m_ref[...] = jnp.full_like(m_ref, NEG_BIG)
        acc_ref[...] = jnp.zeros_like(acc_ref)

    def step(masked):
        s = [lax.dot_general(grp(q_ref, h), grp(k_ref, h), (((1,), (1,)), ((), ())), preferred_element_type=F32)
             for h in heads]
        if masked:
            row = lax.broadcasted_iota(jnp.int32, (tq, tk), 0)
            col = lax.broadcasted_iota(jnp.int32, (tq, tk), 1)
            s = [jnp.where(col <= row, s[h], NEG_BIG) for h in heads]
        m_old = [m_ref[h] for h in heads]
        m_new = [jnp.maximum(m_old[h], jnp.max(s[h], axis=1, keepdims=True)) for h in heads]
        alpha = [jnp.exp2(m_old[h] - m_new[h]) for h in heads]
        pr = [jnp.exp2(s[h] - jnp.concatenate([m_new[h]] * (tk // PAIR), axis=1)).astype(BF16) for h in heads]
        pv = [jnp.dot(pr[h], grp(v_ref, h), preferred_element_type=F32) for h in heads]
        for h in heads:
            acc_ref[h] = alpha[h] * acc_ref[h] + pv[h]
            m_ref[h] = m_new[h]

    @pl.when(j < i)
    def _():
        step(False)

    @pl.when(j == i)
    def _():
        step(True)
        lane = lax.broadcasted_iota(jnp.int32, (tq, PAIR), 1)
        outs = []
        for p in range(ATTN_HEADS // 2):
            a0 = acc_ref[2 * p]
            a1 = acc_ref[2 * p + 1]
            o0 = a0 * pltpu.roll(1.0 / a0, HEAD_DIM, axis=1)
            o1 = pltpu.roll(a1, HEAD_DIM, axis=1) * (1.0 / a1)
            outs.append(jnp.where(lane < HEAD_DIM, o0, o1))
        o = jnp.concatenate(outs, axis=1)
        o_ref[...] = (o * _sigmoid(og_ref[...])).astype(o_ref.dtype)


def _fox_attn(qa, ka, va, og, B, S, width):
    T = qa.shape[0]
    tq, tk = ATTN_TQ, ATTN_TK
    assert tq == tk
    nq = S // tq
    nh = ATTN_HEADS
    ow = nh * HEAD_DIM
    ngroup = width // ow
    pairs = [(i, j) for i in range(nq) for j in range(i + 1)]
    qi = jnp.array([ij[0] for ij in pairs], jnp.int32)
    kj = jnp.array([ij[1] for ij in pairs], jnp.int32)
    kv = pl.BlockSpec((tk, nh * PAIR), lambda b, p, t, qi, kj: (b * nq + kj[t], p))
    grid_spec = pltpu.PrefetchScalarGridSpec(
        num_scalar_prefetch=2,
        grid=(B, ngroup, len(pairs)),
        in_specs=[
            pl.BlockSpec((tq, nh * PAIR), lambda b, p, t, qi, kj: (b * nq + qi[t], p)), kv, kv,
            pl.BlockSpec((tq, ow), lambda b, p, t, qi, kj: (b * nq + qi[t], p)),
        ],
        out_specs=pl.BlockSpec((tq, ow), lambda b, p, t, qi, kj: (b * nq + qi[t], p)),
        scratch_shapes=[pltpu.VMEM((nh, tq, PAIR), F32), pltpu.VMEM((nh, tq, PAIR), F32)],
    )
    return pl.pallas_call(
        _fox_attn_kernel,
        grid_spec=grid_spec,
        out_shape=jax.ShapeDtypeStruct((T, width), BF16),
        compiler_params=_cparams(("parallel", "parallel", "arbitrary")),
        name="fox_attn",
    )(qi, kj, qa, ka, va, og)


def _out_ln_kernel(yrw_ref, yfx_ref, h_ref, wo1_ref, wo2_ref, lnw_ref, lnb_ref, rw_ref, rb_ref,
                   h1_ref, logit_ref, *, alpha):
    mix = (jnp.dot(yrw_ref[...], wo1_ref[...], preferred_element_type=F32)
           + jnp.dot(yfx_ref[...], wo2_ref[...], preferred_element_type=F32))
    h1 = _layer_norm(alpha * h_ref[...] + mix, lnw_ref[...], lnb_ref[...])
    h1_ref[...] = h1
    ne = rb_ref.shape[1]
    h_hi = h1.astype(BF16)
    h_lo = (h1 - h_hi.astype(F32)).astype(BF16)
    rw = rw_ref[...]
    first = jnp.dot(h_hi, rw, preferred_element_type=F32)
    second = jnp.dot(h_lo, rw[:, :ne], preferred_element_type=F32)
    logit_ref[...] = first[:, :ne] + first[:, ne:] + second + rb_ref[...]


def _out_ln(y_rw, y_fx, h0, wo1, wo2, lnw, lnb, rw, rb, alpha):
    T, D = h0.shape
    width = y_rw.shape[1]
    ne = rb.shape[1]
    tm = LN_ROW_TILE
    row = lambda i: (i, 0)
    fixed = lambda i: (0, 0)
    rw_hi = rw.astype(BF16)
    rw_lo = (rw - rw_hi.astype(F32)).astype(BF16)
    rw = jnp.concatenate([rw_hi, rw_lo], axis=1)
    return pl.pallas_call(
        functools.partial(_out_ln_kernel, alpha=alpha),
        grid=(T // tm,),
        in_specs=[
            pl.BlockSpec((tm, width), row), pl.BlockSpec((tm, width), row), pl.BlockSpec((tm, D), row),
            pl.BlockSpec((width, D), fixed), pl.BlockSpec((width, D), fixed),
            pl.BlockSpec((1, D), fixed), pl.BlockSpec((1, D), fixed),
            pl.BlockSpec((D, 2 * ne), fixed), pl.BlockSpec((1, ne), fixed),
        ],
        out_specs=[pl.BlockSpec((tm, D), row), pl.BlockSpec((tm, ne), row)],
        out_shape=[jax.ShapeDtypeStruct((T, D), F32), jax.ShapeDtypeStruct((T, ne), F32)],
        compiler_params=_cparams(("parallel",)),
        name="out_ln",
    )(y_rw, y_fx, h0, wo1, wo2, lnw, lnb, rw, rb)


DEINT_COLS = 256
MOE_VMEM_LIMIT = 56 * 1024 * 1024


def _moe_kernel(bexp_ref, nused_ref, first_ref, nexte_ref, slot_ref, xbase_ref, xrows_ref,
                x_hbm, w1_hbm, w2_hbm, b1g_ref, b1l_ref, b2_ref, perm_ref, o_ref,
                w1_stage, w2_stage, w1g_bf, w1l_bf, w2_bf, x_buf, sem, sem_x):
    i = pl.program_id(0)
    live = i < nused_ref[0]
    tm = x_buf.shape[1]

    def x_copy(blk, s):
        row0 = pl.multiple_of(xbase_ref[blk], ROW_ALIGN)
        return pltpu.make_async_copy(x_hbm.at[pl.ds(row0, tm), :], x_buf.at[s], sem_x.at[s])

    @pl.when(live & (i == 0))
    def _():
        x_copy(0, 0).start()

    def weight_copies(e, s):
        return (pltpu.make_async_copy(w1_hbm.at[e], w1_stage.at[s], sem.at[0, s]),
                pltpu.make_async_copy(w2_hbm.at[e], w2_stage.at[s], sem.at[1, s]))

    @pl.when(live & (i == 0))
    def _():
        for cp in weight_copies(bexp_ref[0], 0):
            cp.start()

    @pl.when(live & (first_ref[i] == 1))
    def _():
        s = slot_ref[i]
        for cp in weight_copies(bexp_ref[i], s):
            cp.wait()

        @pl.when(nexte_ref[i] >= 0)
        def _():
            for cp in weight_copies(nexte_ref[i], 1 - s):
                cp.start()

        half = DEINT_COLS // 2
        for c in range(w1_stage.shape[2] // DEINT_COLS):
            blk = w1_stage[s, :, c * DEINT_COLS:(c + 1) * DEINT_COLS].astype(BF16)
            out = jnp.dot(blk, perm_ref[...], preferred_element_type=F32).astype(BF16)
            w1g_bf[:, c * half:(c + 1) * half] = out[:, :half]
            w1l_bf[:, c * half:(c + 1) * half] = out[:, half:]
        w2_bf[...] = w2_stage[s].astype(BF16)

    @pl.when(live)
    def _():
        xs = i % 2
        x_copy(i, xs).wait()

        @pl.when(i + 1 < nused_ref[0])
        def _():
            x_copy(i + 1, 1 - xs).start()

        def ffn(x):
            x = x.astype(BF16)
            x_glu = jnp.minimum(jnp.dot(x, w1g_bf[...], preferred_element_type=F32) + b1g_ref[0], SWIGLU_LIMIT)
            x_lin = jnp.clip(jnp.dot(x, w1l_bf[...], preferred_element_type=F32) + b1l_ref[0],
                             -SWIGLU_LIMIT, SWIGLU_LIMIT)
            act = x_glu * _sigmoid(SWIGLU_ALPHA * x_glu) * (x_lin + 1.0)
            return _pack_bf16_pairs(jnp.dot(act.astype(BF16), w2_bf[...], preferred_element_type=F32) + b2_ref[0])

        half = tm // 2
        short = xrows_ref[i] <= half

        @pl.when(short)
        def _():
            o_ref[:half, :] = ffn(x_buf[xs, :half, :])
            o_ref[half:, :] = jnp.zeros((tm - half, o_ref.shape[1]), o_ref.dtype)

        @pl.when(jnp.logical_not(short))
        def _():
            o_ref[...] = ffn(x_buf[xs])

    @pl.when(jnp.logical_not(live))
    def _():
        o_ref[...] = jnp.zeros_like(o_ref)


def _moe_ffn(block_exp, n_used, x_base, x_rows, n_blocks, xs, w1, b1g, b1l, w2, b2):
    E, D, F2 = w1.shape
    F = F2 // 2
    tm = MOE_TILE
    n_rows = n_blocks * tm

    idx = jnp.arange(n_blocks, dtype=jnp.int32)
    first = jnp.concatenate([jnp.ones((1,), jnp.bool_), block_exp[1:] != block_exp[:-1]])
    slot = ((jnp.cumsum(first.astype(jnp.int32)) - 1) % 2).astype(jnp.int32)
    cand = jnp.where(first & (idx < n_used[0]), idx, n_blocks)
    next_first = jnp.min(jnp.where(idx[None, :] > idx[:, None], cand[None, :], n_blocks), axis=1)
    next_e = jnp.where(next_first < n_blocks, block_exp[jnp.minimum(next_first, n_blocks - 1)], -1).astype(jnp.int32)

    half = DEINT_COLS // 2
    src = jnp.arange(DEINT_COLS)
    dst = jnp.where(src % 2 == 0, src // 2, half + src // 2)
    perm = (dst[:, None] == jnp.arange(DEINT_COLS)[None, :]).astype(BF16)

    live = lambda i, be, nu: jnp.minimum(i, nu[0] - 1)
    bspec = lambda n: pl.BlockSpec((1, 1, n), lambda i, be, nu, fi, ne, sl, xb, xr: (be[live(i, be, nu)], 0, 0))
    grid_spec = pltpu.PrefetchScalarGridSpec(
        num_scalar_prefetch=7,
        grid=(n_blocks,),
        in_specs=[
            pl.BlockSpec(memory_space=pl.ANY),
            pl.BlockSpec(memory_space=pl.ANY),
            pl.BlockSpec(memory_space=pl.ANY),
            bspec(F), bspec(F), bspec(D),
            pl.BlockSpec((DEINT_COLS, DEINT_COLS), lambda i, be, nu, fi, ne, sl, xb, xr: (0, 0)),
        ],
        out_specs=pl.BlockSpec((tm, D // 2), lambda i, be, nu, fi, ne, sl, xb, xr: (i, 0)),
        scratch_shapes=[
            pltpu.VMEM((2, D, F2), F32), pltpu.VMEM((2, F, D), F32),
            pltpu.VMEM((D, F), BF16), pltpu.VMEM((D, F), BF16), pltpu.VMEM((F, D), BF16),
            pltpu.VMEM((2, tm, D), F32),
            pltpu.SemaphoreType.DMA((2, 2)), pltpu.SemaphoreType.DMA((2,)),
        ],
    )
    return pl.pallas_call(
        _moe_kernel,
        grid_spec=grid_spec,
        out_shape=jax.ShapeDtypeStruct((n_rows, D // 2), jnp.uint32),
        compiler_params=pltpu.CompilerParams(dimension_semantics=("arbitrary",), vmem_limit_bytes=MOE_VMEM_LIMIT),
        name="moe_ffn",
    )(block_exp, n_used, first.astype(jnp.int32), next_e, slot, x_base, x_rows, xs, w1, w2, b1g, b1l, b2, perm)


def _combine_ln_kernel(h_ref, eo_ref, gate_ref, lnw_ref, lnb_ref, o_ref, *, alpha):
    gates = gate_ref[...]
    ffn = _unpack_bf16_pairs(eo_ref[0]) * gates[:, 0:1]
    for j in range(1, TOP_K):
        ffn = ffn + _unpack_bf16_pairs(eo_ref[j]) * gates[:, j:j + 1]
    o_ref[...] = _layer_norm(alpha * h_ref[...] + ffn, lnw_ref[...], lnb_ref[...])


def _combine_ln(h1, eo, gates, lnw, lnb, alpha):
    T, D = h1.shape
    tm = LN_ROW_TILE
    return pl.pallas_call(
        functools.partial(_combine_ln_kernel, alpha=alpha),
        grid=(T // tm,),
        in_specs=[
            pl.BlockSpec((tm, D), lambda i: (i, 0)),
            pl.BlockSpec((TOP_K, tm, D // 2), lambda i: (0, i, 0)),
            pl.BlockSpec((tm, TOP_K), lambda i: (i, 0)),
            pl.BlockSpec((1, D), lambda i: (0, 0)),
            pl.BlockSpec((1, D), lambda i: (0, 0)),
        ],
        out_specs=pl.BlockSpec((tm, D), lambda i: (i, 0)),
        out_shape=jax.ShapeDtypeStruct((T, D), F32),
        compiler_params=_cparams(("parallel",)),
        name="combine_ln",
    )(h1, eo, gates, lnw, lnb)


def _pad_to(x, n, axis):
    pad = [(0, 0)] * x.ndim
    pad[axis] = (0, n - x.shape[axis])
    return jnp.pad(x, pad)


def _block_diag_ones(width, value=1.0):
    idx = jnp.arange(width) // HEAD_DIM
    return jnp.where(idx[:, None] == idx[None, :], value, 0.0).astype(BF16)


def _route(logits, n_experts, tile):
    T = logits.shape[0]
    top_val, top_idx = lax.top_k(logits[:, :n_experts], TOP_K)
    gates = jax.nn.softmax(top_val, axis=-1)
    e_flat = top_idx.reshape(-1).astype(jnp.int32)
    n_assign = T * TOP_K
    assert n_assign % tile == 0 and tile % ROW_ALIGN == 0 and n_experts * n_assign < 2 ** 31
    n_blocks = n_assign // tile + n_experts + 1
    last_base = n_assign - tile
    eids = jnp.arange(n_experts, dtype=jnp.int32)
    aids = jnp.arange(n_assign, dtype=jnp.int32)
    skeys = lax.sort(e_flat * n_assign + aids)
    order = skeys % n_assign
    e_sorted = skeys // n_assign
    counts = jnp.sum((e_flat[:, None] == eids[None, :]).astype(jnp.int32), axis=0)
    starts = jnp.cumsum(counts) - counts
    base = starts // ROW_ALIGN * ROW_ALIGN
    padded = (starts - base + counts + tile - 1) // tile * tile
    pends = jnp.cumsum(padded)
    pstarts = pends - padded
    of_sorted = lambda per_expert: jnp.sum(
        jnp.where(e_sorted[:, None] == eids[None, :], per_expert[None, :], 0), axis=1)
    base_q = of_sorted(base)
    k_q = (aids - base_q) // tile
    window_q = jnp.minimum(base_q + k_q * tile, last_base)
    dest_sorted = of_sorted(pstarts) + k_q * tile + (aids - window_q)
    _, dest = lax.sort((order, dest_sorted), num_keys=1)
    block_start = jnp.arange(n_blocks, dtype=jnp.int32) * tile
    block_exp = jnp.minimum(jnp.sum((pends[None, :] <= block_start[:, None]).astype(jnp.int32), axis=1),
                            n_experts - 1).astype(jnp.int32)
    n_used = (pends[-1] // tile).astype(jnp.int32).reshape(1)
    of_block = lambda per_expert: jnp.sum(
        jnp.where(block_exp[:, None] == eids[None, :], per_expert[None, :], 0), axis=1)
    x_base = jnp.clip(of_block(base - pstarts) + block_start, 0, last_base).astype(jnp.int32)
    x_rows = jnp.clip(of_block(starts + counts) - x_base, 0, tile).astype(jnp.int32)
    return gates, dest.reshape(T, TOP_K), order // TOP_K, block_exp, n_used, x_base, x_rows, n_blocks


def kernel(x, ln_in_w, ln_in_b, w_in, rw_mu, rw_w0, rw_w2, rw_a0, rw_a2, rw_g2, rw_k_k, rw_k_a, rw_r_k,
           rw_gn_w, rw_gn_b, fx_b_f, fx_q_norm, fx_k_norm, w_o, ln1_w, ln1_b, router_w, router_b,
           exp_w1, exp_b1, exp_w2, exp_b2, ln2_w, ln2_b):
    B, S, D = x.shape
    T = B * S
    depth = w_in.shape[0]
    alpha = (2 * depth) ** 0.25
    rw_w = rw_w0.shape[1]
    fx_heads = fx_b_f.shape[1]
    fx_w = fx_heads * HEAD_DIM
    d_lora, a_lora, g_lora = rw_w2.shape[1], rw_a2.shape[1], rw_g2.shape[1]
    n_lora = d_lora + a_lora + g_lora
    lora_pad = -(-n_lora // 128) * 128
    rw_cols = 3 * rw_w + n_lora
    n_rw = 3 * rw_w + lora_pad
    n_experts = router_w.shape[2]
    ne_pad = -(-n_experts // 128) * 128
    row = lambda a: a.reshape(1, -1)

    seg_rw = _block_diag_ones(rw_w)
    segm_rw = _block_diag_ones(rw_w, 1.0 / HEAD_DIM)
    segm_fx = _block_diag_ones(fx_w, 1.0 / HEAD_DIM)
    tidx = jnp.arange(ROW_TILE)
    tril = (tidx[:, None] >= tidx[None, :]).astype(BF16)
    cidx = jnp.arange(WKV_CHUNK)
    tri_c = (cidx[:, None] >= cidx[None, :]).astype(F32)

    assert depth == 1, "single-layer block"
    l = 0
    x2 = x.reshape(T, D)
    w_l = w_in[l]
    wfz = w_l[:, rw_cols + 4 * fx_w:]
    wfz_hi = wfz.astype(BF16)
    wfz_lo = (wfz - wfz_hi.astype(F32)).astype(BF16)
    w_main = jnp.concatenate(
        [_pad_to(w_l[:, :rw_cols], n_rw, 1).astype(BF16), w_l[:, rw_cols:rw_cols + 4 * fx_w].astype(BF16),
         _pad_to(jnp.concatenate([wfz_hi, wfz_lo], axis=1), PAIR, 1)], axis=1)
    mu = _pad_to(row(rw_mu[l]), n_rw, 1)
    w2p = _pad_to(rw_w2[l], lora_pad, 0).astype(BF16)
    a2p = _pad_to(jnp.pad(rw_a2[l], ((d_lora, 0), (0, 0))), lora_pad, 0).astype(BF16)
    g2p = _pad_to(jnp.pad(rw_g2[l], ((d_lora + a_lora, 0), (0, 0))), lora_pad, 0).astype(BF16)
    rw_params = [mu, row(rw_w0[l]), w2p, row(rw_a0[l]), a2p, g2p,
                 row(rw_k_k[l]), row(rw_k_a[l]), row(rw_r_k[l]), seg_rw]
    qw = row(jnp.tile(fx_q_norm[l], fx_heads))
    kw = row(jnp.tile(fx_k_norm[l], fx_heads))
    fx_params = [_pad_to(row(fx_b_f[l]), PAIR, 1), qw, kw, segm_fx, tril]
    h0, r, lw, k2, v, kk, alr, g, bonus, qa, ka, va, og = _front(
        x2, B, S, row(ln_in_w), row(ln_in_b), w_main, _pad_to(wfz_hi, PAIR, 1), rw_params, fx_params,
        rw_w, fx_w, fx_heads)

    y_rw = _wkv(r, lw, k2, v, kk, alr, g, bonus, row(rw_gn_w[l]), row(rw_gn_b[l]), tri_c, segm_rw, B, S)
    y_fx = _fox_attn(qa, ka, va, og, B, S, fx_w)

    wo = w_o[l].astype(BF16)
    rw_pad = _pad_to(router_w[l], ne_pad, 1)
    rb_pad = _pad_to(row(router_b[l]), ne_pad, 1)
    h1, logits = _out_ln(y_rw, y_fx, h0, wo[:rw_w], wo[rw_w:], row(ln1_w[l]), row(ln1_b[l]),
                         rw_pad, rb_pad, alpha)

    gates, pos, gather_tok, block_exp, n_used, x_base, x_rows, n_blocks = _route(logits, n_experts, MOE_TILE)
    xs = h1[gather_tok]
    b1 = exp_b1[l]
    b1g = b1[:, None, 0::2]
    b1l = b1[:, None, 1::2]
    eo_rows = _moe_ffn(block_exp, n_used, x_base, x_rows, n_blocks, xs, exp_w1[l], b1g, b1l, exp_w2[l],
                       exp_b2[l][:, None, :])
    eo = eo_rows[pos.T]
    h = _combine_ln(h1, eo, gates, row(ln2_w[l]), row(ln2_b[l]), alpha)
    return h.reshape(B, S, D)
```

```python
import functools

import jax
import jax.numpy as jnp
from jax import lax
from jax.experimental import pallas as pl
from jax.experimental.pallas import tpu as pltpu

F32 = jnp.float32
BF16 = jnp.bfloat16

HEAD_DIM = 64
PAIR = 2 * HEAD_DIM
WKV_CHUNK = 64
RW_GN_EPS = 64e-5
QK_EPS = 1e-6
LN_EPS = 1e-5
TOP_K = 4
SWIGLU_ALPHA = 1.702
SWIGLU_LIMIT = 7.0
NEG_BIG = -1e30
LOG2E = 1.4426950408889634

ROW_TILE = 256
LN_ROW_TILE = 1024
ATTN_TQ = 512
ATTN_TK = 512
ATTN_HEADS = 8
MOE_TILE = 512
ROW_ALIGN = 8
WKV_BATCH = 4
VMEM_LIMIT = 48 * 1024 * 1024
FRONT_VMEM_LIMIT = 56 * 1024 * 1024


def _cparams(sem):
    return pltpu.CompilerParams(dimension_semantics=sem, vmem_limit_bytes=VMEM_LIMIT)


def _dot(a, b):
    return jnp.dot(a.astype(BF16), b.astype(BF16), preferred_element_type=F32)


def _dot_t(a, b):
    return lax.dot_general(a.astype(BF16), b.astype(BF16), (((1,), (1,)), ((), ())),
                           preferred_element_type=F32)


def _segsum(x, seg):
    return jnp.dot(x.astype(BF16), seg, preferred_element_type=F32)


def _split3(x):
    hi = x.astype(BF16).astype(F32)
    mid = (x - hi).astype(BF16).astype(F32)
    return hi, mid, x - hi - mid


def _sigmoid(x):
    return 1.0 / (1.0 + jnp.exp(-x))


def _softplus(x):
    return jnp.maximum(x, 0.0) + jnp.log(1.0 + jnp.exp(-jnp.abs(x)))


def _layer_norm(x, w, b):
    mu = jnp.mean(x, axis=-1, keepdims=True)
    xc = x - mu
    var = jnp.mean(xc * xc, axis=-1, keepdims=True)
    return xc * lax.rsqrt(var + LN_EPS) * w + b


def _pack_bf16_pairs(x):
    n = x.shape[1] // 2
    bits = pltpu.bitcast(x.astype(BF16).astype(F32), jnp.uint32)
    return lax.shift_right_logical(bits[:, :n], jnp.uint32(16)) | bits[:, n:]


def _unpack_bf16_pairs(u):
    lo = pltpu.bitcast(lax.shift_left(u, jnp.uint32(16)), F32)
    hi = pltpu.bitcast(u & jnp.uint32(0xFFFF0000), F32)
    return jnp.concatenate([lo, hi], axis=1)


def _stack_heads(x):
    lane = lax.broadcasted_iota(jnp.int32, x.shape, 1)
    return jnp.concatenate([jnp.where(lane < HEAD_DIM, x, 0.0), jnp.where(lane >= HEAD_DIM, x, 0.0)], axis=0)


def _wkv_kernel(r_ref, lw_ref, k_ref, v_ref, kk_ref, alr_ref, g_ref, bonus_ref, gnw_ref, gnb_ref, tri_ref,
                segm_ref, y_ref, state_ref):
    C = WKV_CHUNK
    nb, _, width = lw_ref.shape
    npair = width // PAIR

    @pl.when(pl.program_id(1) == 0)
    def _():
        state_ref[...] = jnp.zeros_like(state_ref)

    ri = lax.broadcasted_iota(jnp.int32, (2 * C, 2 * C), 0)
    ci = lax.broadcasted_iota(jnp.int32, (2 * C, 2 * C), 1)
    same = (ri // C) == (ci // C)
    strict = same & ((ci % C) < (ri % C))
    incl = same & ((ci % C) <= (ri % C))
    eye = (ri == ci).astype(F32)

    lhs, rhs, a2s, r2s, v2s, bhts, bkts, ptots = [], [], [], [], [], [], [], []
    for bi in range(nb):
        lw = lw_ref[bi]
        cum = sum(jnp.dot(tri_ref[...], piece.astype(BF16), preferred_element_type=F32) for piece in _split3(lw))
        total = cum[C - 1:C, :]
        p_inv = jnp.exp(-cum)
        p_rem = jnp.exp(total - cum)
        p_tot = jnp.exp(total)
        kk = kk_ref[bi]
        k2 = k_ref[bi]
        b = kk * alr_ref[bi]
        a_t = -kk * jnp.exp(cum - lw)
        r_t = r_ref[bi] * jnp.exp(cum)
        b_t = b * p_inv
        k_t = k2 * p_inv
        b_h = b * p_rem
        k_h = k2 * p_rem
        v = v_ref[bi]
        for p in range(npair):
            sl = slice(p * PAIR, (p + 1) * PAIR)
            a2, r2, b2, kt2 = (_stack_heads(t[:, sl]) for t in (a_t, r_t, b_t, k_t))
            bh2, kh2, v2 = (_stack_heads(t[:, sl]) for t in (b_h, k_h, v))
            lhs.append(jnp.concatenate([a2, r2], axis=0))
            rhs.append(jnp.concatenate([b2, kt2], axis=0))
            a2s.append(a2)
            r2s.append(r2)
            v2s.append(v2)
            bhts.append(bh2.T)
            bkts.append(jnp.concatenate([bh2.T, kh2.T], axis=1))
            ptots.append(p_tot[:, sl])

    chains = range(nb * npair)
    m = [_dot_t(lhs[c], rhs[c]) for c in chains]
    n_ab = [jnp.where(strict, m[c][:2 * C, :2 * C], 0.0) for c in chains]
    m_ak = [jnp.where(strict, m[c][:2 * C, 2 * C:], 0.0) for c in chains]
    m_rb = [jnp.where(incl, m[c][2 * C:, :2 * C], 0.0) for c in chains]
    m_rk = [jnp.where(incl, m[c][2 * C:, 2 * C:], 0.0) for c in chains]
    mv = [_dot(m_ak[c], v2s[c]) for c in chains]
    mrkv = [_dot(m_rk[c], v2s[c]) for c in chains]
    inv = [eye + n_ab[c] for c in chains]
    pw = n_ab
    for _ in range(C.bit_length() - 2):
        pw = [_dot(pw[c], pw[c]) for c in chains]
        inv = [inv[c] + _dot(inv[c], pw[c]) for c in chains]
    wu = [_dot(inv[c], jnp.concatenate([a2s[c], mv[c]], axis=1)) for c in chains]
    qy = [_dot(m_rb[c], wu[c]) + jnp.concatenate([r2s[c], mrkv[c]], axis=1) for c in chains]
    g_t = [_dot(bhts[c], wu[c][:, :PAIR]) + eye * ptots[c] for c in chains]
    h_t = [_dot(bkts[c], jnp.concatenate([wu[c][:, PAIR:], v2s[c]], axis=0)) for c in chains]
    s0 = [state_ref[c] for c in chains]
    y2 = [_dot(qy[c][:, :PAIR], s0[c]) + qy[c][:, PAIR:] for c in chains]
    for c in chains:
        state_ref[c] = _dot(g_t[c], s0[c]) + h_t[c]

    segm = segm_ref[...]
    for bi in range(nb):
        y = jnp.concatenate([y2[bi * npair + p][:C] + y2[bi * npair + p][C:] for p in range(npair)], axis=1)
        mean = _segsum(y, segm)
        yc = y - mean
        var = _segsum(yc * yc, segm)
        yn = yc * lax.rsqrt(var + RW_GN_EPS) * gnw_ref[...] + gnb_ref[...]
        y_ref[bi] = ((yn + bonus_ref[bi]) * g_ref[bi]).astype(y_ref.dtype)


def _wkv(r, lw, k2, v, kk, alr, g, bonus, gnw, gnb, tri, segm, B, S):
    T, width = r.shape
    C = WKV_CHUNK
    nb = WKV_BATCH
    nc = S // C
    fixed = lambda b, c: (0, 0)
    blk = pl.BlockSpec((nb, C, width), lambda b, c: (b, c, 0))
    vec = pl.BlockSpec((1, width), fixed)
    ins = [t.reshape(B, S, width) for t in (r, lw, k2, v, kk, alr, g, bonus)]
    y = pl.pallas_call(
        _wkv_kernel,
        grid=(B // nb, nc),
        in_specs=[blk] * 8 + [vec, vec, pl.BlockSpec((C, C), fixed), pl.BlockSpec((width, width), fixed)],
        out_specs=blk,
        out_shape=jax.ShapeDtypeStruct((B, S, width), BF16),
        scratch_shapes=[pltpu.VMEM((nb * width // PAIR, PAIR, PAIR), F32)],
        compiler_params=_cparams(("parallel", "arbitrary")),
        name="wkv",
    )(*ins, gnw, gnb, tri, segm)
    return y.reshape(T, width)


def _pieces(x, n):
    hi, mid, lo = _split3(x)
    return hi + pltpu.roll(mid, n, axis=1) + pltpu.roll(lo, 2 * n, axis=1)


def _spread_heads(x):
    lane = lax.broadcasted_iota(jnp.int32, (x.shape[0], PAIR), 1)
    groups = []
    for p in range(x.shape[1] // PAIR):
        blk = x[:, p * PAIR:(p + 1) * PAIR]
        groups.append(jnp.where(lane < HEAD_DIM, blk, 0.0))
        groups.append(jnp.where(lane < HEAD_DIM, pltpu.roll(blk, HEAD_DIM, axis=1), 0.0))
    return jnp.concatenate(groups, axis=1)


def _front_kernel(x_ref, lnw_ref, lnb_ref, w_ref, wfz_ref,
                  mu_ref, w0_ref, w2_ref, a0_ref, a2_ref, g2_ref, kkw_ref, kaw_ref, rk_ref, seg_ref,
                  bf_ref, qw_ref, kw_ref, segm_ref, tril_ref, place_ref, maskq_ref, maskk_ref,
                  oneq_ref, onek_ref, onev_ref,
                  h_ref, r_ref, lw_ref, k_ref, v_ref, kk_ref, alr_ref, g_ref, bonus_ref,
                  qa_ref, ka_ref, va_ref, og_ref, shift_carry, c_carry, *, width, nh):
    @pl.when(pl.program_id(1) == 0)
    def _():
        shift_carry[...] = jnp.zeros_like(shift_carry)
        c_carry[...] = jnp.zeros_like(c_carry)

    h = _layer_norm(x_ref[...], lnw_ref[...], lnb_ref[...])
    h_ref[...] = h
    proj = jnp.dot(h.astype(BF16), w_ref[...], preferred_element_type=F32)
    tm = proj.shape[0]
    n_rw = mu_ref.shape[1]

    p = proj[:, :n_rw]
    prev = pltpu.roll(p, 1, axis=0)
    first_row = lax.broadcasted_iota(jnp.int32, p.shape, 0) == 0
    prev = jnp.where(first_row, shift_carry[...], prev)
    shift_carry[...] = p[tm - 1:tm, :]
    ps = p + mu_ref[...] * (prev - p)
    r = ps[:, 0:width]
    k = ps[:, width:2 * width]
    v = ps[:, 2 * width:3 * width]
    lora = ps[:, 3 * width:]
    seg = seg_ref[...]
    wl = w0_ref[...] + _dot(jnp.tanh(lora), w2_ref[...])
    w_raw = -_softplus(-wl) - 0.5
    lw_ref[...] = -jnp.exp(w_raw)
    alr = _sigmoid(a0_ref[...] + _dot(lora, a2_ref[...]))
    g_ref[...] = _dot(_sigmoid(lora), g2_ref[...])
    kkp = k * kkw_ref[...]
    nrm = jnp.sqrt(_segsum(kkp * kkp, seg))
    kk_ref[...] = kkp / jnp.maximum(nrm, 1e-12)
    k2 = k * (1.0 + (alr - 1.0) * kaw_ref[...])
    bonus_ref[...] = _segsum(r * k2 * rk_ref[...], seg) * v
    r_ref[...] = r
    k_ref[...] = k2
    v_ref[...] = v
    alr_ref[...] = alr

    fw = qw_ref.shape[1]
    q = proj[:, n_rw:n_rw + fw]
    kx = proj[:, n_rw + fw:n_rw + 2 * fw]
    vx = proj[:, n_rw + 2 * fw:n_rw + 3 * fw]
    og_ref[...] = proj[:, n_rw + 3 * fw:n_rw + 4 * fw]
    segm = segm_ref[...]
    qn = q * lax.rsqrt(_segsum(q * q, segm) + QK_EPS) * (qw_ref[...] * (HEAD_DIM ** -0.5 * LOG2E))
    kn = kx * lax.rsqrt(_segsum(kx * kx, segm) + QK_EPS) * kw_ref[...]
    fblk = proj[:, n_rw + 4 * fw:]
    h_lo = (h - h.astype(BF16).astype(F32)).astype(BF16)
    fz = fblk + pltpu.roll(fblk, PAIR - nh, axis=1) + jnp.dot(h_lo, wfz_ref[...], preferred_element_type=F32)
    in_heads = lax.broadcasted_iota(jnp.int32, fz.shape, 1) < nh
    lf = jnp.where(in_heads, -_softplus(-(fz + bf_ref[...])), 0.0)
    c3 = jnp.dot(tril_ref[...], _pieces(lf, nh).astype(BF16), preferred_element_type=F32)
    c = c3 + pltpu.roll(c3, PAIR - nh, axis=1) + pltpu.roll(c3, PAIR - 2 * nh, axis=1)
    c = jnp.where(in_heads, c, 0.0) + c_carry[...]
    c_carry[...] = c[tm - 1:tm, :]
    placed = jnp.dot(_pieces(c * LOG2E, nh).astype(BF16), place_ref[...], preferred_element_type=F32)
    qa_ref[...] = (_spread_heads(qn) + placed * maskq_ref[...] + oneq_ref[...]).astype(qa_ref.dtype)
    ka_ref[...] = (_spread_heads(kn) - placed * maskk_ref[...] + onek_ref[...]).astype(ka_ref.dtype)
    va_ref[...] = (_spread_heads(vx) + onev_ref[...]).astype(va_ref.dtype)


def _front(x2, B, S, lnw, lnb, w_main, wfz_hi, rw_params, fx_params, rw_w, fx_w, nh):
    T, D = x2.shape
    wide = nh * PAIR
    tm = ROW_TILE
    ns = S // tm
    row = lambda b, s: (b * ns + s, 0)
    fixed = lambda b, s: (0, 0)
    const = lambda a: pl.BlockSpec(a.shape, fixed)

    lane = jnp.arange(wide) % PAIR
    head = jnp.arange(wide) // PAIR
    src = jnp.arange(PAIR)
    piece = (lane - HEAD_DIM) % 3
    place = ((src[:, None] == (piece * nh + head)[None, :]) & (src[:, None] < 3 * nh)
             & (lane >= HEAD_DIM)[None, :] & (lane < HEAD_DIM + 6)[None, :]).astype(BF16)
    mask_k = ((lane >= HEAD_DIM) & (lane < HEAD_DIM + 3)).astype(F32).reshape(1, wide)
    mask_q = ((lane >= HEAD_DIM + 3) & (lane < HEAD_DIM + 6)).astype(F32).reshape(1, wide)
    one_v = (lane >= HEAD_DIM).astype(F32).reshape(1, wide)
    consts = [lnw, lnb, w_main, wfz_hi, *rw_params, *fx_params, place, mask_q, mask_k, mask_k, mask_q, one_v]

    f32_out = lambda n: (pl.BlockSpec((tm, n), row), jax.ShapeDtypeStruct((T, n), F32))
    bf_out = lambda n: (pl.BlockSpec((tm, n), row), jax.ShapeDtypeStruct((T, n), BF16))
    outs = [f32_out(D)] + [f32_out(rw_w)] * 8 + [bf_out(wide)] * 3 + [f32_out(fx_w)]
    return pl.pallas_call(
        functools.partial(_front_kernel, width=rw_w, nh=nh),
        grid=(B, ns),
        in_specs=[pl.BlockSpec((tm, D), row)] + [const(a) for a in consts],
        out_specs=[o[0] for o in outs],
        out_shape=[o[1] for o in outs],
        scratch_shapes=[pltpu.VMEM((1, rw_params[0].shape[1]), F32), pltpu.VMEM((1, PAIR), F32)],
        compiler_params=pltpu.CompilerParams(dimension_semantics=("parallel", "arbitrary"),
                                             vmem_limit_bytes=FRONT_VMEM_LIMIT),
        name="front",
    )(x2, *consts)


def _fox_attn_kernel(qi_ref, kj_ref, q_ref, k_ref, v_ref, og_ref, o_ref, m_ref, acc_ref):
    i = qi_ref[pl.program_id(2)]
    j = kj_ref[pl.program_id(2)]
    tq = q_ref.shape[0]
    tk = k_ref.shape[0]
    heads = range(ATTN_HEADS)
    grp = lambda ref, h: ref[:, h * PAIR:(h + 1) * PAIR]

    @pl.when(j == 0)
    def _():
        m_ref[...] = jnp.full_like(m_ref, NEG_BIG)
        acc_ref[...] = jnp.zeros_like(acc_ref)

    def step(masked):
        s = [lax.dot_general(grp(q_ref, h), grp(k_ref, h), (((1,), (1,)), ((), ())), preferred_element_type=F32)
             for h in heads]
        if masked:
            row = lax.broadcasted_iota(jnp.int32, (tq, tk), 0)
            col = lax.broadcasted_iota(jnp.int32, (tq, tk), 1)
            s = [jnp.where(col <= row, s[h], NEG_BIG) for h in heads]
        m_old = [m_ref[h] for h in heads]
        m_new = [jnp.maximum(m_old[h], jnp.max(s[h], axis=1, keepdims=True)) for h in heads]
        alpha = [jnp.exp2(m_old[h] - m_new[h]) for h in heads]
        pr = [jnp.exp2(s[h] - jnp.concatenate([m_new[h]] * (tk // PAIR), axis=1)).astype(BF16) for h in heads]
        pv = [jnp.dot(pr[h], grp(v_ref, h), preferred_element_type=F32) for h in heads]
        for h in heads:
            acc_ref[h] = alpha[h] * acc_ref[h] + pv[h]
            m_ref[h] = m_new[h]

    @pl.when(j < i)
    def _():
        step(False)

    @pl.when(j == i)
    def _():
        step(True)
        lane = lax.broadcasted_iota(jnp.int32, (tq, PAIR), 1)
        outs = []
        for p in range(ATTN_HEADS // 2):
            a0 = acc_ref[2 * p]
            a1 = acc_ref[2 * p + 1]
            o0 = a0 * pltpu.roll(1.0 / a0, HEAD_DIM, axis=1)
            o1 = pltpu.roll(a1, HEAD_DIM, axis=1) * (1.0 / a1)
            outs.append(jnp.where(lane < HEAD_DIM, o0, o1))
        o = jnp.concatenate(outs, axis=1)
        o_ref[...] = (o * _sigmoid(og_ref[...])).astype(o_ref.dtype)


def _fox_attn(qa, ka, va, og, B, S, width):
    T = qa.shape[0]
    tq, tk = ATTN_TQ, ATTN_TK
    assert tq == tk
    nq = S // tq
    nh = ATTN_HEADS
    ow = nh * HEAD_DIM
    ngroup = width // ow
    pairs = [(i, j) for i in range(nq) for j in range(i + 1)]
    qi = jnp.array([ij[0] for ij in pairs], jnp.int32)
    kj = jnp.array([ij[1] for ij in pairs], jnp.int32)
    kv = pl.BlockSpec((tk, nh * PAIR), lambda b, p, t, qi, kj: (b * nq + kj[t], p))
    grid_spec = pltpu.PrefetchScalarGridSpec(
        num_scalar_prefetch=2,
        grid=(B, ngroup, len(pairs)),
        in_specs=[
            pl.BlockSpec((tq, nh * PAIR), lambda b, p, t, qi, kj: (b * nq + qi[t], p)), kv, kv,
            pl.BlockSpec((tq, ow), lambda b, p, t, qi, kj: (b * nq + qi[t], p)),
        ],
        out_specs=pl.BlockSpec((tq, ow), lambda b, p, t, qi, kj: (b * nq + qi[t], p)),
        scratch_shapes=[pltpu.VMEM((nh, tq, PAIR), F32), pltpu.VMEM((nh, tq, PAIR), F32)],
    )
    return pl.pallas_call(
        _fox_attn_kernel,
        grid_spec=grid_spec,
        out_shape=jax.ShapeDtypeStruct((T, width), BF16),
        compiler_params=_cparams(("parallel", "parallel", "arbitrary")),
        name="fox_attn",
    )(qi, kj, qa, ka, va, og)


def _out_ln_kernel(yrw_ref, yfx_ref, h_ref, wo1_ref, wo2_ref, lnw_ref, lnb_ref, rw_ref, rb_ref,
                   h1_ref, logit_ref, *, alpha):
    mix = (jnp.dot(yrw_ref[...], wo1_ref[...], preferred_element_type=F32)
           + jnp.dot(yfx_ref[...], wo2_ref[...], preferred_element_type=F32))
    h1 = _layer_norm(alpha * h_ref[...] + mix, lnw_ref[...], lnb_ref[...])
    h1_ref[...] = h1
    ne = rb_ref.shape[1]
    h_hi = h1.astype(BF16)
    h_lo = (h1 - h_hi.astype(F32)).astype(BF16)
    rw = rw_ref[...]
    first = jnp.dot(h_hi, rw, preferred_element_type=F32)
    second = jnp.dot(h_lo, rw[:, :ne], preferred_element_type=F32)
    logit_ref[...] = first[:, :ne] + first[:, ne:] + second + rb_ref[...]


def _out_ln(y_rw, y_fx, h0, wo1, wo2, lnw, lnb, rw, rb, alpha):
    T, D = h0.shape
    width = y_rw.shape[1]
    ne = rb.shape[1]
    tm = LN_ROW_TILE
    row = lambda i: (i, 0)
    fixed = lambda i: (0, 0)
    rw_hi = rw.astype(BF16)
    rw_lo = (rw - rw_hi.astype(F32)).astype(BF16)
    rw = jnp.concatenate([rw_hi, rw_lo], axis=1)
    return pl.pallas_call(
        functools.partial(_out_ln_kernel, alpha=alpha),
        grid=(T // tm,),
        in_specs=[
            pl.BlockSpec((tm, width), row), pl.BlockSpec((tm, width), row), pl.BlockSpec((tm, D), row),
            pl.BlockSpec((width, D), fixed), pl.BlockSpec((width, D), fixed),
            pl.BlockSpec((1, D), fixed), pl.BlockSpec((1, D), fixed),
            pl.BlockSpec((D, 2 * ne), fixed), pl.BlockSpec((1, ne), fixed),
        ],
        out_specs=[pl.BlockSpec((tm, D), row), pl.BlockSpec((tm, ne), row)],
        out_shape=[jax.ShapeDtypeStruct((T, D), F32), jax.ShapeDtypeStruct((T, ne), F32)],
        compiler_params=_cparams(("parallel",)),
        name="out_ln",
    )(y_rw, y_fx, h0, wo1, wo2, lnw, lnb, rw, rb)


DEINT_COLS = 256
MOE_VMEM_LIMIT = 56 * 1024 * 1024


def _moe_kernel(bexp_ref, nused_ref, first_ref, nexte_ref, slot_ref, xbase_ref, xrows_ref,
                x_hbm, w1_hbm, w2_hbm, b1g_ref, b1l_ref, b2_ref, perm_ref, o_ref,
                w1_stage, w2_stage, w1g_bf, w1l_bf, w2_bf, x_buf, sem, sem_x):
    i = pl.program_id(0)
    live = i < nused_ref[0]
    tm = x_buf.shape[1]

    def x_copy(blk, s):
        row0 = pl.multiple_of(xbase_ref[blk], ROW_ALIGN)
        return pltpu.make_async_copy(x_hbm.at[pl.ds(row0, tm), :], x_buf.at[s], sem_x.at[s])

    @pl.when(live & (i == 0))
    def _():
        x_copy(0, 0).start()

    def weight_copies(e, s):
        return (pltpu.make_async_copy(w1_hbm.at[e], w1_stage.at[s], sem.at[0, s]),
                pltpu.make_async_copy(w2_hbm.at[e], w2_stage.at[s], sem.at[1, s]))

    @pl.when(live & (i == 0))
    def _():
        for cp in weight_copies(bexp_ref[0], 0):
            cp.start()

    @pl.when(live & (first_ref[i] == 1))
    def _():
        s = slot_ref[i]
        for cp in weight_copies(bexp_ref[i], s):
            cp.wait()

        @pl.when(nexte_ref[i] >= 0)
        def _():
            for cp in weight_copies(nexte_ref[i], 1 - s):
                cp.start()

        half = DEINT_COLS // 2
        for c in range(w1_stage.shape[2] // DEINT_COLS):
            blk = w1_stage[s, :, c * DEINT_COLS:(c + 1) * DEINT_COLS].astype(BF16)
            out = jnp.dot(blk, perm_ref[...], preferred_element_type=F32).astype(BF16)
            w1g_bf[:, c * half:(c + 1) * half] = out[:, :half]
            w1l_bf[:, c * half:(c + 1) * half] = out[:, half:]
        w2_bf[...] = w2_stage[s].astype(BF16)

    @pl.when(live)
    def _():
        xs = i % 2
        x_copy(i, xs).wait()

        @pl.when(i + 1 < nused_ref[0])
        def _():
            x_copy(i + 1, 1 - xs).start()

        def ffn(x):
            x = x.astype(BF16)
            x_glu = jnp.minimum(jnp.dot(x, w1g_bf[...], preferred_element_type=F32) + b1g_ref[0], SWIGLU_LIMIT)
            x_lin = jnp.clip(jnp.dot(x, w1l_bf[...], preferred_element_type=F32) + b1l_ref[0],
                             -SWIGLU_LIMIT, SWIGLU_LIMIT)
            act = x_glu * _sigmoid(SWIGLU_ALPHA * x_glu) * (x_lin + 1.0)
            return _pack_bf16_pairs(jnp.dot(act.astype(BF16), w2_bf[...], preferred_element_type=F32) + b2_ref[0])

        half = tm // 2
        short = xrows_ref[i] <= half

        @pl.when(short)
        def _():
            o_ref[:half, :] = ffn(x_buf[xs, :half, :])
            o_ref[half:, :] = jnp.zeros((tm - half, o_ref.shape[1]), o_ref.dtype)

        @pl.when(jnp.logical_not(short))
        def _():
            o_ref[...] = ffn(x_buf[xs])

    @pl.when(jnp.logical_not(live))
    def _():
        o_ref[...] = jnp.zeros_like(o_ref)


def _moe_ffn(block_exp, n_used, x_base, x_rows, n_blocks, xs, w1, b1g, b1l, w2, b2):
    E, D, F2 = w1.shape
    F = F2 // 2
    tm = MOE_TILE
    n_rows = n_blocks * tm

    idx = jnp.arange(n_blocks, dtype=jnp.int32)
    first = jnp.concatenate([jnp.ones((1,), jnp.bool_), block_exp[1:] != block_exp[:-1]])
    slot = ((jnp.cumsum(first.astype(jnp.int32)) - 1) % 2).astype(jnp.int32)
    cand = jnp.where(first & (idx < n_used[0]), idx, n_blocks)
    next_first = jnp.min(jnp.where(idx[None, :] > idx[:, None], cand[None, :], n_blocks), axis=1)
    next_e = jnp.where(next_first < n_blocks, block_exp[jnp.minimum(next_first, n_blocks - 1)], -1).astype(jnp.int32)

    half = DEINT_COLS // 2
    src = jnp.arange(DEINT_COLS)
    dst = jnp.where(src % 2 == 0, src // 2, half + src // 2)
    perm = (dst[:, None] == jnp.arange(DEINT_COLS)[None, :]).astype(BF16)

    live = lambda i, be, nu: jnp.minimum(i, nu[0] - 1)
    bspec = lambda n: pl.BlockSpec((1, 1, n), lambda i, be, nu, fi, ne, sl, xb, xr: (be[live(i, be, nu)], 0, 0))
    grid_spec = pltpu.PrefetchScalarGridSpec(
        num_scalar_prefetch=7,
        grid=(n_blocks,),
        in_specs=[
            pl.BlockSpec(memory_space=pl.ANY),
            pl.BlockSpec(memory_space=pl.ANY),
            pl.BlockSpec(memory_space=pl.ANY),
            bspec(F), bspec(F), bspec(D),
            pl.BlockSpec((DEINT_COLS, DEINT_COLS), lambda i, be, nu, fi, ne, sl, xb, xr: (0, 0)),
        ],
        out_specs=pl.BlockSpec((tm, D // 2), lambda i, be, nu, fi, ne, sl, xb, xr: (i, 0)),
        scratch_shapes=[
            pltpu.VMEM((2, D, F2), F32), pltpu.VMEM((2, F, D), F32),
            pltpu.VMEM((D, F), BF16), pltpu.VMEM((D, F), BF16), pltpu.VMEM((F, D), BF16),
            pltpu.VMEM((2, tm, D), F32),
            pltpu.SemaphoreType.DMA((2, 2)), pltpu.SemaphoreType.DMA((2,)),
        ],
    )
    return pl.pallas_call(
        _moe_kernel,
        grid_spec=grid_spec,
        out_shape=jax.ShapeDtypeStruct((n_rows, D // 2), jnp.uint32),
        compiler_params=pltpu.CompilerParams(dimension_semantics=("arbitrary",), vmem_limit_bytes=MOE_VMEM_LIMIT),
        name="moe_ffn",
    )(block_exp, n_used, first.astype(jnp.int32), next_e, slot, x_base, x_rows, xs, w1, w2, b1g, b1l, b2, perm)


def _combine_ln_kernel(h_ref, eo_ref, gate_ref, lnw_ref, lnb_ref, o_ref, *, alpha):
    gates = gate_ref[...]
    ffn = _unpack_bf16_pairs(eo_ref[0]) * gates[:, 0:1]
    for j in range(1, TOP_K):
        ffn = ffn + _unpack_bf16_pairs(eo_ref[j]) * gates[:, j:j + 1]
    o_ref[...] = _layer_norm(alpha * h_ref[...] + ffn, lnw_ref[...], lnb_ref[...])


def _combine_ln(h1, eo, gates, lnw, lnb, alpha):
    T, D = h1.shape
    tm = LN_ROW_TILE
    return pl.pallas_call(
        functools.partial(_combine_ln_kernel, alpha=alpha),
        grid=(T // tm,),
        in_specs=[
            pl.BlockSpec((tm, D), lambda i: (i, 0)),
            pl.BlockSpec((TOP_K, tm, D // 2), lambda i: (0, i, 0)),
            pl.BlockSpec((tm, TOP_K), lambda i: (i, 0)),
            pl.BlockSpec((1, D), lambda i: (0, 0)),
            pl.BlockSpec((1, D), lambda i: (0, 0)),
        ],
        out_specs=pl.BlockSpec((tm, D), lambda i: (i, 0)),
        out_shape=jax.ShapeDtypeStruct((T, D), F32),
        compiler_params=_cparams(("parallel",)),
        name="combine_ln",
    )(h1, eo, gates, lnw, lnb)


def _pad_to(x, n, axis):
    pad = [(0, 0)] * x.ndim
    pad[axis] = (0, n - x.shape[axis])
    return jnp.pad(x, pad)


def _block_diag_ones(width, value=1.0):
    idx = jnp.arange(width) // HEAD_DIM
    return jnp.where(idx[:, None] == idx[None, :], value, 0.0).astype(BF16)


def _route(logits, n_experts, tile):
    T = logits.shape[0]
    top_val, top_idx = lax.top_k(logits[:, :n_experts], TOP_K)
    gates = jax.nn.softmax(top_val, axis=-1)
    e_flat = top_idx.reshape(-1).astype(jnp.int32)
    n_assign = T * TOP_K
    assert n_assign % tile == 0 and tile % ROW_ALIGN == 0 and n_experts * n_assign < 2 ** 31
    n_blocks = n_assign // tile + n_experts + 1
    last_base = n_assign - tile
    eids = jnp.arange(n_experts, dtype=jnp.int32)
    aids = jnp.arange(n_assign, dtype=jnp.int32)
    skeys = lax.sort(e_flat * n_assign + aids)
    order = skeys % n_assign
    e_sorted = skeys // n_assign
    counts = jnp.sum((e_flat[:, None] == eids[None, :]).astype(jnp.int32), axis=0)
    starts = jnp.cumsum(counts) - counts
    base = starts // ROW_ALIGN * ROW_ALIGN
    padded = (starts - base + counts + tile - 1) // tile * tile
    pends = jnp.cumsum(padded)
    pstarts = pends - padded
    of_sorted = lambda per_expert: jnp.sum(
        jnp.where(e_sorted[:, None] == eids[None, :], per_expert[None, :], 0), axis=1)
    base_q = of_sorted(base)
    k_q = (aids - base_q) // tile
    window_q = jnp.minimum(base_q + k_q * tile, last_base)
    dest_sorted = of_sorted(pstarts) + k_q * tile + (aids - window_q)
    _, dest = lax.sort((order, dest_sorted), num_keys=1)
    block_start = jnp.arange(n_blocks, dtype=jnp.int32) * tile
    block_exp = jnp.minimum(jnp.sum((pends[None, :] <= block_start[:, None]).astype(jnp.int32), axis=1),
                            n_experts - 1).astype(jnp.int32)
    n_used = (pends[-1] // tile).astype(jnp.int32).reshape(1)
    of_block = lambda per_expert: jnp.sum(
        jnp.where(block_exp[:, None] == eids[None, :], per_expert[None, :], 0), axis=1)
    x_base = jnp.clip(of_block(base - pstarts) + block_start, 0, last_base).astype(jnp.int32)
    x_rows = jnp.clip(of_block(starts + counts) - x_base, 0, tile).astype(jnp.int32)
    return gates, dest.reshape(T, TOP_K), order // TOP_K, block_exp, n_used, x_base, x_rows, n_blocks


def kernel(x, ln_in_w, ln_in_b, w_in, rw_mu, rw_w0, rw_w2, rw_a0, rw_a2, rw_g2, rw_k_k, rw_k_a, rw_r_k,
           rw_gn_w, rw_gn_b, fx_b_f, fx_q_norm, fx_k_norm, w_o, ln1_w, ln1_b, router_w, router_b,
           exp_w1, exp_b1, exp_w2, exp_b2, ln2_w, ln2_b):
    B, S, D = x.shape
    T = B * S
    depth = w_in.shape[0]
    alpha = (2 * depth) ** 0.25
    rw_w = rw_w0.shape[1]
    fx_heads = fx_b_f.shape[1]
    fx_w = fx_heads * HEAD_DIM
    d_lora, a_lora, g_lora = rw_w2.shape[1], rw_a2.shape[1], rw_g2.shape[1]
    n_lora = d_lora + a_lora + g_lora
    lora_pad = -(-n_lora // 128) * 128
    rw_cols = 3 * rw_w + n_lora
    n_rw = 3 * rw_w + lora_pad
    n_experts = router_w.shape[2]
    ne_pad = -(-n_experts // 128) * 128
    row = lambda a: a.reshape(1, -1)

    seg_rw = _block_diag_ones(rw_w)
    segm_rw = _block_diag_ones(rw_w, 1.0 / HEAD_DIM)
    segm_fx = _block_diag_ones(fx_w, 1.0 / HEAD_DIM)
    tidx = jnp.arange(ROW_TILE)
    tril = (tidx[:, None] >= tidx[None, :]).astype(BF16)
    cidx = jnp.arange(WKV_CHUNK)
    tri_c = (cidx[:, None] >= cidx[None, :]).astype(BF16)

    assert depth == 1, "single-layer block"
    l = 0
    x2 = x.reshape(T, D)
    w_l = w_in[l]
    wfz = w_l[:, rw_cols + 4 * fx_w:]
    wfz_hi = wfz.astype(BF16)
    wfz_lo = (wfz - wfz_hi.astype(F32)).astype(BF16)
    w_main = jnp.concatenate(
        [_pad_to(w_l[:, :rw_cols], n_rw, 1).astype(BF16), w_l[:, rw_cols:rw_cols + 4 * fx_w].astype(BF16),
         _pad_to(jnp.concatenate([wfz_hi, wfz_lo], axis=1), PAIR, 1)], axis=1)
    mu = _pad_to(row(rw_mu[l]), n_rw, 1)
    w2p = _pad_to(rw_w2[l], lora_pad, 0).astype(BF16)
    a2p = _pad_to(jnp.pad(rw_a2[l], ((d_lora, 0), (0, 0))), lora_pad, 0).astype(BF16)
    g2p = _pad_to(jnp.pad(rw_g2[l], ((d_lora + a_lora, 0), (0, 0))), lora_pad, 0).astype(BF16)
    rw_params = [mu, row(rw_w0[l]), w2p, row(rw_a0[l]), a2p, g2p,
                 row(rw_k_k[l]), row(rw_k_a[l]), row(rw_r_k[l]), seg_rw]
    qw = row(jnp.tile(fx_q_norm[l], fx_heads))
    kw = row(jnp.tile(fx_k_norm[l], fx_heads))
    fx_params = [_pad_to(row(fx_b_f[l]), PAIR, 1), qw, kw, segm_fx, tril]
    h0, r, lw, k2, v, kk, alr, g, bonus, qa, ka, va, og = _front(
        x2, B, S, row(ln_in_w), row(ln_in_b), w_main, _pad_to(wfz_hi, PAIR, 1), rw_params, fx_params,
        rw_w, fx_w, fx_heads)

    y_rw = _wkv(r, lw, k2, v, kk, alr, g, bonus, row(rw_gn_w[l]), row(rw_gn_b[l]), tri_c, segm_rw, B, S)
    y_fx = _fox_attn(qa, ka, va, og, B, S, fx_w)

    wo = w_o[l].astype(BF16)
    rw_pad = _pad_to(router_w[l], ne_pad, 1)
    rb_pad = _pad_to(row(router_b[l]), ne_pad, 1)
    h1, logits = _out_ln(y_rw, y_fx, h0, wo[:rw_w], wo[rw_w:], row(ln1_w[l]), row(ln1_b[l]),
                         rw_pad, rb_pad, alpha)

    gates, pos, gather_tok, block_exp, n_used, x_base, x_rows, n_blocks = _route(logits, n_experts, MOE_TILE)
    xs = h1[gather_tok]
    b1 = exp_b1[l]
    b1g = b1[:, None, 0::2]
    b1l = b1[:, None, 1::2]
    eo_rows = _moe_ffn(block_exp, n_used, x_base, x_rows, n_blocks, xs, exp_w1[l], b1g, b1l, exp_w2[l],
                       exp_b2[l][:, None, :])
    eo = eo_rows[pos.T]
    h = _combine_ln(h1, eo, gates, row(ln2_w[l]), row(ln2_b[l]), alpha)
    return h.reshape(B, S, D)
```

```python
import functools

import jax
import jax.numpy as jnp
from jax import lax
from jax.experimental import pallas as pl
from jax.experimental.pallas import tpu as pltpu

F32 = jnp.float32
BF16 = jnp.bfloat16

HEAD_DIM = 64
PAIR = 2 * HEAD_DIM
WKV_CHUNK = 64
RW_GN_EPS = 64e-5
QK_EPS = 1e-6
LN_EPS = 1e-5
TOP_K = 4
SWIGLU_ALPHA = 1.702
SWIGLU_LIMIT = 7.0
NEG_BIG = -1e30
LOG2E = 1.4426950408889634

ROW_TILE = 256
LN_ROW_TILE = 1024
ATTN_TQ = 512
ATTN_TK = 512
ATTN_HEADS = 8
MOE_TILE = 512
ROW_ALIGN = 8
WKV_BATCH = 4
VMEM_LIMIT = 48 * 1024 * 1024
FRONT_VMEM_LIMIT = 56 * 1024 * 1024


def _cparams(sem):
    return pltpu.CompilerParams(dimension_semantics=sem, vmem_limit_bytes=VMEM_LIMIT)


def _dot(a, b):
    return jnp.dot(a.astype(BF16), b.astype(BF16), preferred_element_type=F32)


def _dot_t(a, b):
    return lax.dot_general(a.astype(BF16), b.astype(BF16), (((1,), (1,)), ((), ())),
                           preferred_element_type=F32)


def _segsum(x, seg):
    return jnp.dot(x.astype(BF16), seg, preferred_element_type=F32)


def _split3(x):
    hi = x.astype(BF16).astype(F32)
    mid = (x - hi).astype(BF16).astype(F32)
    return hi, mid, x - hi - mid


def _sigmoid(x):
    return 1.0 / (1.0 + jnp.exp(-x))


def _softplus(x):
    return jnp.maximum(x, 0.0) + jnp.log(1.0 + jnp.exp(-jnp.abs(x)))


def _layer_norm(x, w, b):
    mu = jnp.mean(x, axis=-1, keepdims=True)
    xc = x - mu
    var = jnp.mean(xc * xc, axis=-1, keepdims=True)
    return xc * lax.rsqrt(var + LN_EPS) * w + b


def _pack_bf16_pairs(x):
    n = x.shape[1] // 2
    bits = pltpu.bitcast(x.astype(BF16).astype(F32), jnp.uint32)
    return lax.shift_right_logical(bits[:, :n], jnp.uint32(16)) | bits[:, n:]


def _unpack_bf16_pairs(u):
    lo = pltpu.bitcast(lax.shift_left(u, jnp.uint32(16)), F32)
    hi = pltpu.bitcast(u & jnp.uint32(0xFFFF0000), F32)
    return jnp.concatenate([lo, hi], axis=1)


def _stack_heads(x):
    lane = lax.broadcasted_iota(jnp.int32, x.shape, 1)
    return jnp.concatenate([jnp.where(lane < HEAD_DIM, x, 0.0), jnp.where(lane >= HEAD_DIM, x, 0.0)], axis=0)


def _wkv_kernel(r_ref, lw_ref, k_ref, v_ref, kk_ref, alr_ref, g_ref, bonus_ref, gnw_ref, gnb_ref, tri_ref,
                segm_ref, y_ref, state_ref):
    C = WKV_CHUNK
    nb, _, width = lw_ref.shape
    npair = width // PAIR

    @pl.when(pl.program_id(1) == 0)
    def _():
        state_ref[...] = jnp.zeros_like(state_ref)

    ri = lax.broadcasted_iota(jnp.int32, (2 * C, 2 * C), 0)
    ci = lax.broadcasted_iota(jnp.int32, (2 * C, 2 * C), 1)
    same = (ri // C) == (ci // C)
    strict = same & ((ci % C) < (ri % C))
    incl = same & ((ci % C) <= (ri % C))
    eye = (ri == ci).astype(F32)

    lhs, rhs, a2s, r2s, v2s, bhts, bkts, ptots = [], [], [], [], [], [], [], []
    for bi in range(nb):
        lw = lw_ref[bi]
        cum = sum(jnp.dot(tri_ref[...], piece.astype(BF16), preferred_element_type=F32) for piece in _split3(lw))
        total = cum[C - 1:C, :]
        p_inv = jnp.exp(-cum)
        p_rem = jnp.exp(total - cum)
        p_tot = jnp.exp(total)
        kk = kk_ref[bi]
        k2 = k_ref[bi]
        b = kk * alr_ref[bi]
        a_t = -kk * jnp.exp(cum - lw)
        r_t = r_ref[bi] * jnp.exp(cum)
        b_t = b * p_inv
        k_t = k2 * p_inv
        b_h = b * p_rem
        k_h = k2 * p_rem
        v = v_ref[bi]
        for p in range(npair):
            sl = slice(p * PAIR, (p + 1) * PAIR)
            a2, r2, b2, kt2 = (_stack_heads(t[:, sl]) for t in (a_t, r_t, b_t, k_t))
            bh2, kh2, v2 = (_stack_heads(t[:, sl]) for t in (b_h, k_h, v))
            lhs.append(jnp.concatenate([a2, r2], axis=0))
            rhs.append(jnp.concatenate([b2, kt2], axis=0))
            a2s.append(a2)
            r2s.append(r2)
            v2s.append(v2)
            bhts.append(bh2.T)
            bkts.append(jnp.concatenate([bh2.T, kh2.T], axis=1))
            ptots.append(p_tot[:, sl])

    chains = range(nb * npair)
    m = [_dot_t(lhs[c], rhs[c]) for c in chains]
    n_ab = [jnp.where(strict, m[c][:2 * C, :2 * C], 0.0) for c in chains]
    m_ak = [jnp.where(strict, m[c][:2 * C, 2 * C:], 0.0) for c in chains]
    m_rb = [jnp.where(incl, m[c][2 * C:, :2 * C], 0.0) for c in chains]
    m_rk = [jnp.where(incl, m[c][2 * C:, 2 * C:], 0.0) for c in chains]
    mv = [_dot(m_ak[c], v2s[c]) for c in chains]
    mrkv = [_dot(m_rk[c], v2s[c]) for c in chains]
    inv = [eye + n_ab[c] for c in chains]
    pw = n_ab
    for _ in range(C.bit_length() - 2):
        pw = [_dot(pw[c], pw[c]) for c in chains]
        inv = [inv[c] + _dot(inv[c], pw[c]) for c in chains]
    wu = [_dot(inv[c], jnp.concatenate([a2s[c], mv[c]], axis=1)) for c in chains]
    qy = [_dot(m_rb[c], wu[c]) + jnp.concatenate([r2s[c], mrkv[c]], axis=1) for c in chains]
    g_t = [_dot(bhts[c], wu[c][:, :PAIR]) + eye * ptots[c] for c in chains]
    h_t = [_dot(bkts[c], jnp.concatenate([wu[c][:, PAIR:], v2s[c]], axis=0)) for c in chains]
    s0 = [state_ref[c] for c in chains]
    y2 = [_dot(qy[c][:, :PAIR], s0[c]) + qy[c][:, PAIR:] for c in chains]
    for c in chains:
        state_ref[c] = _dot(g_t[c], s0[c]) + h_t[c]

    segm = segm_ref[...]
    for bi in range(nb):
        y = jnp.concatenate([y2[bi * npair + p][:C] + y2[bi * npair + p][C:] for p in range(npair)], axis=1)
        mean = _segsum(y, segm)
        yc = y - mean
        var = _segsum(yc * yc, segm)
        yn = yc * lax.rsqrt(var + RW_GN_EPS) * gnw_ref[...] + gnb_ref[...]
        y_ref[bi] = ((yn + bonus_ref[bi]) * g_ref[bi]).astype(y_ref.dtype)


def _wkv(r, lw, k2, v, kk, alr, g, bonus, gnw, gnb, tri, segm, B, S):
    T, width = r.shape
    C = WKV_CHUNK
    nb = WKV_BATCH
    nc = S // C
    fixed = lambda b, c: (0, 0)
    blk = pl.BlockSpec((nb, C, width), lambda b, c: (b, c, 0))
    vec = pl.BlockSpec((1, width), fixed)
    ins = [t.reshape(B, S, width) for t in (r, lw, k2, v, kk, alr, g, bonus)]
    y = pl.pallas_call(
        _wkv_kernel,
        grid=(B // nb, nc),
        in_specs=[blk] * 8 + [vec, vec, pl.BlockSpec((C, C), fixed), pl.BlockSpec((width, width), fixed)],
        out_specs=blk,
        out_shape=jax.ShapeDtypeStruct((B, S, width), BF16),
        scratch_shapes=[pltpu.VMEM((nb * width // PAIR, PAIR, PAIR), F32)],
        compiler_params=_cparams(("parallel", "arbitrary")),
        name="wkv",
    )(*ins, gnw, gnb, tri, segm)
    return y.reshape(T, width)


def _pieces(x, n):
    hi, mid, lo = _split3(x)
    return hi + pltpu.roll(mid, n, axis=1) + pltpu.roll(lo, 2 * n, axis=1)


def _spread_heads(x):
    lane = lax.broadcasted_iota(jnp.int32, (x.shape[0], PAIR), 1)
    groups = []
    for p in range(x.shape[1] // PAIR):
        blk = x[:, p * PAIR:(p + 1) * PAIR]
        groups.append(jnp.where(lane < HEAD_DIM, blk, 0.0))
        groups.append(jnp.where(lane < HEAD_DIM, pltpu.roll(blk, HEAD_DIM, axis=1), 0.0))
    return jnp.concatenate(groups, axis=1)


def _front_kernel(x_ref, lnw_ref, lnb_ref, w_ref, wfz_ref,
                  mu_ref, w0_ref, w2_ref, a0_ref, a2_ref, g2_ref, kkw_ref, kaw_ref, rk_ref, seg_ref,
                  bf_ref, qw_ref, kw_ref, segm_ref, tril_ref, place_ref, maskq_ref, maskk_ref,
                  oneq_ref, onek_ref, onev_ref,
                  h_ref, r_ref, lw_ref, k_ref, v_ref, kk_ref, alr_ref, g_ref, bonus_ref,
                  qa_ref, ka_ref, va_ref, og_ref, shift_carry, c_carry, *, width, nh):
    @pl.when(pl.program_id(1) == 0)
    def _():
        shift_carry[...] = jnp.zeros_like(shift_carry)
        c_carry[...] = jnp.zeros_like(c_carry)

    h = _layer_norm(x_ref[...], lnw_ref[...], lnb_ref[...])
    h_ref[...] = h
    proj = jnp.dot(h.astype(BF16), w_ref[...], preferred_element_type=F32)
    tm = proj.shape[0]
    n_rw = mu_ref.shape[1]

    p = proj[:, :n_rw]
    prev = pltpu.roll(p, 1, axis=0)
    first_row = lax.broadcasted_iota(jnp.int32, p.shape, 0) == 0
    prev = jnp.where(first_row, shift_carry[...], prev)
    shift_carry[...] = p[tm - 1:tm, :]
    ps = p + mu_ref[...] * (prev - p)
    r = ps[:, 0:width]
    k = ps[:, width:2 * width]
    v = ps[:, 2 * width:3 * width]
    lora = ps[:, 3 * width:]
    seg = seg_ref[...]
    wl = w0_ref[...] + _dot(jnp.tanh(lora), w2_ref[...])
    w_raw = -_softplus(-wl) - 0.5
    lw_ref[...] = -jnp.exp(w_raw)
    alr = _sigmoid(a0_ref[...] + _dot(lora, a2_ref[...]))
    g_ref[...] = _dot(_sigmoid(lora), g2_ref[...])
    kkp = k * kkw_ref[...]
    nrm = jnp.sqrt(_segsum(kkp * kkp, seg))
    kk_ref[...] = kkp / jnp.maximum(nrm, 1e-12)
    k2 = k * (1.0 + (alr - 1.0) * kaw_ref[...])
    bonus_ref[...] = _segsum(r * k2 * rk_ref[...], seg) * v
    r_ref[...] = r
    k_ref[...] = k2
    v_ref[...] = v
    alr_ref[...] = alr

    fw = qw_ref.shape[1]
    q = proj[:, n_rw:n_rw + fw]
    kx = proj[:, n_rw + fw:n_rw + 2 * fw]
    vx = proj[:, n_rw + 2 * fw:n_rw + 3 * fw]
    og_ref[...] = proj[:, n_rw + 3 * fw:n_rw + 4 * fw]
    segm = segm_ref[...]
    qn = q * lax.rsqrt(_segsum(q * q, segm) + QK_EPS) * (qw_ref[...] * (HEAD_DIM ** -0.5 * LOG2E))
    kn = kx * lax.rsqrt(_segsum(kx * kx, segm) + QK_EPS) * kw_ref[...]
    fblk = proj[:, n_rw + 4 * fw:]
    h_lo = (h - h.astype(BF16).astype(F32)).astype(BF16)
    fz = fblk + pltpu.roll(fblk, PAIR - nh, axis=1) + jnp.dot(h_lo, wfz_ref[...], preferred_element_type=F32)
    in_heads = lax.broadcasted_iota(jnp.int32, fz.shape, 1) < nh
    lf = jnp.where(in_heads, -_softplus(-(fz + bf_ref[...])), 0.0)
    c3 = jnp.dot(tril_ref[...], _pieces(lf, nh).astype(BF16), preferred_element_type=F32)
    c = c3 + pltpu.roll(c3, PAIR - nh, axis=1) + pltpu.roll(c3, PAIR - 2 * nh, axis=1)
    c = jnp.where(in_heads, c, 0.0) + c_carry[...]
    c_carry[...] = c[tm - 1:tm, :]
    placed = jnp.dot(_pieces(c * LOG2E, nh).astype(BF16), place_ref[...], preferred_element_type=F32)
    qa_ref[...] = (_spread_heads(qn) + placed * maskq_ref[...] + oneq_ref[...]).astype(qa_ref.dtype)
    ka_ref[...] = (_spread_heads(kn) - placed * maskk_ref[...] + onek_ref[...]).astype(ka_ref.dtype)
    va_ref[...] = (_spread_heads(vx) + onev_ref[...]).astype(va_ref.dtype)


def _front(x2, B, S, lnw, lnb, w_main, wfz_hi, rw_params, fx_params, rw_w, fx_w, nh):
    T, D = x2.shape
    wide = nh * PAIR
    tm = ROW_TILE
    ns = S // tm
    row = lambda b, s: (b * ns + s, 0)
    fixed = lambda b, s: (0, 0)
    const = lambda a: pl.BlockSpec(a.shape, fixed)

    lane = jnp.arange(wide) % PAIR
    head = jnp.arange(wide) // PAIR
    src = jnp.arange(PAIR)
    piece = (lane - HEAD_DIM) % 3
    place = ((src[:, None] == (piece * nh + head)[None, :]) & (src[:, None] < 3 * nh)
             & (lane >= HEAD_DIM)[None, :] & (lane < HEAD_DIM + 6)[None, :]).astype(BF16)
    mask_k = ((lane >= HEAD_DIM) & (lane < HEAD_DIM + 3)).astype(F32).reshape(1, wide)
    mask_q = ((lane >= HEAD_DIM + 3) & (lane < HEAD_DIM + 6)).astype(F32).reshape(1, wide)
    one_v = (lane >= HEAD_DIM).astype(F32).reshape(1, wide)
    consts = [lnw, lnb, w_main, wfz_hi, *rw_params, *fx_params, place, mask_q, mask_k, mask_k, mask_q, one_v]

    f32_out = lambda n: (pl.BlockSpec((tm, n), row), jax.ShapeDtypeStruct((T, n), F32))
    bf_out = lambda n: (pl.BlockSpec((tm, n), row), jax.ShapeDtypeStruct((T, n), BF16))
    outs = [f32_out(D)] + [f32_out(rw_w)] * 8 + [bf_out(wide)] * 3 + [f32_out(fx_w)]
    return pl.pallas_call(
        functools.partial(_front_kernel, width=rw_w, nh=nh),
        grid=(B, ns),
        in_specs=[pl.BlockSpec((tm, D), row)] + [const(a) for a in consts],
        out_specs=[o[0] for o in outs],
        out_shape=[o[1] for o in outs],
        scratch_shapes=[pltpu.VMEM((1, rw_params[0].shape[1]), F32), pltpu.VMEM((1, PAIR), F32)],
        compiler_params=pltpu.CompilerParams(dimension_semantics=("parallel", "arbitrary"),
                                             vmem_limit_bytes=FRONT_VMEM_LIMIT),
        name="front",
    )(x2, *consts)


def _fox_attn_kernel(qi_ref, kj_ref, q_ref, k_ref, v_ref, og_ref, o_ref, m_ref, acc_ref):
    i = qi_ref[pl.program_id(2)]
    j = kj_ref[pl.program_id(2)]
    tq = q_ref.shape[0]
    tk = k_ref.shape[0]
    heads = range(ATTN_HEADS)

    @pl.when(j == 0)
    def _():
        m_ref[...] = jnp.full_like(m_ref, NEG_BIG)
        acc_ref[...] = jnp.zeros_like(acc_ref)

    def step(r0, nr, nk, masked):
        grp = lambda ref, n0, n, h: ref[n0:n0 + n, h * PAIR:(h + 1) * PAIR]
        s = [lax.dot_general(grp(q_ref, r0, nr, h), grp(k_ref, 0, nk, h), (((1,), (1,)), ((), ())),
                             preferred_element_type=F32) for h in heads]
        if masked:
            row = lax.broadcasted_iota(jnp.int32, (nr, nk), 0) + r0
            col = lax.broadcasted_iota(jnp.int32, (nr, nk), 1)
            s = [jnp.where(col <= row, s[h], NEG_BIG) for h in heads]
        m_old = [m_ref[h, r0:r0 + nr, :] for h in heads]
        m_new = [jnp.maximum(m_old[h], jnp.max(s[h], axis=1, keepdims=True)) for h in heads]
        alpha = [jnp.exp2(m_old[h] - m_new[h]) for h in heads]
        pr = [jnp.exp2(s[h] - jnp.concatenate([m_new[h]] * (nk // PAIR), axis=1)).astype(BF16) for h in heads]
        pv = [jnp.dot(pr[h], grp(v_ref, 0, nk, h), preferred_element_type=F32) for h in heads]
        for h in heads:
            acc_ref[h, r0:r0 + nr, :] = alpha[h] * acc_ref[h, r0:r0 + nr, :] + pv[h]
            m_ref[h, r0:r0 + nr, :] = m_new[h]

    @pl.when(j < i)
    def _():
        step(0, tq, tk, False)

    @pl.when(j == i)
    def _():
        step(0, tq // 2, tk // 2, True)
        step(tq // 2, tq - tq // 2, tk, True)
        lane = lax.broadcasted_iota(jnp.int32, (tq, PAIR), 1)
        outs = []
        for p in range(ATTN_HEADS // 2):
            a0 = acc_ref[2 * p]
            a1 = acc_ref[2 * p + 1]
            o0 = a0 * pltpu.roll(1.0 / a0, HEAD_DIM, axis=1)
            o1 = pltpu.roll(a1, HEAD_DIM, axis=1) * (1.0 / a1)
            outs.append(jnp.where(lane < HEAD_DIM, o0, o1))
        o = jnp.concatenate(outs, axis=1)
        o_ref[...] = (o * _sigmoid(og_ref[...])).astype(o_ref.dtype)


def _fox_attn(qa, ka, va, og, B, S, width):
    T = qa.shape[0]
    tq, tk = ATTN_TQ, ATTN_TK
    assert tq == tk
    nq = S // tq
    nh = ATTN_HEADS
    ow = nh * HEAD_DIM
    ngroup = width // ow
    pairs = [(i, j) for i in range(nq) for j in range(i + 1)]
    qi = jnp.array([ij[0] for ij in pairs], jnp.int32)
    kj = jnp.array([ij[1] for ij in pairs], jnp.int32)
    kv = pl.BlockSpec((tk, nh * PAIR), lambda b, p, t, qi, kj: (b * nq + kj[t], p))
    grid_spec = pltpu.PrefetchScalarGridSpec(
        num_scalar_prefetch=2,
        grid=(B, ngroup, len(pairs)),
        in_specs=[
            pl.BlockSpec((tq, nh * PAIR), lambda b, p, t, qi, kj: (b * nq + qi[t], p)), kv, kv,
            pl.BlockSpec((tq, ow), lambda b, p, t, qi, kj: (b * nq + qi[t], p)),
        ],
        out_specs=pl.BlockSpec((tq, ow), lambda b, p, t, qi, kj: (b * nq + qi[t], p)),
        scratch_shapes=[pltpu.VMEM((nh, tq, PAIR), F32), pltpu.VMEM((nh, tq, PAIR), F32)],
    )
    return pl.pallas_call(
        _fox_attn_kernel,
        grid_spec=grid_spec,
        out_shape=jax.ShapeDtypeStruct((T, width), BF16),
        compiler_params=_cparams(("parallel", "parallel", "arbitrary")),
        name="fox_attn",
    )(qi, kj, qa, ka, va, og)


def _out_ln_kernel(yrw_ref, yfx_ref, h_ref, wo1_ref, wo2_ref, lnw_ref, lnb_ref, rw_ref, rb_ref,
                   h1_ref, logit_ref, *, alpha):
    mix = (jnp.dot(yrw_ref[...], wo1_ref[...], preferred_element_type=F32)
           + jnp.dot(yfx_ref[...], wo2_ref[...], preferred_element_type=F32))
    h1 = _layer_norm(alpha * h_ref[...] + mix, lnw_ref[...], lnb_ref[...])
    h1_ref[...] = h1
    ne = rb_ref.shape[1]
    h_hi = h1.astype(BF16)
    h_lo = (h1 - h_hi.astype(F32)).astype(BF16)
    rw = rw_ref[...]
    first = jnp.dot(h_hi, rw, preferred_element_type=F32)
    second = jnp.dot(h_lo, rw[:, :ne], preferred_element_type=F32)
    logit_ref[...] = first[:, :ne] + first[:, ne:] + second + rb_ref[...]


def _out_ln(y_rw, y_fx, h0, wo1, wo2, lnw, lnb, rw, rb, alpha):
    T, D = h0.shape
    width = y_rw.shape[1]
    ne = rb.shape[1]
    tm = LN_ROW_TILE
    row = lambda i: (i, 0)
    fixed = lambda i: (0, 0)
    rw_hi = rw.astype(BF16)
    rw_lo = (rw - rw_hi.astype(F32)).astype(BF16)
    rw = jnp.concatenate([rw_hi, rw_lo], axis=1)
    return pl.pallas_call(
        functools.partial(_out_ln_kernel, alpha=alpha),
        grid=(T // tm,),
        in_specs=[
            pl.BlockSpec((tm, width), row), pl.BlockSpec((tm, width), row), pl.BlockSpec((tm, D), row),
            pl.BlockSpec((width, D), fixed), pl.BlockSpec((width, D), fixed),
            pl.BlockSpec((1, D), fixed), pl.BlockSpec((1, D), fixed),
            pl.BlockSpec((D, 2 * ne), fixed), pl.BlockSpec((1, ne), fixed),
        ],
        out_specs=[pl.BlockSpec((tm, D), row), pl.BlockSpec((tm, ne), row)],
        out_shape=[jax.ShapeDtypeStruct((T, D), F32), jax.ShapeDtypeStruct((T, ne), F32)],
        compiler_params=_cparams(("parallel",)),
        name="out_ln",
    )(y_rw, y_fx, h0, wo1, wo2, lnw, lnb, rw, rb)


DEINT_COLS = 256
MOE_VMEM_LIMIT = 56 * 1024 * 1024


def _moe_kernel(bexp_ref, nused_ref, first_ref, nexte_ref, slot_ref, xbase_ref, xrows_ref,
                x_hbm, w1_hbm, w2_hbm, b1g_ref, b1l_ref, b2_ref, perm_ref, o_ref,
                w1_stage, w2_stage, w1g_bf, w1l_bf, w2_bf, x_buf, sem, sem_x):
    i = pl.program_id(0)
    live = i < nused_ref[0]
    tm = x_buf.shape[1]

    def x_copy(blk, s):
        row0 = pl.multiple_of(xbase_ref[blk], ROW_ALIGN)
        return pltpu.make_async_copy(x_hbm.at[pl.ds(row0, tm), :], x_buf.at[s], sem_x.at[s])

    @pl.when(live & (i == 0))
    def _():
        x_copy(0, 0).start()

    def weight_copies(e, s):
        return (pltpu.make_async_copy(w1_hbm.at[e], w1_stage.at[s], sem.at[0, s]),
                pltpu.make_async_copy(w2_hbm.at[e], w2_stage.at[s], sem.at[1, s]))

    @pl.when(live & (i == 0))
    def _():
        for cp in weight_copies(bexp_ref[0], 0):
            cp.start()

    @pl.when(live & (first_ref[i] == 1))
    def _():
        s = slot_ref[i]
        for cp in weight_copies(bexp_ref[i], s):
            cp.wait()

        @pl.when(nexte_ref[i] >= 0)
        def _():
            for cp in weight_copies(nexte_ref[i], 1 - s):
                cp.start()

        half = DEINT_COLS // 2
        for c in range(w1_stage.shape[2] // DEINT_COLS):
            blk = w1_stage[s, :, c * DEINT_COLS:(c + 1) * DEINT_COLS].astype(BF16)
            out = jnp.dot(blk, perm_ref[...], preferred_element_type=F32).astype(BF16)
            w1g_bf[:, c * half:(c + 1) * half] = out[:, :half]
            w1l_bf[:, c * half:(c + 1) * half] = out[:, half:]
        w2_bf[...] = w2_stage[s].astype(BF16)

    @pl.when(live)
    def _():
        xs = i % 2
        x_copy(i, xs).wait()

        @pl.when(i + 1 < nused_ref[0])
        def _():
            x_copy(i + 1, 1 - xs).start()

        def ffn(x):
            x = x.astype(BF16)
            x_glu = jnp.minimum(jnp.dot(x, w1g_bf[...], preferred_element_type=F32) + b1g_ref[0], SWIGLU_LIMIT)
            x_lin = jnp.clip(jnp.dot(x, w1l_bf[...], preferred_element_type=F32) + b1l_ref[0],
                             -SWIGLU_LIMIT, SWIGLU_LIMIT)
            act = x_glu * _sigmoid(SWIGLU_ALPHA * x_glu) * (x_lin + 1.0)
            return _pack_bf16_pairs(jnp.dot(act.astype(BF16), w2_bf[...], preferred_element_type=F32) + b2_ref[0])

        half = tm // 2
        short = xrows_ref[i] <= half

        @pl.when(short)
        def _():
            o_ref[:half, :] = ffn(x_buf[xs, :half, :])
            o_ref[half:, :] = jnp.zeros((tm - half, o_ref.shape[1]), o_ref.dtype)

        @pl.when(jnp.logical_not(short))
        def _():
            o_ref[...] = ffn(x_buf[xs])

    @pl.when(jnp.logical_not(live))
    def _():
        o_ref[...] = jnp.zeros_like(o_ref)


def _moe_ffn(block_exp, n_used, x_base, x_rows, n_blocks, xs, w1, b1g, b1l, w2, b2):
    E, D, F2 = w1.shape
    F = F2 // 2
    tm = MOE_TILE
    n_rows = n_blocks * tm

    idx = jnp.arange(n_blocks, dtype=jnp.int32)
    first = jnp.concatenate([jnp.ones((1,), jnp.bool_), block_exp[1:] != block_exp[:-1]])
    slot = ((jnp.cumsum(first.astype(jnp.int32)) - 1) % 2).astype(jnp.int32)
    cand = jnp.where(first & (idx < n_used[0]), idx, n_blocks)
    next_first = jnp.min(jnp.where(idx[None, :] > idx[:, None], cand[None, :], n_blocks), axis=1)
    next_e = jnp.where(next_first < n_blocks, block_exp[jnp.minimum(next_first, n_blocks - 1)], -1).astype(jnp.int32)

    half = DEINT_COLS // 2
    src = jnp.arange(DEINT_COLS)
    dst = jnp.where(src % 2 == 0, src // 2, half + src // 2)
    perm = (dst[:, None] == jnp.arange(DEINT_COLS)[None, :]).astype(BF16)

    live = lambda i, be, nu: jnp.minimum(i, nu[0] - 1)
    bspec = lambda n: pl.BlockSpec((1, 1, n), lambda i, be, nu, fi, ne, sl, xb, xr: (be[live(i, be, nu)], 0, 0))
    grid_spec = pltpu.PrefetchScalarGridSpec(
        num_scalar_prefetch=7,
        grid=(n_blocks,),
        in_specs=[
            pl.BlockSpec(memory_space=pl.ANY),
            pl.BlockSpec(memory_space=pl.ANY),
            pl.BlockSpec(memory_space=pl.ANY),
            bspec(F), bspec(F), bspec(D),
            pl.BlockSpec((DEINT_COLS, DEINT_COLS), lambda i, be, nu, fi, ne, sl, xb, xr: (0, 0)),
        ],
        out_specs=pl.BlockSpec((tm, D // 2), lambda i, be, nu, fi, ne, sl, xb, xr: (i, 0)),
        scratch_shapes=[
            pltpu.VMEM((2, D, F2), F32), pltpu.VMEM((2, F, D), F32),
            pltpu.VMEM((D, F), BF16), pltpu.VMEM((D, F), BF16), pltpu.VMEM((F, D), BF16),
            pltpu.VMEM((2, tm, D), F32),
            pltpu.SemaphoreType.DMA((2, 2)), pltpu.SemaphoreType.DMA((2,)),
        ],
    )
    return pl.pallas_call(
        _moe_kernel,
        grid_spec=grid_spec,
        out_shape=jax.ShapeDtypeStruct((n_rows, D // 2), jnp.uint32),
        compiler_params=pltpu.CompilerParams(dimension_semantics=("arbitrary",), vmem_limit_bytes=MOE_VMEM_LIMIT),
        name="moe_ffn",
    )(block_exp, n_used, first.astype(jnp.int32), next_e, slot, x_base, x_rows, xs, w1, w2, b1g, b1l, b2, perm)


def _combine_ln_kernel(h_ref, eo_ref, gate_ref, lnw_ref, lnb_ref, o_ref, *, alpha):
    gates = gate_ref[...]
    ffn = _unpack_bf16_pairs(eo_ref[0]) * gates[:, 0:1]
    for j in range(1, TOP_K):
        ffn = ffn + _unpack_bf16_pairs(eo_ref[j]) * gates[:, j:j + 1]
    o_ref[...] = _layer_norm(alpha * h_ref[...] + ffn, lnw_ref[...], lnb_ref[...])


def _combine_ln(h1, eo, gates, lnw, lnb, alpha):
    T, D = h1.shape
    tm = LN_ROW_TILE
    return pl.pallas_call(
        functools.partial(_combine_ln_kernel, alpha=alpha),
        grid=(T // tm,),
        in_specs=[
            pl.BlockSpec((tm, D), lambda i: (i, 0)),
            pl.BlockSpec((TOP_K, tm, D // 2), lambda i: (0, i, 0)),
            pl.BlockSpec((tm, TOP_K), lambda i: (i, 0)),
            pl.BlockSpec((1, D), lambda i: (0, 0)),
            pl.BlockSpec((1, D), lambda i: (0, 0)),
        ],
        out_specs=pl.BlockSpec((tm, D), lambda i: (i, 0)),
        out_shape=jax.ShapeDtypeStruct((T, D), F32),
        compiler_params=_cparams(("parallel",)),
        name="combine_ln",
    )(h1, eo, gates, lnw, lnb)


def _pad_to(x, n, axis):
    pad = [(0, 0)] * x.ndim
    pad[axis] = (0, n - x.shape[axis])
    return jnp.pad(x, pad)


def _block_diag_ones(width, value=1.0):
    idx = jnp.arange(width) // HEAD_DIM
    return jnp.where(idx[:, None] == idx[None, :], value, 0.0).astype(BF16)


def _route(logits, n_experts, tile):
    T = logits.shape[0]
    top_val, top_idx = lax.top_k(logits[:, :n_experts], TOP_K)
    gates = jax.nn.softmax(top_val, axis=-1)
    e_flat = top_idx.reshape(-1).astype(jnp.int32)
    n_assign = T * TOP_K
    assert n_assign % tile == 0 and tile % ROW_ALIGN == 0 and n_experts * n_assign < 2 ** 31
    n_blocks = n_assign // tile + n_experts + 1
    last_base = n_assign - tile
    eids = jnp.arange(n_experts, dtype=jnp.int32)
    aids = jnp.arange(n_assign, dtype=jnp.int32)
    skeys = lax.sort(e_flat * n_assign + aids)
    order = skeys % n_assign
    e_sorted = skeys // n_assign
    counts = jnp.sum((e_flat[:, None] == eids[None, :]).astype(jnp.int32), axis=0)
    starts = jnp.cumsum(counts) - counts
    base = starts // ROW_ALIGN * ROW_ALIGN
    padded = (starts - base + counts + tile - 1) // tile * tile
    pends = jnp.cumsum(padded)
    pstarts = pends - padded
    of_sorted = lambda per_expert: jnp.sum(
        jnp.where(e_sorted[:, None] == eids[None, :], per_expert[None, :], 0), axis=1)
    base_q = of_sorted(base)
    k_q = (aids - base_q) // tile
    window_q = jnp.minimum(base_q + k_q * tile, last_base)
    dest_sorted = of_sorted(pstarts) + k_q * tile + (aids - window_q)
    _, dest = lax.sort((order, dest_sorted), num_keys=1)
    block_start = jnp.arange(n_blocks, dtype=jnp.int32) * tile
    block_exp = jnp.minimum(jnp.sum((pends[None, :] <= block_start[:, None]).astype(jnp.int32), axis=1),
                            n_experts - 1).astype(jnp.int32)
    n_used = (pends[-1] // tile).astype(jnp.int32).reshape(1)
    of_block = lambda per_expert: jnp.sum(
        jnp.where(block_exp[:, None] == eids[None, :], per_expert[None, :], 0), axis=1)
    x_base = jnp.clip(of_block(base - pstarts) + block_start, 0, last_base).astype(jnp.int32)
    x_rows = jnp.clip(of_block(starts + counts) - x_base, 0, tile).astype(jnp.int32)
    return gates, dest.reshape(T, TOP_K), order // TOP_K, block_exp, n_used, x_base, x_rows, n_blocks


def kernel(x, ln_in_w, ln_in_b, w_in, rw_mu, rw_w0, rw_w2, rw_a0, rw_a2, rw_g2, rw_k_k, rw_k_a, rw_r_k,
           rw_gn_w, rw_gn_b, fx_b_f, fx_q_norm, fx_k_norm, w_o, ln1_w, ln1_b, router_w, router_b,
           exp_w1, exp_b1, exp_w2, exp_b2, ln2_w, ln2_b):
    B, S, D = x.shape
    T = B * S
    depth = w_in.shape[0]
    alpha = (2 * depth) ** 0.25
    rw_w = rw_w0.shape[1]
    fx_heads = fx_b_f.shape[1]
    fx_w = fx_heads * HEAD_DIM
    d_lora, a_lora, g_lora = rw_w2.shape[1], rw_a2.shape[1], rw_g2.shape[1]
    n_lora = d_lora + a_lora + g_lora
    lora_pad = -(-n_lora // 128) * 128
    rw_cols = 3 * rw_w + n_lora
    n_rw = 3 * rw_w + lora_pad
    n_experts = router_w.shape[2]
    ne_pad = -(-n_experts // 128) * 128
    row = lambda a: a.reshape(1, -1)

    seg_rw = _block_diag_ones(rw_w)
    segm_rw = _block_diag_ones(rw_w, 1.0 / HEAD_DIM)
    segm_fx = _block_diag_ones(fx_w, 1.0 / HEAD_DIM)
    tidx = jnp.arange(ROW_TILE)
    tril = (tidx[:, None] >= tidx[None, :]).astype(BF16)
    cidx = jnp.arange(WKV_CHUNK)
    tri_c = (cidx[:, None] >= cidx[None, :]).astype(BF16)

    assert depth == 1, "single-layer block"
    l = 0
    x2 = x.reshape(T, D)
    w_l = w_in[l]
    wfz = w_l[:, rw_cols + 4 * fx_w:]
    wfz_hi = wfz.astype(BF16)
    wfz_lo = (wfz - wfz_hi.astype(F32)).astype(BF16)
    w_main = jnp.concatenate(
        [_pad_to(w_l[:, :rw_cols], n_rw, 1).astype(BF16), w_l[:, rw_cols:rw_cols + 4 * fx_w].astype(BF16),
         _pad_to(jnp.concatenate([wfz_hi, wfz_lo], axis=1), PAIR, 1)], axis=1)
    mu = _pad_to(row(rw_mu[l]), n_rw, 1)
    w2p = _pad_to(rw_w2[l], lora_pad, 0).astype(BF16)
    a2p = _pad_to(jnp.pad(rw_a2[l], ((d_lora, 0), (0, 0))), lora_pad, 0).astype(BF16)
    g2p = _pad_to(jnp.pad(rw_g2[l], ((d_lora + a_lora, 0), (0, 0))), lora_pad, 0).astype(BF16)
    rw_params = [mu, row(rw_w0[l]), w2p, row(rw_a0[l]), a2p, g2p,
                 row(rw_k_k[l]), row(rw_k_a[l]), row(rw_r_k[l]), seg_rw]
    qw = row(jnp.tile(fx_q_norm[l], fx_heads))
    kw = row(jnp.tile(fx_k_norm[l], fx_heads))
    fx_params = [_pad_to(row(fx_b_f[l]), PAIR, 1), qw, kw, segm_fx, tril]
    h0, r, lw, k2, v, kk, alr, g, bonus, qa, ka, va, og = _front(
        x2, B, S, row(ln_in_w), row(ln_in_b), w_main, _pad_to(wfz_hi, PAIR, 1), rw_params, fx_params,
        rw_w, fx_w, fx_heads)

    y_rw = _wkv(r, lw, k2, v, kk, alr, g, bonus, row(rw_gn_w[l]), row(rw_gn_b[l]), tri_c, segm_rw, B, S)
    y_fx = _fox_attn(qa, ka, va, og, B, S, fx_w)

    wo = w_o[l].astype(BF16)
    rw_pad = _pad_to(router_w[l], ne_pad, 1)
    rb_pad = _pad_to(row(router_b[l]), ne_pad, 1)
    h1, logits = _out_ln(y_rw, y_fx, h0, wo[:rw_w], wo[rw_w:], row(ln1_w[l]), row(ln1_b[l]),
                         rw_pad, rb_pad, alpha)

    gates, pos, gather_tok, block_exp, n_used, x_base, x_rows, n_blocks = _route(logits, n_experts, MOE_TILE)
    xs = h1[gather_tok]
    b1 = exp_b1[l]
    b1g = b1[:, None, 0::2]
    b1l = b1[:, None, 1::2]
    eo_rows = _moe_ffn(block_exp, n_used, x_base, x_rows, n_blocks, xs, exp_w1[l], b1g, b1l, exp_w2[l],
                       exp_b2[l][:, None, :])
    eo = eo_rows[pos.T]
    h = _combine_ln(h1, eo, gates, row(ln2_w[l]), row(ln2_b[l]), alpha)
    return h.reshape(B, S, D)
```

```python
import functools

import jax
import jax.numpy as jnp
from jax import lax
from jax.experimental import pallas as pl
from jax.experimental.pallas import tpu as pltpu

F32 = jnp.float32
BF16 = jnp.bfloat16

HEAD_DIM = 64
PAIR = 2 * HEAD_DIM
WKV_CHUNK = 64
RW_GN_EPS = 64e-5
QK_EPS = 1e-6
LN_EPS = 1e-5
TOP_K = 4
SWIGLU_ALPHA = 1.702
SWIGLU_LIMIT = 7.0
NEG_BIG = -1e30
LOG2E = 1.4426950408889634

ROW_TILE = 256
LN_ROW_TILE = 1024
ATTN_TQ = 512
ATTN_TK = 512
ATTN_HEADS = 8
MOE_TILE = 512
ROW_ALIGN = 8
WKV_BATCH = 4
VMEM_LIMIT = 48 * 1024 * 1024
FRONT_VMEM_LIMIT = 56 * 1024 * 1024


def _cparams(sem):
    return pltpu.CompilerParams(dimension_semantics=sem, vmem_limit_bytes=VMEM_LIMIT)


def _dot(a, b):
    return jnp.dot(a.astype(BF16), b.astype(BF16), preferred_element_type=F32)


def _dot_t(a, b):
    return lax.dot_general(a.astype(BF16), b.astype(BF16), (((1,), (1,)), ((), ())),
                           preferred_element_type=F32)


def _segsum(x, seg):
    return jnp.dot(x.astype(BF16), seg, preferred_element_type=F32)


def _split3(x):
    hi = x.astype(BF16).astype(F32)
    mid = (x - hi).astype(BF16).astype(F32)
    return hi, mid, x - hi - mid


def _sigmoid(x):
    return 1.0 / (1.0 + jnp.exp(-x))


def _softplus(x):
    return jnp.maximum(x, 0.0) + jnp.log(1.0 + jnp.exp(-jnp.abs(x)))


def _layer_norm(x, w, b):
    mu = jnp.mean(x, axis=-1, keepdims=True)
    xc = x - mu
    var = jnp.mean(xc * xc, axis=-1, keepdims=True)
    return xc * lax.rsqrt(var + LN_EPS) * w + b


def _pack_bf16_pairs(x):
    n = x.shape[1] // 2
    bits = pltpu.bitcast(x.astype(BF16).astype(F32), jnp.uint32)
    return lax.shift_right_logical(bits[:, :n], jnp.uint32(16)) | bits[:, n:]


def _unpack_bf16_pairs(u):
    lo = pltpu.bitcast(lax.shift_left(u, jnp.uint32(16)), F32)
    hi = pltpu.bitcast(u & jnp.uint32(0xFFFF0000), F32)
    return jnp.concatenate([lo, hi], axis=1)


def _stack_heads(x):
    lane = lax.broadcasted_iota(jnp.int32, x.shape, 1)
    return jnp.concatenate([jnp.where(lane < HEAD_DIM, x, 0.0), jnp.where(lane >= HEAD_DIM, x, 0.0)], axis=0)


def _wkv_kernel(r_ref, lw_ref, k_ref, v_ref, kk_ref, alr_ref, g_ref, bonus_ref, gnw_ref, gnb_ref, tri_ref,
                segm_ref, y_ref, state_ref):
    C = WKV_CHUNK
    nb, _, width = lw_ref.shape
    npair = width // PAIR

    @pl.when(pl.program_id(1) == 0)
    def _():
        state_ref[...] = jnp.zeros_like(state_ref)

    ri = lax.broadcasted_iota(jnp.int32, (2 * C, 2 * C), 0)
    ci = lax.broadcasted_iota(jnp.int32, (2 * C, 2 * C), 1)
    same = (ri // C) == (ci // C)
    strict = same & ((ci % C) < (ri % C))
    incl = same & ((ci % C) <= (ri % C))
    eye = (ri == ci).astype(F32)

    lhs, rhs, a2s, r2s, v2s, bhts, bkts, ptots = [], [], [], [], [], [], [], []
    for bi in range(nb):
        lw = lw_ref[bi]
        cum = sum(jnp.dot(tri_ref[...], piece.astype(BF16), preferred_element_type=F32) for piece in _split3(lw))
        total = cum[C - 1:C, :]
        p_inv = jnp.exp(-cum)
        p_rem = jnp.exp(total - cum)
        p_tot = jnp.exp(total)
        kk = kk_ref[bi]
        k2 = k_ref[bi]
        b = kk * alr_ref[bi]
        a_t = -kk * jnp.exp(cum - lw)
        r_t = r_ref[bi] * jnp.exp(cum)
        b_t = b * p_inv
        k_t = k2 * p_inv
        b_h = b * p_rem
        k_h = k2 * p_rem
        v = v_ref[bi]
        for p in range(npair):
            sl = slice(p * PAIR, (p + 1) * PAIR)
            a2, r2, b2, kt2 = (_stack_heads(t[:, sl]) for t in (a_t, r_t, b_t, k_t))
            bh2, kh2, v2 = (_stack_heads(t[:, sl]) for t in (b_h, k_h, v))
            lhs.append(jnp.concatenate([a2, r2], axis=0))
            rhs.append(jnp.concatenate([b2, kt2], axis=0))
            a2s.append(a2)
            r2s.append(r2)
            v2s.append(v2)
            bhts.append(bh2.T)
            bkts.append(jnp.concatenate([bh2.T, kh2.T], axis=1))
            ptots.append(p_tot[:, sl])

    chains = range(nb * npair)
    m = [_dot_t(lhs[c], rhs[c]) for c in chains]
    n_ab = [jnp.where(strict, m[c][:2 * C, :2 * C], 0.0) for c in chains]
    m_ak = [jnp.where(strict, m[c][:2 * C, 2 * C:], 0.0) for c in chains]
    m_rb = [jnp.where(incl, m[c][2 * C:, :2 * C], 0.0) for c in chains]
    m_rk = [jnp.where(incl, m[c][2 * C:, 2 * C:], 0.0) for c in chains]
    mv = [_dot(m_ak[c], v2s[c]) for c in chains]
    mrkv = [_dot(m_rk[c], v2s[c]) for c in chains]
    inv = [eye + n_ab[c] for c in chains]
    pw = n_ab
    for _ in range(C.bit_length() - 2):
        pw = [_dot(pw[c], pw[c]) for c in chains]
        inv = [inv[c] + _dot(inv[c], pw[c]) for c in chains]
    wu = [_dot(inv[c], jnp.concatenate([a2s[c], mv[c]], axis=1)) for c in chains]
    qy = [_dot(m_rb[c], wu[c]) + jnp.concatenate([r2s[c], mrkv[c]], axis=1) for c in chains]
    g_t = [_dot(bhts[c], wu[c][:, :PAIR]) + eye * ptots[c] for c in chains]
    h_t = [_dot(bkts[c], jnp.concatenate([wu[c][:, PAIR:], v2s[c]], axis=0)) for c in chains]
    s0 = [state_ref[c] for c in chains]
    y2 = [_dot(qy[c][:, :PAIR], s0[c]) + qy[c][:, PAIR:] for c in chains]
    for c in chains:
        state_ref[c] = _dot(g_t[c], s0[c]) + h_t[c]

    segm = segm_ref[...]
    for bi in range(nb):
        y = jnp.concatenate([y2[bi * npair + p][:C] + y2[bi * npair + p][C:] for p in range(npair)], axis=1)
        mean = _segsum(y, segm)
        yc = y - mean
        var = _segsum(yc * yc, segm)
        yn = yc * lax.rsqrt(var + RW_GN_EPS) * gnw_ref[...] + gnb_ref[...]
        y_ref[bi] = ((yn + bonus_ref[bi]) * g_ref[bi]).astype(y_ref.dtype)


def _wkv(r, lw, k2, v, kk, alr, g, bonus, gnw, gnb, tri, segm, B, S):
    T, width = r.shape
    C = WKV_CHUNK
    nb = WKV_BATCH
    nc = S // C
    fixed = lambda b, c: (0, 0)
    blk = pl.BlockSpec((nb, C, width), lambda b, c: (b, c, 0))
    vec = pl.BlockSpec((1, width), fixed)
    ins = [t.reshape(B, S, width) for t in (r, lw, k2, v, kk, alr, g, bonus)]
    y = pl.pallas_call(
        _wkv_kernel,
        grid=(B // nb, nc),
        in_specs=[blk] * 8 + [vec, vec, pl.BlockSpec((C, C), fixed), pl.BlockSpec((width, width), fixed)],
        out_specs=blk,
        out_shape=jax.ShapeDtypeStruct((B, S, width), BF16),
        scratch_shapes=[pltpu.VMEM((nb * width // PAIR, PAIR, PAIR), F32)],
        compiler_params=_cparams(("parallel", "arbitrary")),
        name="wkv",
    )(*ins, gnw, gnb, tri, segm)
    return y.reshape(T, width)


def _pieces(x, n):
    hi, mid, lo = _split3(x)
    return hi + pltpu.roll(mid, n, axis=1) + pltpu.roll(lo, 2 * n, axis=1)


def _spread_heads(x):
    lane = lax.broadcasted_iota(jnp.int32, (x.shape[0], PAIR), 1)
    groups = []
    for p in range(x.shape[1] // PAIR):
        blk = x[:, p * PAIR:(p + 1) * PAIR]
        groups.append(jnp.where(lane < HEAD_DIM, blk, 0.0))
        groups.append(jnp.where(lane < HEAD_DIM, pltpu.roll(blk, HEAD_DIM, axis=1), 0.0))
    return jnp.concatenate(groups, axis=1)


def _front_kernel(x_ref, lnw_ref, lnb_ref, w_ref, wfz_ref,
                  mu_ref, w0_ref, w2_ref, a0_ref, a2_ref, g2_ref, kkw_ref, kaw_ref, rk_ref, seg_ref,
                  bf_ref, qw_ref, kw_ref, segm_ref, tril_ref, place_ref, maskq_ref, maskk_ref,
                  oneq_ref, onek_ref, onev_ref,
                  h_ref, r_ref, lw_ref, k_ref, v_ref, kk_ref, alr_ref, g_ref, bonus_ref,
                  qa_ref, ka_ref, va_ref, og_ref, shift_carry, c_carry, *, width, nh):
    @pl.when(pl.program_id(1) == 0)
    def _():
        shift_carry[...] = jnp.zeros_like(shift_carry)
        c_carry[...] = jnp.zeros_like(c_carry)

    h = _layer_norm(x_ref[...], lnw_ref[...], lnb_ref[...])
    h_ref[...] = h
    proj = jnp.dot(h.astype(BF16), w_ref[...], preferred_element_type=F32)
    tm = proj.shape[0]
    n_rw = mu_ref.shape[1]

    p = proj[:, :n_rw]
    prev = pltpu.roll(p, 1, axis=0)
    first_row = lax.broadcasted_iota(jnp.int32, p.shape, 0) == 0
    prev = jnp.where(first_row, shift_carry[...], prev)
    shift_carry[...] = p[tm - 1:tm, :]
    ps = p + mu_ref[...] * (prev - p)
    r = ps[:, 0:width]
    k = ps[:, width:2 * width]
    v = ps[:, 2 * width:3 * width]
    lora = ps[:, 3 * width:]
    seg = seg_ref[...]
    wl = w0_ref[...] + _dot(jnp.tanh(lora), w2_ref[...])
    w_raw = -_softplus(-wl) - 0.5
    lw_ref[...] = -jnp.exp(w_raw)
    alr = _sigmoid(a0_ref[...] + _dot(lora, a2_ref[...]))
    g_ref[...] = _dot(_sigmoid(lora), g2_ref[...])
    kkp = k * kkw_ref[...]
    nrm = jnp.sqrt(_segsum(kkp * kkp, seg))
    kk_ref[...] = kkp / jnp.maximum(nrm, 1e-12)
    k2 = k * (1.0 + (alr - 1.0) * kaw_ref[...])
    bonus_ref[...] = _segsum(r * k2 * rk_ref[...], seg) * v
    r_ref[...] = r
    k_ref[...] = k2
    v_ref[...] = v
    alr_ref[...] = alr

    fw = qw_ref.shape[1]
    q = proj[:, n_rw:n_rw + fw]
    kx = proj[:, n_rw + fw:n_rw + 2 * fw]
    vx = proj[:, n_rw + 2 * fw:n_rw + 3 * fw]
    og_ref[...] = proj[:, n_rw + 3 * fw:n_rw + 4 * fw]
    segm = segm_ref[...]
    qn = q * lax.rsqrt(_segsum(q * q, segm) + QK_EPS) * (qw_ref[...] * (HEAD_DIM ** -0.5 * LOG2E))
    kn = kx * lax.rsqrt(_segsum(kx * kx, segm) + QK_EPS) * kw_ref[...]
    fblk = proj[:, n_rw + 4 * fw:]
    h_lo = (h - h.astype(BF16).astype(F32)).astype(BF16)
    fz = fblk + pltpu.roll(fblk, PAIR - nh, axis=1) + jnp.dot(h_lo, wfz_ref[...], preferred_element_type=F32)
    in_heads = lax.broadcasted_iota(jnp.int32, fz.shape, 1) < nh
    lf = jnp.where(in_heads, -_softplus(-(fz + bf_ref[...])), 0.0)
    c3 = jnp.dot(tril_ref[...], _pieces(lf, nh).astype(BF16), preferred_element_type=F32)
    c = c3 + pltpu.roll(c3, PAIR - nh, axis=1) + pltpu.roll(c3, PAIR - 2 * nh, axis=1)
    c = jnp.where(in_heads, c, 0.0) + c_carry[...]
    c_carry[...] = c[tm - 1:tm, :]
    placed = jnp.dot(_pieces(c * LOG2E, nh).astype(BF16), place_ref[...], preferred_element_type=F32)
    qa_ref[...] = (_spread_heads(qn) + placed * maskq_ref[...] + oneq_ref[...]).astype(qa_ref.dtype)
    ka_ref[...] = (_spread_heads(kn) - placed * maskk_ref[...] + onek_ref[...]).astype(ka_ref.dtype)
    va_ref[...] = (_spread_heads(vx) + onev_ref[...]).astype(va_ref.dtype)


def _front(x2, B, S, lnw, lnb, w_main, wfz_hi, rw_params, fx_params, rw_w, fx_w, nh):
    T, D = x2.shape
    wide = nh * PAIR
    tm = ROW_TILE
    ns = S // tm
    row = lambda b, s: (b * ns + s, 0)
    fixed = lambda b, s: (0, 0)
    const = lambda a: pl.BlockSpec(a.shape, fixed)

    lane = jnp.arange(wide) % PAIR
    head = jnp.arange(wide) // PAIR
    src = jnp.arange(PAIR)
    piece = (lane - HEAD_DIM) % 3
    place = ((src[:, None] == (piece * nh + head)[None, :]) & (src[:, None] < 3 * nh)
             & (lane >= HEAD_DIM)[None, :] & (lane < HEAD_DIM + 6)[None, :]).astype(BF16)
    mask_k = ((lane >= HEAD_DIM) & (lane < HEAD_DIM + 3)).astype(F32).reshape(1, wide)
    mask_q = ((lane >= HEAD_DIM + 3) & (lane < HEAD_DIM + 6)).astype(F32).reshape(1, wide)
    one_v = (lane >= HEAD_DIM).astype(F32).reshape(1, wide)
    consts = [lnw, lnb, w_main, wfz_hi, *rw_params, *fx_params, place, mask_q, mask_k, mask_k, mask_q, one_v]

    f32_out = lambda n: (pl.BlockSpec((tm, n), row), jax.ShapeDtypeStruct((T, n), F32))
    bf_out = lambda n: (pl.BlockSpec((tm, n), row), jax.ShapeDtypeStruct((T, n), BF16))
    outs = [f32_out(D)] + [f32_out(rw_w)] * 8 + [bf_out(wide)] * 3 + [f32_out(fx_w)]
    return pl.pallas_call(
        functools.partial(_front_kernel, width=rw_w, nh=nh),
        grid=(B, ns),
        in_specs=[pl.BlockSpec((tm, D), row)] + [const(a) for a in consts],
        out_specs=[o[0] for o in outs],
        out_shape=[o[1] for o in outs],
        scratch_shapes=[pltpu.VMEM((1, rw_params[0].shape[1]), F32), pltpu.VMEM((1, PAIR), F32)],
        compiler_params=pltpu.CompilerParams(dimension_semantics=("parallel", "arbitrary"),
                                             vmem_limit_bytes=FRONT_VMEM_LIMIT),
        name="front",
    )(x2, *consts)


def _fox_attn_kernel(qi_ref, kj_ref, q_ref, k_ref, v_ref, og_ref, o_ref, m_ref, acc_ref):
    i = qi_ref[pl.program_id(2)]
    j = kj_ref[pl.program_id(2)]
    tq = q_ref.shape[0]
    tk = k_ref.shape[0]
    heads = range(ATTN_HEADS)
    grp = lambda ref, h: ref[:, h * PAIR:(h + 1) * PAIR]

    @pl.when(j == 0)
    def _():
        m_ref[...] = jnp.full_like(m_ref, NEG_BIG)
        acc_ref[...] = jnp.zeros_like(acc_ref)

    def step(masked):
        s = [lax.dot_general(grp(q_ref, h), grp(k_ref, h), (((1,), (1,)), ((), ())), preferred_element_type=F32)
             for h in heads]
        if masked:
            row = lax.broadcasted_iota(jnp.int32, (tq, tk), 0)
            col = lax.broadcasted_iota(jnp.int32, (tq, tk), 1)
            s = [jnp.where(col <= row, s[h], NEG_BIG) for h in heads]
        m_old = [m_ref[h] for h in heads]
        m_new = [jnp.maximum(m_old[h], jnp.max(s[h], axis=1, keepdims=True)) for h in heads]
        alpha = [jnp.exp2(m_old[h] - m_new[h]) for h in heads]
        pr = [jnp.exp2(s[h] - jnp.concatenate([m_new[h]] * (tk // PAIR), axis=1)).astype(BF16) for h in heads]
        pv = [jnp.dot(pr[h], grp(v_ref, h), preferred_element_type=F32) for h in heads]
        for h in heads:
            acc_ref[h] = alpha[h] * acc_ref[h] + pv[h]
            m_ref[h] = m_new[h]

    @pl.when(j < i)
    def _():
        step(False)

    @pl.when(j == i)
    def _():
        step(True)
        lane = lax.broadcasted_iota(jnp.int32, (tq, PAIR), 1)
        outs = []
        for p in range(ATTN_HEADS // 2):
            a0 = acc_ref[2 * p]
            a1 = acc_ref[2 * p + 1]
            o0 = a0 * pltpu.roll(1.0 / a0, HEAD_DIM, axis=1)
            o1 = pltpu.roll(a1, HEAD_DIM, axis=1) * (1.0 / a1)
            outs.append(jnp.where(lane < HEAD_DIM, o0, o1))
        o = jnp.concatenate(outs, axis=1)
        o_ref[...] = (o * _sigmoid(og_ref[...])).astype(o_ref.dtype)


def _fox_attn(qa, ka, va, og, B, S, width):
    T = qa.shape[0]
    tq, tk = ATTN_TQ, ATTN_TK
    assert tq == tk
    nq = S // tq
    nh = ATTN_HEADS
    ow = nh * HEAD_DIM
    ngroup = width // ow
    pairs = [(i, j) for i in range(nq) for j in range(i + 1)]
    qi = jnp.array([ij[0] for ij in pairs], jnp.int32)
    kj = jnp.array([ij[1] for ij in pairs], jnp.int32)
    kv = pl.BlockSpec((tk, nh * PAIR), lambda b, p, t, qi, kj: (b * nq + kj[t], p))
    grid_spec = pltpu.PrefetchScalarGridSpec(
        num_scalar_prefetch=2,
        grid=(B, ngroup, len(pairs)),
        in_specs=[
            pl.BlockSpec((tq, nh * PAIR), lambda b, p, t, qi, kj: (b * nq + qi[t], p)), kv, kv,
            pl.BlockSpec((tq, ow), lambda b, p, t, qi, kj: (b * nq + qi[t], p)),
        ],
        out_specs=pl.BlockSpec((tq, ow), lambda b, p, t, qi, kj: (b * nq + qi[t], p)),
        scratch_shapes=[pltpu.VMEM((nh, tq, PAIR), F32), pltpu.VMEM((nh, tq, PAIR), F32)],
    )
    return pl.pallas_call(
        _fox_attn_kernel,
        grid_spec=grid_spec,
        out_shape=jax.ShapeDtypeStruct((T, width), BF16),
        compiler_params=_cparams(("parallel", "parallel", "arbitrary")),
        name="fox_attn",
    )(qi, kj, qa, ka, va, og)


def _out_ln_kernel(yrw_ref, yfx_ref, h_ref, wo1_ref, wo2_ref, lnw_ref, lnb_ref, rw_ref, rb_ref,
                   h1_ref, logit_ref, *, alpha):
    mix = (jnp.dot(yrw_ref[...], wo1_ref[...], preferred_element_type=F32)
           + jnp.dot(yfx_ref[...], wo2_ref[...], preferred_element_type=F32))
    h1 = _layer_norm(alpha * h_ref[...] + mix, lnw_ref[...], lnb_ref[...])
    h1_ref[...] = h1
    ne = rb_ref.shape[1]
    h_hi = h1.astype(BF16)
    h_lo = (h1 - h_hi.astype(F32)).astype(BF16)
    rw = rw_ref[...]
    first = jnp.dot(h_hi, rw, preferred_element_type=F32)
    second = jnp.dot(h_lo, rw[:, :ne], preferred_element_type=F32)
    logit_ref[...] = first[:, :ne] + first[:, ne:] + second + rb_ref[...]


def _out_ln(y_rw, y_fx, h0, wo1, wo2, lnw, lnb, rw, rb, alpha):
    T, D = h0.shape
    width = y_rw.shape[1]
    ne = rb.shape[1]
    tm = LN_ROW_TILE
    row = lambda i: (i, 0)
    fixed = lambda i: (0, 0)
    rw_hi = rw.astype(BF16)
    rw_lo = (rw - rw_hi.astype(F32)).astype(BF16)
    rw = jnp.concatenate([rw_hi, rw_lo], axis=1)
    return pl.pallas_call(
        functools.partial(_out_ln_kernel, alpha=alpha),
        grid=(T // tm,),
        in_specs=[
            pl.BlockSpec((tm, width), row), pl.BlockSpec((tm, width), row), pl.BlockSpec((tm, D), row),
            pl.BlockSpec((width, D), fixed), pl.BlockSpec((width, D), fixed),
            pl.BlockSpec((1, D), fixed), pl.BlockSpec((1, D), fixed),
            pl.BlockSpec((D, 2 * ne), fixed), pl.BlockSpec((1, ne), fixed),
        ],
        out_specs=[pl.BlockSpec((tm, D), row), pl.BlockSpec((tm, ne), row)],
        out_shape=[jax.ShapeDtypeStruct((T, D), F32), jax.ShapeDtypeStruct((T, ne), F32)],
        compiler_params=_cparams(("parallel",)),
        name="out_ln",
    )(y_rw, y_fx, h0, wo1, wo2, lnw, lnb, rw, rb)


DEINT_COLS = 256
MOE_VMEM_LIMIT = 56 * 1024 * 1024


def _moe_kernel(bexp_ref, nused_ref, first_ref, nexte_ref, slot_ref, xbase_ref, xrows_ref,
                x_hbm, w1_hbm, w2_hbm, b1g_ref, b1l_ref, b2_ref, perm_ref, o_ref,
                w1_stage, w2_stage, w1g_bf, w1l_bf, w2_bf, x_buf, sem, sem_x):
    i = pl.program_id(0)
    live = i < nused_ref[0]
    tm = x_buf.shape[1]

    def x_copy(blk, s):
        row0 = pl.multiple_of(xbase_ref[blk], ROW_ALIGN)
        return pltpu.make_async_copy(x_hbm.at[pl.ds(row0, tm), :], x_buf.at[s], sem_x.at[s])

    @pl.when(live & (i == 0))
    def _():
        x_copy(0, 0).start()

    def weight_copies(e, s):
        return (pltpu.make_async_copy(w1_hbm.at[e], w1_stage.at[s], sem.at[0, s]),
                pltpu.make_async_copy(w2_hbm.at[e], w2_stage.at[s], sem.at[1, s]))

    @pl.when(live & (i == 0))
    def _():
        for cp in weight_copies(bexp_ref[0], 0):
            cp.start()

    @pl.when(live & (first_ref[i] == 1))
    def _():
        s = slot_ref[i]
        for cp in weight_copies(bexp_ref[i], s):
            cp.wait()

        @pl.when(nexte_ref[i] >= 0)
        def _():
            for cp in weight_copies(nexte_ref[i], 1 - s):
                cp.start()

        half = DEINT_COLS // 2
        for c in range(w1_stage.shape[2] // DEINT_COLS):
            blk = w1_stage[s, :, c * DEINT_COLS:(c + 1) * DEINT_COLS].astype(BF16)
            out = jnp.dot(blk, perm_ref[...], preferred_element_type=F32).astype(BF16)
            w1g_bf[:, c * half:(c + 1) * half] = out[:, :half]
            w1l_bf[:, c * half:(c + 1) * half] = out[:, half:]
        w2_bf[...] = w2_stage[s].astype(BF16)

    @pl.when(live)
    def _():
        xs = i % 2
        x_copy(i, xs).wait()

        @pl.when(i + 1 < nused_ref[0])
        def _():
            x_copy(i + 1, 1 - xs).start()

        def ffn(x):
            x = x.astype(BF16)
            x_glu = jnp.minimum(jnp.dot(x, w1g_bf[...], preferred_element_type=F32) + b1g_ref[0], SWIGLU_LIMIT)
            x_lin = jnp.clip(jnp.dot(x, w1l_bf[...], preferred_element_type=F32) + b1l_ref[0],
                             -SWIGLU_LIMIT, SWIGLU_LIMIT)
            act = x_glu * _sigmoid(SWIGLU_ALPHA * x_glu) * (x_lin + 1.0)
            return _pack_bf16_pairs(jnp.dot(act.astype(BF16), w2_bf[...], preferred_element_type=F32) + b2_ref[0])

        half = tm // 2
        short = xrows_ref[i] <= half

        @pl.when(short)
        def _():
            o_ref[:half, :] = ffn(x_buf[xs, :half, :])
            o_ref[half:, :] = jnp.zeros((tm - half, o_ref.shape[1]), o_ref.dtype)

        @pl.when(jnp.logical_not(short))
        def _():
            o_ref[...] = ffn(x_buf[xs])

    @pl.when(jnp.logical_not(live))
    def _():
        o_ref[...] = jnp.zeros_like(o_ref)


def _moe_ffn(block_exp, n_used, x_base, x_rows, n_blocks, xs, w1, b1g, b1l, w2, b2):
    E, D, F2 = w1.shape
    F = F2 // 2
    tm = MOE_TILE
    n_rows = n_blocks * tm

    idx = jnp.arange(n_blocks, dtype=jnp.int32)
    first = jnp.concatenate([jnp.ones((1,), jnp.bool_), block_exp[1:] != block_exp[:-1]])
    slot = ((jnp.cumsum(first.astype(jnp.int32)) - 1) % 2).astype(jnp.int32)
    cand = jnp.where(first & (idx < n_used[0]), idx, n_blocks)
    next_first = jnp.min(jnp.where(idx[None, :] > idx[:, None], cand[None, :], n_blocks), axis=1)
    next_e = jnp.where(next_first < n_blocks, block_exp[jnp.minimum(next_first, n_blocks - 1)], -1).astype(jnp.int32)

    half = DEINT_COLS // 2
    src = jnp.arange(DEINT_COLS)
    dst = jnp.where(src % 2 == 0, src // 2, half + src // 2)
    perm = (dst[:, None] == jnp.arange(DEINT_COLS)[None, :]).astype(BF16)

    live = lambda i, be, nu: jnp.minimum(i, nu[0] - 1)
    bspec = lambda n: pl.BlockSpec((1, 1, n), lambda i, be, nu, fi, ne, sl, xb, xr: (be[live(i, be, nu)], 0, 0))
    grid_spec = pltpu.PrefetchScalarGridSpec(
        num_scalar_prefetch=7,
        grid=(n_blocks,),
        in_specs=[
            pl.BlockSpec(memory_space=pl.ANY),
            pl.BlockSpec(memory_space=pl.ANY),
            pl.BlockSpec(memory_space=pl.ANY),
            bspec(F), bspec(F), bspec(D),
            pl.BlockSpec((DEINT_COLS, DEINT_COLS), lambda i, be, nu, fi, ne, sl, xb, xr: (0, 0)),
        ],
        out_specs=pl.BlockSpec((tm, D // 2), lambda i, be, nu, fi, ne, sl, xb, xr: (i, 0)),
        scratch_shapes=[
            pltpu.VMEM((2, D, F2), F32), pltpu.VMEM((2, F, D), F32),
            pltpu.VMEM((D, F), BF16), pltpu.VMEM((D, F), BF16), pltpu.VMEM((F, D), BF16),
            pltpu.VMEM((2, tm, D), F32),
            pltpu.SemaphoreType.DMA((2, 2)), pltpu.SemaphoreType.DMA((2,)),
        ],
    )
    return pl.pallas_call(
        _moe_kernel,
        grid_spec=grid_spec,
        out_shape=jax.ShapeDtypeStruct((n_rows, D // 2), jnp.uint32),
        compiler_params=pltpu.CompilerParams(dimension_semantics=("arbitrary",), vmem_limit_bytes=MOE_VMEM_LIMIT),
        name="moe_ffn",
    )(block_exp, n_used, first.astype(jnp.int32), next_e, slot, x_base, x_rows, xs, w1, w2, b1g, b1l, b2, perm)


def _combine_ln_kernel(h_ref, eo_ref, gate_ref, lnw_ref, lnb_ref, o_ref, *, alpha):
    gates = gate_ref[...]
    ffn = _unpack_bf16_pairs(eo_ref[0]) * gates[:, 0:1]
    for j in range(1, TOP_K):
        ffn = ffn + _unpack_bf16_pairs(eo_ref[j]) * gates[:, j:j + 1]
    o_ref[...] = _layer_norm(alpha * h_ref[...] + ffn, lnw_ref[...], lnb_ref[...])


def _combine_ln(h1, eo, gates, lnw, lnb, alpha):
    T, D = h1.shape
    tm = LN_ROW_TILE
    return pl.pallas_call(
        functools.partial(_combine_ln_kernel, alpha=alpha),
        grid=(T // tm,),
        in_specs=[
            pl.BlockSpec((tm, D), lambda i: (i, 0)),
            pl.BlockSpec((TOP_K, tm, D // 2), lambda i: (0, i, 0)),
            pl.BlockSpec((tm, TOP_K), lambda i: (i, 0)),
            pl.BlockSpec((1, D), lambda i: (0, 0)),
            pl.BlockSpec((1, D), lambda i: (0, 0)),
        ],
        out_specs=pl.BlockSpec((tm, D), lambda i: (i, 0)),
        out_shape=jax.ShapeDtypeStruct((T, D), F32),
        compiler_params=_cparams(("parallel",)),
        name="combine_ln",
    )(h1, eo, gates, lnw, lnb)


def _pad_to(x, n, axis):
    pad = [(0, 0)] * x.ndim
    pad[axis] = (0, n - x.shape[axis])
    return jnp.pad(x, pad)


def _block_diag_ones(width, value=1.0):
    idx = jnp.arange(width) // HEAD_DIM
    return jnp.where(idx[:, None] == idx[None, :], value, 0.0).astype(BF16)


def _route(logits, n_experts, tile):
    T = logits.shape[0]
    top_val, top_idx = lax.top_k(logits[:, :n_experts], TOP_K)
    gates = jax.nn.softmax(top_val, axis=-1)
    e_flat = top_idx.reshape(-1).astype(jnp.int32)
    n_assign = T * TOP_K
    assert n_assign % tile == 0 and tile % ROW_ALIGN == 0 and n_experts * n_assign < 2 ** 31
    n_blocks = n_assign // tile + n_experts + 1
    last_base = n_assign - tile
    eids = jnp.arange(n_experts, dtype=jnp.int32)
    aids = jnp.arange(n_assign, dtype=jnp.int32)
    skeys = lax.sort(e_flat * n_assign + aids)
    order = skeys % n_assign
    e_sorted = skeys // n_assign
    counts = jnp.sum((e_flat[:, None] == eids[None, :]).astype(jnp.int32), axis=0)
    starts = jnp.cumsum(counts) - counts
    base = starts // ROW_ALIGN * ROW_ALIGN
    padded = (starts - base + counts + tile - 1) // tile * tile
    pends = jnp.cumsum(padded)
    pstarts = pends - padded
    of_sorted = lambda per_expert: jnp.sum(
        jnp.where(e_sorted[:, None] == eids[None, :], per_expert[None, :], 0), axis=1)
    base_q = of_sorted(base)
    k_q = (aids - base_q) // tile
    window_q = jnp.minimum(base_q + k_q * tile, last_base)
    dest_sorted = of_sorted(pstarts) + k_q * tile + (aids - window_q)
    _, dest = lax.sort((order, dest_sorted), num_keys=1)
    block_start = jnp.arange(n_blocks, dtype=jnp.int32) * tile
    block_exp = jnp.minimum(jnp.sum((pends[None, :] <= block_start[:, None]).astype(jnp.int32), axis=1),
                            n_experts - 1).astype(jnp.int32)
    n_used = (pends[-1] // tile).astype(jnp.int32).reshape(1)
    of_block = lambda per_expert: jnp.sum(
        jnp.where(block_exp[:, None] == eids[None, :], per_expert[None, :], 0), axis=1)
    x_base = jnp.clip(of_block(base - pstarts) + block_start, 0, last_base).astype(jnp.int32)
    x_rows = jnp.clip(of_block(starts + counts) - x_base, 0, tile).astype(jnp.int32)
    return gates, dest.reshape(T, TOP_K), order // TOP_K, block_exp, n_used, x_base, x_rows, n_blocks


def kernel(x, ln_in_w, ln_in_b, w_in, rw_mu, rw_w0, rw_w2, rw_a0, rw_a2, rw_g2, rw_k_k, rw_k_a, rw_r_k,
           rw_gn_w, rw_gn_b, fx_b_f, fx_q_norm, fx_k_norm, w_o, ln1_w, ln1_b, router_w, router_b,
           exp_w1, exp_b1, exp_w2, exp_b2, ln2_w, ln2_b):
    B, S, D = x.shape
    T = B * S
    depth = w_in.shape[0]
    alpha = (2 * depth) ** 0.25
    rw_w = rw_w0.shape[1]
    fx_heads = fx_b_f.shape[1]
    fx_w = fx_heads * HEAD_DIM
    d_lora, a_lora, g_lora = rw_w2.shape[1], rw_a2.shape[1], rw_g2.shape[1]
    n_lora = d_lora + a_lora + g_lora
    lora_pad = -(-n_lora // 128) * 128
    rw_cols = 3 * rw_w + n_lora
    n_rw = 3 * rw_w + lora_pad
    n_experts = router_w.shape[2]
    ne_pad = -(-n_experts // 128) * 128
    row = lambda a: a.reshape(1, -1)

    seg_rw = _block_diag_ones(rw_w)
    segm_rw = _block_diag_ones(rw_w, 1.0 / HEAD_DIM)
    segm_fx = _block_diag_ones(fx_w, 1.0 / HEAD_DIM)
    tidx = jnp.arange(ROW_TILE)
    tril = (tidx[:, None] >= tidx[None, :]).astype(BF16)
    cidx = jnp.arange(WKV_CHUNK)
    tri_c = (cidx[:, None] >= cidx[None, :]).astype(BF16)

    assert depth == 1, "single-layer block"
    l = 0
    x2 = x.reshape(T, D)
    w_l = w_in.reshape(w_in.shape[1:])
    wfz = w_l[:, rw_cols + 4 * fx_w:]
    wfz_hi = wfz.astype(BF16)
    wfz_lo = (wfz - wfz_hi.astype(F32)).astype(BF16)
    w_main = jnp.concatenate(
        [_pad_to(w_l[:, :rw_cols], n_rw, 1).astype(BF16), w_l[:, rw_cols:rw_cols + 4 * fx_w].astype(BF16),
         _pad_to(jnp.concatenate([wfz_hi, wfz_lo], axis=1), PAIR, 1)], axis=1)
    mu = _pad_to(row(rw_mu[l]), n_rw, 1)
    w2p = _pad_to(rw_w2[l], lora_pad, 0).astype(BF16)
    a2p = _pad_to(jnp.pad(rw_a2[l], ((d_lora, 0), (0, 0))), lora_pad, 0).astype(BF16)
    g2p = _pad_to(jnp.pad(rw_g2[l], ((d_lora + a_lora, 0), (0, 0))), lora_pad, 0).astype(BF16)
    rw_params = [mu, row(rw_w0[l]), w2p, row(rw_a0[l]), a2p, g2p,
                 row(rw_k_k[l]), row(rw_k_a[l]), row(rw_r_k[l]), seg_rw]
    qw = row(jnp.tile(fx_q_norm[l], fx_heads))
    kw = row(jnp.tile(fx_k_norm[l], fx_heads))
    fx_params = [_pad_to(row(fx_b_f[l]), PAIR, 1), qw, kw, segm_fx, tril]
    h0, r, lw, k2, v, kk, alr, g, bonus, qa, ka, va, og = _front(
        x2, B, S, row(ln_in_w), row(ln_in_b), w_main, _pad_to(wfz_hi, PAIR, 1), rw_params, fx_params,
        rw_w, fx_w, fx_heads)

    y_rw = _wkv(r, lw, k2, v, kk, alr, g, bonus, row(rw_gn_w[l]), row(rw_gn_b[l]), tri_c, segm_rw, B, S)
    y_fx = _fox_attn(qa, ka, va, og, B, S, fx_w)

    wo = w_o[l].astype(BF16)
    rw_pad = _pad_to(router_w[l], ne_pad, 1)
    rb_pad = _pad_to(row(router_b[l]), ne_pad, 1)
    h1, logits = _out_ln(y_rw, y_fx, h0, wo[:rw_w], wo[rw_w:], row(ln1_w[l]), row(ln1_b[l]),
                         rw_pad, rb_pad, alpha)

    gates, pos, gather_tok, block_exp, n_used, x_base, x_rows, n_blocks = _route(logits, n_experts, MOE_TILE)
    xs = h1[gather_tok]
    b1 = exp_b1[l]
    b1g = b1[:, None, 0::2]
    b1l = b1[:, None, 1::2]
    eo_rows = _moe_ffn(block_exp, n_used, x_base, x_rows, n_blocks, xs, exp_w1[l], b1g, b1l, exp_w2[l],
                       exp_b2[l][:, None, :])
    eo = eo_rows[pos.T]
    h = _combine_ln(h1, eo, gates, row(ln2_w[l]), row(ln2_b[l]), alpha)
    return h.reshape(B, S, D)
```

```python
import functools

import jax
import jax.numpy as jnp
from jax import lax
from jax.experimental import pallas as pl
from jax.experimental.pallas import tpu as pltpu

F32 = jnp.float32
BF16 = jnp.bfloat16

HEAD_DIM = 64
PAIR = 2 * HEAD_DIM
WKV_CHUNK = 64
RW_GN_EPS = 64e-5
QK_EPS = 1e-6
LN_EPS = 1e-5
TOP_K = 4
SWIGLU_ALPHA = 1.702
SWIGLU_LIMIT = 7.0
NEG_BIG = -1e30
LOG2E = 1.4426950408889634
N_PIECES = 3

ROW_TILE = 256
LN_ROW_TILE = 1024
ATTN_TQ = 512
ATTN_TK = 512
ATTN_HEADS = 8
MOE_TILE = 512
ROW_ALIGN = 8
WKV_BATCH = 4
VMEM_LIMIT = 48 * 1024 * 1024
FRONT_VMEM_LIMIT = 56 * 1024 * 1024


def _cparams(sem):
    return pltpu.CompilerParams(dimension_semantics=sem, vmem_limit_bytes=VMEM_LIMIT)


def _dot(a, b):
    return jnp.dot(a.astype(BF16), b.astype(BF16), preferred_element_type=F32)


def _dot_t(a, b):
    return lax.dot_general(a.astype(BF16), b.astype(BF16), (((1,), (1,)), ((), ())),
                           preferred_element_type=F32)


def _segsum(x, seg):
    return jnp.dot(x.astype(BF16), seg, preferred_element_type=F32)


def _split3(x):
    hi = x.astype(BF16).astype(F32)
    mid = (x - hi).astype(BF16).astype(F32)
    return hi, mid, x - hi - mid


def _sigmoid(x):
    return 1.0 / (1.0 + jnp.exp(-x))


def _softplus(x):
    return jnp.maximum(x, 0.0) + jnp.log(1.0 + jnp.exp(-jnp.abs(x)))


def _layer_norm(x, w, b):
    mu = jnp.mean(x, axis=-1, keepdims=True)
    xc = x - mu
    var = jnp.mean(xc * xc, axis=-1, keepdims=True)
    return xc * lax.rsqrt(var + LN_EPS) * w + b


def _pack_bf16_pairs(x):
    n = x.shape[1] // 2
    bits = pltpu.bitcast(x.astype(BF16).astype(F32), jnp.uint32)
    return lax.shift_right_logical(bits[:, :n], jnp.uint32(16)) | bits[:, n:]


def _unpack_bf16_pairs(u):
    lo = pltpu.bitcast(lax.shift_left(u, jnp.uint32(16)), F32)
    hi = pltpu.bitcast(u & jnp.uint32(0xFFFF0000), F32)
    return jnp.concatenate([lo, hi], axis=1)


def _stack_heads(x):
    lane = lax.broadcasted_iota(jnp.int32, x.shape, 1)
    return jnp.concatenate([jnp.where(lane < HEAD_DIM, x, 0.0), jnp.where(lane >= HEAD_DIM, x, 0.0)], axis=0)


def _wkv_kernel(r_ref, lw_ref, k_ref, v_ref, kk_ref, alr_ref, g_ref, bonus_ref, gnw_ref, gnb_ref, tri_ref,
                segm_ref, y_ref, state_ref):
    C = WKV_CHUNK
    nb, _, width = lw_ref.shape
    npair = width // PAIR

    @pl.when(pl.program_id(1) == 0)
    def _():
        state_ref[...] = jnp.zeros_like(state_ref)

    ri = lax.broadcasted_iota(jnp.int32, (2 * C, 2 * C), 0)
    ci = lax.broadcasted_iota(jnp.int32, (2 * C, 2 * C), 1)
    same = (ri // C) == (ci // C)
    strict = same & ((ci % C) < (ri % C))
    incl = same & ((ci % C) <= (ri % C))
    eye = (ri == ci).astype(F32)

    lhs, rhs, a2s, r2s, v2s, bhts, bkts, ptots = [], [], [], [], [], [], [], []
    for bi in range(nb):
        lw = lw_ref[bi]
        cum = sum(jnp.dot(tri_ref[...], piece.astype(BF16), preferred_element_type=F32) for piece in _split3(lw))
        total = cum[C - 1:C, :]
        p_inv = jnp.exp(-cum)
        p_rem = jnp.exp(total - cum)
        p_tot = jnp.exp(total)
        kk = kk_ref[bi]
        k2 = k_ref[bi]
        b = kk * alr_ref[bi]
        a_t = -kk * jnp.exp(cum - lw)
        r_t = r_ref[bi] * jnp.exp(cum)
        b_t = b * p_inv
        k_t = k2 * p_inv
        b_h = b * p_rem
        k_h = k2 * p_rem
        v = v_ref[bi]
        for p in range(npair):
            sl = slice(p * PAIR, (p + 1) * PAIR)
            a2, r2, b2, kt2 = (_stack_heads(t[:, sl]) for t in (a_t, r_t, b_t, k_t))
            bh2, kh2, v2 = (_stack_heads(t[:, sl]) for t in (b_h, k_h, v))
            lhs.append(jnp.concatenate([a2, r2], axis=0))
            rhs.append(jnp.concatenate([b2, kt2], axis=0))
            a2s.append(a2)
            r2s.append(r2)
            v2s.append(v2)
            bhts.append(bh2.T)
            bkts.append(jnp.concatenate([bh2.T, kh2.T], axis=1))
            ptots.append(p_tot[:, sl])

    chains = range(nb * npair)
    m = [_dot_t(lhs[c], rhs[c]) for c in chains]
    n_ab = [jnp.where(strict, m[c][:2 * C, :2 * C], 0.0) for c in chains]
    m_ak = [jnp.where(strict, m[c][:2 * C, 2 * C:], 0.0) for c in chains]
    m_rb = [jnp.where(incl, m[c][2 * C:, :2 * C], 0.0) for c in chains]
    m_rk = [jnp.where(incl, m[c][2 * C:, 2 * C:], 0.0) for c in chains]
    mv = [_dot(m_ak[c], v2s[c]) for c in chains]
    mrkv = [_dot(m_rk[c], v2s[c]) for c in chains]
    inv = [eye + n_ab[c] for c in chains]
    pw = n_ab
    for _ in range(C.bit_length() - 2):
        pw = [_dot(pw[c], pw[c]) for c in chains]
        inv = [inv[c] + _dot(inv[c], pw[c]) for c in chains]
    wu = [_dot(inv[c], jnp.concatenate([a2s[c], mv[c]], axis=1)) for c in chains]
    qy = [_dot(m_rb[c], wu[c]) + jnp.concatenate([r2s[c], mrkv[c]], axis=1) for c in chains]
    g_t = [_dot(bhts[c], wu[c][:, :PAIR]) + eye * ptots[c] for c in chains]
    h_t = [_dot(bkts[c], jnp.concatenate([wu[c][:, PAIR:], v2s[c]], axis=0)) for c in chains]
    s0 = [state_ref[c] for c in chains]
    y2 = [_dot(qy[c][:, :PAIR], s0[c]) + qy[c][:, PAIR:] for c in chains]
    for c in chains:
        state_ref[c] = _dot(g_t[c], s0[c]) + h_t[c]

    segm = segm_ref[...]
    for bi in range(nb):
        y = jnp.concatenate([y2[bi * npair + p][:C] + y2[bi * npair + p][C:] for p in range(npair)], axis=1)
        mean = _segsum(y, segm)
        yc = y - mean
        var = _segsum(yc * yc, segm)
        yn = yc * lax.rsqrt(var + RW_GN_EPS) * gnw_ref[...] + gnb_ref[...]
        y_ref[bi] = ((yn + bonus_ref[bi]) * g_ref[bi]).astype(y_ref.dtype)


def _wkv(r, lw, k2, v, kk, alr, g, bonus, gnw, gnb, tri, segm, B, S):
    T, width = r.shape
    C = WKV_CHUNK
    nb = WKV_BATCH
    nc = S // C
    fixed = lambda b, c: (0, 0)
    blk = pl.BlockSpec((nb, C, width), lambda b, c: (b, c, 0))
    vec = pl.BlockSpec((1, width), fixed)
    ins = [t.reshape(B, S, width) for t in (r, lw, k2, v, kk, alr, g, bonus)]
    y = pl.pallas_call(
        _wkv_kernel,
        grid=(B // nb, nc),
        in_specs=[blk] * 8 + [vec, vec, pl.BlockSpec((C, C), fixed), pl.BlockSpec((width, width), fixed)],
        out_specs=blk,
        out_shape=jax.ShapeDtypeStruct((B, S, width), BF16),
        scratch_shapes=[pltpu.VMEM((nb * width // PAIR, PAIR, PAIR), F32)],
        compiler_params=_cparams(("parallel", "arbitrary")),
        name="wkv",
    )(*ins, gnw, gnb, tri, segm)
    return y.reshape(T, width)


def _pieces(x, n):
    hi, mid, lo = _split3(x)
    return hi + pltpu.roll(mid, n, axis=1) + pltpu.roll(lo, 2 * n, axis=1)


def _spread_heads(x):
    lane = lax.broadcasted_iota(jnp.int32, (x.shape[0], PAIR), 1)
    groups = []
    for p in range(x.shape[1] // PAIR):
        blk = x[:, p * PAIR:(p + 1) * PAIR]
        groups.append(jnp.where(lane < HEAD_DIM, blk, 0.0))
        groups.append(jnp.where(lane < HEAD_DIM, pltpu.roll(blk, HEAD_DIM, axis=1), 0.0))
    return jnp.concatenate(groups, axis=1)


def _front_kernel(x_ref, lnw_ref, lnb_ref, w_ref, wfz_ref,
                  mu_ref, w0_ref, w2_ref, a0_ref, a2_ref, g2_ref, kkw_ref, kaw_ref, rk_ref, seg_ref,
                  bf_ref, qw_ref, kw_ref, segm_ref, tril_ref, place_ref, maskq_ref, maskk_ref,
                  oneq_ref, onek_ref, onev_ref,
                  h_ref, r_ref, lw_ref, k_ref, v_ref, kk_ref, alr_ref, g_ref, bonus_ref,
                  qa_ref, ka_ref, va_ref, og_ref, shift_carry, c_carry, *, width, nh):
    @pl.when(pl.program_id(1) == 0)
    def _():
        shift_carry[...] = jnp.zeros_like(shift_carry)
        c_carry[...] = jnp.zeros_like(c_carry)

    h = _layer_norm(x_ref[...], lnw_ref[...], lnb_ref[...])
    h_ref[...] = h
    proj = jnp.dot(h.astype(BF16), w_ref[...], preferred_element_type=F32)
    tm = proj.shape[0]
    n_rw = mu_ref.shape[1]

    p = proj[:, :n_rw]
    prev = pltpu.roll(p, 1, axis=0)
    first_row = lax.broadcasted_iota(jnp.int32, p.shape, 0) == 0
    prev = jnp.where(first_row, shift_carry[...], prev)
    shift_carry[...] = p[tm - 1:tm, :]
    ps = p + mu_ref[...] * (prev - p)
    r = ps[:, 0:width]
    k = ps[:, width:2 * width]
    v = ps[:, 2 * width:3 * width]
    lora = ps[:, 3 * width:]
    seg = seg_ref[...]
    wl = w0_ref[...] + _dot(jnp.tanh(lora), w2_ref[...])
    w_raw = -_softplus(-wl) - 0.5
    lw_ref[...] = -jnp.exp(w_raw)
    alr = _sigmoid(a0_ref[...] + _dot(lora, a2_ref[...]))
    g_ref[...] = _dot(_sigmoid(lora), g2_ref[...])
    kkp = k * kkw_ref[...]
    nrm = jnp.sqrt(_segsum(kkp * kkp, seg))
    kk_ref[...] = kkp / jnp.maximum(nrm, 1e-12)
    k2 = k * (1.0 + (alr - 1.0) * kaw_ref[...])
    bonus_ref[...] = _segsum(r * k2 * rk_ref[...], seg) * v
    r_ref[...] = r
    k_ref[...] = k2
    v_ref[...] = v
    alr_ref[...] = alr

    fw = qw_ref.shape[1]
    q = proj[:, n_rw:n_rw + fw]
    kx = proj[:, n_rw + fw:n_rw + 2 * fw]
    vx = proj[:, n_rw + 2 * fw:n_rw + 3 * fw]
    og_ref[...] = proj[:, n_rw + 3 * fw:n_rw + 4 * fw]
    segm = segm_ref[...]
    qn = q * lax.rsqrt(_segsum(q * q, segm) + QK_EPS) * (qw_ref[...] * (HEAD_DIM ** -0.5 * LOG2E))
    kn = kx * lax.rsqrt(_segsum(kx * kx, segm) + QK_EPS) * kw_ref[...]
    fblk = proj[:, n_rw + 4 * fw:]
    h_lo = (h - h.astype(BF16).astype(F32)).astype(BF16)
    fz = fblk + pltpu.roll(fblk, PAIR - nh, axis=1) + jnp.dot(h_lo, wfz_ref[...], preferred_element_type=F32)
    in_heads = lax.broadcasted_iota(jnp.int32, fz.shape, 1) < nh
    lf = jnp.where(in_heads, -_softplus(-(fz + bf_ref[...])), 0.0)
    c3 = jnp.dot(tril_ref[...], _pieces(lf, nh).astype(BF16), preferred_element_type=F32)
    c = c3 + pltpu.roll(c3, PAIR - nh, axis=1) + pltpu.roll(c3, PAIR - 2 * nh, axis=1)
    c = jnp.where(in_heads, c, 0.0) + c_carry[...]
    c_carry[...] = c[tm - 1:tm, :]
    placed = jnp.dot(_pieces(c * LOG2E, nh).astype(BF16), place_ref[...], preferred_element_type=F32)
    qa_ref[...] = (_spread_heads(qn) + placed * maskq_ref[...] + oneq_ref[...]).astype(qa_ref.dtype)
    ka_ref[...] = (_spread_heads(kn) - placed * maskk_ref[...] + onek_ref[...]).astype(ka_ref.dtype)
    va_ref[...] = (_spread_heads(vx) + onev_ref[...]).astype(va_ref.dtype)


def _front(x2, B, S, lnw, lnb, w_main, wfz_hi, rw_params, fx_params, rw_w, fx_w, nh):
    T, D = x2.shape
    wide = nh * PAIR
    tm = ROW_TILE
    ns = S // tm
    row = lambda b, s: (b * ns + s, 0)
    fixed = lambda b, s: (0, 0)
    const = lambda a: pl.BlockSpec(a.shape, fixed)

    lane = jnp.arange(wide) % PAIR
    head = jnp.arange(wide) // PAIR
    src = jnp.arange(PAIR)
    k_end = HEAD_DIM + N_PIECES
    q_end = k_end + N_PIECES
    piece = (lane - HEAD_DIM) % N_PIECES
    place = ((src[:, None] == (piece * nh + head)[None, :]) & (src[:, None] < N_PIECES * nh)
             & (lane >= HEAD_DIM)[None, :] & (lane < q_end)[None, :]).astype(BF16)
    mask_k = ((lane >= HEAD_DIM) & (lane < k_end)).astype(F32).reshape(1, wide)
    mask_q = ((lane >= k_end) & (lane < q_end)).astype(F32).reshape(1, wide)
    one_v = (lane >= HEAD_DIM).astype(F32).reshape(1, wide)
    consts = [lnw, lnb, w_main, wfz_hi, *rw_params, *fx_params, place, mask_q, mask_k, mask_k, mask_q, one_v]

    f32_out = lambda n: (pl.BlockSpec((tm, n), row), jax.ShapeDtypeStruct((T, n), F32))
    bf_out = lambda n: (pl.BlockSpec((tm, n), row), jax.ShapeDtypeStruct((T, n), BF16))
    outs = [f32_out(D)] + [f32_out(rw_w)] * 8 + [bf_out(wide)] * 3 + [f32_out(fx_w)]
    return pl.pallas_call(
        functools.partial(_front_kernel, width=rw_w, nh=nh),
        grid=(B, ns),
        in_specs=[pl.BlockSpec((tm, D), row)] + [const(a) for a in consts],
        out_specs=[o[0] for o in outs],
        out_shape=[o[1] for o in outs],
        scratch_shapes=[pltpu.VMEM((1, rw_params[0].shape[1]), F32), pltpu.VMEM((1, PAIR), F32)],
        compiler_params=pltpu.CompilerParams(dimension_semantics=("parallel", "arbitrary"),
                                             vmem_limit_bytes=FRONT_VMEM_LIMIT),
        name="front",
    )(x2, *consts)


def _fox_attn_kernel(qi_ref, kj_ref, q_ref, k_ref, v_ref, og_ref, o_ref, m_ref, acc_ref):
    i = qi_ref[pl.program_id(2)]
    j = kj_ref[pl.program_id(2)]
    tq = q_ref.shape[0]
    tk = k_ref.shape[0]
    heads = range(ATTN_HEADS)
    grp = lambda ref, h: ref[:, h * PAIR:(h + 1) * PAIR]

    @pl.when(j == 0)
    def _():
        m_ref[...] = jnp.full_like(m_ref, NEG_BIG)
        acc_ref[...] = jnp.zeros_like(acc_ref)

    def step(masked):
        s = [lax.dot_general(grp(q_ref, h), grp(k_ref, h), (((1,), (1,)), ((), ())), preferred_element_type=F32)
             for h in heads]
        if masked:
            row = lax.broadcasted_iota(jnp.int32, (tq, tk), 0)
            col = lax.broadcasted_iota(jnp.int32, (tq, tk), 1)
            s = [jnp.where(col <= row, s[h], NEG_BIG) for h in heads]
        m_old = [m_ref[h] for h in heads]
        m_new = [jnp.maximum(m_old[h], jnp.max(s[h], axis=1, keepdims=True)) for h in heads]
        alpha = [jnp.exp2(m_old[h] - m_new[h]) for h in heads]
        pr = [jnp.exp2(s[h] - jnp.concatenate([m_new[h]] * (tk // PAIR), axis=1)).astype(BF16) for h in heads]
        pv = [jnp.dot(pr[h], grp(v_ref, h), preferred_element_type=F32) for h in heads]
        for h in heads:
            acc_ref[h] = alpha[h] * acc_ref[h] + pv[h]
            m_ref[h] = m_new[h]

    @pl.when(j < i)
    def _():
        step(False)

    @pl.when(j == i)
    def _():
        step(True)
        lane = lax.broadcasted_iota(jnp.int32, (tq, PAIR), 1)
        outs = []
        for p in range(ATTN_HEADS // 2):
            a0 = acc_ref[2 * p]
            a1 = acc_ref[2 * p + 1]
            o0 = a0 * pltpu.roll(1.0 / a0, HEAD_DIM, axis=1)
            o1 = pltpu.roll(a1, HEAD_DIM, axis=1) * (1.0 / a1)
            outs.append(jnp.where(lane < HEAD_DIM, o0, o1))
        o = jnp.concatenate(outs, axis=1)
        o_ref[...] = (o * _sigmoid(og_ref[...])).astype(o_ref.dtype)


def _fox_attn(qa, ka, va, og, B, S, width):
    T = qa.shape[0]
    tq, tk = ATTN_TQ, ATTN_TK
    assert tq == tk
    nq = S // tq
    nh = ATTN_HEADS
    ow = nh * HEAD_DIM
    ngroup = width // ow
    pairs = [(i, j) for i in range(nq) for j in range(i + 1)]
    qi = jnp.array([ij[0] for ij in pairs], jnp.int32)
    kj = jnp.array([ij[1] for ij in pairs], jnp.int32)
    kv = pl.BlockSpec((tk, nh * PAIR), lambda b, p, t, qi, kj: (b * nq + kj[t], p))
    grid_spec = pltpu.PrefetchScalarGridSpec(
        num_scalar_prefetch=2,
        grid=(B, ngroup, len(pairs)),
        in_specs=[
            pl.BlockSpec((tq, nh * PAIR), lambda b, p, t, qi, kj: (b * nq + qi[t], p)), kv, kv,
            pl.BlockSpec((tq, ow), lambda b, p, t, qi, kj: (b * nq + qi[t], p)),
        ],
        out_specs=pl.BlockSpec((tq, ow), lambda b, p, t, qi, kj: (b * nq + qi[t], p)),
        scratch_shapes=[pltpu.VMEM((nh, tq, PAIR), F32), pltpu.VMEM((nh, tq, PAIR), F32)],
    )
    return pl.pallas_call(
        _fox_attn_kernel,
        grid_spec=grid_spec,
        out_shape=jax.ShapeDtypeStruct((T, width), BF16),
        compiler_params=_cparams(("parallel", "parallel", "arbitrary")),
        name="fox_attn",
    )(qi, kj, qa, ka, va, og)


def _out_ln_kernel(yrw_ref, yfx_ref, h_ref, wo1_ref, wo2_ref, lnw_ref, lnb_ref, rw_ref, rb_ref,
                   h1_ref, logit_ref, *, alpha):
    mix = (jnp.dot(yrw_ref[...], wo1_ref[...], preferred_element_type=F32)
           + jnp.dot(yfx_ref[...], wo2_ref[...], preferred_element_type=F32))
    h1 = _layer_norm(alpha * h_ref[...] + mix, lnw_ref[...], lnb_ref[...])
    h1_ref[...] = h1
    ne = rb_ref.shape[1]
    h_hi = h1.astype(BF16)
    h_lo = (h1 - h_hi.astype(F32)).astype(BF16)
    rw = rw_ref[...]
    first = jnp.dot(h_hi, rw, preferred_element_type=F32)
    second = jnp.dot(h_lo, rw[:, :ne], preferred_element_type=F32)
    logit_ref[...] = first[:, :ne] + first[:, ne:] + second + rb_ref[...]


def _out_ln(y_rw, y_fx, h0, wo1, wo2, lnw, lnb, rw, rb, alpha):
    T, D = h0.shape
    width = y_rw.shape[1]
    ne = rb.shape[1]
    tm = LN_ROW_TILE
    row = lambda i: (i, 0)
    fixed = lambda i: (0, 0)
    rw_hi = rw.astype(BF16)
    rw_lo = (rw - rw_hi.astype(F32)).astype(BF16)
    rw = jnp.concatenate([rw_hi, rw_lo], axis=1)
    return pl.pallas_call(
        functools.partial(_out_ln_kernel, alpha=alpha),
        grid=(T // tm,),
        in_specs=[
            pl.BlockSpec((tm, width), row), pl.BlockSpec((tm, width), row), pl.BlockSpec((tm, D), row),
            pl.BlockSpec((width, D), fixed), pl.BlockSpec((width, D), fixed),
            pl.BlockSpec((1, D), fixed), pl.BlockSpec((1, D), fixed),
            pl.BlockSpec((D, 2 * ne), fixed), pl.BlockSpec((1, ne), fixed),
        ],
        out_specs=[pl.BlockSpec((tm, D), row), pl.BlockSpec((tm, ne), row)],
        out_shape=[jax.ShapeDtypeStruct((T, D), F32), jax.ShapeDtypeStruct((T, ne), F32)],
        compiler_params=_cparams(("parallel",)),
        name="out_ln",
    )(y_rw, y_fx, h0, wo1, wo2, lnw, lnb, rw, rb)


DEINT_COLS = 256
MOE_VMEM_LIMIT = 56 * 1024 * 1024


def _moe_kernel(bexp_ref, nused_ref, first_ref, nexte_ref, slot_ref, xbase_ref, xrows_ref,
                x_hbm, w1_hbm, w2_hbm, b1g_ref, b1l_ref, b2_ref, perm_ref, o_ref,
                w1_stage, w2_stage, w1g_bf, w1l_bf, w2_bf, x_buf, sem, sem_x):
    i = pl.program_id(0)
    live = i < nused_ref[0]
    tm = x_buf.shape[1]

    def x_copy(blk, s):
        row0 = pl.multiple_of(xbase_ref[blk], ROW_ALIGN)
        return pltpu.make_async_copy(x_hbm.at[pl.ds(row0, tm), :], x_buf.at[s], sem_x.at[s])

    @pl.when(live & (i == 0))
    def _():
        x_copy(0, 0).start()

    def weight_copies(e, s):
        return (pltpu.make_async_copy(w1_hbm.at[e], w1_stage.at[s], sem.at[0, s]),
                pltpu.make_async_copy(w2_hbm.at[e], w2_stage.at[s], sem.at[1, s]))

    @pl.when(live & (i == 0))
    def _():
        for cp in weight_copies(bexp_ref[0], 0):
            cp.start()

    @pl.when(live & (first_ref[i] == 1))
    def _():
        s = slot_ref[i]
        for cp in weight_copies(bexp_ref[i], s):
            cp.wait()

        @pl.when(nexte_ref[i] >= 0)
        def _():
            for cp in weight_copies(nexte_ref[i], 1 - s):
                cp.start()

        half = DEINT_COLS // 2
        for c in range(w1_stage.shape[2] // DEINT_COLS):
            blk = w1_stage[s, :, c * DEINT_COLS:(c + 1) * DEINT_COLS].astype(BF16)
            out = jnp.dot(blk, perm_ref[...], preferred_element_type=F32).astype(BF16)
            w1g_bf[:, c * half:(c + 1) * half] = out[:, :half]
            w1l_bf[:, c * half:(c + 1) * half] = out[:, half:]
        w2_bf[...] = w2_stage[s].astype(BF16)

    @pl.when(live)
    def _():
        xs = i % 2
        x_copy(i, xs).wait()

        @pl.when(i + 1 < nused_ref[0])
        def _():
            x_copy(i + 1, 1 - xs).start()

        def ffn(x):
            x = x.astype(BF16)
            x_glu = jnp.minimum(jnp.dot(x, w1g_bf[...], preferred_element_type=F32) + b1g_ref[0], SWIGLU_LIMIT)
            x_lin = jnp.clip(jnp.dot(x, w1l_bf[...], preferred_element_type=F32) + b1l_ref[0],
                             -SWIGLU_LIMIT, SWIGLU_LIMIT)
            act = x_glu * _sigmoid(SWIGLU_ALPHA * x_glu) * (x_lin + 1.0)
            return _pack_bf16_pairs(jnp.dot(act.astype(BF16), w2_bf[...], preferred_element_type=F32) + b2_ref[0])

        half = tm // 2
        short = xrows_ref[i] <= half

        @pl.when(short)
        def _():
            o_ref[:half, :] = ffn(x_buf[xs, :half, :])
            o_ref[half:, :] = jnp.zeros((tm - half, o_ref.shape[1]), o_ref.dtype)

        @pl.when(jnp.logical_not(short))
        def _():
            o_ref[...] = ffn(x_buf[xs])

    @pl.when(jnp.logical_not(live))
    def _():
        o_ref[...] = jnp.zeros_like(o_ref)


def _moe_ffn(block_exp, n_used, x_base, x_rows, n_blocks, xs, w1, b1g, b1l, w2, b2):
    E, D, F2 = w1.shape
    F = F2 // 2
    tm = MOE_TILE
    n_rows = n_blocks * tm

    idx = jnp.arange(n_blocks, dtype=jnp.int32)
    first = jnp.concatenate([jnp.ones((1,), jnp.bool_), block_exp[1:] != block_exp[:-1]])
    slot = ((jnp.cumsum(first.astype(jnp.int32)) - 1) % 2).astype(jnp.int32)
    cand = jnp.where(first & (idx < n_used[0]), idx, n_blocks)
    next_first = jnp.min(jnp.where(idx[None, :] > idx[:, None], cand[None, :], n_blocks), axis=1)
    next_e = jnp.where(next_first < n_blocks, block_exp[jnp.minimum(next_first, n_blocks - 1)], -1).astype(jnp.int32)

    half = DEINT_COLS // 2
    src = jnp.arange(DEINT_COLS)
    dst = jnp.where(src % 2 == 0, src // 2, half + src // 2)
    perm = (dst[:, None] == jnp.arange(DEINT_COLS)[None, :]).astype(BF16)

    live = lambda i, be, nu: jnp.minimum(i, nu[0] - 1)
    bspec = lambda n: pl.BlockSpec((1, 1, n), lambda i, be, nu, fi, ne, sl, xb, xr: (be[live(i, be, nu)], 0, 0))
    grid_spec = pltpu.PrefetchScalarGridSpec(
        num_scalar_prefetch=7,
        grid=(n_blocks,),
        in_specs=[
            pl.BlockSpec(memory_space=pl.ANY),
            pl.BlockSpec(memory_space=pl.ANY),
            pl.BlockSpec(memory_space=pl.ANY),
            bspec(F), bspec(F), bspec(D),
            pl.BlockSpec((DEINT_COLS, DEINT_COLS), lambda i, be, nu, fi, ne, sl, xb, xr: (0, 0)),
        ],
        out_specs=pl.BlockSpec((tm, D // 2), lambda i, be, nu, fi, ne, sl, xb, xr: (i, 0)),
        scratch_shapes=[
            pltpu.VMEM((2, D, F2), F32), pltpu.VMEM((2, F, D), F32),
            pltpu.VMEM((D, F), BF16), pltpu.VMEM((D, F), BF16), pltpu.VMEM((F, D), BF16),
            pltpu.VMEM((2, tm, D), F32),
            pltpu.SemaphoreType.DMA((2, 2)), pltpu.SemaphoreType.DMA((2,)),
        ],
    )
    return pl.pallas_call(
        _moe_kernel,
        grid_spec=grid_spec,
        out_shape=jax.ShapeDtypeStruct((n_rows, D // 2), jnp.uint32),
        compiler_params=pltpu.CompilerParams(dimension_semantics=("arbitrary",), vmem_limit_bytes=MOE_VMEM_LIMIT),
        name="moe_ffn",
    )(block_exp, n_used, first.astype(jnp.int32), next_e, slot, x_base, x_rows, xs, w1, w2, b1g, b1l, b2, perm)


def _combine_ln_kernel(h_ref, eo_ref, gate_ref, lnw_ref, lnb_ref, o_ref, *, alpha):
    gates = gate_ref[...]
    ffn = _unpack_bf16_pairs(eo_ref[0]) * gates[:, 0:1]
    for j in range(1, TOP_K):
        ffn = ffn + _unpack_bf16_pairs(eo_ref[j]) * gates[:, j:j + 1]
    o_ref[...] = _layer_norm(alpha * h_ref[...] + ffn, lnw_ref[...], lnb_ref[...])


def _combine_ln(h1, eo, gates, lnw, lnb, alpha):
    T, D = h1.shape
    tm = LN_ROW_TILE
    return pl.pallas_call(
        functools.partial(_combine_ln_kernel, alpha=alpha),
        grid=(T // tm,),
        in_specs=[
            pl.BlockSpec((tm, D), lambda i: (i, 0)),
            pl.BlockSpec((TOP_K, tm, D // 2), lambda i: (0, i, 0)),
            pl.BlockSpec((tm, TOP_K), lambda i: (i, 0)),
            pl.BlockSpec((1, D), lambda i: (0, 0)),
            pl.BlockSpec((1, D), lambda i: (0, 0)),
        ],
        out_specs=pl.BlockSpec((tm, D), lambda i: (i, 0)),
        out_shape=jax.ShapeDtypeStruct((T, D), F32),
        compiler_params=_cparams(("parallel",)),
        name="combine_ln",
    )(h1, eo, gates, lnw, lnb)


def _pad_to(x, n, axis):
    pad = [(0, 0)] * x.ndim
    pad[axis] = (0, n - x.shape[axis])
    return jnp.pad(x, pad)


def _block_diag_ones(width, value=1.0):
    idx = jnp.arange(width) // HEAD_DIM
    return jnp.where(idx[:, None] == idx[None, :], value, 0.0).astype(BF16)


def _route(logits, n_experts, tile):
    T = logits.shape[0]
    top_val, top_idx = lax.top_k(logits[:, :n_experts], TOP_K)
    gates = jax.nn.softmax(top_val, axis=-1)
    e_flat = top_idx.reshape(-1).astype(jnp.int32)
    n_assign = T * TOP_K
    assert n_assign % tile == 0 and tile % ROW_ALIGN == 0 and n_experts * n_assign < 2 ** 31
    n_blocks = n_assign // tile + n_experts + 1
    last_base = n_assign - tile
    eids = jnp.arange(n_experts, dtype=jnp.int32)
    aids = jnp.arange(n_assign, dtype=jnp.int32)
    skeys = lax.sort(e_flat * n_assign + aids)
    order = skeys % n_assign
    e_sorted = skeys // n_assign
    counts = jnp.sum((e_flat[:, None] == eids[None, :]).astype(jnp.int32), axis=0)
    starts = jnp.cumsum(counts) - counts
    base = starts // ROW_ALIGN * ROW_ALIGN
    padded = (starts - base + counts + tile - 1) // tile * tile
    pends = jnp.cumsum(padded)
    pstarts = pends - padded
    of_sorted = lambda per_expert: jnp.sum(
        jnp.where(e_sorted[:, None] == eids[None, :], per_expert[None, :], 0), axis=1)
    base_q = of_sorted(base)
    k_q = (aids - base_q) // tile
    window_q = jnp.minimum(base_q + k_q * tile, last_base)
    dest_sorted = of_sorted(pstarts) + k_q * tile + (aids - window_q)
    _, dest = lax.sort((order, dest_sorted), num_keys=1)
    block_start = jnp.arange(n_blocks, dtype=jnp.int32) * tile
    block_exp = jnp.minimum(jnp.sum((pends[None, :] <= block_start[:, None]).astype(jnp.int32), axis=1),
                            n_experts - 1).astype(jnp.int32)
    n_used = (pends[-1] // tile).astype(jnp.int32).reshape(1)
    of_block = lambda per_expert: jnp.sum(
        jnp.where(block_exp[:, None] == eids[None, :], per_expert[None, :], 0), axis=1)
    x_base = jnp.clip(of_block(base - pstarts) + block_start, 0, last_base).astype(jnp.int32)
    x_rows = jnp.clip(of_block(starts + counts) - x_base, 0, tile).astype(jnp.int32)
    return gates, dest.reshape(T, TOP_K), order // TOP_K, block_exp, n_used, x_base, x_rows, n_blocks


def kernel(x, ln_in_w, ln_in_b, w_in, rw_mu, rw_w0, rw_w2, rw_a0, rw_a2, rw_g2, rw_k_k, rw_k_a, rw_r_k,
           rw_gn_w, rw_gn_b, fx_b_f, fx_q_norm, fx_k_norm, w_o, ln1_w, ln1_b, router_w, router_b,
           exp_w1, exp_b1, exp_w2, exp_b2, ln2_w, ln2_b):
    B, S, D = x.shape
    T = B * S
    depth = w_in.shape[0]
    alpha = (2 * depth) ** 0.25
    rw_w = rw_w0.shape[1]
    fx_heads = fx_b_f.shape[1]
    fx_w = fx_heads * HEAD_DIM
    d_lora, a_lora, g_lora = rw_w2.shape[1], rw_a2.shape[1], rw_g2.shape[1]
    n_lora = d_lora + a_lora + g_lora
    lora_pad = -(-n_lora // 128) * 128
    rw_cols = 3 * rw_w + n_lora
    n_rw = 3 * rw_w + lora_pad
    n_experts = router_w.shape[2]
    ne_pad = -(-n_experts // 128) * 128
    row = lambda a: a.reshape(1, -1)

    seg_rw = _block_diag_ones(rw_w)
    segm_rw = _block_diag_ones(rw_w, 1.0 / HEAD_DIM)
    segm_fx = _block_diag_ones(fx_w, 1.0 / HEAD_DIM)
    tidx = jnp.arange(ROW_TILE)
    tril = (tidx[:, None] >= tidx[None, :]).astype(BF16)
    cidx = jnp.arange(WKV_CHUNK)
    tri_c = (cidx[:, None] >= cidx[None, :]).astype(BF16)

    assert depth == 1, "single-layer block"
    l = 0
    x2 = x.reshape(T, D)
    w_l = w_in.reshape(w_in.shape[1:])
    wfz = w_l[:, rw_cols + 4 * fx_w:]
    wfz_hi = wfz.astype(BF16)
    wfz_lo = (wfz - wfz_hi.astype(F32)).astype(BF16)
    w_main = jnp.concatenate(
        [_pad_to(w_l[:, :rw_cols], n_rw, 1).astype(BF16), w_l[:, rw_cols:rw_cols + 4 * fx_w].astype(BF16),
         _pad_to(jnp.concatenate([wfz_hi, wfz_lo], axis=1), PAIR, 1)], axis=1)
    mu = _pad_to(row(rw_mu[l]), n_rw, 1)
    w2p = _pad_to(rw_w2[l], lora_pad, 0).astype(BF16)
    a2p = _pad_to(jnp.pad(rw_a2[l], ((d_lora, 0), (0, 0))), lora_pad, 0).astype(BF16)
    g2p = _pad_to(jnp.pad(rw_g2[l], ((d_lora + a_lora, 0), (0, 0))), lora_pad, 0).astype(BF16)
    rw_params = [mu, row(rw_w0[l]), w2p, row(rw_a0[l]), a2p, g2p,
                 row(rw_k_k[l]), row(rw_k_a[l]), row(rw_r_k[l]), seg_rw]
    qw = row(jnp.tile(fx_q_norm[l], fx_heads))
    kw = row(jnp.tile(fx_k_norm[l], fx_heads))
    fx_params = [_pad_to(row(fx_b_f[l]), PAIR, 1), qw, kw, segm_fx, tril]
    h0, r, lw, k2, v, kk, alr, g, bonus, qa, ka, va, og = _front(
        x2, B, S, row(ln_in_w), row(ln_in_b), w_main, _pad_to(wfz_hi, PAIR, 1), rw_params, fx_params,
        rw_w, fx_w, fx_heads)

    y_rw = _wkv(r, lw, k2, v, kk, alr, g, bonus, row(rw_gn_w[l]), row(rw_gn_b[l]), tri_c, segm_rw, B, S)
    y_fx = _fox_attn(qa, ka, va, og, B, S, fx_w)

    wo = w_o[l].astype(BF16)
    rw_pad = _pad_to(router_w[l], ne_pad, 1)
    rb_pad = _pad_to(row(router_b[l]), ne_pad, 1)
    h1, logits = _out_ln(y_rw, y_fx, h0, wo[:rw_w], wo[rw_w:], row(ln1_w[l]), row(ln1_b[l]),
                         rw_pad, rb_pad, alpha)

    gates, pos, gather_tok, block_exp, n_used, x_base, x_rows, n_blocks = _route(logits, n_experts, MOE_TILE)
    xs = h1[gather_tok]
    b1 = exp_b1[l]
    b1g = b1[:, None, 0::2]
    b1l = b1[:, None, 1::2]
    eo_rows = _moe_ffn(block_exp, n_used, x_base, x_rows, n_blocks, xs, exp_w1[l], b1g, b1l, exp_w2[l],
                       exp_b2[l][:, None, :])
    eo = eo_rows[pos.T]
    h = _combine_ln(h1, eo, gates, row(ln2_w[l]), row(ln2_b[l]), alpha)
    return h.reshape(B, S, D)
```

```python
import functools

import jax
import jax.numpy as jnp
from jax import lax
from jax.experimental import pallas as pl
from jax.experimental.pallas import tpu as pltpu

F32 = jnp.float32
BF16 = jnp.bfloat16

HEAD_DIM = 64
PAIR = 2 * HEAD_DIM
WKV_CHUNK = 64
RW_GN_EPS = 64e-5
QK_EPS = 1e-6
LN_EPS = 1e-5
TOP_K = 4
SWIGLU_ALPHA = 1.702
SWIGLU_LIMIT = 7.0
NEG_BIG = -1e30
LOG2E = 1.4426950408889634
N_PIECES = 3

ROW_TILE = 256
LN_ROW_TILE = 1024
ATTN_TQ = 512
ATTN_TK = 512
ATTN_HEADS = 8
MOE_TILE = 512
ROW_ALIGN = 8
WKV_BATCH = 4
VMEM_LIMIT = 48 * 1024 * 1024
FRONT_VMEM_LIMIT = 56 * 1024 * 1024


def _cparams(sem):
    return pltpu.CompilerParams(dimension_semantics=sem, vmem_limit_bytes=VMEM_LIMIT)


def _dot(a, b):
    return jnp.dot(a.astype(BF16), b.astype(BF16), preferred_element_type=F32)


def _dot_t(a, b):
    return lax.dot_general(a.astype(BF16), b.astype(BF16), (((1,), (1,)), ((), ())),
                           preferred_element_type=F32)


def _segsum(x, seg):
    return jnp.dot(x.astype(BF16), seg, preferred_element_type=F32)


def _split3(x):
    hi = x.astype(BF16).astype(F32)
    mid = (x - hi).astype(BF16).astype(F32)
    return hi, mid, x - hi - mid


def _sigmoid(x):
    return 1.0 / (1.0 + jnp.exp(-x))


def _softplus(x):
    return jnp.maximum(x, 0.0) + jnp.log(1.0 + jnp.exp(-jnp.abs(x)))


def _layer_norm(x, w, b):
    mu = jnp.mean(x, axis=-1, keepdims=True)
    xc = x - mu
    var = jnp.mean(xc * xc, axis=-1, keepdims=True)
    return xc * lax.rsqrt(var + LN_EPS) * w + b


def _pack_bf16_pairs(x):
    n = x.shape[1] // 2
    bits = pltpu.bitcast(x.astype(BF16).astype(F32), jnp.uint32)
    return lax.shift_right_logical(bits[:, :n], jnp.uint32(16)) | bits[:, n:]


def _unpack_bf16_pairs(u):
    lo = pltpu.bitcast(lax.shift_left(u, jnp.uint32(16)), F32)
    hi = pltpu.bitcast(u & jnp.uint32(0xFFFF0000), F32)
    return jnp.concatenate([lo, hi], axis=1)


def _stack_heads(x):
    lane = lax.broadcasted_iota(jnp.int32, x.shape, 1)
    return jnp.concatenate([jnp.where(lane < HEAD_DIM, x, 0.0), jnp.where(lane >= HEAD_DIM, x, 0.0)], axis=0)


def _wkv_kernel(r_ref, lw_ref, k_ref, v_ref, kk_ref, alr_ref, g_ref, bonus_ref, gnw_ref, gnb_ref, tri_ref,
                segm_ref, y_ref, state_ref):
    C = WKV_CHUNK
    nb, _, width = lw_ref.shape
    npair = width // PAIR

    @pl.when(pl.program_id(1) == 0)
    def _():
        state_ref[...] = jnp.zeros_like(state_ref)

    ri = lax.broadcasted_iota(jnp.int32, (2 * C, 2 * C), 0)
    ci = lax.broadcasted_iota(jnp.int32, (2 * C, 2 * C), 1)
    same = (ri // C) == (ci // C)
    strict = same & ((ci % C) < (ri % C))
    incl = same & ((ci % C) <= (ri % C))
    eye = (ri == ci).astype(F32)

    lhs, rhs, a2s, r2s, v2s, bhts, bkts, ptots = [], [], [], [], [], [], [], []
    for bi in range(nb):
        lw = lw_ref[bi]
        cum = sum(jnp.dot(tri_ref[...], piece.astype(BF16), preferred_element_type=F32) for piece in _split3(lw))
        total = cum[C - 1:C, :]
        p_inv = jnp.exp(-cum)
        p_rem = jnp.exp(total - cum)
        p_tot = jnp.exp(total)
        kk = kk_ref[bi]
        k2 = k_ref[bi]
        b = kk * alr_ref[bi]
        a_t = -kk * jnp.exp(cum - lw)
        r_t = r_ref[bi] * jnp.exp(cum)
        b_t = b * p_inv
        k_t = k2 * p_inv
        b_h = b * p_rem
        k_h = k2 * p_rem
        v = v_ref[bi]
        for p in range(npair):
            sl = slice(p * PAIR, (p + 1) * PAIR)
            a2, r2, b2, kt2 = (_stack_heads(t[:, sl]) for t in (a_t, r_t, b_t, k_t))
            bh2, kh2, v2 = (_stack_heads(t[:, sl]) for t in (b_h, k_h, v))
            lhs.append(jnp.concatenate([a2, r2], axis=0))
            rhs.append(jnp.concatenate([b2, kt2], axis=0))
            a2s.append(a2)
            r2s.append(r2)
            v2s.append(v2)
            bhts.append(bh2.T)
            bkts.append(jnp.concatenate([bh2.T, kh2.T], axis=1))
            ptots.append(p_tot[:, sl])

    chains = range(nb * npair)
    m = [_dot_t(lhs[c], rhs[c]) for c in chains]
    n_ab = [jnp.where(strict, m[c][:2 * C, :2 * C], 0.0) for c in chains]
    m_ak = [jnp.where(strict, m[c][:2 * C, 2 * C:], 0.0) for c in chains]
    m_rb = [jnp.where(incl, m[c][2 * C:, :2 * C], 0.0) for c in chains]
    m_rk = [jnp.where(incl, m[c][2 * C:, 2 * C:], 0.0) for c in chains]
    mv = [_dot(m_ak[c], v2s[c]) for c in chains]
    mrkv = [_dot(m_rk[c], v2s[c]) for c in chains]
    inv = [eye + n_ab[c] for c in chains]
    pw = n_ab
    for _ in range(C.bit_length() - 2):
        pw = [_dot(pw[c], pw[c]) for c in chains]
        inv = [inv[c] + _dot(inv[c], pw[c]) for c in chains]
    wu = [_dot(inv[c], jnp.concatenate([a2s[c], mv[c]], axis=1)) for c in chains]
    qy = [_dot(m_rb[c], wu[c]) + jnp.concatenate([r2s[c], mrkv[c]], axis=1) for c in chains]
    g_t = [_dot(bhts[c], wu[c][:, :PAIR]) + eye * ptots[c] for c in chains]
    h_t = [_dot(bkts[c], jnp.concatenate([wu[c][:, PAIR:], v2s[c]], axis=0)) for c in chains]
    s0 = [state_ref[c] for c in chains]
    y2 = [_dot(qy[c][:, :PAIR], s0[c]) + qy[c][:, PAIR:] for c in chains]
    for c in chains:
        state_ref[c] = _dot(g_t[c], s0[c]) + h_t[c]

    segm = segm_ref[...]
    for bi in range(nb):
        y = jnp.concatenate([y2[bi * npair + p][:C] + y2[bi * npair + p][C:] for p in range(npair)], axis=1)
        mean = _segsum(y, segm)
        yc = y - mean
        var = _segsum(yc * yc, segm)
        yn = yc * lax.rsqrt(var + RW_GN_EPS) * gnw_ref[...] + gnb_ref[...]
        y_ref[bi] = ((yn + bonus_ref[bi]) * g_ref[bi]).astype(y_ref.dtype)


def _wkv(r, lw, k2, v, kk, alr, g, bonus, gnw, gnb, tri, segm, B, S):
    T, width = r.shape
    C = WKV_CHUNK
    nb = WKV_BATCH
    nc = S // C
    fixed = lambda b, c: (0, 0)
    blk = pl.BlockSpec((nb, C, width), lambda b, c: (b, c, 0))
    vec = pl.BlockSpec((1, width), fixed)
    ins = [t.reshape(B, S, width) for t in (r, lw, k2, v, kk, alr, g, bonus)]
    y = pl.pallas_call(
        _wkv_kernel,
        grid=(B // nb, nc),
        in_specs=[blk] * 8 + [vec, vec, pl.BlockSpec((C, C), fixed), pl.BlockSpec((width, width), fixed)],
        out_specs=blk,
        out_shape=jax.ShapeDtypeStruct((B, S, width), BF16),
        scratch_shapes=[pltpu.VMEM((nb * width // PAIR, PAIR, PAIR), F32)],
        compiler_params=_cparams(("parallel", "arbitrary")),
        name="wkv",
    )(*ins, gnw, gnb, tri, segm)
    return y.reshape(T, width)


def _pieces(x, n):
    hi, mid, lo = _split3(x)
    return hi + pltpu.roll(mid, n, axis=1) + pltpu.roll(lo, 2 * n, axis=1)


def _spread_heads(x):
    lane = lax.broadcasted_iota(jnp.int32, (x.shape[0], PAIR), 1)
    groups = []
    for p in range(x.shape[1] // PAIR):
        blk = x[:, p * PAIR:(p + 1) * PAIR]
        groups.append(jnp.where(lane < HEAD_DIM, blk, 0.0))
        groups.append(jnp.where(lane < HEAD_DIM, pltpu.roll(blk, HEAD_DIM, axis=1), 0.0))
    return jnp.concatenate(groups, axis=1)


def _front_kernel(x_ref, lnw_ref, lnb_ref, w_ref, wfz_ref,
                  mu_ref, w0_ref, w2_ref, a0_ref, a2_ref, g2_ref, kkw_ref, kaw_ref, rk_ref, seg_ref,
                  bf_ref, qw_ref, kw_ref, segm_ref, tril_ref, place_ref, maskq_ref, maskk_ref,
                  oneq_ref, onek_ref, onev_ref,
                  h_ref, r_ref, lw_ref, k_ref, v_ref, kk_ref, alr_ref, g_ref, bonus_ref,
                  qa_ref, ka_ref, va_ref, og_ref, shift_carry, c_carry, *, width, nh):
    @pl.when(pl.program_id(1) == 0)
    def _():
        shift_carry[...] = jnp.zeros_like(shift_carry)
        c_carry[...] = jnp.zeros_like(c_carry)

    h = _layer_norm(x_ref[...], lnw_ref[...], lnb_ref[...])
    h_ref[...] = h
    proj = jnp.dot(h.astype(BF16), w_ref[...], preferred_element_type=F32)
    tm = proj.shape[0]
    n_rw = mu_ref.shape[1]

    p = proj[:, :n_rw]
    prev = pltpu.roll(p, 1, axis=0)
    first_row = lax.broadcasted_iota(jnp.int32, p.shape, 0) == 0
    prev = jnp.where(first_row, shift_carry[...], prev)
    shift_carry[...] = p[tm - 1:tm, :]
    ps = p + mu_ref[...] * (prev - p)
    r = ps[:, 0:width]
    k = ps[:, width:2 * width]
    v = ps[:, 2 * width:3 * width]
    lora = ps[:, 3 * width:]
    seg = seg_ref[...]
    wl = w0_ref[...] + _dot(jnp.tanh(lora), w2_ref[...])
    w_raw = -_softplus(-wl) - 0.5
    lw_ref[...] = -jnp.exp(w_raw)
    alr = _sigmoid(a0_ref[...] + _dot(lora, a2_ref[...]))
    g_ref[...] = _dot(_sigmoid(lora), g2_ref[...])
    kkp = k * kkw_ref[...]
    nrm = jnp.sqrt(_segsum(kkp * kkp, seg))
    kk_ref[...] = kkp / jnp.maximum(nrm, 1e-12)
    k2 = k * (1.0 + (alr - 1.0) * kaw_ref[...])
    bonus_ref[...] = _segsum(r * k2 * rk_ref[...], seg) * v
    r_ref[...] = r
    k_ref[...] = k2
    v_ref[...] = v
    alr_ref[...] = alr

    fw = qw_ref.shape[1]
    q = proj[:, n_rw:n_rw + fw]
    kx = proj[:, n_rw + fw:n_rw + 2 * fw]
    vx = proj[:, n_rw + 2 * fw:n_rw + 3 * fw]
    og_ref[...] = proj[:, n_rw + 3 * fw:n_rw + 4 * fw]
    segm = segm_ref[...]
    qn = q * lax.rsqrt(_segsum(q * q, segm) + QK_EPS) * (qw_ref[...] * (HEAD_DIM ** -0.5 * LOG2E))
    kn = kx * lax.rsqrt(_segsum(kx * kx, segm) + QK_EPS) * kw_ref[...]
    fblk = proj[:, n_rw + 4 * fw:]
    h_lo = (h - h.astype(BF16).astype(F32)).astype(BF16)
    fz = fblk + pltpu.roll(fblk, PAIR - nh, axis=1) + jnp.dot(h_lo, wfz_ref[...], preferred_element_type=F32)
    in_heads = lax.broadcasted_iota(jnp.int32, fz.shape, 1) < nh
    lf = jnp.where(in_heads, -_softplus(-(fz + bf_ref[...])), 0.0)
    c3 = jnp.dot(tril_ref[...], _pieces(lf, nh).astype(BF16), preferred_element_type=F32)
    c = c3 + pltpu.roll(c3, PAIR - nh, axis=1) + pltpu.roll(c3, PAIR - 2 * nh, axis=1)
    c = jnp.where(in_heads, c, 0.0) + c_carry[...]
    c_carry[...] = c[tm - 1:tm, :]
    placed = jnp.dot(_pieces(c * LOG2E, nh).astype(BF16), place_ref[...], preferred_element_type=F32)
    qa_ref[...] = (_spread_heads(qn) + placed * maskq_ref[...] + oneq_ref[...]).astype(qa_ref.dtype)
    ka_ref[...] = (_spread_heads(kn) - placed * maskk_ref[...] + onek_ref[...]).astype(ka_ref.dtype)
    va_ref[...] = (_spread_heads(vx) + onev_ref[...]).astype(va_ref.dtype)


def _front(x2, B, S, lnw, lnb, w_main, wfz_hi, rw_params, fx_params, rw_w, fx_w, nh):
    T, D = x2.shape
    wide = nh * PAIR
    tm = ROW_TILE
    ns = S // tm
    row = lambda b, s: (b * ns + s, 0)
    fixed = lambda b, s: (0, 0)
    const = lambda a: pl.BlockSpec(a.shape, fixed)

    lane = jnp.arange(wide) % PAIR
    head = jnp.arange(wide) // PAIR
    src = jnp.arange(PAIR)
    k_end = HEAD_DIM + N_PIECES
    q_end = k_end + N_PIECES
    piece = (lane - HEAD_DIM) % N_PIECES
    place = ((src[:, None] == (piece * nh + head)[None, :]) & (src[:, None] < N_PIECES * nh)
             & (lane >= HEAD_DIM)[None, :] & (lane < q_end)[None, :]).astype(BF16)
    mask_k = ((lane >= HEAD_DIM) & (lane < k_end)).astype(F32).reshape(1, wide)
    mask_q = ((lane >= k_end) & (lane < q_end)).astype(F32).reshape(1, wide)
    one_v = (lane >= HEAD_DIM).astype(F32).reshape(1, wide)
    consts = [lnw, lnb, w_main, wfz_hi, *rw_params, *fx_params, place, mask_q, mask_k, mask_k, mask_q, one_v]

    f32_out = lambda n: (pl.BlockSpec((tm, n), row), jax.ShapeDtypeStruct((T, n), F32))
    bf_out = lambda n: (pl.BlockSpec((tm, n), row), jax.ShapeDtypeStruct((T, n), BF16))
    outs = [f32_out(D)] + [f32_out(rw_w)] * 8 + [bf_out(wide)] * 3 + [f32_out(fx_w)]
    return pl.pallas_call(
        functools.partial(_front_kernel, width=rw_w, nh=nh),
        grid=(B, ns),
        in_specs=[pl.BlockSpec((tm, D), row)] + [const(a) for a in consts],
        out_specs=[o[0] for o in outs],
        out_shape=[o[1] for o in outs],
        scratch_shapes=[pltpu.VMEM((1, rw_params[0].shape[1]), F32), pltpu.VMEM((1, PAIR), F32)],
        compiler_params=pltpu.CompilerParams(dimension_semantics=("parallel", "arbitrary"),
                                             vmem_limit_bytes=FRONT_VMEM_LIMIT),
        name="front",
    )(x2, *consts)


def _fox_attn_kernel(qi_ref, kj_ref, q_ref, k_ref, v_ref, og_ref, o_ref, m_ref, acc_ref):
    i = qi_ref[pl.program_id(2)]
    j = kj_ref[pl.program_id(2)]
    tq = q_ref.shape[0]
    tk = k_ref.shape[0]
    heads = range(ATTN_HEADS)
    grp = lambda ref, h: ref[:, h * PAIR:(h + 1) * PAIR]

    @pl.when(j == 0)
    def _():
        m_ref[...] = jnp.full_like(m_ref, NEG_BIG)
        acc_ref[...] = jnp.zeros_like(acc_ref)

    def step(masked):
        s = [lax.dot_general(grp(q_ref, h), grp(k_ref, h), (((1,), (1,)), ((), ())), preferred_element_type=F32)
             for h in heads]
        if masked:
            row = lax.broadcasted_iota(jnp.int32, (tq, tk), 0)
            col = lax.broadcasted_iota(jnp.int32, (tq, tk), 1)
            s = [jnp.where(col <= row, s[h], NEG_BIG) for h in heads]
        m_old = [m_ref[h] for h in heads]
        m_new = [jnp.maximum(m_old[h], jnp.max(s[h], axis=1, keepdims=True)) for h in heads]
        alpha = [jnp.exp2(m_old[h] - m_new[h]) for h in heads]
        pr = [jnp.exp2(s[h] - jnp.concatenate([m_new[h]] * (tk // PAIR), axis=1)).astype(BF16) for h in heads]
        pv = [jnp.dot(pr[h], grp(v_ref, h), preferred_element_type=F32) for h in heads]
        for h in heads:
            acc_ref[h] = alpha[h] * acc_ref[h] + pv[h]
            m_ref[h] = m_new[h]

    @pl.when(j < i)
    def _():
        step(False)

    @pl.when(j == i)
    def _():
        step(True)
        lane = lax.broadcasted_iota(jnp.int32, (tq, PAIR), 1)
        outs = []
        for p in range(ATTN_HEADS // 2):
            a0 = acc_ref[2 * p]
            a1 = acc_ref[2 * p + 1]
            o0 = a0 * pltpu.roll(1.0 / a0, HEAD_DIM, axis=1)
            o1 = pltpu.roll(a1, HEAD_DIM, axis=1) * (1.0 / a1)
            outs.append(jnp.where(lane < HEAD_DIM, o0, o1))
        o = jnp.concatenate(outs, axis=1)
        o_ref[...] = (o * _sigmoid(og_ref[...])).astype(o_ref.dtype)


def _fox_attn(qa, ka, va, og, B, S, width):
    T = qa.shape[0]
    tq, tk = ATTN_TQ, ATTN_TK
    assert tq == tk
    nq = S // tq
    nh = ATTN_HEADS
    ow = nh * HEAD_DIM
    ngroup = width // ow
    pairs = [(i, j) for i in range(nq) for j in range(i + 1)]
    qi = jnp.array([ij[0] for ij in pairs], jnp.int32)
    kj = jnp.array([ij[1] for ij in pairs], jnp.int32)
    kv = pl.BlockSpec((tk, nh * PAIR), lambda b, p, t, qi, kj: (b * nq + kj[t], p))
    grid_spec = pltpu.PrefetchScalarGridSpec(
        num_scalar_prefetch=2,
        grid=(B, ngroup, len(pairs)),
        in_specs=[
            pl.BlockSpec((tq, nh * PAIR), lambda b, p, t, qi, kj: (b * nq + qi[t], p)), kv, kv,
            pl.BlockSpec((tq, ow), lambda b, p, t, qi, kj: (b * nq + qi[t], p)),
        ],
        out_specs=pl.BlockSpec((tq, ow), lambda b, p, t, qi, kj: (b * nq + qi[t], p)),
        scratch_shapes=[pltpu.VMEM((nh, tq, PAIR), F32), pltpu.VMEM((nh, tq, PAIR), F32)],
    )
    return pl.pallas_call(
        _fox_attn_kernel,
        grid_spec=grid_spec,
        out_shape=jax.ShapeDtypeStruct((T, width), BF16),
        compiler_params=_cparams(("parallel", "parallel", "arbitrary")),
        name="fox_attn",
    )(qi, kj, qa, ka, va, og)


def _out_ln_kernel(yrw_ref, yfx_ref, h_ref, wo1_ref, wo2_ref, lnw_ref, lnb_ref, rw_ref, rb_ref,
                   h1_ref, logit_ref, *, alpha):
    mix = (jnp.dot(yrw_ref[...], wo1_ref[...], preferred_element_type=F32)
           + jnp.dot(yfx_ref[...], wo2_ref[...], preferred_element_type=F32))
    h1 = _layer_norm(alpha * h_ref[...] + mix, lnw_ref[...], lnb_ref[...])
    h1_ref[...] = h1
    ne = rb_ref.shape[1]
    h_hi = h1.astype(BF16)
    h_lo = (h1 - h_hi.astype(F32)).astype(BF16)
    rw = rw_ref[...]
    first = jnp.dot(h_hi, rw, preferred_element_type=F32)
    second = jnp.dot(h_lo, rw[:, :ne], preferred_element_type=F32)
    logit_ref[...] = first[:, :ne] + first[:, ne:] + second + rb_ref[...]


def _out_ln(y_rw, y_fx, h0, wo1, wo2, lnw, lnb, rw, rb, alpha):
    T, D = h0.shape
    width = y_rw.shape[1]
    ne = rb.shape[1]
    tm = LN_ROW_TILE
    row = lambda i: (i, 0)
    fixed = lambda i: (0, 0)
    rw_hi = rw.astype(BF16)
    rw_lo = (rw - rw_hi.astype(F32)).astype(BF16)
    rw = jnp.concatenate([rw_hi, rw_lo], axis=1)
    return pl.pallas_call(
        functools.partial(_out_ln_kernel, alpha=alpha),
        grid=(T // tm,),
        in_specs=[
            pl.BlockSpec((tm, width), row), pl.BlockSpec((tm, width), row), pl.BlockSpec((tm, D), row),
            pl.BlockSpec((width, D), fixed), pl.BlockSpec((width, D), fixed),
            pl.BlockSpec((1, D), fixed), pl.BlockSpec((1, D), fixed),
            pl.BlockSpec((D, 2 * ne), fixed), pl.BlockSpec((1, ne), fixed),
        ],
        out_specs=[pl.BlockSpec((tm, D), row), pl.BlockSpec((tm, ne), row)],
        out_shape=[jax.ShapeDtypeStruct((T, D), F32), jax.ShapeDtypeStruct((T, ne), F32)],
        compiler_params=_cparams(("parallel",)),
        name="out_ln",
    )(y_rw, y_fx, h0, wo1, wo2, lnw, lnb, rw, rb)


DEINT_COLS = 256
FFN_SPLIT = 2
MOE_VMEM_LIMIT = 56 * 1024 * 1024


def _moe_kernel(bexp_ref, nused_ref, first_ref, nexte_ref, slot_ref, xbase_ref, xrows_ref,
                x_hbm, w1_hbm, w2_hbm, b1g_ref, b1l_ref, b2_ref, perm_ref, o_ref,
                w1_stage, w2_stage, w1g_bf, w1l_bf, w2_bf, x_buf, sem, sem_x):
    i = pl.program_id(0)
    live = i < nused_ref[0]
    tm = x_buf.shape[1]

    def x_copy(blk, s):
        row0 = pl.multiple_of(xbase_ref[blk], ROW_ALIGN)
        return pltpu.make_async_copy(x_hbm.at[pl.ds(row0, tm), :], x_buf.at[s], sem_x.at[s])

    @pl.when(live & (i == 0))
    def _():
        x_copy(0, 0).start()

    def weight_copies(e, s):
        return (pltpu.make_async_copy(w1_hbm.at[e], w1_stage.at[s], sem.at[0, s]),
                pltpu.make_async_copy(w2_hbm.at[e], w2_stage.at[s], sem.at[1, s]))

    @pl.when(live & (i == 0))
    def _():
        for cp in weight_copies(bexp_ref[0], 0):
            cp.start()

    @pl.when(live & (first_ref[i] == 1))
    def _():
        s = slot_ref[i]
        for cp in weight_copies(bexp_ref[i], s):
            cp.wait()

        @pl.when(nexte_ref[i] >= 0)
        def _():
            for cp in weight_copies(nexte_ref[i], 1 - s):
                cp.start()

        half = DEINT_COLS // 2
        for c in range(w1_stage.shape[2] // DEINT_COLS):
            blk = w1_stage[s, :, c * DEINT_COLS:(c + 1) * DEINT_COLS].astype(BF16)
            out = jnp.dot(blk, perm_ref[...], preferred_element_type=F32).astype(BF16)
            w1g_bf[:, c * half:(c + 1) * half] = out[:, :half]
            w1l_bf[:, c * half:(c + 1) * half] = out[:, half:]
        w2_bf[...] = w2_stage[s].astype(BF16)

    @pl.when(live)
    def _():
        xs = i % 2
        x_copy(i, xs).wait()

        @pl.when(i + 1 < nused_ref[0])
        def _():
            x_copy(i + 1, 1 - xs).start()

        def ffn(x):
            x = x.astype(BF16)
            fs = w1g_bf.shape[1] // FFN_SPLIT
            groups = [slice(c * fs, (c + 1) * fs) for c in range(FFN_SPLIT)]
            hid = [(jnp.dot(x, w1g_bf[:, g], preferred_element_type=F32) + b1g_ref[0][:, g],
                    jnp.dot(x, w1l_bf[:, g], preferred_element_type=F32) + b1l_ref[0][:, g]) for g in groups]
            o = b2_ref[0]
            for g, (h_glu, h_lin) in zip(groups, hid):
                x_glu = jnp.minimum(h_glu, SWIGLU_LIMIT)
                x_lin = jnp.clip(h_lin, -SWIGLU_LIMIT, SWIGLU_LIMIT)
                act = x_glu * _sigmoid(SWIGLU_ALPHA * x_glu) * (x_lin + 1.0)
                o = o + jnp.dot(act.astype(BF16), w2_bf[g, :], preferred_element_type=F32)
            return _pack_bf16_pairs(o)

        half = tm // 2
        short = xrows_ref[i] <= half

        @pl.when(short)
        def _():
            o_ref[:half, :] = ffn(x_buf[xs, :half, :])
            o_ref[half:, :] = jnp.zeros((tm - half, o_ref.shape[1]), o_ref.dtype)

        @pl.when(jnp.logical_not(short))
        def _():
            o_ref[...] = ffn(x_buf[xs])

    @pl.when(jnp.logical_not(live))
    def _():
        o_ref[...] = jnp.zeros_like(o_ref)


def _moe_ffn(block_exp, n_used, x_base, x_rows, n_blocks, xs, w1, b1g, b1l, w2, b2):
    E, D, F2 = w1.shape
    F = F2 // 2
    tm = MOE_TILE
    n_rows = n_blocks * tm

    idx = jnp.arange(n_blocks, dtype=jnp.int32)
    first = jnp.concatenate([jnp.ones((1,), jnp.bool_), block_exp[1:] != block_exp[:-1]])
    slot = ((jnp.cumsum(first.astype(jnp.int32)) - 1) % 2).astype(jnp.int32)
    cand = jnp.where(first & (idx < n_used[0]), idx, n_blocks)
    next_first = jnp.min(jnp.where(idx[None, :] > idx[:, None], cand[None, :], n_blocks), axis=1)
    next_e = jnp.where(next_first < n_blocks, block_exp[jnp.minimum(next_first, n_blocks - 1)], -1).astype(jnp.int32)

    half = DEINT_COLS // 2
    src = jnp.arange(DEINT_COLS)
    dst = jnp.where(src % 2 == 0, src // 2, half + src // 2)
    perm = (dst[:, None] == jnp.arange(DEINT_COLS)[None, :]).astype(BF16)

    live = lambda i, be, nu: jnp.minimum(i, nu[0] - 1)
    bspec = lambda n: pl.BlockSpec((1, 1, n), lambda i, be, nu, fi, ne, sl, xb, xr: (be[live(i, be, nu)], 0, 0))
    grid_spec = pltpu.PrefetchScalarGridSpec(
        num_scalar_prefetch=7,
        grid=(n_blocks,),
        in_specs=[
            pl.BlockSpec(memory_space=pl.ANY),
            pl.BlockSpec(memory_space=pl.ANY),
            pl.BlockSpec(memory_space=pl.ANY),
            bspec(F), bspec(F), bspec(D),
            pl.BlockSpec((DEINT_COLS, DEINT_COLS), lambda i, be, nu, fi, ne, sl, xb, xr: (0, 0)),
        ],
        out_specs=pl.BlockSpec((tm, D // 2), lambda i, be, nu, fi, ne, sl, xb, xr: (i, 0)),
        scratch_shapes=[
            pltpu.VMEM((2, D, F2), F32), pltpu.VMEM((2, F, D), F32),
            pltpu.VMEM((D, F), BF16), pltpu.VMEM((D, F), BF16), pltpu.VMEM((F, D), BF16),
            pltpu.VMEM((2, tm, D), F32),
            pltpu.SemaphoreType.DMA((2, 2)), pltpu.SemaphoreType.DMA((2,)),
        ],
    )
    return pl.pallas_call(
        _moe_kernel,
        grid_spec=grid_spec,
        out_shape=jax.ShapeDtypeStruct((n_rows, D // 2), jnp.uint32),
        compiler_params=pltpu.CompilerParams(dimension_semantics=("arbitrary",), vmem_limit_bytes=MOE_VMEM_LIMIT),
        name="moe_ffn",
    )(block_exp, n_used, first.astype(jnp.int32), next_e, slot, x_base, x_rows, xs, w1, w2, b1g, b1l, b2, perm)


def _combine_ln_kernel(h_ref, eo_ref, gate_ref, lnw_ref, lnb_ref, o_ref, *, alpha):
    gates = gate_ref[...]
    ffn = _unpack_bf16_pairs(eo_ref[0]) * gates[:, 0:1]
    for j in range(1, TOP_K):
        ffn = ffn + _unpack_bf16_pairs(eo_ref[j]) * gates[:, j:j + 1]
    o_ref[...] = _layer_norm(alpha * h_ref[...] + ffn, lnw_ref[...], lnb_ref[...])


def _combine_ln(h1, eo, gates, lnw, lnb, alpha):
    T, D = h1.shape
    tm = LN_ROW_TILE
    return pl.pallas_call(
        functools.partial(_combine_ln_kernel, alpha=alpha),
        grid=(T // tm,),
        in_specs=[
            pl.BlockSpec((tm, D), lambda i: (i, 0)),
            pl.BlockSpec((TOP_K, tm, D // 2), lambda i: (0, i, 0)),
            pl.BlockSpec((tm, TOP_K), lambda i: (i, 0)),
            pl.BlockSpec((1, D), lambda i: (0, 0)),
            pl.BlockSpec((1, D), lambda i: (0, 0)),
        ],
        out_specs=pl.BlockSpec((tm, D), lambda i: (i, 0)),
        out_shape=jax.ShapeDtypeStruct((T, D), F32),
        compiler_params=_cparams(("parallel",)),
        name="combine_ln",
    )(h1, eo, gates, lnw, lnb)


def _pad_to(x, n, axis):
    pad = [(0, 0)] * x.ndim
    pad[axis] = (0, n - x.shape[axis])
    return jnp.pad(x, pad)


def _block_diag_ones(width, value=1.0):
    idx = jnp.arange(width) // HEAD_DIM
    return jnp.where(idx[:, None] == idx[None, :], value, 0.0).astype(BF16)


def _route(logits, n_experts, tile):
    T = logits.shape[0]
    top_val, top_idx = lax.top_k(logits[:, :n_experts], TOP_K)
    gates = jax.nn.softmax(top_val, axis=-1)
    e_flat = top_idx.reshape(-1).astype(jnp.int32)
    n_assign = T * TOP_K
    assert n_assign % tile == 0 and tile % ROW_ALIGN == 0 and n_experts * n_assign < 2 ** 31
    n_blocks = n_assign // tile + n_experts + 1
    last_base = n_assign - tile
    eids = jnp.arange(n_experts, dtype=jnp.int32)
    aids = jnp.arange(n_assign, dtype=jnp.int32)
    skeys = lax.sort(e_flat * n_assign + aids)
    order = skeys % n_assign
    e_sorted = skeys // n_assign
    counts = jnp.sum((e_flat[:, None] == eids[None, :]).astype(jnp.int32), axis=0)
    starts = jnp.cumsum(counts) - counts
    base = starts // ROW_ALIGN * ROW_ALIGN
    padded = (starts - base + counts + tile - 1) // tile * tile
    pends = jnp.cumsum(padded)
    pstarts = pends - padded
    of_sorted = lambda per_expert: jnp.sum(
        jnp.where(e_sorted[:, None] == eids[None, :], per_expert[None, :], 0), axis=1)
    base_q = of_sorted(base)
    k_q = (aids - base_q) // tile
    window_q = jnp.minimum(base_q + k_q * tile, last_base)
    dest_sorted = of_sorted(pstarts) + k_q * tile + (aids - window_q)
    _, dest = lax.sort((order, dest_sorted), num_keys=1)
    block_start = jnp.arange(n_blocks, dtype=jnp.int32) * tile
    block_exp = jnp.minimum(jnp.sum((pends[None, :] <= block_start[:, None]).astype(jnp.int32), axis=1),
                            n_experts - 1).astype(jnp.int32)
    n_used = (pends[-1] // tile).astype(jnp.int32).reshape(1)
    of_block = lambda per_expert: jnp.sum(
        jnp.where(block_exp[:, None] == eids[None, :], per_expert[None, :], 0), axis=1)
    x_base = jnp.clip(of_block(base - pstarts) + block_start, 0, last_base).astype(jnp.int32)
    x_rows = jnp.clip(of_block(starts + counts) - x_base, 0, tile).astype(jnp.int32)
    return gates, dest.reshape(T, TOP_K), order // TOP_K, block_exp, n_used, x_base, x_rows, n_blocks


def kernel(x, ln_in_w, ln_in_b, w_in, rw_mu, rw_w0, rw_w2, rw_a0, rw_a2, rw_g2, rw_k_k, rw_k_a, rw_r_k,
           rw_gn_w, rw_gn_b, fx_b_f, fx_q_norm, fx_k_norm, w_o, ln1_w, ln1_b, router_w, router_b,
           exp_w1, exp_b1, exp_w2, exp_b2, ln2_w, ln2_b):
    B, S, D = x.shape
    T = B * S
    depth = w_in.shape[0]
    alpha = (2 * depth) ** 0.25
    rw_w = rw_w0.shape[1]
    fx_heads = fx_b_f.shape[1]
    fx_w = fx_heads * HEAD_DIM
    d_lora, a_lora, g_lora = rw_w2.shape[1], rw_a2.shape[1], rw_g2.shape[1]
    n_lora = d_lora + a_lora + g_lora
    lora_pad = -(-n_lora // 128) * 128
    rw_cols = 3 * rw_w + n_lora
    n_rw = 3 * rw_w + lora_pad
    n_experts = router_w.shape[2]
    ne_pad = -(-n_experts // 128) * 128
    row = lambda a: a.reshape(1, -1)

    seg_rw = _block_diag_ones(rw_w)
    segm_rw = _block_diag_ones(rw_w, 1.0 / HEAD_DIM)
    segm_fx = _block_diag_ones(fx_w, 1.0 / HEAD_DIM)
    tidx = jnp.arange(ROW_TILE)
    tril = (tidx[:, None] >= tidx[None, :]).astype(BF16)
    cidx = jnp.arange(WKV_CHUNK)
    tri_c = (cidx[:, None] >= cidx[None, :]).astype(BF16)

    assert depth == 1, "single-layer block"
    l = 0
    x2 = x.reshape(T, D)
    w_l = w_in.reshape(w_in.shape[1:])
    wfz = w_l[:, rw_cols + 4 * fx_w:]
    wfz_hi = wfz.astype(BF16)
    wfz_lo = (wfz - wfz_hi.astype(F32)).astype(BF16)
    w_main = jnp.concatenate(
        [_pad_to(w_l[:, :rw_cols], n_rw, 1).astype(BF16), w_l[:, rw_cols:rw_cols + 4 * fx_w].astype(BF16),
         _pad_to(jnp.concatenate([wfz_hi, wfz_lo], axis=1), PAIR, 1)], axis=1)
    mu = _pad_to(row(rw_mu[l]), n_rw, 1)
    w2p = _pad_to(rw_w2[l], lora_pad, 0).astype(BF16)
    a2p = _pad_to(jnp.pad(rw_a2[l], ((d_lora, 0), (0, 0))), lora_pad, 0).astype(BF16)
    g2p = _pad_to(jnp.pad(rw_g2[l], ((d_lora + a_lora, 0), (0, 0))), lora_pad, 0).astype(BF16)
    rw_params = [mu, row(rw_w0[l]), w2p, row(rw_a0[l]), a2p, g2p,
                 row(rw_k_k[l]), row(rw_k_a[l]), row(rw_r_k[l]), seg_rw]
    qw = row(jnp.tile(fx_q_norm[l], fx_heads))
    kw = row(jnp.tile(fx_k_norm[l], fx_heads))
    fx_params = [_pad_to(row(fx_b_f[l]), PAIR, 1), qw, kw, segm_fx, tril]
    h0, r, lw, k2, v, kk, alr, g, bonus, qa, ka, va, og = _front(
        x2, B, S, row(ln_in_w), row(ln_in_b), w_main, _pad_to(wfz_hi, PAIR, 1), rw_params, fx_params,
        rw_w, fx_w, fx_heads)

    y_rw = _wkv(r, lw, k2, v, kk, alr, g, bonus, row(rw_gn_w[l]), row(rw_gn_b[l]), tri_c, segm_rw, B, S)
    y_fx = _fox_attn(qa, ka, va, og, B, S, fx_w)

    wo = w_o[l].astype(BF16)
    rw_pad = _pad_to(router_w[l], ne_pad, 1)
    rb_pad = _pad_to(row(router_b[l]), ne_pad, 1)
    h1, logits = _out_ln(y_rw, y_fx, h0, wo[:rw_w], wo[rw_w:], row(ln1_w[l]), row(ln1_b[l]),
                         rw_pad, rb_pad, alpha)

    gates, pos, gather_tok, block_exp, n_used, x_base, x_rows, n_blocks = _route(logits, n_experts, MOE_TILE)
    xs = h1[gather_tok]
    b1 = exp_b1[l]
    b1g = b1[:, None, 0::2]
    b1l = b1[:, None, 1::2]
    eo_rows = _moe_ffn(block_exp, n_used, x_base, x_rows, n_blocks, xs, exp_w1[l], b1g, b1l, exp_w2[l],
                       exp_b2[l][:, None, :])
    eo = eo_rows[pos.T]
    h = _combine_ln(h1, eo, gates, row(ln2_w[l]), row(ln2_b[l]), alpha)
    return h.reshape(B, S, D)
```

```python
import functools

import jax
import jax.numpy as jnp
from jax import lax
from jax.experimental import pallas as pl
from jax.experimental.pallas import tpu as pltpu

F32 = jnp.float32
BF16 = jnp.bfloat16

HEAD_DIM = 64
PAIR = 2 * HEAD_DIM
WKV_CHUNK = 64
RW_GN_EPS = 64e-5
QK_EPS = 1e-6
LN_EPS = 1e-5
TOP_K = 4
SWIGLU_ALPHA = 1.702
SWIGLU_LIMIT = 7.0
NEG_BIG = -1e30
LOG2E = 1.4426950408889634
N_PIECES = 3

ROW_TILE = 256
LN_ROW_TILE = 1024
ATTN_TQ = 512
ATTN_TK = 512
ATTN_HEADS = 8
MOE_TILE = 512
ROW_ALIGN = 8
WKV_BATCH = 4
VMEM_LIMIT = 48 * 1024 * 1024
FRONT_VMEM_LIMIT = 56 * 1024 * 1024


def _cparams(sem):
    return pltpu.CompilerParams(dimension_semantics=sem, vmem_limit_bytes=VMEM_LIMIT)


def _dot(a, b):
    return jnp.dot(a.astype(BF16), b.astype(BF16), preferred_element_type=F32)


def _dot_t(a, b):
    return lax.dot_general(a.astype(BF16), b.astype(BF16), (((1,), (1,)), ((), ())),
                           preferred_element_type=F32)


def _segsum(x, seg):
    return jnp.dot(x.astype(BF16), seg, preferred_element_type=F32)


def _split3(x):
    hi = x.astype(BF16).astype(F32)
    mid = (x - hi).astype(BF16).astype(F32)
    return hi, mid, x - hi - mid


def _sigmoid(x):
    return 1.0 / (1.0 + jnp.exp(-x))


def _softplus(x):
    return jnp.maximum(x, 0.0) + jnp.log(1.0 + jnp.exp(-jnp.abs(x)))


def _layer_norm(x, w, b):
    mu = jnp.mean(x, axis=-1, keepdims=True)
    xc = x - mu
    var = jnp.mean(xc * xc, axis=-1, keepdims=True)
    return xc * lax.rsqrt(var + LN_EPS) * w + b


def _pack_bf16_pairs(x):
    n = x.shape[1] // 2
    bits = pltpu.bitcast(x.astype(BF16).astype(F32), jnp.uint32)
    return lax.shift_right_logical(bits[:, :n], jnp.uint32(16)) | bits[:, n:]


def _unpack_bf16_pairs(u):
    lo = pltpu.bitcast(lax.shift_left(u, jnp.uint32(16)), F32)
    hi = pltpu.bitcast(u & jnp.uint32(0xFFFF0000), F32)
    return jnp.concatenate([lo, hi], axis=1)


def _stack_heads(x):
    lane = lax.broadcasted_iota(jnp.int32, x.shape, 1)
    return jnp.concatenate([jnp.where(lane < HEAD_DIM, x, 0.0), jnp.where(lane >= HEAD_DIM, x, 0.0)], axis=0)


def _wkv_kernel(r_ref, lw_ref, k_ref, v_ref, kk_ref, alr_ref, g_ref, bonus_ref, gnw_ref, gnb_ref, tri_ref,
                segm_ref, y_ref, state_ref):
    C = WKV_CHUNK
    nb, _, width = lw_ref.shape
    npair = width // PAIR

    @pl.when(pl.program_id(1) == 0)
    def _():
        state_ref[...] = jnp.zeros_like(state_ref)

    ri = lax.broadcasted_iota(jnp.int32, (2 * C, 2 * C), 0)
    ci = lax.broadcasted_iota(jnp.int32, (2 * C, 2 * C), 1)
    same = (ri // C) == (ci // C)
    strict = same & ((ci % C) < (ri % C))
    incl = same & ((ci % C) <= (ri % C))
    eye = (ri == ci).astype(F32)

    lhs, rhs, a2s, r2s, v2s, bhts, bkts, ptots = [], [], [], [], [], [], [], []
    for bi in range(nb):
        lw = lw_ref[bi]
        cum = sum(jnp.dot(tri_ref[...], piece.astype(BF16), preferred_element_type=F32) for piece in _split3(lw))
        total = cum[C - 1:C, :]
        p_inv = jnp.exp(-cum)
        p_rem = jnp.exp(total - cum)
        p_tot = jnp.exp(total)
        kk = kk_ref[bi]
        k2 = k_ref[bi]
        b = kk * alr_ref[bi]
        a_t = -kk * jnp.exp(cum - lw)
        r_t = r_ref[bi] * jnp.exp(cum)
        b_t = b * p_inv
        k_t = k2 * p_inv
        b_h = b * p_rem
        k_h = k2 * p_rem
        v = v_ref[bi]
        for p in range(npair):
            sl = slice(p * PAIR, (p + 1) * PAIR)
            a2, r2, b2, kt2 = (_stack_heads(t[:, sl]) for t in (a_t, r_t, b_t, k_t))
            bh2, kh2, v2 = (_stack_heads(t[:, sl]) for t in (b_h, k_h, v))
            lhs.append(jnp.concatenate([a2, r2], axis=0))
            rhs.append(jnp.concatenate([b2, kt2], axis=0))
            a2s.append(a2)
            r2s.append(r2)
            v2s.append(v2)
            bhts.append(bh2.T)
            bkts.append(jnp.concatenate([bh2.T, kh2.T], axis=1))
            ptots.append(p_tot[:, sl])

    chains = range(nb * npair)
    m = [_dot_t(lhs[c], rhs[c]) for c in chains]
    n_ab = [jnp.where(strict, m[c][:2 * C, :2 * C], 0.0) for c in chains]
    m_ak = [jnp.where(strict, m[c][:2 * C, 2 * C:], 0.0) for c in chains]
    m_rb = [jnp.where(incl, m[c][2 * C:, :2 * C], 0.0) for c in chains]
    m_rk = [jnp.where(incl, m[c][2 * C:, 2 * C:], 0.0) for c in chains]
    mv = [_dot(m_ak[c], v2s[c]) for c in chains]
    mrkv = [_dot(m_rk[c], v2s[c]) for c in chains]
    inv = [eye + n_ab[c] for c in chains]
    pw = n_ab
    for _ in range(C.bit_length() - 2):
        pw = [_dot(pw[c], pw[c]) for c in chains]
        inv = [inv[c] + _dot(inv[c], pw[c]) for c in chains]
    wu = [_dot(inv[c], jnp.concatenate([a2s[c], mv[c]], axis=1)) for c in chains]
    qy = [_dot(m_rb[c], wu[c]) + jnp.concatenate([r2s[c], mrkv[c]], axis=1) for c in chains]
    g_t = [_dot(bhts[c], wu[c][:, :PAIR]) + eye * ptots[c] for c in chains]
    h_t = [_dot(bkts[c], jnp.concatenate([wu[c][:, PAIR:], v2s[c]], axis=0)) for c in chains]
    s0 = [state_ref[c] for c in chains]
    y2 = [_dot(qy[c][:, :PAIR], s0[c]) + qy[c][:, PAIR:] for c in chains]
    for c in chains:
        state_ref[c] = _dot(g_t[c], s0[c]) + h_t[c]

    segm = segm_ref[...]
    for bi in range(nb):
        y = jnp.concatenate([y2[bi * npair + p][:C] + y2[bi * npair + p][C:] for p in range(npair)], axis=1)
        mean = _segsum(y, segm)
        yc = y - mean
        var = _segsum(yc * yc, segm)
        yn = yc * lax.rsqrt(var + RW_GN_EPS) * gnw_ref[...] + gnb_ref[...]
        y_ref[bi] = ((yn + bonus_ref[bi]) * g_ref[bi]).astype(y_ref.dtype)


def _wkv(r, lw, k2, v, kk, alr, g, bonus, gnw, gnb, tri, segm, B, S):
    T, width = r.shape
    C = WKV_CHUNK
    nb = WKV_BATCH
    nc = S // C
    fixed = lambda b, c: (0, 0)
    blk = pl.BlockSpec((nb, C, width), lambda b, c: (b, c, 0))
    vec = pl.BlockSpec((1, width), fixed)
    ins = [t.reshape(B, S, width) for t in (r, lw, k2, v, kk, alr, g, bonus)]
    y = pl.pallas_call(
        _wkv_kernel,
        grid=(B // nb, nc),
        in_specs=[blk] * 8 + [vec, vec, pl.BlockSpec((C, C), fixed), pl.BlockSpec((width, width), fixed)],
        out_specs=blk,
        out_shape=jax.ShapeDtypeStruct((B, S, width), BF16),
        scratch_shapes=[pltpu.VMEM((nb * width // PAIR, PAIR, PAIR), F32)],
        compiler_params=_cparams(("parallel", "arbitrary")),
        name="wkv",
    )(*ins, gnw, gnb, tri, segm)
    return y.reshape(T, width)


def _pieces(x, n):
    hi, mid, lo = _split3(x)
    return hi + pltpu.roll(mid, n, axis=1) + pltpu.roll(lo, 2 * n, axis=1)


def _spread_heads(x):
    lane = lax.broadcasted_iota(jnp.int32, (x.shape[0], PAIR), 1)
    groups = []
    for p in range(x.shape[1] // PAIR):
        blk = x[:, p * PAIR:(p + 1) * PAIR]
        groups.append(jnp.where(lane < HEAD_DIM, blk, 0.0))
        groups.append(jnp.where(lane < HEAD_DIM, pltpu.roll(blk, HEAD_DIM, axis=1), 0.0))
    return jnp.concatenate(groups, axis=1)


def _front_kernel(x_ref, lnw_ref, lnb_ref, w_ref, wfz_ref,
                  mu_ref, w0_ref, w2_ref, a0_ref, a2_ref, g2_ref, kkw_ref, kaw_ref, rk_ref, seg_ref,
                  bf_ref, qw_ref, kw_ref, segm_ref, tril_ref, place_ref, maskq_ref, maskk_ref,
                  oneq_ref, onek_ref, onev_ref,
                  h_ref, r_ref, lw_ref, k_ref, v_ref, kk_ref, alr_ref, g_ref, bonus_ref,
                  qa_ref, ka_ref, va_ref, og_ref, shift_carry, c_carry, *, width, nh):
    @pl.when(pl.program_id(1) == 0)
    def _():
        shift_carry[...] = jnp.zeros_like(shift_carry)
        c_carry[...] = jnp.zeros_like(c_carry)

    h = _layer_norm(x_ref[...], lnw_ref[...], lnb_ref[...])
    h_ref[...] = h
    proj = jnp.dot(h.astype(BF16), w_ref[...], preferred_element_type=F32)
    tm = proj.shape[0]
    n_rw = mu_ref.shape[1]

    p = proj[:, :n_rw]
    prev = pltpu.roll(p, 1, axis=0)
    first_row = lax.broadcasted_iota(jnp.int32, p.shape, 0) == 0
    prev = jnp.where(first_row, shift_carry[...], prev)
    shift_carry[...] = p[tm - 1:tm, :]
    ps = p + mu_ref[...] * (prev - p)
    r = ps[:, 0:width]
    k = ps[:, width:2 * width]
    v = ps[:, 2 * width:3 * width]
    lora = ps[:, 3 * width:]
    seg = seg_ref[...]
    wl = w0_ref[...] + _dot(jnp.tanh(lora), w2_ref[...])
    w_raw = -_softplus(-wl) - 0.5
    lw_ref[...] = -jnp.exp(w_raw)
    alr = _sigmoid(a0_ref[...] + _dot(lora, a2_ref[...]))
    g_ref[...] = _dot(_sigmoid(lora), g2_ref[...])
    kkp = k * kkw_ref[...]
    nrm = jnp.sqrt(_segsum(kkp * kkp, seg))
    kk_ref[...] = kkp / jnp.maximum(nrm, 1e-12)
    k2 = k * (1.0 + (alr - 1.0) * kaw_ref[...])
    bonus_ref[...] = _segsum(r * k2 * rk_ref[...], seg) * v
    r_ref[...] = r
    k_ref[...] = k2
    v_ref[...] = v
    alr_ref[...] = alr

    fw = qw_ref.shape[1]
    q = proj[:, n_rw:n_rw + fw]
    kx = proj[:, n_rw + fw:n_rw + 2 * fw]
    vx = proj[:, n_rw + 2 * fw:n_rw + 3 * fw]
    og_ref[...] = proj[:, n_rw + 3 * fw:n_rw + 4 * fw]
    segm = segm_ref[...]
    qn = q * lax.rsqrt(_segsum(q * q, segm) + QK_EPS) * (qw_ref[...] * (HEAD_DIM ** -0.5 * LOG2E))
    kn = kx * lax.rsqrt(_segsum(kx * kx, segm) + QK_EPS) * kw_ref[...]
    fblk = proj[:, n_rw + 4 * fw:]
    h_lo = (h - h.astype(BF16).astype(F32)).astype(BF16)
    fz = fblk + pltpu.roll(fblk, PAIR - nh, axis=1) + jnp.dot(h_lo, wfz_ref[...], preferred_element_type=F32)
    in_heads = lax.broadcasted_iota(jnp.int32, fz.shape, 1) < nh
    lf = jnp.where(in_heads, -_softplus(-(fz + bf_ref[...])), 0.0)
    c3 = jnp.dot(tril_ref[...], _pieces(lf, nh).astype(BF16), preferred_element_type=F32)
    c = c3 + pltpu.roll(c3, PAIR - nh, axis=1) + pltpu.roll(c3, PAIR - 2 * nh, axis=1)
    c = jnp.where(in_heads, c, 0.0) + c_carry[...]
    c_carry[...] = c[tm - 1:tm, :]
    placed = jnp.dot(_pieces(c * LOG2E, nh).astype(BF16), place_ref[...], preferred_element_type=F32)
    qa_ref[...] = (_spread_heads(qn) + placed * maskq_ref[...] + oneq_ref[...]).astype(qa_ref.dtype)
    ka_ref[...] = (_spread_heads(kn) - placed * maskk_ref[...] + onek_ref[...]).astype(ka_ref.dtype)
    va_ref[...] = (_spread_heads(vx) + onev_ref[...]).astype(va_ref.dtype)


def _front(x2, B, S, lnw, lnb, w_main, wfz_hi, rw_params, fx_params, rw_w, fx_w, nh):
    T, D = x2.shape
    wide = nh * PAIR
    tm = ROW_TILE
    ns = S // tm
    row = lambda b, s: (b * ns + s, 0)
    fixed = lambda b, s: (0, 0)
    const = lambda a: pl.BlockSpec(a.shape, fixed)

    lane = jnp.arange(wide) % PAIR
    head = jnp.arange(wide) // PAIR
    src = jnp.arange(PAIR)
    k_end = HEAD_DIM + N_PIECES
    q_end = k_end + N_PIECES
    piece = (lane - HEAD_DIM) % N_PIECES
    place = ((src[:, None] == (piece * nh + head)[None, :]) & (src[:, None] < N_PIECES * nh)
             & (lane >= HEAD_DIM)[None, :] & (lane < q_end)[None, :]).astype(BF16)
    mask_k = ((lane >= HEAD_DIM) & (lane < k_end)).astype(F32).reshape(1, wide)
    mask_q = ((lane >= k_end) & (lane < q_end)).astype(F32).reshape(1, wide)
    one_v = (lane >= HEAD_DIM).astype(F32).reshape(1, wide)
    consts = [lnw, lnb, w_main, wfz_hi, *rw_params, *fx_params, place, mask_q, mask_k, mask_k, mask_q, one_v]

    f32_out = lambda n: (pl.BlockSpec((tm, n), row), jax.ShapeDtypeStruct((T, n), F32))
    bf_out = lambda n: (pl.BlockSpec((tm, n), row), jax.ShapeDtypeStruct((T, n), BF16))
    outs = [f32_out(D)] + [f32_out(rw_w)] * 8 + [bf_out(wide)] * 3 + [f32_out(fx_w)]
    return pl.pallas_call(
        functools.partial(_front_kernel, width=rw_w, nh=nh),
        grid=(B, ns),
        in_specs=[pl.BlockSpec((tm, D), row)] + [const(a) for a in consts],
        out_specs=[o[0] for o in outs],
        out_shape=[o[1] for o in outs],
        scratch_shapes=[pltpu.VMEM((1, rw_params[0].shape[1]), F32), pltpu.VMEM((1, PAIR), F32)],
        compiler_params=pltpu.CompilerParams(dimension_semantics=("parallel", "arbitrary"),
                                             vmem_limit_bytes=FRONT_VMEM_LIMIT),
        name="front",
    )(x2, *consts)


def _fox_attn_kernel(qi_ref, kj_ref, q_ref, k_ref, v_ref, og_ref, o_ref, m_ref, acc_ref):
    i = qi_ref[pl.program_id(2)]
    j = kj_ref[pl.program_id(2)]
    tq = q_ref.shape[0]
    tk = k_ref.shape[0]
    heads = range(ATTN_HEADS)
    grp = lambda ref, h: ref[:, h * PAIR:(h + 1) * PAIR]

    @pl.when(j == 0)
    def _():
        m_ref[...] = jnp.full_like(m_ref, NEG_BIG)
        acc_ref[...] = jnp.zeros_like(acc_ref)

    def step(masked):
        s = [lax.dot_general(grp(q_ref, h), grp(k_ref, h), (((1,), (1,)), ((), ())), preferred_element_type=F32)
             for h in heads]
        if masked:
            row = lax.broadcasted_iota(jnp.int32, (tq, tk), 0)
            col = lax.broadcasted_iota(jnp.int32, (tq, tk), 1)
            s = [jnp.where(col <= row, s[h], NEG_BIG) for h in heads]
        m_old = [m_ref[h] for h in heads]
        m_new = [jnp.maximum(m_old[h], jnp.max(s[h], axis=1, keepdims=True)) for h in heads]
        alpha = [jnp.exp2(m_old[h] - m_new[h]) for h in heads]
        pr = [jnp.exp2(s[h] - jnp.concatenate([m_new[h]] * (tk // PAIR), axis=1)).astype(BF16) for h in heads]
        pv = [jnp.dot(pr[h], grp(v_ref, h), preferred_element_type=F32) for h in heads]
        for h in heads:
            acc_ref[h] = alpha[h] * acc_ref[h] + pv[h]
            m_ref[h] = m_new[h]

    @pl.when(j < i)
    def _():
        step(False)

    @pl.when(j == i)
    def _():
        step(True)
        lane = lax.broadcasted_iota(jnp.int32, (tq, PAIR), 1)
        outs = []
        for p in range(ATTN_HEADS // 2):
            a0 = acc_ref[2 * p]
            a1 = acc_ref[2 * p + 1]
            o0 = a0 * pltpu.roll(1.0 / a0, HEAD_DIM, axis=1)
            o1 = pltpu.roll(a1, HEAD_DIM, axis=1) * (1.0 / a1)
            outs.append(jnp.where(lane < HEAD_DIM, o0, o1))
        o = jnp.concatenate(outs, axis=1)
        o_ref[...] = (o * _sigmoid(og_ref[...])).astype(o_ref.dtype)


def _fox_attn(qa, ka, va, og, B, S, width):
    T = qa.shape[0]
    tq, tk = ATTN_TQ, ATTN_TK
    assert tq == tk
    nq = S // tq
    nh = ATTN_HEADS
    ow = nh * HEAD_DIM
    ngroup = width // ow
    pairs = [(i, j) for i in range(nq) for j in range(i + 1)]
    qi = jnp.array([ij[0] for ij in pairs], jnp.int32)
    kj = jnp.array([ij[1] for ij in pairs], jnp.int32)
    kv = pl.BlockSpec((tk, nh * PAIR), lambda b, p, t, qi, kj: (b * nq + kj[t], p))
    grid_spec = pltpu.PrefetchScalarGridSpec(
        num_scalar_prefetch=2,
        grid=(B, ngroup, len(pairs)),
        in_specs=[
            pl.BlockSpec((tq, nh * PAIR), lambda b, p, t, qi, kj: (b * nq + qi[t], p)), kv, kv,
            pl.BlockSpec((tq, ow), lambda b, p, t, qi, kj: (b * nq + qi[t], p)),
        ],
        out_specs=pl.BlockSpec((tq, ow), lambda b, p, t, qi, kj: (b * nq + qi[t], p)),
        scratch_shapes=[pltpu.VMEM((nh, tq, PAIR), F32), pltpu.VMEM((nh, tq, PAIR), F32)],
    )
    return pl.pallas_call(
        _fox_attn_kernel,
        grid_spec=grid_spec,
        out_shape=jax.ShapeDtypeStruct((T, width), BF16),
        compiler_params=_cparams(("parallel", "parallel", "arbitrary")),
        name="fox_attn",
    )(qi, kj, qa, ka, va, og)


def _out_ln_kernel(yrw_ref, yfx_ref, h_ref, wo1_ref, wo2_ref, lnw_ref, lnb_ref, rw_ref, rb_ref,
                   h1_ref, logit_ref, *, alpha):
    mix = (jnp.dot(yrw_ref[...], wo1_ref[...], preferred_element_type=F32)
           + jnp.dot(yfx_ref[...], wo2_ref[...], preferred_element_type=F32))
    h1 = _layer_norm(alpha * h_ref[...] + mix, lnw_ref[...], lnb_ref[...])
    h1_ref[...] = h1
    ne = rb_ref.shape[1]
    h_hi = h1.astype(BF16)
    h_lo = (h1 - h_hi.astype(F32)).astype(BF16)
    rw = rw_ref[...]
    first = jnp.dot(h_hi, rw, preferred_element_type=F32)
    second = jnp.dot(h_lo, rw[:, :ne], preferred_element_type=F32)
    logit_ref[...] = first[:, :ne] + first[:, ne:] + second + rb_ref[...]


def _out_ln(y_rw, y_fx, h0, wo1, wo2, lnw, lnb, rw, rb, alpha):
    T, D = h0.shape
    width = y_rw.shape[1]
    ne = rb.shape[1]
    tm = LN_ROW_TILE
    row = lambda i: (i, 0)
    fixed = lambda i: (0, 0)
    rw_hi = rw.astype(BF16)
    rw_lo = (rw - rw_hi.astype(F32)).astype(BF16)
    rw = jnp.concatenate([rw_hi, rw_lo], axis=1)
    return pl.pallas_call(
        functools.partial(_out_ln_kernel, alpha=alpha),
        grid=(T // tm,),
        in_specs=[
            pl.BlockSpec((tm, width), row), pl.BlockSpec((tm, width), row), pl.BlockSpec((tm, D), row),
            pl.BlockSpec((width, D), fixed), pl.BlockSpec((width, D), fixed),
            pl.BlockSpec((1, D), fixed), pl.BlockSpec((1, D), fixed),
            pl.BlockSpec((D, 2 * ne), fixed), pl.BlockSpec((1, ne), fixed),
        ],
        out_specs=[pl.BlockSpec((tm, D), row), pl.BlockSpec((tm, ne), row)],
        out_shape=[jax.ShapeDtypeStruct((T, D), F32), jax.ShapeDtypeStruct((T, ne), F32)],
        compiler_params=_cparams(("parallel",)),
        name="out_ln",
    )(y_rw, y_fx, h0, wo1, wo2, lnw, lnb, rw, rb)


DEINT_COLS = 256
WEIGHT_DMA_PRIORITY = 1
MOE_VMEM_LIMIT = 56 * 1024 * 1024


def _moe_kernel(bexp_ref, nused_ref, first_ref, nexte_ref, slot_ref, xbase_ref, xrows_ref,
                x_hbm, w1_hbm, w2_hbm, b1g_ref, b1l_ref, b2_ref, perm_ref, o_ref,
                w1_stage, w2_stage, w1g_bf, w1l_bf, w2_bf, x_buf, sem, sem_x):
    i = pl.program_id(0)
    live = i < nused_ref[0]
    tm = x_buf.shape[1]

    def x_copy(blk, s):
        row0 = pl.multiple_of(xbase_ref[blk], ROW_ALIGN)
        return pltpu.make_async_copy(x_hbm.at[pl.ds(row0, tm), :], x_buf.at[s], sem_x.at[s])

    @pl.when(live & (i == 0))
    def _():
        x_copy(0, 0).start()

    def weight_copies(e, s):
        return (pltpu.make_async_copy(w1_hbm.at[e], w1_stage.at[s], sem.at[0, s]),
                pltpu.make_async_copy(w2_hbm.at[e], w2_stage.at[s], sem.at[1, s]))

    @pl.when(live & (i == 0))
    def _():
        for cp in weight_copies(bexp_ref[0], 0):
            cp.start(priority=WEIGHT_DMA_PRIORITY)

    @pl.when(live & (first_ref[i] == 1))
    def _():
        s = slot_ref[i]
        for cp in weight_copies(bexp_ref[i], s):
            cp.wait()

        @pl.when(nexte_ref[i] >= 0)
        def _():
            for cp in weight_copies(nexte_ref[i], 1 - s):
                cp.start(priority=WEIGHT_DMA_PRIORITY)

        half = DEINT_COLS // 2
        for c in range(w1_stage.shape[2] // DEINT_COLS):
            blk = w1_stage[s, :, c * DEINT_COLS:(c + 1) * DEINT_COLS].astype(BF16)
            out = jnp.dot(blk, perm_ref[...], preferred_element_type=F32).astype(BF16)
            w1g_bf[:, c * half:(c + 1) * half] = out[:, :half]
            w1l_bf[:, c * half:(c + 1) * half] = out[:, half:]
        w2_bf[...] = w2_stage[s].astype(BF16)

    @pl.when(live)
    def _():
        xs = i % 2
        x_copy(i, xs).wait()

        @pl.when(i + 1 < nused_ref[0])
        def _():
            x_copy(i + 1, 1 - xs).start()

        def ffn(x):
            x = x.astype(BF16)
            x_glu = jnp.minimum(jnp.dot(x, w1g_bf[...], preferred_element_type=F32) + b1g_ref[0], SWIGLU_LIMIT)
            x_lin = jnp.clip(jnp.dot(x, w1l_bf[...], preferred_element_type=F32) + b1l_ref[0],
                             -SWIGLU_LIMIT, SWIGLU_LIMIT)
            act = x_glu * _sigmoid(SWIGLU_ALPHA * x_glu) * (x_lin + 1.0)
            return _pack_bf16_pairs(jnp.dot(act.astype(BF16), w2_bf[...], preferred_element_type=F32) + b2_ref[0])

        half = tm // 2
        short = xrows_ref[i] <= half

        @pl.when(short)
        def _():
            o_ref[:half, :] = ffn(x_buf[xs, :half, :])
            o_ref[half:, :] = jnp.zeros((tm - half, o_ref.shape[1]), o_ref.dtype)

        @pl.when(jnp.logical_not(short))
        def _():
            o_ref[...] = ffn(x_buf[xs])

    @pl.when(jnp.logical_not(live))
    def _():
        o_ref[...] = jnp.zeros_like(o_ref)


def _moe_ffn(block_exp, n_used, x_base, x_rows, n_blocks, xs, w1, b1g, b1l, w2, b2):
    E, D, F2 = w1.shape
    F = F2 // 2
    tm = MOE_TILE
    n_rows = n_blocks * tm

    idx = jnp.arange(n_blocks, dtype=jnp.int32)
    first = jnp.concatenate([jnp.ones((1,), jnp.bool_), block_exp[1:] != block_exp[:-1]])
    slot = ((jnp.cumsum(first.astype(jnp.int32)) - 1) % 2).astype(jnp.int32)
    cand = jnp.where(first & (idx < n_used[0]), idx, n_blocks)
    next_first = jnp.min(jnp.where(idx[None, :] > idx[:, None], cand[None, :], n_blocks), axis=1)
    next_e = jnp.where(next_first < n_blocks, block_exp[jnp.minimum(next_first, n_blocks - 1)], -1).astype(jnp.int32)

    half = DEINT_COLS // 2
    src = jnp.arange(DEINT_COLS)
    dst = jnp.where(src % 2 == 0, src // 2, half + src // 2)
    perm = (dst[:, None] == jnp.arange(DEINT_COLS)[None, :]).astype(BF16)

    live = lambda i, be, nu: jnp.minimum(i, nu[0] - 1)
    bspec = lambda n: pl.BlockSpec((1, 1, n), lambda i, be, nu, fi, ne, sl, xb, xr: (be[live(i, be, nu)], 0, 0))
    grid_spec = pltpu.PrefetchScalarGridSpec(
        num_scalar_prefetch=7,
        grid=(n_blocks,),
        in_specs=[
            pl.BlockSpec(memory_space=pl.ANY),
            pl.BlockSpec(memory_space=pl.ANY),
            pl.BlockSpec(memory_space=pl.ANY),
            bspec(F), bspec(F), bspec(D),
            pl.BlockSpec((DEINT_COLS, DEINT_COLS), lambda i, be, nu, fi, ne, sl, xb, xr: (0, 0)),
        ],
        out_specs=pl.BlockSpec((tm, D // 2), lambda i, be, nu, fi, ne, sl, xb, xr: (i, 0)),
        scratch_shapes=[
            pltpu.VMEM((2, D, F2), F32), pltpu.VMEM((2, F, D), F32),
            pltpu.VMEM((D, F), BF16), pltpu.VMEM((D, F), BF16), pltpu.VMEM((F, D), BF16),
            pltpu.VMEM((2, tm, D), F32),
            pltpu.SemaphoreType.DMA((2, 2)), pltpu.SemaphoreType.DMA((2,)),
        ],
    )
    return pl.pallas_call(
        _moe_kernel,
        grid_spec=grid_spec,
        out_shape=jax.ShapeDtypeStruct((n_rows, D // 2), jnp.uint32),
        compiler_params=pltpu.CompilerParams(dimension_semantics=("arbitrary",), vmem_limit_bytes=MOE_VMEM_LIMIT),
        name="moe_ffn",
    )(block_exp, n_used, first.astype(jnp.int32), next_e, slot, x_base, x_rows, xs, w1, w2, b1g, b1l, b2, perm)


def _combine_ln_kernel(h_ref, eo_ref, gate_ref, lnw_ref, lnb_ref, o_ref, *, alpha):
    gates = gate_ref[...]
    ffn = _unpack_bf16_pairs(eo_ref[0]) * gates[:, 0:1]
    for j in range(1, TOP_K):
        ffn = ffn + _unpack_bf16_pairs(eo_ref[j]) * gates[:, j:j + 1]
    o_ref[...] = _layer_norm(alpha * h_ref[...] + ffn, lnw_ref[...], lnb_ref[...])


def _combine_ln(h1, eo, gates, lnw, lnb, alpha):
    T, D = h1.shape
    tm = LN_ROW_TILE
    return pl.pallas_call(
        functools.partial(_combine_ln_kernel, alpha=alpha),
        grid=(T // tm,),
        in_specs=[
            pl.BlockSpec((tm, D), lambda i: (i, 0)),
            pl.BlockSpec((TOP_K, tm, D // 2), lambda i: (0, i, 0)),
            pl.BlockSpec((tm, TOP_K), lambda i: (i, 0)),
            pl.BlockSpec((1, D), lambda i: (0, 0)),
            pl.BlockSpec((1, D), lambda i: (0, 0)),
        ],
        out_specs=pl.BlockSpec((tm, D), lambda i: (i, 0)),
        out_shape=jax.ShapeDtypeStruct((T, D), F32),
        compiler_params=_cparams(("parallel",)),
        name="combine_ln",
    )(h1, eo, gates, lnw, lnb)


def _pad_to(x, n, axis):
    pad = [(0, 0)] * x.ndim
    pad[axis] = (0, n - x.shape[axis])
    return jnp.pad(x, pad)


def _block_diag_ones(width, value=1.0):
    idx = jnp.arange(width) // HEAD_DIM
    return jnp.where(idx[:, None] == idx[None, :], value, 0.0).astype(BF16)


def _route(logits, n_experts, tile):
    T = logits.shape[0]
    top_val, top_idx = lax.top_k(logits[:, :n_experts], TOP_K)
    gates = jax.nn.softmax(top_val, axis=-1)
    e_flat = top_idx.reshape(-1).astype(jnp.int32)
    n_assign = T * TOP_K
    assert n_assign % tile == 0 and tile % ROW_ALIGN == 0 and n_experts * n_assign < 2 ** 31
    n_blocks = n_assign // tile + n_experts + 1
    last_base = n_assign - tile
    eids = jnp.arange(n_experts, dtype=jnp.int32)
    aids = jnp.arange(n_assign, dtype=jnp.int32)
    skeys = lax.sort(e_flat * n_assign + aids)
    order = skeys % n_assign
    e_sorted = skeys // n_assign
    counts = jnp.sum((e_flat[:, None] == eids[None, :]).astype(jnp.int32), axis=0)
    starts = jnp.cumsum(counts) - counts
    base = starts // ROW_ALIGN * ROW_ALIGN
    padded = (starts - base + counts + tile - 1) // tile * tile
    pends = jnp.cumsum(padded)
    pstarts = pends - padded
    of_sorted = lambda per_expert: jnp.sum(
        jnp.where(e_sorted[:, None] == eids[None, :], per_expert[None, :], 0), axis=1)
    base_q = of_sorted(base)
    k_q = (aids - base_q) // tile
    window_q = jnp.minimum(base_q + k_q * tile, last_base)
    dest_sorted = of_sorted(pstarts) + k_q * tile + (aids - window_q)
    _, dest = lax.sort((order, dest_sorted), num_keys=1)
    block_start = jnp.arange(n_blocks, dtype=jnp.int32) * tile
    block_exp = jnp.minimum(jnp.sum((pends[None, :] <= block_start[:, None]).astype(jnp.int32), axis=1),
                            n_experts - 1).astype(jnp.int32)
    n_used = (pends[-1] // tile).astype(jnp.int32).reshape(1)
    of_block = lambda per_expert: jnp.sum(
        jnp.where(block_exp[:, None] == eids[None, :], per_expert[None, :], 0), axis=1)
    x_base = jnp.clip(of_block(base - pstarts) + block_start, 0, last_base).astype(jnp.int32)
    x_rows = jnp.clip(of_block(starts + counts) - x_base, 0, tile).astype(jnp.int32)
    return gates, dest.reshape(T, TOP_K), order // TOP_K, block_exp, n_used, x_base, x_rows, n_blocks


def kernel(x, ln_in_w, ln_in_b, w_in, rw_mu, rw_w0, rw_w2, rw_a0, rw_a2, rw_g2, rw_k_k, rw_k_a, rw_r_k,
           rw_gn_w, rw_gn_b, fx_b_f, fx_q_norm, fx_k_norm, w_o, ln1_w, ln1_b, router_w, router_b,
           exp_w1, exp_b1, exp_w2, exp_b2, ln2_w, ln2_b):
    B, S, D = x.shape
    T = B * S
    depth = w_in.shape[0]
    alpha = (2 * depth) ** 0.25
    rw_w = rw_w0.shape[1]
    fx_heads = fx_b_f.shape[1]
    fx_w = fx_heads * HEAD_DIM
    d_lora, a_lora, g_lora = rw_w2.shape[1], rw_a2.shape[1], rw_g2.shape[1]
    n_lora = d_lora + a_lora + g_lora
    lora_pad = -(-n_lora // 128) * 128
    rw_cols = 3 * rw_w + n_lora
    n_rw = 3 * rw_w + lora_pad
    n_experts = router_w.shape[2]
    ne_pad = -(-n_experts // 128) * 128
    row = lambda a: a.reshape(1, -1)

    seg_rw = _block_diag_ones(rw_w)
    segm_rw = _block_diag_ones(rw_w, 1.0 / HEAD_DIM)
    segm_fx = _block_diag_ones(fx_w, 1.0 / HEAD_DIM)
    tidx = jnp.arange(ROW_TILE)
    tril = (tidx[:, None] >= tidx[None, :]).astype(BF16)
    cidx = jnp.arange(WKV_CHUNK)
    tri_c = (cidx[:, None] >= cidx[None, :]).astype(BF16)

    assert depth == 1, "single-layer block"
    l = 0
    x2 = x.reshape(T, D)
    w_l = w_in.reshape(w_in.shape[1:])
    wfz = w_l[:, rw_cols + 4 * fx_w:]
    wfz_hi = wfz.astype(BF16)
    wfz_lo = (wfz - wfz_hi.astype(F32)).astype(BF16)
    w_main = jnp.concatenate(
        [_pad_to(w_l[:, :rw_cols], n_rw, 1).astype(BF16), w_l[:, rw_cols:rw_cols + 4 * fx_w].astype(BF16),
         _pad_to(jnp.concatenate([wfz_hi, wfz_lo], axis=1), PAIR, 1)], axis=1)
    mu = _pad_to(row(rw_mu[l]), n_rw, 1)
    w2p = _pad_to(rw_w2[l], lora_pad, 0).astype(BF16)
    a2p = _pad_to(jnp.pad(rw_a2[l], ((d_lora, 0), (0, 0))), lora_pad, 0).astype(BF16)
    g2p = _pad_to(jnp.pad(rw_g2[l], ((d_lora + a_lora, 0), (0, 0))), lora_pad, 0).astype(BF16)
    rw_params = [mu, row(rw_w0[l]), w2p, row(rw_a0[l]), a2p, g2p,
                 row(rw_k_k[l]), row(rw_k_a[l]), row(rw_r_k[l]), seg_rw]
    qw = row(jnp.tile(fx_q_norm[l], fx_heads))
    kw = row(jnp.tile(fx_k_norm[l], fx_heads))
    fx_params = [_pad_to(row(fx_b_f[l]), PAIR, 1), qw, kw, segm_fx, tril]
    h0, r, lw, k2, v, kk, alr, g, bonus, qa, ka, va, og = _front(
        x2, B, S, row(ln_in_w), row(ln_in_b), w_main, _pad_to(wfz_hi, PAIR, 1), rw_params, fx_params,
        rw_w, fx_w, fx_heads)

    y_rw = _wkv(r, lw, k2, v, kk, alr, g, bonus, row(rw_gn_w[l]), row(rw_gn_b[l]), tri_c, segm_rw, B, S)
    y_fx = _fox_attn(qa, ka, va, og, B, S, fx_w)

    wo = w_o[l].astype(BF16)
    rw_pad = _pad_to(router_w[l], ne_pad, 1)
    rb_pad = _pad_to(row(router_b[l]), ne_pad, 1)
    h1, logits = _out_ln(y_rw, y_fx, h0, wo[:rw_w], wo[rw_w:], row(ln1_w[l]), row(ln1_b[l]),
                         rw_pad, rb_pad, alpha)

    gates, pos, gather_tok, block_exp, n_used, x_base, x_rows, n_blocks = _route(logits, n_experts, MOE_TILE)
    xs = h1[gather_tok]
    b1 = exp_b1[l]
    b1g = b1[:, None, 0::2]
    b1l = b1[:, None, 1::2]
    eo_rows = _moe_ffn(block_exp, n_used, x_base, x_rows, n_blocks, xs, exp_w1[l], b1g, b1l, exp_w2[l],
                       exp_b2[l][:, None, :])
    eo = eo_rows[pos.T]
    h = _combine_ln(h1, eo, gates, row(ln2_w[l]), row(ln2_b[l]), alpha)
    return h.reshape(B, S, D)
```
